```python
import jax
import jax.numpy as jnp
from jax import lax
import numpy as np

D_MODEL = 1024
BATCH = 2
SEQ = 8192
DEPTH = 1

N_MEM = 256
EPS = 1e-6

GLA_HEADS = 4
GLA_DK = D_MODEL // 16
GLA_DV = D_MODEL // 8
GLA_RANK = 16
GLA_GATE_NORMALIZER = 16.0
GLA_CHUNK = 64

MOBA_HEADS = 4
MOBA_DH = D_MODEL // 16
MOBA_BLOCK = 256
MOBA_TOPK = 3
MOBA_QBLOCK = 128

MEM_HEADS = 4
MEM_DH = D_MODEL // 16

GLA_QK_W = GLA_HEADS * GLA_DK
GLA_V_W = GLA_HEADS * GLA_DV
MOBA_W = MOBA_HEADS * MOBA_DH
MEM_W = MEM_HEADS * MEM_DH
D_MIX = GLA_V_W + MOBA_W + MEM_W
IN_SPLITS = (GLA_QK_W, GLA_QK_W, GLA_V_W, GLA_V_W, GLA_RANK, MOBA_W, MOBA_W, MOBA_W, MEM_W)
D_IN = sum(IN_SPLITS)

N_GROUPS = 4
EXPERTS_PER_GROUP = 8
N_EXPERTS = N_GROUPS * EXPERTS_PER_GROUP
MOE_TOPK = 2
MOE_FF = D_MODEL // 2
MOE_BLOCK = 128

kernel_name = 'hymba_gla_moba_mem_hmoe_layer'


def rmsnorm(x, g):
    xf = x.astype(jnp.float32)
    y = xf * lax.rsqrt(jnp.mean(xf * xf, axis=-1, keepdims=True) + EPS)
    return (y * g.astype(jnp.float32)).astype(x.dtype)


def split_heads(t, n):
    b, s, w = t.shape
    return t.reshape(b, s, n, w // n).transpose(0, 2, 1, 3)


def merge_heads(t):
    b, n, s, d = t.shape
    return t.transpose(0, 2, 1, 3).reshape(b, s, n * d)


def alibi_slopes(n):
    return jnp.asarray(2.0 ** (-8.0 * np.arange(1, n + 1) / n), dtype=jnp.float32)


def gla_chunked(q, k, v, g):
    f32 = jnp.float32
    b, h, s, dk = q.shape
    dv = v.shape[-1]
    c = GLA_CHUNK
    n = s // c
    qf = q.astype(f32).reshape(b, h, n, c, dk) * (dk ** -0.5)
    kf = k.astype(f32).reshape(b, h, n, c, dk)
    vf = v.astype(f32).reshape(b, h, n, c, dv)
    cum = jnp.cumsum(g.astype(f32).reshape(b, h, n, c, dk), axis=3)
    cum_last = cum[:, :, :, -1:, :]
    q_dec = qf * jnp.exp(cum)
    k_inv = kf * jnp.exp(-cum)
    causal = jnp.tril(jnp.ones((c, c), dtype=bool))
    a_intra = jnp.where(causal, jnp.einsum('bhnik,bhnjk->bhnij', q_dec, k_inv), 0.0)
    o_intra = jnp.einsum('bhnij,bhnjv->bhniv', a_intra, vf)
    k_end = kf * jnp.exp(cum_last - cum)
    d_state = jnp.einsum('bhnck,bhncv->nbhkv', k_end, vf)
    chunk_decay = jnp.moveaxis(jnp.exp(cum_last[:, :, :, 0, :]), 2, 0)

    def step(state, inp):
        dec, ds = inp
        return state * dec[..., None] + ds, state

    _, s_prev = lax.scan(step, jnp.zeros((b, h, dk, dv), f32), (chunk_decay, d_state))
    o_inter = jnp.einsum('bhnck,nbhkv->bhncv', q_dec, s_prev)
    return (o_intra + o_inter).reshape(b, h, s, dv)


def moba_attention(q, k, v, slopes):
    f32 = jnp.float32
    b, h, s, d = q.shape
    nb = -(-s // MOBA_BLOCK)
    pad = nb * MOBA_BLOCK - s
    k_p = jnp.pad(k, ((0, 0), (0, 0), (0, pad), (0, 0)))
    v_p = jnp.pad(v, ((0, 0), (0, 0), (0, pad), (0, 0)))
    kb = k_p.reshape(b, h, nb, MOBA_BLOCK, d)
    vb = v_p.reshape(b, h, nb, MOBA_BLOCK, d)
    k_mean = jnp.mean(kb.astype(f32), axis=3)
    nb_gate = max(nb, MOBA_TOPK)
    scale = d ** -0.5
    bi = jnp.arange(b)[:, None, None, None]
    hi = jnp.arange(h)[None, :, None, None]
    blk_pos = jnp.arange(MOBA_BLOCK)
    m = slopes[None, :, None, None]
    n_sel = MOBA_TOPK * MOBA_BLOCK

    def query_block(qb):
        start = qb * MOBA_QBLOCK
        qc = lax.dynamic_slice_in_dim(q, start, MOBA_QBLOCK, axis=2)
        t = start + jnp.arange(MOBA_QBLOCK)
        own = start // MOBA_BLOCK
        gate = jnp.einsum('bhqd,bhnd->bhqn', qc.astype(f32), k_mean)
        gate = jnp.where(jnp.arange(nb) < own, gate, -jnp.inf)
        gate = jnp.pad(gate, ((0, 0), (0, 0), (0, 0), (0, nb_gate - nb)), constant_values=-jnp.inf)
        _, idx = lax.top_k(gate, MOBA_TOPK)
        idx = jnp.minimum(idx, nb - 1)
        valid = jnp.arange(MOBA_TOPK) < own
        k_sel = kb[bi, hi, idx]
        v_sel = vb[bi, hi, idx]
        s_sel = jnp.einsum('bhqd,bhqkjd->bhqkj', qc, k_sel, preferred_element_type=f32) * scale
        dist_sel = (t[:, None, None] - (idx[..., None] * MOBA_BLOCK + blk_pos)).astype(f32)
        s_sel = jnp.where(valid[:, None], s_sel - m[..., None] * dist_sel, -jnp.inf)
        k_own = lax.dynamic_slice_in_dim(k_p, own * MOBA_BLOCK, MOBA_BLOCK, axis=2)
        v_own = lax.dynamic_slice_in_dim(v_p, own * MOBA_BLOCK, MOBA_BLOCK, axis=2)
        dist_own = (t[:, None] - (own * MOBA_BLOCK + blk_pos)[None, :]).astype(f32)
        s_own = jnp.einsum('bhqd,bhjd->bhqj', qc, k_own, preferred_element_type=f32) * scale
        s_own = jnp.where(dist_own >= 0, s_own - m * dist_own, -jnp.inf)
        scores = jnp.concatenate([s_sel.reshape(b, h, MOBA_QBLOCK, n_sel), s_own], axis=-1)
        p = jax.nn.softmax(scores, axis=-1)
        p_sel = p[..., :n_sel].reshape(b, h, MOBA_QBLOCK, MOBA_TOPK, MOBA_BLOCK).astype(v.dtype)
        p_own = p[..., n_sel:].astype(v.dtype)
        o = (jnp.einsum('bhqkj,bhqkjd->bhqd', p_sel, v_sel, preferred_element_type=f32)
             + jnp.einsum('bhqj,bhjd->bhqd', p_own, v_own, preferred_element_type=f32))
        return o.astype(q.dtype)

    out = lax.map(query_block, jnp.arange(s // MOBA_QBLOCK))
    return out.transpose(1, 2, 0, 3, 4).reshape(b, h, s, d)


def memory_attention(q, mk, mv):
    f32 = jnp.float32
    scale = q.shape[-1] ** -0.5
    s = jnp.einsum('bhsd,bhmd->bhsm', q, mk, preferred_element_type=f32) * scale
    p = jax.nn.softmax(s, axis=-1).astype(mv.dtype)
    return jnp.einsum('bhsm,bhmd->bhsd', p, mv, preferred_element_type=f32).astype(q.dtype)


def hierarchical_moe(h, w_rg, b_rg, w_re, b_re, w_gate, w_up, w_down):
    f32 = jnp.float32
    b, s, d = h.shape
    t_tok = b * s
    xt = h.reshape(t_tok, d)
    lg = jnp.einsum('td,dg->tg', xt, w_rg).astype(f32) + b_rg.astype(f32)
    g_sel = jnp.argmax(lg, axis=-1)
    p_group = jnp.take_along_axis(jax.nn.softmax(lg, axis=-1), g_sel[:, None], axis=-1)
    le = jnp.einsum('td,gde->tge', xt, w_re).astype(f32) + b_re.astype(f32)
    le = jnp.take_along_axis(le, g_sel[:, None, None], axis=1)[:, 0]
    le_top, e_loc = lax.top_k(le, MOE_TOPK)
    gates = p_group * jax.nn.softmax(le_top, axis=-1)
    expert = (g_sel[:, None] * EXPERTS_PER_GROUP + e_loc).astype(jnp.int32)

    n_assign = t_tok * MOE_TOPK
    e_flat = expert.reshape(n_assign)
    t_flat = jnp.repeat(jnp.arange(t_tok, dtype=jnp.int32), MOE_TOPK)
    w_flat = gates.reshape(n_assign)
    order = jnp.argsort(e_flat)
    e_s, t_s, w_s = e_flat[order], t_flat[order], w_flat[order]
    counts = jax.ops.segment_sum(jnp.ones((n_assign,), jnp.int32), e_flat, num_segments=N_EXPERTS)
    padded = (counts + MOE_BLOCK - 1) // MOE_BLOCK * MOE_BLOCK
    starts = jnp.cumsum(counts) - counts
    pstarts = jnp.cumsum(padded) - padded
    dest = pstarts[e_s] + (jnp.arange(n_assign, dtype=jnp.int32) - starts[e_s])
    n_blocks = (n_assign + N_EXPERTS * (MOE_BLOCK - 1) + MOE_BLOCK - 1) // MOE_BLOCK
    n_rows = n_blocks * MOE_BLOCK
    tok_buf = jnp.full((n_rows,), t_tok, jnp.int32).at[dest].set(t_s)
    w_buf = jnp.zeros((n_rows,), f32).at[dest].set(w_s)
    block_start = jnp.arange(n_blocks, dtype=jnp.int32) * MOE_BLOCK
    block_expert = jnp.minimum(jnp.searchsorted(pstarts + padded, block_start, side='right'), N_EXPERTS - 1)
    x_pad = jnp.concatenate([xt, jnp.zeros((1, d), xt.dtype)], axis=0)

    def expert_block(args):
        toks, e = args
        xb = x_pad[toks]
        hid = jax.nn.silu(xb @ w_gate[e]) * (xb @ w_up[e])
        return hid @ w_down[e]

    y = lax.map(expert_block, (tok_buf.reshape(n_blocks, MOE_BLOCK), block_expert))
    y = y.reshape(n_rows, d) * w_buf[:, None].astype(y.dtype)
    out = jnp.zeros((t_tok + 1, d), y.dtype).at[tok_buf].add(y)[:t_tok]
    return out.reshape(b, s, d)


def setup_inputs(seed: int = 0) -> dict:
    key = jax.random.key(seed)
    ks = jax.random.split(key, 24)
    L = DEPTH

    def nrm(k, shape, fan_in):
        return jax.random.normal(k, shape, jnp.float32) * fan_in ** -0.5

    def gain(k, shape):
        return 1.0 + 0.02 * jax.random.normal(k, shape, jnp.float32)

    return {
        'x': jax.random.normal(ks[0], (BATCH, SEQ, D_MODEL), jnp.float32),
        'mem': jax.random.normal(ks[1], (BATCH, N_MEM, D_MODEL), jnp.float32),
        'attn_norm_g': gain(ks[2], (L, D_MODEL)),
        'mem_norm_g': gain(ks[3], (L, D_MODEL)),
        'w_in': nrm(ks[4], (L, D_MODEL, D_IN), D_MODEL),
        'w_gla_gk': nrm(ks[5], (L, GLA_RANK, GLA_QK_W), GLA_RANK),
        'b_gla_gk': 0.1 * jax.random.normal(ks[6], (L, GLA_QK_W), jnp.float32),
        'gla_out_norm_g': gain(ks[7], (L, GLA_DV)),
        'moba_q_norm_g': gain(ks[8], (L, MOBA_DH)),
        'moba_k_norm_g': gain(ks[9], (L, MOBA_DH)),
        'w_mem_kv': nrm(ks[10], (L, D_MODEL, 2 * MEM_W), D_MODEL),
        'mem_q_norm_g': gain(ks[11], (L, MEM_DH)),
        'mem_k_norm_g': gain(ks[12], (L, MEM_DH)),
        'w_out': nrm(ks[13], (L, D_MIX, D_MODEL), D_MIX),
        'ffn_norm_g': gain(ks[14], (L, D_MODEL)),
        'w_router_group': nrm(ks[15], (L, D_MODEL, N_GROUPS), D_MODEL),
        'b_router_group': 0.01 * jax.random.normal(ks[16], (L, N_GROUPS), jnp.float32),
        'w_router_expert': nrm(ks[17], (L, N_GROUPS, D_MODEL, EXPERTS_PER_GROUP), D_MODEL),
        'b_router_expert': 0.01 * jax.random.normal(ks[18], (L, N_GROUPS, EXPERTS_PER_GROUP), jnp.float32),
        'w_gate': nrm(ks[19], (L, N_EXPERTS, D_MODEL, MOE_FF), D_MODEL),
        'w_up': nrm(ks[20], (L, N_EXPERTS, D_MODEL, MOE_FF), D_MODEL),
        'w_down': nrm(ks[21], (L, N_EXPERTS, MOE_FF, D_MODEL), MOE_FF),
    }


def reference(x, mem, attn_norm_g, mem_norm_g, w_in, w_gla_gk, b_gla_gk, gla_out_norm_g,
              moba_q_norm_g, moba_k_norm_g, w_mem_kv, mem_q_norm_g, mem_k_norm_g, w_out,
              ffn_norm_g, w_router_group, b_router_group, w_router_expert, b_router_expert,
              w_gate, w_up, w_down):
    slopes = alibi_slopes(MOBA_HEADS)
    split_at = np.cumsum(IN_SPLITS)[:-1].tolist()
    for l in range(DEPTH):
        h = rmsnorm(x, attn_norm_g[l])
        proj = jnp.einsum('bsd,de->bse', h, w_in[l])
        gla_q, gla_k, gla_v, gla_r, gla_lr, mb_q, mb_k, mb_v, mem_q = jnp.split(proj, split_at, axis=-1)

        g_log = jax.nn.log_sigmoid((gla_lr @ w_gla_gk[l] + b_gla_gk[l]).astype(jnp.float32)) / GLA_GATE_NORMALIZER
        o_gla = gla_chunked(split_heads(gla_q, GLA_HEADS), split_heads(gla_k, GLA_HEADS),
                            split_heads(gla_v, GLA_HEADS), split_heads(g_log, GLA_HEADS))
        o_gla = rmsnorm(o_gla, gla_out_norm_g[l]).astype(x.dtype)
        o_gla = merge_heads(o_gla) * jax.nn.silu(gla_r)

        q_b = rmsnorm(split_heads(mb_q, MOBA_HEADS), moba_q_norm_g[l])
        k_b = rmsnorm(split_heads(mb_k, MOBA_HEADS), moba_k_norm_g[l])
        v_b = split_heads(mb_v, MOBA_HEADS)
        o_moba = merge_heads(moba_attention(q_b, k_b, v_b, slopes))

        mem_h = rmsnorm(mem, mem_norm_g[l])
        mem_k, mem_v = jnp.split(jnp.einsum('bmd,de->bme', mem_h, w_mem_kv[l]), 2, axis=-1)
        q_c = rmsnorm(split_heads(mem_q, MEM_HEADS), mem_q_norm_g[l])
        k_c = rmsnorm(split_heads(mem_k, MEM_HEADS), mem_k_norm_g[l])
        o_mem = merge_heads(memory_attention(q_c, k_c, split_heads(mem_v, MEM_HEADS)))

        mix = jnp.concatenate([o_gla, o_moba, o_mem], axis=-1)
        x = x + jnp.einsum('bse,ed->bsd', mix, w_out[l])

        x = x + hierarchical_moe(rmsnorm(x, ffn_norm_g[l]), w_router_group[l], b_router_group[l],
                                 w_router_expert[l], b_router_expert[l], w_gate[l], w_up[l], w_down[l])
    return x
```

```python
import functools

import jax
import jax.numpy as jnp
import numpy as np
from jax import lax
from jax.experimental import pallas as pl
from jax.experimental.pallas import tpu as pltpu

F32 = jnp.float32
BF16 = jnp.bfloat16
EPS = 1e-6
NEG_INF = float("-inf")

D_MODEL = 1024
N_MEM = 256
HEADS = 4
DH = 64
GLA_DV = 128
GLA_RANK = 16
GLA_GATE_NORMALIZER = 16.0
GLA_CHUNK = 64
MOBA_BLOCK = 256
MOBA_TOPK = 3
QK_W = HEADS * DH
GLA_V_W = HEADS * GLA_DV
N_GROUPS = 4
EXPERTS_PER_GROUP = 8
N_EXPERTS = N_GROUPS * EXPERTS_PER_GROUP
MOE_FF = 512
LANES = 128
ROW_BLOCK = 256

VMEM_LIMIT = 56 * 1024 * 1024


def _params(sem):
    return pltpu.CompilerParams(dimension_semantics=sem, vmem_limit_bytes=VMEM_LIMIT)


def _nt(a, b):
    return lax.dot_general(a, b, (((1,), (1,)), ((), ())), preferred_element_type=F32)


def _tn(a, b):
    return lax.dot_general(a, b, (((0,), (0,)), ((), ())), preferred_element_type=F32)


def _dot(a, b):
    return jnp.dot(a, b, preferred_element_type=F32)


def _rms_rows(x, g):
    ms = jnp.mean(x * x, axis=-1, keepdims=True)
    return x * lax.rsqrt(ms + EPS) * g


def _split3(x):
    h1 = x.astype(BF16)
    r1 = x - h1.astype(F32)
    h2 = r1.astype(BF16)
    h3 = (r1 - h2.astype(F32)).astype(BF16)
    return h1, h2, h3


def _head_mean_sq(x, seg):
    sq = x * x
    hi = sq.astype(BF16)
    lo = (sq - hi.astype(F32)).astype(BF16)
    return _dot(hi, seg) + _dot(lo, seg)


def _silu(x):
    return x * (1.0 / (1.0 + jnp.exp(-x)))


def _mem_kv_kernel(mem_ref, g_ref, w_ref, gk_ref, seg_ref, k_ref, vt_ref):
    h = _rms_rows(mem_ref[0], g_ref[...]).astype(BF16)
    kv = _dot(h, w_ref[...])
    k = kv[:, :QK_W]
    kn = k * lax.rsqrt(_head_mean_sq(k, seg_ref[...]) + EPS) * gk_ref[...]
    k_ref[0] = kn.astype(BF16)
    vt_ref[0] = kv[:, QK_W:].T.astype(BF16)


def _mem_kv(mem, g, w_bf, gk_t, seg):
    b = mem.shape[0]
    return pl.pallas_call(
        _mem_kv_kernel,
        grid=(b,),
        in_specs=[
            pl.BlockSpec((1, N_MEM, D_MODEL), lambda i: (i, 0, 0)),
            pl.BlockSpec((1, D_MODEL), lambda i: (0, 0)),
            pl.BlockSpec((D_MODEL, 2 * QK_W), lambda i: (0, 0)),
            pl.BlockSpec((1, QK_W), lambda i: (0, 0)),
            pl.BlockSpec((QK_W, QK_W), lambda i: (0, 0)),
        ],
        out_specs=[
            pl.BlockSpec((1, N_MEM, QK_W), lambda i: (i, 0, 0)),
            pl.BlockSpec((1, QK_W, N_MEM), lambda i: (i, 0, 0)),
        ],
        out_shape=[
            jax.ShapeDtypeStruct((b, N_MEM, QK_W), BF16),
            jax.ShapeDtypeStruct((b, QK_W, N_MEM), BF16),
        ],
        compiler_params=_params(("arbitrary",)),
        name="mem_kv",
    )(mem, g, w_bf, gk_t, seg)


IN_TM = 512
_C_QK, _C_V, _C_R, _C_MQ, _C_MK, _C_MV, _C_CQ, _C_LR = 0, 512, 1024, 1536, 1792, 2048, 2304, 2560
D_IN = 2576


def _in_proj_kernel(x_ref, g_ref, w_ref, seg_ref, gq_ref, gk_ref, gc_ref,
                    qk_ref, v_ref, r_ref, lr_ref, mq_ref, mk_ref, mvt_ref, kmean_ref, cq_ref):
    h = _rms_rows(x_ref[...], g_ref[...]).astype(BF16)
    p = _dot(h, w_ref[...])
    seg = seg_ref[...]
    qk_ref[...] = p[:, _C_QK:_C_V]
    v_ref[...] = p[:, _C_V:_C_R].astype(BF16)
    r_ref[...] = p[:, _C_R:_C_MQ]
    lr_ref[...] = p[:, _C_LR:D_IN]

    def head_norm(t, gain):
        return t * lax.rsqrt(_head_mean_sq(t, seg) + EPS) * gain

    scale = DH ** -0.5
    mq_ref[...] = (head_norm(p[:, _C_MQ:_C_MK], gq_ref[...]) * scale).astype(BF16)
    cq_ref[...] = (head_norm(p[:, _C_CQ:_C_LR], gc_ref[...]) * scale).astype(BF16)
    kn = head_norm(p[:, _C_MK:_C_MV], gk_ref[...])
    mk_ref[...] = kn.astype(BF16)
    mv = p[:, _C_MV:_C_CQ]
    for j in range(IN_TM // MOBA_BLOCK):
        rows = slice(j * MOBA_BLOCK, (j + 1) * MOBA_BLOCK)
        kmean_ref[0, j:j + 1, :] = jnp.mean(kn[rows], axis=0, keepdims=True)
        mvt_ref[j] = mv[rows].T.astype(BF16)


def _in_proj(x2, g, w_bf, seg, gq_t, gk_t, gc_t):
    t = x2.shape[0]
    nt = t // IN_TM
    nb = IN_TM // MOBA_BLOCK
    row = lambda w: pl.BlockSpec((IN_TM, w), lambda i: (i, 0))
    const = lambda a, b: pl.BlockSpec((a, b), lambda i: (0, 0))
    return pl.pallas_call(
        _in_proj_kernel,
        grid=(nt,),
        in_specs=[row(D_MODEL), const(1, D_MODEL), const(D_MODEL, D_IN), const(QK_W, QK_W),
                  const(1, QK_W), const(1, QK_W), const(1, QK_W)],
        out_specs=[row(2 * QK_W), row(GLA_V_W), row(GLA_V_W), row(GLA_RANK), row(QK_W), row(QK_W),
                   pl.BlockSpec((nb, QK_W, MOBA_BLOCK), lambda i: (i, 0, 0)),
                   pl.BlockSpec((1, nb, QK_W), lambda i: (i, 0, 0)),
                   row(QK_W)],
        out_shape=[
            jax.ShapeDtypeStruct((t, 2 * QK_W), F32),
            jax.ShapeDtypeStruct((t, GLA_V_W), BF16),
            jax.ShapeDtypeStruct((t, GLA_V_W), F32),
            jax.ShapeDtypeStruct((t, GLA_RANK), F32),
            jax.ShapeDtypeStruct((t, QK_W), BF16),
            jax.ShapeDtypeStruct((t, QK_W), BF16),
            jax.ShapeDtypeStruct((t // MOBA_BLOCK, QK_W, MOBA_BLOCK), BF16),
            jax.ShapeDtypeStruct((nt, nb, QK_W), F32),
            jax.ShapeDtypeStruct((t, QK_W), BF16),
        ],
        compiler_params=_params(("arbitrary",)),
        name="in_proj",
    )(x2, g, w_bf, seg, gq_t, gk_t, gc_t)


GLA_TC = 256


def _gla_kernel(qk_ref, v_ref, r_ref, lr_ref, wgk_ref, bgk_ref, gn_ref, tri_ref, ones_ref,
                o_ref, st_ref):
    @pl.when(pl.program_id(1) == 0)
    def _():
        st_ref[...] = jnp.zeros_like(st_ref)

    qk = qk_ref[0]
    q = qk[:, :QK_W]
    k = qk[:, QK_W:]
    gk = _dot(lr_ref[0].astype(BF16), wgk_ref[...]) + bgk_ref[...]
    g = -(jnp.maximum(-gk, 0.0) + jnp.log1p(jnp.exp(-jnp.abs(gk)))) / GLA_GATE_NORMALIZER
    g1, g2, g3 = _split3(g)
    tri = tri_ref[...]
    ones = ones_ref[...]
    cum = _dot(tri, g1) + _dot(tri, g2) + _dot(tri, g3)
    tot = _dot(ones, g1) + _dot(ones, g2) + _dot(ones, g3)
    q_dec = (q * (DH ** -0.5) * jnp.exp(cum)).astype(BF16)
    k_inv = (k * jnp.exp(-cum)).astype(BF16)
    k_end = (k * jnp.exp(tot - cum)).astype(BF16)
    decay = jnp.exp(tot)

    lane_head = lax.broadcasted_iota(jnp.int32, (GLA_CHUNK, QK_W), 1) // DH
    causal = (lax.broadcasted_iota(jnp.int32, (GLA_CHUNK, GLA_CHUNK), 0)
              >= lax.broadcasted_iota(jnp.int32, (GLA_CHUNK, GLA_CHUNK), 1))
    same_head = (lax.broadcasted_iota(jnp.int32, (GLA_V_W, QK_W), 0) // GLA_DV
                 == lax.broadcasted_iota(jnp.int32, (GLA_V_W, QK_W), 1) // DH)
    gain = gn_ref[...]

    for c in range(GLA_TC // GLA_CHUNK):
        rows = slice(c * GLA_CHUNK, (c + 1) * GLA_CHUNK)
        qd, ki, ke, vc = q_dec[rows], k_inv[rows], k_end[rows], v_ref[0, rows, :]
        st = st_ref[...]
        o_inter = _nt(qd, st.astype(BF16))
        outs = []
        for h in range(HEADS):
            a = _nt(jnp.where(lane_head == h, qd, jnp.zeros_like(qd)), ki)
            a = jnp.where(causal, a, 0.0).astype(BF16)
            vs = slice(h * GLA_DV, (h + 1) * GLA_DV)
            oh = _dot(a, vc[:, vs]) + o_inter[:, vs]
            y = _rms_rows(oh, gain)
            outs.append(y * _silu(r_ref[0, rows, vs]))
        o_ref[0, rows, :] = jnp.concatenate(outs, axis=-1).astype(BF16)
        d_st = _tn(vc, ke)
        st_ref[...] = st * decay[c * GLA_CHUNK:c * GLA_CHUNK + 1, :] + jnp.where(same_head, d_st, 0.0)


def _gla(qk, v, r, lr, wgk_bf, bgk, gn, b, s):
    ns = s // GLA_TC
    idx = np.arange(GLA_TC)
    same_chunk = (idx[:, None] // GLA_CHUNK) == (idx[None, :] // GLA_CHUNK)
    tri = jnp.asarray(same_chunk & (idx[:, None] >= idx[None, :]), BF16)
    ones = jnp.asarray(same_chunk, BF16)
    seq = lambda w: pl.BlockSpec((1, GLA_TC, w), lambda i, j: (i, j, 0))
    const = lambda a, c: pl.BlockSpec((a, c), lambda i, j: (0, 0))
    return pl.pallas_call(
        _gla_kernel,
        grid=(b, ns),
        in_specs=[seq(2 * QK_W), seq(GLA_V_W), seq(GLA_V_W), seq(GLA_RANK),
                  const(GLA_RANK, QK_W), const(1, QK_W), const(1, GLA_DV),
                  const(GLA_TC, GLA_TC), const(GLA_TC, GLA_TC)],
        out_specs=seq(GLA_V_W),
        out_shape=jax.ShapeDtypeStruct((b, s, GLA_V_W), BF16),
        scratch_shapes=[pltpu.VMEM((GLA_V_W, QK_W), F32)],
        compiler_params=_params(("arbitrary", "arbitrary")),
        name="gla",
    )(qk.reshape(b, s, 2 * QK_W), v.reshape(b, s, GLA_V_W), r.reshape(b, s, GLA_V_W),
      lr.reshape(b, s, GLA_RANK), wgk_bf, bgk, gn, tri, ones)


TQ = MOBA_BLOCK


def _moba_mem_kernel(q_ref, k_ref, vt_ref, kmean_ref, cq_ref, mk_ref, mvt_ref,
                     o_moba_ref, o_mem_ref,
                     qm_ref, negd_ref, selb_ref, m_ref, l_ref, acc_ref, *, n_blocks):
    i = pl.program_id(1)
    lane_head = lax.broadcasted_iota(jnp.int32, (TQ, QK_W), 1) // DH
    dist0 = (lax.broadcasted_iota(jnp.int32, (MOBA_BLOCK, TQ), 1)
             - lax.broadcasted_iota(jnp.int32, (MOBA_BLOCK, TQ), 0)).astype(F32)
    slopes = [2.0 ** (-8.0 * (h + 1) / HEADS) for h in range(HEADS)]

    q = q_ref[0]
    kmean = kmean_ref[0].astype(BF16)
    blk = lax.broadcasted_iota(jnp.int32, (n_blocks, TQ), 0)
    blk_f = blk.astype(F32)
    k_own = k_ref[0, pl.ds(pl.multiple_of(i * MOBA_BLOCK, MOBA_BLOCK), MOBA_BLOCK), :]
    vt_own = vt_ref[i]

    for h in range(HEADS):
        qm = jnp.where(lane_head == h, q, jnp.zeros_like(q))
        qm_ref[h] = qm
        negd_ref[h] = -slopes[h] * dist0
        gate = jnp.where(blk < i, _nt(kmean, qm), NEG_INF)
        chosen = jnp.zeros((n_blocks, TQ), jnp.bool_)
        for r in range(MOBA_TOPK):
            mx = jnp.max(gate, axis=0, keepdims=True)
            first = jnp.min(jnp.where(gate == mx, blk_f, float(n_blocks)), axis=0, keepdims=True)
            hit = blk_f == first
            chosen = chosen | (hit & (mx > NEG_INF))
            gate = jnp.where(hit, NEG_INF, gate)
        selb_ref[h] = jnp.where(chosen, 0.0, NEG_INF)
        s = _nt(k_own, qm)
        s = jnp.where(dist0 >= 0.0, s - slopes[h] * dist0, NEG_INF)
        m = jnp.max(s, axis=0, keepdims=True)
        p = jnp.exp(s - m)
        m_ref[h] = m
        l_ref[h] = jnp.sum(p, axis=0, keepdims=True)
        acc_ref[h] = _dot(vt_own[h * DH:(h + 1) * DH, :], p.astype(BF16))

    def past_block(j, carry):
        kj = k_ref[0, pl.ds(pl.multiple_of(j * MOBA_BLOCK, MOBA_BLOCK), MOBA_BLOCK), :]
        vtj = vt_ref[j]
        off = jnp.full((1, TQ), (i - j) * MOBA_BLOCK, jnp.int32).astype(F32)
        for h in range(HEADS):
            s = _nt(kj, qm_ref[h])
            s = s + negd_ref[h] + (selb_ref[h, pl.ds(j, 1), :] - slopes[h] * off)
            m_old = m_ref[h]
            m_new = jnp.maximum(m_old, jnp.max(s, axis=0, keepdims=True))
            alpha = jnp.exp(m_old - m_new)
            p = jnp.exp(s - m_new)
            l_ref[h] = alpha * l_ref[h] + jnp.sum(p, axis=0, keepdims=True)
            acc_ref[h] = alpha * acc_ref[h] + _dot(vtj[h * DH:(h + 1) * DH, :], p.astype(BF16))
            m_ref[h] = m_new
        return carry

    lax.fori_loop(0, i, past_block, 0)

    o_t = jnp.concatenate([acc_ref[h] * (1.0 / l_ref[h]) for h in range(HEADS)], axis=0)
    o_moba_ref[0] = o_t.T.astype(BF16)

    cq = cq_ref[0]
    mk = mk_ref[0]
    mvt = mvt_ref[0]
    outs = []
    for h in range(HEADS):
        s = _nt(mk, jnp.where(lane_head == h, cq, jnp.zeros_like(cq)))
        m = jnp.max(s, axis=0, keepdims=True)
        p = jnp.exp(s - m)
        l = jnp.sum(p, axis=0, keepdims=True)
        outs.append(_dot(mvt[h * DH:(h + 1) * DH, :], p.astype(BF16)) * (1.0 / l))
    o_mem_ref[0] = jnp.concatenate(outs, axis=0).T.astype(BF16)


def _moba_mem(mq, mk, mvt, kmean, cq, memk, memvt, b, s):
    nq = s // TQ
    n_blocks = s // MOBA_BLOCK
    qspec = pl.BlockSpec((1, TQ, QK_W), lambda i, j: (i, j, 0))
    return pl.pallas_call(
        functools.partial(_moba_mem_kernel, n_blocks=n_blocks),
        grid=(b, nq),
        in_specs=[
            qspec,
            pl.BlockSpec((1, s, QK_W), lambda i, j: (i, 0, 0)),
            pl.BlockSpec((n_blocks, QK_W, MOBA_BLOCK), lambda i, j: (i, 0, 0)),
            pl.BlockSpec((1, n_blocks, QK_W), lambda i, j: (i, 0, 0)),
            qspec,
            pl.BlockSpec((1, N_MEM, QK_W), lambda i, j: (i, 0, 0)),
            pl.BlockSpec((1, QK_W, N_MEM), lambda i, j: (i, 0, 0)),
        ],
        out_specs=[qspec, qspec],
        out_shape=[jax.ShapeDtypeStruct((b, s, QK_W), BF16), jax.ShapeDtypeStruct((b, s, QK_W), BF16)],
        scratch_shapes=[
            pltpu.VMEM((HEADS, TQ, QK_W), BF16),
            pltpu.VMEM((HEADS, MOBA_BLOCK, TQ), F32),
            pltpu.VMEM((HEADS, n_blocks, TQ), F32),
            pltpu.VMEM((HEADS, 1, TQ), F32),
            pltpu.VMEM((HEADS, 1, TQ), F32),
            pltpu.VMEM((HEADS, DH, TQ), F32),
        ],
        compiler_params=_params(("arbitrary", "arbitrary")),
        name="moba_mem",
    )(mq.reshape(b, s, QK_W), mk.reshape(b, s, QK_W), mvt, kmean.reshape(b, n_blocks, QK_W),
      cq.reshape(b, s, QK_W), memk, memvt)


OUT_TM = 512


def _out_router_kernel(x_ref, og_ref, om_ref, oc_ref, w_ref, g_ref, wr_ref, br_ref,
                       x1_ref, h_ref, e_ref, gate_ref):
    w = w_ref[...]
    x1 = (x_ref[...] + _dot(og_ref[...], w[:GLA_V_W]) + _dot(om_ref[...], w[GLA_V_W:GLA_V_W + QK_W])
          + _dot(oc_ref[...], w[GLA_V_W + QK_W:]))
    x1_ref[...] = x1
    hn = _rms_rows(x1, g_ref[...])
    h_ref[...] = hn
    logits = _dot(hn.astype(BF16), wr_ref[...]) + br_ref[...]
    lane = lax.broadcasted_iota(jnp.int32, logits.shape, 1).astype(F32)
    lg = jnp.where(lane < N_GROUPS, logits, NEG_INF)
    mg = jnp.max(lg, axis=-1, keepdims=True)
    g_sel = jnp.min(jnp.where(lg == mg, lane, float(LANES)), axis=-1, keepdims=True)
    p_group = 1.0 / jnp.sum(jnp.exp(lg - mg), axis=-1, keepdims=True)
    lo = N_GROUPS + g_sel * EXPERTS_PER_GROUP
    le = jnp.where((lane >= lo) & (lane < lo + EXPERTS_PER_GROUP), logits, NEG_INF)
    m0 = jnp.max(le, axis=-1, keepdims=True)
    i0 = jnp.min(jnp.where(le == m0, lane, float(LANES)), axis=-1, keepdims=True)
    le1 = jnp.where(lane == i0, NEG_INF, le)
    m1 = jnp.max(le1, axis=-1, keepdims=True)
    i1 = jnp.min(jnp.where(le1 == m1, lane, float(LANES)), axis=-1, keepdims=True)
    z = jnp.exp(m1 - m0)
    w0 = p_group / (1.0 + z)
    w1 = p_group * z / (1.0 + z)
    ids = jnp.where(lane == 0.0, i0 - N_GROUPS, jnp.where(lane == 1.0, i1 - N_GROUPS, 0.0))
    e_ref[...] = ids.astype(jnp.int32)
    gate_ref[...] = jnp.where(lane == 0.0, w0, jnp.where(lane == 1.0, w1, 0.0))


def _out_router(x2, og, om, oc, w_bf, g, wr_bf, br):
    t = x2.shape[0]
    row = lambda w: pl.BlockSpec((OUT_TM, w), lambda i: (i, 0))
    const = lambda a, b: pl.BlockSpec((a, b), lambda i: (0, 0))
    return pl.pallas_call(
        _out_router_kernel,
        grid=(t // OUT_TM,),
        in_specs=[row(D_MODEL), row(GLA_V_W), row(QK_W), row(QK_W), const(D_MODEL, D_MODEL),
                  const(1, D_MODEL), const(D_MODEL, LANES), const(1, LANES)],
        out_specs=[row(D_MODEL), row(D_MODEL), row(LANES), row(LANES)],
        out_shape=[
            jax.ShapeDtypeStruct((t, D_MODEL), F32),
            jax.ShapeDtypeStruct((t, D_MODEL), F32),
            jax.ShapeDtypeStruct((t, LANES), jnp.int32),
            jax.ShapeDtypeStruct((t, LANES), F32),
        ],
        compiler_params=_params(("arbitrary",)),
        name="out_router",
    )(x2, og, om, oc, w_bf, g, wr_bf, br)


RANK_TM = 256


def _rank_kernel(e_ref, tri_ref, rank_ref, count_ref, carry_ref):
    @pl.when(pl.program_id(0) == 0)
    def _():
        carry_ref[...] = jnp.zeros_like(carry_ref)

    e = e_ref[...].astype(F32)
    lane = lax.broadcasted_iota(jnp.int32, e.shape, 1).astype(F32)
    e0 = jnp.sum(jnp.where(lane == 0.0, e, 0.0), axis=-1, keepdims=True)
    e1 = jnp.sum(jnp.where(lane == 1.0, e, 0.0), axis=-1, keepdims=True)
    oh0 = (lane == e0).astype(F32)
    oh1 = (lane == e1).astype(F32)
    both = oh0 + oh1
    before = _dot(tri_ref[...], both.astype(BF16)) + carry_ref[...]
    r0 = jnp.sum(oh0 * before, axis=-1, keepdims=True)
    r1 = jnp.sum(oh1 * before, axis=-1, keepdims=True)
    rank_ref[...] = jnp.where(lane == 0.0, r0, jnp.where(lane == 1.0, r1, 0.0)).astype(jnp.int32)
    carry_ref[...] = carry_ref[...] + jnp.sum(both, axis=0, keepdims=True)
    count_ref[...] = jnp.broadcast_to(carry_ref[...], count_ref.shape).astype(jnp.int32)


def _rank(e_ids):
    t = e_ids.shape[0]
    idx = np.arange(RANK_TM)
    tri = jnp.asarray(idx[:, None] > idx[None, :], BF16)
    return pl.pallas_call(
        _rank_kernel,
        grid=(t // RANK_TM,),
        in_specs=[pl.BlockSpec((RANK_TM, LANES), lambda i: (i, 0)),
                  pl.BlockSpec((RANK_TM, RANK_TM), lambda i: (0, 0))],
        out_specs=[pl.BlockSpec((RANK_TM, LANES), lambda i: (i, 0)),
                   pl.BlockSpec((8, LANES), lambda i: (0, 0))],
        out_shape=[jax.ShapeDtypeStruct((t, LANES), jnp.int32),
                   jax.ShapeDtypeStruct((8, LANES), jnp.int32)],
        scratch_shapes=[pltpu.VMEM((1, LANES), F32)],
        compiler_params=_params(("arbitrary",)),
        name="rank",
    )(e_ids, tri)


DISP_TM = 512


def _dispatch_kernel(dest_ref, h_ref, xs_in_ref, xs_ref, sem):
    del xs_in_ref
    base = pl.program_id(0) * DISP_TM

    def row_copy(t, d):
        return pltpu.make_async_copy(h_ref.at[pl.ds(t, 1), :], xs_ref.at[pl.ds(d, 1), :], sem)

    def issue(r, c):
        row_copy(base + r, dest_ref[0, 0, 2 * r]).start()
        row_copy(base + r, dest_ref[0, 0, 2 * r + 1]).start()
        return c

    lax.fori_loop(0, DISP_TM, issue, 0)

    def drain(r, c):
        row_copy(0, 0).wait()
        return c

    lax.fori_loop(0, 2 * DISP_TM, drain, 0)


def _dispatch(hn, dest, n_rows):
    t = hn.shape[0]
    nt = t // DISP_TM
    xs0 = jnp.zeros((n_rows, D_MODEL), F32)
    return pl.pallas_call(
        _dispatch_kernel,
        grid=(nt,),
        in_specs=[pl.BlockSpec((1, 1, 2 * DISP_TM), lambda i: (i, 0, 0), memory_space=pltpu.SMEM),
                  pl.BlockSpec(memory_space=pl.ANY),
                  pl.BlockSpec(memory_space=pl.ANY)],
        out_specs=pl.BlockSpec(memory_space=pl.ANY),
        out_shape=jax.ShapeDtypeStruct((n_rows, D_MODEL), F32),
        scratch_shapes=[pltpu.SemaphoreType.DMA],
        input_output_aliases={2: 0},
        compiler_params=_params(("arbitrary",)),
        name="dispatch",
    )(dest.reshape(nt, 1, 2 * DISP_TM), hn, xs0)


def _experts_kernel(be_ref, nused_ref, xs_ref, wg_ref, wu_ref, wd_ref, ys_ref, wg_bf, wu_bf, wd_bf):
    i = pl.program_id(0)
    new_expert = jnp.logical_or(i == 0, be_ref[i] != be_ref[jnp.maximum(i - 1, 0)])

    @pl.when(new_expert)
    def _():
        wg_bf[...] = wg_ref[0].astype(BF16)
        wu_bf[...] = wu_ref[0].astype(BF16)
        wd_bf[...] = wd_ref[0].astype(BF16)

    @pl.when(i < nused_ref[0])
    def _():
        xb = xs_ref[...].astype(BF16)
        hid = _silu(_dot(xb, wg_bf[...])) * _dot(xb, wu_bf[...])
        ys_ref[...] = _dot(hid.astype(BF16), wd_bf[...])

    @pl.when(i >= nused_ref[0])
    def _():
        ys_ref[...] = jnp.zeros_like(ys_ref)


def _experts(xs, block_expert, n_used, w_gate, w_up, w_down):
    n_rows = xs.shape[0]
    nb = n_rows // ROW_BLOCK
    grid_spec = pltpu.PrefetchScalarGridSpec(
        num_scalar_prefetch=2,
        grid=(nb,),
        in_specs=[
            pl.BlockSpec((ROW_BLOCK, D_MODEL), lambda i, be, nu: (i, 0)),
            pl.BlockSpec((1, D_MODEL, MOE_FF), lambda i, be, nu: (be[i], 0, 0)),
            pl.BlockSpec((1, D_MODEL, MOE_FF), lambda i, be, nu: (be[i], 0, 0)),
            pl.BlockSpec((1, MOE_FF, D_MODEL), lambda i, be, nu: (be[i], 0, 0)),
        ],
        out_specs=pl.BlockSpec((ROW_BLOCK, D_MODEL), lambda i, be, nu: (i, 0)),
        scratch_shapes=[pltpu.VMEM((D_MODEL, MOE_FF), BF16), pltpu.VMEM((D_MODEL, MOE_FF), BF16),
                        pltpu.VMEM((MOE_FF, D_MODEL), BF16)],
    )
    return pl.pallas_call(
        _experts_kernel,
        grid_spec=grid_spec,
        out_shape=jax.ShapeDtypeStruct((n_rows, D_MODEL), F32),
        compiler_params=_params(("arbitrary",)),
        name="experts",
    )(block_expert, n_used, xs, w_gate, w_up, w_down)


COMB_TM = 256


def _combine_kernel(dest_ref, x1_ref, gate_ref, ys_ref, out_ref, ybuf, sem):
    def row_copy(d, k, r):
        return pltpu.make_async_copy(ys_ref.at[pl.ds(d, 1), :], ybuf.at[k, pl.ds(r, 1), :], sem)

    def issue(r, c):
        row_copy(dest_ref[0, 0, 2 * r], 0, r).start()
        row_copy(dest_ref[0, 0, 2 * r + 1], 1, r).start()
        return c

    lax.fori_loop(0, COMB_TM, issue, 0)

    def drain(r, c):
        row_copy(0, 0, 0).wait()
        return c

    lax.fori_loop(0, 2 * COMB_TM, drain, 0)

    gate = gate_ref[...]
    lane = lax.broadcasted_iota(jnp.int32, gate.shape, 1)
    w0 = jnp.sum(jnp.where(lane == 0, gate, 0.0), axis=-1, keepdims=True)
    w1 = jnp.sum(jnp.where(lane == 1, gate, 0.0), axis=-1, keepdims=True)
    out_ref[...] = x1_ref[...] + (ybuf[0] * w0 + ybuf[1] * w1)


def _combine(x1, gates, ys, dest):
    t = x1.shape[0]
    nt = t // COMB_TM
    return pl.pallas_call(
        _combine_kernel,
        grid=(nt,),
        in_specs=[pl.BlockSpec((1, 1, 2 * COMB_TM), lambda i: (i, 0, 0), memory_space=pltpu.SMEM),
                  pl.BlockSpec((COMB_TM, D_MODEL), lambda i: (i, 0)),
                  pl.BlockSpec((COMB_TM, LANES), lambda i: (i, 0)),
                  pl.BlockSpec(memory_space=pl.ANY)],
        out_specs=pl.BlockSpec((COMB_TM, D_MODEL), lambda i: (i, 0)),
        out_shape=jax.ShapeDtypeStruct((t, D_MODEL), F32),
        scratch_shapes=[pltpu.VMEM((2, COMB_TM, D_MODEL), F32), pltpu.SemaphoreType.DMA],
        compiler_params=_params(("arbitrary",)),
        name="combine",
    )(dest.reshape(nt, 1, 2 * COMB_TM), x1, gates, ys)


def _layer(x, mem, attn_norm_g, mem_norm_g, w_in, w_gla_gk, b_gla_gk, gla_out_norm_g,
           moba_q_norm_g, moba_k_norm_g, w_mem_kv, mem_q_norm_g, mem_k_norm_g, w_out,
           ffn_norm_g, w_router_group, b_router_group, w_router_expert, b_router_expert,
           w_gate, w_up, w_down):
    b, s, d = x.shape
    t = b * s
    x2 = x.reshape(t, d)
    row = lambda v: v.reshape(1, -1).astype(F32)
    tile_heads = lambda v: jnp.tile(v.astype(F32), HEADS).reshape(1, QK_W)
    hid = np.arange(QK_W) // DH
    seg = jnp.asarray((hid[:, None] == hid[None, :]) / DH, BF16)

    w_in_p = jnp.concatenate([w_in[:, :1536], w_in[:, 1552:], w_in[:, 1536:1552]], axis=1).astype(BF16)
    wr = jnp.concatenate([w_router_group,
                          jnp.transpose(w_router_expert, (1, 0, 2)).reshape(d, N_EXPERTS),
                          jnp.zeros((d, LANES - N_GROUPS - N_EXPERTS), F32)], axis=1).astype(BF16)
    br = jnp.concatenate([b_router_group, b_router_expert.reshape(N_EXPERTS),
                          jnp.zeros((LANES - N_GROUPS - N_EXPERTS,), F32)]).reshape(1, LANES)

    memk, memvt = _mem_kv(mem, row(mem_norm_g), w_mem_kv.astype(BF16), tile_heads(mem_k_norm_g), seg)
    qk, gv, gr, glr, mq, mk, mvt, kmean, cq = _in_proj(
        x2, row(attn_norm_g), w_in_p, seg, tile_heads(moba_q_norm_g), tile_heads(moba_k_norm_g),
        tile_heads(mem_q_norm_g))
    o_gla = _gla(qk, gv, gr, glr, w_gla_gk.astype(BF16), row(b_gla_gk), row(gla_out_norm_g), b, s)
    o_moba, o_mem = _moba_mem(mq, mk, mvt, kmean, cq, memk, memvt, b, s)
    x1, hn, e_ids, gates = _out_router(
        x2, o_gla.reshape(t, GLA_V_W), o_moba.reshape(t, QK_W), o_mem.reshape(t, QK_W),
        w_out.astype(BF16), row(ffn_norm_g), wr, br)

    rank, counts = _rank(e_ids)
    counts = counts[0, :N_EXPERTS]
    padded = (counts + ROW_BLOCK - 1) // ROW_BLOCK * ROW_BLOCK
    pends = jnp.cumsum(padded)
    pstarts = pends - padded
    e2 = e_ids[:, :2]
    dest = (pstarts[e2] + rank[:, :2]).astype(jnp.int32).reshape(-1)
    n_rows = (t * 2 + N_EXPERTS * (ROW_BLOCK - 1) + ROW_BLOCK - 1) // ROW_BLOCK * ROW_BLOCK
    nb = n_rows // ROW_BLOCK
    block_start = jnp.arange(nb, dtype=jnp.int32) * ROW_BLOCK
    block_expert = jnp.minimum(jnp.searchsorted(pends, block_start, side="right"),
                               N_EXPERTS - 1).astype(jnp.int32)
    n_used = (pends[-1] // ROW_BLOCK).astype(jnp.int32).reshape(1)

    xs = _dispatch(hn, dest, n_rows)
    ys = _experts(xs, block_expert, n_used, w_gate, w_up, w_down)
    out = _combine(x1, gates, ys, dest)
    return out.reshape(b, s, d)


def kernel(x, mem, attn_norm_g, mem_norm_g, w_in, w_gla_gk, b_gla_gk, gla_out_norm_g, moba_q_norm_g, moba_k_norm_g, w_mem_kv, mem_q_norm_g, mem_k_norm_g, w_out, ffn_norm_g, w_router_group, b_router_group, w_router_expert, b_router_expert, w_gate, w_up, w_down):
    depth = w_in.shape[0]
    for l in range(depth):
        x = _layer(x, mem, attn_norm_g[l], mem_norm_g[l], w_in[l], w_gla_gk[l], b_gla_gk[l],
                   gla_out_norm_g[l], moba_q_norm_g[l], moba_k_norm_g[l], w_mem_kv[l],
                   mem_q_norm_g[l], mem_k_norm_g[l], w_out[l], ffn_norm_g[l], w_router_group[l],
                   b_router_group[l], w_router_expert[l], b_router_expert[l],
                   w_gate[l], w_up[l], w_down[l])
    return x
```

```python
import functools

import jax
import jax.numpy as jnp
import numpy as np
from jax import lax
from jax.experimental import pallas as pl
from jax.experimental.pallas import tpu as pltpu

F32 = jnp.float32
BF16 = jnp.bfloat16
EPS = 1e-6
NEG_INF = float("-inf")

D_MODEL = 1024
N_MEM = 256
HEADS = 4
DH = 64
GLA_DV = 128
GLA_RANK = 16
GLA_GATE_NORMALIZER = 16.0
GLA_CHUNK = 64
MOBA_BLOCK = 256
MOBA_TOPK = 3
QK_W = HEADS * DH
GLA_V_W = HEADS * GLA_DV
N_GROUPS = 4
EXPERTS_PER_GROUP = 8
N_EXPERTS = N_GROUPS * EXPERTS_PER_GROUP
MOE_FF = 512
LANES = 128
ROW_BLOCK = 256

VMEM_LIMIT = 56 * 1024 * 1024


def _params(sem):
    return pltpu.CompilerParams(dimension_semantics=sem, vmem_limit_bytes=VMEM_LIMIT)


def _nt(a, b):
    return lax.dot_general(a, b, (((1,), (1,)), ((), ())), preferred_element_type=F32)


def _tn(a, b):
    return lax.dot_general(a, b, (((0,), (0,)), ((), ())), preferred_element_type=F32)


def _dot(a, b):
    return jnp.dot(a, b, preferred_element_type=F32)


def _rms_rows(x, g):
    ms = jnp.mean(x * x, axis=-1, keepdims=True)
    return x * lax.rsqrt(ms + EPS) * g


def _split3(x):
    h1 = x.astype(BF16)
    r1 = x - h1.astype(F32)
    h2 = r1.astype(BF16)
    h3 = (r1 - h2.astype(F32)).astype(BF16)
    return h1, h2, h3


def _head_mean_sq(x, seg):
    sq = x * x
    hi = sq.astype(BF16)
    lo = (sq - hi.astype(F32)).astype(BF16)
    return _dot(hi, seg) + _dot(lo, seg)


def _silu(x):
    return x * (1.0 / (1.0 + jnp.exp(-x)))


def _mem_kv_kernel(mem_ref, g_ref, w_ref, gk_ref, seg_ref, k_ref, vt_ref):
    h = _rms_rows(mem_ref[0], g_ref[...]).astype(BF16)
    kv = _dot(h, w_ref[...])
    k = kv[:, :QK_W]
    kn = k * lax.rsqrt(_head_mean_sq(k, seg_ref[...]) + EPS) * gk_ref[...]
    k_ref[0] = kn.astype(BF16)
    vt_ref[0] = kv[:, QK_W:].T.astype(BF16)


def _mem_kv(mem, g, w_bf, gk_t, seg):
    b = mem.shape[0]
    return pl.pallas_call(
        _mem_kv_kernel,
        grid=(b,),
        in_specs=[
            pl.BlockSpec((1, N_MEM, D_MODEL), lambda i: (i, 0, 0)),
            pl.BlockSpec((1, D_MODEL), lambda i: (0, 0)),
            pl.BlockSpec((D_MODEL, 2 * QK_W), lambda i: (0, 0)),
            pl.BlockSpec((1, QK_W), lambda i: (0, 0)),
            pl.BlockSpec((QK_W, QK_W), lambda i: (0, 0)),
        ],
        out_specs=[
            pl.BlockSpec((1, N_MEM, QK_W), lambda i: (i, 0, 0)),
            pl.BlockSpec((1, QK_W, N_MEM), lambda i: (i, 0, 0)),
        ],
        out_shape=[
            jax.ShapeDtypeStruct((b, N_MEM, QK_W), BF16),
            jax.ShapeDtypeStruct((b, QK_W, N_MEM), BF16),
        ],
        compiler_params=_params(("arbitrary",)),
        name="mem_kv",
    )(mem, g, w_bf, gk_t, seg)


IN_TM = 512
_C_QK, _C_V, _C_R, _C_MQ, _C_MK, _C_MV, _C_CQ, _C_LR = 0, 512, 1024, 1536, 1792, 2048, 2304, 2560
D_IN = 2576


def _in_proj_kernel(x_ref, g_ref, w_ref, seg_ref, gq_ref, gk_ref, gc_ref,
                    qk_ref, v_ref, r_ref, lr_ref, mq_ref, mk_ref, mvt_ref, kmean_ref, cq_ref):
    h = _rms_rows(x_ref[...], g_ref[...]).astype(BF16)
    p = _dot(h, w_ref[...])
    seg = seg_ref[...]
    qk_ref[...] = p[:, _C_QK:_C_V]
    v_ref[...] = p[:, _C_V:_C_R].astype(BF16)
    r_ref[...] = p[:, _C_R:_C_MQ]
    lr_ref[...] = p[:, _C_LR:D_IN]

    def head_norm(t, gain):
        return t * lax.rsqrt(_head_mean_sq(t, seg) + EPS) * gain

    scale = DH ** -0.5
    mq_ref[...] = (head_norm(p[:, _C_MQ:_C_MK], gq_ref[...]) * scale).astype(BF16)
    cq_ref[...] = (head_norm(p[:, _C_CQ:_C_LR], gc_ref[...]) * scale).astype(BF16)
    kn = head_norm(p[:, _C_MK:_C_MV], gk_ref[...])
    mk_ref[...] = kn.astype(BF16)
    mv = p[:, _C_MV:_C_CQ]
    for j in range(IN_TM // MOBA_BLOCK):
        rows = slice(j * MOBA_BLOCK, (j + 1) * MOBA_BLOCK)
        kmean_ref[0, j:j + 1, :] = jnp.mean(kn[rows], axis=0, keepdims=True)
        mvt_ref[j] = mv[rows].T.astype(BF16)


def _in_proj(x2, g, w_bf, seg, gq_t, gk_t, gc_t):
    t = x2.shape[0]
    nt = t // IN_TM
    nb = IN_TM // MOBA_BLOCK
    row = lambda w: pl.BlockSpec((IN_TM, w), lambda i: (i, 0))
    const = lambda a, b: pl.BlockSpec((a, b), lambda i: (0, 0))
    return pl.pallas_call(
        _in_proj_kernel,
        grid=(nt,),
        in_specs=[row(D_MODEL), const(1, D_MODEL), const(D_MODEL, D_IN), const(QK_W, QK_W),
                  const(1, QK_W), const(1, QK_W), const(1, QK_W)],
        out_specs=[row(2 * QK_W), row(GLA_V_W), row(GLA_V_W), row(GLA_RANK), row(QK_W), row(QK_W),
                   pl.BlockSpec((nb, QK_W, MOBA_BLOCK), lambda i: (i, 0, 0)),
                   pl.BlockSpec((1, nb, QK_W), lambda i: (i, 0, 0)),
                   row(QK_W)],
        out_shape=[
            jax.ShapeDtypeStruct((t, 2 * QK_W), F32),
            jax.ShapeDtypeStruct((t, GLA_V_W), BF16),
            jax.ShapeDtypeStruct((t, GLA_V_W), F32),
            jax.ShapeDtypeStruct((t, GLA_RANK), F32),
            jax.ShapeDtypeStruct((t, QK_W), BF16),
            jax.ShapeDtypeStruct((t, QK_W), BF16),
            jax.ShapeDtypeStruct((t // MOBA_BLOCK, QK_W, MOBA_BLOCK), BF16),
            jax.ShapeDtypeStruct((nt, nb, QK_W), F32),
            jax.ShapeDtypeStruct((t, QK_W), BF16),
        ],
        compiler_params=_params(("arbitrary",)),
        name="in_proj",
    )(x2, g, w_bf, seg, gq_t, gk_t, gc_t)


GLA_TC = 256


def _gla_kernel(qk_ref, v_ref, r_ref, lr_ref, wgk_ref, bgk_ref, gn_ref, tri_ref, ones_ref,
                o_ref, st_ref):
    @pl.when(pl.program_id(1) == 0)
    def _():
        st_ref[...] = jnp.zeros_like(st_ref)

    qk = qk_ref[0]
    q = qk[:, :QK_W]
    k = qk[:, QK_W:]
    gk = _dot(lr_ref[0].astype(BF16), wgk_ref[...]) + bgk_ref[...]
    g = -(jnp.maximum(-gk, 0.0) + jnp.log1p(jnp.exp(-jnp.abs(gk)))) / GLA_GATE_NORMALIZER
    g1, g2, g3 = _split3(g)
    tri = tri_ref[...]
    ones = ones_ref[...]
    cum = _dot(tri, g1) + _dot(tri, g2) + _dot(tri, g3)
    tot = _dot(ones, g1) + _dot(ones, g2) + _dot(ones, g3)
    q_dec = (q * (DH ** -0.5) * jnp.exp(cum)).astype(BF16)
    k_inv = (k * jnp.exp(-cum)).astype(BF16)
    k_end = (k * jnp.exp(tot - cum)).astype(BF16)
    decay = jnp.exp(tot)

    lane_head = lax.broadcasted_iota(jnp.int32, (GLA_CHUNK, QK_W), 1) // DH
    causal = (lax.broadcasted_iota(jnp.int32, (GLA_CHUNK, GLA_CHUNK), 0)
              >= lax.broadcasted_iota(jnp.int32, (GLA_CHUNK, GLA_CHUNK), 1))
    same_head = (lax.broadcasted_iota(jnp.int32, (GLA_V_W, QK_W), 0) // GLA_DV
                 == lax.broadcasted_iota(jnp.int32, (GLA_V_W, QK_W), 1) // DH)
    gain = gn_ref[...]

    for c in range(GLA_TC // GLA_CHUNK):
        rows = slice(c * GLA_CHUNK, (c + 1) * GLA_CHUNK)
        qd, ki, ke, vc = q_dec[rows], k_inv[rows], k_end[rows], v_ref[0, rows, :]
        st = st_ref[...]
        o_inter = _nt(qd, st.astype(BF16))
        outs = []
        for h in range(HEADS):
            a = _nt(jnp.where(lane_head == h, qd, jnp.zeros_like(qd)), ki)
            a = jnp.where(causal, a, 0.0).astype(BF16)
            vs = slice(h * GLA_DV, (h + 1) * GLA_DV)
            oh = _dot(a, vc[:, vs]) + o_inter[:, vs]
            y = _rms_rows(oh, gain)
            outs.append(y * _silu(r_ref[0, rows, vs]))
        o_ref[0, rows, :] = jnp.concatenate(outs, axis=-1).astype(BF16)
        d_st = _tn(vc, ke)
        st_ref[...] = st * decay[c * GLA_CHUNK:c * GLA_CHUNK + 1, :] + jnp.where(same_head, d_st, 0.0)


def _gla(qk, v, r, lr, wgk_bf, bgk, gn, b, s):
    ns = s // GLA_TC
    idx = np.arange(GLA_TC)
    same_chunk = (idx[:, None] // GLA_CHUNK) == (idx[None, :] // GLA_CHUNK)
    tri = jnp.asarray(same_chunk & (idx[:, None] >= idx[None, :]), BF16)
    ones = jnp.asarray(same_chunk, BF16)
    seq = lambda w: pl.BlockSpec((1, GLA_TC, w), lambda i, j: (i, j, 0))
    const = lambda a, c: pl.BlockSpec((a, c), lambda i, j: (0, 0))
    return pl.pallas_call(
        _gla_kernel,
        grid=(b, ns),
        in_specs=[seq(2 * QK_W), seq(GLA_V_W), seq(GLA_V_W), seq(GLA_RANK),
                  const(GLA_RANK, QK_W), const(1, QK_W), const(1, GLA_DV),
                  const(GLA_TC, GLA_TC), const(GLA_TC, GLA_TC)],
        out_specs=seq(GLA_V_W),
        out_shape=jax.ShapeDtypeStruct((b, s, GLA_V_W), BF16),
        scratch_shapes=[pltpu.VMEM((GLA_V_W, QK_W), F32)],
        compiler_params=_params(("arbitrary", "arbitrary")),
        name="gla",
    )(qk.reshape(b, s, 2 * QK_W), v.reshape(b, s, GLA_V_W), r.reshape(b, s, GLA_V_W),
      lr.reshape(b, s, GLA_RANK), wgk_bf, bgk, gn, tri, ones)


TQ = MOBA_BLOCK
KV_UNROLL = 8


def _moba_mem_kernel(q_ref, k_ref, vt_ref, kmean_ref, cq_ref, mk_ref, mvt_ref,
                     o_moba_ref, o_mem_ref,
                     qm_ref, negd_ref, selb_ref, *, n_blocks):
    i = pl.program_id(1)
    lane_head = lax.broadcasted_iota(jnp.int32, (TQ, QK_W), 1) // DH
    dist0 = (lax.broadcasted_iota(jnp.int32, (MOBA_BLOCK, TQ), 1)
             - lax.broadcasted_iota(jnp.int32, (MOBA_BLOCK, TQ), 0)).astype(F32)
    slopes = [2.0 ** (-8.0 * (h + 1) / HEADS) for h in range(HEADS)]

    q = q_ref[0]
    kmean = kmean_ref[0].astype(BF16)
    blk = lax.broadcasted_iota(jnp.int32, (n_blocks, TQ), 0)
    blk_f = blk.astype(F32)
    k_own = k_ref[0, pl.ds(pl.multiple_of(i * MOBA_BLOCK, MOBA_BLOCK), MOBA_BLOCK), :]
    ones_l = jnp.ones((16, MOBA_BLOCK), BF16)
    n_groups = (i + KV_UNROLL - 1) // KV_UNROLL

    outs = []
    for h in range(HEADS):
        hd = slice(h * DH, (h + 1) * DH)
        qm = jnp.where(lane_head == h, q, jnp.zeros_like(q))
        qm_ref[...] = qm
        negd_ref[...] = -slopes[h] * dist0
        gate = jnp.where(blk < i, _nt(kmean, qm), NEG_INF)
        chosen = jnp.zeros((n_blocks, TQ), jnp.bool_)
        for r in range(MOBA_TOPK):
            mx = jnp.max(gate, axis=0, keepdims=True)
            first = jnp.min(jnp.where(gate == mx, blk_f, float(n_blocks)), axis=0, keepdims=True)
            hit = blk_f == first
            chosen = chosen | (hit & (mx > NEG_INF))
            gate = jnp.where(hit, NEG_INF, gate)
        selb_ref[...] = jnp.where(chosen, 0.0, NEG_INF)
        s = _nt(k_own, qm)
        s = jnp.where(dist0 >= 0.0, s - slopes[h] * dist0, NEG_INF)
        m0 = jnp.max(s, axis=0, keepdims=True)
        p = jnp.exp(s - m0).astype(BF16)
        l0 = _dot(ones_l, p)[0:1]
        acc0 = _dot(vt_ref[i, hd, :], p)

        def past_blocks(g, carry, h=h, hd=hd):
            m, l, acc = carry
            for u in range(KV_UNROLL):
                j = g * KV_UNROLL + u
                kj = k_ref[0, pl.ds(pl.multiple_of(j * MOBA_BLOCK, MOBA_BLOCK), MOBA_BLOCK), :]
                s1 = _nt(kj, qm_ref[...]) + negd_ref[...]
                off = jnp.full((1, TQ), (i - j) * MOBA_BLOCK, jnp.int32).astype(F32)
                rb = selb_ref[pl.ds(j, 1), :] - slopes[h] * off
                m_new = jnp.maximum(m, jnp.max(s1, axis=0, keepdims=True) + rb)
                p = jnp.exp(s1 - (m_new - rb)).astype(BF16)
                alpha = jnp.exp(m - m_new)
                acc = alpha * acc + _dot(vt_ref[j, hd, :], p)
                l = alpha * l + _dot(ones_l, p)[0:1]
                m = m_new
            return m, l, acc

        m, l, acc = lax.fori_loop(0, n_groups, past_blocks, (m0, l0, acc0))
        outs.append(acc * (1.0 / l))

    o_moba_ref[0] = jnp.concatenate(outs, axis=0).T.astype(BF16)

    cq = cq_ref[0]
    mk = mk_ref[0]
    mvt = mvt_ref[0]
    outs = []
    for h in range(HEADS):
        s = _nt(mk, jnp.where(lane_head == h, cq, jnp.zeros_like(cq)))
        m = jnp.max(s, axis=0, keepdims=True)
        p = jnp.exp(s - m)
        l = jnp.sum(p, axis=0, keepdims=True)
        outs.append(_dot(mvt[h * DH:(h + 1) * DH, :], p.astype(BF16)) * (1.0 / l))
    o_mem_ref[0] = jnp.concatenate(outs, axis=0).T.astype(BF16)


def _moba_mem(mq, mk, mvt, kmean, cq, memk, memvt, b, s):
    nq = s // TQ
    n_blocks = s // MOBA_BLOCK
    qspec = pl.BlockSpec((1, TQ, QK_W), lambda i, j: (i, j, 0))
    return pl.pallas_call(
        functools.partial(_moba_mem_kernel, n_blocks=n_blocks),
        grid=(b, nq),
        in_specs=[
            qspec,
            pl.BlockSpec((1, s, QK_W), lambda i, j: (i, 0, 0)),
            pl.BlockSpec((n_blocks, QK_W, MOBA_BLOCK), lambda i, j: (i, 0, 0)),
            pl.BlockSpec((1, n_blocks, QK_W), lambda i, j: (i, 0, 0)),
            qspec,
            pl.BlockSpec((1, N_MEM, QK_W), lambda i, j: (i, 0, 0)),
            pl.BlockSpec((1, QK_W, N_MEM), lambda i, j: (i, 0, 0)),
        ],
        out_specs=[qspec, qspec],
        out_shape=[jax.ShapeDtypeStruct((b, s, QK_W), BF16), jax.ShapeDtypeStruct((b, s, QK_W), BF16)],
        scratch_shapes=[
            pltpu.VMEM((TQ, QK_W), BF16),
            pltpu.VMEM((MOBA_BLOCK, TQ), F32),
            pltpu.VMEM((n_blocks, TQ), F32),
        ],
        compiler_params=_params(("arbitrary", "arbitrary")),
        name="moba_mem",
    )(mq.reshape(b, s, QK_W), mk.reshape(b, s, QK_W), mvt, kmean.reshape(b, n_blocks, QK_W),
      cq.reshape(b, s, QK_W), memk, memvt)


OUT_TM = 512


def _out_router_kernel(x_ref, og_ref, om_ref, oc_ref, w_ref, g_ref, wr_ref, br_ref,
                       x1_ref, h_ref, e_ref, gate_ref):
    w = w_ref[...]
    x1 = (x_ref[...] + _dot(og_ref[...], w[:GLA_V_W]) + _dot(om_ref[...], w[GLA_V_W:GLA_V_W + QK_W])
          + _dot(oc_ref[...], w[GLA_V_W + QK_W:]))
    x1_ref[...] = x1
    hn = _rms_rows(x1, g_ref[...])
    h_ref[...] = hn
    logits = _dot(hn.astype(BF16), wr_ref[...]) + br_ref[...]
    lane = lax.broadcasted_iota(jnp.int32, logits.shape, 1).astype(F32)
    lg = jnp.where(lane < N_GROUPS, logits, NEG_INF)
    mg = jnp.max(lg, axis=-1, keepdims=True)
    g_sel = jnp.min(jnp.where(lg == mg, lane, float(LANES)), axis=-1, keepdims=True)
    p_group = 1.0 / jnp.sum(jnp.exp(lg - mg), axis=-1, keepdims=True)
    lo = N_GROUPS + g_sel * EXPERTS_PER_GROUP
    le = jnp.where((lane >= lo) & (lane < lo + EXPERTS_PER_GROUP), logits, NEG_INF)
    m0 = jnp.max(le, axis=-1, keepdims=True)
    i0 = jnp.min(jnp.where(le == m0, lane, float(LANES)), axis=-1, keepdims=True)
    le1 = jnp.where(lane == i0, NEG_INF, le)
    m1 = jnp.max(le1, axis=-1, keepdims=True)
    i1 = jnp.min(jnp.where(le1 == m1, lane, float(LANES)), axis=-1, keepdims=True)
    z = jnp.exp(m1 - m0)
    w0 = p_group / (1.0 + z)
    w1 = p_group * z / (1.0 + z)
    ids = jnp.where(lane == 0.0, i0 - N_GROUPS, jnp.where(lane == 1.0, i1 - N_GROUPS, 0.0))
    e_ref[...] = ids.astype(jnp.int32)
    gate_ref[...] = jnp.where(lane == 0.0, w0, jnp.where(lane == 1.0, w1, 0.0))


def _out_router(x2, og, om, oc, w_bf, g, wr_bf, br):
    t = x2.shape[0]
    row = lambda w: pl.BlockSpec((OUT_TM, w), lambda i: (i, 0))
    const = lambda a, b: pl.BlockSpec((a, b), lambda i: (0, 0))
    return pl.pallas_call(
        _out_router_kernel,
        grid=(t // OUT_TM,),
        in_specs=[row(D_MODEL), row(GLA_V_W), row(QK_W), row(QK_W), const(D_MODEL, D_MODEL),
                  const(1, D_MODEL), const(D_MODEL, LANES), const(1, LANES)],
        out_specs=[row(D_MODEL), row(D_MODEL), row(LANES), row(LANES)],
        out_shape=[
            jax.ShapeDtypeStruct((t, D_MODEL), F32),
            jax.ShapeDtypeStruct((t, D_MODEL), F32),
            jax.ShapeDtypeStruct((t, LANES), jnp.int32),
            jax.ShapeDtypeStruct((t, LANES), F32),
        ],
        compiler_params=_params(("arbitrary",)),
        name="out_router",
    )(x2, og, om, oc, w_bf, g, wr_bf, br)


RANK_TM = 256


def _rank_kernel(e_ref, tri_ref, rank_ref, count_ref, carry_ref):
    @pl.when(pl.program_id(0) == 0)
    def _():
        carry_ref[...] = jnp.zeros_like(carry_ref)

    e = e_ref[...].astype(F32)
    lane = lax.broadcasted_iota(jnp.int32, e.shape, 1).astype(F32)
    e0 = jnp.sum(jnp.where(lane == 0.0, e, 0.0), axis=-1, keepdims=True)
    e1 = jnp.sum(jnp.where(lane == 1.0, e, 0.0), axis=-1, keepdims=True)
    oh0 = (lane == e0).astype(F32)
    oh1 = (lane == e1).astype(F32)
    both = oh0 + oh1
    before = _dot(tri_ref[...], both.astype(BF16)) + carry_ref[...]
    r0 = jnp.sum(oh0 * before, axis=-1, keepdims=True)
    r1 = jnp.sum(oh1 * before, axis=-1, keepdims=True)
    rank_ref[...] = jnp.where(lane == 0.0, r0, jnp.where(lane == 1.0, r1, 0.0)).astype(jnp.int32)
    carry_ref[...] = carry_ref[...] + jnp.sum(both, axis=0, keepdims=True)
    count_ref[...] = jnp.broadcast_to(carry_ref[...], count_ref.shape).astype(jnp.int32)


def _rank(e_ids):
    t = e_ids.shape[0]
    idx = np.arange(RANK_TM)
    tri = jnp.asarray(idx[:, None] > idx[None, :], BF16)
    return pl.pallas_call(
        _rank_kernel,
        grid=(t // RANK_TM,),
        in_specs=[pl.BlockSpec((RANK_TM, LANES), lambda i: (i, 0)),
                  pl.BlockSpec((RANK_TM, RANK_TM), lambda i: (0, 0))],
        out_specs=[pl.BlockSpec((RANK_TM, LANES), lambda i: (i, 0)),
                   pl.BlockSpec((8, LANES), lambda i: (0, 0))],
        out_shape=[jax.ShapeDtypeStruct((t, LANES), jnp.int32),
                   jax.ShapeDtypeStruct((8, LANES), jnp.int32)],
        scratch_shapes=[pltpu.VMEM((1, LANES), F32)],
        compiler_params=_params(("arbitrary",)),
        name="rank",
    )(e_ids, tri)


DISP_TM = 512


def _dispatch_kernel(dest_ref, h_ref, xs_in_ref, xs_ref, sem):
    del xs_in_ref

    def row_copy(r, d):
        return pltpu.make_async_copy(h_ref.at[pl.ds(r, 1), :], xs_ref.at[pl.ds(d, 1), :], sem)

    def issue(r, c):
        row_copy(r, dest_ref[0, 0, 2 * r]).start()
        row_copy(r, dest_ref[0, 0, 2 * r + 1]).start()
        return c

    lax.fori_loop(0, DISP_TM, issue, 0)

    def drain(r, c):
        row_copy(0, 0).wait()
        return c

    lax.fori_loop(0, 2 * DISP_TM, drain, 0)


def _dispatch(hn, dest, n_rows):
    t = hn.shape[0]
    nt = t // DISP_TM
    xs0 = jnp.zeros((n_rows, D_MODEL), F32)
    return pl.pallas_call(
        _dispatch_kernel,
        grid=(nt,),
        in_specs=[pl.BlockSpec((1, 1, 2 * DISP_TM), lambda i: (i, 0, 0), memory_space=pltpu.SMEM),
                  pl.BlockSpec((DISP_TM, D_MODEL), lambda i: (i, 0)),
                  pl.BlockSpec(memory_space=pl.ANY)],
        out_specs=pl.BlockSpec(memory_space=pl.ANY),
        out_shape=jax.ShapeDtypeStruct((n_rows, D_MODEL), F32),
        scratch_shapes=[pltpu.SemaphoreType.DMA],
        input_output_aliases={2: 0},
        compiler_params=_params(("arbitrary",)),
        name="dispatch",
    )(dest.reshape(nt, 1, 2 * DISP_TM), hn, xs0)


def _experts_kernel(be_ref, nused_ref, xs_ref, wg_ref, wu_ref, wd_ref, ys_ref, wg_bf, wu_bf, wd_bf):
    i = pl.program_id(0)
    new_expert = jnp.logical_or(i == 0, be_ref[i] != be_ref[jnp.maximum(i - 1, 0)])

    @pl.when(new_expert)
    def _():
        wg_bf[...] = wg_ref[0].astype(BF16)
        wu_bf[...] = wu_ref[0].astype(BF16)
        wd_bf[...] = wd_ref[0].astype(BF16)

    @pl.when(i < nused_ref[0])
    def _():
        xb = xs_ref[...].astype(BF16)
        hid = _silu(_dot(xb, wg_bf[...])) * _dot(xb, wu_bf[...])
        ys_ref[...] = _dot(hid.astype(BF16), wd_bf[...])

    @pl.when(i >= nused_ref[0])
    def _():
        ys_ref[...] = jnp.zeros_like(ys_ref)


def _experts(xs, block_expert, n_used, w_gate, w_up, w_down):
    n_rows = xs.shape[0]
    nb = n_rows // ROW_BLOCK
    grid_spec = pltpu.PrefetchScalarGridSpec(
        num_scalar_prefetch=2,
        grid=(nb,),
        in_specs=[
            pl.BlockSpec((ROW_BLOCK, D_MODEL), lambda i, be, nu: (i, 0)),
            pl.BlockSpec((1, D_MODEL, MOE_FF), lambda i, be, nu: (be[i], 0, 0)),
            pl.BlockSpec((1, D_MODEL, MOE_FF), lambda i, be, nu: (be[i], 0, 0)),
            pl.BlockSpec((1, MOE_FF, D_MODEL), lambda i, be, nu: (be[i], 0, 0)),
        ],
        out_specs=pl.BlockSpec((ROW_BLOCK, D_MODEL), lambda i, be, nu: (i, 0)),
        scratch_shapes=[pltpu.VMEM((D_MODEL, MOE_FF), BF16), pltpu.VMEM((D_MODEL, MOE_FF), BF16),
                        pltpu.VMEM((MOE_FF, D_MODEL), BF16)],
    )
    return pl.pallas_call(
        _experts_kernel,
        grid_spec=grid_spec,
        out_shape=jax.ShapeDtypeStruct((n_rows, D_MODEL), F32),
        compiler_params=_params(("arbitrary",)),
        name="experts",
    )(block_expert, n_used, xs, w_gate, w_up, w_down)


COMB_TM = 256


def _combine_kernel(dest_ref, x1_ref, gate_ref, ys_ref, out_ref, ybuf, sem):
    def row_copy(d, k, r):
        return pltpu.make_async_copy(ys_ref.at[pl.ds(d, 1), :], ybuf.at[k, pl.ds(r, 1), :], sem)

    def issue(r, c):
        row_copy(dest_ref[0, 0, 2 * r], 0, r).start()
        row_copy(dest_ref[0, 0, 2 * r + 1], 1, r).start()
        return c

    lax.fori_loop(0, COMB_TM, issue, 0)

    def drain(r, c):
        row_copy(0, 0, 0).wait()
        return c

    lax.fori_loop(0, 2 * COMB_TM, drain, 0)

    gate = gate_ref[...]
    lane = lax.broadcasted_iota(jnp.int32, gate.shape, 1)
    w0 = jnp.sum(jnp.where(lane == 0, gate, 0.0), axis=-1, keepdims=True)
    w1 = jnp.sum(jnp.where(lane == 1, gate, 0.0), axis=-1, keepdims=True)
    out_ref[...] = x1_ref[...] + (ybuf[0] * w0 + ybuf[1] * w1)


def _combine(x1, gates, ys, dest):
    t = x1.shape[0]
    nt = t // COMB_TM
    return pl.pallas_call(
        _combine_kernel,
        grid=(nt,),
        in_specs=[pl.BlockSpec((1, 1, 2 * COMB_TM), lambda i: (i, 0, 0), memory_space=pltpu.SMEM),
                  pl.BlockSpec((COMB_TM, D_MODEL), lambda i: (i, 0)),
                  pl.BlockSpec((COMB_TM, LANES), lambda i: (i, 0)),
                  pl.BlockSpec(memory_space=pl.ANY)],
        out_specs=pl.BlockSpec((COMB_TM, D_MODEL), lambda i: (i, 0)),
        out_shape=jax.ShapeDtypeStruct((t, D_MODEL), F32),
        scratch_shapes=[pltpu.VMEM((2, COMB_TM, D_MODEL), F32), pltpu.SemaphoreType.DMA],
        compiler_params=_params(("arbitrary",)),
        name="combine",
    )(dest.reshape(nt, 1, 2 * COMB_TM), x1, gates, ys)


def _layer(x, mem, attn_norm_g, mem_norm_g, w_in, w_gla_gk, b_gla_gk, gla_out_norm_g,
           moba_q_norm_g, moba_k_norm_g, w_mem_kv, mem_q_norm_g, mem_k_norm_g, w_out,
           ffn_norm_g, w_router_group, b_router_group, w_router_expert, b_router_expert,
           w_gate, w_up, w_down):
    b, s, d = x.shape
    t = b * s
    x2 = x.reshape(t, d)
    row = lambda v: v.reshape(1, -1).astype(F32)
    tile_heads = lambda v: jnp.tile(v.astype(F32), HEADS).reshape(1, QK_W)
    hid = np.arange(QK_W) // DH
    seg = jnp.asarray((hid[:, None] == hid[None, :]) / DH, BF16)

    w_in_p = jnp.concatenate([w_in[:, :1536], w_in[:, 1552:], w_in[:, 1536:1552]], axis=1).astype(BF16)
    wr = jnp.concatenate([w_router_group,
                          jnp.transpose(w_router_expert, (1, 0, 2)).reshape(d, N_EXPERTS),
                          jnp.zeros((d, LANES - N_GROUPS - N_EXPERTS), F32)], axis=1).astype(BF16)
    br = jnp.concatenate([b_router_group, b_router_expert.reshape(N_EXPERTS),
                          jnp.zeros((LANES - N_GROUPS - N_EXPERTS,), F32)]).reshape(1, LANES)

    memk, memvt = _mem_kv(mem, row(mem_norm_g), w_mem_kv.astype(BF16), tile_heads(mem_k_norm_g), seg)
    qk, gv, gr, glr, mq, mk, mvt, kmean, cq = _in_proj(
        x2, row(attn_norm_g), w_in_p, seg, tile_heads(moba_q_norm_g), tile_heads(moba_k_norm_g),
        tile_heads(mem_q_norm_g))
    o_gla = _gla(qk, gv, gr, glr, w_gla_gk.astype(BF16), row(b_gla_gk), row(gla_out_norm_g), b, s)
    o_moba, o_mem = _moba_mem(mq, mk, mvt, kmean, cq, memk, memvt, b, s)
    x1, hn, e_ids, gates = _out_router(
        x2, o_gla.reshape(t, GLA_V_W), o_moba.reshape(t, QK_W), o_mem.reshape(t, QK_W),
        w_out.astype(BF16), row(ffn_norm_g), wr, br)

    rank, counts = _rank(e_ids)
    counts = counts[0, :N_EXPERTS]
    padded = (counts + ROW_BLOCK - 1) // ROW_BLOCK * ROW_BLOCK
    pends = jnp.cumsum(padded)
    pstarts = pends - padded
    e2 = e_ids[:, :2]
    onehot = e2[:, :, None] == jnp.arange(N_EXPERTS, dtype=jnp.int32)
    dest = (jnp.sum(jnp.where(onehot, pstarts, 0), axis=-1) + rank[:, :2]).astype(jnp.int32).reshape(-1)
    n_rows = (t * 2 + N_EXPERTS * (ROW_BLOCK - 1) + ROW_BLOCK - 1) // ROW_BLOCK * ROW_BLOCK
    nb = n_rows // ROW_BLOCK
    block_start = jnp.arange(nb, dtype=jnp.int32) * ROW_BLOCK
    block_expert = jnp.minimum(jnp.sum(block_start[:, None] >= pends[None, :], axis=1),
                               N_EXPERTS - 1).astype(jnp.int32)
    n_used = (pends[-1] // ROW_BLOCK).astype(jnp.int32).reshape(1)

    xs = _dispatch(hn, dest, n_rows)
    ys = _experts(xs, block_expert, n_used, w_gate, w_up, w_down)
    out = _combine(x1, gates, ys, dest)
    return out.reshape(b, s, d)


def kernel(x, mem, attn_norm_g, mem_norm_g, w_in, w_gla_gk, b_gla_gk, gla_out_norm_g, moba_q_norm_g, moba_k_norm_g, w_mem_kv, mem_q_norm_g, mem_k_norm_g, w_out, ffn_norm_g, w_router_group, b_router_group, w_router_expert, b_router_expert, w_gate, w_up, w_down):
    depth = w_in.shape[0]
    for l in range(depth):
        x = _layer(x, mem, attn_norm_g[l], mem_norm_g[l], w_in[l], w_gla_gk[l], b_gla_gk[l],
                   gla_out_norm_g[l], moba_q_norm_g[l], moba_k_norm_g[l], w_mem_kv[l],
                   mem_q_norm_g[l], mem_k_norm_g[l], w_out[l], ffn_norm_g[l], w_router_group[l],
                   b_router_group[l], w_router_expert[l], b_router_expert[l],
                   w_gate[l], w_up[l], w_down[l])
    return x
```

```python
import functools

import jax
import jax.numpy as jnp
import numpy as np
from jax import lax
from jax.experimental import pallas as pl
from jax.experimental.pallas import tpu as pltpu

F32 = jnp.float32
BF16 = jnp.bfloat16
EPS = 1e-6
NEG_INF = float("-inf")

D_MODEL = 1024
N_MEM = 256
HEADS = 4
DH = 64
GLA_DV = 128
GLA_RANK = 16
GLA_GATE_NORMALIZER = 16.0
GLA_CHUNK = 64
MOBA_BLOCK = 256
MOBA_TOPK = 3
QK_W = HEADS * DH
GLA_V_W = HEADS * GLA_DV
N_GROUPS = 4
EXPERTS_PER_GROUP = 8
N_EXPERTS = N_GROUPS * EXPERTS_PER_GROUP
MOE_FF = 512
LANES = 128
ROW_BLOCK = 256

VMEM_LIMIT = 56 * 1024 * 1024


def _params(sem):
    return pltpu.CompilerParams(dimension_semantics=sem, vmem_limit_bytes=VMEM_LIMIT)


def _nt(a, b):
    return lax.dot_general(a, b, (((1,), (1,)), ((), ())), preferred_element_type=F32)


def _tn(a, b):
    return lax.dot_general(a, b, (((0,), (0,)), ((), ())), preferred_element_type=F32)


def _dot(a, b):
    return jnp.dot(a, b, preferred_element_type=F32)


def _rms_rows(x, g):
    ms = jnp.mean(x * x, axis=-1, keepdims=True)
    return x * lax.rsqrt(ms + EPS) * g


def _split3(x):
    h1 = x.astype(BF16)
    r1 = x - h1.astype(F32)
    h2 = r1.astype(BF16)
    h3 = (r1 - h2.astype(F32)).astype(BF16)
    return h1, h2, h3


def _head_mean_sq(x, seg):
    sq = x * x
    hi = sq.astype(BF16)
    lo = (sq - hi.astype(F32)).astype(BF16)
    return _dot(hi, seg) + _dot(lo, seg)


def _silu(x):
    return x * (1.0 / (1.0 + jnp.exp(-x)))


ROW_TILE = D_MODEL // LANES


def _store_row_tiles(ref, x):
    n = x.shape[0]
    for c in range(ROW_TILE):
        ref[pl.ds(c, n, stride=ROW_TILE), :] = x[:, c * LANES:(c + 1) * LANES]


def _load_row_tiles(ref, n):
    return jnp.concatenate([ref[pl.ds(c, n, stride=ROW_TILE), :] for c in range(ROW_TILE)], axis=1)


def _mem_kv_kernel(mem_ref, g_ref, w_ref, gk_ref, seg_ref, k_ref, vt_ref):
    h = _rms_rows(mem_ref[0], g_ref[...]).astype(BF16)
    kv = _dot(h, w_ref[...])
    k = kv[:, :QK_W]
    kn = k * lax.rsqrt(_head_mean_sq(k, seg_ref[...]) + EPS) * gk_ref[...]
    k_ref[0] = kn.astype(BF16)
    vt_ref[0] = kv[:, QK_W:].T.astype(BF16)


def _mem_kv(mem, g, w_bf, gk_t, seg):
    b = mem.shape[0]
    return pl.pallas_call(
        _mem_kv_kernel,
        grid=(b,),
        in_specs=[
            pl.BlockSpec((1, N_MEM, D_MODEL), lambda i: (i, 0, 0)),
            pl.BlockSpec((1, D_MODEL), lambda i: (0, 0)),
            pl.BlockSpec((D_MODEL, 2 * QK_W), lambda i: (0, 0)),
            pl.BlockSpec((1, QK_W), lambda i: (0, 0)),
            pl.BlockSpec((QK_W, QK_W), lambda i: (0, 0)),
        ],
        out_specs=[
            pl.BlockSpec((1, N_MEM, QK_W), lambda i: (i, 0, 0)),
            pl.BlockSpec((1, QK_W, N_MEM), lambda i: (i, 0, 0)),
        ],
        out_shape=[
            jax.ShapeDtypeStruct((b, N_MEM, QK_W), BF16),
            jax.ShapeDtypeStruct((b, QK_W, N_MEM), BF16),
        ],
        compiler_params=_params(("arbitrary",)),
        name="mem_kv",
    )(mem, g, w_bf, gk_t, seg)


IN_TM = 512
_C_QK, _C_V, _C_R, _C_MQ, _C_MK, _C_MV, _C_CQ, _C_LR = 0, 512, 1024, 1536, 1792, 2048, 2304, 2560
D_IN = 2576


def _in_proj_kernel(x_ref, g_ref, w_ref, seg_ref, gq_ref, gk_ref, gc_ref,
                    qk_ref, v_ref, r_ref, lr_ref, mq_ref, mk_ref, mvt_ref, kmean_ref, cq_ref):
    h = _rms_rows(x_ref[...], g_ref[...]).astype(BF16)
    p = _dot(h, w_ref[...])
    seg = seg_ref[...]
    qk_ref[...] = p[:, _C_QK:_C_V]
    v_ref[...] = p[:, _C_V:_C_R].astype(BF16)
    r_ref[...] = p[:, _C_R:_C_MQ]
    lr_ref[...] = p[:, _C_LR:D_IN]

    def head_norm(t, gain):
        return t * lax.rsqrt(_head_mean_sq(t, seg) + EPS) * gain

    scale = DH ** -0.5
    mq_ref[...] = (head_norm(p[:, _C_MQ:_C_MK], gq_ref[...]) * scale).astype(BF16)
    cq_ref[...] = (head_norm(p[:, _C_CQ:_C_LR], gc_ref[...]) * scale).astype(BF16)
    kn = head_norm(p[:, _C_MK:_C_MV], gk_ref[...])
    mk_ref[...] = kn.astype(BF16)
    mv = p[:, _C_MV:_C_CQ]
    for j in range(IN_TM // MOBA_BLOCK):
        rows = slice(j * MOBA_BLOCK, (j + 1) * MOBA_BLOCK)
        kmean_ref[0, j:j + 1, :] = jnp.mean(kn[rows], axis=0, keepdims=True)
        mvt_ref[j] = mv[rows].T.astype(BF16)


def _in_proj(x2, g, w_bf, seg, gq_t, gk_t, gc_t):
    t = x2.shape[0]
    nt = t // IN_TM
    nb = IN_TM // MOBA_BLOCK
    row = lambda w: pl.BlockSpec((IN_TM, w), lambda i: (i, 0))
    const = lambda a, b: pl.BlockSpec((a, b), lambda i: (0, 0))
    return pl.pallas_call(
        _in_proj_kernel,
        grid=(nt,),
        in_specs=[row(D_MODEL), const(1, D_MODEL), const(D_MODEL, D_IN), const(QK_W, QK_W),
                  const(1, QK_W), const(1, QK_W), const(1, QK_W)],
        out_specs=[row(2 * QK_W), row(GLA_V_W), row(GLA_V_W), row(GLA_RANK), row(QK_W), row(QK_W),
                   pl.BlockSpec((nb, QK_W, MOBA_BLOCK), lambda i: (i, 0, 0)),
                   pl.BlockSpec((1, nb, QK_W), lambda i: (i, 0, 0)),
                   row(QK_W)],
        out_shape=[
            jax.ShapeDtypeStruct((t, 2 * QK_W), F32),
            jax.ShapeDtypeStruct((t, GLA_V_W), BF16),
            jax.ShapeDtypeStruct((t, GLA_V_W), F32),
            jax.ShapeDtypeStruct((t, GLA_RANK), F32),
            jax.ShapeDtypeStruct((t, QK_W), BF16),
            jax.ShapeDtypeStruct((t, QK_W), BF16),
            jax.ShapeDtypeStruct((t // MOBA_BLOCK, QK_W, MOBA_BLOCK), BF16),
            jax.ShapeDtypeStruct((nt, nb, QK_W), F32),
            jax.ShapeDtypeStruct((t, QK_W), BF16),
        ],
        compiler_params=_params(("arbitrary",)),
        name="in_proj",
    )(x2, g, w_bf, seg, gq_t, gk_t, gc_t)


GLA_TC = 256


def _gla_kernel(qk_ref, v_ref, r_ref, lr_ref, wgk_ref, bgk_ref, gn_ref, tri_ref, ones_ref,
                o_ref, st_ref):
    @pl.when(pl.program_id(1) == 0)
    def _():
        st_ref[...] = jnp.zeros_like(st_ref)

    qk = qk_ref[0]
    q = qk[:, :QK_W]
    k = qk[:, QK_W:]
    gk = _dot(lr_ref[0].astype(BF16), wgk_ref[...]) + bgk_ref[...]
    g = -(jnp.maximum(-gk, 0.0) + jnp.log1p(jnp.exp(-jnp.abs(gk)))) / GLA_GATE_NORMALIZER
    g1, g2, g3 = _split3(g)
    tri = tri_ref[...]
    ones = ones_ref[...]
    cum = _dot(tri, g1) + _dot(tri, g2) + _dot(tri, g3)
    tot = _dot(ones, g1) + _dot(ones, g2) + _dot(ones, g3)
    q_dec = (q * (DH ** -0.5) * jnp.exp(cum)).astype(BF16)
    k_inv = (k * jnp.exp(-cum)).astype(BF16)
    k_end = (k * jnp.exp(tot - cum)).astype(BF16)
    decay = jnp.exp(tot)

    lane_head = lax.broadcasted_iota(jnp.int32, (GLA_CHUNK, QK_W), 1) // DH
    causal = (lax.broadcasted_iota(jnp.int32, (GLA_CHUNK, GLA_CHUNK), 0)
              >= lax.broadcasted_iota(jnp.int32, (GLA_CHUNK, GLA_CHUNK), 1))
    same_head = (lax.broadcasted_iota(jnp.int32, (GLA_V_W, QK_W), 0) // GLA_DV
                 == lax.broadcasted_iota(jnp.int32, (GLA_V_W, QK_W), 1) // DH)
    gain = gn_ref[...]

    for c in range(GLA_TC // GLA_CHUNK):
        rows = slice(c * GLA_CHUNK, (c + 1) * GLA_CHUNK)
        qd, ki, ke, vc = q_dec[rows], k_inv[rows], k_end[rows], v_ref[0, rows, :]
        st = st_ref[...]
        o_inter = _nt(qd, st.astype(BF16))
        outs = []
        for h in range(HEADS):
            a = _nt(jnp.where(lane_head == h, qd, jnp.zeros_like(qd)), ki)
            a = jnp.where(causal, a, 0.0).astype(BF16)
            vs = slice(h * GLA_DV, (h + 1) * GLA_DV)
            oh = _dot(a, vc[:, vs]) + o_inter[:, vs]
            y = _rms_rows(oh, gain)
            outs.append(y * _silu(r_ref[0, rows, vs]))
        o_ref[0, rows, :] = jnp.concatenate(outs, axis=-1).astype(BF16)
        d_st = _tn(vc, ke)
        st_ref[...] = st * decay[c * GLA_CHUNK:c * GLA_CHUNK + 1, :] + jnp.where(same_head, d_st, 0.0)


def _gla(qk, v, r, lr, wgk_bf, bgk, gn, b, s):
    ns = s // GLA_TC
    idx = np.arange(GLA_TC)
    same_chunk = (idx[:, None] // GLA_CHUNK) == (idx[None, :] // GLA_CHUNK)
    tri = jnp.asarray(same_chunk & (idx[:, None] >= idx[None, :]), BF16)
    ones = jnp.asarray(same_chunk, BF16)
    seq = lambda w: pl.BlockSpec((1, GLA_TC, w), lambda i, j: (i, j, 0))
    const = lambda a, c: pl.BlockSpec((a, c), lambda i, j: (0, 0))
    return pl.pallas_call(
        _gla_kernel,
        grid=(b, ns),
        in_specs=[seq(2 * QK_W), seq(GLA_V_W), seq(GLA_V_W), seq(GLA_RANK),
                  const(GLA_RANK, QK_W), const(1, QK_W), const(1, GLA_DV),
                  const(GLA_TC, GLA_TC), const(GLA_TC, GLA_TC)],
        out_specs=seq(GLA_V_W),
        out_shape=jax.ShapeDtypeStruct((b, s, GLA_V_W), BF16),
        scratch_shapes=[pltpu.VMEM((GLA_V_W, QK_W), F32)],
        compiler_params=_params(("arbitrary", "arbitrary")),
        name="gla",
    )(qk.reshape(b, s, 2 * QK_W), v.reshape(b, s, GLA_V_W), r.reshape(b, s, GLA_V_W),
      lr.reshape(b, s, GLA_RANK), wgk_bf, bgk, gn, tri, ones)


TQ = MOBA_BLOCK
KV_UNROLL = 8


def _moba_mem_kernel(q_ref, k_ref, vt_ref, kmean_ref, cq_ref, mk_ref, mvt_ref,
                     o_moba_ref, o_mem_ref,
                     qm_ref, negd_ref, selb_ref, *, n_blocks):
    i = pl.program_id(1)
    lane_head = lax.broadcasted_iota(jnp.int32, (TQ, QK_W), 1) // DH
    dist0 = (lax.broadcasted_iota(jnp.int32, (MOBA_BLOCK, TQ), 1)
             - lax.broadcasted_iota(jnp.int32, (MOBA_BLOCK, TQ), 0)).astype(F32)
    slopes = [2.0 ** (-8.0 * (h + 1) / HEADS) for h in range(HEADS)]

    q = q_ref[0]
    kmean = kmean_ref[0].astype(BF16)
    blk = lax.broadcasted_iota(jnp.int32, (n_blocks, TQ), 0)
    blk_f = blk.astype(F32)
    k_own = k_ref[0, pl.ds(pl.multiple_of(i * MOBA_BLOCK, MOBA_BLOCK), MOBA_BLOCK), :]
    ones_l = jnp.ones((16, MOBA_BLOCK), BF16)
    n_groups = (i + KV_UNROLL - 1) // KV_UNROLL

    outs = []
    for h in range(HEADS):
        hd = slice(h * DH, (h + 1) * DH)
        qm = jnp.where(lane_head == h, q, jnp.zeros_like(q))
        qm_ref[...] = qm
        negd_ref[...] = -slopes[h] * dist0
        gate = jnp.where(blk < i, _nt(kmean, qm), NEG_INF)
        chosen = jnp.zeros((n_blocks, TQ), jnp.bool_)
        for r in range(MOBA_TOPK):
            mx = jnp.max(gate, axis=0, keepdims=True)
            first = jnp.min(jnp.where(gate == mx, blk_f, float(n_blocks)), axis=0, keepdims=True)
            hit = blk_f == first
            chosen = chosen | (hit & (mx > NEG_INF))
            gate = jnp.where(hit, NEG_INF, gate)
        selb_ref[...] = jnp.where(chosen, 0.0, NEG_INF)
        s = _nt(k_own, qm)
        s = jnp.where(dist0 >= 0.0, s - slopes[h] * dist0, NEG_INF)
        m0 = jnp.max(s, axis=0, keepdims=True)
        p = jnp.exp(s - m0).astype(BF16)
        l0 = _dot(ones_l, p)[0:1]
        acc0 = _dot(vt_ref[i, hd, :], p)

        def past_blocks(g, carry, h=h, hd=hd):
            m, l, acc = carry
            for u in range(KV_UNROLL):
                j = g * KV_UNROLL + u
                kj = k_ref[0, pl.ds(pl.multiple_of(j * MOBA_BLOCK, MOBA_BLOCK), MOBA_BLOCK), :]
                s1 = _nt(kj, qm_ref[...]) + negd_ref[...]
                off = jnp.full((1, TQ), (i - j) * MOBA_BLOCK, jnp.int32).astype(F32)
                rb = selb_ref[pl.ds(j, 1), :] - slopes[h] * off
                m_new = jnp.maximum(m, jnp.max(s1, axis=0, keepdims=True) + rb)
                p = jnp.exp(s1 - (m_new - rb)).astype(BF16)
                alpha = jnp.exp(m - m_new)
                acc = alpha * acc + _dot(vt_ref[j, hd, :], p)
                l = alpha * l + _dot(ones_l, p)[0:1]
                m = m_new
            return m, l, acc

        m, l, acc = lax.fori_loop(0, n_groups, past_blocks, (m0, l0, acc0))
        outs.append(acc * (1.0 / l))

    o_moba_ref[0] = jnp.concatenate(outs, axis=0).T.astype(BF16)

    cq = cq_ref[0]
    mk = mk_ref[0]
    mvt = mvt_ref[0]
    outs = []
    for h in range(HEADS):
        s = _nt(mk, jnp.where(lane_head == h, cq, jnp.zeros_like(cq)))
        m = jnp.max(s, axis=0, keepdims=True)
        p = jnp.exp(s - m)
        l = jnp.sum(p, axis=0, keepdims=True)
        outs.append(_dot(mvt[h * DH:(h + 1) * DH, :], p.astype(BF16)) * (1.0 / l))
    o_mem_ref[0] = jnp.concatenate(outs, axis=0).T.astype(BF16)


def _moba_mem(mq, mk, mvt, kmean, cq, memk, memvt, b, s):
    nq = s // TQ
    n_blocks = s // MOBA_BLOCK
    qspec = pl.BlockSpec((1, TQ, QK_W), lambda i, j: (i, j, 0))
    return pl.pallas_call(
        functools.partial(_moba_mem_kernel, n_blocks=n_blocks),
        grid=(b, nq),
        in_specs=[
            qspec,
            pl.BlockSpec((1, s, QK_W), lambda i, j: (i, 0, 0)),
            pl.BlockSpec((n_blocks, QK_W, MOBA_BLOCK), lambda i, j: (i, 0, 0)),
            pl.BlockSpec((1, n_blocks, QK_W), lambda i, j: (i, 0, 0)),
            qspec,
            pl.BlockSpec((1, N_MEM, QK_W), lambda i, j: (i, 0, 0)),
            pl.BlockSpec((1, QK_W, N_MEM), lambda i, j: (i, 0, 0)),
        ],
        out_specs=[qspec, qspec],
        out_shape=[jax.ShapeDtypeStruct((b, s, QK_W), BF16), jax.ShapeDtypeStruct((b, s, QK_W), BF16)],
        scratch_shapes=[
            pltpu.VMEM((TQ, QK_W), BF16),
            pltpu.VMEM((MOBA_BLOCK, TQ), F32),
            pltpu.VMEM((n_blocks, TQ), F32),
        ],
        compiler_params=_params(("arbitrary", "arbitrary")),
        name="moba_mem",
    )(mq.reshape(b, s, QK_W), mk.reshape(b, s, QK_W), mvt, kmean.reshape(b, n_blocks, QK_W),
      cq.reshape(b, s, QK_W), memk, memvt)


OUT_TM = 512


def _out_router_kernel(x_ref, og_ref, om_ref, oc_ref, w_ref, g_ref, wr_ref, br_ref,
                       x1_ref, h_ref, e_ref, gate_ref):
    w = w_ref[...]
    x1 = (x_ref[...] + _dot(og_ref[...], w[:GLA_V_W]) + _dot(om_ref[...], w[GLA_V_W:GLA_V_W + QK_W])
          + _dot(oc_ref[...], w[GLA_V_W + QK_W:]))
    x1_ref[...] = x1
    hn = _rms_rows(x1, g_ref[...])
    _store_row_tiles(h_ref, hn)
    logits = _dot(hn.astype(BF16), wr_ref[...]) + br_ref[...]
    lane = lax.broadcasted_iota(jnp.int32, logits.shape, 1).astype(F32)
    lg = jnp.where(lane < N_GROUPS, logits, NEG_INF)
    mg = jnp.max(lg, axis=-1, keepdims=True)
    g_sel = jnp.min(jnp.where(lg == mg, lane, float(LANES)), axis=-1, keepdims=True)
    p_group = 1.0 / jnp.sum(jnp.exp(lg - mg), axis=-1, keepdims=True)
    lo = N_GROUPS + g_sel * EXPERTS_PER_GROUP
    le = jnp.where((lane >= lo) & (lane < lo + EXPERTS_PER_GROUP), logits, NEG_INF)
    m0 = jnp.max(le, axis=-1, keepdims=True)
    i0 = jnp.min(jnp.where(le == m0, lane, float(LANES)), axis=-1, keepdims=True)
    le1 = jnp.where(lane == i0, NEG_INF, le)
    m1 = jnp.max(le1, axis=-1, keepdims=True)
    i1 = jnp.min(jnp.where(le1 == m1, lane, float(LANES)), axis=-1, keepdims=True)
    z = jnp.exp(m1 - m0)
    w0 = p_group / (1.0 + z)
    w1 = p_group * z / (1.0 + z)
    ids = jnp.where(lane == 0.0, i0 - N_GROUPS, jnp.where(lane == 1.0, i1 - N_GROUPS, 0.0))
    e_ref[...] = ids.astype(jnp.int32)
    gate_ref[...] = jnp.where(lane == 0.0, w0, jnp.where(lane == 1.0, w1, 0.0))


def _out_router(x2, og, om, oc, w_bf, g, wr_bf, br):
    t = x2.shape[0]
    row = lambda w: pl.BlockSpec((OUT_TM, w), lambda i: (i, 0))
    const = lambda a, b: pl.BlockSpec((a, b), lambda i: (0, 0))
    return pl.pallas_call(
        _out_router_kernel,
        grid=(t // OUT_TM,),
        in_specs=[row(D_MODEL), row(GLA_V_W), row(QK_W), row(QK_W), const(D_MODEL, D_MODEL),
                  const(1, D_MODEL), const(D_MODEL, LANES), const(1, LANES)],
        out_specs=[row(D_MODEL), pl.BlockSpec((OUT_TM * ROW_TILE, LANES), lambda i: (i, 0)),
                   row(LANES), row(LANES)],
        out_shape=[
            jax.ShapeDtypeStruct((t, D_MODEL), F32),
            jax.ShapeDtypeStruct((t * ROW_TILE, LANES), F32),
            jax.ShapeDtypeStruct((t, LANES), jnp.int32),
            jax.ShapeDtypeStruct((t, LANES), F32),
        ],
        compiler_params=_params(("arbitrary",)),
        name="out_router",
    )(x2, og, om, oc, w_bf, g, wr_bf, br)


RANK_TM = 256


def _rank_kernel(e_ref, tri_ref, rank_ref, count_ref, carry_ref):
    @pl.when(pl.program_id(0) == 0)
    def _():
        carry_ref[...] = jnp.zeros_like(carry_ref)

    e = e_ref[...].astype(F32)
    lane = lax.broadcasted_iota(jnp.int32, e.shape, 1).astype(F32)
    e0 = jnp.sum(jnp.where(lane == 0.0, e, 0.0), axis=-1, keepdims=True)
    e1 = jnp.sum(jnp.where(lane == 1.0, e, 0.0), axis=-1, keepdims=True)
    oh0 = (lane == e0).astype(F32)
    oh1 = (lane == e1).astype(F32)
    both = oh0 + oh1
    before = _dot(tri_ref[...], both.astype(BF16)) + carry_ref[...]
    r0 = jnp.sum(oh0 * before, axis=-1, keepdims=True)
    r1 = jnp.sum(oh1 * before, axis=-1, keepdims=True)
    rank_ref[...] = jnp.where(lane == 0.0, r0, jnp.where(lane == 1.0, r1, 0.0)).astype(jnp.int32)
    carry_ref[...] = carry_ref[...] + jnp.sum(both, axis=0, keepdims=True)
    count_ref[...] = jnp.broadcast_to(carry_ref[...], count_ref.shape).astype(jnp.int32)


def _rank(e_ids):
    t = e_ids.shape[0]
    idx = np.arange(RANK_TM)
    tri = jnp.asarray(idx[:, None] > idx[None, :], BF16)
    return pl.pallas_call(
        _rank_kernel,
        grid=(t // RANK_TM,),
        in_specs=[pl.BlockSpec((RANK_TM, LANES), lambda i: (i, 0)),
                  pl.BlockSpec((RANK_TM, RANK_TM), lambda i: (0, 0))],
        out_specs=[pl.BlockSpec((RANK_TM, LANES), lambda i: (i, 0)),
                   pl.BlockSpec((8, LANES), lambda i: (0, 0))],
        out_shape=[jax.ShapeDtypeStruct((t, LANES), jnp.int32),
                   jax.ShapeDtypeStruct((8, LANES), jnp.int32)],
        scratch_shapes=[pltpu.VMEM((1, LANES), F32)],
        compiler_params=_params(("arbitrary",)),
        name="rank",
    )(e_ids, tri)


DISP_TM = 512
DMA_UNROLL = 8


def _dispatch_kernel(dest_ref, h_ref, xs_in_ref, xs_ref, sem):
    del xs_in_ref

    def row_copy(r, d):
        src = h_ref.at[pl.ds(pl.multiple_of(r * ROW_TILE, ROW_TILE), ROW_TILE), :]
        dst = xs_ref.at[pl.ds(pl.multiple_of(d * ROW_TILE, ROW_TILE), ROW_TILE), :]
        return pltpu.make_async_copy(src, dst, sem)

    def issue(r, c):
        row_copy(r, dest_ref[0, 0, 2 * r]).start(priority=0)
        row_copy(r, dest_ref[0, 0, 2 * r + 1]).start(priority=1)
        return c

    lax.fori_loop(0, DISP_TM, issue, 0, unroll=DMA_UNROLL)

    def drain(r, c):
        row_copy(0, 0).wait()
        return c

    lax.fori_loop(0, 2 * DISP_TM, drain, 0, unroll=DMA_UNROLL)


def _dispatch(hn, dest, n_rows):
    t = hn.shape[0] // ROW_TILE
    nt = t // DISP_TM
    xs0 = jnp.zeros((n_rows * ROW_TILE, LANES), F32)
    return pl.pallas_call(
        _dispatch_kernel,
        grid=(nt,),
        in_specs=[pl.BlockSpec((1, 1, 2 * DISP_TM), lambda i: (i, 0, 0), memory_space=pltpu.SMEM),
                  pl.BlockSpec((DISP_TM * ROW_TILE, LANES), lambda i: (i, 0)),
                  pl.BlockSpec(memory_space=pl.ANY)],
        out_specs=pl.BlockSpec(memory_space=pl.ANY),
        out_shape=jax.ShapeDtypeStruct((n_rows * ROW_TILE, LANES), F32),
        scratch_shapes=[pltpu.SemaphoreType.DMA],
        input_output_aliases={2: 0},
        compiler_params=_params(("arbitrary",)),
        name="dispatch",
    )(dest.reshape(nt, 1, 2 * DISP_TM), hn, xs0)


def _experts_kernel(be_ref, nused_ref, xs_ref, wg_ref, wu_ref, wd_ref, ys_ref, wg_bf, wu_bf, wd_bf):
    i = pl.program_id(0)
    new_expert = jnp.logical_or(i == 0, be_ref[i] != be_ref[jnp.maximum(i - 1, 0)])

    @pl.when(new_expert)
    def _():
        wg_bf[...] = wg_ref[0].astype(BF16)
        wu_bf[...] = wu_ref[0].astype(BF16)
        wd_bf[...] = wd_ref[0].astype(BF16)

    @pl.when(i < nused_ref[0])
    def _():
        xb = _load_row_tiles(xs_ref, ROW_BLOCK).astype(BF16)
        hid = _silu(_dot(xb, wg_bf[...])) * _dot(xb, wu_bf[...])
        _store_row_tiles(ys_ref, _dot(hid.astype(BF16), wd_bf[...]))

    @pl.when(i >= nused_ref[0])
    def _():
        ys_ref[...] = jnp.zeros_like(ys_ref)


def _experts(xs, block_expert, n_used, w_gate, w_up, w_down):
    n_rows = xs.shape[0] // ROW_TILE
    nb = n_rows // ROW_BLOCK
    grid_spec = pltpu.PrefetchScalarGridSpec(
        num_scalar_prefetch=2,
        grid=(nb,),
        in_specs=[
            pl.BlockSpec((ROW_BLOCK * ROW_TILE, LANES), lambda i, be, nu: (i, 0)),
            pl.BlockSpec((1, D_MODEL, MOE_FF), lambda i, be, nu: (be[i], 0, 0)),
            pl.BlockSpec((1, D_MODEL, MOE_FF), lambda i, be, nu: (be[i], 0, 0)),
            pl.BlockSpec((1, MOE_FF, D_MODEL), lambda i, be, nu: (be[i], 0, 0)),
        ],
        out_specs=pl.BlockSpec((ROW_BLOCK * ROW_TILE, LANES), lambda i, be, nu: (i, 0)),
        scratch_shapes=[pltpu.VMEM((D_MODEL, MOE_FF), BF16), pltpu.VMEM((D_MODEL, MOE_FF), BF16),
                        pltpu.VMEM((MOE_FF, D_MODEL), BF16)],
    )
    return pl.pallas_call(
        _experts_kernel,
        grid_spec=grid_spec,
        out_shape=jax.ShapeDtypeStruct((n_rows * ROW_TILE, LANES), F32),
        compiler_params=_params(("arbitrary",)),
        name="experts",
    )(block_expert, n_used, xs, w_gate, w_up, w_down)


COMB_TM = 256


def _combine_kernel(dest_ref, x1_ref, gate_ref, ys_ref, out_ref, ybuf, sem):
    def row_copy(d, k, r):
        src = ys_ref.at[pl.ds(pl.multiple_of(d * ROW_TILE, ROW_TILE), ROW_TILE), :]
        dst = ybuf.at[k, pl.ds(pl.multiple_of(r * ROW_TILE, ROW_TILE), ROW_TILE), :]
        return pltpu.make_async_copy(src, dst, sem)

    def issue(r, c):
        row_copy(dest_ref[0, 0, 2 * r], 0, r).start(priority=0)
        row_copy(dest_ref[0, 0, 2 * r + 1], 1, r).start(priority=1)
        return c

    lax.fori_loop(0, COMB_TM, issue, 0, unroll=DMA_UNROLL)

    def drain(r, c):
        row_copy(0, 0, 0).wait()
        return c

    lax.fori_loop(0, 2 * COMB_TM, drain, 0, unroll=DMA_UNROLL)

    gate = gate_ref[...]
    lane = lax.broadcasted_iota(jnp.int32, gate.shape, 1)
    w0 = jnp.sum(jnp.where(lane == 0, gate, 0.0), axis=-1, keepdims=True)
    w1 = jnp.sum(jnp.where(lane == 1, gate, 0.0), axis=-1, keepdims=True)
    y0 = _load_row_tiles(ybuf.at[0], COMB_TM)
    y1 = _load_row_tiles(ybuf.at[1], COMB_TM)
    out_ref[...] = x1_ref[...] + (y0 * w0 + y1 * w1)


def _combine(x1, gates, ys, dest):
    t = x1.shape[0]
    nt = t // COMB_TM
    return pl.pallas_call(
        _combine_kernel,
        grid=(nt,),
        in_specs=[pl.BlockSpec((1, 1, 2 * COMB_TM), lambda i: (i, 0, 0), memory_space=pltpu.SMEM),
                  pl.BlockSpec((COMB_TM, D_MODEL), lambda i: (i, 0)),
                  pl.BlockSpec((COMB_TM, LANES), lambda i: (i, 0)),
                  pl.BlockSpec(memory_space=pl.ANY)],
        out_specs=pl.BlockSpec((COMB_TM, D_MODEL), lambda i: (i, 0)),
        out_shape=jax.ShapeDtypeStruct((t, D_MODEL), F32),
        scratch_shapes=[pltpu.VMEM((2, COMB_TM * ROW_TILE, LANES), F32), pltpu.SemaphoreType.DMA],
        compiler_params=_params(("arbitrary",)),
        name="combine",
    )(dest.reshape(nt, 1, 2 * COMB_TM), x1, gates, ys)


def _layer(x, mem, attn_norm_g, mem_norm_g, w_in, w_gla_gk, b_gla_gk, gla_out_norm_g,
           moba_q_norm_g, moba_k_norm_g, w_mem_kv, mem_q_norm_g, mem_k_norm_g, w_out,
           ffn_norm_g, w_router_group, b_router_group, w_router_expert, b_router_expert,
           w_gate, w_up, w_down):
    b, s, d = x.shape
    t = b * s
    x2 = x.reshape(t, d)
    row = lambda v: v.reshape(1, -1).astype(F32)
    tile_heads = lambda v: jnp.tile(v.astype(F32), HEADS).reshape(1, QK_W)
    hid = np.arange(QK_W) // DH
    seg = jnp.asarray((hid[:, None] == hid[None, :]) / DH, BF16)

    w_in_p = jnp.concatenate([w_in[:, :1536], w_in[:, 1552:], w_in[:, 1536:1552]], axis=1).astype(BF16)
    wr = jnp.concatenate([w_router_group,
                          jnp.transpose(w_router_expert, (1, 0, 2)).reshape(d, N_EXPERTS),
                          jnp.zeros((d, LANES - N_GROUPS - N_EXPERTS), F32)], axis=1).astype(BF16)
    br = jnp.concatenate([b_router_group, b_router_expert.reshape(N_EXPERTS),
                          jnp.zeros((LANES - N_GROUPS - N_EXPERTS,), F32)]).reshape(1, LANES)

    memk, memvt = _mem_kv(mem, row(mem_norm_g), w_mem_kv.astype(BF16), tile_heads(mem_k_norm_g), seg)
    qk, gv, gr, glr, mq, mk, mvt, kmean, cq = _in_proj(
        x2, row(attn_norm_g), w_in_p, seg, tile_heads(moba_q_norm_g), tile_heads(moba_k_norm_g),
        tile_heads(mem_q_norm_g))
    o_gla = _gla(qk, gv, gr, glr, w_gla_gk.astype(BF16), row(b_gla_gk), row(gla_out_norm_g), b, s)
    o_moba, o_mem = _moba_mem(mq, mk, mvt, kmean, cq, memk, memvt, b, s)
    x1, hn, e_ids, gates = _out_router(
        x2, o_gla.reshape(t, GLA_V_W), o_moba.reshape(t, QK_W), o_mem.reshape(t, QK_W),
        w_out.astype(BF16), row(ffn_norm_g), wr, br)

    rank, counts = _rank(e_ids)
    counts = counts[0, :N_EXPERTS]
    padded = (counts + ROW_BLOCK - 1) // ROW_BLOCK * ROW_BLOCK
    pends = jnp.cumsum(padded)
    pstarts = pends - padded
    e2 = e_ids[:, :2]
    onehot = e2[:, :, None] == jnp.arange(N_EXPERTS, dtype=jnp.int32)
    dest = (jnp.sum(jnp.where(onehot, pstarts, 0), axis=-1) + rank[:, :2]).astype(jnp.int32).reshape(-1)
    n_rows = (t * 2 + N_EXPERTS * (ROW_BLOCK - 1) + ROW_BLOCK - 1) // ROW_BLOCK * ROW_BLOCK
    nb = n_rows // ROW_BLOCK
    block_start = jnp.arange(nb, dtype=jnp.int32) * ROW_BLOCK
    block_expert = jnp.minimum(jnp.sum(block_start[:, None] >= pends[None, :], axis=1),
                               N_EXPERTS - 1).astype(jnp.int32)
    n_used = (pends[-1] // ROW_BLOCK).astype(jnp.int32).reshape(1)

    xs = _dispatch(hn, dest, n_rows)
    ys = _experts(xs, block_expert, n_used, w_gate, w_up, w_down)
    out = _combine(x1, gates, ys, dest)
    return out.reshape(b, s, d)


def kernel(x, mem, attn_norm_g, mem_norm_g, w_in, w_gla_gk, b_gla_gk, gla_out_norm_g, moba_q_norm_g, moba_k_norm_g, w_mem_kv, mem_q_norm_g, mem_k_norm_g, w_out, ffn_norm_g, w_router_group, b_router_group, w_router_expert, b_router_expert, w_gate, w_up, w_down):
    depth = w_in.shape[0]
    for l in range(depth):
        x = _layer(x, mem, attn_norm_g[l], mem_norm_g[l], w_in[l], w_gla_gk[l], b_gla_gk[l],
                   gla_out_norm_g[l], moba_q_norm_g[l], moba_k_norm_g[l], w_mem_kv[l],
                   mem_q_norm_g[l], mem_k_norm_g[l], w_out[l], ffn_norm_g[l], w_router_group[l],
                   b_router_group[l], w_router_expert[l], b_router_expert[l],
                   w_gate[l], w_up[l], w_down[l])
    return x
```

```python
import functools

import jax
import jax.numpy as jnp
import numpy as np
from jax import lax
from jax.experimental import pallas as pl
from jax.experimental.pallas import tpu as pltpu

F32 = jnp.float32
BF16 = jnp.bfloat16
EPS = 1e-6
NEG_INF = float("-inf")

D_MODEL = 1024
N_MEM = 256
HEADS = 4
DH = 64
GLA_DV = 128
GLA_RANK = 16
GLA_GATE_NORMALIZER = 16.0
GLA_CHUNK = 64
MOBA_BLOCK = 256
MOBA_TOPK = 3
QK_W = HEADS * DH
GLA_V_W = HEADS * GLA_DV
N_GROUPS = 4
EXPERTS_PER_GROUP = 8
N_EXPERTS = N_GROUPS * EXPERTS_PER_GROUP
MOE_FF = 512
LANES = 128
ROW_BLOCK = 256

VMEM_LIMIT = 56 * 1024 * 1024


def _params(sem):
    return pltpu.CompilerParams(dimension_semantics=sem, vmem_limit_bytes=VMEM_LIMIT)


def _nt(a, b):
    return lax.dot_general(a, b, (((1,), (1,)), ((), ())), preferred_element_type=F32)


def _tn(a, b):
    return lax.dot_general(a, b, (((0,), (0,)), ((), ())), preferred_element_type=F32)


def _dot(a, b):
    return jnp.dot(a, b, preferred_element_type=F32)


def _rms_rows(x, g):
    ms = jnp.mean(x * x, axis=-1, keepdims=True)
    return x * lax.rsqrt(ms + EPS) * g


def _split3(x):
    h1 = x.astype(BF16)
    r1 = x - h1.astype(F32)
    h2 = r1.astype(BF16)
    h3 = (r1 - h2.astype(F32)).astype(BF16)
    return h1, h2, h3


def _head_mean_sq(x, seg):
    sq = x * x
    hi = sq.astype(BF16)
    lo = (sq - hi.astype(F32)).astype(BF16)
    return _dot(hi, seg) + _dot(lo, seg)


def _silu(x):
    return x * (1.0 / (1.0 + jnp.exp(-x)))


VT_ROWS = DH + 16
VT_ALL = HEADS * VT_ROWS


def _vt_with_ones(v):
    vt = v.T
    ones = jnp.ones((VT_ROWS - DH, v.shape[0]), F32)
    parts = []
    for h in range(HEADS):
        parts += [vt[h * DH:(h + 1) * DH], ones]
    return jnp.concatenate(parts, axis=0).astype(BF16)


ROW_TILE = D_MODEL // LANES


def _store_row_tiles(ref, x):
    n = x.shape[0]
    for c in range(ROW_TILE):
        ref[pl.ds(c, n, stride=ROW_TILE), :] = x[:, c * LANES:(c + 1) * LANES]


def _load_row_tiles(ref, n):
    return jnp.concatenate([ref[pl.ds(c, n, stride=ROW_TILE), :] for c in range(ROW_TILE)], axis=1)


def _mem_kv_kernel(mem_ref, g_ref, w_ref, gk_ref, seg_ref, k_ref, vt_ref):
    h = _rms_rows(mem_ref[0], g_ref[...]).astype(BF16)
    kv = _dot(h, w_ref[...])
    k = kv[:, :QK_W]
    kn = k * lax.rsqrt(_head_mean_sq(k, seg_ref[...]) + EPS) * gk_ref[...]
    k_ref[0] = kn.astype(BF16)
    vt_ref[0] = _vt_with_ones(kv[:, QK_W:])


def _mem_kv(mem, g, w_bf, gk_t, seg):
    b = mem.shape[0]
    return pl.pallas_call(
        _mem_kv_kernel,
        grid=(b,),
        in_specs=[
            pl.BlockSpec((1, N_MEM, D_MODEL), lambda i: (i, 0, 0)),
            pl.BlockSpec((1, D_MODEL), lambda i: (0, 0)),
            pl.BlockSpec((D_MODEL, 2 * QK_W), lambda i: (0, 0)),
            pl.BlockSpec((1, QK_W), lambda i: (0, 0)),
            pl.BlockSpec((QK_W, QK_W), lambda i: (0, 0)),
        ],
        out_specs=[
            pl.BlockSpec((1, N_MEM, QK_W), lambda i: (i, 0, 0)),
            pl.BlockSpec((1, VT_ALL, N_MEM), lambda i: (i, 0, 0)),
        ],
        out_shape=[
            jax.ShapeDtypeStruct((b, N_MEM, QK_W), BF16),
            jax.ShapeDtypeStruct((b, VT_ALL, N_MEM), BF16),
        ],
        compiler_params=_params(("arbitrary",)),
        name="mem_kv",
    )(mem, g, w_bf, gk_t, seg)


IN_TM = 512
_C_QK, _C_V, _C_R, _C_MQ, _C_MK, _C_MV, _C_CQ, _C_LR = 0, 512, 1024, 1536, 1792, 2048, 2304, 2560
D_IN = 2576


def _in_proj_kernel(x_ref, g_ref, w_ref, seg_ref, gq_ref, gk_ref, gc_ref,
                    qk_ref, v_ref, r_ref, lr_ref, mq_ref, mk_ref, mvt_ref, kmean_ref, cq_ref):
    h = _rms_rows(x_ref[...], g_ref[...]).astype(BF16)
    p = _dot(h, w_ref[...])
    seg = seg_ref[...]
    qk_ref[...] = p[:, _C_QK:_C_V]
    v_ref[...] = p[:, _C_V:_C_R].astype(BF16)
    r_ref[...] = p[:, _C_R:_C_MQ]
    lr_ref[...] = p[:, _C_LR:D_IN]

    def head_norm(t, gain):
        return t * lax.rsqrt(_head_mean_sq(t, seg) + EPS) * gain

    scale = DH ** -0.5
    mq_ref[...] = (head_norm(p[:, _C_MQ:_C_MK], gq_ref[...]) * scale).astype(BF16)
    cq_ref[...] = (head_norm(p[:, _C_CQ:_C_LR], gc_ref[...]) * scale).astype(BF16)
    kn = head_norm(p[:, _C_MK:_C_MV], gk_ref[...])
    mk_ref[...] = kn.astype(BF16)
    mv = p[:, _C_MV:_C_CQ]
    for j in range(IN_TM // MOBA_BLOCK):
        rows = slice(j * MOBA_BLOCK, (j + 1) * MOBA_BLOCK)
        kmean_ref[0, j:j + 1, :] = jnp.mean(kn[rows], axis=0, keepdims=True)
        mvt_ref[j] = _vt_with_ones(mv[rows])


def _in_proj(x2, g, w_bf, seg, gq_t, gk_t, gc_t):
    t = x2.shape[0]
    nt = t // IN_TM
    nb = IN_TM // MOBA_BLOCK
    row = lambda w: pl.BlockSpec((IN_TM, w), lambda i: (i, 0))
    const = lambda a, b: pl.BlockSpec((a, b), lambda i: (0, 0))
    return pl.pallas_call(
        _in_proj_kernel,
        grid=(nt,),
        in_specs=[row(D_MODEL), const(1, D_MODEL), const(D_MODEL, D_IN), const(QK_W, QK_W),
                  const(1, QK_W), const(1, QK_W), const(1, QK_W)],
        out_specs=[row(2 * QK_W), row(GLA_V_W), row(GLA_V_W), row(GLA_RANK), row(QK_W), row(QK_W),
                   pl.BlockSpec((nb, VT_ALL, MOBA_BLOCK), lambda i: (i, 0, 0)),
                   pl.BlockSpec((1, nb, QK_W), lambda i: (i, 0, 0)),
                   row(QK_W)],
        out_shape=[
            jax.ShapeDtypeStruct((t, 2 * QK_W), F32),
            jax.ShapeDtypeStruct((t, GLA_V_W), BF16),
            jax.ShapeDtypeStruct((t, GLA_V_W), F32),
            jax.ShapeDtypeStruct((t, GLA_RANK), F32),
            jax.ShapeDtypeStruct((t, QK_W), BF16),
            jax.ShapeDtypeStruct((t, QK_W), BF16),
            jax.ShapeDtypeStruct((t // MOBA_BLOCK, VT_ALL, MOBA_BLOCK), BF16),
            jax.ShapeDtypeStruct((nt, nb, QK_W), F32),
            jax.ShapeDtypeStruct((t, QK_W), BF16),
        ],
        compiler_params=_params(("arbitrary",)),
        name="in_proj",
    )(x2, g, w_bf, seg, gq_t, gk_t, gc_t)


GLA_TC = 256


def _gla_kernel(qk_ref, v_ref, r_ref, lr_ref, wgk_ref, bgk_ref, gn_ref, tri_ref, ones_ref,
                o_ref, st_ref):
    @pl.when(pl.program_id(1) == 0)
    def _():
        st_ref[...] = jnp.zeros_like(st_ref)

    qk = qk_ref[0]
    q = qk[:, :QK_W]
    k = qk[:, QK_W:]
    gk = _dot(lr_ref[0].astype(BF16), wgk_ref[...]) + bgk_ref[...]
    g = -(jnp.maximum(-gk, 0.0) + jnp.log1p(jnp.exp(-jnp.abs(gk)))) / GLA_GATE_NORMALIZER
    g1, g2, g3 = _split3(g)
    tri = tri_ref[...]
    ones = ones_ref[...]
    cum = _dot(tri, g1) + _dot(tri, g2) + _dot(tri, g3)
    tot = _dot(ones, g1) + _dot(ones, g2) + _dot(ones, g3)
    q_dec = (q * (DH ** -0.5) * jnp.exp(cum)).astype(BF16)
    k_inv = (k * jnp.exp(-cum)).astype(BF16)
    k_end = (k * jnp.exp(tot - cum)).astype(BF16)
    decay = jnp.exp(tot)

    lane_head = lax.broadcasted_iota(jnp.int32, (GLA_CHUNK, QK_W), 1) // DH
    causal = (lax.broadcasted_iota(jnp.int32, (GLA_CHUNK, GLA_CHUNK), 0)
              >= lax.broadcasted_iota(jnp.int32, (GLA_CHUNK, GLA_CHUNK), 1))
    same_head = (lax.broadcasted_iota(jnp.int32, (GLA_V_W, QK_W), 0) // GLA_DV
                 == lax.broadcasted_iota(jnp.int32, (GLA_V_W, QK_W), 1) // DH)
    gain = gn_ref[...]

    for c in range(GLA_TC // GLA_CHUNK):
        rows = slice(c * GLA_CHUNK, (c + 1) * GLA_CHUNK)
        qd, ki, ke, vc = q_dec[rows], k_inv[rows], k_end[rows], v_ref[0, rows, :]
        st = st_ref[...]
        o_inter = _nt(qd, st.astype(BF16))
        outs = []
        for h in range(HEADS):
            a = _nt(jnp.where(lane_head == h, qd, jnp.zeros_like(qd)), ki)
            a = jnp.where(causal, a, 0.0).astype(BF16)
            vs = slice(h * GLA_DV, (h + 1) * GLA_DV)
            oh = _dot(a, vc[:, vs]) + o_inter[:, vs]
            y = _rms_rows(oh, gain)
            outs.append(y * _silu(r_ref[0, rows, vs]))
        o_ref[0, rows, :] = jnp.concatenate(outs, axis=-1).astype(BF16)
        d_st = _tn(vc, ke)
        st_ref[...] = st * decay[c * GLA_CHUNK:c * GLA_CHUNK + 1, :] + jnp.where(same_head, d_st, 0.0)


def _gla(qk, v, r, lr, wgk_bf, bgk, gn, b, s):
    ns = s // GLA_TC
    idx = np.arange(GLA_TC)
    same_chunk = (idx[:, None] // GLA_CHUNK) == (idx[None, :] // GLA_CHUNK)
    tri = jnp.asarray(same_chunk & (idx[:, None] >= idx[None, :]), BF16)
    ones = jnp.asarray(same_chunk, BF16)
    seq = lambda w: pl.BlockSpec((1, GLA_TC, w), lambda i, j: (i, j, 0))
    const = lambda a, c: pl.BlockSpec((a, c), lambda i, j: (0, 0))
    return pl.pallas_call(
        _gla_kernel,
        grid=(b, ns),
        in_specs=[seq(2 * QK_W), seq(GLA_V_W), seq(GLA_V_W), seq(GLA_RANK),
                  const(GLA_RANK, QK_W), const(1, QK_W), const(1, GLA_DV),
                  const(GLA_TC, GLA_TC), const(GLA_TC, GLA_TC)],
        out_specs=seq(GLA_V_W),
        out_shape=jax.ShapeDtypeStruct((b, s, GLA_V_W), BF16),
        scratch_shapes=[pltpu.VMEM((GLA_V_W, QK_W), F32)],
        compiler_params=_params(("arbitrary", "arbitrary")),
        name="gla",
    )(qk.reshape(b, s, 2 * QK_W), v.reshape(b, s, GLA_V_W), r.reshape(b, s, GLA_V_W),
      lr.reshape(b, s, GLA_RANK), wgk_bf, bgk, gn, tri, ones)


TQ = MOBA_BLOCK
KV_UNROLL = 4


def _moba_mem_kernel(q_ref, k_ref, vt_ref, kmean_ref, cq_ref, mk_ref, mvt_ref,
                     o_moba_ref, o_mem_ref,
                     qcat_ref, negd_ref, selb_ref, s_ref, *, n_blocks):
    i = pl.program_id(1)
    lane_head = lax.broadcasted_iota(jnp.int32, (TQ, QK_W), 1) // DH
    dist0 = (lax.broadcasted_iota(jnp.int32, (MOBA_BLOCK, TQ), 1)
             - lax.broadcasted_iota(jnp.int32, (MOBA_BLOCK, TQ), 0)).astype(F32)
    slopes = [2.0 ** (-8.0 * (h + 1) / HEADS) for h in range(HEADS)]
    heads = [slice(h * TQ, (h + 1) * TQ) for h in range(HEADS)]
    vrows = [slice(h * VT_ROWS, (h + 1) * VT_ROWS) for h in range(HEADS)]

    def stack_heads(x):
        return jnp.concatenate([jnp.where(lane_head == h, x, jnp.zeros_like(x)) for h in range(HEADS)], axis=0)

    def finish(accl):
        return accl[:DH] * (1.0 / accl[DH:DH + 1])

    s_all = _nt(mk_ref[0], stack_heads(cq_ref[0]))
    outs = []
    for h in range(HEADS):
        s = s_all[:, heads[h]]
        p = jnp.exp(s - jnp.max(s, axis=0, keepdims=True)).astype(BF16)
        outs.append(finish(_dot(mvt_ref[0, vrows[h], :], p)))
    o_mem_ref[0] = jnp.concatenate(outs, axis=0).T.astype(BF16)

    qcat = stack_heads(q_ref[0])
    qcat_ref[...] = qcat
    k_own = k_ref[0, pl.ds(pl.multiple_of(i * MOBA_BLOCK, MOBA_BLOCK), MOBA_BLOCK), :]
    gate_all = _nt(kmean_ref[0].astype(BF16), qcat)
    s_all = _nt(k_own, qcat)
    blk = lax.broadcasted_iota(jnp.int32, (n_blocks, TQ), 0)
    blk_f = blk.astype(F32)

    init = []
    for h in range(HEADS):
        negd_ref[h] = -slopes[h] * dist0
        gate = jnp.where(blk < i, gate_all[:, heads[h]], NEG_INF)
        chosen = jnp.zeros((n_blocks, TQ), jnp.bool_)
        for r in range(MOBA_TOPK):
            mx = jnp.max(gate, axis=0, keepdims=True)
            first = jnp.min(jnp.where(gate == mx, blk_f, float(n_blocks)), axis=0, keepdims=True)
            hit = blk_f == first
            chosen = chosen | (hit & (mx > NEG_INF))
            gate = jnp.where(hit, NEG_INF, gate)
        selb_ref[h] = jnp.where(chosen, 0.0, NEG_INF)
        s = jnp.where(dist0 >= 0.0, s_all[:, heads[h]] - slopes[h] * dist0, NEG_INF)
        m0 = jnp.max(s, axis=0, keepdims=True)
        p = jnp.exp(s - m0).astype(BF16)
        init += [m0, _dot(vt_ref[i, vrows[h], :], p)]

    def stage_scores(j, slot):
        kj = k_ref[0, pl.ds(pl.multiple_of(j * MOBA_BLOCK, MOBA_BLOCK), MOBA_BLOCK), :]
        s_ref[slot] = _nt(kj, qcat_ref[...])

    stage_scores(0, 0)

    def past_blocks(g, carry):
        carry = list(carry)
        for u in range(KV_UNROLL):
            j = g * KV_UNROLL + u
            stage_scores(jnp.minimum(j + 1, n_blocks - 1), (u + 1) % 2)
            off = jnp.full((1, TQ), (i - j) * MOBA_BLOCK, jnp.int32).astype(F32)
            for h in range(HEADS):
                m, accl = carry[2 * h:2 * h + 2]
                s1 = s_ref[u % 2, :, heads[h]] + negd_ref[h]
                rb = selb_ref[h, pl.ds(j, 1), :] - slopes[h] * off
                m_new = jnp.maximum(m, jnp.max(s1, axis=0, keepdims=True) + rb)
                p = jnp.exp(s1 - (m_new - rb)).astype(BF16)
                carry[2 * h:2 * h + 2] = [m_new, jnp.exp(m - m_new) * accl + _dot(vt_ref[j, vrows[h], :], p)]
        return tuple(carry)

    n_groups = (i + KV_UNROLL - 1) // KV_UNROLL
    final = lax.fori_loop(0, n_groups, past_blocks, tuple(init))
    o_t = jnp.concatenate([finish(final[2 * h + 1]) for h in range(HEADS)], axis=0)
    o_moba_ref[0] = o_t.T.astype(BF16)


def _moba_mem(mq, mk, mvt, kmean, cq, memk, memvt, b, s):
    nq = s // TQ
    n_blocks = s // MOBA_BLOCK
    qspec = pl.BlockSpec((1, TQ, QK_W), lambda i, j: (i, j, 0))
    return pl.pallas_call(
        functools.partial(_moba_mem_kernel, n_blocks=n_blocks),
        grid=(b, nq),
        in_specs=[
            qspec,
            pl.BlockSpec((1, s, QK_W), lambda i, j: (i, 0, 0)),
            pl.BlockSpec((n_blocks, VT_ALL, MOBA_BLOCK), lambda i, j: (i, 0, 0)),
            pl.BlockSpec((1, n_blocks, QK_W), lambda i, j: (i, 0, 0)),
            qspec,
            pl.BlockSpec((1, N_MEM, QK_W), lambda i, j: (i, 0, 0)),
            pl.BlockSpec((1, VT_ALL, N_MEM), lambda i, j: (i, 0, 0)),
        ],
        out_specs=[qspec, qspec],
        out_shape=[jax.ShapeDtypeStruct((b, s, QK_W), BF16), jax.ShapeDtypeStruct((b, s, QK_W), BF16)],
        scratch_shapes=[
            pltpu.VMEM((HEADS * TQ, QK_W), BF16),
            pltpu.VMEM((HEADS, MOBA_BLOCK, TQ), F32),
            pltpu.VMEM((HEADS, n_blocks, TQ), F32),
            pltpu.VMEM((2, MOBA_BLOCK, HEADS * TQ), F32),
        ],
        compiler_params=_params(("arbitrary", "arbitrary")),
        name="moba_mem",
    )(mq.reshape(b, s, QK_W), mk.reshape(b, s, QK_W), mvt, kmean.reshape(b, n_blocks, QK_W),
      cq.reshape(b, s, QK_W), memk, memvt)


OUT_TM = 512


def _out_router_kernel(x_ref, og_ref, om_ref, oc_ref, w_ref, g_ref, wr_ref, br_ref, tri_ref,
                       x1_ref, h_ref, e_ref, gate_ref, count_ref, carry_ref):
    w = w_ref[...]
    x1 = (x_ref[...] + _dot(og_ref[...], w[:GLA_V_W]) + _dot(om_ref[...], w[GLA_V_W:GLA_V_W + QK_W])
          + _dot(oc_ref[...], w[GLA_V_W + QK_W:]))
    x1_ref[...] = x1
    hn = _rms_rows(x1, g_ref[...])
    _store_row_tiles(h_ref, hn)
    logits = _dot(hn.astype(BF16), wr_ref[...]) + br_ref[...]
    lane = lax.broadcasted_iota(jnp.int32, logits.shape, 1).astype(F32)
    lg = jnp.where(lane < N_GROUPS, logits, NEG_INF)
    mg = jnp.max(lg, axis=-1, keepdims=True)
    g_sel = jnp.min(jnp.where(lg == mg, lane, float(LANES)), axis=-1, keepdims=True)
    p_group = 1.0 / jnp.sum(jnp.exp(lg - mg), axis=-1, keepdims=True)
    lo = N_GROUPS + g_sel * EXPERTS_PER_GROUP
    le = jnp.where((lane >= lo) & (lane < lo + EXPERTS_PER_GROUP), logits, NEG_INF)
    m0 = jnp.max(le, axis=-1, keepdims=True)
    i0 = jnp.min(jnp.where(le == m0, lane, float(LANES)), axis=-1, keepdims=True)
    le1 = jnp.where(lane == i0, NEG_INF, le)
    m1 = jnp.max(le1, axis=-1, keepdims=True)
    i1 = jnp.min(jnp.where(le1 == m1, lane, float(LANES)), axis=-1, keepdims=True)
    z = jnp.exp(m1 - m0)
    w0 = p_group / (1.0 + z)
    w1 = p_group * z / (1.0 + z)
    gate_ref[...] = jnp.where(lane == 0.0, w0, jnp.where(lane == 1.0, w1, 0.0))

    @pl.when(pl.program_id(0) == 0)
    def _():
        carry_ref[...] = jnp.zeros_like(carry_ref)

    oh0 = (lane == i0).astype(F32)
    oh1 = (lane == i1).astype(F32)
    both = oh0 + oh1
    before = _dot(tri_ref[...], both.astype(BF16)) + carry_ref[...]
    r0 = jnp.sum(oh0 * before, axis=-1, keepdims=True)
    r1 = jnp.sum(oh1 * before, axis=-1, keepdims=True)
    carry_ref[...] = carry_ref[...] + jnp.sum(both, axis=0, keepdims=True)
    count_ref[...] = jnp.broadcast_to(carry_ref[...], count_ref.shape).astype(jnp.int32)
    info = jnp.where(lane == 0.0, i0 - N_GROUPS, jnp.where(lane == 1.0, i1 - N_GROUPS,
                     jnp.where(lane == 2.0, r0, jnp.where(lane == 3.0, r1, 0.0))))
    e_ref[...] = info.astype(jnp.int32)


def _out_router(x2, og, om, oc, w_bf, g, wr_bf, br):
    t = x2.shape[0]
    row = lambda w: pl.BlockSpec((OUT_TM, w), lambda i: (i, 0))
    const = lambda a, b: pl.BlockSpec((a, b), lambda i: (0, 0))
    idx = np.arange(OUT_TM)
    tri = jnp.asarray(idx[:, None] > idx[None, :], BF16)
    return pl.pallas_call(
        _out_router_kernel,
        grid=(t // OUT_TM,),
        in_specs=[row(D_MODEL), row(GLA_V_W), row(QK_W), row(QK_W), const(D_MODEL, D_MODEL),
                  const(1, D_MODEL), const(D_MODEL, LANES), const(1, LANES), const(OUT_TM, OUT_TM)],
        out_specs=[row(D_MODEL), pl.BlockSpec((OUT_TM * ROW_TILE, LANES), lambda i: (i, 0)),
                   row(LANES), row(LANES), const(8, LANES)],
        out_shape=[
            jax.ShapeDtypeStruct((t, D_MODEL), F32),
            jax.ShapeDtypeStruct((t * ROW_TILE, LANES), F32),
            jax.ShapeDtypeStruct((t, LANES), jnp.int32),
            jax.ShapeDtypeStruct((t, LANES), F32),
            jax.ShapeDtypeStruct((8, LANES), jnp.int32),
        ],
        scratch_shapes=[pltpu.VMEM((1, LANES), F32)],
        compiler_params=_params(("arbitrary",)),
        name="out_router",
    )(x2, og, om, oc, w_bf, g, wr_bf, br, tri)


DISP_TM = 512
DMA_UNROLL = 8


def _dispatch_kernel(dest_ref, h_ref, xs_in_ref, xs_ref, sem):
    del xs_in_ref

    def row_copy(r, d):
        src = h_ref.at[pl.ds(pl.multiple_of(r * ROW_TILE, ROW_TILE), ROW_TILE), :]
        dst = xs_ref.at[pl.ds(pl.multiple_of(d * ROW_TILE, ROW_TILE), ROW_TILE), :]
        return pltpu.make_async_copy(src, dst, sem)

    def issue(r, c):
        row_copy(r, dest_ref[0, 0, 2 * r]).start(priority=0)
        row_copy(r, dest_ref[0, 0, 2 * r + 1]).start(priority=1)
        return c

    lax.fori_loop(0, DISP_TM, issue, 0, unroll=DMA_UNROLL)

    def drain(r, c):
        row_copy(0, 0).wait()
        return c

    lax.fori_loop(0, 2 * DISP_TM, drain, 0, unroll=DMA_UNROLL)


def _dispatch(hn, dest, n_rows):
    t = hn.shape[0] // ROW_TILE
    nt = t // DISP_TM
    xs0 = jnp.zeros((n_rows * ROW_TILE, LANES), F32)
    return pl.pallas_call(
        _dispatch_kernel,
        grid=(nt,),
        in_specs=[pl.BlockSpec((1, 1, 2 * DISP_TM), lambda i: (i, 0, 0), memory_space=pltpu.SMEM),
                  pl.BlockSpec((DISP_TM * ROW_TILE, LANES), lambda i: (i, 0)),
                  pl.BlockSpec(memory_space=pl.ANY)],
        out_specs=pl.BlockSpec(memory_space=pl.ANY),
        out_shape=jax.ShapeDtypeStruct((n_rows * ROW_TILE, LANES), F32),
        scratch_shapes=[pltpu.SemaphoreType.DMA],
        input_output_aliases={2: 0},
        compiler_params=_params(("arbitrary",)),
        name="dispatch",
    )(dest.reshape(nt, 1, 2 * DISP_TM), hn, xs0)


def _experts_kernel(be_ref, nused_ref, xs_ref, wg_ref, wu_ref, wd_ref, ys_ref, wg_bf, wu_bf, wd_bf):
    i = pl.program_id(0)
    new_expert = jnp.logical_or(i == 0, be_ref[i] != be_ref[jnp.maximum(i - 1, 0)])

    @pl.when(new_expert)
    def _():
        wg_bf[...] = wg_ref[0].astype(BF16)
        wu_bf[...] = wu_ref[0].astype(BF16)
        wd_bf[...] = wd_ref[0].astype(BF16)

    @pl.when(i < nused_ref[0])
    def _():
        xb = _load_row_tiles(xs_ref, ROW_BLOCK).astype(BF16)
        hid = _silu(_dot(xb, wg_bf[...])) * _dot(xb, wu_bf[...])
        _store_row_tiles(ys_ref, _dot(hid.astype(BF16), wd_bf[...]))

    @pl.when(i >= nused_ref[0])
    def _():
        ys_ref[...] = jnp.zeros_like(ys_ref)


def _experts(xs, block_expert, n_used, w_gate, w_up, w_down):
    n_rows = xs.shape[0] // ROW_TILE
    nb = n_rows // ROW_BLOCK
    grid_spec = pltpu.PrefetchScalarGridSpec(
        num_scalar_prefetch=2,
        grid=(nb,),
        in_specs=[
            pl.BlockSpec((ROW_BLOCK * ROW_TILE, LANES), lambda i, be, nu: (i, 0)),
            pl.BlockSpec((1, D_MODEL, MOE_FF), lambda i, be, nu: (be[i], 0, 0)),
            pl.BlockSpec((1, D_MODEL, MOE_FF), lambda i, be, nu: (be[i], 0, 0)),
            pl.BlockSpec((1, MOE_FF, D_MODEL), lambda i, be, nu: (be[i], 0, 0)),
        ],
        out_specs=pl.BlockSpec((ROW_BLOCK * ROW_TILE, LANES), lambda i, be, nu: (i, 0)),
        scratch_shapes=[pltpu.VMEM((D_MODEL, MOE_FF), BF16), pltpu.VMEM((D_MODEL, MOE_FF), BF16),
                        pltpu.VMEM((MOE_FF, D_MODEL), BF16)],
    )
    return pl.pallas_call(
        _experts_kernel,
        grid_spec=grid_spec,
        out_shape=jax.ShapeDtypeStruct((n_rows * ROW_TILE, LANES), F32),
        compiler_params=_params(("arbitrary",)),
        name="experts",
    )(block_expert, n_used, xs, w_gate, w_up, w_down)


COMB_TM = 256


def _combine_kernel(dest_ref, dest_next_ref, x1_ref, gate_ref, ys_ref, out_ref, ybuf, sems):
    g = pl.program_id(0)

    def row_copy(d, slot, k, r):
        src = ys_ref.at[pl.ds(pl.multiple_of(d * ROW_TILE, ROW_TILE), ROW_TILE), :]
        dst = ybuf.at[slot, k, pl.ds(pl.multiple_of(r * ROW_TILE, ROW_TILE), ROW_TILE), :]
        return pltpu.make_async_copy(src, dst, sems.at[slot])

    def gather(dref, base, slot):
        def issue(r, c):
            row_copy(dref[0, 0, base + 2 * r], slot, 0, r).start(priority=0)
            row_copy(dref[0, 0, base + 2 * r + 1], slot, 1, r).start(priority=1)
            return c
        lax.fori_loop(0, COMB_TM, issue, 0, unroll=DMA_UNROLL)

    def finish(slot):
        def drain(r, c):
            row_copy(0, slot, 0, 0).wait()
            return c
        lax.fori_loop(0, 2 * COMB_TM, drain, 0, unroll=DMA_UNROLL)
        rows = slice(slot * COMB_TM, (slot + 1) * COMB_TM)
        gate = gate_ref[rows, :]
        lane = lax.broadcasted_iota(jnp.int32, gate.shape, 1)
        w0 = jnp.sum(jnp.where(lane == 0, gate, 0.0), axis=-1, keepdims=True)
        w1 = jnp.sum(jnp.where(lane == 1, gate, 0.0), axis=-1, keepdims=True)
        y0 = _load_row_tiles(ybuf.at[slot, 0], COMB_TM)
        y1 = _load_row_tiles(ybuf.at[slot, 1], COMB_TM)
        out_ref[rows, :] = x1_ref[rows, :] + (y0 * w0 + y1 * w1)

    @pl.when(g == 0)
    def _():
        gather(dest_ref, 0, 0)

    gather(dest_ref, 2 * COMB_TM, 1)
    finish(0)

    @pl.when(g + 1 < pl.num_programs(0))
    def _():
        gather(dest_next_ref, 0, 0)

    finish(1)


def _combine(x1, gates, ys, dest):
    t = x1.shape[0]
    ng = t // (2 * COMB_TM)
    dest3 = dest.reshape(ng, 1, 4 * COMB_TM)
    smem = lambda f: pl.BlockSpec((1, 1, 4 * COMB_TM), f, memory_space=pltpu.SMEM)
    return pl.pallas_call(
        _combine_kernel,
        grid=(ng,),
        in_specs=[smem(lambda i: (i, 0, 0)),
                  smem(lambda i: (jnp.minimum(i + 1, ng - 1), 0, 0)),
                  pl.BlockSpec((2 * COMB_TM, D_MODEL), lambda i: (i, 0)),
                  pl.BlockSpec((2 * COMB_TM, LANES), lambda i: (i, 0)),
                  pl.BlockSpec(memory_space=pl.ANY)],
        out_specs=pl.BlockSpec((2 * COMB_TM, D_MODEL), lambda i: (i, 0)),
        out_shape=jax.ShapeDtypeStruct((t, D_MODEL), F32),
        scratch_shapes=[pltpu.VMEM((2, 2, COMB_TM * ROW_TILE, LANES), F32),
                        pltpu.SemaphoreType.DMA((2,))],
        compiler_params=_params(("arbitrary",)),
        name="combine",
    )(dest3, dest3, x1, gates, ys)


def _layer(x, mem, attn_norm_g, mem_norm_g, w_in, w_gla_gk, b_gla_gk, gla_out_norm_g,
           moba_q_norm_g, moba_k_norm_g, w_mem_kv, mem_q_norm_g, mem_k_norm_g, w_out,
           ffn_norm_g, w_router_group, b_router_group, w_router_expert, b_router_expert,
           w_gate, w_up, w_down):
    b, s, d = x.shape
    t = b * s
    x2 = x.reshape(t, d)
    row = lambda v: v.reshape(1, -1).astype(F32)
    tile_heads = lambda v: jnp.tile(v.astype(F32), HEADS).reshape(1, QK_W)
    hid = np.arange(QK_W) // DH
    seg = jnp.asarray((hid[:, None] == hid[None, :]) / DH, BF16)

    w_in_p = jnp.concatenate([w_in[:, :1536], w_in[:, 1552:], w_in[:, 1536:1552]], axis=1).astype(BF16)
    wr = jnp.concatenate([w_router_group,
                          jnp.transpose(w_router_expert, (1, 0, 2)).reshape(d, N_EXPERTS),
                          jnp.zeros((d, LANES - N_GROUPS - N_EXPERTS), F32)], axis=1).astype(BF16)
    br = jnp.concatenate([b_router_group, b_router_expert.reshape(N_EXPERTS),
                          jnp.zeros((LANES - N_GROUPS - N_EXPERTS,), F32)]).reshape(1, LANES)

    memk, memvt = _mem_kv(mem, row(mem_norm_g), w_mem_kv.astype(BF16), tile_heads(mem_k_norm_g), seg)
    qk, gv, gr, glr, mq, mk, mvt, kmean, cq = _in_proj(
        x2, row(attn_norm_g), w_in_p, seg, tile_heads(moba_q_norm_g), tile_heads(moba_k_norm_g),
        tile_heads(mem_q_norm_g))
    o_gla = _gla(qk, gv, gr, glr, w_gla_gk.astype(BF16), row(b_gla_gk), row(gla_out_norm_g), b, s)
    o_moba, o_mem = _moba_mem(mq, mk, mvt, kmean, cq, memk, memvt, b, s)
    x1, hn, e_ids, gates, counts = _out_router(
        x2, o_gla.reshape(t, GLA_V_W), o_moba.reshape(t, QK_W), o_mem.reshape(t, QK_W),
        w_out.astype(BF16), row(ffn_norm_g), wr, br)

    counts = counts[0, N_GROUPS:N_GROUPS + N_EXPERTS]
    padded = (counts + ROW_BLOCK - 1) // ROW_BLOCK * ROW_BLOCK
    pends = jnp.cumsum(padded)
    pstarts = pends - padded
    onehot = e_ids[:, :2, None] == jnp.arange(N_EXPERTS, dtype=jnp.int32)
    dest = (jnp.sum(jnp.where(onehot, pstarts, 0), axis=-1) + e_ids[:, 2:4]).astype(jnp.int32).reshape(-1)
    n_rows = (t * 2 + N_EXPERTS * (ROW_BLOCK - 1) + ROW_BLOCK - 1) // ROW_BLOCK * ROW_BLOCK
    nb = n_rows // ROW_BLOCK
    block_start = jnp.arange(nb, dtype=jnp.int32) * ROW_BLOCK
    block_expert = jnp.minimum(jnp.sum(block_start[:, None] >= pends[None, :], axis=1),
                               N_EXPERTS - 1).astype(jnp.int32)
    n_used = (pends[-1] // ROW_BLOCK).astype(jnp.int32).reshape(1)

    xs = _dispatch(hn, dest, n_rows)
    ys = _experts(xs, block_expert, n_used, w_gate, w_up, w_down)
    out = _combine(x1, gates, ys, dest)
    return out.reshape(b, s, d)


def kernel(x, mem, attn_norm_g, mem_norm_g, w_in, w_gla_gk, b_gla_gk, gla_out_norm_g, moba_q_norm_g, moba_k_norm_g, w_mem_kv, mem_q_norm_g, mem_k_norm_g, w_out, ffn_norm_g, w_router_group, b_router_group, w_router_expert, b_router_expert, w_gate, w_up, w_down):
    depth = w_in.shape[0]
    for l in range(depth):
        x = _layer(x, mem, attn_norm_g[l], mem_norm_g[l], w_in[l], w_gla_gk[l], b_gla_gk[l],
                   gla_out_norm_g[l], moba_q_norm_g[l], moba_k_norm_g[l], w_mem_kv[l],
                   mem_q_norm_g[l], mem_k_norm_g[l], w_out[l], ffn_norm_g[l], w_router_group[l],
                   b_router_group[l], w_router_expert[l], b_router_expert[l],
                   w_gate[l], w_up[l], w_down[l])
    return x
```

```python
import functools

import jax
import jax.numpy as jnp
import numpy as np
from jax import lax
from jax.experimental import pallas as pl
from jax.experimental.pallas import tpu as pltpu

F32 = jnp.float32
BF16 = jnp.bfloat16
EPS = 1e-6
NEG_INF = float("-inf")

D_MODEL = 1024
N_MEM = 256
HEADS = 4
DH = 64
GLA_DV = 128
GLA_RANK = 16
GLA_GATE_NORMALIZER = 16.0
GLA_CHUNK = 64
MOBA_BLOCK = 256
MOBA_TOPK = 3
QK_W = HEADS * DH
GLA_V_W = HEADS * GLA_DV
N_GROUPS = 4
EXPERTS_PER_GROUP = 8
N_EXPERTS = N_GROUPS * EXPERTS_PER_GROUP
MOE_FF = 512
LANES = 128
ROW_BLOCK = 256

VMEM_LIMIT = 56 * 1024 * 1024


def _params(sem):
    return pltpu.CompilerParams(dimension_semantics=sem, vmem_limit_bytes=VMEM_LIMIT)


def _nt(a, b):
    return lax.dot_general(a, b, (((1,), (1,)), ((), ())), preferred_element_type=F32)


def _tn(a, b):
    return lax.dot_general(a, b, (((0,), (0,)), ((), ())), preferred_element_type=F32)


def _dot(a, b):
    return jnp.dot(a, b, preferred_element_type=F32)


def _rms_rows(x, g):
    ms = jnp.mean(x * x, axis=-1, keepdims=True)
    return x * lax.rsqrt(ms + EPS) * g


def _split3(x):
    h1 = x.astype(BF16)
    r1 = x - h1.astype(F32)
    h2 = r1.astype(BF16)
    h3 = (r1 - h2.astype(F32)).astype(BF16)
    return h1, h2, h3


def _head_mean_sq(x, seg):
    sq = x * x
    hi = sq.astype(BF16)
    lo = (sq - hi.astype(F32)).astype(BF16)
    return _dot(hi, seg) + _dot(lo, seg)


def _silu(x):
    return x * (1.0 / (1.0 + jnp.exp(-x)))


VT_ROWS = DH + 16
VT_ALL = HEADS * VT_ROWS


ALIBI_SLOPES = tuple(2.0 ** (-8.0 * (h + 1) / HEADS) for h in range(HEADS))


def _vt_with_ones(v, key_slopes=None):
    n = v.shape[0]
    vt = v.T
    ones = jnp.ones((VT_ROWS - DH, n), F32)
    pos = lax.broadcasted_iota(jnp.int32, (1, n), 1).astype(F32)
    parts = []
    for h in range(HEADS):
        scale = 1.0 if key_slopes is None else jnp.exp(key_slopes[h] * pos)
        parts += [vt[h * DH:(h + 1) * DH] * scale, ones * scale]
    return jnp.concatenate(parts, axis=0).astype(BF16)


ROW_TILE = D_MODEL // LANES


def _store_row_tiles(ref, x):
    n = x.shape[0]
    for c in range(ROW_TILE):
        ref[pl.ds(c, n, stride=ROW_TILE), :] = x[:, c * LANES:(c + 1) * LANES]


def _load_row_tiles(ref, n):
    return jnp.concatenate([ref[pl.ds(c, n, stride=ROW_TILE), :] for c in range(ROW_TILE)], axis=1)


def _mem_kv_kernel(mem_ref, g_ref, w_ref, gk_ref, seg_ref, k_ref, vt_ref):
    h = _rms_rows(mem_ref[0], g_ref[...]).astype(BF16)
    kv = _dot(h, w_ref[...])
    k = kv[:, :QK_W]
    kn = k * lax.rsqrt(_head_mean_sq(k, seg_ref[...]) + EPS) * gk_ref[...]
    k_ref[0] = kn.astype(BF16)
    vt_ref[0] = _vt_with_ones(kv[:, QK_W:])


def _mem_kv(mem, g, w_bf, gk_t, seg):
    b = mem.shape[0]
    return pl.pallas_call(
        _mem_kv_kernel,
        grid=(b,),
        in_specs=[
            pl.BlockSpec((1, N_MEM, D_MODEL), lambda i: (i, 0, 0)),
            pl.BlockSpec((1, D_MODEL), lambda i: (0, 0)),
            pl.BlockSpec((D_MODEL, 2 * QK_W), lambda i: (0, 0)),
            pl.BlockSpec((1, QK_W), lambda i: (0, 0)),
            pl.BlockSpec((QK_W, QK_W), lambda i: (0, 0)),
        ],
        out_specs=[
            pl.BlockSpec((1, N_MEM, QK_W), lambda i: (i, 0, 0)),
            pl.BlockSpec((1, VT_ALL, N_MEM), lambda i: (i, 0, 0)),
        ],
        out_shape=[
            jax.ShapeDtypeStruct((b, N_MEM, QK_W), BF16),
            jax.ShapeDtypeStruct((b, VT_ALL, N_MEM), BF16),
        ],
        compiler_params=_params(("arbitrary",)),
        name="mem_kv",
    )(mem, g, w_bf, gk_t, seg)


IN_TM = 512
_C_QK, _C_V, _C_R, _C_MQ, _C_MK, _C_MV, _C_CQ, _C_LR = 0, 512, 1024, 1536, 1792, 2048, 2304, 2560
D_IN = 2576


def _in_proj_kernel(x_ref, g_ref, w_ref, seg_ref, gq_ref, gk_ref, gc_ref,
                    qk_ref, v_ref, r_ref, lr_ref, mq_ref, mk_ref, mvt_ref, kmean_ref, cq_ref):
    h = _rms_rows(x_ref[...], g_ref[...]).astype(BF16)
    p = _dot(h, w_ref[...])
    seg = seg_ref[...]
    qk_ref[...] = p[:, _C_QK:_C_V]
    v_ref[...] = p[:, _C_V:_C_R].astype(BF16)
    r_ref[...] = p[:, _C_R:_C_MQ]
    lr_ref[...] = p[:, _C_LR:D_IN]

    def head_norm(t, gain):
        return t * lax.rsqrt(_head_mean_sq(t, seg) + EPS) * gain

    scale = DH ** -0.5
    mq_ref[...] = (head_norm(p[:, _C_MQ:_C_MK], gq_ref[...]) * scale).astype(BF16)
    cq_ref[...] = (head_norm(p[:, _C_CQ:_C_LR], gc_ref[...]) * scale).astype(BF16)
    kn = head_norm(p[:, _C_MK:_C_MV], gk_ref[...])
    mk_ref[...] = kn.astype(BF16)
    mv = p[:, _C_MV:_C_CQ]
    for j in range(IN_TM // MOBA_BLOCK):
        rows = slice(j * MOBA_BLOCK, (j + 1) * MOBA_BLOCK)
        kmean_ref[0, j:j + 1, :] = jnp.mean(kn[rows], axis=0, keepdims=True)
        mvt_ref[j] = _vt_with_ones(mv[rows], ALIBI_SLOPES)


def _in_proj(x2, g, w_bf, seg, gq_t, gk_t, gc_t):
    t = x2.shape[0]
    nt = t // IN_TM
    nb = IN_TM // MOBA_BLOCK
    row = lambda w: pl.BlockSpec((IN_TM, w), lambda i: (i, 0))
    const = lambda a, b: pl.BlockSpec((a, b), lambda i: (0, 0))
    return pl.pallas_call(
        _in_proj_kernel,
        grid=(nt,),
        in_specs=[row(D_MODEL), const(1, D_MODEL), const(D_MODEL, D_IN), const(QK_W, QK_W),
                  const(1, QK_W), const(1, QK_W), const(1, QK_W)],
        out_specs=[row(2 * QK_W), row(GLA_V_W), row(GLA_V_W), row(GLA_RANK), row(QK_W), row(QK_W),
                   pl.BlockSpec((nb, VT_ALL, MOBA_BLOCK), lambda i: (i, 0, 0)),
                   pl.BlockSpec((1, nb, QK_W), lambda i: (i, 0, 0)),
                   row(QK_W)],
        out_shape=[
            jax.ShapeDtypeStruct((t, 2 * QK_W), F32),
            jax.ShapeDtypeStruct((t, GLA_V_W), BF16),
            jax.ShapeDtypeStruct((t, GLA_V_W), F32),
            jax.ShapeDtypeStruct((t, GLA_RANK), F32),
            jax.ShapeDtypeStruct((t, QK_W), BF16),
            jax.ShapeDtypeStruct((t, QK_W), BF16),
            jax.ShapeDtypeStruct((t // MOBA_BLOCK, VT_ALL, MOBA_BLOCK), BF16),
            jax.ShapeDtypeStruct((nt, nb, QK_W), F32),
            jax.ShapeDtypeStruct((t, QK_W), BF16),
        ],
        compiler_params=_params(("arbitrary",)),
        name="in_proj",
    )(x2, g, w_bf, seg, gq_t, gk_t, gc_t)


GLA_TC = 256


def _gla_kernel(qk_ref, v_ref, r_ref, lr_ref, wgk_ref, bgk_ref, gn_ref, tri_ref, ones_ref,
                o_ref, st_ref):
    @pl.when(pl.program_id(1) == 0)
    def _():
        st_ref[...] = jnp.zeros_like(st_ref)

    qk = qk_ref[0]
    q = qk[:, :QK_W]
    k = qk[:, QK_W:]
    gk = _dot(lr_ref[0].astype(BF16), wgk_ref[...]) + bgk_ref[...]
    g = -(jnp.maximum(-gk, 0.0) + jnp.log1p(jnp.exp(-jnp.abs(gk)))) / GLA_GATE_NORMALIZER
    g1, g2, g3 = _split3(g)
    tri = tri_ref[...]
    ones = ones_ref[...]
    cum = _dot(tri, g1) + _dot(tri, g2) + _dot(tri, g3)
    tot = _dot(ones, g1) + _dot(ones, g2) + _dot(ones, g3)
    q_dec = (q * (DH ** -0.5) * jnp.exp(cum)).astype(BF16)
    k_inv = (k * jnp.exp(-cum)).astype(BF16)
    k_end = (k * jnp.exp(tot - cum)).astype(BF16)
    decay = jnp.exp(tot)

    lane_head = lax.broadcasted_iota(jnp.int32, (GLA_TC, QK_W), 1) // DH
    row_t = lax.broadcasted_iota(jnp.int32, (GLA_TC, GLA_TC), 0)
    col_t = lax.broadcasted_iota(jnp.int32, (GLA_TC, GLA_TC), 1)
    causal = (row_t >= col_t) & (row_t // GLA_CHUNK == col_t // GLA_CHUNK)
    same_head = (lax.broadcasted_iota(jnp.int32, (GLA_V_W, QK_W), 0) // GLA_DV
                 == lax.broadcasted_iota(jnp.int32, (GLA_V_W, QK_W), 1) // DH)
    gain = gn_ref[...]

    v = v_ref[0]
    chunks = [slice(c * GLA_CHUNK, (c + 1) * GLA_CHUNK) for c in range(GLA_TC // GLA_CHUNK)]
    d_st = [jnp.where(same_head, _tn(v[rows], k_end[rows]), 0.0) for rows in chunks]
    states = [st_ref[...]]
    for c, rows in enumerate(chunks):
        states.append(states[c] * decay[rows.start:rows.start + 1, :] + d_st[c])
    st_ref[...] = states[-1]
    o_inter = jnp.concatenate([_nt(q_dec[rows], states[c].astype(BF16)) for c, rows in enumerate(chunks)],
                              axis=0)
    q_stack = jnp.concatenate([jnp.where(lane_head == h, q_dec, jnp.zeros_like(q_dec)) for h in range(HEADS)],
                              axis=0)
    a_all = _nt(q_stack, k_inv)
    outs = []
    for h in range(HEADS):
        a = jnp.where(causal, a_all[h * GLA_TC:(h + 1) * GLA_TC], 0.0).astype(BF16)
        vs = slice(h * GLA_DV, (h + 1) * GLA_DV)
        oh = _dot(a, v[:, vs]) + o_inter[:, vs]
        outs.append(_rms_rows(oh, gain) * _silu(r_ref[0, :, vs]))
    o_ref[0] = jnp.concatenate(outs, axis=-1).astype(BF16)


def _gla(qk, v, r, lr, wgk_bf, bgk, gn, b, s):
    ns = s // GLA_TC
    idx = np.arange(GLA_TC)
    same_chunk = (idx[:, None] // GLA_CHUNK) == (idx[None, :] // GLA_CHUNK)
    tri = jnp.asarray(same_chunk & (idx[:, None] >= idx[None, :]), BF16)
    ones = jnp.asarray(same_chunk, BF16)
    seq = lambda w: pl.BlockSpec((1, GLA_TC, w), lambda i, j: (i, j, 0))
    const = lambda a, c: pl.BlockSpec((a, c), lambda i, j: (0, 0))
    return pl.pallas_call(
        _gla_kernel,
        grid=(b, ns),
        in_specs=[seq(2 * QK_W), seq(GLA_V_W), seq(GLA_V_W), seq(GLA_RANK),
                  const(GLA_RANK, QK_W), const(1, QK_W), const(1, GLA_DV),
                  const(GLA_TC, GLA_TC), const(GLA_TC, GLA_TC)],
        out_specs=seq(GLA_V_W),
        out_shape=jax.ShapeDtypeStruct((b, s, GLA_V_W), BF16),
        scratch_shapes=[pltpu.VMEM((GLA_V_W, QK_W), F32)],
        compiler_params=_params(("arbitrary", "arbitrary")),
        name="gla",
    )(qk.reshape(b, s, 2 * QK_W), v.reshape(b, s, GLA_V_W), r.reshape(b, s, GLA_V_W),
      lr.reshape(b, s, GLA_RANK), wgk_bf, bgk, gn, tri, ones)


TQ = MOBA_BLOCK
KV_UNROLL = 4


def _moba_mem_kernel(q_ref, k_ref, vt_ref, kmean_ref, cq_ref, mk_ref, mvt_ref,
                     o_moba_ref, o_mem_ref,
                     qcat_ref, selb_ref, s_ref, *, n_blocks):
    i = pl.program_id(1)
    lane_head = lax.broadcasted_iota(jnp.int32, (TQ, QK_W), 1) // DH
    dist0 = (lax.broadcasted_iota(jnp.int32, (MOBA_BLOCK, TQ), 1)
             - lax.broadcasted_iota(jnp.int32, (MOBA_BLOCK, TQ), 0)).astype(F32)
    slopes = ALIBI_SLOPES
    heads = [slice(h * TQ, (h + 1) * TQ) for h in range(HEADS)]
    vrows = [slice(h * VT_ROWS, (h + 1) * VT_ROWS) for h in range(HEADS)]

    def stack_heads(x):
        return jnp.concatenate([jnp.where(lane_head == h, x, jnp.zeros_like(x)) for h in range(HEADS)], axis=0)

    def finish(accl):
        return accl[:DH] * (1.0 / accl[DH:DH + 1])

    s_all = _nt(mk_ref[0], stack_heads(cq_ref[0]))
    outs = []
    for h in range(HEADS):
        s = s_all[:, heads[h]]
        p = jnp.exp(s - jnp.max(s, axis=0, keepdims=True)).astype(BF16)
        outs.append(finish(_dot(mvt_ref[0, vrows[h], :], p)))
    o_mem_ref[0] = jnp.concatenate(outs, axis=0).T.astype(BF16)

    qcat = stack_heads(q_ref[0])
    qcat_ref[...] = qcat
    k_own = k_ref[0, pl.ds(pl.multiple_of(i * MOBA_BLOCK, MOBA_BLOCK), MOBA_BLOCK), :]
    gate_all = _nt(kmean_ref[0].astype(BF16), qcat)
    s_all = _nt(k_own, qcat)
    blk = lax.broadcasted_iota(jnp.int32, (n_blocks, TQ), 0)
    blk_f = blk.astype(F32)

    tq = lax.broadcasted_iota(jnp.int32, (1, TQ), 1).astype(F32)
    init = []
    for h in range(HEADS):
        gate = jnp.where(blk < i, gate_all[:, heads[h]], NEG_INF)
        chosen = jnp.zeros((n_blocks, TQ), jnp.bool_)
        for r in range(MOBA_TOPK):
            mx = jnp.max(gate, axis=0, keepdims=True)
            first = jnp.min(jnp.where(gate == mx, blk_f, float(n_blocks)), axis=0, keepdims=True)
            hit = blk_f == first
            chosen = chosen | (hit & (mx > NEG_INF))
            gate = jnp.where(hit, NEG_INF, gate)
        selb_ref[h] = jnp.where(chosen, 0.0, NEG_INF)
        aq = -slopes[h] * tq
        s = jnp.where(dist0 >= 0.0, s_all[:, heads[h]], NEG_INF)
        m0 = jnp.max(s, axis=0, keepdims=True) + aq
        p = jnp.exp(s - (m0 - aq)).astype(BF16)
        init += [m0, _dot(vt_ref[i, vrows[h], :], p)]

    def stage_scores(j, slot):
        kj = k_ref[0, pl.ds(pl.multiple_of(j * MOBA_BLOCK, MOBA_BLOCK), MOBA_BLOCK), :]
        s_ref[slot] = _nt(kj, qcat_ref[...])

    stage_scores(0, 0)

    def past_blocks(g, carry):
        carry = list(carry)
        for u in range(KV_UNROLL):
            j = g * KV_UNROLL + u
            stage_scores(jnp.minimum(j + 1, n_blocks - 1), (u + 1) % 2)
            off = jnp.full((1, TQ), (i - j) * MOBA_BLOCK, jnp.int32).astype(F32) + tq
            for h in range(HEADS):
                m, accl = carry[2 * h:2 * h + 2]
                s1 = s_ref[u % 2, :, heads[h]]
                rb = selb_ref[h, pl.ds(j, 1), :] - slopes[h] * off
                m_new = jnp.maximum(m, jnp.max(s1, axis=0, keepdims=True) + rb)
                p = jnp.exp(s1 - (m_new - rb)).astype(BF16)
                carry[2 * h:2 * h + 2] = [m_new, jnp.exp(m - m_new) * accl + _dot(vt_ref[j, vrows[h], :], p)]
        return tuple(carry)

    n_groups = (i + KV_UNROLL - 1) // KV_UNROLL
    final = lax.fori_loop(0, n_groups, past_blocks, tuple(init))
    o_t = jnp.concatenate([finish(final[2 * h + 1]) for h in range(HEADS)], axis=0)
    o_moba_ref[0] = o_t.T.astype(BF16)


def _moba_mem(mq, mk, mvt, kmean, cq, memk, memvt, b, s):
    nq = s // TQ
    n_blocks = s // MOBA_BLOCK
    qspec = pl.BlockSpec((1, TQ, QK_W), lambda i, j: (i, j, 0))
    return pl.pallas_call(
        functools.partial(_moba_mem_kernel, n_blocks=n_blocks),
        grid=(b, nq),
        in_specs=[
            qspec,
            pl.BlockSpec((1, s, QK_W), lambda i, j: (i, 0, 0)),
            pl.BlockSpec((n_blocks, VT_ALL, MOBA_BLOCK), lambda i, j: (i, 0, 0)),
            pl.BlockSpec((1, n_blocks, QK_W), lambda i, j: (i, 0, 0)),
            qspec,
            pl.BlockSpec((1, N_MEM, QK_W), lambda i, j: (i, 0, 0)),
            pl.BlockSpec((1, VT_ALL, N_MEM), lambda i, j: (i, 0, 0)),
        ],
        out_specs=[qspec, qspec],
        out_shape=[jax.ShapeDtypeStruct((b, s, QK_W), BF16), jax.ShapeDtypeStruct((b, s, QK_W), BF16)],
        scratch_shapes=[
            pltpu.VMEM((HEADS * TQ, QK_W), BF16),
            pltpu.VMEM((HEADS, n_blocks, TQ), F32),
            pltpu.VMEM((2, MOBA_BLOCK, HEADS * TQ), F32),
        ],
        compiler_params=_params(("arbitrary", "arbitrary")),
        name="moba_mem",
    )(mq.reshape(b, s, QK_W), mk.reshape(b, s, QK_W), mvt, kmean.reshape(b, n_blocks, QK_W),
      cq.reshape(b, s, QK_W), memk, memvt)


OUT_TM = 512


def _out_router_kernel(x_ref, og_ref, om_ref, oc_ref, w_ref, g_ref, wr_ref, br_ref, tri_ref,
                       x1_ref, h_ref, e_ref, gate_ref, count_ref, carry_ref):
    w = w_ref[...]
    x1 = (x_ref[...] + _dot(og_ref[...], w[:GLA_V_W]) + _dot(om_ref[...], w[GLA_V_W:GLA_V_W + QK_W])
          + _dot(oc_ref[...], w[GLA_V_W + QK_W:]))
    x1_ref[...] = x1
    hn = _rms_rows(x1, g_ref[...])
    _store_row_tiles(h_ref, hn)
    logits = _dot(hn.astype(BF16), wr_ref[...]) + br_ref[...]
    lane = lax.broadcasted_iota(jnp.int32, logits.shape, 1).astype(F32)
    lg = jnp.where(lane < N_GROUPS, logits, NEG_INF)
    mg = jnp.max(lg, axis=-1, keepdims=True)
    g_sel = jnp.min(jnp.where(lg == mg, lane, float(LANES)), axis=-1, keepdims=True)
    p_group = 1.0 / jnp.sum(jnp.exp(lg - mg), axis=-1, keepdims=True)
    lo = N_GROUPS + g_sel * EXPERTS_PER_GROUP
    le = jnp.where((lane >= lo) & (lane < lo + EXPERTS_PER_GROUP), logits, NEG_INF)
    m0 = jnp.max(le, axis=-1, keepdims=True)
    i0 = jnp.min(jnp.where(le == m0, lane, float(LANES)), axis=-1, keepdims=True)
    le1 = jnp.where(lane == i0, NEG_INF, le)
    m1 = jnp.max(le1, axis=-1, keepdims=True)
    i1 = jnp.min(jnp.where(le1 == m1, lane, float(LANES)), axis=-1, keepdims=True)
    z = jnp.exp(m1 - m0)
    w0 = p_group / (1.0 + z)
    w1 = p_group * z / (1.0 + z)
    gate_ref[...] = jnp.where(lane == 0.0, w0, jnp.where(lane == 1.0, w1, 0.0))

    @pl.when(pl.program_id(0) == 0)
    def _():
        carry_ref[...] = jnp.zeros_like(carry_ref)

    oh0 = (lane == i0).astype(F32)
    oh1 = (lane == i1).astype(F32)
    both = oh0 + oh1
    before = _dot(tri_ref[...], both.astype(BF16)) + carry_ref[...]
    r0 = jnp.sum(oh0 * before, axis=-1, keepdims=True)
    r1 = jnp.sum(oh1 * before, axis=-1, keepdims=True)
    carry_ref[...] = carry_ref[...] + jnp.sum(both, axis=0, keepdims=True)
    count_ref[...] = jnp.broadcast_to(carry_ref[...], count_ref.shape).astype(jnp.int32)
    info = jnp.where(lane == 0.0, i0 - N_GROUPS, jnp.where(lane == 1.0, i1 - N_GROUPS,
                     jnp.where(lane == 2.0, r0, jnp.where(lane == 3.0, r1, 0.0))))
    e_ref[...] = info.astype(jnp.int32)


def _out_router(x2, og, om, oc, w_bf, g, wr_bf, br):
    t = x2.shape[0]
    row = lambda w: pl.BlockSpec((OUT_TM, w), lambda i: (i, 0))
    const = lambda a, b: pl.BlockSpec((a, b), lambda i: (0, 0))
    idx = np.arange(OUT_TM)
    tri = jnp.asarray(idx[:, None] > idx[None, :], BF16)
    return pl.pallas_call(
        _out_router_kernel,
        grid=(t // OUT_TM,),
        in_specs=[row(D_MODEL), row(GLA_V_W), row(QK_W), row(QK_W), const(D_MODEL, D_MODEL),
                  const(1, D_MODEL), const(D_MODEL, LANES), const(1, LANES), const(OUT_TM, OUT_TM)],
        out_specs=[row(D_MODEL), pl.BlockSpec((OUT_TM * ROW_TILE, LANES), lambda i: (i, 0)),
                   row(LANES), row(LANES), const(8, LANES)],
        out_shape=[
            jax.ShapeDtypeStruct((t, D_MODEL), F32),
            jax.ShapeDtypeStruct((t * ROW_TILE, LANES), F32),
            jax.ShapeDtypeStruct((t, LANES), jnp.int32),
            jax.ShapeDtypeStruct((t, LANES), F32),
            jax.ShapeDtypeStruct((8, LANES), jnp.int32),
        ],
        scratch_shapes=[pltpu.VMEM((1, LANES), F32)],
        compiler_params=_params(("arbitrary",)),
        name="out_router",
    )(x2, og, om, oc, w_bf, g, wr_bf, br, tri)


DISP_TM = 512
DMA_UNROLL = 8


def _dispatch_kernel(dest_ref, h_ref, xs_in_ref, xs_ref, sem):
    del xs_in_ref

    def row_copy(r, d):
        src = h_ref.at[pl.ds(pl.multiple_of(r * ROW_TILE, ROW_TILE), ROW_TILE), :]
        dst = xs_ref.at[pl.ds(pl.multiple_of(d * ROW_TILE, ROW_TILE), ROW_TILE), :]
        return pltpu.make_async_copy(src, dst, sem)

    def issue(r, c):
        row_copy(r, dest_ref[0, 0, 2 * r]).start(priority=0)
        row_copy(r, dest_ref[0, 0, 2 * r + 1]).start(priority=1)
        return c

    lax.fori_loop(0, DISP_TM, issue, 0, unroll=DMA_UNROLL)

    def drain(r, c):
        row_copy(0, 0).wait()
        return c

    lax.fori_loop(0, 2 * DISP_TM, drain, 0, unroll=DMA_UNROLL)


def _dispatch(hn, dest, n_rows):
    t = hn.shape[0] // ROW_TILE
    nt = t // DISP_TM
    xs0 = jnp.zeros((n_rows * ROW_TILE, LANES), F32)
    return pl.pallas_call(
        _dispatch_kernel,
        grid=(nt,),
        in_specs=[pl.BlockSpec((1, 1, 2 * DISP_TM), lambda i: (i, 0, 0), memory_space=pltpu.SMEM),
                  pl.BlockSpec((DISP_TM * ROW_TILE, LANES), lambda i: (i, 0)),
                  pl.BlockSpec(memory_space=pl.ANY)],
        out_specs=pl.BlockSpec(memory_space=pl.ANY),
        out_shape=jax.ShapeDtypeStruct((n_rows * ROW_TILE, LANES), F32),
        scratch_shapes=[pltpu.SemaphoreType.DMA],
        input_output_aliases={2: 0},
        compiler_params=_params(("arbitrary",)),
        name="dispatch",
    )(dest.reshape(nt, 1, 2 * DISP_TM), hn, xs0)


def _experts_kernel(be_ref, nused_ref, xs_ref, wg_ref, wu_ref, wd_ref, ys_ref, wg_bf, wu_bf, wd_bf):
    i = pl.program_id(0)
    new_expert = jnp.logical_or(i == 0, be_ref[i] != be_ref[jnp.maximum(i - 1, 0)])

    @pl.when(new_expert)
    def _():
        wg_bf[...] = wg_ref[0].astype(BF16)
        wu_bf[...] = wu_ref[0].astype(BF16)
        wd_bf[...] = wd_ref[0].astype(BF16)

    @pl.when(i < nused_ref[0])
    def _():
        xb = _load_row_tiles(xs_ref, ROW_BLOCK).astype(BF16)
        hid = _silu(_dot(xb, wg_bf[...])) * _dot(xb, wu_bf[...])
        _store_row_tiles(ys_ref, _dot(hid.astype(BF16), wd_bf[...]))

    @pl.when(i >= nused_ref[0])
    def _():
        ys_ref[...] = jnp.zeros_like(ys_ref)


def _experts(xs, block_expert, n_used, w_gate, w_up, w_down):
    n_rows = xs.shape[0] // ROW_TILE
    nb = n_rows // ROW_BLOCK
    grid_spec = pltpu.PrefetchScalarGridSpec(
        num_scalar_prefetch=2,
        grid=(nb,),
        in_specs=[
            pl.BlockSpec((ROW_BLOCK * ROW_TILE, LANES), lambda i, be, nu: (i, 0)),
            pl.BlockSpec((1, D_MODEL, MOE_FF), lambda i, be, nu: (be[i], 0, 0)),
            pl.BlockSpec((1, D_MODEL, MOE_FF), lambda i, be, nu: (be[i], 0, 0)),
            pl.BlockSpec((1, MOE_FF, D_MODEL), lambda i, be, nu: (be[i], 0, 0)),
        ],
        out_specs=pl.BlockSpec((ROW_BLOCK * ROW_TILE, LANES), lambda i, be, nu: (i, 0)),
        scratch_shapes=[pltpu.VMEM((D_MODEL, MOE_FF), BF16), pltpu.VMEM((D_MODEL, MOE_FF), BF16),
                        pltpu.VMEM((MOE_FF, D_MODEL), BF16)],
    )
    return pl.pallas_call(
        _experts_kernel,
        grid_spec=grid_spec,
        out_shape=jax.ShapeDtypeStruct((n_rows * ROW_TILE, LANES), F32),
        compiler_params=_params(("arbitrary",)),
        name="experts",
    )(block_expert, n_used, xs, w_gate, w_up, w_down)


COMB_TM = 256


def _combine_kernel(dest_ref, dest_next_ref, x1_ref, gate_ref, ys_ref, out_ref, ybuf, sems):
    g = pl.program_id(0)

    def row_copy(d, slot, k, r):
        src = ys_ref.at[pl.ds(pl.multiple_of(d * ROW_TILE, ROW_TILE), ROW_TILE), :]
        dst = ybuf.at[slot, k, pl.ds(pl.multiple_of(r * ROW_TILE, ROW_TILE), ROW_TILE), :]
        return pltpu.make_async_copy(src, dst, sems.at[slot])

    def gather(dref, base, slot):
        def issue(r, c):
            row_copy(dref[0, 0, base + 2 * r], slot, 0, r).start(priority=0)
            row_copy(dref[0, 0, base + 2 * r + 1], slot, 1, r).start(priority=1)
            return c
        lax.fori_loop(0, COMB_TM, issue, 0, unroll=DMA_UNROLL)

    def finish(slot):
        def drain(r, c):
            row_copy(0, slot, 0, 0).wait()
            return c
        lax.fori_loop(0, 2 * COMB_TM, drain, 0, unroll=DMA_UNROLL)
        rows = slice(slot * COMB_TM, (slot + 1) * COMB_TM)
        gate = gate_ref[rows, :]
        lane = lax.broadcasted_iota(jnp.int32, gate.shape, 1)
        w0 = jnp.sum(jnp.where(lane == 0, gate, 0.0), axis=-1, keepdims=True)
        w1 = jnp.sum(jnp.where(lane == 1, gate, 0.0), axis=-1, keepdims=True)
        y0 = _load_row_tiles(ybuf.at[slot, 0], COMB_TM)
        y1 = _load_row_tiles(ybuf.at[slot, 1], COMB_TM)
        out_ref[rows, :] = x1_ref[rows, :] + (y0 * w0 + y1 * w1)

    @pl.when(g == 0)
    def _():
        gather(dest_ref, 0, 0)

    gather(dest_ref, 2 * COMB_TM, 1)
    finish(0)

    @pl.when(g + 1 < pl.num_programs(0))
    def _():
        gather(dest_next_ref, 0, 0)

    finish(1)


def _combine(x1, gates, ys, dest):
    t = x1.shape[0]
    ng = t // (2 * COMB_TM)
    dest3 = dest.reshape(ng, 1, 4 * COMB_TM)
    smem = lambda f: pl.BlockSpec((1, 1, 4 * COMB_TM), f, memory_space=pltpu.SMEM)
    return pl.pallas_call(
        _combine_kernel,
        grid=(ng,),
        in_specs=[smem(lambda i: (i, 0, 0)),
                  smem(lambda i: (jnp.minimum(i + 1, ng - 1), 0, 0)),
                  pl.BlockSpec((2 * COMB_TM, D_MODEL), lambda i: (i, 0)),
                  pl.BlockSpec((2 * COMB_TM, LANES), lambda i: (i, 0)),
                  pl.BlockSpec(memory_space=pl.ANY)],
        out_specs=pl.BlockSpec((2 * COMB_TM, D_MODEL), lambda i: (i, 0)),
        out_shape=jax.ShapeDtypeStruct((t, D_MODEL), F32),
        scratch_shapes=[pltpu.VMEM((2, 2, COMB_TM * ROW_TILE, LANES), F32),
                        pltpu.SemaphoreType.DMA((2,))],
        compiler_params=_params(("arbitrary",)),
        name="combine",
    )(dest3, dest3, x1, gates, ys)


def _layer(x, mem, attn_norm_g, mem_norm_g, w_in, w_gla_gk, b_gla_gk, gla_out_norm_g,
           moba_q_norm_g, moba_k_norm_g, w_mem_kv, mem_q_norm_g, mem_k_norm_g, w_out,
           ffn_norm_g, w_router_group, b_router_group, w_router_expert, b_router_expert,
           w_gate, w_up, w_down):
    b, s, d = x.shape
    t = b * s
    x2 = x.reshape(t, d)
    row = lambda v: v.reshape(1, -1).astype(F32)
    tile_heads = lambda v: jnp.tile(v.astype(F32), HEADS).reshape(1, QK_W)
    hid = np.arange(QK_W) // DH
    seg = jnp.asarray((hid[:, None] == hid[None, :]) / DH, BF16)

    w_in_p = jnp.concatenate([w_in[:, :1536], w_in[:, 1552:], w_in[:, 1536:1552]], axis=1).astype(BF16)
    wr = jnp.concatenate([w_router_group,
                          jnp.transpose(w_router_expert, (1, 0, 2)).reshape(d, N_EXPERTS),
                          jnp.zeros((d, LANES - N_GROUPS - N_EXPERTS), F32)], axis=1).astype(BF16)
    br = jnp.concatenate([b_router_group, b_router_expert.reshape(N_EXPERTS),
                          jnp.zeros((LANES - N_GROUPS - N_EXPERTS,), F32)]).reshape(1, LANES)

    memk, memvt = _mem_kv(mem, row(mem_norm_g), w_mem_kv.astype(BF16), tile_heads(mem_k_norm_g), seg)
    qk, gv, gr, glr, mq, mk, mvt, kmean, cq = _in_proj(
        x2, row(attn_norm_g), w_in_p, seg, tile_heads(moba_q_norm_g), tile_heads(moba_k_norm_g),
        tile_heads(mem_q_norm_g))
    o_gla = _gla(qk, gv, gr, glr, w_gla_gk.astype(BF16), row(b_gla_gk), row(gla_out_norm_g), b, s)
    o_moba, o_mem = _moba_mem(mq, mk, mvt, kmean, cq, memk, memvt, b, s)
    x1, hn, e_ids, gates, counts = _out_router(
        x2, o_gla.reshape(t, GLA_V_W), o_moba.reshape(t, QK_W), o_mem.reshape(t, QK_W),
        w_out.astype(BF16), row(ffn_norm_g), wr, br)

    counts = counts[0, N_GROUPS:N_GROUPS + N_EXPERTS]
    padded = (counts + ROW_BLOCK - 1) // ROW_BLOCK * ROW_BLOCK
    pends = jnp.cumsum(padded)
    pstarts = pends - padded
    onehot = e_ids[:, :2, None] == jnp.arange(N_EXPERTS, dtype=jnp.int32)
    dest = (jnp.sum(jnp.where(onehot, pstarts, 0), axis=-1) + e_ids[:, 2:4]).astype(jnp.int32).reshape(-1)
    n_rows = (t * 2 + N_EXPERTS * (ROW_BLOCK - 1) + ROW_BLOCK - 1) // ROW_BLOCK * ROW_BLOCK
    nb = n_rows // ROW_BLOCK
    block_start = jnp.arange(nb, dtype=jnp.int32) * ROW_BLOCK
    block_expert = jnp.minimum(jnp.sum(block_start[:, None] >= pends[None, :], axis=1),
                               N_EXPERTS - 1).astype(jnp.int32)
    n_used = (pends[-1] // ROW_BLOCK).astype(jnp.int32).reshape(1)

    xs = _dispatch(hn, dest, n_rows)
    ys = _experts(xs, block_expert, n_used, w_gate, w_up, w_down)
    out = _combine(x1, gates, ys, dest)
    return out.reshape(b, s, d)


def kernel(x, mem, attn_norm_g, mem_norm_g, w_in, w_gla_gk, b_gla_gk, gla_out_norm_g, moba_q_norm_g, moba_k_norm_g, w_mem_kv, mem_q_norm_g, mem_k_norm_g, w_out, ffn_norm_g, w_router_group, b_router_group, w_router_expert, b_router_expert, w_gate, w_up, w_down):
    depth = w_in.shape[0]
    for l in range(depth):
        x = _layer(x, mem, attn_norm_g[l], mem_norm_g[l], w_in[l], w_gla_gk[l], b_gla_gk[l],
                   gla_out_norm_g[l], moba_q_norm_g[l], moba_k_norm_g[l], w_mem_kv[l],
                   mem_q_norm_g[l], mem_k_norm_g[l], w_out[l], ffn_norm_g[l], w_router_group[l],
                   b_router_group[l], w_router_expert[l], b_router_expert[l],
                   w_gate[l], w_up[l], w_down[l])
    return x
```

```python
import functools

import jax
import jax.numpy as jnp
import numpy as np
from jax import lax
from jax.experimental import pallas as pl
from jax.experimental.pallas import tpu as pltpu

F32 = jnp.float32
BF16 = jnp.bfloat16
EPS = 1e-6
NEG_INF = float("-inf")

D_MODEL = 1024
N_MEM = 256
HEADS = 4
DH = 64
GLA_DV = 128
GLA_RANK = 16
GLA_GATE_NORMALIZER = 16.0
GLA_CHUNK = 64
MOBA_BLOCK = 256
MOBA_TOPK = 3
QK_W = HEADS * DH
GLA_V_W = HEADS * GLA_DV
N_GROUPS = 4
EXPERTS_PER_GROUP = 8
N_EXPERTS = N_GROUPS * EXPERTS_PER_GROUP
MOE_FF = 512
LANES = 128
ROW_BLOCK = 256

VMEM_LIMIT = 56 * 1024 * 1024


def _params(sem):
    return pltpu.CompilerParams(dimension_semantics=sem, vmem_limit_bytes=VMEM_LIMIT)


def _nt(a, b):
    return lax.dot_general(a, b, (((1,), (1,)), ((), ())), preferred_element_type=F32)


def _tn(a, b):
    return lax.dot_general(a, b, (((0,), (0,)), ((), ())), preferred_element_type=F32)


def _dot(a, b):
    return jnp.dot(a, b, preferred_element_type=F32)


def _rms_rows(x, g):
    ms = jnp.mean(x * x, axis=-1, keepdims=True)
    return x * lax.rsqrt(ms + EPS) * g


def _split3(x):
    h1 = x.astype(BF16)
    r1 = x - h1.astype(F32)
    h2 = r1.astype(BF16)
    h3 = (r1 - h2.astype(F32)).astype(BF16)
    return h1, h2, h3


def _head_mean_sq(x, seg):
    sq = x * x
    hi = sq.astype(BF16)
    lo = (sq - hi.astype(F32)).astype(BF16)
    return _dot(hi, seg) + _dot(lo, seg)


def _silu(x):
    return x * (1.0 / (1.0 + jnp.exp(-x)))


VT_ROWS = DH + 16
VT_ALL = HEADS * VT_ROWS


ALIBI_SLOPES = tuple(2.0 ** (-8.0 * (h + 1) / HEADS) for h in range(HEADS))


def _vt_with_ones(v, key_slopes=None):
    n = v.shape[0]
    vt = v.T
    ones = jnp.ones((VT_ROWS - DH, n), F32)
    pos = lax.broadcasted_iota(jnp.int32, (1, n), 1).astype(F32)
    parts = []
    for h in range(HEADS):
        scale = 1.0 if key_slopes is None else jnp.exp(key_slopes[h] * pos)
        parts += [vt[h * DH:(h + 1) * DH] * scale, ones * scale]
    return jnp.concatenate(parts, axis=0).astype(BF16)


ROW_TILE = D_MODEL // LANES
X_TILE = ROW_TILE // 2


def _store_row_tiles(ref, x):
    n, tiles = x.shape[0], x.shape[1] // LANES
    for c in range(tiles):
        ref[pl.ds(c, n, stride=tiles), :] = x[:, c * LANES:(c + 1) * LANES]


def _load_row_tiles(ref, n):
    tiles = ref.shape[0] // n
    return jnp.concatenate([ref[pl.ds(c, n, stride=tiles), :] for c in range(tiles)], axis=1)


def _pack_bf16_pairs(x):
    w = x.shape[1] // 2
    bits = lambda t: lax.bitcast_convert_type(t.astype(BF16).astype(F32), jnp.uint32)
    return (bits(x[:, w:]) & jnp.uint32(0xFFFF0000)) | (bits(x[:, :w]) >> 16)


def _unpack_bf16_pairs(words):
    lo = lax.bitcast_convert_type(words << 16, F32).astype(BF16)
    hi = lax.bitcast_convert_type(words & jnp.uint32(0xFFFF0000), F32).astype(BF16)
    return lo, hi


def _mem_kv_kernel(mem_ref, g_ref, w_ref, gk_ref, seg_ref, k_ref, vt_ref):
    h = _rms_rows(mem_ref[0], g_ref[...]).astype(BF16)
    kv = _dot(h, w_ref[...])
    k = kv[:, :QK_W]
    kn = k * lax.rsqrt(_head_mean_sq(k, seg_ref[...]) + EPS) * gk_ref[...]
    k_ref[0] = kn.astype(BF16)
    vt_ref[0] = _vt_with_ones(kv[:, QK_W:])


def _mem_kv(mem, g, w_bf, gk_t, seg):
    b = mem.shape[0]
    return pl.pallas_call(
        _mem_kv_kernel,
        grid=(b,),
        in_specs=[
            pl.BlockSpec((1, N_MEM, D_MODEL), lambda i: (i, 0, 0)),
            pl.BlockSpec((1, D_MODEL), lambda i: (0, 0)),
            pl.BlockSpec((D_MODEL, 2 * QK_W), lambda i: (0, 0)),
            pl.BlockSpec((1, QK_W), lambda i: (0, 0)),
            pl.BlockSpec((QK_W, QK_W), lambda i: (0, 0)),
        ],
        out_specs=[
            pl.BlockSpec((1, N_MEM, QK_W), lambda i: (i, 0, 0)),
            pl.BlockSpec((1, VT_ALL, N_MEM), lambda i: (i, 0, 0)),
        ],
        out_shape=[
            jax.ShapeDtypeStruct((b, N_MEM, QK_W), BF16),
            jax.ShapeDtypeStruct((b, VT_ALL, N_MEM), BF16),
        ],
        compiler_params=_params(("arbitrary",)),
        name="mem_kv",
    )(mem, g, w_bf, gk_t, seg)


IN_TM = 512
_C_QK, _C_V, _C_R, _C_MQ, _C_MK, _C_MV, _C_CQ, _C_LR = 0, 512, 1024, 1536, 1792, 2048, 2304, 2560
D_IN = 2576


def _in_proj_kernel(x_ref, g_ref, w_ref, seg_ref, gq_ref, gk_ref, gc_ref,
                    qk_ref, v_ref, r_ref, lr_ref, mq_ref, mk_ref, mvt_ref, kmean_ref, cq_ref):
    h = _rms_rows(x_ref[...], g_ref[...]).astype(BF16)
    p = _dot(h, w_ref[...])
    seg = seg_ref[...]
    qk_ref[...] = p[:, _C_QK:_C_V]
    v_ref[...] = p[:, _C_V:_C_R].astype(BF16)
    r_ref[...] = p[:, _C_R:_C_MQ]
    lr_ref[...] = p[:, _C_LR:D_IN]

    def head_norm(t, gain):
        return t * lax.rsqrt(_head_mean_sq(t, seg) + EPS) * gain

    scale = DH ** -0.5
    mq_ref[...] = (head_norm(p[:, _C_MQ:_C_MK], gq_ref[...]) * scale).astype(BF16)
    cq_ref[...] = (head_norm(p[:, _C_CQ:_C_LR], gc_ref[...]) * scale).astype(BF16)
    kn = head_norm(p[:, _C_MK:_C_MV], gk_ref[...])
    mk_ref[...] = kn.astype(BF16)
    mv = p[:, _C_MV:_C_CQ]
    for j in range(IN_TM // MOBA_BLOCK):
        rows = slice(j * MOBA_BLOCK, (j + 1) * MOBA_BLOCK)
        kmean_ref[0, j:j + 1, :] = jnp.mean(kn[rows], axis=0, keepdims=True)
        mvt_ref[j] = _vt_with_ones(mv[rows], ALIBI_SLOPES)


def _in_proj(x2, g, w_bf, seg, gq_t, gk_t, gc_t):
    t = x2.shape[0]
    nt = t // IN_TM
    nb = IN_TM // MOBA_BLOCK
    row = lambda w: pl.BlockSpec((IN_TM, w), lambda i: (i, 0))
    const = lambda a, b: pl.BlockSpec((a, b), lambda i: (0, 0))
    return pl.pallas_call(
        _in_proj_kernel,
        grid=(nt,),
        in_specs=[row(D_MODEL), const(1, D_MODEL), const(D_MODEL, D_IN), const(QK_W, QK_W),
                  const(1, QK_W), const(1, QK_W), const(1, QK_W)],
        out_specs=[row(2 * QK_W), row(GLA_V_W), row(GLA_V_W), row(GLA_RANK), row(QK_W), row(QK_W),
                   pl.BlockSpec((nb, VT_ALL, MOBA_BLOCK), lambda i: (i, 0, 0)),
                   pl.BlockSpec((1, nb, QK_W), lambda i: (i, 0, 0)),
                   row(QK_W)],
        out_shape=[
            jax.ShapeDtypeStruct((t, 2 * QK_W), F32),
            jax.ShapeDtypeStruct((t, GLA_V_W), BF16),
            jax.ShapeDtypeStruct((t, GLA_V_W), F32),
            jax.ShapeDtypeStruct((t, GLA_RANK), F32),
            jax.ShapeDtypeStruct((t, QK_W), BF16),
            jax.ShapeDtypeStruct((t, QK_W), BF16),
            jax.ShapeDtypeStruct((t // MOBA_BLOCK, VT_ALL, MOBA_BLOCK), BF16),
            jax.ShapeDtypeStruct((nt, nb, QK_W), F32),
            jax.ShapeDtypeStruct((t, QK_W), BF16),
        ],
        compiler_params=_params(("arbitrary",)),
        name="in_proj",
    )(x2, g, w_bf, seg, gq_t, gk_t, gc_t)


GLA_TC = 256


def _gla_kernel(qk_ref, v_ref, r_ref, lr_ref, wgk_ref, bgk_ref, gn_ref, tri_ref, ones_ref,
                o_ref, st_ref):
    @pl.when(pl.program_id(1) == 0)
    def _():
        st_ref[...] = jnp.zeros_like(st_ref)

    qk = qk_ref[0]
    q = qk[:, :QK_W]
    k = qk[:, QK_W:]
    gk = _dot(lr_ref[0].astype(BF16), wgk_ref[...]) + bgk_ref[...]
    g = -(jnp.maximum(-gk, 0.0) + jnp.log1p(jnp.exp(-jnp.abs(gk)))) / GLA_GATE_NORMALIZER
    g1, g2, g3 = _split3(g)
    tri = tri_ref[...]
    ones = ones_ref[...]
    cum = _dot(tri, g1) + _dot(tri, g2) + _dot(tri, g3)
    tot = _dot(ones, g1) + _dot(ones, g2) + _dot(ones, g3)
    q_dec = (q * (DH ** -0.5) * jnp.exp(cum)).astype(BF16)
    k_inv = (k * jnp.exp(-cum)).astype(BF16)
    k_end = (k * jnp.exp(tot - cum)).astype(BF16)
    decay = jnp.exp(tot)

    lane_head = lax.broadcasted_iota(jnp.int32, (GLA_TC, QK_W), 1) // DH
    row_t = lax.broadcasted_iota(jnp.int32, (GLA_TC, GLA_TC), 0)
    col_t = lax.broadcasted_iota(jnp.int32, (GLA_TC, GLA_TC), 1)
    causal = (row_t >= col_t) & (row_t // GLA_CHUNK == col_t // GLA_CHUNK)
    same_head = (lax.broadcasted_iota(jnp.int32, (GLA_V_W, QK_W), 0) // GLA_DV
                 == lax.broadcasted_iota(jnp.int32, (GLA_V_W, QK_W), 1) // DH)
    gain = gn_ref[...]

    v = v_ref[0]
    chunks = [slice(c * GLA_CHUNK, (c + 1) * GLA_CHUNK) for c in range(GLA_TC // GLA_CHUNK)]
    d_st = [jnp.where(same_head, _tn(v[rows], k_end[rows]), 0.0) for rows in chunks]
    states = [st_ref[...]]
    for c, rows in enumerate(chunks):
        states.append(states[c] * decay[rows.start:rows.start + 1, :] + d_st[c])
    st_ref[...] = states[-1]
    o_inter = jnp.concatenate([_nt(q_dec[rows], states[c].astype(BF16)) for c, rows in enumerate(chunks)],
                              axis=0)
    q_stack = jnp.concatenate([jnp.where(lane_head == h, q_dec, jnp.zeros_like(q_dec)) for h in range(HEADS)],
                              axis=0)
    a_all = _nt(q_stack, k_inv)
    outs = []
    for h in range(HEADS):
        a = jnp.where(causal, a_all[h * GLA_TC:(h + 1) * GLA_TC], 0.0).astype(BF16)
        vs = slice(h * GLA_DV, (h + 1) * GLA_DV)
        oh = _dot(a, v[:, vs]) + o_inter[:, vs]
        outs.append(_rms_rows(oh, gain) * _silu(r_ref[0, :, vs]))
    o_ref[0] = jnp.concatenate(outs, axis=-1).astype(BF16)


def _gla(qk, v, r, lr, wgk_bf, bgk, gn, b, s):
    ns = s // GLA_TC
    idx = np.arange(GLA_TC)
    same_chunk = (idx[:, None] // GLA_CHUNK) == (idx[None, :] // GLA_CHUNK)
    tri = jnp.asarray(same_chunk & (idx[:, None] >= idx[None, :]), BF16)
    ones = jnp.asarray(same_chunk, BF16)
    seq = lambda w: pl.BlockSpec((1, GLA_TC, w), lambda i, j: (i, j, 0))
    const = lambda a, c: pl.BlockSpec((a, c), lambda i, j: (0, 0))
    return pl.pallas_call(
        _gla_kernel,
        grid=(b, ns),
        in_specs=[seq(2 * QK_W), seq(GLA_V_W), seq(GLA_V_W), seq(GLA_RANK),
                  const(GLA_RANK, QK_W), const(1, QK_W), const(1, GLA_DV),
                  const(GLA_TC, GLA_TC), const(GLA_TC, GLA_TC)],
        out_specs=seq(GLA_V_W),
        out_shape=jax.ShapeDtypeStruct((b, s, GLA_V_W), BF16),
        scratch_shapes=[pltpu.VMEM((GLA_V_W, QK_W), F32)],
        compiler_params=_params(("arbitrary", "arbitrary")),
        name="gla",
    )(qk.reshape(b, s, 2 * QK_W), v.reshape(b, s, GLA_V_W), r.reshape(b, s, GLA_V_W),
      lr.reshape(b, s, GLA_RANK), wgk_bf, bgk, gn, tri, ones)


TQ = MOBA_BLOCK
KV_UNROLL = 4


def _moba_mem_kernel(q_ref, k_ref, vt_ref, kmean_ref, cq_ref, mk_ref, mvt_ref,
                     o_moba_ref, o_mem_ref,
                     qcat_ref, selb_ref, s_ref, *, n_blocks):
    i = pl.program_id(1)
    lane_head = lax.broadcasted_iota(jnp.int32, (TQ, QK_W), 1) // DH
    dist0 = (lax.broadcasted_iota(jnp.int32, (MOBA_BLOCK, TQ), 1)
             - lax.broadcasted_iota(jnp.int32, (MOBA_BLOCK, TQ), 0)).astype(F32)
    slopes = ALIBI_SLOPES
    heads = [slice(h * TQ, (h + 1) * TQ) for h in range(HEADS)]
    vrows = [slice(h * VT_ROWS, (h + 1) * VT_ROWS) for h in range(HEADS)]

    def stack_heads(x):
        return jnp.concatenate([jnp.where(lane_head == h, x, jnp.zeros_like(x)) for h in range(HEADS)], axis=0)

    def finish(accl):
        return accl[:DH] * (1.0 / accl[DH:DH + 1])

    s_all = _nt(mk_ref[0], stack_heads(cq_ref[0]))
    outs = []
    for h in range(HEADS):
        s = s_all[:, heads[h]]
        p = jnp.exp(s - jnp.max(s, axis=0, keepdims=True)).astype(BF16)
        outs.append(finish(_dot(mvt_ref[0, vrows[h], :], p)))
    o_mem_ref[0] = jnp.concatenate(outs, axis=0).T.astype(BF16)

    qcat = stack_heads(q_ref[0])
    qcat_ref[...] = qcat
    k_own = k_ref[0, pl.ds(pl.multiple_of(i * MOBA_BLOCK, MOBA_BLOCK), MOBA_BLOCK), :]
    gate_all = _nt(kmean_ref[0].astype(BF16), qcat)
    s_all = _nt(k_own, qcat)
    blk = lax.broadcasted_iota(jnp.int32, (n_blocks, TQ), 0)
    blk_f = blk.astype(F32)

    tq = lax.broadcasted_iota(jnp.int32, (1, TQ), 1).astype(F32)
    init = []
    for h in range(HEADS):
        gate = jnp.where(blk < i, gate_all[:, heads[h]], NEG_INF)
        chosen = jnp.zeros((n_blocks, TQ), jnp.bool_)
        for r in range(MOBA_TOPK):
            mx = jnp.max(gate, axis=0, keepdims=True)
            first = jnp.min(jnp.where(gate == mx, blk_f, float(n_blocks)), axis=0, keepdims=True)
            hit = blk_f == first
            chosen = chosen | (hit & (mx > NEG_INF))
            gate = jnp.where(hit, NEG_INF, gate)
        selb_ref[h] = jnp.where(chosen, 0.0, NEG_INF)
        aq = -slopes[h] * tq
        s = jnp.where(dist0 >= 0.0, s_all[:, heads[h]], NEG_INF)
        m0 = jnp.max(s, axis=0, keepdims=True) + aq
        p = jnp.exp(s - (m0 - aq)).astype(BF16)
        init += [m0, _dot(vt_ref[i, vrows[h], :], p)]

    def stage_scores(j, slot):
        kj = k_ref[0, pl.ds(pl.multiple_of(j * MOBA_BLOCK, MOBA_BLOCK), MOBA_BLOCK), :]
        s_ref[slot] = _nt(kj, qcat_ref[...])

    stage_scores(0, 0)

    def past_blocks(g, carry, unroll, base):
        carry = list(carry)
        for u in range(unroll):
            j = base + g * unroll + u
            stage_scores(jnp.minimum(j + 1, n_blocks - 1), (u + 1) % 2)
            off = jnp.full((1, TQ), (i - j) * MOBA_BLOCK, jnp.int32).astype(F32) + tq
            for h in range(HEADS):
                m, accl = carry[2 * h:2 * h + 2]
                s1 = s_ref[u % 2, :, heads[h]]
                rb = selb_ref[h, pl.ds(j, 1), :] - slopes[h] * off
                m_new = jnp.maximum(m, jnp.max(s1, axis=0, keepdims=True) + rb)
                p = jnp.exp(s1 - (m_new - rb)).astype(BF16)
                carry[2 * h:2 * h + 2] = [m_new, jnp.exp(m - m_new) * accl + _dot(vt_ref[j, vrows[h], :], p)]
        return tuple(carry)

    n_main = i // KV_UNROLL
    rest = i - n_main * KV_UNROLL
    carry = lax.fori_loop(0, n_main, functools.partial(past_blocks, unroll=KV_UNROLL, base=0), tuple(init))
    final = lax.fori_loop(0, (rest + 1) // 2,
                          functools.partial(past_blocks, unroll=2, base=n_main * KV_UNROLL), carry)
    o_t = jnp.concatenate([finish(final[2 * h + 1]) for h in range(HEADS)], axis=0)
    o_moba_ref[0] = o_t.T.astype(BF16)


def _moba_mem(mq, mk, mvt, kmean, cq, memk, memvt, b, s):
    nq = s // TQ
    n_blocks = s // MOBA_BLOCK
    qspec = pl.BlockSpec((1, TQ, QK_W), lambda i, j: (i, j, 0))
    return pl.pallas_call(
        functools.partial(_moba_mem_kernel, n_blocks=n_blocks),
        grid=(b, nq),
        in_specs=[
            qspec,
            pl.BlockSpec((1, s, QK_W), lambda i, j: (i, 0, 0)),
            pl.BlockSpec((n_blocks, VT_ALL, MOBA_BLOCK), lambda i, j: (i, 0, 0)),
            pl.BlockSpec((1, n_blocks, QK_W), lambda i, j: (i, 0, 0)),
            qspec,
            pl.BlockSpec((1, N_MEM, QK_W), lambda i, j: (i, 0, 0)),
            pl.BlockSpec((1, VT_ALL, N_MEM), lambda i, j: (i, 0, 0)),
        ],
        out_specs=[qspec, qspec],
        out_shape=[jax.ShapeDtypeStruct((b, s, QK_W), BF16), jax.ShapeDtypeStruct((b, s, QK_W), BF16)],
        scratch_shapes=[
            pltpu.VMEM((HEADS * TQ, QK_W), BF16),
            pltpu.VMEM((HEADS, n_blocks, TQ), F32),
            pltpu.VMEM((2, MOBA_BLOCK, HEADS * TQ), F32),
        ],
        compiler_params=_params(("arbitrary", "arbitrary")),
        name="moba_mem",
    )(mq.reshape(b, s, QK_W), mk.reshape(b, s, QK_W), mvt, kmean.reshape(b, n_blocks, QK_W),
      cq.reshape(b, s, QK_W), memk, memvt)


OUT_TM = 512


def _out_router_kernel(x_ref, og_ref, om_ref, oc_ref, w_ref, g_ref, wr_ref, br_ref, tri_ref,
                       x1_ref, h_ref, e_ref, gate_ref, count_ref, carry_ref):
    w = w_ref[...]
    x1 = (x_ref[...] + _dot(og_ref[...], w[:GLA_V_W]) + _dot(om_ref[...], w[GLA_V_W:GLA_V_W + QK_W])
          + _dot(oc_ref[...], w[GLA_V_W + QK_W:]))
    x1_ref[...] = x1
    hn = _rms_rows(x1, g_ref[...])
    _store_row_tiles(h_ref, _pack_bf16_pairs(hn))
    logits = _dot(hn.astype(BF16), wr_ref[...]) + br_ref[...]
    lane = lax.broadcasted_iota(jnp.int32, logits.shape, 1).astype(F32)
    lg = jnp.where(lane < N_GROUPS, logits, NEG_INF)
    mg = jnp.max(lg, axis=-1, keepdims=True)
    g_sel = jnp.min(jnp.where(lg == mg, lane, float(LANES)), axis=-1, keepdims=True)
    p_group = 1.0 / jnp.sum(jnp.exp(lg - mg), axis=-1, keepdims=True)
    lo = N_GROUPS + g_sel * EXPERTS_PER_GROUP
    le = jnp.where((lane >= lo) & (lane < lo + EXPERTS_PER_GROUP), logits, NEG_INF)
    m0 = jnp.max(le, axis=-1, keepdims=True)
    i0 = jnp.min(jnp.where(le == m0, lane, float(LANES)), axis=-1, keepdims=True)
    le1 = jnp.where(lane == i0, NEG_INF, le)
    m1 = jnp.max(le1, axis=-1, keepdims=True)
    i1 = jnp.min(jnp.where(le1 == m1, lane, float(LANES)), axis=-1, keepdims=True)
    z = jnp.exp(m1 - m0)
    w0 = p_group / (1.0 + z)
    w1 = p_group * z / (1.0 + z)
    gate_ref[...] = jnp.where(lane == 0.0, w0, jnp.where(lane == 1.0, w1, 0.0))

    @pl.when(pl.program_id(0) == 0)
    def _():
        carry_ref[...] = jnp.zeros_like(carry_ref)

    oh0 = (lane == i0).astype(F32)
    oh1 = (lane == i1).astype(F32)
    both = oh0 + oh1
    before = _dot(tri_ref[...], both.astype(BF16)) + carry_ref[...]
    r0 = jnp.sum(oh0 * before, axis=-1, keepdims=True)
    r1 = jnp.sum(oh1 * before, axis=-1, keepdims=True)
    carry_ref[...] = carry_ref[...] + jnp.sum(both, axis=0, keepdims=True)
    count_ref[...] = jnp.broadcast_to(carry_ref[...], count_ref.shape).astype(jnp.int32)
    info = jnp.where(lane == 0.0, i0 - N_GROUPS, jnp.where(lane == 1.0, i1 - N_GROUPS,
                     jnp.where(lane == 2.0, r0, jnp.where(lane == 3.0, r1, 0.0))))
    e_ref[...] = info.astype(jnp.int32)


def _out_router(x2, og, om, oc, w_bf, g, wr_bf, br):
    t = x2.shape[0]
    row = lambda w: pl.BlockSpec((OUT_TM, w), lambda i: (i, 0))
    const = lambda a, b: pl.BlockSpec((a, b), lambda i: (0, 0))
    idx = np.arange(OUT_TM)
    tri = jnp.asarray(idx[:, None] > idx[None, :], BF16)
    return pl.pallas_call(
        _out_router_kernel,
        grid=(t // OUT_TM,),
        in_specs=[row(D_MODEL), row(GLA_V_W), row(QK_W), row(QK_W), const(D_MODEL, D_MODEL),
                  const(1, D_MODEL), const(D_MODEL, LANES), const(1, LANES), const(OUT_TM, OUT_TM)],
        out_specs=[row(D_MODEL), pl.BlockSpec((OUT_TM * X_TILE, LANES), lambda i: (i, 0)),
                   row(LANES), row(LANES), const(8, LANES)],
        out_shape=[
            jax.ShapeDtypeStruct((t, D_MODEL), F32),
            jax.ShapeDtypeStruct((t * X_TILE, LANES), jnp.uint32),
            jax.ShapeDtypeStruct((t, LANES), jnp.int32),
            jax.ShapeDtypeStruct((t, LANES), F32),
            jax.ShapeDtypeStruct((8, LANES), jnp.int32),
        ],
        scratch_shapes=[pltpu.VMEM((1, LANES), F32)],
        compiler_params=_params(("arbitrary",)),
        name="out_router",
    )(x2, og, om, oc, w_bf, g, wr_bf, br, tri)


DISP_TM = 512
DMA_UNROLL = 8


def _dispatch_kernel(dest_ref, h_ref, xs_in_ref, xs_ref, sem):
    del xs_in_ref

    def row_copy(r, d):
        src = h_ref.at[pl.ds(pl.multiple_of(r * X_TILE, X_TILE), X_TILE), :]
        dst = xs_ref.at[pl.ds(pl.multiple_of(d * X_TILE, X_TILE), X_TILE), :]
        return pltpu.make_async_copy(src, dst, sem)

    def issue(r, c):
        row_copy(r, dest_ref[0, 0, 2 * r]).start(priority=0)
        row_copy(r, dest_ref[0, 0, 2 * r + 1]).start(priority=1)
        return c

    lax.fori_loop(0, DISP_TM, issue, 0, unroll=DMA_UNROLL)

    def drain(r, c):
        row_copy(0, 0).wait()
        return c

    lax.fori_loop(0, 2 * DISP_TM, drain, 0, unroll=DMA_UNROLL)


def _dispatch(hn, dest, n_rows):
    t = hn.shape[0] // X_TILE
    nt = t // DISP_TM
    xs0 = jnp.zeros((n_rows * X_TILE, LANES), jnp.uint32)
    return pl.pallas_call(
        _dispatch_kernel,
        grid=(nt,),
        in_specs=[pl.BlockSpec((1, 1, 2 * DISP_TM), lambda i: (i, 0, 0), memory_space=pltpu.SMEM),
                  pl.BlockSpec((DISP_TM * X_TILE, LANES), lambda i: (i, 0)),
                  pl.BlockSpec(memory_space=pl.ANY)],
        out_specs=pl.BlockSpec(memory_space=pl.ANY),
        out_shape=jax.ShapeDtypeStruct((n_rows * X_TILE, LANES), jnp.uint32),
        scratch_shapes=[pltpu.SemaphoreType.DMA],
        input_output_aliases={2: 0},
        compiler_params=_params(("arbitrary",)),
        name="dispatch",
    )(dest.reshape(nt, 1, 2 * DISP_TM), hn, xs0)


def _experts_kernel(be_ref, nused_ref, xs_ref, wg_ref, wu_ref, wd_ref, ys_ref, wg_bf, wu_bf, wd_bf):
    i = pl.program_id(0)
    new_expert = jnp.logical_or(i == 0, be_ref[i] != be_ref[jnp.maximum(i - 1, 0)])

    @pl.when(new_expert)
    def _():
        wg_bf[...] = wg_ref[0].astype(BF16)
        wu_bf[...] = wu_ref[0].astype(BF16)
        wd_bf[...] = wd_ref[0].astype(BF16)

    @pl.when(i < nused_ref[0])
    def _():
        kh = D_MODEL // 2
        x_lo, x_hi = _unpack_bf16_pairs(_load_row_tiles(xs_ref, ROW_BLOCK))
        gate = _dot(x_lo, wg_bf[:kh, :]) + _dot(x_hi, wg_bf[kh:, :])
        up = _dot(x_lo, wu_bf[:kh, :]) + _dot(x_hi, wu_bf[kh:, :])
        hid = (_silu(gate) * up).astype(BF16)
        for s in range(2):
            y = _dot(hid, wd_bf[:, s * kh:(s + 1) * kh])
            for c in range(ROW_TILE // 2):
                ys_ref[pl.ds(s * (ROW_TILE // 2) + c, ROW_BLOCK, stride=ROW_TILE), :] = y[:, c * LANES:(c + 1) * LANES]

    @pl.when(i >= nused_ref[0])
    def _():
        ys_ref[...] = jnp.zeros_like(ys_ref)


def _experts(xs, block_expert, n_used, w_gate, w_up, w_down):
    n_rows = xs.shape[0] // X_TILE
    nb = n_rows // ROW_BLOCK
    grid_spec = pltpu.PrefetchScalarGridSpec(
        num_scalar_prefetch=2,
        grid=(nb,),
        in_specs=[
            pl.BlockSpec((ROW_BLOCK * X_TILE, LANES), lambda i, be, nu: (jnp.minimum(i, nu[0] - 1), 0)),
            pl.BlockSpec((1, D_MODEL, MOE_FF), lambda i, be, nu: (be[i], 0, 0)),
            pl.BlockSpec((1, D_MODEL, MOE_FF), lambda i, be, nu: (be[i], 0, 0)),
            pl.BlockSpec((1, MOE_FF, D_MODEL), lambda i, be, nu: (be[i], 0, 0)),
        ],
        out_specs=pl.BlockSpec((ROW_BLOCK * ROW_TILE, LANES), lambda i, be, nu: (i, 0)),
        scratch_shapes=[pltpu.VMEM((D_MODEL, MOE_FF), BF16), pltpu.VMEM((D_MODEL, MOE_FF), BF16),
                        pltpu.VMEM((MOE_FF, D_MODEL), BF16)],
    )
    return pl.pallas_call(
        _experts_kernel,
        grid_spec=grid_spec,
        out_shape=jax.ShapeDtypeStruct((n_rows * ROW_TILE, LANES), F32),
        compiler_params=_params(("arbitrary",)),
        name="experts",
    )(block_expert, n_used, xs, w_gate, w_up, w_down)


COMB_TM = 256


def _combine_kernel(dest_ref, dest_next_ref, x1_ref, gate_ref, ys_ref, out_ref, ybuf, sems):
    g = pl.program_id(0)

    def row_copy(d, slot, k, r):
        src = ys_ref.at[pl.ds(pl.multiple_of(d * ROW_TILE, ROW_TILE), ROW_TILE), :]
        dst = ybuf.at[slot, k, pl.ds(pl.multiple_of(r * ROW_TILE, ROW_TILE), ROW_TILE), :]
        return pltpu.make_async_copy(src, dst, sems.at[slot])

    def gather(dref, base, slot):
        def issue(r, c):
            row_copy(dref[0, 0, base + 2 * r], slot, 0, r).start(priority=0)
            row_copy(dref[0, 0, base + 2 * r + 1], slot, 1, r).start(priority=1)
            return c
        lax.fori_loop(0, COMB_TM, issue, 0, unroll=DMA_UNROLL)

    def finish(slot):
        def drain(r, c):
            row_copy(0, slot, 0, 0).wait()
            return c
        lax.fori_loop(0, 2 * COMB_TM, drain, 0, unroll=DMA_UNROLL)
        rows = slice(slot * COMB_TM, (slot + 1) * COMB_TM)
        gate = gate_ref[rows, :]
        lane = lax.broadcasted_iota(jnp.int32, gate.shape, 1)
        w0 = jnp.sum(jnp.where(lane == 0, gate, 0.0), axis=-1, keepdims=True)
        w1 = jnp.sum(jnp.where(lane == 1, gate, 0.0), axis=-1, keepdims=True)
        y0 = _load_row_tiles(ybuf.at[slot, 0], COMB_TM)
        y1 = _load_row_tiles(ybuf.at[slot, 1], COMB_TM)
        out_ref[rows, :] = x1_ref[rows, :] + (y0 * w0 + y1 * w1)

    @pl.when(g == 0)
    def _():
        gather(dest_ref, 0, 0)

    gather(dest_ref, 2 * COMB_TM, 1)
    finish(0)

    @pl.when(g + 1 < pl.num_programs(0))
    def _():
        gather(dest_next_ref, 0, 0)

    finish(1)


def _combine(x1, gates, ys, dest):
    t = x1.shape[0]
    ng = t // (2 * COMB_TM)
    dest3 = dest.reshape(ng, 1, 4 * COMB_TM)
    smem = lambda f: pl.BlockSpec((1, 1, 4 * COMB_TM), f, memory_space=pltpu.SMEM)
    return pl.pallas_call(
        _combine_kernel,
        grid=(ng,),
        in_specs=[smem(lambda i: (i, 0, 0)),
                  smem(lambda i: (jnp.minimum(i + 1, ng - 1), 0, 0)),
                  pl.BlockSpec((2 * COMB_TM, D_MODEL), lambda i: (i, 0)),
                  pl.BlockSpec((2 * COMB_TM, LANES), lambda i: (i, 0)),
                  pl.BlockSpec(memory_space=pl.ANY)],
        out_specs=pl.BlockSpec((2 * COMB_TM, D_MODEL), lambda i: (i, 0)),
        out_shape=jax.ShapeDtypeStruct((t, D_MODEL), F32),
        scratch_shapes=[pltpu.VMEM((2, 2, COMB_TM * ROW_TILE, LANES), F32),
                        pltpu.SemaphoreType.DMA((2,))],
        compiler_params=_params(("arbitrary",)),
        name="combine",
    )(dest3, dest3, x1, gates, ys)


def _layer(x, mem, attn_norm_g, mem_norm_g, w_in, w_gla_gk, b_gla_gk, gla_out_norm_g,
           moba_q_norm_g, moba_k_norm_g, w_mem_kv, mem_q_norm_g, mem_k_norm_g, w_out,
           ffn_norm_g, w_router_group, b_router_group, w_router_expert, b_router_expert,
           w_gate, w_up, w_down):
    b, s, d = x.shape
    t = b * s
    x2 = x.reshape(t, d)
    row = lambda v: v.reshape(1, -1).astype(F32)
    tile_heads = lambda v: jnp.tile(v.astype(F32), HEADS).reshape(1, QK_W)
    hid = np.arange(QK_W) // DH
    seg = jnp.asarray((hid[:, None] == hid[None, :]) / DH, BF16)

    w_in_p = jnp.concatenate([w_in[:, :1536], w_in[:, 1552:], w_in[:, 1536:1552]], axis=1).astype(BF16)
    wr = jnp.concatenate([w_router_group,
                          jnp.transpose(w_router_expert, (1, 0, 2)).reshape(d, N_EXPERTS),
                          jnp.zeros((d, LANES - N_GROUPS - N_EXPERTS), F32)], axis=1).astype(BF16)
    br = jnp.concatenate([b_router_group, b_router_expert.reshape(N_EXPERTS),
                          jnp.zeros((LANES - N_GROUPS - N_EXPERTS,), F32)]).reshape(1, LANES)

    memk, memvt = _mem_kv(mem, row(mem_norm_g), w_mem_kv.astype(BF16), tile_heads(mem_k_norm_g), seg)
    qk, gv, gr, glr, mq, mk, mvt, kmean, cq = _in_proj(
        x2, row(attn_norm_g), w_in_p, seg, tile_heads(moba_q_norm_g), tile_heads(moba_k_norm_g),
        tile_heads(mem_q_norm_g))
    o_gla = _gla(qk, gv, gr, glr, w_gla_gk.astype(BF16), row(b_gla_gk), row(gla_out_norm_g), b, s)
    o_moba, o_mem = _moba_mem(mq, mk, mvt, kmean, cq, memk, memvt, b, s)
    x1, hn, e_ids, gates, counts = _out_router(
        x2, o_gla.reshape(t, GLA_V_W), o_moba.reshape(t, QK_W), o_mem.reshape(t, QK_W),
        w_out.astype(BF16), row(ffn_norm_g), wr, br)

    counts = counts[0, N_GROUPS:N_GROUPS + N_EXPERTS]
    padded = (counts + ROW_BLOCK - 1) // ROW_BLOCK * ROW_BLOCK
    pends = jnp.cumsum(padded)
    pstarts = pends - padded
    onehot = e_ids[:, :2, None] == jnp.arange(N_EXPERTS, dtype=jnp.int32)
    dest = (jnp.sum(jnp.where(onehot, pstarts, 0), axis=-1) + e_ids[:, 2:4]).astype(jnp.int32).reshape(-1)
    n_rows = (t * 2 + N_EXPERTS * (ROW_BLOCK - 1) + ROW_BLOCK - 1) // ROW_BLOCK * ROW_BLOCK
    nb = n_rows // ROW_BLOCK
    block_start = jnp.arange(nb, dtype=jnp.int32) * ROW_BLOCK
    block_expert = jnp.minimum(jnp.sum(block_start[:, None] >= pends[None, :], axis=1),
                               N_EXPERTS - 1).astype(jnp.int32)
    n_used = (pends[-1] // ROW_BLOCK).astype(jnp.int32).reshape(1)

    xs = _dispatch(hn, dest, n_rows)
    ys = _experts(xs, block_expert, n_used, w_gate, w_up, w_down)
    out = _combine(x1, gates, ys, dest)
    return out.reshape(b, s, d)


def kernel(x, mem, attn_norm_g, mem_norm_g, w_in, w_gla_gk, b_gla_gk, gla_out_norm_g, moba_q_norm_g, moba_k_norm_g, w_mem_kv, mem_q_norm_g, mem_k_norm_g, w_out, ffn_norm_g, w_router_group, b_router_group, w_router_expert, b_router_expert, w_gate, w_up, w_down):
    depth = w_in.shape[0]
    for l in range(depth):
        x = _layer(x, mem, attn_norm_g[l], mem_norm_g[l], w_in[l], w_gla_gk[l], b_gla_gk[l],
                   gla_out_norm_g[l], moba_q_norm_g[l], moba_k_norm_g[l], w_mem_kv[l],
                   mem_q_norm_g[l], mem_k_norm_g[l], w_out[l], ffn_norm_g[l], w_router_group[l],
                   b_router_group[l], w_router_expert[l], b_router_expert[l],
                   w_gate[l], w_up[l], w_down[l])
    return x
```

```python
import functools

import jax
import jax.numpy as jnp
import numpy as np
from jax import lax
from jax.experimental import pallas as pl
from jax.experimental.pallas import tpu as pltpu

F32 = jnp.float32
BF16 = jnp.bfloat16
EPS = 1e-6
NEG_INF = float("-inf")

D_MODEL = 1024
N_MEM = 256
HEADS = 4
DH = 64
GLA_DV = 128
GLA_RANK = 16
GLA_GATE_NORMALIZER = 16.0
GLA_CHUNK = 64
MOBA_BLOCK = 256
MOBA_TOPK = 3
QK_W = HEADS * DH
GLA_V_W = HEADS * GLA_DV
N_GROUPS = 4
EXPERTS_PER_GROUP = 8
N_EXPERTS = N_GROUPS * EXPERTS_PER_GROUP
MOE_FF = 512
LANES = 128
ROW_BLOCK = 256

VMEM_LIMIT = 56 * 1024 * 1024


def _params(sem):
    return pltpu.CompilerParams(dimension_semantics=sem, vmem_limit_bytes=VMEM_LIMIT)


def _nt(a, b):
    return lax.dot_general(a, b, (((1,), (1,)), ((), ())), preferred_element_type=F32)


def _tn(a, b):
    return lax.dot_general(a, b, (((0,), (0,)), ((), ())), preferred_element_type=F32)


def _dot(a, b):
    return jnp.dot(a, b, preferred_element_type=F32)


def _rms_rows(x, g):
    ms = jnp.mean(x * x, axis=-1, keepdims=True)
    return x * lax.rsqrt(ms + EPS) * g


def _split3(x):
    h1 = x.astype(BF16)
    r1 = x - h1.astype(F32)
    h2 = r1.astype(BF16)
    h3 = (r1 - h2.astype(F32)).astype(BF16)
    return h1, h2, h3


def _head_mean_sq(x, seg):
    sq = x * x
    hi = sq.astype(BF16)
    lo = (sq - hi.astype(F32)).astype(BF16)
    return _dot(hi, seg) + _dot(lo, seg)


def _silu(x):
    return x * (1.0 / (1.0 + jnp.exp(-x)))


VT_ROWS = DH + 16
VT_ALL = HEADS * VT_ROWS


ALIBI_SLOPES = tuple(2.0 ** (-8.0 * (h + 1) / HEADS) for h in range(HEADS))


def _vt_with_ones(v, key_slopes=None):
    n = v.shape[0]
    vt = v.T
    ones = jnp.ones((VT_ROWS - DH, n), F32)
    pos = lax.broadcasted_iota(jnp.int32, (1, n), 1).astype(F32)
    parts = []
    for h in range(HEADS):
        scale = 1.0 if key_slopes is None else jnp.exp(key_slopes[h] * pos)
        parts += [vt[h * DH:(h + 1) * DH] * scale, ones * scale]
    return jnp.concatenate(parts, axis=0).astype(BF16)


ROW_TILE = D_MODEL // LANES
X_TILE = ROW_TILE // 2


def _store_row_tiles(ref, x):
    n, tiles = x.shape[0], x.shape[1] // LANES
    for c in range(tiles):
        ref[pl.ds(c, n, stride=tiles), :] = x[:, c * LANES:(c + 1) * LANES]


def _load_row_tiles(ref, n):
    tiles = ref.shape[0] // n
    return jnp.concatenate([ref[pl.ds(c, n, stride=tiles), :] for c in range(tiles)], axis=1)


def _pack_bf16_pairs(x):
    w = x.shape[1] // 2
    bits = lambda t: lax.bitcast_convert_type(t.astype(BF16).astype(F32), jnp.uint32)
    return (bits(x[:, w:]) & jnp.uint32(0xFFFF0000)) | (bits(x[:, :w]) >> 16)


def _unpack_bf16_pairs(words):
    lo = lax.bitcast_convert_type(words << 16, F32).astype(BF16)
    hi = lax.bitcast_convert_type(words & jnp.uint32(0xFFFF0000), F32).astype(BF16)
    return lo, hi


def _mem_kv_kernel(mem_ref, g_ref, w_ref, gk_ref, seg_ref, k_ref, vt_ref):
    h = _rms_rows(mem_ref[0], g_ref[...]).astype(BF16)
    kv = _dot(h, w_ref[...])
    k = kv[:, :QK_W]
    kn = k * lax.rsqrt(_head_mean_sq(k, seg_ref[...]) + EPS) * gk_ref[...]
    k_ref[0] = kn.astype(BF16)
    vt_ref[0] = _vt_with_ones(kv[:, QK_W:])


def _mem_kv(mem, g, w_bf, gk_t, seg):
    b = mem.shape[0]
    return pl.pallas_call(
        _mem_kv_kernel,
        grid=(b,),
        in_specs=[
            pl.BlockSpec((1, N_MEM, D_MODEL), lambda i: (i, 0, 0)),
            pl.BlockSpec((1, D_MODEL), lambda i: (0, 0)),
            pl.BlockSpec((D_MODEL, 2 * QK_W), lambda i: (0, 0)),
            pl.BlockSpec((1, QK_W), lambda i: (0, 0)),
            pl.BlockSpec((QK_W, QK_W), lambda i: (0, 0)),
        ],
        out_specs=[
            pl.BlockSpec((1, N_MEM, QK_W), lambda i: (i, 0, 0)),
            pl.BlockSpec((1, VT_ALL, N_MEM), lambda i: (i, 0, 0)),
        ],
        out_shape=[
            jax.ShapeDtypeStruct((b, N_MEM, QK_W), BF16),
            jax.ShapeDtypeStruct((b, VT_ALL, N_MEM), BF16),
        ],
        compiler_params=_params(("arbitrary",)),
        name="mem_kv",
    )(mem, g, w_bf, gk_t, seg)


IN_TM = 512
_C_QK, _C_V, _C_R, _C_MQ, _C_MK, _C_MV, _C_CQ, _C_LR = 0, 512, 1024, 1536, 1792, 2048, 2304, 2560
D_IN = 2576


def _in_proj_kernel(x_ref, g_ref, w_ref, seg_ref, gq_ref, gk_ref, gc_ref,
                    qk_ref, v_ref, r_ref, lr_ref, mq_ref, mk_ref, mvt_ref, kmean_ref, cq_ref):
    h = _rms_rows(x_ref[...], g_ref[...]).astype(BF16)
    p = _dot(h, w_ref[...])
    seg = seg_ref[...]
    qk_ref[...] = p[:, _C_QK:_C_V]
    v_ref[...] = p[:, _C_V:_C_R].astype(BF16)
    r_ref[...] = p[:, _C_R:_C_MQ]
    lr_ref[...] = p[:, _C_LR:D_IN]

    def head_norm(t, gain):
        return t * lax.rsqrt(_head_mean_sq(t, seg) + EPS) * gain

    scale = DH ** -0.5
    mq_ref[...] = (head_norm(p[:, _C_MQ:_C_MK], gq_ref[...]) * scale).astype(BF16)
    cq_ref[...] = (head_norm(p[:, _C_CQ:_C_LR], gc_ref[...]) * scale).astype(BF16)
    kn = head_norm(p[:, _C_MK:_C_MV], gk_ref[...])
    mk_ref[...] = kn.astype(BF16)
    mv = p[:, _C_MV:_C_CQ]
    for j in range(IN_TM // MOBA_BLOCK):
        rows = slice(j * MOBA_BLOCK, (j + 1) * MOBA_BLOCK)
        kmean_ref[0, j:j + 1, :] = jnp.mean(kn[rows], axis=0, keepdims=True)
        mvt_ref[j] = _vt_with_ones(mv[rows], ALIBI_SLOPES)


def _in_proj(x2, g, w_bf, seg, gq_t, gk_t, gc_t):
    t = x2.shape[0]
    nt = t // IN_TM
    nb = IN_TM // MOBA_BLOCK
    row = lambda w: pl.BlockSpec((IN_TM, w), lambda i: (i, 0))
    const = lambda a, b: pl.BlockSpec((a, b), lambda i: (0, 0))
    return pl.pallas_call(
        _in_proj_kernel,
        grid=(nt,),
        in_specs=[row(D_MODEL), const(1, D_MODEL), const(D_MODEL, D_IN), const(QK_W, QK_W),
                  const(1, QK_W), const(1, QK_W), const(1, QK_W)],
        out_specs=[row(2 * QK_W), row(GLA_V_W), row(GLA_V_W), row(GLA_RANK), row(QK_W), row(QK_W),
                   pl.BlockSpec((nb, VT_ALL, MOBA_BLOCK), lambda i: (i, 0, 0)),
                   pl.BlockSpec((1, nb, QK_W), lambda i: (i, 0, 0)),
                   row(QK_W)],
        out_shape=[
            jax.ShapeDtypeStruct((t, 2 * QK_W), F32),
            jax.ShapeDtypeStruct((t, GLA_V_W), BF16),
            jax.ShapeDtypeStruct((t, GLA_V_W), F32),
            jax.ShapeDtypeStruct((t, GLA_RANK), F32),
            jax.ShapeDtypeStruct((t, QK_W), BF16),
            jax.ShapeDtypeStruct((t, QK_W), BF16),
            jax.ShapeDtypeStruct((t // MOBA_BLOCK, VT_ALL, MOBA_BLOCK), BF16),
            jax.ShapeDtypeStruct((nt, nb, QK_W), F32),
            jax.ShapeDtypeStruct((t, QK_W), BF16),
        ],
        compiler_params=_params(("arbitrary",)),
        name="in_proj",
    )(x2, g, w_bf, seg, gq_t, gk_t, gc_t)


GLA_TC = 256


def _gla_kernel(qk_ref, v_ref, r_ref, lr_ref, wgk_ref, bgk_ref, gn_ref, tri_ref, ones_ref,
                o_ref, st_ref):
    @pl.when(pl.program_id(1) == 0)
    def _():
        st_ref[...] = jnp.zeros_like(st_ref)

    qk = qk_ref[0]
    q = qk[:, :QK_W]
    k = qk[:, QK_W:]
    gk = _dot(lr_ref[0].astype(BF16), wgk_ref[...]) + bgk_ref[...]
    g = -(jnp.maximum(-gk, 0.0) + jnp.log1p(jnp.exp(-jnp.abs(gk)))) / GLA_GATE_NORMALIZER
    g1, g2, g3 = _split3(g)
    tri = tri_ref[...]
    ones = ones_ref[...]
    cum = _dot(tri, g1) + _dot(tri, g2) + _dot(tri, g3)
    tot = _dot(ones, g1) + _dot(ones, g2) + _dot(ones, g3)
    q_dec = (q * (DH ** -0.5) * jnp.exp(cum)).astype(BF16)
    k_inv = (k * jnp.exp(-cum)).astype(BF16)
    k_end = (k * jnp.exp(tot - cum)).astype(BF16)
    decay = jnp.exp(tot)

    lane_head = lax.broadcasted_iota(jnp.int32, (GLA_TC, QK_W), 1) // DH
    row_t = lax.broadcasted_iota(jnp.int32, (GLA_TC, GLA_TC), 0)
    col_t = lax.broadcasted_iota(jnp.int32, (GLA_TC, GLA_TC), 1)
    causal = (row_t >= col_t) & (row_t // GLA_CHUNK == col_t // GLA_CHUNK)
    same_head = (lax.broadcasted_iota(jnp.int32, (GLA_V_W, QK_W), 0) // GLA_DV
                 == lax.broadcasted_iota(jnp.int32, (GLA_V_W, QK_W), 1) // DH)
    gain = gn_ref[...]

    v = v_ref[0]
    chunks = [slice(c * GLA_CHUNK, (c + 1) * GLA_CHUNK) for c in range(GLA_TC // GLA_CHUNK)]
    d_st = [jnp.where(same_head, _tn(v[rows], k_end[rows]), 0.0) for rows in chunks]
    states = [st_ref[...]]
    for c, rows in enumerate(chunks):
        states.append(states[c] * decay[rows.start:rows.start + 1, :] + d_st[c])
    st_ref[...] = states[-1]
    o_inter = jnp.concatenate([_nt(q_dec[rows], states[c].astype(BF16)) for c, rows in enumerate(chunks)],
                              axis=0)
    q_stack = jnp.concatenate([jnp.where(lane_head == h, q_dec, jnp.zeros_like(q_dec)) for h in range(HEADS)],
                              axis=0)
    a_all = _nt(q_stack, k_inv)
    outs = []
    for h in range(HEADS):
        a = jnp.where(causal, a_all[h * GLA_TC:(h + 1) * GLA_TC], 0.0).astype(BF16)
        vs = slice(h * GLA_DV, (h + 1) * GLA_DV)
        oh = _dot(a, v[:, vs]) + o_inter[:, vs]
        outs.append(_rms_rows(oh, gain) * _silu(r_ref[0, :, vs]))
    o_ref[0] = jnp.concatenate(outs, axis=-1).astype(BF16)


def _gla(qk, v, r, lr, wgk_bf, bgk, gn, b, s):
    ns = s // GLA_TC
    idx = np.arange(GLA_TC)
    same_chunk = (idx[:, None] // GLA_CHUNK) == (idx[None, :] // GLA_CHUNK)
    tri = jnp.asarray(same_chunk & (idx[:, None] >= idx[None, :]), BF16)
    ones = jnp.asarray(same_chunk, BF16)
    seq = lambda w: pl.BlockSpec((1, GLA_TC, w), lambda i, j: (i, j, 0))
    const = lambda a, c: pl.BlockSpec((a, c), lambda i, j: (0, 0))
    return pl.pallas_call(
        _gla_kernel,
        grid=(b, ns),
        in_specs=[seq(2 * QK_W), seq(GLA_V_W), seq(GLA_V_W), seq(GLA_RANK),
                  const(GLA_RANK, QK_W), const(1, QK_W), const(1, GLA_DV),
                  const(GLA_TC, GLA_TC), const(GLA_TC, GLA_TC)],
        out_specs=seq(GLA_V_W),
        out_shape=jax.ShapeDtypeStruct((b, s, GLA_V_W), BF16),
        scratch_shapes=[pltpu.VMEM((GLA_V_W, QK_W), F32)],
        compiler_params=_params(("arbitrary", "arbitrary")),
        name="gla",
    )(qk.reshape(b, s, 2 * QK_W), v.reshape(b, s, GLA_V_W), r.reshape(b, s, GLA_V_W),
      lr.reshape(b, s, GLA_RANK), wgk_bf, bgk, gn, tri, ones)


TQ = MOBA_BLOCK
KV_UNROLL = 4


def _moba_mem_kernel(q_ref, k_ref, vt_ref, kmean_ref, cq_ref, mk_ref, mvt_ref,
                     o_moba_ref, o_mem_ref,
                     qcat_ref, selb_ref, s_ref, *, n_blocks):
    i = pl.program_id(1)
    lane_head = lax.broadcasted_iota(jnp.int32, (TQ, QK_W), 1) // DH
    dist0 = (lax.broadcasted_iota(jnp.int32, (MOBA_BLOCK, TQ), 1)
             - lax.broadcasted_iota(jnp.int32, (MOBA_BLOCK, TQ), 0)).astype(F32)
    slopes = ALIBI_SLOPES
    heads = [slice(h * TQ, (h + 1) * TQ) for h in range(HEADS)]
    vrows = [slice(h * VT_ROWS, (h + 1) * VT_ROWS) for h in range(HEADS)]

    def stack_heads(x):
        return jnp.concatenate([jnp.where(lane_head == h, x, jnp.zeros_like(x)) for h in range(HEADS)], axis=0)

    def finish(accl):
        return accl[:DH] * (1.0 / accl[DH:DH + 1])

    s_all = _nt(mk_ref[0], stack_heads(cq_ref[0]))
    outs = []
    for h in range(HEADS):
        s = s_all[:, heads[h]]
        p = jnp.exp(s - jnp.max(s, axis=0, keepdims=True)).astype(BF16)
        outs.append(finish(_dot(mvt_ref[0, vrows[h], :], p)))
    o_mem_ref[0] = jnp.concatenate(outs, axis=0).T.astype(BF16)

    qcat = stack_heads(q_ref[0])
    qcat_ref[...] = qcat
    k_own = k_ref[0, pl.ds(pl.multiple_of(i * MOBA_BLOCK, MOBA_BLOCK), MOBA_BLOCK), :]
    gate_all = _nt(kmean_ref[0].astype(BF16), qcat)
    s_all = _nt(k_own, qcat)
    blk = lax.broadcasted_iota(jnp.int32, (n_blocks, TQ), 0)
    blk_f = blk.astype(F32)

    tq = lax.broadcasted_iota(jnp.int32, (1, TQ), 1).astype(F32)
    init = []
    for h in range(HEADS):
        gate = jnp.where(blk < i, gate_all[:, heads[h]], NEG_INF)
        chosen = jnp.zeros((n_blocks, TQ), jnp.bool_)
        for r in range(MOBA_TOPK):
            mx = jnp.max(gate, axis=0, keepdims=True)
            first = jnp.min(jnp.where(gate == mx, blk_f, float(n_blocks)), axis=0, keepdims=True)
            hit = blk_f == first
            chosen = chosen | (hit & (mx > NEG_INF))
            gate = jnp.where(hit, NEG_INF, gate)
        selb_ref[h] = jnp.where(chosen, 0.0, NEG_INF)
        aq = -slopes[h] * tq
        s = jnp.where(dist0 >= 0.0, s_all[:, heads[h]], NEG_INF)
        m0 = jnp.max(s, axis=0, keepdims=True) + aq
        p = jnp.exp(s - (m0 - aq)).astype(BF16)
        init += [m0, _dot(vt_ref[i, vrows[h], :], p)]

    def stage_scores(j, slot):
        kj = k_ref[0, pl.ds(pl.multiple_of(j * MOBA_BLOCK, MOBA_BLOCK), MOBA_BLOCK), :]
        s_ref[slot] = _nt(kj, qcat_ref[...])

    stage_scores(0, 0)

    def past_blocks(g, carry, unroll, base):
        carry = list(carry)
        for u in range(unroll):
            j = base + g * unroll + u
            stage_scores(jnp.minimum(j + 1, n_blocks - 1), (u + 1) % 2)
            off = jnp.full((1, TQ), (i - j) * MOBA_BLOCK, jnp.int32).astype(F32) + tq
            for h in range(HEADS):
                m, accl = carry[2 * h:2 * h + 2]
                s1 = s_ref[u % 2, :, heads[h]]
                rb = selb_ref[h, pl.ds(j, 1), :] - slopes[h] * off
                m_new = jnp.maximum(m, jnp.max(s1, axis=0, keepdims=True) + rb)
                p = jnp.exp(s1 - (m_new - rb)).astype(BF16)
                carry[2 * h:2 * h + 2] = [m_new, jnp.exp(m - m_new) * accl + _dot(vt_ref[j, vrows[h], :], p)]
        return tuple(carry)

    n_main = i // KV_UNROLL
    rest = i - n_main * KV_UNROLL
    carry = lax.fori_loop(0, n_main, functools.partial(past_blocks, unroll=KV_UNROLL, base=0), tuple(init))
    final = lax.fori_loop(0, (rest + 1) // 2,
                          functools.partial(past_blocks, unroll=2, base=n_main * KV_UNROLL), carry)
    o_t = jnp.concatenate([finish(final[2 * h + 1]) for h in range(HEADS)], axis=0)
    o_moba_ref[0] = o_t.T.astype(BF16)


def _moba_mem(mq, mk, mvt, kmean, cq, memk, memvt, b, s):
    nq = s // TQ
    n_blocks = s // MOBA_BLOCK
    qspec = pl.BlockSpec((1, TQ, QK_W), lambda i, j: (i, j, 0))
    return pl.pallas_call(
        functools.partial(_moba_mem_kernel, n_blocks=n_blocks),
        grid=(b, nq),
        in_specs=[
            qspec,
            pl.BlockSpec((1, s, QK_W), lambda i, j: (i, 0, 0)),
            pl.BlockSpec((n_blocks, VT_ALL, MOBA_BLOCK), lambda i, j: (i, 0, 0)),
            pl.BlockSpec((1, n_blocks, QK_W), lambda i, j: (i, 0, 0)),
            qspec,
            pl.BlockSpec((1, N_MEM, QK_W), lambda i, j: (i, 0, 0)),
            pl.BlockSpec((1, VT_ALL, N_MEM), lambda i, j: (i, 0, 0)),
        ],
        out_specs=[qspec, qspec],
        out_shape=[jax.ShapeDtypeStruct((b, s, QK_W), BF16), jax.ShapeDtypeStruct((b, s, QK_W), BF16)],
        scratch_shapes=[
            pltpu.VMEM((HEADS * TQ, QK_W), BF16),
            pltpu.VMEM((HEADS, n_blocks, TQ), F32),
            pltpu.VMEM((2, MOBA_BLOCK, HEADS * TQ), F32),
        ],
        compiler_params=_params(("arbitrary", "arbitrary")),
        name="moba_mem",
    )(mq.reshape(b, s, QK_W), mk.reshape(b, s, QK_W), mvt, kmean.reshape(b, n_blocks, QK_W),
      cq.reshape(b, s, QK_W), memk, memvt)


OUT_TM = 512


def _out_router_kernel(x_ref, og_ref, om_ref, oc_ref, w_ref, g_ref, wr_ref, br_ref, tri_ref,
                       x1_ref, h_ref, e_ref, gate_ref, count_ref, carry_ref):
    w = w_ref[...]
    x1 = (x_ref[...] + _dot(og_ref[...], w[:GLA_V_W]) + _dot(om_ref[...], w[GLA_V_W:GLA_V_W + QK_W])
          + _dot(oc_ref[...], w[GLA_V_W + QK_W:]))
    x1_ref[...] = x1
    hn = _rms_rows(x1, g_ref[...])
    _store_row_tiles(h_ref, _pack_bf16_pairs(hn))
    logits = _dot(hn.astype(BF16), wr_ref[...]) + br_ref[...]
    lane = lax.broadcasted_iota(jnp.int32, logits.shape, 1).astype(F32)
    lg = jnp.where(lane < N_GROUPS, logits, NEG_INF)
    mg = jnp.max(lg, axis=-1, keepdims=True)
    g_sel = jnp.min(jnp.where(lg == mg, lane, float(LANES)), axis=-1, keepdims=True)
    p_group = 1.0 / jnp.sum(jnp.exp(lg - mg), axis=-1, keepdims=True)
    lo = N_GROUPS + g_sel * EXPERTS_PER_GROUP
    le = jnp.where((lane >= lo) & (lane < lo + EXPERTS_PER_GROUP), logits, NEG_INF)
    m0 = jnp.max(le, axis=-1, keepdims=True)
    i0 = jnp.min(jnp.where(le == m0, lane, float(LANES)), axis=-1, keepdims=True)
    le1 = jnp.where(lane == i0, NEG_INF, le)
    m1 = jnp.max(le1, axis=-1, keepdims=True)
    i1 = jnp.min(jnp.where(le1 == m1, lane, float(LANES)), axis=-1, keepdims=True)
    z = jnp.exp(m1 - m0)
    w0 = p_group / (1.0 + z)
    w1 = p_group * z / (1.0 + z)
    gate_ref[...] = jnp.where(lane == 0.0, w0, jnp.where(lane == 1.0, w1, 0.0))

    @pl.when(pl.program_id(0) == 0)
    def _():
        carry_ref[...] = jnp.zeros_like(carry_ref)

    oh0 = (lane == i0).astype(F32)
    oh1 = (lane == i1).astype(F32)
    both = oh0 + oh1
    before = _dot(tri_ref[...], both.astype(BF16)) + carry_ref[...]
    r0 = jnp.sum(oh0 * before, axis=-1, keepdims=True)
    r1 = jnp.sum(oh1 * before, axis=-1, keepdims=True)
    carry_ref[...] = carry_ref[...] + jnp.sum(both, axis=0, keepdims=True)
    count_ref[...] = jnp.broadcast_to(carry_ref[...], count_ref.shape).astype(jnp.int32)
    info = jnp.where(lane == 0.0, i0 - N_GROUPS, jnp.where(lane == 1.0, i1 - N_GROUPS,
                     jnp.where(lane == 2.0, r0, jnp.where(lane == 3.0, r1, 0.0))))
    e_ref[...] = info.astype(jnp.int32)


def _out_router(x2, og, om, oc, w_bf, g, wr_bf, br):
    t = x2.shape[0]
    row = lambda w: pl.BlockSpec((OUT_TM, w), lambda i: (i, 0))
    const = lambda a, b: pl.BlockSpec((a, b), lambda i: (0, 0))
    idx = np.arange(OUT_TM)
    tri = jnp.asarray(idx[:, None] > idx[None, :], BF16)
    return pl.pallas_call(
        _out_router_kernel,
        grid=(t // OUT_TM,),
        in_specs=[row(D_MODEL), row(GLA_V_W), row(QK_W), row(QK_W), const(D_MODEL, D_MODEL),
                  const(1, D_MODEL), const(D_MODEL, LANES), const(1, LANES), const(OUT_TM, OUT_TM)],
        out_specs=[row(D_MODEL), pl.BlockSpec((OUT_TM * X_TILE, LANES), lambda i: (i, 0)),
                   row(LANES), row(LANES), const(8, LANES)],
        out_shape=[
            jax.ShapeDtypeStruct((t, D_MODEL), F32),
            jax.ShapeDtypeStruct((t * X_TILE, LANES), jnp.uint32),
            jax.ShapeDtypeStruct((t, LANES), jnp.int32),
            jax.ShapeDtypeStruct((t, LANES), F32),
            jax.ShapeDtypeStruct((8, LANES), jnp.int32),
        ],
        scratch_shapes=[pltpu.VMEM((1, LANES), F32)],
        compiler_params=_params(("arbitrary",)),
        name="out_router",
    )(x2, og, om, oc, w_bf, g, wr_bf, br, tri)


DMA_UNROLL = 8


def _experts_kernel(be_ref, nused_ref, tok_ref, tok_next_ref, hn_hbm, wg_ref, wu_ref, wd_ref, ys_ref,
                    hn_vmem, xg_even, xg_odd, wg_bf, wu_bf, wd_bf, sem):
    i = pl.program_id(0)

    def gather(tref, dst):
        for r in range(ROW_BLOCK):
            src = pl.ds(pl.multiple_of(tref[0, 0, r] * X_TILE, X_TILE), X_TILE)
            dst[r * X_TILE:(r + 1) * X_TILE, :] = hn_vmem[src, :]

    @pl.when(i == 0)
    def _():
        load = pltpu.make_async_copy(hn_hbm, hn_vmem, sem)
        load.start()
        load.wait()
        gather(tok_ref, xg_even)

    new_expert = jnp.logical_or(i == 0, be_ref[i] != be_ref[jnp.maximum(i - 1, 0)])

    @pl.when(new_expert)
    def _():
        wg_bf[...] = wg_ref[0].astype(BF16)
        wu_bf[...] = wu_ref[0].astype(BF16)
        wd_bf[...] = wd_ref[0].astype(BF16)

    def block(cur, nxt):
        kh = D_MODEL // 2
        gather(tok_next_ref, nxt)
        x_lo, x_hi = _unpack_bf16_pairs(_load_row_tiles(cur, ROW_BLOCK))
        gate = _dot(x_lo, wg_bf[:kh, :]) + _dot(x_hi, wg_bf[kh:, :])
        up = _dot(x_lo, wu_bf[:kh, :]) + _dot(x_hi, wu_bf[kh:, :])
        hid = (_silu(gate) * up).astype(BF16)
        for s in range(2):
            y = _dot(hid, wd_bf[:, s * kh:(s + 1) * kh])
            for c in range(ROW_TILE // 2):
                ys_ref[pl.ds(s * (ROW_TILE // 2) + c, ROW_BLOCK, stride=ROW_TILE), :] = y[:, c * LANES:(c + 1) * LANES]

    used = i < nused_ref[0]
    odd = jnp.bitwise_and(i, 1) == 1
    pl.when(jnp.logical_and(used, jnp.logical_not(odd)))(lambda: block(xg_even, xg_odd))
    pl.when(jnp.logical_and(used, odd))(lambda: block(xg_odd, xg_even))

    @pl.when(i >= nused_ref[0])
    def _():
        ys_ref[...] = jnp.zeros_like(ys_ref)


def _experts(hn, tok_of_row, block_expert, n_used, w_gate, w_up, w_down):
    n_rows = tok_of_row.shape[0]
    nb = n_rows // ROW_BLOCK
    tok3 = tok_of_row.reshape(nb, 1, ROW_BLOCK)
    smem = lambda f: pl.BlockSpec((1, 1, ROW_BLOCK), f, memory_space=pltpu.SMEM)
    grid_spec = pltpu.PrefetchScalarGridSpec(
        num_scalar_prefetch=2,
        grid=(nb,),
        in_specs=[
            smem(lambda i, be, nu: (i, 0, 0)),
            smem(lambda i, be, nu: (jnp.minimum(i + 1, nb - 1), 0, 0)),
            pl.BlockSpec(memory_space=pl.ANY),
            pl.BlockSpec((1, D_MODEL, MOE_FF), lambda i, be, nu: (be[i], 0, 0)),
            pl.BlockSpec((1, D_MODEL, MOE_FF), lambda i, be, nu: (be[i], 0, 0)),
            pl.BlockSpec((1, MOE_FF, D_MODEL), lambda i, be, nu: (be[i], 0, 0)),
        ],
        out_specs=pl.BlockSpec((ROW_BLOCK * ROW_TILE, LANES), lambda i, be, nu: (i, 0)),
        scratch_shapes=[pltpu.VMEM(hn.shape, jnp.uint32),
                        pltpu.VMEM((ROW_BLOCK * X_TILE, LANES), jnp.uint32),
                        pltpu.VMEM((ROW_BLOCK * X_TILE, LANES), jnp.uint32),
                        pltpu.VMEM((D_MODEL, MOE_FF), BF16), pltpu.VMEM((D_MODEL, MOE_FF), BF16),
                        pltpu.VMEM((MOE_FF, D_MODEL), BF16),
                        pltpu.SemaphoreType.DMA],
    )
    return pl.pallas_call(
        _experts_kernel,
        grid_spec=grid_spec,
        out_shape=jax.ShapeDtypeStruct((n_rows * ROW_TILE, LANES), F32),
        compiler_params=_params(("arbitrary",)),
        name="experts",
    )(block_expert, n_used, tok3, tok3, hn, w_gate, w_up, w_down)


COMB_TM = 256


def _combine_kernel(dest_ref, dest_next_ref, x1_ref, gate_ref, ys_ref, out_ref, ybuf, sems):
    g = pl.program_id(0)

    def row_copy(d, slot, k, r):
        src = ys_ref.at[pl.ds(pl.multiple_of(d * ROW_TILE, ROW_TILE), ROW_TILE), :]
        dst = ybuf.at[slot, k, pl.ds(pl.multiple_of(r * ROW_TILE, ROW_TILE), ROW_TILE), :]
        return pltpu.make_async_copy(src, dst, sems.at[slot])

    def gather(dref, base, slot):
        def issue(r, c):
            row_copy(dref[0, 0, base + 2 * r], slot, 0, r).start(priority=0)
            row_copy(dref[0, 0, base + 2 * r + 1], slot, 1, r).start(priority=1)
            return c
        lax.fori_loop(0, COMB_TM, issue, 0, unroll=DMA_UNROLL)

    def finish(slot):
        def drain(r, c):
            row_copy(0, slot, 0, 0).wait()
            return c
        lax.fori_loop(0, 2 * COMB_TM, drain, 0, unroll=DMA_UNROLL)
        rows = slice(slot * COMB_TM, (slot + 1) * COMB_TM)
        gate = gate_ref[rows, :]
        lane = lax.broadcasted_iota(jnp.int32, gate.shape, 1)
        w0 = jnp.sum(jnp.where(lane == 0, gate, 0.0), axis=-1, keepdims=True)
        w1 = jnp.sum(jnp.where(lane == 1, gate, 0.0), axis=-1, keepdims=True)
        y0 = _load_row_tiles(ybuf.at[slot, 0], COMB_TM)
        y1 = _load_row_tiles(ybuf.at[slot, 1], COMB_TM)
        out_ref[rows, :] = x1_ref[rows, :] + (y0 * w0 + y1 * w1)

    @pl.when(g == 0)
    def _():
        gather(dest_ref, 0, 0)

    gather(dest_ref, 2 * COMB_TM, 1)
    finish(0)

    @pl.when(g + 1 < pl.num_programs(0))
    def _():
        gather(dest_next_ref, 0, 0)

    finish(1)


def _combine(x1, gates, ys, dest):
    t = x1.shape[0]
    ng = t // (2 * COMB_TM)
    dest3 = dest.reshape(ng, 1, 4 * COMB_TM)
    smem = lambda f: pl.BlockSpec((1, 1, 4 * COMB_TM), f, memory_space=pltpu.SMEM)
    return pl.pallas_call(
        _combine_kernel,
        grid=(ng,),
        in_specs=[smem(lambda i: (i, 0, 0)),
                  smem(lambda i: (jnp.minimum(i + 1, ng - 1), 0, 0)),
                  pl.BlockSpec((2 * COMB_TM, D_MODEL), lambda i: (i, 0)),
                  pl.BlockSpec((2 * COMB_TM, LANES), lambda i: (i, 0)),
                  pl.BlockSpec(memory_space=pl.ANY)],
        out_specs=pl.BlockSpec((2 * COMB_TM, D_MODEL), lambda i: (i, 0)),
        out_shape=jax.ShapeDtypeStruct((t, D_MODEL), F32),
        scratch_shapes=[pltpu.VMEM((2, 2, COMB_TM * ROW_TILE, LANES), F32),
                        pltpu.SemaphoreType.DMA((2,))],
        compiler_params=_params(("arbitrary",)),
        name="combine",
    )(dest3, dest3, x1, gates, ys)


def _layer(x, mem, attn_norm_g, mem_norm_g, w_in, w_gla_gk, b_gla_gk, gla_out_norm_g,
           moba_q_norm_g, moba_k_norm_g, w_mem_kv, mem_q_norm_g, mem_k_norm_g, w_out,
           ffn_norm_g, w_router_group, b_router_group, w_router_expert, b_router_expert,
           w_gate, w_up, w_down):
    b, s, d = x.shape
    t = b * s
    x2 = x.reshape(t, d)
    row = lambda v: v.reshape(1, -1).astype(F32)
    tile_heads = lambda v: jnp.tile(v.astype(F32), HEADS).reshape(1, QK_W)
    hid = np.arange(QK_W) // DH
    seg = jnp.asarray((hid[:, None] == hid[None, :]) / DH, BF16)

    w_in_p = jnp.concatenate([w_in[:, :1536], w_in[:, 1552:], w_in[:, 1536:1552]], axis=1).astype(BF16)
    wr = jnp.concatenate([w_router_group,
                          jnp.transpose(w_router_expert, (1, 0, 2)).reshape(d, N_EXPERTS),
                          jnp.zeros((d, LANES - N_GROUPS - N_EXPERTS), F32)], axis=1).astype(BF16)
    br = jnp.concatenate([b_router_group, b_router_expert.reshape(N_EXPERTS),
                          jnp.zeros((LANES - N_GROUPS - N_EXPERTS,), F32)]).reshape(1, LANES)

    memk, memvt = _mem_kv(mem, row(mem_norm_g), w_mem_kv.astype(BF16), tile_heads(mem_k_norm_g), seg)
    qk, gv, gr, glr, mq, mk, mvt, kmean, cq = _in_proj(
        x2, row(attn_norm_g), w_in_p, seg, tile_heads(moba_q_norm_g), tile_heads(moba_k_norm_g),
        tile_heads(mem_q_norm_g))
    o_gla = _gla(qk, gv, gr, glr, w_gla_gk.astype(BF16), row(b_gla_gk), row(gla_out_norm_g), b, s)
    o_moba, o_mem = _moba_mem(mq, mk, mvt, kmean, cq, memk, memvt, b, s)
    x1, hn, e_ids, gates, counts = _out_router(
        x2, o_gla.reshape(t, GLA_V_W), o_moba.reshape(t, QK_W), o_mem.reshape(t, QK_W),
        w_out.astype(BF16), row(ffn_norm_g), wr, br)

    counts = counts[0, N_GROUPS:N_GROUPS + N_EXPERTS]
    padded = (counts + ROW_BLOCK - 1) // ROW_BLOCK * ROW_BLOCK
    pends = jnp.cumsum(padded)
    pstarts = pends - padded
    onehot = e_ids[:, :2, None] == jnp.arange(N_EXPERTS, dtype=jnp.int32)
    dest = (jnp.sum(jnp.where(onehot, pstarts, 0), axis=-1) + e_ids[:, 2:4]).astype(jnp.int32).reshape(-1)
    n_rows = (t * 2 + N_EXPERTS * (ROW_BLOCK - 1) + ROW_BLOCK - 1) // ROW_BLOCK * ROW_BLOCK
    nb = n_rows // ROW_BLOCK
    block_start = jnp.arange(nb, dtype=jnp.int32) * ROW_BLOCK
    block_expert = jnp.minimum(jnp.sum(block_start[:, None] >= pends[None, :], axis=1),
                               N_EXPERTS - 1).astype(jnp.int32)
    n_used = (pends[-1] // ROW_BLOCK).astype(jnp.int32).reshape(1)

    tok_of_row = jnp.zeros((n_rows,), jnp.int32).at[dest].set(
        jnp.arange(2 * t, dtype=jnp.int32) // 2, unique_indices=True)
    ys = _experts(hn, tok_of_row, block_expert, n_used, w_gate, w_up, w_down)
    out = _combine(x1, gates, ys, dest)
    return out.reshape(b, s, d)


def kernel(x, mem, attn_norm_g, mem_norm_g, w_in, w_gla_gk, b_gla_gk, gla_out_norm_g, moba_q_norm_g, moba_k_norm_g, w_mem_kv, mem_q_norm_g, mem_k_norm_g, w_out, ffn_norm_g, w_router_group, b_router_group, w_router_expert, b_router_expert, w_gate, w_up, w_down):
    depth = w_in.shape[0]
    for l in range(depth):
        x = _layer(x, mem, attn_norm_g[l], mem_norm_g[l], w_in[l], w_gla_gk[l], b_gla_gk[l],
                   gla_out_norm_g[l], moba_q_norm_g[l], moba_k_norm_g[l], w_mem_kv[l],
                   mem_q_norm_g[l], mem_k_norm_g[l], w_out[l], ffn_norm_g[l], w_router_group[l],
                   b_router_group[l], w_router_expert[l], b_router_expert[l],
                   w_gate[l], w_up[l], w_down[l])
    return x
```

```python
import functools

import jax
import jax.numpy as jnp
import numpy as np
from jax import lax
from jax.experimental import pallas as pl
from jax.experimental.pallas import tpu as pltpu

F32 = jnp.float32
BF16 = jnp.bfloat16
EPS = 1e-6
NEG_INF = float("-inf")

D_MODEL = 1024
N_MEM = 256
HEADS = 4
DH = 64
GLA_DV = 128
GLA_RANK = 16
GLA_GATE_NORMALIZER = 16.0
GLA_CHUNK = 64
MOBA_BLOCK = 256
MOBA_TOPK = 3
QK_W = HEADS * DH
GLA_V_W = HEADS * GLA_DV
N_GROUPS = 4
EXPERTS_PER_GROUP = 8
N_EXPERTS = N_GROUPS * EXPERTS_PER_GROUP
MOE_FF = 512
LANES = 128
ROW_BLOCK = 256

VMEM_LIMIT = 56 * 1024 * 1024


def _params(sem):
    return pltpu.CompilerParams(dimension_semantics=sem, vmem_limit_bytes=VMEM_LIMIT)


def _nt(a, b):
    return lax.dot_general(a, b, (((1,), (1,)), ((), ())), preferred_element_type=F32)


def _tn(a, b):
    return lax.dot_general(a, b, (((0,), (0,)), ((), ())), preferred_element_type=F32)


def _dot(a, b):
    return jnp.dot(a, b, preferred_element_type=F32)


def _rms_rows(x, g):
    ms = jnp.mean(x * x, axis=-1, keepdims=True)
    return x * lax.rsqrt(ms + EPS) * g


def _split3(x):
    h1 = x.astype(BF16)
    r1 = x - h1.astype(F32)
    h2 = r1.astype(BF16)
    h3 = (r1 - h2.astype(F32)).astype(BF16)
    return h1, h2, h3


def _head_mean_sq(x, seg):
    sq = x * x
    hi = sq.astype(BF16)
    lo = (sq - hi.astype(F32)).astype(BF16)
    return _dot(hi, seg) + _dot(lo, seg)


def _silu(x):
    return x * (1.0 / (1.0 + jnp.exp(-x)))


VT_ROWS = DH + 16
VT_ALL = HEADS * VT_ROWS


ALIBI_SLOPES = tuple(2.0 ** (-8.0 * (h + 1) / HEADS) for h in range(HEADS))


def _vt_with_ones(v, key_slopes=None):
    n = v.shape[0]
    vt = v.T
    ones = jnp.ones((VT_ROWS - DH, n), F32)
    pos = lax.broadcasted_iota(jnp.int32, (1, n), 1).astype(F32)
    parts = []
    for h in range(HEADS):
        scale = 1.0 if key_slopes is None else jnp.exp(key_slopes[h] * pos)
        parts += [vt[h * DH:(h + 1) * DH] * scale, ones * scale]
    return jnp.concatenate(parts, axis=0).astype(BF16)


ROW_TILE = D_MODEL // LANES
X_TILE = ROW_TILE // 2


def _store_row_tiles(ref, x):
    n, tiles = x.shape[0], x.shape[1] // LANES
    for c in range(tiles):
        ref[pl.ds(c, n, stride=tiles), :] = x[:, c * LANES:(c + 1) * LANES]


def _load_row_tiles(ref, n):
    tiles = ref.shape[0] // n
    return jnp.concatenate([ref[pl.ds(c, n, stride=tiles), :] for c in range(tiles)], axis=1)


def _pack_bf16_pairs(x):
    w = x.shape[1] // 2
    bits = lambda t: lax.bitcast_convert_type(t.astype(BF16).astype(F32), jnp.uint32)
    return (bits(x[:, w:]) & jnp.uint32(0xFFFF0000)) | (bits(x[:, :w]) >> 16)


def _unpack_bf16_pairs(words):
    lo = lax.bitcast_convert_type(words << 16, F32).astype(BF16)
    hi = lax.bitcast_convert_type(words & jnp.uint32(0xFFFF0000), F32).astype(BF16)
    return lo, hi


def _mem_kv_kernel(mem_ref, g_ref, w_ref, gk_ref, seg_ref, k_ref, vt_ref):
    h = _rms_rows(mem_ref[0], g_ref[...]).astype(BF16)
    kv = _dot(h, w_ref[...])
    k = kv[:, :QK_W]
    kn = k * lax.rsqrt(_head_mean_sq(k, seg_ref[...]) + EPS) * gk_ref[...]
    k_ref[0] = kn.astype(BF16)
    vt_ref[0] = _vt_with_ones(kv[:, QK_W:])


def _mem_kv(mem, g, w_bf, gk_t, seg):
    b = mem.shape[0]
    return pl.pallas_call(
        _mem_kv_kernel,
        grid=(b,),
        in_specs=[
            pl.BlockSpec((1, N_MEM, D_MODEL), lambda i: (i, 0, 0)),
            pl.BlockSpec((1, D_MODEL), lambda i: (0, 0)),
            pl.BlockSpec((D_MODEL, 2 * QK_W), lambda i: (0, 0)),
            pl.BlockSpec((1, QK_W), lambda i: (0, 0)),
            pl.BlockSpec((QK_W, QK_W), lambda i: (0, 0)),
        ],
        out_specs=[
            pl.BlockSpec((1, N_MEM, QK_W), lambda i: (i, 0, 0)),
            pl.BlockSpec((1, VT_ALL, N_MEM), lambda i: (i, 0, 0)),
        ],
        out_shape=[
            jax.ShapeDtypeStruct((b, N_MEM, QK_W), BF16),
            jax.ShapeDtypeStruct((b, VT_ALL, N_MEM), BF16),
        ],
        compiler_params=_params(("arbitrary",)),
        name="mem_kv",
    )(mem, g, w_bf, gk_t, seg)


IN_TM = 512
_C_QK, _C_V, _C_R, _C_MQ, _C_MK, _C_MV, _C_CQ, _C_LR = 0, 512, 1024, 1536, 1792, 2048, 2304, 2560
D_IN = 2576


def _in_proj_kernel(x_ref, g_ref, w_ref, seg_ref, gq_ref, gk_ref, gc_ref,
                    qk_ref, v_ref, r_ref, lr_ref, mq_ref, mk_ref, mvt_ref, kmean_ref, cq_ref):
    h = _rms_rows(x_ref[...], g_ref[...]).astype(BF16)
    p = _dot(h, w_ref[...])
    seg = seg_ref[...]
    qk_ref[...] = p[:, _C_QK:_C_V]
    v_ref[...] = p[:, _C_V:_C_R].astype(BF16)
    r_ref[...] = p[:, _C_R:_C_MQ]
    lr_ref[...] = p[:, _C_LR:D_IN]

    def head_norm(t, gain):
        return t * lax.rsqrt(_head_mean_sq(t, seg) + EPS) * gain

    scale = DH ** -0.5
    mq_ref[...] = (head_norm(p[:, _C_MQ:_C_MK], gq_ref[...]) * scale).astype(BF16)
    cq_ref[...] = (head_norm(p[:, _C_CQ:_C_LR], gc_ref[...]) * scale).astype(BF16)
    kn = head_norm(p[:, _C_MK:_C_MV], gk_ref[...])
    mk_ref[...] = kn.astype(BF16)
    mv = p[:, _C_MV:_C_CQ]
    for j in range(IN_TM // MOBA_BLOCK):
        rows = slice(j * MOBA_BLOCK, (j + 1) * MOBA_BLOCK)
        kmean_ref[0, j:j + 1, :] = jnp.mean(kn[rows], axis=0, keepdims=True)
        mvt_ref[j] = _vt_with_ones(mv[rows], ALIBI_SLOPES)


def _in_proj(x2, g, w_bf, seg, gq_t, gk_t, gc_t):
    t = x2.shape[0]
    nt = t // IN_TM
    nb = IN_TM // MOBA_BLOCK
    row = lambda w: pl.BlockSpec((IN_TM, w), lambda i: (i, 0))
    const = lambda a, b: pl.BlockSpec((a, b), lambda i: (0, 0))
    return pl.pallas_call(
        _in_proj_kernel,
        grid=(nt,),
        in_specs=[row(D_MODEL), const(1, D_MODEL), const(D_MODEL, D_IN), const(QK_W, QK_W),
                  const(1, QK_W), const(1, QK_W), const(1, QK_W)],
        out_specs=[row(2 * QK_W), row(GLA_V_W), row(GLA_V_W), row(GLA_RANK), row(QK_W), row(QK_W),
                   pl.BlockSpec((nb, VT_ALL, MOBA_BLOCK), lambda i: (i, 0, 0)),
                   pl.BlockSpec((1, nb, QK_W), lambda i: (i, 0, 0)),
                   row(QK_W)],
        out_shape=[
            jax.ShapeDtypeStruct((t, 2 * QK_W), F32),
            jax.ShapeDtypeStruct((t, GLA_V_W), BF16),
            jax.ShapeDtypeStruct((t, GLA_V_W), F32),
            jax.ShapeDtypeStruct((t, GLA_RANK), F32),
            jax.ShapeDtypeStruct((t, QK_W), BF16),
            jax.ShapeDtypeStruct((t, QK_W), BF16),
            jax.ShapeDtypeStruct((t // MOBA_BLOCK, VT_ALL, MOBA_BLOCK), BF16),
            jax.ShapeDtypeStruct((nt, nb, QK_W), F32),
            jax.ShapeDtypeStruct((t, QK_W), BF16),
        ],
        compiler_params=_params(("arbitrary",)),
        name="in_proj",
    )(x2, g, w_bf, seg, gq_t, gk_t, gc_t)


GLA_TC = 256


def _gla_kernel(qk_ref, v_ref, r_ref, lr_ref, wgk_ref, bgk_ref, gn_ref, tri_ref, ones_ref,
                o_ref, st_ref):
    @pl.when(pl.program_id(1) == 0)
    def _():
        st_ref[...] = jnp.zeros_like(st_ref)

    qk = qk_ref[0]
    q = qk[:, :QK_W]
    k = qk[:, QK_W:]
    gk = _dot(lr_ref[0].astype(BF16), wgk_ref[...]) + bgk_ref[...]
    g = -(jnp.maximum(-gk, 0.0) + jnp.log1p(jnp.exp(-jnp.abs(gk)))) / GLA_GATE_NORMALIZER
    g1, g2, g3 = _split3(g)
    tri = tri_ref[...]
    ones = ones_ref[...]
    cum = _dot(tri, g1) + _dot(tri, g2) + _dot(tri, g3)
    tot = _dot(ones, g1) + _dot(ones, g2) + _dot(ones, g3)
    q_dec = (q * (DH ** -0.5) * jnp.exp(cum)).astype(BF16)
    k_inv = (k * jnp.exp(-cum)).astype(BF16)
    k_end = (k * jnp.exp(tot - cum)).astype(BF16)
    decay = jnp.exp(tot)

    lane_head = lax.broadcasted_iota(jnp.int32, (GLA_TC, QK_W), 1) // DH
    row_t = lax.broadcasted_iota(jnp.int32, (GLA_TC, GLA_TC), 0)
    col_t = lax.broadcasted_iota(jnp.int32, (GLA_TC, GLA_TC), 1)
    causal = (row_t >= col_t) & (row_t // GLA_CHUNK == col_t // GLA_CHUNK)
    same_head = (lax.broadcasted_iota(jnp.int32, (GLA_V_W, QK_W), 0) // GLA_DV
                 == lax.broadcasted_iota(jnp.int32, (GLA_V_W, QK_W), 1) // DH)
    gain = gn_ref[...]

    v = v_ref[0]
    chunks = [slice(c * GLA_CHUNK, (c + 1) * GLA_CHUNK) for c in range(GLA_TC // GLA_CHUNK)]
    d_st = [jnp.where(same_head, _tn(v[rows], k_end[rows]), 0.0) for rows in chunks]
    states = [st_ref[...]]
    for c, rows in enumerate(chunks):
        states.append(states[c] * decay[rows.start:rows.start + 1, :] + d_st[c])
    st_ref[...] = states[-1]
    o_inter = jnp.concatenate([_nt(q_dec[rows], states[c].astype(BF16)) for c, rows in enumerate(chunks)],
                              axis=0)
    q_stack = jnp.concatenate([jnp.where(lane_head == h, q_dec, jnp.zeros_like(q_dec)) for h in range(HEADS)],
                              axis=0)
    a_all = _nt(q_stack, k_inv)
    outs = []
    for h in range(HEADS):
        a = jnp.where(causal, a_all[h * GLA_TC:(h + 1) * GLA_TC], 0.0).astype(BF16)
        vs = slice(h * GLA_DV, (h + 1) * GLA_DV)
        oh = _dot(a, v[:, vs]) + o_inter[:, vs]
        outs.append(_rms_rows(oh, gain) * _silu(r_ref[0, :, vs]))
    o_ref[0] = jnp.concatenate(outs, axis=-1).astype(BF16)


def _gla(qk, v, r, lr, wgk_bf, bgk, gn, b, s):
    ns = s // GLA_TC
    idx = np.arange(GLA_TC)
    same_chunk = (idx[:, None] // GLA_CHUNK) == (idx[None, :] // GLA_CHUNK)
    tri = jnp.asarray(same_chunk & (idx[:, None] >= idx[None, :]), BF16)
    ones = jnp.asarray(same_chunk, BF16)
    seq = lambda w: pl.BlockSpec((1, GLA_TC, w), lambda i, j: (i, j, 0))
    const = lambda a, c: pl.BlockSpec((a, c), lambda i, j: (0, 0))
    return pl.pallas_call(
        _gla_kernel,
        grid=(b, ns),
        in_specs=[seq(2 * QK_W), seq(GLA_V_W), seq(GLA_V_W), seq(GLA_RANK),
                  const(GLA_RANK, QK_W), const(1, QK_W), const(1, GLA_DV),
                  const(GLA_TC, GLA_TC), const(GLA_TC, GLA_TC)],
        out_specs=seq(GLA_V_W),
        out_shape=jax.ShapeDtypeStruct((b, s, GLA_V_W), BF16),
        scratch_shapes=[pltpu.VMEM((GLA_V_W, QK_W), F32)],
        compiler_params=_params(("arbitrary", "arbitrary")),
        name="gla",
    )(qk.reshape(b, s, 2 * QK_W), v.reshape(b, s, GLA_V_W), r.reshape(b, s, GLA_V_W),
      lr.reshape(b, s, GLA_RANK), wgk_bf, bgk, gn, tri, ones)


TQ = MOBA_BLOCK
KV_UNROLL = 4


def _moba_mem_kernel(q_ref, k_ref, vt_ref, kmean_ref, cq_ref, mk_ref, mvt_ref,
                     o_moba_ref, o_mem_ref,
                     qcat_ref, selb_ref, s_ref, *, n_blocks):
    i = pl.program_id(1)
    lane_head = lax.broadcasted_iota(jnp.int32, (TQ, QK_W), 1) // DH
    dist0 = (lax.broadcasted_iota(jnp.int32, (MOBA_BLOCK, TQ), 1)
             - lax.broadcasted_iota(jnp.int32, (MOBA_BLOCK, TQ), 0)).astype(F32)
    slopes = ALIBI_SLOPES
    heads = [slice(h * TQ, (h + 1) * TQ) for h in range(HEADS)]
    vrows = [slice(h * VT_ROWS, (h + 1) * VT_ROWS) for h in range(HEADS)]

    def stack_heads(x):
        return jnp.concatenate([jnp.where(lane_head == h, x, jnp.zeros_like(x)) for h in range(HEADS)], axis=0)

    def finish(accl):
        return accl[:DH] * (1.0 / accl[DH:DH + 1])

    s_all = _nt(mk_ref[0], stack_heads(cq_ref[0]))
    outs = []
    for h in range(HEADS):
        s = s_all[:, heads[h]]
        p = jnp.exp(s - jnp.max(s, axis=0, keepdims=True)).astype(BF16)
        outs.append(finish(_dot(mvt_ref[0, vrows[h], :], p)))
    o_mem_ref[0] = jnp.concatenate(outs, axis=0).T.astype(BF16)

    qcat = stack_heads(q_ref[0])
    qcat_ref[...] = qcat
    k_own = k_ref[0, pl.ds(pl.multiple_of(i * MOBA_BLOCK, MOBA_BLOCK), MOBA_BLOCK), :]
    gate_all = _nt(kmean_ref[0].astype(BF16), qcat)
    s_all = _nt(k_own, qcat)
    blk = lax.broadcasted_iota(jnp.int32, (n_blocks, TQ), 0)
    blk_f = blk.astype(F32)

    tq = lax.broadcasted_iota(jnp.int32, (1, TQ), 1).astype(F32)
    init = []
    for h in range(HEADS):
        gate = jnp.where(blk < i, gate_all[:, heads[h]], NEG_INF)
        chosen = jnp.zeros((n_blocks, TQ), jnp.bool_)
        for r in range(MOBA_TOPK):
            mx = jnp.max(gate, axis=0, keepdims=True)
            first = jnp.min(jnp.where(gate == mx, blk_f, float(n_blocks)), axis=0, keepdims=True)
            hit = blk_f == first
            chosen = chosen | (hit & (mx > NEG_INF))
            gate = jnp.where(hit, NEG_INF, gate)
        selb_ref[h] = jnp.where(chosen, 0.0, NEG_INF)
        aq = -slopes[h] * tq
        s = jnp.where(dist0 >= 0.0, s_all[:, heads[h]], NEG_INF)
        m0 = jnp.max(s, axis=0, keepdims=True) + aq
        p = jnp.exp(s - (m0 - aq)).astype(BF16)
        init += [m0, _dot(vt_ref[i, vrows[h], :], p)]

    def stage_scores(j, slot):
        kj = k_ref[0, pl.ds(pl.multiple_of(j * MOBA_BLOCK, MOBA_BLOCK), MOBA_BLOCK), :]
        s_ref[slot] = _nt(kj, qcat_ref[...])

    stage_scores(0, 0)

    def past_blocks(g, carry, unroll, base):
        carry = list(carry)
        for u in range(unroll):
            j = base + g * unroll + u
            stage_scores(jnp.minimum(j + 1, n_blocks - 1), (u + 1) % 2)
            off = jnp.full((1, TQ), (i - j) * MOBA_BLOCK, jnp.int32).astype(F32) + tq
            for h in range(HEADS):
                m, accl = carry[2 * h:2 * h + 2]
                s1 = s_ref[u % 2, :, heads[h]]
                rb = selb_ref[h, pl.ds(j, 1), :] - slopes[h] * off
                m_new = jnp.maximum(m, jnp.max(s1, axis=0, keepdims=True) + rb)
                p = jnp.exp(s1 - (m_new - rb)).astype(BF16)
                carry[2 * h:2 * h + 2] = [m_new, jnp.exp(m - m_new) * accl + _dot(vt_ref[j, vrows[h], :], p)]
        return tuple(carry)

    n_main = i // KV_UNROLL
    rest = i - n_main * KV_UNROLL
    carry = lax.fori_loop(0, n_main, functools.partial(past_blocks, unroll=KV_UNROLL, base=0), tuple(init))
    final = lax.fori_loop(0, (rest + 1) // 2,
                          functools.partial(past_blocks, unroll=2, base=n_main * KV_UNROLL), carry)
    o_t = jnp.concatenate([finish(final[2 * h + 1]) for h in range(HEADS)], axis=0)
    o_moba_ref[0] = o_t.T.astype(BF16)


def _moba_mem(mq, mk, mvt, kmean, cq, memk, memvt, b, s):
    nq = s // TQ
    n_blocks = s // MOBA_BLOCK
    qspec = pl.BlockSpec((1, TQ, QK_W), lambda i, j: (i, j, 0))
    return pl.pallas_call(
        functools.partial(_moba_mem_kernel, n_blocks=n_blocks),
        grid=(b, nq),
        in_specs=[
            qspec,
            pl.BlockSpec((1, s, QK_W), lambda i, j: (i, 0, 0)),
            pl.BlockSpec((n_blocks, VT_ALL, MOBA_BLOCK), lambda i, j: (i, 0, 0)),
            pl.BlockSpec((1, n_blocks, QK_W), lambda i, j: (i, 0, 0)),
            qspec,
            pl.BlockSpec((1, N_MEM, QK_W), lambda i, j: (i, 0, 0)),
            pl.BlockSpec((1, VT_ALL, N_MEM), lambda i, j: (i, 0, 0)),
        ],
        out_specs=[qspec, qspec],
        out_shape=[jax.ShapeDtypeStruct((b, s, QK_W), BF16), jax.ShapeDtypeStruct((b, s, QK_W), BF16)],
        scratch_shapes=[
            pltpu.VMEM((HEADS * TQ, QK_W), BF16),
            pltpu.VMEM((HEADS, n_blocks, TQ), F32),
            pltpu.VMEM((2, MOBA_BLOCK, HEADS * TQ), F32),
        ],
        compiler_params=_params(("arbitrary", "arbitrary")),
        name="moba_mem",
    )(mq.reshape(b, s, QK_W), mk.reshape(b, s, QK_W), mvt, kmean.reshape(b, n_blocks, QK_W),
      cq.reshape(b, s, QK_W), memk, memvt)


OUT_TM = 512


def _out_router_kernel(x_ref, og_ref, om_ref, oc_ref, w_ref, g_ref, wr_ref, br_ref, tri_ref,
                       x1_ref, h_ref, e_ref, gate_ref, count_ref, carry_ref):
    w = w_ref[...]
    x1 = (x_ref[...] + _dot(og_ref[...], w[:GLA_V_W]) + _dot(om_ref[...], w[GLA_V_W:GLA_V_W + QK_W])
          + _dot(oc_ref[...], w[GLA_V_W + QK_W:]))
    x1_ref[...] = x1
    hn = _rms_rows(x1, g_ref[...])
    _store_row_tiles(h_ref, _pack_bf16_pairs(hn))
    logits = _dot(hn.astype(BF16), wr_ref[...]) + br_ref[...]
    lane = lax.broadcasted_iota(jnp.int32, logits.shape, 1).astype(F32)
    lg = jnp.where(lane < N_GROUPS, logits, NEG_INF)
    mg = jnp.max(lg, axis=-1, keepdims=True)
    g_sel = jnp.min(jnp.where(lg == mg, lane, float(LANES)), axis=-1, keepdims=True)
    p_group = 1.0 / jnp.sum(jnp.exp(lg - mg), axis=-1, keepdims=True)
    lo = N_GROUPS + g_sel * EXPERTS_PER_GROUP
    le = jnp.where((lane >= lo) & (lane < lo + EXPERTS_PER_GROUP), logits, NEG_INF)
    m0 = jnp.max(le, axis=-1, keepdims=True)
    i0 = jnp.min(jnp.where(le == m0, lane, float(LANES)), axis=-1, keepdims=True)
    le1 = jnp.where(lane == i0, NEG_INF, le)
    m1 = jnp.max(le1, axis=-1, keepdims=True)
    i1 = jnp.min(jnp.where(le1 == m1, lane, float(LANES)), axis=-1, keepdims=True)
    z = jnp.exp(m1 - m0)
    w0 = p_group / (1.0 + z)
    w1 = p_group * z / (1.0 + z)
    gate_ref[...] = jnp.where(lane == 0.0, w0, jnp.where(lane == 1.0, w1, 0.0))

    @pl.when(pl.program_id(0) == 0)
    def _():
        carry_ref[...] = jnp.zeros_like(carry_ref)

    oh0 = (lane == i0).astype(F32)
    oh1 = (lane == i1).astype(F32)
    both = oh0 + oh1
    before = _dot(tri_ref[...], both.astype(BF16)) + carry_ref[...]
    r0 = jnp.sum(oh0 * before, axis=-1, keepdims=True)
    r1 = jnp.sum(oh1 * before, axis=-1, keepdims=True)
    carry_ref[...] = carry_ref[...] + jnp.sum(both, axis=0, keepdims=True)
    count_ref[...] = jnp.broadcast_to(carry_ref[...], count_ref.shape).astype(jnp.int32)
    info = jnp.where(lane == 0.0, i0 - N_GROUPS, jnp.where(lane == 1.0, i1 - N_GROUPS,
                     jnp.where(lane == 2.0, r0, jnp.where(lane == 3.0, r1, 0.0))))
    e_ref[...] = info.astype(jnp.int32)


def _out_router(x2, og, om, oc, w_bf, g, wr_bf, br):
    t = x2.shape[0]
    row = lambda w: pl.BlockSpec((OUT_TM, w), lambda i: (i, 0))
    const = lambda a, b: pl.BlockSpec((a, b), lambda i: (0, 0))
    idx = np.arange(OUT_TM)
    tri = jnp.asarray(idx[:, None] > idx[None, :], BF16)
    return pl.pallas_call(
        _out_router_kernel,
        grid=(t // OUT_TM,),
        in_specs=[row(D_MODEL), row(GLA_V_W), row(QK_W), row(QK_W), const(D_MODEL, D_MODEL),
                  const(1, D_MODEL), const(D_MODEL, LANES), const(1, LANES), const(OUT_TM, OUT_TM)],
        out_specs=[row(D_MODEL), pl.BlockSpec((OUT_TM * X_TILE, LANES), lambda i: (i, 0)),
                   row(LANES), row(LANES), const(8, LANES)],
        out_shape=[
            jax.ShapeDtypeStruct((t, D_MODEL), F32),
            jax.ShapeDtypeStruct((t * X_TILE, LANES), jnp.uint32),
            jax.ShapeDtypeStruct((t, LANES), jnp.int32),
            jax.ShapeDtypeStruct((t, LANES), F32),
            jax.ShapeDtypeStruct((8, LANES), jnp.int32),
        ],
        scratch_shapes=[pltpu.VMEM((1, LANES), F32)],
        compiler_params=_params(("arbitrary",)),
        name="out_router",
    )(x2, og, om, oc, w_bf, g, wr_bf, br, tri)


DMA_UNROLL = 8


def _invert_kernel(dest_ref, pad_lo_ref, pad_hi_ref, tok_ref):
    def pad_segment(e, c):
        hi = pad_hi_ref[e]

        def pad(g, c2):
            for r in range(8):
                tok_ref[hi - 8 * (g + 1) + r] = 0
            return c2
        lax.fori_loop(0, lax.shift_right_logical(hi - pad_lo_ref[e] + 7, 3), pad, 0)
        return c

    lax.fori_loop(0, pad_lo_ref.shape[0], pad_segment, 0)

    def place(a, c):
        tok_ref[dest_ref[a]] = lax.shift_right_logical(a, 1)
        return c

    lax.fori_loop(0, dest_ref.shape[0], place, 0, unroll=DMA_UNROLL)


def _invert(dest, pad_lo, pad_hi, n_rows):
    smem = pl.BlockSpec(memory_space=pltpu.SMEM)
    return pl.pallas_call(
        _invert_kernel,
        in_specs=[smem, smem, smem],
        out_specs=smem,
        out_shape=jax.ShapeDtypeStruct((n_rows,), jnp.int32),
        name="invert",
    )(dest, pad_lo, pad_hi)


def _experts_kernel(be_ref, nused_ref, tok_ref, tok_next_ref, hn_hbm, wg_ref, wu_ref, wd_ref, ys_ref,
                    hn_vmem, xg_even, xg_odd, wg_bf, wu_bf, wd_bf, sem):
    i = pl.program_id(0)

    def gather(tref, dst):
        for r in range(ROW_BLOCK):
            src = pl.ds(pl.multiple_of(tref[0, 0, r] * X_TILE, X_TILE), X_TILE)
            dst[r * X_TILE:(r + 1) * X_TILE, :] = hn_vmem[src, :]

    @pl.when(i == 0)
    def _():
        load = pltpu.make_async_copy(hn_hbm, hn_vmem, sem)
        load.start()
        load.wait()
        gather(tok_ref, xg_even)

    new_expert = jnp.logical_or(i == 0, be_ref[i] != be_ref[jnp.maximum(i - 1, 0)])

    @pl.when(new_expert)
    def _():
        wg_bf[...] = wg_ref[0].astype(BF16)
        wu_bf[...] = wu_ref[0].astype(BF16)
        wd_bf[...] = wd_ref[0].astype(BF16)

    def block(cur, nxt):
        kh = D_MODEL // 2
        gather(tok_next_ref, nxt)
        x_lo, x_hi = _unpack_bf16_pairs(_load_row_tiles(cur, ROW_BLOCK))
        gate = _dot(x_lo, wg_bf[:kh, :]) + _dot(x_hi, wg_bf[kh:, :])
        up = _dot(x_lo, wu_bf[:kh, :]) + _dot(x_hi, wu_bf[kh:, :])
        hid = (_silu(gate) * up).astype(BF16)
        for s in range(2):
            y = _dot(hid, wd_bf[:, s * kh:(s + 1) * kh])
            for c in range(ROW_TILE // 2):
                ys_ref[pl.ds(s * (ROW_TILE // 2) + c, ROW_BLOCK, stride=ROW_TILE), :] = y[:, c * LANES:(c + 1) * LANES]

    used = i < nused_ref[0]
    odd = jnp.bitwise_and(i, 1) == 1
    pl.when(jnp.logical_and(used, jnp.logical_not(odd)))(lambda: block(xg_even, xg_odd))
    pl.when(jnp.logical_and(used, odd))(lambda: block(xg_odd, xg_even))

    @pl.when(i >= nused_ref[0])
    def _():
        ys_ref[...] = jnp.zeros_like(ys_ref)


def _experts(hn, tok_of_row, block_expert, n_used, w_gate, w_up, w_down):
    n_rows = tok_of_row.shape[0]
    nb = n_rows // ROW_BLOCK
    tok3 = tok_of_row.reshape(nb, 1, ROW_BLOCK)
    smem = lambda f: pl.BlockSpec((1, 1, ROW_BLOCK), f, memory_space=pltpu.SMEM)
    grid_spec = pltpu.PrefetchScalarGridSpec(
        num_scalar_prefetch=2,
        grid=(nb,),
        in_specs=[
            smem(lambda i, be, nu: (i, 0, 0)),
            smem(lambda i, be, nu: (jnp.minimum(i + 1, nb - 1), 0, 0)),
            pl.BlockSpec(memory_space=pl.ANY),
            pl.BlockSpec((1, D_MODEL, MOE_FF), lambda i, be, nu: (be[i], 0, 0)),
            pl.BlockSpec((1, D_MODEL, MOE_FF), lambda i, be, nu: (be[i], 0, 0)),
            pl.BlockSpec((1, MOE_FF, D_MODEL), lambda i, be, nu: (be[i], 0, 0)),
        ],
        out_specs=pl.BlockSpec((ROW_BLOCK * ROW_TILE, LANES), lambda i, be, nu: (i, 0)),
        scratch_shapes=[pltpu.VMEM(hn.shape, jnp.uint32),
                        pltpu.VMEM((ROW_BLOCK * X_TILE, LANES), jnp.uint32),
                        pltpu.VMEM((ROW_BLOCK * X_TILE, LANES), jnp.uint32),
                        pltpu.VMEM((D_MODEL, MOE_FF), BF16), pltpu.VMEM((D_MODEL, MOE_FF), BF16),
                        pltpu.VMEM((MOE_FF, D_MODEL), BF16),
                        pltpu.SemaphoreType.DMA],
    )
    return pl.pallas_call(
        _experts_kernel,
        grid_spec=grid_spec,
        out_shape=jax.ShapeDtypeStruct((n_rows * ROW_TILE, LANES), F32),
        compiler_params=_params(("arbitrary",)),
        name="experts",
    )(block_expert, n_used, tok3, tok3, hn, w_gate, w_up, w_down)


COMB_TM = 256


def _combine_kernel(dest_ref, dest_next_ref, x1_ref, gate_ref, ys_ref, out_ref, ybuf, sems):
    g = pl.program_id(0)

    def row_copy(d, slot, k, r):
        src = ys_ref.at[pl.ds(pl.multiple_of(d * ROW_TILE, ROW_TILE), ROW_TILE), :]
        dst = ybuf.at[slot, k, pl.ds(pl.multiple_of(r * ROW_TILE, ROW_TILE), ROW_TILE), :]
        return pltpu.make_async_copy(src, dst, sems.at[slot])

    def gather(dref, base, slot):
        def issue(r, c):
            row_copy(dref[0, 0, base + 2 * r], slot, 0, r).start(priority=0)
            row_copy(dref[0, 0, base + 2 * r + 1], slot, 1, r).start(priority=1)
            return c
        lax.fori_loop(0, COMB_TM, issue, 0, unroll=DMA_UNROLL)

    def finish(slot):
        def drain(r, c):
            row_copy(0, slot, 0, 0).wait()
            return c
        lax.fori_loop(0, 2 * COMB_TM, drain, 0, unroll=DMA_UNROLL)
        rows = slice(slot * COMB_TM, (slot + 1) * COMB_TM)
        gate = gate_ref[rows, :]
        lane = lax.broadcasted_iota(jnp.int32, gate.shape, 1)
        w0 = jnp.sum(jnp.where(lane == 0, gate, 0.0), axis=-1, keepdims=True)
        w1 = jnp.sum(jnp.where(lane == 1, gate, 0.0), axis=-1, keepdims=True)
        y0 = _load_row_tiles(ybuf.at[slot, 0], COMB_TM)
        y1 = _load_row_tiles(ybuf.at[slot, 1], COMB_TM)
        out_ref[rows, :] = x1_ref[rows, :] + (y0 * w0 + y1 * w1)

    @pl.when(g == 0)
    def _():
        gather(dest_ref, 0, 0)

    gather(dest_ref, 2 * COMB_TM, 1)
    finish(0)

    @pl.when(g + 1 < pl.num_programs(0))
    def _():
        gather(dest_next_ref, 0, 0)

    finish(1)


def _combine(x1, gates, ys, dest):
    t = x1.shape[0]
    ng = t // (2 * COMB_TM)
    dest3 = dest.reshape(ng, 1, 4 * COMB_TM)
    smem = lambda f: pl.BlockSpec((1, 1, 4 * COMB_TM), f, memory_space=pltpu.SMEM)
    return pl.pallas_call(
        _combine_kernel,
        grid=(ng,),
        in_specs=[smem(lambda i: (i, 0, 0)),
                  smem(lambda i: (jnp.minimum(i + 1, ng - 1), 0, 0)),
                  pl.BlockSpec((2 * COMB_TM, D_MODEL), lambda i: (i, 0)),
                  pl.BlockSpec((2 * COMB_TM, LANES), lambda i: (i, 0)),
                  pl.BlockSpec(memory_space=pl.ANY)],
        out_specs=pl.BlockSpec((2 * COMB_TM, D_MODEL), lambda i: (i, 0)),
        out_shape=jax.ShapeDtypeStruct((t, D_MODEL), F32),
        scratch_shapes=[pltpu.VMEM((2, 2, COMB_TM * ROW_TILE, LANES), F32),
                        pltpu.SemaphoreType.DMA((2,))],
        compiler_params=_params(("arbitrary",)),
        name="combine",
    )(dest3, dest3, x1, gates, ys)


def _layer(x, mem, attn_norm_g, mem_norm_g, w_in, w_gla_gk, b_gla_gk, gla_out_norm_g,
           moba_q_norm_g, moba_k_norm_g, w_mem_kv, mem_q_norm_g, mem_k_norm_g, w_out,
           ffn_norm_g, w_router_group, b_router_group, w_router_expert, b_router_expert,
           w_gate, w_up, w_down):
    b, s, d = x.shape
    t = b * s
    x2 = x.reshape(t, d)
    row = lambda v: v.reshape(1, -1).astype(F32)
    tile_heads = lambda v: jnp.tile(v.astype(F32), HEADS).reshape(1, QK_W)
    hid = np.arange(QK_W) // DH
    seg = jnp.asarray((hid[:, None] == hid[None, :]) / DH, BF16)

    w_in_p = jnp.concatenate([w_in[:, :1536], w_in[:, 1552:], w_in[:, 1536:1552]], axis=1).astype(BF16)
    wr = jnp.concatenate([w_router_group,
                          jnp.transpose(w_router_expert, (1, 0, 2)).reshape(d, N_EXPERTS),
                          jnp.zeros((d, LANES - N_GROUPS - N_EXPERTS), F32)], axis=1).astype(BF16)
    br = jnp.concatenate([b_router_group, b_router_expert.reshape(N_EXPERTS),
                          jnp.zeros((LANES - N_GROUPS - N_EXPERTS,), F32)]).reshape(1, LANES)

    memk, memvt = _mem_kv(mem, row(mem_norm_g), w_mem_kv.astype(BF16), tile_heads(mem_k_norm_g), seg)
    qk, gv, gr, glr, mq, mk, mvt, kmean, cq = _in_proj(
        x2, row(attn_norm_g), w_in_p, seg, tile_heads(moba_q_norm_g), tile_heads(moba_k_norm_g),
        tile_heads(mem_q_norm_g))
    o_gla = _gla(qk, gv, gr, glr, w_gla_gk.astype(BF16), row(b_gla_gk), row(gla_out_norm_g), b, s)
    o_moba, o_mem = _moba_mem(mq, mk, mvt, kmean, cq, memk, memvt, b, s)
    x1, hn, e_ids, gates, counts = _out_router(
        x2, o_gla.reshape(t, GLA_V_W), o_moba.reshape(t, QK_W), o_mem.reshape(t, QK_W),
        w_out.astype(BF16), row(ffn_norm_g), wr, br)

    counts = counts[0, N_GROUPS:N_GROUPS + N_EXPERTS]
    padded = (counts + ROW_BLOCK - 1) // ROW_BLOCK * ROW_BLOCK
    pends = jnp.cumsum(padded)
    pstarts = pends - padded
    onehot = e_ids[:, :2, None] == jnp.arange(N_EXPERTS, dtype=jnp.int32)
    dest = (jnp.sum(jnp.where(onehot, pstarts, 0), axis=-1) + e_ids[:, 2:4]).astype(jnp.int32).reshape(-1)
    n_rows = (t * 2 + N_EXPERTS * (ROW_BLOCK - 1) + ROW_BLOCK - 1) // ROW_BLOCK * ROW_BLOCK
    nb = n_rows // ROW_BLOCK
    block_start = jnp.arange(nb, dtype=jnp.int32) * ROW_BLOCK
    block_expert = jnp.minimum(jnp.sum(block_start[:, None] >= pends[None, :], axis=1),
                               N_EXPERTS - 1).astype(jnp.int32)
    n_used = (pends[-1] // ROW_BLOCK).astype(jnp.int32).reshape(1)

    pad_lo = jnp.concatenate([pstarts + counts, pends[-1:]]).astype(jnp.int32)
    pad_hi = jnp.concatenate([pends, jnp.full((1,), n_rows)]).astype(jnp.int32)
    tok_of_row = _invert(dest, pad_lo, pad_hi, n_rows)
    ys = _experts(hn, tok_of_row, block_expert, n_used, w_gate, w_up, w_down)
    out = _combine(x1, gates, ys, dest)
    return out.reshape(b, s, d)


def kernel(x, mem, attn_norm_g, mem_norm_g, w_in, w_gla_gk, b_gla_gk, gla_out_norm_g, moba_q_norm_g, moba_k_norm_g, w_mem_kv, mem_q_norm_g, mem_k_norm_g, w_out, ffn_norm_g, w_router_group, b_router_group, w_router_expert, b_router_expert, w_gate, w_up, w_down):
    depth = w_in.shape[0]
    for l in range(depth):
        x = _layer(x, mem, attn_norm_g[l], mem_norm_g[l], w_in[l], w_gla_gk[l], b_gla_gk[l],
                   gla_out_norm_g[l], moba_q_norm_g[l], moba_k_norm_g[l], w_mem_kv[l],
                   mem_q_norm_g[l], mem_k_norm_g[l], w_out[l], ffn_norm_g[l], w_router_group[l],
                   b_router_group[l], w_router_expert[l], b_router_expert[l],
                   w_gate[l], w_up[l], w_down[l])
    return x
```

```python
import functools

import jax
import jax.numpy as jnp
import numpy as np
from jax import lax
from jax.experimental import pallas as pl
from jax.experimental.pallas import tpu as pltpu

F32 = jnp.float32
BF16 = jnp.bfloat16
EPS = 1e-6
NEG_INF = float("-inf")

D_MODEL = 1024
N_MEM = 256
HEADS = 4
DH = 64
GLA_DV = 128
GLA_RANK = 16
GLA_GATE_NORMALIZER = 16.0
GLA_CHUNK = 64
MOBA_BLOCK = 256
MOBA_TOPK = 3
QK_W = HEADS * DH
GLA_V_W = HEADS * GLA_DV
N_GROUPS = 4
EXPERTS_PER_GROUP = 8
N_EXPERTS = N_GROUPS * EXPERTS_PER_GROUP
MOE_FF = 512
LANES = 128
ROW_BLOCK = 256

VMEM_LIMIT = 56 * 1024 * 1024


def _params(sem):
    return pltpu.CompilerParams(dimension_semantics=sem, vmem_limit_bytes=VMEM_LIMIT)


def _nt(a, b):
    return lax.dot_general(a, b, (((1,), (1,)), ((), ())), preferred_element_type=F32)


def _tn(a, b):
    return lax.dot_general(a, b, (((0,), (0,)), ((), ())), preferred_element_type=F32)


def _dot(a, b):
    return jnp.dot(a, b, preferred_element_type=F32)


def _rms_rows(x, g):
    ms = jnp.mean(x * x, axis=-1, keepdims=True)
    return x * lax.rsqrt(ms + EPS) * g


def _split3(x):
    h1 = x.astype(BF16)
    r1 = x - h1.astype(F32)
    h2 = r1.astype(BF16)
    h3 = (r1 - h2.astype(F32)).astype(BF16)
    return h1, h2, h3


def _head_mean_sq(x, seg):
    sq = x * x
    hi = sq.astype(BF16)
    lo = (sq - hi.astype(F32)).astype(BF16)
    return _dot(hi, seg) + _dot(lo, seg)


def _silu(x):
    return x * (1.0 / (1.0 + jnp.exp(-x)))


VT_ROWS = DH + 16
VT_ALL = HEADS * VT_ROWS


ALIBI_SLOPES = tuple(2.0 ** (-8.0 * (h + 1) / HEADS) for h in range(HEADS))


def _vt_with_ones(v, key_slopes=None):
    n = v.shape[0]
    vt = v.T
    ones = jnp.ones((VT_ROWS - DH, n), F32)
    pos = lax.broadcasted_iota(jnp.int32, (1, n), 1).astype(F32)
    parts = []
    for h in range(HEADS):
        scale = 1.0 if key_slopes is None else jnp.exp(key_slopes[h] * pos)
        parts += [vt[h * DH:(h + 1) * DH] * scale, ones * scale]
    return jnp.concatenate(parts, axis=0).astype(BF16)


ROW_TILE = D_MODEL // LANES
X_TILE = ROW_TILE // 2


def _store_row_tiles(ref, x):
    n, tiles = x.shape[0], x.shape[1] // LANES
    for c in range(tiles):
        ref[pl.ds(c, n, stride=tiles), :] = x[:, c * LANES:(c + 1) * LANES]


def _load_row_tiles(ref, n):
    tiles = ref.shape[0] // n
    return jnp.concatenate([ref[pl.ds(c, n, stride=tiles), :] for c in range(tiles)], axis=1)


def _pack_bf16_pairs(x):
    w = x.shape[1] // 2
    bits = lambda t: lax.bitcast_convert_type(t.astype(BF16).astype(F32), jnp.uint32)
    return (bits(x[:, w:]) & jnp.uint32(0xFFFF0000)) | (bits(x[:, :w]) >> 16)


def _unpack_bf16_pairs(words):
    lo = lax.bitcast_convert_type(words << 16, F32).astype(BF16)
    hi = lax.bitcast_convert_type(words & jnp.uint32(0xFFFF0000), F32).astype(BF16)
    return lo, hi


def _mem_kv_kernel(mem_ref, g_ref, w_ref, gk_ref, seg_ref, k_ref, vt_ref):
    h = _rms_rows(mem_ref[0], g_ref[...]).astype(BF16)
    kv = _dot(h, w_ref[...])
    k = kv[:, :QK_W]
    kn = k * lax.rsqrt(_head_mean_sq(k, seg_ref[...]) + EPS) * gk_ref[...]
    k_ref[0] = kn.astype(BF16)
    vt_ref[0] = _vt_with_ones(kv[:, QK_W:])


def _mem_kv(mem, g, w_bf, gk_t, seg):
    b = mem.shape[0]
    return pl.pallas_call(
        _mem_kv_kernel,
        grid=(b,),
        in_specs=[
            pl.BlockSpec((1, N_MEM, D_MODEL), lambda i: (i, 0, 0)),
            pl.BlockSpec((1, D_MODEL), lambda i: (0, 0)),
            pl.BlockSpec((D_MODEL, 2 * QK_W), lambda i: (0, 0)),
            pl.BlockSpec((1, QK_W), lambda i: (0, 0)),
            pl.BlockSpec((QK_W, QK_W), lambda i: (0, 0)),
        ],
        out_specs=[
            pl.BlockSpec((1, N_MEM, QK_W), lambda i: (i, 0, 0)),
            pl.BlockSpec((1, VT_ALL, N_MEM), lambda i: (i, 0, 0)),
        ],
        out_shape=[
            jax.ShapeDtypeStruct((b, N_MEM, QK_W), BF16),
            jax.ShapeDtypeStruct((b, VT_ALL, N_MEM), BF16),
        ],
        compiler_params=_params(("arbitrary",)),
        name="mem_kv",
    )(mem, g, w_bf, gk_t, seg)


IN_TM = 1024
_C_QK, _C_V, _C_R, _C_MQ, _C_MK, _C_MV, _C_CQ, _C_LR = 0, 512, 1024, 1536, 1792, 2048, 2304, 2560
D_IN = 2576


def _in_proj_kernel(x_ref, g_ref, w_ref, seg_ref, gq_ref, gk_ref, gc_ref,
                    qk_ref, v_ref, r_ref, lr_ref, mq_ref, mk_ref, mvt_ref, kmean_ref, cq_ref):
    h = _rms_rows(x_ref[...], g_ref[...]).astype(BF16)
    p = _dot(h, w_ref[...])
    seg = seg_ref[...]
    qk_ref[...] = p[:, _C_QK:_C_V]
    v_ref[...] = p[:, _C_V:_C_R].astype(BF16)
    r_ref[...] = p[:, _C_R:_C_MQ]
    lr_ref[...] = p[:, _C_LR:D_IN]

    def head_norm(t, gain):
        return t * lax.rsqrt(_head_mean_sq(t, seg) + EPS) * gain

    scale = DH ** -0.5
    mq_ref[...] = (head_norm(p[:, _C_MQ:_C_MK], gq_ref[...]) * scale).astype(BF16)
    cq_ref[...] = (head_norm(p[:, _C_CQ:_C_LR], gc_ref[...]) * scale).astype(BF16)
    kn = head_norm(p[:, _C_MK:_C_MV], gk_ref[...])
    mk_ref[...] = kn.astype(BF16)
    mv = p[:, _C_MV:_C_CQ]
    for j in range(IN_TM // MOBA_BLOCK):
        rows = slice(j * MOBA_BLOCK, (j + 1) * MOBA_BLOCK)
        kmean_ref[0, j:j + 1, :] = jnp.mean(kn[rows], axis=0, keepdims=True)
        mvt_ref[j] = _vt_with_ones(mv[rows], ALIBI_SLOPES)


def _in_proj(x2, g, w_bf, seg, gq_t, gk_t, gc_t):
    t = x2.shape[0]
    nt = t // IN_TM
    nb = IN_TM // MOBA_BLOCK
    row = lambda w: pl.BlockSpec((IN_TM, w), lambda i: (i, 0))
    const = lambda a, b: pl.BlockSpec((a, b), lambda i: (0, 0))
    return pl.pallas_call(
        _in_proj_kernel,
        grid=(nt,),
        in_specs=[row(D_MODEL), const(1, D_MODEL), const(D_MODEL, D_IN), const(QK_W, QK_W),
                  const(1, QK_W), const(1, QK_W), const(1, QK_W)],
        out_specs=[row(2 * QK_W), row(GLA_V_W), row(GLA_V_W), row(GLA_RANK), row(QK_W), row(QK_W),
                   pl.BlockSpec((nb, VT_ALL, MOBA_BLOCK), lambda i: (i, 0, 0)),
                   pl.BlockSpec((1, nb, QK_W), lambda i: (i, 0, 0)),
                   row(QK_W)],
        out_shape=[
            jax.ShapeDtypeStruct((t, 2 * QK_W), F32),
            jax.ShapeDtypeStruct((t, GLA_V_W), BF16),
            jax.ShapeDtypeStruct((t, GLA_V_W), F32),
            jax.ShapeDtypeStruct((t, GLA_RANK), F32),
            jax.ShapeDtypeStruct((t, QK_W), BF16),
            jax.ShapeDtypeStruct((t, QK_W), BF16),
            jax.ShapeDtypeStruct((t // MOBA_BLOCK, VT_ALL, MOBA_BLOCK), BF16),
            jax.ShapeDtypeStruct((nt, nb, QK_W), F32),
            jax.ShapeDtypeStruct((t, QK_W), BF16),
        ],
        compiler_params=_params(("arbitrary",)),
        name="in_proj",
    )(x2, g, w_bf, seg, gq_t, gk_t, gc_t)


GLA_TC = 256


def _gla_kernel(qk_ref, v_ref, r_ref, lr_ref, wgk_ref, bgk_ref, gn_ref, tri_ref, ones_ref,
                o_ref, st_ref):
    @pl.when(pl.program_id(1) == 0)
    def _():
        st_ref[...] = jnp.zeros_like(st_ref)

    qk = qk_ref[0]
    q = qk[:, :QK_W]
    k = qk[:, QK_W:]
    gk = _dot(lr_ref[0].astype(BF16), wgk_ref[...]) + bgk_ref[...]
    g = -(jnp.maximum(-gk, 0.0) + jnp.log1p(jnp.exp(-jnp.abs(gk)))) / GLA_GATE_NORMALIZER
    g1, g2, g3 = _split3(g)
    tri = tri_ref[...]
    ones = ones_ref[...]
    cum = _dot(tri, g1) + _dot(tri, g2) + _dot(tri, g3)
    tot = _dot(ones, g1) + _dot(ones, g2) + _dot(ones, g3)
    q_dec = (q * (DH ** -0.5) * jnp.exp(cum)).astype(BF16)
    k_inv = (k * jnp.exp(-cum)).astype(BF16)
    k_end = (k * jnp.exp(tot - cum)).astype(BF16)
    decay = jnp.exp(tot)

    lane_head = lax.broadcasted_iota(jnp.int32, (GLA_TC, QK_W), 1) // DH
    row_t = lax.broadcasted_iota(jnp.int32, (GLA_TC, GLA_TC), 0)
    col_t = lax.broadcasted_iota(jnp.int32, (GLA_TC, GLA_TC), 1)
    causal = (row_t >= col_t) & (row_t // GLA_CHUNK == col_t // GLA_CHUNK)
    same_head = (lax.broadcasted_iota(jnp.int32, (GLA_V_W, QK_W), 0) // GLA_DV
                 == lax.broadcasted_iota(jnp.int32, (GLA_V_W, QK_W), 1) // DH)
    gain = gn_ref[...]

    v = v_ref[0]
    chunks = [slice(c * GLA_CHUNK, (c + 1) * GLA_CHUNK) for c in range(GLA_TC // GLA_CHUNK)]
    d_st = [jnp.where(same_head, _tn(v[rows], k_end[rows]), 0.0) for rows in chunks]
    states = [st_ref[...]]
    for c, rows in enumerate(chunks):
        states.append(states[c] * decay[rows.start:rows.start + 1, :] + d_st[c])
    st_ref[...] = states[-1]
    o_inter = jnp.concatenate([_nt(q_dec[rows], states[c].astype(BF16)) for c, rows in enumerate(chunks)],
                              axis=0)
    q_stack = jnp.concatenate([jnp.where(lane_head == h, q_dec, jnp.zeros_like(q_dec)) for h in range(HEADS)],
                              axis=0)
    a_all = _nt(q_stack, k_inv)
    outs = []
    for h in range(HEADS):
        a = jnp.where(causal, a_all[h * GLA_TC:(h + 1) * GLA_TC], 0.0).astype(BF16)
        vs = slice(h * GLA_DV, (h + 1) * GLA_DV)
        oh = _dot(a, v[:, vs]) + o_inter[:, vs]
        outs.append(_rms_rows(oh, gain) * _silu(r_ref[0, :, vs]))
    o_ref[0] = jnp.concatenate(outs, axis=-1).astype(BF16)


def _gla(qk, v, r, lr, wgk_bf, bgk, gn, b, s):
    ns = s // GLA_TC
    idx = np.arange(GLA_TC)
    same_chunk = (idx[:, None] // GLA_CHUNK) == (idx[None, :] // GLA_CHUNK)
    tri = jnp.asarray(same_chunk & (idx[:, None] >= idx[None, :]), BF16)
    ones = jnp.asarray(same_chunk, BF16)
    seq = lambda w: pl.BlockSpec((1, GLA_TC, w), lambda i, j: (i, j, 0))
    const = lambda a, c: pl.BlockSpec((a, c), lambda i, j: (0, 0))
    return pl.pallas_call(
        _gla_kernel,
        grid=(b, ns),
        in_specs=[seq(2 * QK_W), seq(GLA_V_W), seq(GLA_V_W), seq(GLA_RANK),
                  const(GLA_RANK, QK_W), const(1, QK_W), const(1, GLA_DV),
                  const(GLA_TC, GLA_TC), const(GLA_TC, GLA_TC)],
        out_specs=seq(GLA_V_W),
        out_shape=jax.ShapeDtypeStruct((b, s, GLA_V_W), BF16),
        scratch_shapes=[pltpu.VMEM((GLA_V_W, QK_W), F32)],
        compiler_params=_params(("arbitrary", "arbitrary")),
        name="gla",
    )(qk.reshape(b, s, 2 * QK_W), v.reshape(b, s, GLA_V_W), r.reshape(b, s, GLA_V_W),
      lr.reshape(b, s, GLA_RANK), wgk_bf, bgk, gn, tri, ones)


TQ = MOBA_BLOCK
KV_UNROLL = 8


def _moba_mem_kernel(q_ref, k_ref, vt_ref, kmean_ref, cq_ref, mk_ref, mvt_ref,
                     o_moba_ref, o_mem_ref,
                     qcat_ref, selb_ref, s_ref, *, n_blocks):
    i = pl.program_id(1)
    lane_head = lax.broadcasted_iota(jnp.int32, (TQ, QK_W), 1) // DH
    dist0 = (lax.broadcasted_iota(jnp.int32, (MOBA_BLOCK, TQ), 1)
             - lax.broadcasted_iota(jnp.int32, (MOBA_BLOCK, TQ), 0)).astype(F32)
    slopes = ALIBI_SLOPES
    heads = [slice(h * TQ, (h + 1) * TQ) for h in range(HEADS)]
    vrows = [slice(h * VT_ROWS, (h + 1) * VT_ROWS) for h in range(HEADS)]

    def stack_heads(x):
        return jnp.concatenate([jnp.where(lane_head == h, x, jnp.zeros_like(x)) for h in range(HEADS)], axis=0)

    def finish(accl):
        return accl[:DH] * (1.0 / accl[DH:DH + 1])

    s_all = _nt(mk_ref[0], stack_heads(cq_ref[0]))
    outs = []
    for h in range(HEADS):
        s = s_all[:, heads[h]]
        p = jnp.exp(s - jnp.max(s, axis=0, keepdims=True)).astype(BF16)
        outs.append(finish(_dot(mvt_ref[0, vrows[h], :], p)))
    o_mem_ref[0] = jnp.concatenate(outs, axis=0).T.astype(BF16)

    qcat = stack_heads(q_ref[0])
    qcat_ref[...] = qcat
    k_own = k_ref[0, pl.ds(pl.multiple_of(i * MOBA_BLOCK, MOBA_BLOCK), MOBA_BLOCK), :]
    gate_all = _nt(kmean_ref[0].astype(BF16), qcat)
    s_all = _nt(k_own, qcat)
    blk = lax.broadcasted_iota(jnp.int32, (n_blocks, TQ), 0)
    blk_f = blk.astype(F32)

    tq = lax.broadcasted_iota(jnp.int32, (1, TQ), 1).astype(F32)
    init = []
    for h in range(HEADS):
        gate = jnp.where(blk < i, gate_all[:, heads[h]], NEG_INF)
        chosen = jnp.zeros((n_blocks, TQ), jnp.bool_)
        for r in range(MOBA_TOPK):
            mx = jnp.max(gate, axis=0, keepdims=True)
            first = jnp.min(jnp.where(gate == mx, blk_f, float(n_blocks)), axis=0, keepdims=True)
            hit = blk_f == first
            chosen = chosen | (hit & (mx > NEG_INF))
            gate = jnp.where(hit, NEG_INF, gate)
        selb_ref[h] = jnp.where(chosen, 0.0, NEG_INF)
        aq = -slopes[h] * tq
        s = jnp.where(dist0 >= 0.0, s_all[:, heads[h]], NEG_INF)
        m0 = jnp.max(s, axis=0, keepdims=True) + aq
        p = jnp.exp(s - (m0 - aq)).astype(BF16)
        init += [m0, _dot(vt_ref[i, vrows[h], :], p)]

    def stage_scores(j, slot):
        kj = k_ref[0, pl.ds(pl.multiple_of(j * MOBA_BLOCK, MOBA_BLOCK), MOBA_BLOCK), :]
        s_ref[slot] = _nt(kj, qcat_ref[...])

    stage_scores(0, 0)

    def past_blocks(g, carry, unroll, base):
        carry = list(carry)
        for u in range(unroll):
            j = base + g * unroll + u
            stage_scores(jnp.minimum(j + 1, n_blocks - 1), (u + 1) % 2)
            off = jnp.full((1, TQ), (i - j) * MOBA_BLOCK, jnp.int32).astype(F32) + tq
            for h in range(HEADS):
                m, accl = carry[2 * h:2 * h + 2]
                s1 = s_ref[u % 2, :, heads[h]]
                rb = selb_ref[h, pl.ds(j, 1), :] - slopes[h] * off
                m_new = jnp.maximum(m, jnp.max(s1, axis=0, keepdims=True) + rb)
                p = jnp.exp(s1 - (m_new - rb)).astype(BF16)
                carry[2 * h:2 * h + 2] = [m_new, jnp.exp(m - m_new) * accl + _dot(vt_ref[j, vrows[h], :], p)]
        return tuple(carry)

    n_main = i // KV_UNROLL
    rest = i - n_main * KV_UNROLL
    carry = lax.fori_loop(0, n_main, functools.partial(past_blocks, unroll=KV_UNROLL, base=0), tuple(init))
    final = lax.fori_loop(0, (rest + 1) // 2,
                          functools.partial(past_blocks, unroll=2, base=n_main * KV_UNROLL), carry)
    o_t = jnp.concatenate([finish(final[2 * h + 1]) for h in range(HEADS)], axis=0)
    o_moba_ref[0] = o_t.T.astype(BF16)


def _moba_mem(mq, mk, mvt, kmean, cq, memk, memvt, b, s):
    nq = s // TQ
    n_blocks = s // MOBA_BLOCK
    qspec = pl.BlockSpec((1, TQ, QK_W), lambda i, j: (i, j, 0))
    return pl.pallas_call(
        functools.partial(_moba_mem_kernel, n_blocks=n_blocks),
        grid=(b, nq),
        in_specs=[
            qspec,
            pl.BlockSpec((1, s, QK_W), lambda i, j: (i, 0, 0)),
            pl.BlockSpec((n_blocks, VT_ALL, MOBA_BLOCK), lambda i, j: (i, 0, 0)),
            pl.BlockSpec((1, n_blocks, QK_W), lambda i, j: (i, 0, 0)),
            qspec,
            pl.BlockSpec((1, N_MEM, QK_W), lambda i, j: (i, 0, 0)),
            pl.BlockSpec((1, VT_ALL, N_MEM), lambda i, j: (i, 0, 0)),
        ],
        out_specs=[qspec, qspec],
        out_shape=[jax.ShapeDtypeStruct((b, s, QK_W), BF16), jax.ShapeDtypeStruct((b, s, QK_W), BF16)],
        scratch_shapes=[
            pltpu.VMEM((HEADS * TQ, QK_W), BF16),
            pltpu.VMEM((HEADS, n_blocks, TQ), F32),
            pltpu.VMEM((2, MOBA_BLOCK, HEADS * TQ), F32),
        ],
        compiler_params=_params(("arbitrary", "arbitrary")),
        name="moba_mem",
    )(mq.reshape(b, s, QK_W), mk.reshape(b, s, QK_W), mvt, kmean.reshape(b, n_blocks, QK_W),
      cq.reshape(b, s, QK_W), memk, memvt)


OUT_TM = 1024
ROUTE_ROWS = 40


def _out_router_kernel(x_ref, og_ref, om_ref, oc_ref, w_ref, g_ref, wr_ref, br_ref, tri_ref,
                       x1_ref, h_ref, e_ref, gate_ref, count_ref, carry_ref):
    w = w_ref[...]
    x1 = (x_ref[...] + _dot(og_ref[...], w[:GLA_V_W]) + _dot(om_ref[...], w[GLA_V_W:GLA_V_W + QK_W])
          + _dot(oc_ref[...], w[GLA_V_W + QK_W:]))
    x1_ref[...] = x1
    hn = _rms_rows(x1, g_ref[...])
    _store_row_tiles(h_ref, _pack_bf16_pairs(hn))
    logits = _dot(hn.astype(BF16), wr_ref[...]) + br_ref[...]
    lt = logits.T[:ROUTE_ROWS]
    row = lax.broadcasted_iota(jnp.int32, lt.shape, 0).astype(F32)
    lg = jnp.where(row < N_GROUPS, lt, NEG_INF)
    mg = jnp.max(lg, axis=0, keepdims=True)
    g_sel = jnp.min(jnp.where(lg == mg, row, float(LANES)), axis=0, keepdims=True)
    p_group = 1.0 / jnp.sum(jnp.exp(lg - mg), axis=0, keepdims=True)
    lo = N_GROUPS + g_sel * EXPERTS_PER_GROUP
    le = jnp.where((row >= lo) & (row < lo + EXPERTS_PER_GROUP), lt, NEG_INF)
    m0 = jnp.max(le, axis=0, keepdims=True)
    i0 = jnp.min(jnp.where(le == m0, row, float(LANES)), axis=0, keepdims=True)
    le1 = jnp.where(row == i0, NEG_INF, le)
    m1 = jnp.max(le1, axis=0, keepdims=True)
    i1 = jnp.min(jnp.where(le1 == m1, row, float(LANES)), axis=0, keepdims=True)
    z = jnp.exp(m1 - m0)
    w0 = p_group / (1.0 + z)
    w1 = p_group * z / (1.0 + z)
    pad = jnp.zeros((8 - 2, lt.shape[1]), F32)
    gate_ref[...] = jnp.concatenate([w0, w1, pad], axis=0)

    @pl.when(pl.program_id(0) == 0)
    def _():
        carry_ref[...] = jnp.zeros_like(carry_ref)

    oh0 = (row == i0).astype(F32)
    oh1 = (row == i1).astype(F32)
    both = oh0 + oh1
    before = _dot(both.astype(BF16), tri_ref[...]) + carry_ref[:, 0:1]
    r0 = jnp.sum(oh0 * before, axis=0, keepdims=True)
    r1 = jnp.sum(oh1 * before, axis=0, keepdims=True)
    carry_ref[...] = carry_ref[...] + jnp.sum(both, axis=1, keepdims=True)
    count_ref[...] = carry_ref[...].astype(jnp.int32)
    e_ref[...] = jnp.concatenate([i0 - N_GROUPS, i1 - N_GROUPS, r0, r1, pad[:4]], axis=0).astype(jnp.int32)


def _out_router(x2, og, om, oc, w_bf, g, wr_bf, br):
    t = x2.shape[0]
    row = lambda w: pl.BlockSpec((OUT_TM, w), lambda i: (i, 0))
    const = lambda a, b: pl.BlockSpec((a, b), lambda i: (0, 0))
    idx = np.arange(OUT_TM)
    tri = jnp.asarray(idx[:, None] < idx[None, :], BF16)
    col = lambda r: pl.BlockSpec((r, OUT_TM), lambda i: (0, i))
    return pl.pallas_call(
        _out_router_kernel,
        grid=(t // OUT_TM,),
        in_specs=[row(D_MODEL), row(GLA_V_W), row(QK_W), row(QK_W), const(D_MODEL, D_MODEL),
                  const(1, D_MODEL), const(D_MODEL, LANES), const(1, LANES), const(OUT_TM, OUT_TM)],
        out_specs=[row(D_MODEL), pl.BlockSpec((OUT_TM * X_TILE, LANES), lambda i: (i, 0)),
                   col(8), col(8), const(ROUTE_ROWS, LANES)],
        out_shape=[
            jax.ShapeDtypeStruct((t, D_MODEL), F32),
            jax.ShapeDtypeStruct((t * X_TILE, LANES), jnp.uint32),
            jax.ShapeDtypeStruct((8, t), jnp.int32),
            jax.ShapeDtypeStruct((8, t), F32),
            jax.ShapeDtypeStruct((ROUTE_ROWS, LANES), jnp.int32),
        ],
        scratch_shapes=[pltpu.VMEM((ROUTE_ROWS, LANES), F32)],
        compiler_params=_params(("arbitrary",)),
        name="out_router",
    )(x2, og, om, oc, w_bf, g, wr_bf, br, tri)


DMA_UNROLL = 8


def _invert_kernel(dest_ref, pad_lo_ref, pad_hi_ref, tok_ref):
    def pad_segment(e, c):
        hi = pad_hi_ref[e]

        def pad(g, c2):
            for r in range(8):
                tok_ref[hi - 8 * (g + 1) + r] = 0
            return c2
        lax.fori_loop(0, lax.shift_right_logical(hi - pad_lo_ref[e] + 7, 3), pad, 0)
        return c

    lax.fori_loop(0, pad_lo_ref.shape[0], pad_segment, 0)

    def place(a, c):
        tok_ref[dest_ref[a]] = lax.shift_right_logical(a, 1)
        return c

    lax.fori_loop(0, dest_ref.shape[0], place, 0, unroll=DMA_UNROLL)


def _invert(dest, pad_lo, pad_hi, n_rows):
    smem = pl.BlockSpec(memory_space=pltpu.SMEM)
    return pl.pallas_call(
        _invert_kernel,
        in_specs=[smem, smem, smem],
        out_specs=smem,
        out_shape=jax.ShapeDtypeStruct((n_rows,), jnp.int32),
        name="invert",
    )(dest, pad_lo, pad_hi)


def _experts_kernel(be_ref, nused_ref, tok_ref, tok_next_ref, hn_hbm, wg_ref, wu_ref, wd_ref, ys_ref,
                    hn_vmem, xg_even, xg_odd, wg_bf, wu_bf, wd_bf, sem):
    i = pl.program_id(0)

    def gather(tref, dst):
        for r in range(ROW_BLOCK):
            src = pl.ds(pl.multiple_of(tref[0, 0, r] * X_TILE, X_TILE), X_TILE)
            dst[r * X_TILE:(r + 1) * X_TILE, :] = hn_vmem[src, :]

    @pl.when(i == 0)
    def _():
        load = pltpu.make_async_copy(hn_hbm, hn_vmem, sem)
        load.start()
        load.wait()
        gather(tok_ref, xg_even)

    new_expert = jnp.logical_or(i == 0, be_ref[i] != be_ref[jnp.maximum(i - 1, 0)])

    @pl.when(new_expert)
    def _():
        wg_bf[...] = wg_ref[0].astype(BF16)
        wu_bf[...] = wu_ref[0].astype(BF16)
        wd_bf[...] = wd_ref[0].astype(BF16)

    def block(cur, nxt):
        kh = D_MODEL // 2
        gather(tok_next_ref, nxt)
        x_lo, x_hi = _unpack_bf16_pairs(_load_row_tiles(cur, ROW_BLOCK))
        gate = _dot(x_lo, wg_bf[:kh, :]) + _dot(x_hi, wg_bf[kh:, :])
        up = _dot(x_lo, wu_bf[:kh, :]) + _dot(x_hi, wu_bf[kh:, :])
        hid = (_silu(gate) * up).astype(BF16)
        for s in range(2):
            y = _dot(hid, wd_bf[:, s * kh:(s + 1) * kh])
            for c in range(ROW_TILE // 2):
                ys_ref[pl.ds(s * (ROW_TILE // 2) + c, ROW_BLOCK, stride=ROW_TILE), :] = y[:, c * LANES:(c + 1) * LANES]

    used = i < nused_ref[0]
    odd = jnp.bitwise_and(i, 1) == 1
    pl.when(jnp.logical_and(used, jnp.logical_not(odd)))(lambda: block(xg_even, xg_odd))
    pl.when(jnp.logical_and(used, odd))(lambda: block(xg_odd, xg_even))

    @pl.when(i >= nused_ref[0])
    def _():
        ys_ref[...] = jnp.zeros_like(ys_ref)


def _experts(hn, tok_of_row, block_expert, n_used, w_gate, w_up, w_down):
    n_rows = tok_of_row.shape[0]
    nb = n_rows // ROW_BLOCK
    tok3 = tok_of_row.reshape(nb, 1, ROW_BLOCK)
    smem = lambda f: pl.BlockSpec((1, 1, ROW_BLOCK), f, memory_space=pltpu.SMEM)
    grid_spec = pltpu.PrefetchScalarGridSpec(
        num_scalar_prefetch=2,
        grid=(nb,),
        in_specs=[
            smem(lambda i, be, nu: (i, 0, 0)),
            smem(lambda i, be, nu: (jnp.minimum(i + 1, nb - 1), 0, 0)),
            pl.BlockSpec(memory_space=pl.ANY),
            pl.BlockSpec((1, D_MODEL, MOE_FF), lambda i, be, nu: (be[i], 0, 0)),
            pl.BlockSpec((1, D_MODEL, MOE_FF), lambda i, be, nu: (be[i], 0, 0)),
            pl.BlockSpec((1, MOE_FF, D_MODEL), lambda i, be, nu: (be[i], 0, 0)),
        ],
        out_specs=pl.BlockSpec((ROW_BLOCK * ROW_TILE, LANES), lambda i, be, nu: (i, 0)),
        scratch_shapes=[pltpu.VMEM(hn.shape, jnp.uint32),
                        pltpu.VMEM((ROW_BLOCK * X_TILE, LANES), jnp.uint32),
                        pltpu.VMEM((ROW_BLOCK * X_TILE, LANES), jnp.uint32),
                        pltpu.VMEM((D_MODEL, MOE_FF), BF16), pltpu.VMEM((D_MODEL, MOE_FF), BF16),
                        pltpu.VMEM((MOE_FF, D_MODEL), BF16),
                        pltpu.SemaphoreType.DMA],
    )
    return pl.pallas_call(
        _experts_kernel,
        grid_spec=grid_spec,
        out_shape=jax.ShapeDtypeStruct((n_rows * ROW_TILE, LANES), F32),
        compiler_params=_params(("arbitrary",)),
        name="experts",
    )(block_expert, n_used, tok3, tok3, hn, w_gate, w_up, w_down)


COMB_TM = 256


def _combine_kernel(dest_ref, dest_next_ref, x1_ref, gate_ref, ys_ref, out_ref, ybuf, sems):
    g = pl.program_id(0)

    def row_copy(d, slot, k, r):
        src = ys_ref.at[pl.ds(pl.multiple_of(d * ROW_TILE, ROW_TILE), ROW_TILE), :]
        dst = ybuf.at[slot, k, pl.ds(pl.multiple_of(r * ROW_TILE, ROW_TILE), ROW_TILE), :]
        return pltpu.make_async_copy(src, dst, sems.at[slot])

    def gather(dref, base, slot):
        def issue(r, c):
            row_copy(dref[0, 0, base + 2 * r], slot, 0, r).start(priority=0)
            row_copy(dref[0, 0, base + 2 * r + 1], slot, 1, r).start(priority=1)
            return c
        lax.fori_loop(0, COMB_TM, issue, 0, unroll=DMA_UNROLL)

    def finish(slot):
        def drain(r, c):
            row_copy(0, slot, 0, 0).wait()
            return c
        lax.fori_loop(0, 2 * COMB_TM, drain, 0, unroll=DMA_UNROLL)
        rows = slice(slot * COMB_TM, (slot + 1) * COMB_TM)
        gate = gate_ref[:, rows].T
        w0 = gate[:, 0:1]
        w1 = gate[:, 1:2]
        y0 = _load_row_tiles(ybuf.at[slot, 0], COMB_TM)
        y1 = _load_row_tiles(ybuf.at[slot, 1], COMB_TM)
        out_ref[rows, :] = x1_ref[rows, :] + (y0 * w0 + y1 * w1)

    @pl.when(g == 0)
    def _():
        gather(dest_ref, 0, 0)

    gather(dest_ref, 2 * COMB_TM, 1)
    finish(0)

    @pl.when(g + 1 < pl.num_programs(0))
    def _():
        gather(dest_next_ref, 0, 0)

    finish(1)


def _combine(x1, gates, ys, dest):
    t = x1.shape[0]
    ng = t // (2 * COMB_TM)
    dest3 = dest.reshape(ng, 1, 4 * COMB_TM)
    smem = lambda f: pl.BlockSpec((1, 1, 4 * COMB_TM), f, memory_space=pltpu.SMEM)
    return pl.pallas_call(
        _combine_kernel,
        grid=(ng,),
        in_specs=[smem(lambda i: (i, 0, 0)),
                  smem(lambda i: (jnp.minimum(i + 1, ng - 1), 0, 0)),
                  pl.BlockSpec((2 * COMB_TM, D_MODEL), lambda i: (i, 0)),
                  pl.BlockSpec((8, 2 * COMB_TM), lambda i: (0, i)),
                  pl.BlockSpec(memory_space=pl.ANY)],
        out_specs=pl.BlockSpec((2 * COMB_TM, D_MODEL), lambda i: (i, 0)),
        out_shape=jax.ShapeDtypeStruct((t, D_MODEL), F32),
        scratch_shapes=[pltpu.VMEM((2, 2, COMB_TM * ROW_TILE, LANES), F32),
                        pltpu.SemaphoreType.DMA((2,))],
        compiler_params=_params(("arbitrary",)),
        name="combine",
    )(dest3, dest3, x1, gates, ys)


def _layer(x, mem, attn_norm_g, mem_norm_g, w_in, w_gla_gk, b_gla_gk, gla_out_norm_g,
           moba_q_norm_g, moba_k_norm_g, w_mem_kv, mem_q_norm_g, mem_k_norm_g, w_out,
           ffn_norm_g, w_router_group, b_router_group, w_router_expert, b_router_expert,
           w_gate, w_up, w_down):
    b, s, d = x.shape
    t = b * s
    x2 = x.reshape(t, d)
    row = lambda v: v.reshape(1, -1).astype(F32)
    tile_heads = lambda v: jnp.tile(v.astype(F32), HEADS).reshape(1, QK_W)
    hid = np.arange(QK_W) // DH
    seg = jnp.asarray((hid[:, None] == hid[None, :]) / DH, BF16)

    w_in_p = jnp.concatenate([w_in[:, :1536], w_in[:, 1552:], w_in[:, 1536:1552]], axis=1).astype(BF16)
    wr = jnp.concatenate([w_router_group,
                          jnp.transpose(w_router_expert, (1, 0, 2)).reshape(d, N_EXPERTS),
                          jnp.zeros((d, LANES - N_GROUPS - N_EXPERTS), F32)], axis=1).astype(BF16)
    br = jnp.concatenate([b_router_group, b_router_expert.reshape(N_EXPERTS),
                          jnp.zeros((LANES - N_GROUPS - N_EXPERTS,), F32)]).reshape(1, LANES)

    memk, memvt = _mem_kv(mem, row(mem_norm_g), w_mem_kv.astype(BF16), tile_heads(mem_k_norm_g), seg)
    qk, gv, gr, glr, mq, mk, mvt, kmean, cq = _in_proj(
        x2, row(attn_norm_g), w_in_p, seg, tile_heads(moba_q_norm_g), tile_heads(moba_k_norm_g),
        tile_heads(mem_q_norm_g))
    o_gla = _gla(qk, gv, gr, glr, w_gla_gk.astype(BF16), row(b_gla_gk), row(gla_out_norm_g), b, s)
    o_moba, o_mem = _moba_mem(mq, mk, mvt, kmean, cq, memk, memvt, b, s)
    x1, hn, e_ids, gates, counts = _out_router(
        x2, o_gla.reshape(t, GLA_V_W), o_moba.reshape(t, QK_W), o_mem.reshape(t, QK_W),
        w_out.astype(BF16), row(ffn_norm_g), wr, br)

    counts = counts[N_GROUPS:N_GROUPS + N_EXPERTS, 0]
    padded = (counts + ROW_BLOCK - 1) // ROW_BLOCK * ROW_BLOCK
    pends = jnp.cumsum(padded)
    pstarts = pends - padded
    onehot = e_ids[:2, :, None] == jnp.arange(N_EXPERTS, dtype=jnp.int32)
    dest = (jnp.sum(jnp.where(onehot, pstarts, 0), axis=-1) + e_ids[2:4]).astype(jnp.int32).T.reshape(-1)
    n_rows = (t * 2 + N_EXPERTS * (ROW_BLOCK - 1) + ROW_BLOCK - 1) // ROW_BLOCK * ROW_BLOCK
    nb = n_rows // ROW_BLOCK
    block_start = jnp.arange(nb, dtype=jnp.int32) * ROW_BLOCK
    block_expert = jnp.minimum(jnp.sum(block_start[:, None] >= pends[None, :], axis=1),
                               N_EXPERTS - 1).astype(jnp.int32)
    n_used = (pends[-1] // ROW_BLOCK).astype(jnp.int32).reshape(1)

    pad_lo = jnp.concatenate([pstarts + counts, pends[-1:]]).astype(jnp.int32)
    pad_hi = jnp.concatenate([pends, jnp.full((1,), n_rows)]).astype(jnp.int32)
    tok_of_row = _invert(dest, pad_lo, pad_hi, n_rows)
    ys = _experts(hn, tok_of_row, block_expert, n_used, w_gate, w_up, w_down)
    out = _combine(x1, gates, ys, dest)
    return out.reshape(b, s, d)


def kernel(x, mem, attn_norm_g, mem_norm_g, w_in, w_gla_gk, b_gla_gk, gla_out_norm_g, moba_q_norm_g, moba_k_norm_g, w_mem_kv, mem_q_norm_g, mem_k_norm_g, w_out, ffn_norm_g, w_router_group, b_router_group, w_router_expert, b_router_expert, w_gate, w_up, w_down):
    depth = w_in.shape[0]
    for l in range(depth):
        x = _layer(x, mem, attn_norm_g[l], mem_norm_g[l], w_in[l], w_gla_gk[l], b_gla_gk[l],
                   gla_out_norm_g[l], moba_q_norm_g[l], moba_k_norm_g[l], w_mem_kv[l],
                   mem_q_norm_g[l], mem_k_norm_g[l], w_out[l], ffn_norm_g[l], w_router_group[l],
                   b_router_group[l], w_router_expert[l], b_router_expert[l],
                   w_gate[l], w_up[l], w_down[l])
    return x
```

```python
import functools

import jax
import jax.numpy as jnp
import numpy as np
from jax import lax
from jax.experimental import pallas as pl
from jax.experimental.pallas import tpu as pltpu

F32 = jnp.float32
BF16 = jnp.bfloat16
EPS = 1e-6
NEG_INF = float("-inf")

D_MODEL = 1024
N_MEM = 256
HEADS = 4
DH = 64
GLA_DV = 128
GLA_RANK = 16
GLA_GATE_NORMALIZER = 16.0
GLA_CHUNK = 64
MOBA_BLOCK = 256
MOBA_TOPK = 3
QK_W = HEADS * DH
GLA_V_W = HEADS * GLA_DV
N_GROUPS = 4
EXPERTS_PER_GROUP = 8
N_EXPERTS = N_GROUPS * EXPERTS_PER_GROUP
MOE_FF = 512
LANES = 128
ROW_BLOCK = 256

VMEM_LIMIT = 56 * 1024 * 1024


def _params(sem):
    return pltpu.CompilerParams(dimension_semantics=sem, vmem_limit_bytes=VMEM_LIMIT)


def _nt(a, b):
    return lax.dot_general(a, b, (((1,), (1,)), ((), ())), preferred_element_type=F32)


def _tn(a, b):
    return lax.dot_general(a, b, (((0,), (0,)), ((), ())), preferred_element_type=F32)


def _dot(a, b):
    return jnp.dot(a, b, preferred_element_type=F32)


def _rms_rows(x, g):
    ms = jnp.mean(x * x, axis=-1, keepdims=True)
    return x * lax.rsqrt(ms + EPS) * g


def _split3(x):
    h1 = x.astype(BF16)
    r1 = x - h1.astype(F32)
    h2 = r1.astype(BF16)
    h3 = (r1 - h2.astype(F32)).astype(BF16)
    return h1, h2, h3


def _head_mean_sq(x, seg):
    sq = x * x
    hi = sq.astype(BF16)
    lo = (sq - hi.astype(F32)).astype(BF16)
    return _dot(hi, seg) + _dot(lo, seg)


def _silu(x):
    return x * (1.0 / (1.0 + jnp.exp(-x)))


VT_ROWS = DH + 16
VT_ALL = HEADS * VT_ROWS


ALIBI_SLOPES = tuple(2.0 ** (-8.0 * (h + 1) / HEADS) for h in range(HEADS))


def _vt_with_ones(v, key_slopes=None):
    n = v.shape[0]
    vt = v.T
    ones = jnp.ones((VT_ROWS - DH, n), F32)
    pos = lax.broadcasted_iota(jnp.int32, (1, n), 1).astype(F32)
    parts = []
    for h in range(HEADS):
        scale = 1.0 if key_slopes is None else jnp.exp(key_slopes[h] * pos)
        parts += [vt[h * DH:(h + 1) * DH] * scale, ones * scale]
    return jnp.concatenate(parts, axis=0).astype(BF16)


ROW_TILE = D_MODEL // LANES
X_TILE = ROW_TILE // 2


def _store_row_tiles(ref, x):
    n, tiles = x.shape[0], x.shape[1] // LANES
    for c in range(tiles):
        ref[pl.ds(c, n, stride=tiles), :] = x[:, c * LANES:(c + 1) * LANES]


def _load_row_tiles(ref, n):
    tiles = ref.shape[0] // n
    return jnp.concatenate([ref[pl.ds(c, n, stride=tiles), :] for c in range(tiles)], axis=1)


def _pack_bf16_pairs(x):
    w = x.shape[1] // 2
    bits = lambda t: lax.bitcast_convert_type(t.astype(BF16).astype(F32), jnp.uint32)
    return (bits(x[:, w:]) & jnp.uint32(0xFFFF0000)) | (bits(x[:, :w]) >> 16)


def _unpack_bf16_pairs(words):
    lo = lax.bitcast_convert_type(words << 16, F32).astype(BF16)
    hi = lax.bitcast_convert_type(words & jnp.uint32(0xFFFF0000), F32).astype(BF16)
    return lo, hi


def _mem_kv_kernel(mem_ref, g_ref, w_ref, gk_ref, seg_ref, k_ref, vt_ref):
    h = _rms_rows(mem_ref[0], g_ref[...]).astype(BF16)
    kv = _dot(h, w_ref[...])
    k = kv[:, :QK_W]
    kn = k * lax.rsqrt(_head_mean_sq(k, seg_ref[...]) + EPS) * gk_ref[...]
    k_ref[0] = kn.astype(BF16)
    vt_ref[0] = _vt_with_ones(kv[:, QK_W:])


def _mem_kv(mem, g, w_bf, gk_t, seg):
    b = mem.shape[0]
    return pl.pallas_call(
        _mem_kv_kernel,
        grid=(b,),
        in_specs=[
            pl.BlockSpec((1, N_MEM, D_MODEL), lambda i: (i, 0, 0)),
            pl.BlockSpec((1, D_MODEL), lambda i: (0, 0)),
            pl.BlockSpec((D_MODEL, 2 * QK_W), lambda i: (0, 0)),
            pl.BlockSpec((1, QK_W), lambda i: (0, 0)),
            pl.BlockSpec((QK_W, QK_W), lambda i: (0, 0)),
        ],
        out_specs=[
            pl.BlockSpec((1, N_MEM, QK_W), lambda i: (i, 0, 0)),
            pl.BlockSpec((1, VT_ALL, N_MEM), lambda i: (i, 0, 0)),
        ],
        out_shape=[
            jax.ShapeDtypeStruct((b, N_MEM, QK_W), BF16),
            jax.ShapeDtypeStruct((b, VT_ALL, N_MEM), BF16),
        ],
        compiler_params=_params(("arbitrary",)),
        name="mem_kv",
    )(mem, g, w_bf, gk_t, seg)


IN_TM = 1024
_C_QK, _C_V, _C_R, _C_MQ, _C_MK, _C_MV, _C_CQ, _C_LR = 0, 512, 1024, 1536, 1792, 2048, 2304, 2560
D_IN = 2576


def _in_proj_kernel(x_ref, g_ref, w_ref, seg_ref, gq_ref, gk_ref, gc_ref,
                    qk_ref, v_ref, r_ref, lr_ref, mq_ref, mk_ref, mvt_ref, kmean_ref, cq_ref):
    h = _rms_rows(x_ref[...], g_ref[...]).astype(BF16)
    p = _dot(h, w_ref[...])
    seg = seg_ref[...]
    qk_ref[...] = p[:, _C_QK:_C_V]
    v_ref[...] = p[:, _C_V:_C_R].astype(BF16)
    r_ref[...] = p[:, _C_R:_C_MQ]
    lr_ref[...] = p[:, _C_LR:D_IN]

    def head_norm(t, gain):
        return t * lax.rsqrt(_head_mean_sq(t, seg) + EPS) * gain

    scale = DH ** -0.5
    mq_ref[...] = (head_norm(p[:, _C_MQ:_C_MK], gq_ref[...]) * scale).astype(BF16)
    cq_ref[...] = (head_norm(p[:, _C_CQ:_C_LR], gc_ref[...]) * scale).astype(BF16)
    kn = head_norm(p[:, _C_MK:_C_MV], gk_ref[...])
    mk_ref[...] = kn.astype(BF16)
    mv = p[:, _C_MV:_C_CQ]
    for j in range(IN_TM // MOBA_BLOCK):
        rows = slice(j * MOBA_BLOCK, (j + 1) * MOBA_BLOCK)
        kmean_ref[0, j:j + 1, :] = jnp.mean(kn[rows], axis=0, keepdims=True)
        mvt_ref[j] = _vt_with_ones(mv[rows], ALIBI_SLOPES)


def _in_proj(x2, g, w_bf, seg, gq_t, gk_t, gc_t):
    t = x2.shape[0]
    nt = t // IN_TM
    nb = IN_TM // MOBA_BLOCK
    row = lambda w: pl.BlockSpec((IN_TM, w), lambda i: (i, 0))
    const = lambda a, b: pl.BlockSpec((a, b), lambda i: (0, 0))
    return pl.pallas_call(
        _in_proj_kernel,
        grid=(nt,),
        in_specs=[row(D_MODEL), const(1, D_MODEL), const(D_MODEL, D_IN), const(QK_W, QK_W),
                  const(1, QK_W), const(1, QK_W), const(1, QK_W)],
        out_specs=[row(2 * QK_W), row(GLA_V_W), row(GLA_V_W), row(GLA_RANK), row(QK_W), row(QK_W),
                   pl.BlockSpec((nb, VT_ALL, MOBA_BLOCK), lambda i: (i, 0, 0)),
                   pl.BlockSpec((1, nb, QK_W), lambda i: (i, 0, 0)),
                   row(QK_W)],
        out_shape=[
            jax.ShapeDtypeStruct((t, 2 * QK_W), F32),
            jax.ShapeDtypeStruct((t, GLA_V_W), BF16),
            jax.ShapeDtypeStruct((t, GLA_V_W), F32),
            jax.ShapeDtypeStruct((t, GLA_RANK), F32),
            jax.ShapeDtypeStruct((t, QK_W), BF16),
            jax.ShapeDtypeStruct((t, QK_W), BF16),
            jax.ShapeDtypeStruct((t // MOBA_BLOCK, VT_ALL, MOBA_BLOCK), BF16),
            jax.ShapeDtypeStruct((nt, nb, QK_W), F32),
            jax.ShapeDtypeStruct((t, QK_W), BF16),
        ],
        compiler_params=_params(("arbitrary",)),
        name="in_proj",
    )(x2, g, w_bf, seg, gq_t, gk_t, gc_t)


GLA_TC = 256


def _gla_kernel(qk_ref, v_ref, r_ref, lr_ref, wgk_ref, bgk_ref, gn_ref, tri_ref, ones_ref,
                o_ref, st_ref):
    @pl.when(pl.program_id(1) == 0)
    def _():
        st_ref[...] = jnp.zeros_like(st_ref)

    qk = qk_ref[0]
    q = qk[:, :QK_W]
    k = qk[:, QK_W:]
    gk = _dot(lr_ref[0].astype(BF16), wgk_ref[...]) + bgk_ref[...]
    g = -(jnp.maximum(-gk, 0.0) + jnp.log1p(jnp.exp(-jnp.abs(gk)))) / GLA_GATE_NORMALIZER
    g1, g2, g3 = _split3(g)
    tri = tri_ref[...]
    ones = ones_ref[...]
    cum = _dot(tri, g1) + _dot(tri, g2) + _dot(tri, g3)
    tot = _dot(ones, g1) + _dot(ones, g2) + _dot(ones, g3)
    q_dec = (q * (DH ** -0.5) * jnp.exp(cum)).astype(BF16)
    k_inv = (k * jnp.exp(-cum)).astype(BF16)
    k_end = (k * jnp.exp(tot - cum)).astype(BF16)
    decay = jnp.exp(tot)

    lane_head = lax.broadcasted_iota(jnp.int32, (GLA_TC, QK_W), 1) // DH
    row_t = lax.broadcasted_iota(jnp.int32, (GLA_TC, GLA_TC), 0)
    col_t = lax.broadcasted_iota(jnp.int32, (GLA_TC, GLA_TC), 1)
    causal = (row_t >= col_t) & (row_t // GLA_CHUNK == col_t // GLA_CHUNK)
    same_head = (lax.broadcasted_iota(jnp.int32, (GLA_V_W, QK_W), 0) // GLA_DV
                 == lax.broadcasted_iota(jnp.int32, (GLA_V_W, QK_W), 1) // DH)
    gain = gn_ref[...]

    v = v_ref[0]
    chunks = [slice(c * GLA_CHUNK, (c + 1) * GLA_CHUNK) for c in range(GLA_TC // GLA_CHUNK)]
    d_st = [jnp.where(same_head, _tn(v[rows], k_end[rows]), 0.0) for rows in chunks]
    states = [st_ref[...]]
    for c, rows in enumerate(chunks):
        states.append(states[c] * decay[rows.start:rows.start + 1, :] + d_st[c])
    st_ref[...] = states[-1]
    o_inter = jnp.concatenate([_nt(q_dec[rows], states[c].astype(BF16)) for c, rows in enumerate(chunks)],
                              axis=0)
    q_stack = jnp.concatenate([jnp.where(lane_head == h, q_dec, jnp.zeros_like(q_dec)) for h in range(HEADS)],
                              axis=0)
    a_all = _nt(q_stack, k_inv)
    outs = []
    for h in range(HEADS):
        a = jnp.where(causal, a_all[h * GLA_TC:(h + 1) * GLA_TC], 0.0).astype(BF16)
        vs = slice(h * GLA_DV, (h + 1) * GLA_DV)
        oh = _dot(a, v[:, vs]) + o_inter[:, vs]
        outs.append(_rms_rows(oh, gain) * _silu(r_ref[0, :, vs]))
    o_ref[0] = jnp.concatenate(outs, axis=-1).astype(BF16)


def _gla(qk, v, r, lr, wgk_bf, bgk, gn, b, s):
    ns = s // GLA_TC
    idx = np.arange(GLA_TC)
    same_chunk = (idx[:, None] // GLA_CHUNK) == (idx[None, :] // GLA_CHUNK)
    tri = jnp.asarray(same_chunk & (idx[:, None] >= idx[None, :]), BF16)
    ones = jnp.asarray(same_chunk, BF16)
    seq = lambda w: pl.BlockSpec((1, GLA_TC, w), lambda i, j: (i, j, 0))
    const = lambda a, c: pl.BlockSpec((a, c), lambda i, j: (0, 0))
    return pl.pallas_call(
        _gla_kernel,
        grid=(b, ns),
        in_specs=[seq(2 * QK_W), seq(GLA_V_W), seq(GLA_V_W), seq(GLA_RANK),
                  const(GLA_RANK, QK_W), const(1, QK_W), const(1, GLA_DV),
                  const(GLA_TC, GLA_TC), const(GLA_TC, GLA_TC)],
        out_specs=seq(GLA_V_W),
        out_shape=jax.ShapeDtypeStruct((b, s, GLA_V_W), BF16),
        scratch_shapes=[pltpu.VMEM((GLA_V_W, QK_W), F32)],
        compiler_params=_params(("arbitrary", "arbitrary")),
        name="gla",
    )(qk.reshape(b, s, 2 * QK_W), v.reshape(b, s, GLA_V_W), r.reshape(b, s, GLA_V_W),
      lr.reshape(b, s, GLA_RANK), wgk_bf, bgk, gn, tri, ones)


TQ = MOBA_BLOCK
KV_UNROLL = 8


def _moba_mem_kernel(q_ref, k_ref, vt_ref, kmean_ref, cq_ref, mk_ref, mvt_ref,
                     o_moba_ref, o_mem_ref,
                     qcat_ref, selb_ref, s_ref, *, n_blocks):
    i = pl.program_id(1)
    lane_head = lax.broadcasted_iota(jnp.int32, (TQ, QK_W), 1) // DH
    dist0 = (lax.broadcasted_iota(jnp.int32, (MOBA_BLOCK, TQ), 1)
             - lax.broadcasted_iota(jnp.int32, (MOBA_BLOCK, TQ), 0)).astype(F32)
    slopes = ALIBI_SLOPES
    heads = [slice(h * TQ, (h + 1) * TQ) for h in range(HEADS)]
    vrows = [slice(h * VT_ROWS, (h + 1) * VT_ROWS) for h in range(HEADS)]

    def stack_heads(x):
        return jnp.concatenate([jnp.where(lane_head == h, x, jnp.zeros_like(x)) for h in range(HEADS)], axis=0)

    def finish(accl):
        return accl[:DH] * (1.0 / accl[DH:DH + 1])

    s_all = _nt(mk_ref[0], stack_heads(cq_ref[0]))
    outs = []
    for h in range(HEADS):
        s = s_all[:, heads[h]]
        p = jnp.exp(s - jnp.max(s, axis=0, keepdims=True)).astype(BF16)
        outs.append(finish(_dot(mvt_ref[0, vrows[h], :], p)))
    o_mem_ref[0] = jnp.concatenate(outs, axis=0).T.astype(BF16)

    qcat = stack_heads(q_ref[0])
    qcat_ref[...] = qcat
    k_own = k_ref[0, pl.ds(pl.multiple_of(i * MOBA_BLOCK, MOBA_BLOCK), MOBA_BLOCK), :]
    gate_all = _nt(kmean_ref[0].astype(BF16), qcat)
    s_all = _nt(k_own, qcat)
    blk = lax.broadcasted_iota(jnp.int32, (n_blocks, TQ), 0)
    blk_f = blk.astype(F32)

    tq = lax.broadcasted_iota(jnp.int32, (1, TQ), 1).astype(F32)
    init = []
    for h in range(HEADS):
        gate = jnp.where(blk < i, gate_all[:, heads[h]], NEG_INF)
        chosen = jnp.zeros((n_blocks, TQ), jnp.bool_)
        for r in range(MOBA_TOPK):
            mx = jnp.max(gate, axis=0, keepdims=True)
            first = jnp.min(jnp.where(gate == mx, blk_f, float(n_blocks)), axis=0, keepdims=True)
            hit = blk_f == first
            chosen = chosen | (hit & (mx > NEG_INF))
            gate = jnp.where(hit, NEG_INF, gate)
        selb_ref[h] = jnp.where(chosen, 0.0, NEG_INF)
        aq = -slopes[h] * tq
        s = jnp.where(dist0 >= 0.0, s_all[:, heads[h]], NEG_INF)
        m0 = jnp.max(s, axis=0, keepdims=True) + aq
        p = jnp.exp(s - (m0 - aq)).astype(BF16)
        init += [m0, _dot(vt_ref[i, vrows[h], :], p)]

    def stage_scores(j, slot):
        kj = k_ref[0, pl.ds(pl.multiple_of(j * MOBA_BLOCK, MOBA_BLOCK), MOBA_BLOCK), :]
        s_ref[slot] = _nt(kj, qcat_ref[...])

    stage_scores(0, 0)

    def past_blocks(g, carry, unroll, base):
        carry = list(carry)
        for u in range(unroll):
            j = base + g * unroll + u
            stage_scores(jnp.minimum(j + 1, n_blocks - 1), (u + 1) % 2)
            off = jnp.full((1, TQ), (i - j) * MOBA_BLOCK, jnp.int32).astype(F32) + tq
            for h in range(HEADS):
                m, accl = carry[2 * h:2 * h + 2]
                s1 = s_ref[u % 2, :, heads[h]]
                rb = selb_ref[h, pl.ds(j, 1), :] - slopes[h] * off
                m_new = jnp.maximum(m, jnp.max(s1, axis=0, keepdims=True) + rb)
                p = jnp.exp(s1 - (m_new - rb)).astype(BF16)
                carry[2 * h:2 * h + 2] = [m_new, jnp.exp(m - m_new) * accl + _dot(vt_ref[j, vrows[h], :], p)]
        return tuple(carry)

    n_main = i // KV_UNROLL
    rest = i - n_main * KV_UNROLL
    carry = lax.fori_loop(0, n_main, functools.partial(past_blocks, unroll=KV_UNROLL, base=0), tuple(init))
    final = lax.fori_loop(0, (rest + 1) // 2,
                          functools.partial(past_blocks, unroll=2, base=n_main * KV_UNROLL), carry)
    o_t = jnp.concatenate([finish(final[2 * h + 1]) for h in range(HEADS)], axis=0)
    o_moba_ref[0] = o_t.T.astype(BF16)


def _moba_mem(mq, mk, mvt, kmean, cq, memk, memvt, b, s):
    nq = s // TQ
    n_blocks = s // MOBA_BLOCK
    qspec = pl.BlockSpec((1, TQ, QK_W), lambda i, j: (i, j, 0))
    return pl.pallas_call(
        functools.partial(_moba_mem_kernel, n_blocks=n_blocks),
        grid=(b, nq),
        in_specs=[
            qspec,
            pl.BlockSpec((1, s, QK_W), lambda i, j: (i, 0, 0)),
            pl.BlockSpec((n_blocks, VT_ALL, MOBA_BLOCK), lambda i, j: (i, 0, 0)),
            pl.BlockSpec((1, n_blocks, QK_W), lambda i, j: (i, 0, 0)),
            qspec,
            pl.BlockSpec((1, N_MEM, QK_W), lambda i, j: (i, 0, 0)),
            pl.BlockSpec((1, VT_ALL, N_MEM), lambda i, j: (i, 0, 0)),
        ],
        out_specs=[qspec, qspec],
        out_shape=[jax.ShapeDtypeStruct((b, s, QK_W), BF16), jax.ShapeDtypeStruct((b, s, QK_W), BF16)],
        scratch_shapes=[
            pltpu.VMEM((HEADS * TQ, QK_W), BF16),
            pltpu.VMEM((HEADS, n_blocks, TQ), F32),
            pltpu.VMEM((2, MOBA_BLOCK, HEADS * TQ), F32),
        ],
        compiler_params=_params(("arbitrary", "arbitrary")),
        name="moba_mem",
    )(mq.reshape(b, s, QK_W), mk.reshape(b, s, QK_W), mvt, kmean.reshape(b, n_blocks, QK_W),
      cq.reshape(b, s, QK_W), memk, memvt)


OUT_TM = 1024
ROUTE_ROWS = 40


def _out_router_kernel(x_ref, og_ref, om_ref, oc_ref, w_ref, g_ref, wr_ref, br_ref, tri_ref,
                       x1_ref, h_ref, e_ref, gate_ref, count_ref, carry_ref):
    w = w_ref[...]
    x1 = (x_ref[...] + _dot(og_ref[...], w[:GLA_V_W]) + _dot(om_ref[...], w[GLA_V_W:GLA_V_W + QK_W])
          + _dot(oc_ref[...], w[GLA_V_W + QK_W:]))
    x1_ref[...] = x1
    hn = _rms_rows(x1, g_ref[...])
    _store_row_tiles(h_ref, _pack_bf16_pairs(hn))
    logits = _dot(hn.astype(BF16), wr_ref[...]) + br_ref[...]
    lt = logits.T[:ROUTE_ROWS]
    row = lax.broadcasted_iota(jnp.int32, lt.shape, 0).astype(F32)
    lg = jnp.where(row < N_GROUPS, lt, NEG_INF)
    mg = jnp.max(lg, axis=0, keepdims=True)
    g_sel = jnp.min(jnp.where(lg == mg, row, float(LANES)), axis=0, keepdims=True)
    p_group = 1.0 / jnp.sum(jnp.exp(lg - mg), axis=0, keepdims=True)
    lo = N_GROUPS + g_sel * EXPERTS_PER_GROUP
    le = jnp.where((row >= lo) & (row < lo + EXPERTS_PER_GROUP), lt, NEG_INF)
    m0 = jnp.max(le, axis=0, keepdims=True)
    i0 = jnp.min(jnp.where(le == m0, row, float(LANES)), axis=0, keepdims=True)
    le1 = jnp.where(row == i0, NEG_INF, le)
    m1 = jnp.max(le1, axis=0, keepdims=True)
    i1 = jnp.min(jnp.where(le1 == m1, row, float(LANES)), axis=0, keepdims=True)
    z = jnp.exp(m1 - m0)
    w0 = p_group / (1.0 + z)
    w1 = p_group * z / (1.0 + z)
    pad = jnp.zeros((8 - 2, lt.shape[1]), F32)
    gate_ref[...] = jnp.concatenate([w0, w1, pad], axis=0)

    @pl.when(pl.program_id(0) == 0)
    def _():
        carry_ref[...] = jnp.zeros_like(carry_ref)

    oh0 = (row == i0).astype(F32)
    oh1 = (row == i1).astype(F32)
    both = oh0 + oh1
    before = _dot(both.astype(BF16), tri_ref[...]) + carry_ref[:, 0:1]
    r0 = jnp.sum(oh0 * before, axis=0, keepdims=True)
    r1 = jnp.sum(oh1 * before, axis=0, keepdims=True)
    carry_ref[...] = carry_ref[...] + jnp.sum(both, axis=1, keepdims=True)
    count_ref[...] = carry_ref[...].astype(jnp.int32)
    e_ref[...] = jnp.concatenate([i0 - N_GROUPS, i1 - N_GROUPS, r0, r1, pad[:4]], axis=0).astype(jnp.int32)


def _out_router(x2, og, om, oc, w_bf, g, wr_bf, br):
    t = x2.shape[0]
    row = lambda w: pl.BlockSpec((OUT_TM, w), lambda i: (i, 0))
    const = lambda a, b: pl.BlockSpec((a, b), lambda i: (0, 0))
    idx = np.arange(OUT_TM)
    tri = jnp.asarray(idx[:, None] < idx[None, :], BF16)
    col = lambda r: pl.BlockSpec((r, OUT_TM), lambda i: (0, i))
    return pl.pallas_call(
        _out_router_kernel,
        grid=(t // OUT_TM,),
        in_specs=[row(D_MODEL), row(GLA_V_W), row(QK_W), row(QK_W), const(D_MODEL, D_MODEL),
                  const(1, D_MODEL), const(D_MODEL, LANES), const(1, LANES), const(OUT_TM, OUT_TM)],
        out_specs=[row(D_MODEL), pl.BlockSpec((OUT_TM * X_TILE, LANES), lambda i: (i, 0)),
                   col(8), col(8), const(ROUTE_ROWS, LANES)],
        out_shape=[
            jax.ShapeDtypeStruct((t, D_MODEL), F32),
            jax.ShapeDtypeStruct((t * X_TILE, LANES), jnp.uint32),
            jax.ShapeDtypeStruct((8, t), jnp.int32),
            jax.ShapeDtypeStruct((8, t), F32),
            jax.ShapeDtypeStruct((ROUTE_ROWS, LANES), jnp.int32),
        ],
        scratch_shapes=[pltpu.VMEM((ROUTE_ROWS, LANES), F32)],
        compiler_params=_params(("arbitrary",)),
        name="out_router",
    )(x2, og, om, oc, w_bf, g, wr_bf, br, tri)


DMA_UNROLL = 8


def _invert_kernel(dest_ref, pad_lo_ref, pad_hi_ref, tok_ref):
    def pad_segment(e, c):
        hi = pad_hi_ref[e]

        def pad(g, c2):
            for r in range(8):
                tok_ref[hi - 8 * (g + 1) + r] = 0
            return c2
        lax.fori_loop(0, lax.shift_right_logical(hi - pad_lo_ref[e] + 7, 3), pad, 0)
        return c

    lax.fori_loop(0, pad_lo_ref.shape[0], pad_segment, 0)

    def place(a, c):
        tok_ref[dest_ref[a]] = lax.shift_right_logical(a, 1)
        return c

    lax.fori_loop(0, dest_ref.shape[0], place, 0, unroll=DMA_UNROLL)


def _invert(dest, pad_lo, pad_hi, n_rows):
    smem = pl.BlockSpec(memory_space=pltpu.SMEM)
    return pl.pallas_call(
        _invert_kernel,
        in_specs=[smem, smem, smem],
        out_specs=smem,
        out_shape=jax.ShapeDtypeStruct((n_rows,), jnp.int32),
        name="invert",
    )(dest, pad_lo, pad_hi)


def _experts_kernel(be_ref, run_ref, next_ref, nused_ref, tok_ref,
                    hn_hbm, wg_hbm, wu_hbm, wd_hbm, ys_hbm,
                    hn_vmem, xg_even, xg_odd, y_even, y_odd, wg_f, wu_f, wd_f, wg_bf, wu_bf, wd_bf,
                    hn_sem, w_sem, y_sem):
    n_used = nused_ref[0]
    n_blocks = be_ref.shape[0]
    block_rows = ROW_BLOCK * ROW_TILE

    def weight_copies(e, slot):
        return [pltpu.make_async_copy(src.at[e], dst.at[slot], w_sem.at[slot])
                for src, dst in ((wg_hbm, wg_f), (wu_hbm, wu_f), (wd_hbm, wd_f))]

    def y_copy(buf, parity, i):
        dst = ys_hbm.at[pl.ds(pl.multiple_of(i * block_rows, block_rows), block_rows), :]
        return pltpu.make_async_copy(buf, dst, y_sem.at[parity])

    def gather(i, dst):
        base = jnp.minimum(i, n_blocks - 1) * ROW_BLOCK
        for r in range(ROW_BLOCK):
            src = pl.ds(pl.multiple_of(tok_ref[base + r] * X_TILE, X_TILE), X_TILE)
            dst[r * X_TILE:(r + 1) * X_TILE, :] = hn_vmem[src, :]

    def block(i, parity, cur, nxt, ybuf):
        @pl.when(jnp.logical_or(i == 0, be_ref[i] != be_ref[jnp.maximum(i - 1, 0)]))
        def _():
            slot = jnp.bitwise_and(run_ref[i], 1)
            for c in weight_copies(be_ref[i], slot):
                c.wait()
            wg_bf[...] = wg_f[slot].astype(BF16)
            wu_bf[...] = wu_f[slot].astype(BF16)
            wd_bf[...] = wd_f[slot].astype(BF16)

            @pl.when(next_ref[i] < n_used)
            def _():
                for c in weight_copies(be_ref[jnp.minimum(next_ref[i], n_blocks - 1)], 1 - slot):
                    c.start()

        @pl.when(i >= 2)
        def _():
            y_copy(ybuf, parity, i).wait()

        kh = D_MODEL // 2
        gather(i + 1, nxt)
        x_lo, x_hi = _unpack_bf16_pairs(_load_row_tiles(cur, ROW_BLOCK))
        gate = _dot(x_lo, wg_bf[:kh, :]) + _dot(x_hi, wg_bf[kh:, :])
        up = _dot(x_lo, wu_bf[:kh, :]) + _dot(x_hi, wu_bf[kh:, :])
        hid = (_silu(gate) * up).astype(BF16)
        _store_row_tiles(ybuf, _dot(hid, wd_bf[...]))
        y_copy(ybuf, parity, i).start()

    hn_load = pltpu.make_async_copy(hn_hbm, hn_vmem, hn_sem)
    hn_load.start()
    for c in weight_copies(be_ref[0], 0):
        c.start()
    hn_load.wait()
    gather(0, xg_even)

    def pair(p, carry):
        block(2 * p, 0, xg_even, xg_odd, y_even)

        @pl.when(2 * p + 1 < n_used)
        def _():
            block(2 * p + 1, 1, xg_odd, xg_even, y_odd)
        return carry

    lax.fori_loop(0, lax.shift_right_logical(n_used + 1, 1), pair, 0)

    y_copy(y_even, 0, 0).wait()

    @pl.when(n_used >= 2)
    def _():
        y_copy(y_odd, 1, 0).wait()

    y_even[...] = jnp.zeros_like(y_even)

    def zero_start(i, carry):
        y_copy(y_even, 0, i).start()
        return carry

    def zero_wait(i, carry):
        y_copy(y_even, 0, i).wait()
        return carry

    lax.fori_loop(n_used, n_blocks, zero_start, 0)
    lax.fori_loop(n_used, n_blocks, zero_wait, 0)


def _experts(hn, tok_of_row, block_expert, block_run, block_next, n_used, w_gate, w_up, w_down):
    n_rows = tok_of_row.shape[0]
    hbm = pl.BlockSpec(memory_space=pl.ANY)
    grid_spec = pltpu.PrefetchScalarGridSpec(
        num_scalar_prefetch=5,
        grid=(1,),
        in_specs=[hbm, hbm, hbm, hbm],
        out_specs=hbm,
        scratch_shapes=[pltpu.VMEM(hn.shape, jnp.uint32),
                        pltpu.VMEM((ROW_BLOCK * X_TILE, LANES), jnp.uint32),
                        pltpu.VMEM((ROW_BLOCK * X_TILE, LANES), jnp.uint32),
                        pltpu.VMEM((ROW_BLOCK * ROW_TILE, LANES), F32),
                        pltpu.VMEM((ROW_BLOCK * ROW_TILE, LANES), F32),
                        pltpu.VMEM((2, D_MODEL, MOE_FF), F32), pltpu.VMEM((2, D_MODEL, MOE_FF), F32),
                        pltpu.VMEM((2, MOE_FF, D_MODEL), F32),
                        pltpu.VMEM((D_MODEL, MOE_FF), BF16), pltpu.VMEM((D_MODEL, MOE_FF), BF16),
                        pltpu.VMEM((MOE_FF, D_MODEL), BF16),
                        pltpu.SemaphoreType.DMA, pltpu.SemaphoreType.DMA((2,)), pltpu.SemaphoreType.DMA((2,))],
    )
    return pl.pallas_call(
        _experts_kernel,
        grid_spec=grid_spec,
        out_shape=jax.ShapeDtypeStruct((n_rows * ROW_TILE, LANES), F32),
        compiler_params=_params(("arbitrary",)),
        name="experts",
    )(block_expert, block_run, block_next, n_used, tok_of_row, hn, w_gate, w_up, w_down)


COMB_TM = 256


def _combine_kernel(dest_ref, dest_next_ref, x1_ref, gate_ref, ys_ref, out_ref, ybuf, sems):
    g = pl.program_id(0)

    def row_copy(d, slot, k, r):
        src = ys_ref.at[pl.ds(pl.multiple_of(d * ROW_TILE, ROW_TILE), ROW_TILE), :]
        dst = ybuf.at[slot, k, pl.ds(pl.multiple_of(r * ROW_TILE, ROW_TILE), ROW_TILE), :]
        return pltpu.make_async_copy(src, dst, sems.at[slot])

    def gather(dref, base, slot):
        def issue(r, c):
            row_copy(dref[0, 0, base + 2 * r], slot, 0, r).start(priority=0)
            row_copy(dref[0, 0, base + 2 * r + 1], slot, 1, r).start(priority=1)
            return c
        lax.fori_loop(0, COMB_TM, issue, 0, unroll=DMA_UNROLL)

    def finish(slot):
        def drain(r, c):
            row_copy(0, slot, 0, 0).wait()
            return c
        lax.fori_loop(0, 2 * COMB_TM, drain, 0, unroll=DMA_UNROLL)
        rows = slice(slot * COMB_TM, (slot + 1) * COMB_TM)
        gate = gate_ref[:, rows].T
        w0 = gate[:, 0:1]
        w1 = gate[:, 1:2]
        y0 = _load_row_tiles(ybuf.at[slot, 0], COMB_TM)
        y1 = _load_row_tiles(ybuf.at[slot, 1], COMB_TM)
        out_ref[rows, :] = x1_ref[rows, :] + (y0 * w0 + y1 * w1)

    @pl.when(g == 0)
    def _():
        gather(dest_ref, 0, 0)

    gather(dest_ref, 2 * COMB_TM, 1)
    finish(0)

    @pl.when(g + 1 < pl.num_programs(0))
    def _():
        gather(dest_next_ref, 0, 0)

    finish(1)


def _combine(x1, gates, ys, dest):
    t = x1.shape[0]
    ng = t // (2 * COMB_TM)
    dest3 = dest.reshape(ng, 1, 4 * COMB_TM)
    smem = lambda f: pl.BlockSpec((1, 1, 4 * COMB_TM), f, memory_space=pltpu.SMEM)
    return pl.pallas_call(
        _combine_kernel,
        grid=(ng,),
        in_specs=[smem(lambda i: (i, 0, 0)),
                  smem(lambda i: (jnp.minimum(i + 1, ng - 1), 0, 0)),
                  pl.BlockSpec((2 * COMB_TM, D_MODEL), lambda i: (i, 0)),
                  pl.BlockSpec((8, 2 * COMB_TM), lambda i: (0, i)),
                  pl.BlockSpec(memory_space=pl.ANY)],
        out_specs=pl.BlockSpec((2 * COMB_TM, D_MODEL), lambda i: (i, 0)),
        out_shape=jax.ShapeDtypeStruct((t, D_MODEL), F32),
        scratch_shapes=[pltpu.VMEM((2, 2, COMB_TM * ROW_TILE, LANES), F32),
                        pltpu.SemaphoreType.DMA((2,))],
        compiler_params=_params(("arbitrary",)),
        name="combine",
    )(dest3, dest3, x1, gates, ys)


def _layer(x, mem, attn_norm_g, mem_norm_g, w_in, w_gla_gk, b_gla_gk, gla_out_norm_g,
           moba_q_norm_g, moba_k_norm_g, w_mem_kv, mem_q_norm_g, mem_k_norm_g, w_out,
           ffn_norm_g, w_router_group, b_router_group, w_router_expert, b_router_expert,
           w_gate, w_up, w_down):
    b, s, d = x.shape
    t = b * s
    x2 = x.reshape(t, d)
    row = lambda v: v.reshape(1, -1).astype(F32)
    tile_heads = lambda v: jnp.tile(v.astype(F32), HEADS).reshape(1, QK_W)
    hid = np.arange(QK_W) // DH
    seg = jnp.asarray((hid[:, None] == hid[None, :]) / DH, BF16)

    w_in_p = jnp.concatenate([w_in[:, :1536], w_in[:, 1552:], w_in[:, 1536:1552]], axis=1).astype(BF16)
    wr = jnp.concatenate([w_router_group,
                          jnp.transpose(w_router_expert, (1, 0, 2)).reshape(d, N_EXPERTS),
                          jnp.zeros((d, LANES - N_GROUPS - N_EXPERTS), F32)], axis=1).astype(BF16)
    br = jnp.concatenate([b_router_group, b_router_expert.reshape(N_EXPERTS),
                          jnp.zeros((LANES - N_GROUPS - N_EXPERTS,), F32)]).reshape(1, LANES)

    memk, memvt = _mem_kv(mem, row(mem_norm_g), w_mem_kv.astype(BF16), tile_heads(mem_k_norm_g), seg)
    qk, gv, gr, glr, mq, mk, mvt, kmean, cq = _in_proj(
        x2, row(attn_norm_g), w_in_p, seg, tile_heads(moba_q_norm_g), tile_heads(moba_k_norm_g),
        tile_heads(mem_q_norm_g))
    o_gla = _gla(qk, gv, gr, glr, w_gla_gk.astype(BF16), row(b_gla_gk), row(gla_out_norm_g), b, s)
    o_moba, o_mem = _moba_mem(mq, mk, mvt, kmean, cq, memk, memvt, b, s)
    x1, hn, e_ids, gates, counts = _out_router(
        x2, o_gla.reshape(t, GLA_V_W), o_moba.reshape(t, QK_W), o_mem.reshape(t, QK_W),
        w_out.astype(BF16), row(ffn_norm_g), wr, br)

    counts = counts[N_GROUPS:N_GROUPS + N_EXPERTS, 0]
    padded = (counts + ROW_BLOCK - 1) // ROW_BLOCK * ROW_BLOCK
    pends = jnp.cumsum(padded)
    pstarts = pends - padded
    onehot = e_ids[:2, :, None] == jnp.arange(N_EXPERTS, dtype=jnp.int32)
    dest = (jnp.sum(jnp.where(onehot, pstarts, 0), axis=-1) + e_ids[2:4]).astype(jnp.int32).T.reshape(-1)
    n_rows = (t * 2 + N_EXPERTS * (ROW_BLOCK - 1) + ROW_BLOCK - 1) // ROW_BLOCK * ROW_BLOCK
    nb = n_rows // ROW_BLOCK
    block_start = jnp.arange(nb, dtype=jnp.int32) * ROW_BLOCK
    block_expert = jnp.minimum(jnp.sum(block_start[:, None] >= pends[None, :], axis=1),
                               N_EXPERTS - 1).astype(jnp.int32)
    n_used = (pends[-1] // ROW_BLOCK).astype(jnp.int32).reshape(1)
    block_run = (jnp.cumsum(counts > 0) - 1)[block_expert].astype(jnp.int32)
    block_next = (pends[block_expert] // ROW_BLOCK).astype(jnp.int32)

    pad_lo = jnp.concatenate([pstarts + counts, pends[-1:]]).astype(jnp.int32)
    pad_hi = jnp.concatenate([pends, jnp.full((1,), n_rows)]).astype(jnp.int32)
    tok_of_row = _invert(dest, pad_lo, pad_hi, n_rows)
    ys = _experts(hn, tok_of_row, block_expert, block_run, block_next, n_used, w_gate, w_up, w_down)
    out = _combine(x1, gates, ys, dest)
    return out.reshape(b, s, d)


def kernel(x, mem, attn_norm_g, mem_norm_g, w_in, w_gla_gk, b_gla_gk, gla_out_norm_g, moba_q_norm_g, moba_k_norm_g, w_mem_kv, mem_q_norm_g, mem_k_norm_g, w_out, ffn_norm_g, w_router_group, b_router_group, w_router_expert, b_router_expert, w_gate, w_up, w_down):
    depth = w_in.shape[0]
    for l in range(depth):
        x = _layer(x, mem, attn_norm_g[l], mem_norm_g[l], w_in[l], w_gla_gk[l], b_gla_gk[l],
                   gla_out_norm_g[l], moba_q_norm_g[l], moba_k_norm_g[l], w_mem_kv[l],
                   mem_q_norm_g[l], mem_k_norm_g[l], w_out[l], ffn_norm_g[l], w_router_group[l],
                   b_router_group[l], w_router_expert[l], b_router_expert[l],
                   w_gate[l], w_up[l], w_down[l])
    return x
```

```python
import functools

import jax
import jax.numpy as jnp
import numpy as np
from jax import lax
from jax.experimental import pallas as pl
from jax.experimental.pallas import tpu as pltpu

F32 = jnp.float32
BF16 = jnp.bfloat16
EPS = 1e-6
NEG_INF = float("-inf")

D_MODEL = 1024
N_MEM = 256
HEADS = 4
DH = 64
GLA_DV = 128
GLA_RANK = 16
GLA_GATE_NORMALIZER = 16.0
GLA_CHUNK = 64
MOBA_BLOCK = 256
MOBA_TOPK = 3
QK_W = HEADS * DH
GLA_V_W = HEADS * GLA_DV
N_GROUPS = 4
EXPERTS_PER_GROUP = 8
N_EXPERTS = N_GROUPS * EXPERTS_PER_GROUP
MOE_FF = 512
LANES = 128
ROW_BLOCK = 256

VMEM_LIMIT = 56 * 1024 * 1024


def _params(sem):
    return pltpu.CompilerParams(dimension_semantics=sem, vmem_limit_bytes=VMEM_LIMIT)


def _nt(a, b):
    return lax.dot_general(a, b, (((1,), (1,)), ((), ())), preferred_element_type=F32)


def _tn(a, b):
    return lax.dot_general(a, b, (((0,), (0,)), ((), ())), preferred_element_type=F32)


def _dot(a, b):
    return jnp.dot(a, b, preferred_element_type=F32)


def _rms_rows(x, g):
    ms = jnp.mean(x * x, axis=-1, keepdims=True)
    return x * lax.rsqrt(ms + EPS) * g


def _split3(x):
    h1 = x.astype(BF16)
    r1 = x - h1.astype(F32)
    h2 = r1.astype(BF16)
    h3 = (r1 - h2.astype(F32)).astype(BF16)
    return h1, h2, h3


def _head_mean_sq(x, seg):
    sq = x * x
    hi = sq.astype(BF16)
    lo = (sq - hi.astype(F32)).astype(BF16)
    return _dot(hi, seg) + _dot(lo, seg)


def _silu(x):
    return x * (1.0 / (1.0 + jnp.exp(-x)))


VT_ROWS = DH + 16
VT_ALL = HEADS * VT_ROWS


ALIBI_SLOPES = tuple(2.0 ** (-8.0 * (h + 1) / HEADS) for h in range(HEADS))


def _vt_with_ones(v, key_slopes=None):
    n = v.shape[0]
    vt = v.T
    ones = jnp.ones((VT_ROWS - DH, n), F32)
    pos = lax.broadcasted_iota(jnp.int32, (1, n), 1).astype(F32)
    parts = []
    for h in range(HEADS):
        scale = 1.0 if key_slopes is None else jnp.exp(key_slopes[h] * pos)
        parts += [vt[h * DH:(h + 1) * DH] * scale, ones * scale]
    return jnp.concatenate(parts, axis=0).astype(BF16)


ROW_TILE = D_MODEL // LANES
X_TILE = ROW_TILE // 2


def _store_row_tiles(ref, x):
    n, tiles = x.shape[0], x.shape[1] // LANES
    for c in range(tiles):
        ref[pl.ds(c, n, stride=tiles), :] = x[:, c * LANES:(c + 1) * LANES]


def _load_row_tiles(ref, n):
    tiles = ref.shape[0] // n
    return jnp.concatenate([ref[pl.ds(c, n, stride=tiles), :] for c in range(tiles)], axis=1)


def _pack_bf16_pairs(x):
    w = x.shape[1] // 2
    bits = lambda t: lax.bitcast_convert_type(t.astype(BF16).astype(F32), jnp.uint32)
    return (bits(x[:, w:]) & jnp.uint32(0xFFFF0000)) | (bits(x[:, :w]) >> 16)


def _unpack_bf16_pairs(words):
    lo = lax.bitcast_convert_type(words << 16, F32).astype(BF16)
    hi = lax.bitcast_convert_type(words & jnp.uint32(0xFFFF0000), F32).astype(BF16)
    return lo, hi


def _mem_kv_kernel(mem_ref, g_ref, w_ref, gk_ref, seg_ref, k_ref, vt_ref):
    h = _rms_rows(mem_ref[0], g_ref[...]).astype(BF16)
    kv = _dot(h, w_ref[...])
    k = kv[:, :QK_W]
    kn = k * lax.rsqrt(_head_mean_sq(k, seg_ref[...]) + EPS) * gk_ref[...]
    k_ref[0] = kn.astype(BF16)
    vt_ref[0] = _vt_with_ones(kv[:, QK_W:])


def _mem_kv(mem, g, w_bf, gk_t, seg):
    b = mem.shape[0]
    return pl.pallas_call(
        _mem_kv_kernel,
        grid=(b,),
        in_specs=[
            pl.BlockSpec((1, N_MEM, D_MODEL), lambda i: (i, 0, 0)),
            pl.BlockSpec((1, D_MODEL), lambda i: (0, 0)),
            pl.BlockSpec((D_MODEL, 2 * QK_W), lambda i: (0, 0)),
            pl.BlockSpec((1, QK_W), lambda i: (0, 0)),
            pl.BlockSpec((QK_W, QK_W), lambda i: (0, 0)),
        ],
        out_specs=[
            pl.BlockSpec((1, N_MEM, QK_W), lambda i: (i, 0, 0)),
            pl.BlockSpec((1, VT_ALL, N_MEM), lambda i: (i, 0, 0)),
        ],
        out_shape=[
            jax.ShapeDtypeStruct((b, N_MEM, QK_W), BF16),
            jax.ShapeDtypeStruct((b, VT_ALL, N_MEM), BF16),
        ],
        compiler_params=_params(("arbitrary",)),
        name="mem_kv",
    )(mem, g, w_bf, gk_t, seg)


IN_TM = 1024
_C_QK, _C_V, _C_R, _C_MQ, _C_MK, _C_MV, _C_CQ, _C_LR = 0, 512, 1024, 1536, 1792, 2048, 2304, 2560
D_IN = 2576


def _in_proj_kernel(x_ref, g_ref, w_ref, seg_ref, gq_ref, gk_ref, gc_ref,
                    qk_ref, v_ref, r_ref, lr_ref, mq_ref, mk_ref, mvt_ref, kmean_ref, cq_ref):
    h = _rms_rows(x_ref[...], g_ref[...]).astype(BF16)
    p = _dot(h, w_ref[...])
    seg = seg_ref[...]
    qk_ref[...] = p[:, _C_QK:_C_V]
    v_ref[...] = p[:, _C_V:_C_R].astype(BF16)
    r_ref[...] = p[:, _C_R:_C_MQ]
    lr_ref[...] = p[:, _C_LR:D_IN]

    def head_norm(t, gain):
        return t * lax.rsqrt(_head_mean_sq(t, seg) + EPS) * gain

    scale = DH ** -0.5
    mq_ref[...] = (head_norm(p[:, _C_MQ:_C_MK], gq_ref[...]) * scale).astype(BF16)
    cq_ref[...] = (head_norm(p[:, _C_CQ:_C_LR], gc_ref[...]) * scale).astype(BF16)
    kn = head_norm(p[:, _C_MK:_C_MV], gk_ref[...])
    mk_ref[...] = kn.astype(BF16)
    mv = p[:, _C_MV:_C_CQ]
    for j in range(IN_TM // MOBA_BLOCK):
        rows = slice(j * MOBA_BLOCK, (j + 1) * MOBA_BLOCK)
        kmean_ref[0, j:j + 1, :] = jnp.mean(kn[rows], axis=0, keepdims=True)
        mvt_ref[j] = _vt_with_ones(mv[rows], ALIBI_SLOPES)


def _in_proj(x2, g, w_bf, seg, gq_t, gk_t, gc_t):
    t = x2.shape[0]
    nt = t // IN_TM
    nb = IN_TM // MOBA_BLOCK
    row = lambda w: pl.BlockSpec((IN_TM, w), lambda i: (i, 0))
    const = lambda a, b: pl.BlockSpec((a, b), lambda i: (0, 0))
    return pl.pallas_call(
        _in_proj_kernel,
        grid=(nt,),
        in_specs=[row(D_MODEL), const(1, D_MODEL), const(D_MODEL, D_IN), const(QK_W, QK_W),
                  const(1, QK_W), const(1, QK_W), const(1, QK_W)],
        out_specs=[row(2 * QK_W), row(GLA_V_W), row(GLA_V_W), row(GLA_RANK), row(QK_W), row(QK_W),
                   pl.BlockSpec((nb, VT_ALL, MOBA_BLOCK), lambda i: (i, 0, 0)),
                   pl.BlockSpec((1, nb, QK_W), lambda i: (i, 0, 0)),
                   row(QK_W)],
        out_shape=[
            jax.ShapeDtypeStruct((t, 2 * QK_W), F32),
            jax.ShapeDtypeStruct((t, GLA_V_W), BF16),
            jax.ShapeDtypeStruct((t, GLA_V_W), F32),
            jax.ShapeDtypeStruct((t, GLA_RANK), F32),
            jax.ShapeDtypeStruct((t, QK_W), BF16),
            jax.ShapeDtypeStruct((t, QK_W), BF16),
            jax.ShapeDtypeStruct((t // MOBA_BLOCK, VT_ALL, MOBA_BLOCK), BF16),
            jax.ShapeDtypeStruct((nt, nb, QK_W), F32),
            jax.ShapeDtypeStruct((t, QK_W), BF16),
        ],
        compiler_params=_params(("arbitrary",)),
        name="in_proj",
    )(x2, g, w_bf, seg, gq_t, gk_t, gc_t)


GLA_TC = 256


def _gla_kernel(qk_ref, v_ref, r_ref, lr_ref, wgk_ref, bgk_ref, gn_ref, tri_ref, ones_ref,
                o_ref, st_ref):
    @pl.when(pl.program_id(1) == 0)
    def _():
        st_ref[...] = jnp.zeros_like(st_ref)

    qk = qk_ref[0]
    q = qk[:, :QK_W]
    k = qk[:, QK_W:]
    gk = _dot(lr_ref[0].astype(BF16), wgk_ref[...]) + bgk_ref[...]
    g = -(jnp.maximum(-gk, 0.0) + jnp.log1p(jnp.exp(-jnp.abs(gk)))) / GLA_GATE_NORMALIZER
    g1, g2, g3 = _split3(g)
    tri = tri_ref[...]
    ones = ones_ref[...]
    cum = _dot(tri, g1) + _dot(tri, g2) + _dot(tri, g3)
    tot = _dot(ones, g1) + _dot(ones, g2) + _dot(ones, g3)
    q_dec = (q * (DH ** -0.5) * jnp.exp(cum)).astype(BF16)
    k_inv = (k * jnp.exp(-cum)).astype(BF16)
    k_end = (k * jnp.exp(tot - cum)).astype(BF16)
    decay = jnp.exp(tot)

    lane_head = lax.broadcasted_iota(jnp.int32, (GLA_TC, QK_W), 1) // DH
    row_t = lax.broadcasted_iota(jnp.int32, (GLA_TC, GLA_TC), 0)
    col_t = lax.broadcasted_iota(jnp.int32, (GLA_TC, GLA_TC), 1)
    causal = (row_t >= col_t) & (row_t // GLA_CHUNK == col_t // GLA_CHUNK)
    same_head = (lax.broadcasted_iota(jnp.int32, (GLA_V_W, QK_W), 0) // GLA_DV
                 == lax.broadcasted_iota(jnp.int32, (GLA_V_W, QK_W), 1) // DH)
    gain = gn_ref[...]

    v = v_ref[0]
    chunks = [slice(c * GLA_CHUNK, (c + 1) * GLA_CHUNK) for c in range(GLA_TC // GLA_CHUNK)]
    d_st = [jnp.where(same_head, _tn(v[rows], k_end[rows]), 0.0) for rows in chunks]
    states = [st_ref[...]]
    for c, rows in enumerate(chunks):
        states.append(states[c] * decay[rows.start:rows.start + 1, :] + d_st[c])
    st_ref[...] = states[-1]
    o_inter = jnp.concatenate([_nt(q_dec[rows], states[c].astype(BF16)) for c, rows in enumerate(chunks)],
                              axis=0)
    q_stack = jnp.concatenate([jnp.where(lane_head == h, q_dec, jnp.zeros_like(q_dec)) for h in range(HEADS)],
                              axis=0)
    a_all = _nt(q_stack, k_inv)
    outs = []
    for h in range(HEADS):
        a = jnp.where(causal, a_all[h * GLA_TC:(h + 1) * GLA_TC], 0.0).astype(BF16)
        vs = slice(h * GLA_DV, (h + 1) * GLA_DV)
        oh = _dot(a, v[:, vs]) + o_inter[:, vs]
        outs.append(_rms_rows(oh, gain) * _silu(r_ref[0, :, vs]))
    o_ref[0] = jnp.concatenate(outs, axis=-1).astype(BF16)


def _gla(qk, v, r, lr, wgk_bf, bgk, gn, b, s):
    ns = s // GLA_TC
    idx = np.arange(GLA_TC)
    same_chunk = (idx[:, None] // GLA_CHUNK) == (idx[None, :] // GLA_CHUNK)
    tri = jnp.asarray(same_chunk & (idx[:, None] >= idx[None, :]), BF16)
    ones = jnp.asarray(same_chunk, BF16)
    seq = lambda w: pl.BlockSpec((1, GLA_TC, w), lambda i, j: (i, j, 0))
    const = lambda a, c: pl.BlockSpec((a, c), lambda i, j: (0, 0))
    return pl.pallas_call(
        _gla_kernel,
        grid=(b, ns),
        in_specs=[seq(2 * QK_W), seq(GLA_V_W), seq(GLA_V_W), seq(GLA_RANK),
                  const(GLA_RANK, QK_W), const(1, QK_W), const(1, GLA_DV),
                  const(GLA_TC, GLA_TC), const(GLA_TC, GLA_TC)],
        out_specs=seq(GLA_V_W),
        out_shape=jax.ShapeDtypeStruct((b, s, GLA_V_W), BF16),
        scratch_shapes=[pltpu.VMEM((GLA_V_W, QK_W), F32)],
        compiler_params=_params(("arbitrary", "arbitrary")),
        name="gla",
    )(qk.reshape(b, s, 2 * QK_W), v.reshape(b, s, GLA_V_W), r.reshape(b, s, GLA_V_W),
      lr.reshape(b, s, GLA_RANK), wgk_bf, bgk, gn, tri, ones)


TQ = MOBA_BLOCK
KV_UNROLL = 8


def _moba_mem_kernel(q_ref, k_ref, vt_ref, kmean_ref, cq_ref, mk_ref, mvt_ref,
                     o_moba_ref, o_mem_ref,
                     qcat_ref, selb_ref, s_ref, *, n_blocks):
    i = pl.program_id(1)
    lane_head = lax.broadcasted_iota(jnp.int32, (TQ, QK_W), 1) // DH
    dist0 = (lax.broadcasted_iota(jnp.int32, (MOBA_BLOCK, TQ), 1)
             - lax.broadcasted_iota(jnp.int32, (MOBA_BLOCK, TQ), 0)).astype(F32)
    slopes = ALIBI_SLOPES
    heads = [slice(h * TQ, (h + 1) * TQ) for h in range(HEADS)]
    vrows = [slice(h * VT_ROWS, (h + 1) * VT_ROWS) for h in range(HEADS)]

    def stack_heads(x):
        return jnp.concatenate([jnp.where(lane_head == h, x, jnp.zeros_like(x)) for h in range(HEADS)], axis=0)

    def finish(accl):
        return accl[:DH] * (1.0 / accl[DH:DH + 1])

    s_all = _nt(mk_ref[0], stack_heads(cq_ref[0]))
    outs = []
    for h in range(HEADS):
        s = s_all[:, heads[h]]
        p = jnp.exp(s - jnp.max(s, axis=0, keepdims=True)).astype(BF16)
        outs.append(finish(_dot(mvt_ref[0, vrows[h], :], p)))
    o_mem_ref[0] = jnp.concatenate(outs, axis=0).T.astype(BF16)

    qcat = stack_heads(q_ref[0])
    qcat_ref[...] = qcat
    k_own = k_ref[0, pl.ds(pl.multiple_of(i * MOBA_BLOCK, MOBA_BLOCK), MOBA_BLOCK), :]
    gate_all = _nt(kmean_ref[0].astype(BF16), qcat)
    s_all = _nt(k_own, qcat)
    blk = lax.broadcasted_iota(jnp.int32, (n_blocks, TQ), 0)
    blk_f = blk.astype(F32)

    tq = lax.broadcasted_iota(jnp.int32, (1, TQ), 1).astype(F32)
    init = []
    for h in range(HEADS):
        gate = jnp.where(blk < i, gate_all[:, heads[h]], NEG_INF)
        chosen = jnp.zeros((n_blocks, TQ), jnp.bool_)
        for r in range(MOBA_TOPK):
            mx = jnp.max(gate, axis=0, keepdims=True)
            first = jnp.min(jnp.where(gate == mx, blk_f, float(n_blocks)), axis=0, keepdims=True)
            hit = blk_f == first
            chosen = chosen | (hit & (mx > NEG_INF))
            gate = jnp.where(hit, NEG_INF, gate)
        selb_ref[h] = jnp.where(chosen, 0.0, NEG_INF)
        aq = -slopes[h] * tq
        s = jnp.where(dist0 >= 0.0, s_all[:, heads[h]], NEG_INF)
        m0 = jnp.max(s, axis=0, keepdims=True) + aq
        p = jnp.exp(s - (m0 - aq)).astype(BF16)
        init += [m0, _dot(vt_ref[i, vrows[h], :], p)]

    def stage_scores(j, slot):
        kj = k_ref[0, pl.ds(pl.multiple_of(j * MOBA_BLOCK, MOBA_BLOCK), MOBA_BLOCK), :]
        s_ref[slot] = _nt(kj, qcat_ref[...])

    stage_scores(0, 0)

    def past_blocks(g, carry, unroll, base):
        carry = list(carry)
        for u in range(unroll):
            j = base + g * unroll + u
            stage_scores(jnp.minimum(j + 1, n_blocks - 1), (u + 1) % 2)
            off = jnp.full((1, TQ), (i - j) * MOBA_BLOCK, jnp.int32).astype(F32) + tq
            for h in range(HEADS):
                m, accl = carry[2 * h:2 * h + 2]
                s1 = s_ref[u % 2, :, heads[h]]
                rb = selb_ref[h, pl.ds(j, 1), :] - slopes[h] * off
                m_new = jnp.maximum(m, jnp.max(s1, axis=0, keepdims=True) + rb)
                p = jnp.exp(s1 - (m_new - rb)).astype(BF16)
                carry[2 * h:2 * h + 2] = [m_new, jnp.exp(m - m_new) * accl + _dot(vt_ref[j, vrows[h], :], p)]
        return tuple(carry)

    n_main = i // KV_UNROLL
    rest = i - n_main * KV_UNROLL
    carry = lax.fori_loop(0, n_main, functools.partial(past_blocks, unroll=KV_UNROLL, base=0), tuple(init))
    final = lax.fori_loop(0, (rest + 1) // 2,
                          functools.partial(past_blocks, unroll=2, base=n_main * KV_UNROLL), carry)
    o_t = jnp.concatenate([finish(final[2 * h + 1]) for h in range(HEADS)], axis=0)
    o_moba_ref[0] = o_t.T.astype(BF16)


def _moba_mem(mq, mk, mvt, kmean, cq, memk, memvt, b, s):
    nq = s // TQ
    n_blocks = s // MOBA_BLOCK
    qspec = pl.BlockSpec((1, TQ, QK_W), lambda i, j: (i, j, 0))
    return pl.pallas_call(
        functools.partial(_moba_mem_kernel, n_blocks=n_blocks),
        grid=(b, nq),
        in_specs=[
            qspec,
            pl.BlockSpec((1, s, QK_W), lambda i, j: (i, 0, 0)),
            pl.BlockSpec((n_blocks, VT_ALL, MOBA_BLOCK), lambda i, j: (i, 0, 0)),
            pl.BlockSpec((1, n_blocks, QK_W), lambda i, j: (i, 0, 0)),
            qspec,
            pl.BlockSpec((1, N_MEM, QK_W), lambda i, j: (i, 0, 0)),
            pl.BlockSpec((1, VT_ALL, N_MEM), lambda i, j: (i, 0, 0)),
        ],
        out_specs=[qspec, qspec],
        out_shape=[jax.ShapeDtypeStruct((b, s, QK_W), BF16), jax.ShapeDtypeStruct((b, s, QK_W), BF16)],
        scratch_shapes=[
            pltpu.VMEM((HEADS * TQ, QK_W), BF16),
            pltpu.VMEM((HEADS, n_blocks, TQ), F32),
            pltpu.VMEM((2, MOBA_BLOCK, HEADS * TQ), F32),
        ],
        compiler_params=_params(("arbitrary", "arbitrary")),
        name="moba_mem",
    )(mq.reshape(b, s, QK_W), mk.reshape(b, s, QK_W), mvt, kmean.reshape(b, n_blocks, QK_W),
      cq.reshape(b, s, QK_W), memk, memvt)


OUT_TM = 1024
ROUTE_ROWS = 40


def _out_router_kernel(x_ref, og_ref, om_ref, oc_ref, w_ref, g_ref, wr_ref, br_ref, tri_ref,
                       x1_ref, h_ref, e_ref, gate_ref, count_ref, carry_ref):
    w = w_ref[...]
    x1 = (x_ref[...] + _dot(og_ref[...], w[:GLA_V_W]) + _dot(om_ref[...], w[GLA_V_W:GLA_V_W + QK_W])
          + _dot(oc_ref[...], w[GLA_V_W + QK_W:]))
    x1_ref[...] = x1
    hn = _rms_rows(x1, g_ref[...])
    _store_row_tiles(h_ref, _pack_bf16_pairs(hn))
    logits = _dot(hn.astype(BF16), wr_ref[...]) + br_ref[...]
    lt = logits.T[:ROUTE_ROWS]
    row = lax.broadcasted_iota(jnp.int32, lt.shape, 0).astype(F32)
    lg = jnp.where(row < N_GROUPS, lt, NEG_INF)
    mg = jnp.max(lg, axis=0, keepdims=True)
    g_sel = jnp.min(jnp.where(lg == mg, row, float(LANES)), axis=0, keepdims=True)
    p_group = 1.0 / jnp.sum(jnp.exp(lg - mg), axis=0, keepdims=True)
    lo = N_GROUPS + g_sel * EXPERTS_PER_GROUP
    le = jnp.where((row >= lo) & (row < lo + EXPERTS_PER_GROUP), lt, NEG_INF)
    m0 = jnp.max(le, axis=0, keepdims=True)
    i0 = jnp.min(jnp.where(le == m0, row, float(LANES)), axis=0, keepdims=True)
    le1 = jnp.where(row == i0, NEG_INF, le)
    m1 = jnp.max(le1, axis=0, keepdims=True)
    i1 = jnp.min(jnp.where(le1 == m1, row, float(LANES)), axis=0, keepdims=True)
    z = jnp.exp(m1 - m0)
    w0 = p_group / (1.0 + z)
    w1 = p_group * z / (1.0 + z)
    pad = jnp.zeros((8 - 2, lt.shape[1]), F32)
    gate_ref[...] = jnp.concatenate([w0, w1, pad], axis=0)

    @pl.when(pl.program_id(0) == 0)
    def _():
        carry_ref[...] = jnp.zeros_like(carry_ref)

    oh0 = (row == i0).astype(F32)
    oh1 = (row == i1).astype(F32)
    both = oh0 + oh1
    before = _dot(both.astype(BF16), tri_ref[...]) + carry_ref[:, 0:1]
    r0 = jnp.sum(oh0 * before, axis=0, keepdims=True)
    r1 = jnp.sum(oh1 * before, axis=0, keepdims=True)
    carry_ref[...] = carry_ref[...] + jnp.sum(both, axis=1, keepdims=True)
    count_ref[...] = carry_ref[...].astype(jnp.int32)
    e_ref[...] = jnp.concatenate([i0 - N_GROUPS, i1 - N_GROUPS, r0, r1, pad[:4]], axis=0).astype(jnp.int32)


def _out_router(x2, og, om, oc, w_bf, g, wr_bf, br):
    t = x2.shape[0]
    row = lambda w: pl.BlockSpec((OUT_TM, w), lambda i: (i, 0))
    const = lambda a, b: pl.BlockSpec((a, b), lambda i: (0, 0))
    idx = np.arange(OUT_TM)
    tri = jnp.asarray(idx[:, None] < idx[None, :], BF16)
    col = lambda r: pl.BlockSpec((r, OUT_TM), lambda i: (0, i))
    return pl.pallas_call(
        _out_router_kernel,
        grid=(t // OUT_TM,),
        in_specs=[row(D_MODEL), row(GLA_V_W), row(QK_W), row(QK_W), const(D_MODEL, D_MODEL),
                  const(1, D_MODEL), const(D_MODEL, LANES), const(1, LANES), const(OUT_TM, OUT_TM)],
        out_specs=[row(D_MODEL), pl.BlockSpec((OUT_TM * X_TILE, LANES), lambda i: (i, 0)),
                   col(8), col(8), const(ROUTE_ROWS, LANES)],
        out_shape=[
            jax.ShapeDtypeStruct((t, D_MODEL), F32),
            jax.ShapeDtypeStruct((t * X_TILE, LANES), jnp.uint32),
            jax.ShapeDtypeStruct((8, t), jnp.int32),
            jax.ShapeDtypeStruct((8, t), F32),
            jax.ShapeDtypeStruct((ROUTE_ROWS, LANES), jnp.int32),
        ],
        scratch_shapes=[pltpu.VMEM((ROUTE_ROWS, LANES), F32)],
        compiler_params=_params(("arbitrary",)),
        name="out_router",
    )(x2, og, om, oc, w_bf, g, wr_bf, br, tri)


DMA_UNROLL = 8


def _invert_kernel(dest_ref, pad_lo_ref, pad_hi_ref, tok_ref):
    def pad_segment(e, c):
        hi = pad_hi_ref[e]

        def pad(g, c2):
            for r in range(8):
                tok_ref[hi - 8 * (g + 1) + r] = 0
            return c2
        lax.fori_loop(0, lax.shift_right_logical(hi - pad_lo_ref[e] + 7, 3), pad, 0)
        return c

    lax.fori_loop(0, pad_lo_ref.shape[0], pad_segment, 0)

    def place(a, c):
        tok_ref[dest_ref[a]] = lax.shift_right_logical(a, 1)
        return c

    lax.fori_loop(0, dest_ref.shape[0], place, 0, unroll=DMA_UNROLL)


def _invert(dest, pad_lo, pad_hi, n_rows):
    smem = pl.BlockSpec(memory_space=pltpu.SMEM)
    return pl.pallas_call(
        _invert_kernel,
        in_specs=[smem, smem, smem],
        out_specs=smem,
        out_shape=jax.ShapeDtypeStruct((n_rows,), jnp.int32),
        name="invert",
    )(dest, pad_lo, pad_hi)


def _experts_kernel(be_ref, run_ref, next_ref, nused_ref, tok_ref,
                    hn_hbm, wg_hbm, wu_hbm, wd_hbm, ys_hbm,
                    hn_vmem, xg_even, xg_odd, y_even, y_odd, wg_f, wu_f, wd_f, wg_bf, wu_bf, wd_bf,
                    hn_sem, w_sem, y_sem):
    n_used = nused_ref[0]
    n_blocks = be_ref.shape[0]
    block_rows = ROW_BLOCK * ROW_TILE

    def weight_copies(e, slot):
        return [pltpu.make_async_copy(src.at[e], dst.at[slot], w_sem.at[slot])
                for src, dst in ((wg_hbm, wg_f), (wu_hbm, wu_f), (wd_hbm, wd_f))]

    def y_copy(buf, parity, i):
        dst = ys_hbm.at[pl.ds(pl.multiple_of(i * block_rows, block_rows), block_rows), :]
        return pltpu.make_async_copy(buf, dst, y_sem.at[parity])

    def gather(i, dst):
        base = jnp.minimum(i, n_blocks - 1) * ROW_BLOCK
        for r in range(ROW_BLOCK):
            src = pl.ds(pl.multiple_of(tok_ref[base + r] * X_TILE, X_TILE), X_TILE)
            dst[r * X_TILE:(r + 1) * X_TILE, :] = hn_vmem[src, :]

    def block(i, parity, cur, nxt, ybuf):
        @pl.when(jnp.logical_or(i == 0, be_ref[i] != be_ref[jnp.maximum(i - 1, 0)]))
        def _():
            slot = jnp.bitwise_and(run_ref[i], 1)
            for c in weight_copies(be_ref[i], slot):
                c.wait()
            wg_bf[...] = wg_f[slot].astype(BF16)
            wu_bf[...] = wu_f[slot].astype(BF16)
            wd_bf[...] = wd_f[slot].astype(BF16)

            @pl.when(next_ref[i] < n_used)
            def _():
                for c in weight_copies(be_ref[jnp.minimum(next_ref[i], n_blocks - 1)], 1 - slot):
                    c.start()

        @pl.when(i >= 2)
        def _():
            y_copy(ybuf, parity, i).wait()

        kh = D_MODEL // 2
        gather(i + 1, nxt)
        x_lo, x_hi = _unpack_bf16_pairs(_load_row_tiles(cur, ROW_BLOCK))
        gate = _dot(x_lo, wg_bf[:kh, :]) + _dot(x_hi, wg_bf[kh:, :])
        up = _dot(x_lo, wu_bf[:kh, :]) + _dot(x_hi, wu_bf[kh:, :])
        hid = (_silu(gate) * up).astype(BF16)
        _store_row_tiles(ybuf, _dot(hid, wd_bf[...]))
        y_copy(ybuf, parity, i).start()

    hn_load = pltpu.make_async_copy(hn_hbm, hn_vmem, hn_sem)
    hn_load.start()
    for c in weight_copies(be_ref[0], 0):
        c.start()
    hn_load.wait()
    gather(0, xg_even)

    def pair(p, carry):
        block(2 * p, 0, xg_even, xg_odd, y_even)

        @pl.when(2 * p + 1 < n_used)
        def _():
            block(2 * p + 1, 1, xg_odd, xg_even, y_odd)
        return carry

    lax.fori_loop(0, lax.shift_right_logical(n_used + 1, 1), pair, 0)

    y_copy(y_even, 0, 0).wait()

    @pl.when(n_used >= 2)
    def _():
        y_copy(y_odd, 1, 0).wait()

    y_even[...] = jnp.zeros_like(y_even)

    def zero_start(i, carry):
        y_copy(y_even, 0, i).start()
        return carry

    def zero_wait(i, carry):
        y_copy(y_even, 0, i).wait()
        return carry

    lax.fori_loop(n_used, n_blocks, zero_start, 0)
    lax.fori_loop(n_used, n_blocks, zero_wait, 0)


def _experts(hn, tok_of_row, block_expert, block_run, block_next, n_used, w_gate, w_up, w_down):
    n_rows = tok_of_row.shape[0]
    hbm = pl.BlockSpec(memory_space=pl.ANY)
    grid_spec = pltpu.PrefetchScalarGridSpec(
        num_scalar_prefetch=5,
        grid=(1,),
        in_specs=[hbm, hbm, hbm, hbm],
        out_specs=hbm,
        scratch_shapes=[pltpu.VMEM(hn.shape, jnp.uint32),
                        pltpu.VMEM((ROW_BLOCK * X_TILE, LANES), jnp.uint32),
                        pltpu.VMEM((ROW_BLOCK * X_TILE, LANES), jnp.uint32),
                        pltpu.VMEM((ROW_BLOCK * ROW_TILE, LANES), F32),
                        pltpu.VMEM((ROW_BLOCK * ROW_TILE, LANES), F32),
                        pltpu.VMEM((2, D_MODEL, MOE_FF), F32), pltpu.VMEM((2, D_MODEL, MOE_FF), F32),
                        pltpu.VMEM((2, MOE_FF, D_MODEL), F32),
                        pltpu.VMEM((D_MODEL, MOE_FF), BF16), pltpu.VMEM((D_MODEL, MOE_FF), BF16),
                        pltpu.VMEM((MOE_FF, D_MODEL), BF16),
                        pltpu.SemaphoreType.DMA, pltpu.SemaphoreType.DMA((2,)), pltpu.SemaphoreType.DMA((2,))],
    )
    return pl.pallas_call(
        _experts_kernel,
        grid_spec=grid_spec,
        out_shape=jax.ShapeDtypeStruct((n_rows * ROW_TILE, LANES), F32),
        compiler_params=_params(("arbitrary",)),
        name="experts",
    )(block_expert, block_run, block_next, n_used, tok_of_row, hn, w_gate, w_up, w_down)


COMB_TM = 256


def _combine_kernel(dest_ref, dest_next_ref, x1_ref, gate_ref, ys_ref, out_ref, ybuf, sems):
    g = pl.program_id(0)

    def row_copy(d, slot, k, r):
        src = ys_ref.at[pl.ds(pl.multiple_of(d * ROW_TILE, ROW_TILE), ROW_TILE), :]
        dst = ybuf.at[slot, k, pl.ds(pl.multiple_of(r * ROW_TILE, ROW_TILE), ROW_TILE), :]
        return pltpu.make_async_copy(src, dst, sems.at[slot])

    def gather(dref, base, slot):
        def issue(r, c):
            row_copy(dref[0, 0, base + 2 * r], slot, 0, r).start(priority=0)
            row_copy(dref[0, 0, base + 2 * r + 1], slot, 1, r).start(priority=1)
            return c
        lax.fori_loop(0, COMB_TM, issue, 0, unroll=DMA_UNROLL)

    def finish(slot):
        def drain(r, c):
            row_copy(0, slot, 0, 0).wait()
            return c
        lax.fori_loop(0, 2 * COMB_TM, drain, 0, unroll=DMA_UNROLL)
        rows = slice(slot * COMB_TM, (slot + 1) * COMB_TM)
        gate = gate_ref[:, rows].T
        w0 = gate[:, 0:1]
        w1 = gate[:, 1:2]
        y0 = _load_row_tiles(ybuf.at[slot, 0], COMB_TM)
        y1 = _load_row_tiles(ybuf.at[slot, 1], COMB_TM)
        out_ref[rows, :] = x1_ref[rows, :] + (y0 * w0 + y1 * w1)

    @pl.when(g == 0)
    def _():
        gather(dest_ref, 0, 0)

    gather(dest_ref, 2 * COMB_TM, 1)
    finish(0)

    @pl.when(g + 1 < pl.num_programs(0))
    def _():
        gather(dest_next_ref, 0, 0)

    finish(1)


def _combine(x1, gates, ys, dest):
    t = x1.shape[0]
    ng = t // (2 * COMB_TM)
    dest3 = dest.reshape(ng, 1, 4 * COMB_TM)
    smem = lambda f: pl.BlockSpec((1, 1, 4 * COMB_TM), f, memory_space=pltpu.SMEM)
    return pl.pallas_call(
        _combine_kernel,
        grid=(ng,),
        in_specs=[smem(lambda i: (i, 0, 0)),
                  smem(lambda i: (jnp.minimum(i + 1, ng - 1), 0, 0)),
                  pl.BlockSpec((2 * COMB_TM, D_MODEL), lambda i: (i, 0)),
                  pl.BlockSpec((8, 2 * COMB_TM), lambda i: (0, i)),
                  pl.BlockSpec(memory_space=pl.ANY)],
        out_specs=pl.BlockSpec((2 * COMB_TM, D_MODEL), lambda i: (i, 0)),
        out_shape=jax.ShapeDtypeStruct((t, D_MODEL), F32),
        scratch_shapes=[pltpu.VMEM((2, 2, COMB_TM * ROW_TILE, LANES), F32),
                        pltpu.SemaphoreType.DMA((2,))],
        compiler_params=_params(("arbitrary",)),
        name="combine",
    )(dest3, dest3, x1, gates, ys)


def _layer(x, mem, attn_norm_g, mem_norm_g, w_in, w_gla_gk, b_gla_gk, gla_out_norm_g,
           moba_q_norm_g, moba_k_norm_g, w_mem_kv, mem_q_norm_g, mem_k_norm_g, w_out,
           ffn_norm_g, w_router_group, b_router_group, w_router_expert, b_router_expert,
           w_gate, w_up, w_down):
    b, s, d = x.shape
    t = b * s
    x2 = x.reshape(t, d)
    row = lambda v: v.reshape(1, -1).astype(F32)
    tile_heads = lambda v: jnp.tile(v.astype(F32), HEADS).reshape(1, QK_W)
    hid = np.arange(QK_W) // DH
    seg = jnp.asarray((hid[:, None] == hid[None, :]) / DH, BF16)

    w_in_p = jnp.concatenate([w_in[:, :1536], w_in[:, 1552:], w_in[:, 1536:1552]], axis=1).astype(BF16)
    wr = jnp.concatenate([w_router_group,
                          jnp.transpose(w_router_expert, (1, 0, 2)).reshape(d, N_EXPERTS),
                          jnp.zeros((d, LANES - N_GROUPS - N_EXPERTS), F32)], axis=1).astype(BF16)
    br = jnp.concatenate([b_router_group, b_router_expert.reshape(N_EXPERTS),
                          jnp.zeros((LANES - N_GROUPS - N_EXPERTS,), F32)]).reshape(1, LANES)

    memk, memvt = _mem_kv(mem, row(mem_norm_g), w_mem_kv.astype(BF16), tile_heads(mem_k_norm_g), seg)
    qk, gv, gr, glr, mq, mk, mvt, kmean, cq = _in_proj(
        x2, row(attn_norm_g), w_in_p, seg, tile_heads(moba_q_norm_g), tile_heads(moba_k_norm_g),
        tile_heads(mem_q_norm_g))
    o_gla = _gla(qk, gv, gr, glr, w_gla_gk.astype(BF16), row(b_gla_gk), row(gla_out_norm_g), b, s)
    o_moba, o_mem = _moba_mem(mq, mk, mvt, kmean, cq, memk, memvt, b, s)
    x1, hn, e_ids, gates, counts = _out_router(
        x2, o_gla.reshape(t, GLA_V_W), o_moba.reshape(t, QK_W), o_mem.reshape(t, QK_W),
        w_out.astype(BF16), row(ffn_norm_g), wr, br)

    counts = counts[N_GROUPS:N_GROUPS + N_EXPERTS, 0]
    padded = (counts + ROW_BLOCK - 1) // ROW_BLOCK * ROW_BLOCK
    pends = jnp.cumsum(padded)
    pstarts = pends - padded
    onehot = e_ids[:2, :, None] == jnp.arange(N_EXPERTS, dtype=jnp.int32)
    dest = (jnp.sum(jnp.where(onehot, pstarts, 0), axis=-1) + e_ids[2:4]).astype(jnp.int32).T.reshape(-1)
    n_rows = (t * 2 + N_EXPERTS * (ROW_BLOCK - 1) + ROW_BLOCK - 1) // ROW_BLOCK * ROW_BLOCK
    nb = n_rows // ROW_BLOCK
    block_start = jnp.arange(nb, dtype=jnp.int32) * ROW_BLOCK
    block_expert = jnp.minimum(jnp.sum(block_start[:, None] >= pends[None, :], axis=1),
                               N_EXPERTS - 1).astype(jnp.int32)
    n_used = (pends[-1] // ROW_BLOCK).astype(jnp.int32).reshape(1)
    ended = block_start[:, None] >= pends[None, :]
    block_run = jnp.sum(ended & (counts > 0)[None, :], axis=1).astype(jnp.int32)
    block_next = (jnp.min(jnp.where(ended, n_rows, pends[None, :]), axis=1) // ROW_BLOCK).astype(jnp.int32)

    pad_lo = jnp.concatenate([pstarts + counts, pends[-1:]]).astype(jnp.int32)
    pad_hi = jnp.concatenate([pends, jnp.full((1,), n_rows)]).astype(jnp.int32)
    tok_of_row = _invert(dest, pad_lo, pad_hi, n_rows)
    ys = _experts(hn, tok_of_row, block_expert, block_run, block_next, n_used, w_gate, w_up, w_down)
    out = _combine(x1, gates, ys, dest)
    return out.reshape(b, s, d)


def kernel(x, mem, attn_norm_g, mem_norm_g, w_in, w_gla_gk, b_gla_gk, gla_out_norm_g, moba_q_norm_g, moba_k_norm_g, w_mem_kv, mem_q_norm_g, mem_k_norm_g, w_out, ffn_norm_g, w_router_group, b_router_group, w_router_expert, b_router_expert, w_gate, w_up, w_down):
    depth = w_in.shape[0]
    for l in range(depth):
        x = _layer(x, mem, attn_norm_g[l], mem_norm_g[l], w_in[l], w_gla_gk[l], b_gla_gk[l],
                   gla_out_norm_g[l], moba_q_norm_g[l], moba_k_norm_g[l], w_mem_kv[l],
                   mem_q_norm_g[l], mem_k_norm_g[l], w_out[l], ffn_norm_g[l], w_router_group[l],
                   b_router_group[l], w_router_expert[l], b_router_expert[l],
                   w_gate[l], w_up[l], w_down[l])
    return x
```

```python
import functools

import jax
import jax.numpy as jnp
import numpy as np
from jax import lax
from jax.experimental import pallas as pl
from jax.experimental.pallas import tpu as pltpu

F32 = jnp.float32
BF16 = jnp.bfloat16
EPS = 1e-6
NEG_INF = float("-inf")

D_MODEL = 1024
N_MEM = 256
HEADS = 4
DH = 64
GLA_DV = 128
GLA_RANK = 16
GLA_GATE_NORMALIZER = 16.0
GLA_CHUNK = 64
MOBA_BLOCK = 256
MOBA_TOPK = 3
QK_W = HEADS * DH
GLA_V_W = HEADS * GLA_DV
N_GROUPS = 4
EXPERTS_PER_GROUP = 8
N_EXPERTS = N_GROUPS * EXPERTS_PER_GROUP
MOE_FF = 512
LANES = 128
ROW_BLOCK = 256

VMEM_LIMIT = 56 * 1024 * 1024


def _params(sem):
    return pltpu.CompilerParams(dimension_semantics=sem, vmem_limit_bytes=VMEM_LIMIT)


def _nt(a, b):
    return lax.dot_general(a, b, (((1,), (1,)), ((), ())), preferred_element_type=F32)


def _tn(a, b):
    return lax.dot_general(a, b, (((0,), (0,)), ((), ())), preferred_element_type=F32)


def _dot(a, b):
    return jnp.dot(a, b, preferred_element_type=F32)


def _rms_rows(x, g):
    ms = jnp.mean(x * x, axis=-1, keepdims=True)
    return x * lax.rsqrt(ms + EPS) * g


def _split3(x):
    h1 = x.astype(BF16)
    r1 = x - h1.astype(F32)
    h2 = r1.astype(BF16)
    h3 = (r1 - h2.astype(F32)).astype(BF16)
    return h1, h2, h3


def _head_mean_sq(x, seg):
    sq = x * x
    hi = sq.astype(BF16)
    lo = (sq - hi.astype(F32)).astype(BF16)
    return _dot(hi, seg) + _dot(lo, seg)


def _silu(x):
    return x * (1.0 / (1.0 + jnp.exp(-x)))


VT_ROWS = DH + 16
VT_ALL = HEADS * VT_ROWS


ALIBI_SLOPES = tuple(2.0 ** (-8.0 * (h + 1) / HEADS) for h in range(HEADS))


def _vt_with_ones(v, key_slopes=None):
    n = v.shape[0]
    vt = v.T
    ones = jnp.ones((VT_ROWS - DH, n), F32)
    pos = lax.broadcasted_iota(jnp.int32, (1, n), 1).astype(F32)
    parts = []
    for h in range(HEADS):
        scale = 1.0 if key_slopes is None else jnp.exp(key_slopes[h] * pos)
        parts += [vt[h * DH:(h + 1) * DH] * scale, ones * scale]
    return jnp.concatenate(parts, axis=0).astype(BF16)


ROW_TILE = D_MODEL // LANES
X_TILE = ROW_TILE // 2


def _store_row_tiles(ref, x):
    n, tiles = x.shape[0], x.shape[1] // LANES
    for c in range(tiles):
        ref[pl.ds(c, n, stride=tiles), :] = x[:, c * LANES:(c + 1) * LANES]


def _load_row_tiles(ref, n):
    tiles = ref.shape[0] // n
    return jnp.concatenate([ref[pl.ds(c, n, stride=tiles), :] for c in range(tiles)], axis=1)


def _pack_bf16_pairs(x):
    w = x.shape[1] // 2
    bits = lambda t: lax.bitcast_convert_type(t.astype(BF16).astype(F32), jnp.uint32)
    return (bits(x[:, w:]) & jnp.uint32(0xFFFF0000)) | (bits(x[:, :w]) >> 16)


def _unpack_bf16_pairs(words):
    lo = lax.bitcast_convert_type(words << 16, F32).astype(BF16)
    hi = lax.bitcast_convert_type(words & jnp.uint32(0xFFFF0000), F32).astype(BF16)
    return lo, hi


def _mem_kv_kernel(mem_ref, g_ref, w_ref, gk_ref, seg_ref, k_ref, vt_ref):
    h = _rms_rows(mem_ref[0], g_ref[...]).astype(BF16)
    kv = _dot(h, w_ref[...])
    k = kv[:, :QK_W]
    kn = k * lax.rsqrt(_head_mean_sq(k, seg_ref[...]) + EPS) * gk_ref[...]
    k_ref[0] = kn.astype(BF16)
    vt_ref[0] = _vt_with_ones(kv[:, QK_W:])


def _mem_kv(mem, g, w_bf, gk_t, seg):
    b = mem.shape[0]
    return pl.pallas_call(
        _mem_kv_kernel,
        grid=(b,),
        in_specs=[
            pl.BlockSpec((1, N_MEM, D_MODEL), lambda i: (i, 0, 0)),
            pl.BlockSpec((1, D_MODEL), lambda i: (0, 0)),
            pl.BlockSpec((D_MODEL, 2 * QK_W), lambda i: (0, 0)),
            pl.BlockSpec((1, QK_W), lambda i: (0, 0)),
            pl.BlockSpec((QK_W, QK_W), lambda i: (0, 0)),
        ],
        out_specs=[
            pl.BlockSpec((1, N_MEM, QK_W), lambda i: (i, 0, 0)),
            pl.BlockSpec((1, VT_ALL, N_MEM), lambda i: (i, 0, 0)),
        ],
        out_shape=[
            jax.ShapeDtypeStruct((b, N_MEM, QK_W), BF16),
            jax.ShapeDtypeStruct((b, VT_ALL, N_MEM), BF16),
        ],
        compiler_params=_params(("arbitrary",)),
        name="mem_kv",
    )(mem, g, w_bf, gk_t, seg)


IN_TM = 1024
_C_QK, _C_V, _C_R, _C_MQ, _C_MK, _C_MV, _C_CQ, _C_LR = 0, 512, 1024, 1536, 1792, 2048, 2304, 2560
D_IN = 2576


def _in_proj_kernel(x_ref, g_ref, w_ref, seg_ref, gq_ref, gk_ref, gc_ref,
                    qk_ref, v_ref, r_ref, lr_ref, mq_ref, mk_ref, mvt_ref, kmean_ref, cq_ref):
    h = _rms_rows(x_ref[...], g_ref[...]).astype(BF16)
    p = _dot(h, w_ref[...])
    seg = seg_ref[...]
    qk_ref[...] = p[:, _C_QK:_C_V]
    v_ref[...] = p[:, _C_V:_C_R].astype(BF16)
    r_ref[...] = p[:, _C_R:_C_MQ]
    lr_ref[...] = p[:, _C_LR:D_IN]

    def head_norm(t, gain):
        return t * lax.rsqrt(_head_mean_sq(t, seg) + EPS) * gain

    scale = DH ** -0.5
    mq_ref[...] = (head_norm(p[:, _C_MQ:_C_MK], gq_ref[...]) * scale).astype(BF16)
    cq_ref[...] = (head_norm(p[:, _C_CQ:_C_LR], gc_ref[...]) * scale).astype(BF16)
    kn = head_norm(p[:, _C_MK:_C_MV], gk_ref[...])
    mk_ref[...] = kn.astype(BF16)
    mv = p[:, _C_MV:_C_CQ]
    for j in range(IN_TM // MOBA_BLOCK):
        rows = slice(j * MOBA_BLOCK, (j + 1) * MOBA_BLOCK)
        kmean_ref[0, j:j + 1, :] = jnp.mean(kn[rows], axis=0, keepdims=True)
        mvt_ref[j] = _vt_with_ones(mv[rows], ALIBI_SLOPES)


def _in_proj(x2, g, w_bf, seg, gq_t, gk_t, gc_t):
    t = x2.shape[0]
    nt = t // IN_TM
    nb = IN_TM // MOBA_BLOCK
    row = lambda w: pl.BlockSpec((IN_TM, w), lambda i: (i, 0))
    const = lambda a, b: pl.BlockSpec((a, b), lambda i: (0, 0))
    return pl.pallas_call(
        _in_proj_kernel,
        grid=(nt,),
        in_specs=[row(D_MODEL), const(1, D_MODEL), const(D_MODEL, D_IN), const(QK_W, QK_W),
                  const(1, QK_W), const(1, QK_W), const(1, QK_W)],
        out_specs=[row(2 * QK_W), row(GLA_V_W), row(GLA_V_W), row(GLA_RANK), row(QK_W), row(QK_W),
                   pl.BlockSpec((nb, VT_ALL, MOBA_BLOCK), lambda i: (i, 0, 0)),
                   pl.BlockSpec((1, nb, QK_W), lambda i: (i, 0, 0)),
                   row(QK_W)],
        out_shape=[
            jax.ShapeDtypeStruct((t, 2 * QK_W), F32),
            jax.ShapeDtypeStruct((t, GLA_V_W), BF16),
            jax.ShapeDtypeStruct((t, GLA_V_W), F32),
            jax.ShapeDtypeStruct((t, GLA_RANK), F32),
            jax.ShapeDtypeStruct((t, QK_W), BF16),
            jax.ShapeDtypeStruct((t, QK_W), BF16),
            jax.ShapeDtypeStruct((t // MOBA_BLOCK, VT_ALL, MOBA_BLOCK), BF16),
            jax.ShapeDtypeStruct((nt, nb, QK_W), F32),
            jax.ShapeDtypeStruct((t, QK_W), BF16),
        ],
        compiler_params=_params(("arbitrary",)),
        name="in_proj",
    )(x2, g, w_bf, seg, gq_t, gk_t, gc_t)


GLA_TC = 256


def _gla_kernel(qk_ref, v_ref, r_ref, lr_ref, wgk_ref, bgk_ref, gn_ref, tri_ref, ones_ref,
                o_ref, st_ref):
    @pl.when(pl.program_id(1) == 0)
    def _():
        st_ref[...] = jnp.zeros_like(st_ref)

    qk = qk_ref[0]
    q = qk[:, :QK_W]
    k = qk[:, QK_W:]
    gk = _dot(lr_ref[0].astype(BF16), wgk_ref[...]) + bgk_ref[...]
    g = -(jnp.maximum(-gk, 0.0) + jnp.log1p(jnp.exp(-jnp.abs(gk)))) / GLA_GATE_NORMALIZER
    g1, g2, g3 = _split3(g)
    tri = tri_ref[...]
    ones = ones_ref[...]
    cum = _dot(tri, g1) + _dot(tri, g2) + _dot(tri, g3)
    tot = _dot(ones, g1) + _dot(ones, g2) + _dot(ones, g3)
    q_dec = (q * (DH ** -0.5) * jnp.exp(cum)).astype(BF16)
    k_inv = (k * jnp.exp(-cum)).astype(BF16)
    k_end = (k * jnp.exp(tot - cum)).astype(BF16)
    decay = jnp.exp(tot)

    lane_head = lax.broadcasted_iota(jnp.int32, (GLA_TC, QK_W), 1) // DH
    row_t = lax.broadcasted_iota(jnp.int32, (GLA_TC, GLA_TC), 0)
    col_t = lax.broadcasted_iota(jnp.int32, (GLA_TC, GLA_TC), 1)
    causal = (row_t >= col_t) & (row_t // GLA_CHUNK == col_t // GLA_CHUNK)
    same_head = (lax.broadcasted_iota(jnp.int32, (GLA_V_W, QK_W), 0) // GLA_DV
                 == lax.broadcasted_iota(jnp.int32, (GLA_V_W, QK_W), 1) // DH)
    gain = gn_ref[...]

    v = v_ref[0]
    chunks = [slice(c * GLA_CHUNK, (c + 1) * GLA_CHUNK) for c in range(GLA_TC // GLA_CHUNK)]
    d_st = [jnp.where(same_head, _tn(v[rows], k_end[rows]), 0.0) for rows in chunks]
    states = [st_ref[...]]
    for c, rows in enumerate(chunks):
        states.append(states[c] * decay[rows.start:rows.start + 1, :] + d_st[c])
    st_ref[...] = states[-1]
    o_inter = jnp.concatenate([_nt(q_dec[rows], states[c].astype(BF16)) for c, rows in enumerate(chunks)],
                              axis=0)
    q_stack = jnp.concatenate([jnp.where(lane_head == h, q_dec, jnp.zeros_like(q_dec)) for h in range(HEADS)],
                              axis=0)
    a_all = _nt(q_stack, k_inv)
    outs = []
    for h in range(HEADS):
        a = jnp.where(causal, a_all[h * GLA_TC:(h + 1) * GLA_TC], 0.0).astype(BF16)
        vs = slice(h * GLA_DV, (h + 1) * GLA_DV)
        oh = _dot(a, v[:, vs]) + o_inter[:, vs]
        outs.append(_rms_rows(oh, gain) * _silu(r_ref[0, :, vs]))
    o_ref[0] = jnp.concatenate(outs, axis=-1).astype(BF16)


def _gla(qk, v, r, lr, wgk_bf, bgk, gn, b, s):
    ns = s // GLA_TC
    idx = np.arange(GLA_TC)
    same_chunk = (idx[:, None] // GLA_CHUNK) == (idx[None, :] // GLA_CHUNK)
    tri = jnp.asarray(same_chunk & (idx[:, None] >= idx[None, :]), BF16)
    ones = jnp.asarray(same_chunk, BF16)
    seq = lambda w: pl.BlockSpec((1, GLA_TC, w), lambda i, j: (i, j, 0))
    const = lambda a, c: pl.BlockSpec((a, c), lambda i, j: (0, 0))
    return pl.pallas_call(
        _gla_kernel,
        grid=(b, ns),
        in_specs=[seq(2 * QK_W), seq(GLA_V_W), seq(GLA_V_W), seq(GLA_RANK),
                  const(GLA_RANK, QK_W), const(1, QK_W), const(1, GLA_DV),
                  const(GLA_TC, GLA_TC), const(GLA_TC, GLA_TC)],
        out_specs=seq(GLA_V_W),
        out_shape=jax.ShapeDtypeStruct((b, s, GLA_V_W), BF16),
        scratch_shapes=[pltpu.VMEM((GLA_V_W, QK_W), F32)],
        compiler_params=_params(("arbitrary", "arbitrary")),
        name="gla",
    )(qk.reshape(b, s, 2 * QK_W), v.reshape(b, s, GLA_V_W), r.reshape(b, s, GLA_V_W),
      lr.reshape(b, s, GLA_RANK), wgk_bf, bgk, gn, tri, ones)


TQ = MOBA_BLOCK
KV_UNROLL = 8


def _moba_mem_kernel(q_ref, k_ref, vt_ref, kmean_ref, cq_ref, mk_ref, mvt_ref,
                     o_moba_ref, o_mem_ref,
                     qcat_ref, selb_ref, s_ref, *, n_blocks):
    i = pl.program_id(1)
    lane_head = lax.broadcasted_iota(jnp.int32, (TQ, QK_W), 1) // DH
    dist0 = (lax.broadcasted_iota(jnp.int32, (MOBA_BLOCK, TQ), 1)
             - lax.broadcasted_iota(jnp.int32, (MOBA_BLOCK, TQ), 0)).astype(F32)
    slopes = ALIBI_SLOPES
    heads = [slice(h * TQ, (h + 1) * TQ) for h in range(HEADS)]
    vrows = [slice(h * VT_ROWS, (h + 1) * VT_ROWS) for h in range(HEADS)]

    def stack_heads(x):
        return jnp.concatenate([jnp.where(lane_head == h, x, jnp.zeros_like(x)) for h in range(HEADS)], axis=0)

    def finish(accl):
        return accl[:DH] * (1.0 / accl[DH:DH + 1])

    s_all = _nt(mk_ref[0], stack_heads(cq_ref[0]))
    outs = []
    for h in range(HEADS):
        s = s_all[:, heads[h]]
        p = jnp.exp(s - jnp.max(s, axis=0, keepdims=True)).astype(BF16)
        outs.append(finish(_dot(mvt_ref[0, vrows[h], :], p)))
    o_mem_ref[0] = jnp.concatenate(outs, axis=0).T.astype(BF16)

    qcat = stack_heads(q_ref[0])
    qcat_ref[...] = qcat
    k_own = k_ref[0, pl.ds(pl.multiple_of(i * MOBA_BLOCK, MOBA_BLOCK), MOBA_BLOCK), :]
    gate_all = _nt(kmean_ref[0].astype(BF16), qcat)
    s_all = _nt(k_own, qcat)
    blk = lax.broadcasted_iota(jnp.int32, (n_blocks, TQ), 0)
    blk_f = blk.astype(F32)

    tq = lax.broadcasted_iota(jnp.int32, (1, TQ), 1).astype(F32)
    init = []
    for h in range(HEADS):
        gate = jnp.where(blk < i, gate_all[:, heads[h]], NEG_INF)
        chosen = jnp.zeros((n_blocks, TQ), jnp.bool_)
        for r in range(MOBA_TOPK):
            mx = jnp.max(gate, axis=0, keepdims=True)
            first = jnp.min(jnp.where(gate == mx, blk_f, float(n_blocks)), axis=0, keepdims=True)
            hit = blk_f == first
            chosen = chosen | (hit & (mx > NEG_INF))
            gate = jnp.where(hit, NEG_INF, gate)
        selb_ref[h] = jnp.where(chosen, 0.0, NEG_INF)
        aq = -slopes[h] * tq
        s = jnp.where(dist0 >= 0.0, s_all[:, heads[h]], NEG_INF)
        m0 = jnp.max(s, axis=0, keepdims=True) + aq
        p = jnp.exp(s - (m0 - aq)).astype(BF16)
        init += [m0, _dot(vt_ref[i, vrows[h], :], p)]

    def stage_scores(j, slot):
        kj = k_ref[0, pl.ds(pl.multiple_of(j * MOBA_BLOCK, MOBA_BLOCK), MOBA_BLOCK), :]
        s_ref[slot] = _nt(kj, qcat_ref[...])

    stage_scores(0, 0)

    def past_blocks(g, carry, unroll, base):
        carry = list(carry)
        for u in range(unroll):
            j = base + g * unroll + u
            stage_scores(jnp.minimum(j + 1, n_blocks - 1), (u + 1) % 2)
            off = jnp.full((1, TQ), (i - j) * MOBA_BLOCK, jnp.int32).astype(F32) + tq
            for h in range(HEADS):
                m, accl = carry[2 * h:2 * h + 2]
                s1 = s_ref[u % 2, :, heads[h]]
                rb = selb_ref[h, pl.ds(j, 1), :] - slopes[h] * off
                m_new = jnp.maximum(m, jnp.max(s1, axis=0, keepdims=True) + rb)
                p = jnp.exp(s1 - (m_new - rb)).astype(BF16)
                carry[2 * h:2 * h + 2] = [m_new, jnp.exp(m - m_new) * accl + _dot(vt_ref[j, vrows[h], :], p)]
        return tuple(carry)

    n_main = i // KV_UNROLL
    rest = i - n_main * KV_UNROLL
    carry = lax.fori_loop(0, n_main, functools.partial(past_blocks, unroll=KV_UNROLL, base=0), tuple(init))
    final = lax.fori_loop(0, (rest + 1) // 2,
                          functools.partial(past_blocks, unroll=2, base=n_main * KV_UNROLL), carry)
    o_t = jnp.concatenate([finish(final[2 * h + 1]) for h in range(HEADS)], axis=0)
    o_moba_ref[0] = o_t.T.astype(BF16)


def _moba_mem(mq, mk, mvt, kmean, cq, memk, memvt, b, s):
    nq = s // TQ
    n_blocks = s // MOBA_BLOCK
    qspec = pl.BlockSpec((1, TQ, QK_W), lambda i, j: (i, j, 0))
    return pl.pallas_call(
        functools.partial(_moba_mem_kernel, n_blocks=n_blocks),
        grid=(b, nq),
        in_specs=[
            qspec,
            pl.BlockSpec((1, s, QK_W), lambda i, j: (i, 0, 0)),
            pl.BlockSpec((n_blocks, VT_ALL, MOBA_BLOCK), lambda i, j: (i, 0, 0)),
            pl.BlockSpec((1, n_blocks, QK_W), lambda i, j: (i, 0, 0)),
            qspec,
            pl.BlockSpec((1, N_MEM, QK_W), lambda i, j: (i, 0, 0)),
            pl.BlockSpec((1, VT_ALL, N_MEM), lambda i, j: (i, 0, 0)),
        ],
        out_specs=[qspec, qspec],
        out_shape=[jax.ShapeDtypeStruct((b, s, QK_W), BF16), jax.ShapeDtypeStruct((b, s, QK_W), BF16)],
        scratch_shapes=[
            pltpu.VMEM((HEADS * TQ, QK_W), BF16),
            pltpu.VMEM((HEADS, n_blocks, TQ), F32),
            pltpu.VMEM((2, MOBA_BLOCK, HEADS * TQ), F32),
        ],
        compiler_params=_params(("arbitrary", "arbitrary")),
        name="moba_mem",
    )(mq.reshape(b, s, QK_W), mk.reshape(b, s, QK_W), mvt, kmean.reshape(b, n_blocks, QK_W),
      cq.reshape(b, s, QK_W), memk, memvt)


OUT_TM = 1024
ROUTE_ROWS = 40


def _out_router_kernel(x_ref, og_ref, om_ref, oc_ref, w_ref, g_ref, wr_ref, br_ref, tri_ref,
                       x1_ref, h_ref, e_ref, gate_ref, count_ref, carry_ref):
    w = w_ref[...]
    x1 = (x_ref[...] + _dot(og_ref[...], w[:GLA_V_W]) + _dot(om_ref[...], w[GLA_V_W:GLA_V_W + QK_W])
          + _dot(oc_ref[...], w[GLA_V_W + QK_W:]))
    x1_ref[...] = x1
    hn = _rms_rows(x1, g_ref[...])
    _store_row_tiles(h_ref, _pack_bf16_pairs(hn))
    logits = _dot(hn.astype(BF16), wr_ref[...]) + br_ref[...]
    lt = logits.T[:ROUTE_ROWS]
    row = lax.broadcasted_iota(jnp.int32, lt.shape, 0).astype(F32)
    lg = jnp.where(row < N_GROUPS, lt, NEG_INF)
    mg = jnp.max(lg, axis=0, keepdims=True)
    g_sel = jnp.min(jnp.where(lg == mg, row, float(LANES)), axis=0, keepdims=True)
    p_group = 1.0 / jnp.sum(jnp.exp(lg - mg), axis=0, keepdims=True)
    lo = N_GROUPS + g_sel * EXPERTS_PER_GROUP
    le = jnp.where((row >= lo) & (row < lo + EXPERTS_PER_GROUP), lt, NEG_INF)
    m0 = jnp.max(le, axis=0, keepdims=True)
    i0 = jnp.min(jnp.where(le == m0, row, float(LANES)), axis=0, keepdims=True)
    le1 = jnp.where(row == i0, NEG_INF, le)
    m1 = jnp.max(le1, axis=0, keepdims=True)
    i1 = jnp.min(jnp.where(le1 == m1, row, float(LANES)), axis=0, keepdims=True)
    z = jnp.exp(m1 - m0)
    w0 = p_group / (1.0 + z)
    w1 = p_group * z / (1.0 + z)
    pad = jnp.zeros((8 - 2, lt.shape[1]), F32)
    gate_ref[...] = jnp.concatenate([w0, w1, pad], axis=0)

    @pl.when(pl.program_id(0) == 0)
    def _():
        carry_ref[...] = jnp.zeros_like(carry_ref)

    oh0 = (row == i0).astype(F32)
    oh1 = (row == i1).astype(F32)
    both = oh0 + oh1
    before = _dot(both.astype(BF16), tri_ref[...]) + carry_ref[:, 0:1]
    r0 = jnp.sum(oh0 * before, axis=0, keepdims=True)
    r1 = jnp.sum(oh1 * before, axis=0, keepdims=True)
    carry_ref[...] = carry_ref[...] + jnp.sum(both, axis=1, keepdims=True)
    count_ref[...] = carry_ref[...].astype(jnp.int32)
    e_ref[...] = jnp.concatenate([i0 - N_GROUPS, i1 - N_GROUPS, r0, r1, pad[:4]], axis=0).astype(jnp.int32)


def _out_router(x2, og, om, oc, w_bf, g, wr_bf, br):
    t = x2.shape[0]
    row = lambda w: pl.BlockSpec((OUT_TM, w), lambda i: (i, 0))
    const = lambda a, b: pl.BlockSpec((a, b), lambda i: (0, 0))
    idx = np.arange(OUT_TM)
    tri = jnp.asarray(idx[:, None] < idx[None, :], BF16)
    col = lambda r: pl.BlockSpec((r, OUT_TM), lambda i: (0, i))
    return pl.pallas_call(
        _out_router_kernel,
        grid=(t // OUT_TM,),
        in_specs=[row(D_MODEL), row(GLA_V_W), row(QK_W), row(QK_W), const(D_MODEL, D_MODEL),
                  const(1, D_MODEL), const(D_MODEL, LANES), const(1, LANES), const(OUT_TM, OUT_TM)],
        out_specs=[row(D_MODEL), pl.BlockSpec((OUT_TM * X_TILE, LANES), lambda i: (i, 0)),
                   col(8), col(8), const(ROUTE_ROWS, LANES)],
        out_shape=[
            jax.ShapeDtypeStruct((t, D_MODEL), F32),
            jax.ShapeDtypeStruct((t * X_TILE, LANES), jnp.uint32),
            jax.ShapeDtypeStruct((8, t), jnp.int32),
            jax.ShapeDtypeStruct((8, t), F32),
            jax.ShapeDtypeStruct((ROUTE_ROWS, LANES), jnp.int32),
        ],
        scratch_shapes=[pltpu.VMEM((ROUTE_ROWS, LANES), F32)],
        compiler_params=_params(("arbitrary",)),
        name="out_router",
    )(x2, og, om, oc, w_bf, g, wr_bf, br, tri)


DMA_UNROLL = 8


def _invert_kernel(dest_ref, pad_lo_ref, pad_hi_ref, tok_ref):
    def pad_segment(e, c):
        hi = pad_hi_ref[e]

        def pad(g, c2):
            for r in range(8):
                tok_ref[hi - 8 * (g + 1) + r] = 0
            return c2
        lax.fori_loop(0, lax.shift_right_logical(hi - pad_lo_ref[e] + 7, 3), pad, 0)
        return c

    lax.fori_loop(0, pad_lo_ref.shape[0], pad_segment, 0)

    n_tok = dest_ref.shape[0] // 2

    def place(t, c):
        tok_ref[dest_ref[t]] = t
        tok_ref[dest_ref[n_tok + t]] = t
        return c

    lax.fori_loop(0, n_tok, place, 0, unroll=DMA_UNROLL)


def _invert(dest, pad_lo, pad_hi, n_rows):
    smem = pl.BlockSpec(memory_space=pltpu.SMEM)
    return pl.pallas_call(
        _invert_kernel,
        in_specs=[smem, smem, smem],
        out_specs=smem,
        out_shape=jax.ShapeDtypeStruct((n_rows,), jnp.int32),
        name="invert",
    )(dest, pad_lo, pad_hi)


def _experts_kernel(be_ref, run_ref, next_ref, nused_ref, tok_ref,
                    hn_hbm, wg_hbm, wu_hbm, wd_hbm, ys_hbm,
                    hn_vmem, xg_even, xg_odd, y_even, y_odd, wg_f, wu_f, wd_f, wg_bf, wu_bf, wd_bf,
                    hn_sem, w_sem, y_sem):
    n_used = nused_ref[0]
    n_blocks = be_ref.shape[0]
    block_rows = ROW_BLOCK * ROW_TILE

    def weight_copies(e, slot):
        return [pltpu.make_async_copy(src.at[e], dst.at[slot], w_sem.at[slot])
                for src, dst in ((wg_hbm, wg_f), (wu_hbm, wu_f), (wd_hbm, wd_f))]

    def y_copy(buf, parity, i):
        dst = ys_hbm.at[pl.ds(pl.multiple_of(i * block_rows, block_rows), block_rows), :]
        return pltpu.make_async_copy(buf, dst, y_sem.at[parity])

    def gather(i, dst):
        base = jnp.minimum(i, n_blocks - 1) * ROW_BLOCK
        for r in range(ROW_BLOCK):
            src = pl.ds(pl.multiple_of(tok_ref[base + r] * X_TILE, X_TILE), X_TILE)
            dst[r * X_TILE:(r + 1) * X_TILE, :] = hn_vmem[src, :]

    def block(i, parity, cur, nxt, ybuf):
        @pl.when(jnp.logical_or(i == 0, be_ref[i] != be_ref[jnp.maximum(i - 1, 0)]))
        def _():
            slot = jnp.bitwise_and(run_ref[i], 1)
            for c in weight_copies(be_ref[i], slot):
                c.wait()
            wg_bf[...] = wg_f[slot].astype(BF16)
            wu_bf[...] = wu_f[slot].astype(BF16)
            wd_bf[...] = wd_f[slot].astype(BF16)

            @pl.when(next_ref[i] < n_used)
            def _():
                for c in weight_copies(be_ref[jnp.minimum(next_ref[i], n_blocks - 1)], 1 - slot):
                    c.start()

        @pl.when(i >= 2)
        def _():
            y_copy(ybuf, parity, i).wait()

        kh = D_MODEL // 2
        gather(i + 1, nxt)
        x_lo, x_hi = _unpack_bf16_pairs(_load_row_tiles(cur, ROW_BLOCK))
        gate = _dot(x_lo, wg_bf[:kh, :]) + _dot(x_hi, wg_bf[kh:, :])
        up = _dot(x_lo, wu_bf[:kh, :]) + _dot(x_hi, wu_bf[kh:, :])
        hid = (_silu(gate) * up).astype(BF16)
        _store_row_tiles(ybuf, _dot(hid, wd_bf[...]))
        y_copy(ybuf, parity, i).start()

    hn_load = pltpu.make_async_copy(hn_hbm, hn_vmem, hn_sem)
    hn_load.start()
    for c in weight_copies(be_ref[0], 0):
        c.start()
    hn_load.wait()
    gather(0, xg_even)

    def pair(p, carry):
        block(2 * p, 0, xg_even, xg_odd, y_even)

        @pl.when(2 * p + 1 < n_used)
        def _():
            block(2 * p + 1, 1, xg_odd, xg_even, y_odd)
        return carry

    lax.fori_loop(0, lax.shift_right_logical(n_used + 1, 1), pair, 0)

    y_copy(y_even, 0, 0).wait()

    @pl.when(n_used >= 2)
    def _():
        y_copy(y_odd, 1, 0).wait()

    y_even[...] = jnp.zeros_like(y_even)

    def zero_start(i, carry):
        y_copy(y_even, 0, i).start()
        return carry

    def zero_wait(i, carry):
        y_copy(y_even, 0, i).wait()
        return carry

    lax.fori_loop(n_used, n_blocks, zero_start, 0)
    lax.fori_loop(n_used, n_blocks, zero_wait, 0)


def _experts(hn, tok_of_row, block_expert, block_run, block_next, n_used, w_gate, w_up, w_down):
    n_rows = tok_of_row.shape[0]
    hbm = pl.BlockSpec(memory_space=pl.ANY)
    grid_spec = pltpu.PrefetchScalarGridSpec(
        num_scalar_prefetch=5,
        grid=(1,),
        in_specs=[hbm, hbm, hbm, hbm],
        out_specs=hbm,
        scratch_shapes=[pltpu.VMEM(hn.shape, jnp.uint32),
                        pltpu.VMEM((ROW_BLOCK * X_TILE, LANES), jnp.uint32),
                        pltpu.VMEM((ROW_BLOCK * X_TILE, LANES), jnp.uint32),
                        pltpu.VMEM((ROW_BLOCK * ROW_TILE, LANES), F32),
                        pltpu.VMEM((ROW_BLOCK * ROW_TILE, LANES), F32),
                        pltpu.VMEM((2, D_MODEL, MOE_FF), F32), pltpu.VMEM((2, D_MODEL, MOE_FF), F32),
                        pltpu.VMEM((2, MOE_FF, D_MODEL), F32),
                        pltpu.VMEM((D_MODEL, MOE_FF), BF16), pltpu.VMEM((D_MODEL, MOE_FF), BF16),
                        pltpu.VMEM((MOE_FF, D_MODEL), BF16),
                        pltpu.SemaphoreType.DMA, pltpu.SemaphoreType.DMA((2,)), pltpu.SemaphoreType.DMA((2,))],
    )
    return pl.pallas_call(
        _experts_kernel,
        grid_spec=grid_spec,
        out_shape=jax.ShapeDtypeStruct((n_rows * ROW_TILE, LANES), F32),
        compiler_params=_params(("arbitrary",)),
        name="experts",
    )(block_expert, block_run, block_next, n_used, tok_of_row, hn, w_gate, w_up, w_down)


COMB_TM = 512


def _combine_kernel(dest_ref, dest_next_ref, x1_ref, gate_ref, ys_ref, out_ref, ybuf, sems):
    g = pl.program_id(0)

    def row_copy(d, slot, k, r):
        src = ys_ref.at[pl.ds(pl.multiple_of(d * ROW_TILE, ROW_TILE), ROW_TILE), :]
        dst = ybuf.at[slot, k, pl.ds(pl.multiple_of(r * ROW_TILE, ROW_TILE), ROW_TILE), :]
        return pltpu.make_async_copy(src, dst, sems.at[slot])

    def gather(dref, base, slot):
        def issue(r, c):
            row_copy(dref[0, 0, 0, base + r], slot, 0, r).start(priority=0)
            row_copy(dref[1, 0, 0, base + r], slot, 1, r).start(priority=1)
            return c
        lax.fori_loop(0, COMB_TM, issue, 0, unroll=DMA_UNROLL)

    def finish(slot):
        def drain(r, c):
            row_copy(0, slot, 0, 0).wait()
            return c
        lax.fori_loop(0, 2 * COMB_TM, drain, 0, unroll=DMA_UNROLL)
        rows = slice(slot * COMB_TM, (slot + 1) * COMB_TM)
        gate = gate_ref[:, rows].T
        w0 = gate[:, 0:1]
        w1 = gate[:, 1:2]
        y0 = _load_row_tiles(ybuf.at[slot, 0], COMB_TM)
        y1 = _load_row_tiles(ybuf.at[slot, 1], COMB_TM)
        out_ref[rows, :] = x1_ref[rows, :] + (y0 * w0 + y1 * w1)

    @pl.when(g == 0)
    def _():
        gather(dest_ref, 0, 0)

    gather(dest_ref, COMB_TM, 1)
    finish(0)

    @pl.when(g + 1 < pl.num_programs(0))
    def _():
        gather(dest_next_ref, 0, 0)

    finish(1)


def _combine(x1, gates, ys, dest):
    t = x1.shape[0]
    ng = t // (2 * COMB_TM)
    dest3 = dest.reshape(2, ng, 1, 2 * COMB_TM)
    smem = lambda f: pl.BlockSpec((2, 1, 1, 2 * COMB_TM), f, memory_space=pltpu.SMEM)
    return pl.pallas_call(
        _combine_kernel,
        grid=(ng,),
        in_specs=[smem(lambda i: (0, i, 0, 0)),
                  smem(lambda i: (0, jnp.minimum(i + 1, ng - 1), 0, 0)),
                  pl.BlockSpec((2 * COMB_TM, D_MODEL), lambda i: (i, 0)),
                  pl.BlockSpec((8, 2 * COMB_TM), lambda i: (0, i)),
                  pl.BlockSpec(memory_space=pl.ANY)],
        out_specs=pl.BlockSpec((2 * COMB_TM, D_MODEL), lambda i: (i, 0)),
        out_shape=jax.ShapeDtypeStruct((t, D_MODEL), F32),
        scratch_shapes=[pltpu.VMEM((2, 2, COMB_TM * ROW_TILE, LANES), F32),
                        pltpu.SemaphoreType.DMA((2,))],
        compiler_params=_params(("arbitrary",)),
        name="combine",
    )(dest3, dest3, x1, gates, ys)


def _layer(x, mem, attn_norm_g, mem_norm_g, w_in, w_gla_gk, b_gla_gk, gla_out_norm_g,
           moba_q_norm_g, moba_k_norm_g, w_mem_kv, mem_q_norm_g, mem_k_norm_g, w_out,
           ffn_norm_g, w_router_group, b_router_group, w_router_expert, b_router_expert,
           w_gate, w_up, w_down):
    b, s, d = x.shape
    t = b * s
    x2 = x.reshape(t, d)
    row = lambda v: v.reshape(1, -1).astype(F32)
    tile_heads = lambda v: jnp.tile(v.astype(F32), HEADS).reshape(1, QK_W)
    hid = np.arange(QK_W) // DH
    seg = jnp.asarray((hid[:, None] == hid[None, :]) / DH, BF16)

    w_in_p = jnp.concatenate([w_in[:, :1536], w_in[:, 1552:], w_in[:, 1536:1552]], axis=1).astype(BF16)
    wr = jnp.concatenate([w_router_group,
                          jnp.transpose(w_router_expert, (1, 0, 2)).reshape(d, N_EXPERTS),
                          jnp.zeros((d, LANES - N_GROUPS - N_EXPERTS), F32)], axis=1).astype(BF16)
    br = jnp.concatenate([b_router_group, b_router_expert.reshape(N_EXPERTS),
                          jnp.zeros((LANES - N_GROUPS - N_EXPERTS,), F32)]).reshape(1, LANES)

    memk, memvt = _mem_kv(mem, row(mem_norm_g), w_mem_kv.astype(BF16), tile_heads(mem_k_norm_g), seg)
    qk, gv, gr, glr, mq, mk, mvt, kmean, cq = _in_proj(
        x2, row(attn_norm_g), w_in_p, seg, tile_heads(moba_q_norm_g), tile_heads(moba_k_norm_g),
        tile_heads(mem_q_norm_g))
    o_gla = _gla(qk, gv, gr, glr, w_gla_gk.astype(BF16), row(b_gla_gk), row(gla_out_norm_g), b, s)
    o_moba, o_mem = _moba_mem(mq, mk, mvt, kmean, cq, memk, memvt, b, s)
    x1, hn, e_ids, gates, counts = _out_router(
        x2, o_gla.reshape(t, GLA_V_W), o_moba.reshape(t, QK_W), o_mem.reshape(t, QK_W),
        w_out.astype(BF16), row(ffn_norm_g), wr, br)

    counts = counts[N_GROUPS:N_GROUPS + N_EXPERTS, 0]
    padded = (counts + ROW_BLOCK - 1) // ROW_BLOCK * ROW_BLOCK
    pends = jnp.cumsum(padded)
    pstarts = pends - padded
    onehot = e_ids[:2, :, None] == jnp.arange(N_EXPERTS, dtype=jnp.int32)
    dest = (jnp.sum(jnp.where(onehot, pstarts, 0), axis=-1) + e_ids[2:4]).astype(jnp.int32)
    n_rows = (t * 2 + N_EXPERTS * (ROW_BLOCK - 1) + ROW_BLOCK - 1) // ROW_BLOCK * ROW_BLOCK
    nb = n_rows // ROW_BLOCK
    block_start = jnp.arange(nb, dtype=jnp.int32) * ROW_BLOCK
    block_expert = jnp.minimum(jnp.sum(block_start[:, None] >= pends[None, :], axis=1),
                               N_EXPERTS - 1).astype(jnp.int32)
    n_used = (pends[-1] // ROW_BLOCK).astype(jnp.int32).reshape(1)
    ended = block_start[:, None] >= pends[None, :]
    block_run = jnp.sum(ended & (counts > 0)[None, :], axis=1).astype(jnp.int32)
    block_next = (jnp.min(jnp.where(ended, n_rows, pends[None, :]), axis=1) // ROW_BLOCK).astype(jnp.int32)

    pad_lo = jnp.concatenate([pstarts + counts, pends[-1:]]).astype(jnp.int32)
    pad_hi = jnp.concatenate([pends, jnp.full((1,), n_rows)]).astype(jnp.int32)
    tok_of_row = _invert(dest.reshape(-1), pad_lo, pad_hi, n_rows)
    ys = _experts(hn, tok_of_row, block_expert, block_run, block_next, n_used, w_gate, w_up, w_down)
    out = _combine(x1, gates, ys, dest)
    return out.reshape(b, s, d)


def kernel(x, mem, attn_norm_g, mem_norm_g, w_in, w_gla_gk, b_gla_gk, gla_out_norm_g, moba_q_norm_g, moba_k_norm_g, w_mem_kv, mem_q_norm_g, mem_k_norm_g, w_out, ffn_norm_g, w_router_group, b_router_group, w_router_expert, b_router_expert, w_gate, w_up, w_down):
    depth = w_in.shape[0]
    for l in range(depth):
        x = _layer(x, mem, attn_norm_g[l], mem_norm_g[l], w_in[l], w_gla_gk[l], b_gla_gk[l],
                   gla_out_norm_g[l], moba_q_norm_g[l], moba_k_norm_g[l], w_mem_kv[l],
                   mem_q_norm_g[l], mem_k_norm_g[l], w_out[l], ffn_norm_g[l], w_router_group[l],
                   b_router_group[l], w_router_expert[l], b_router_expert[l],
                   w_gate[l], w_up[l], w_down[l])
    return x
```

```python
import functools

import jax
import jax.numpy as jnp
import numpy as np
from jax import lax
from jax.experimental import pallas as pl
from jax.experimental.pallas import tpu as pltpu

F32 = jnp.float32
BF16 = jnp.bfloat16
EPS = 1e-6
NEG_INF = float("-inf")

D_MODEL = 1024
N_MEM = 256
HEADS = 4
DH = 64
GLA_DV = 128
GLA_RANK = 16
GLA_GATE_NORMALIZER = 16.0
GLA_CHUNK = 64
MOBA_BLOCK = 256
MOBA_TOPK = 3
QK_W = HEADS * DH
GLA_V_W = HEADS * GLA_DV
N_GROUPS = 4
EXPERTS_PER_GROUP = 8
N_EXPERTS = N_GROUPS * EXPERTS_PER_GROUP
MOE_FF = 512
LANES = 128
ROW_BLOCK = 256

VMEM_LIMIT = 56 * 1024 * 1024


def _params(sem):
    return pltpu.CompilerParams(dimension_semantics=sem, vmem_limit_bytes=VMEM_LIMIT)


def _nt(a, b):
    return lax.dot_general(a, b, (((1,), (1,)), ((), ())), preferred_element_type=F32)


def _tn(a, b):
    return lax.dot_general(a, b, (((0,), (0,)), ((), ())), preferred_element_type=F32)


def _dot(a, b):
    return jnp.dot(a, b, preferred_element_type=F32)


def _rms_rows(x, g):
    ms = jnp.mean(x * x, axis=-1, keepdims=True)
    return x * lax.rsqrt(ms + EPS) * g


def _split3(x):
    h1 = x.astype(BF16)
    r1 = x - h1.astype(F32)
    h2 = r1.astype(BF16)
    h3 = (r1 - h2.astype(F32)).astype(BF16)
    return h1, h2, h3


def _head_mean_sq(x, seg):
    sq = x * x
    hi = sq.astype(BF16)
    lo = (sq - hi.astype(F32)).astype(BF16)
    return _dot(hi, seg) + _dot(lo, seg)


def _silu(x):
    return x * (1.0 / (1.0 + jnp.exp(-x)))


VT_ROWS = DH + 16
VT_ALL = HEADS * VT_ROWS


ALIBI_SLOPES = tuple(2.0 ** (-8.0 * (h + 1) / HEADS) for h in range(HEADS))


def _vt_with_ones(v, key_slopes=None):
    n = v.shape[0]
    vt = v.T
    ones = jnp.ones((VT_ROWS - DH, n), F32)
    pos = lax.broadcasted_iota(jnp.int32, (1, n), 1).astype(F32)
    parts = []
    for h in range(HEADS):
        scale = 1.0 if key_slopes is None else jnp.exp(key_slopes[h] * pos)
        parts += [vt[h * DH:(h + 1) * DH] * scale, ones * scale]
    return jnp.concatenate(parts, axis=0).astype(BF16)


ROW_TILE = D_MODEL // LANES
X_TILE = ROW_TILE // 2


def _store_row_tiles(ref, x):
    n, tiles = x.shape[0], x.shape[1] // LANES
    for c in range(tiles):
        ref[pl.ds(c, n, stride=tiles), :] = x[:, c * LANES:(c + 1) * LANES]


def _load_row_tiles(ref, n):
    tiles = ref.shape[0] // n
    return jnp.concatenate([ref[pl.ds(c, n, stride=tiles), :] for c in range(tiles)], axis=1)


def _pack_bf16_pairs(x):
    w = x.shape[1] // 2
    bits = lambda t: lax.bitcast_convert_type(t.astype(BF16).astype(F32), jnp.uint32)
    return (bits(x[:, w:]) & jnp.uint32(0xFFFF0000)) | (bits(x[:, :w]) >> 16)


def _unpack_bf16_pairs(words):
    lo = lax.bitcast_convert_type(words << 16, F32).astype(BF16)
    hi = lax.bitcast_convert_type(words & jnp.uint32(0xFFFF0000), F32).astype(BF16)
    return lo, hi


def _mem_kv_kernel(mem_ref, g_ref, w_ref, gk_ref, seg_ref, k_ref, vt_ref):
    h = _rms_rows(mem_ref[0], g_ref[...]).astype(BF16)
    kv = _dot(h, w_ref[...])
    k = kv[:, :QK_W]
    kn = k * lax.rsqrt(_head_mean_sq(k, seg_ref[...]) + EPS) * gk_ref[...]
    k_ref[0] = kn.astype(BF16)
    vt_ref[0] = _vt_with_ones(kv[:, QK_W:])


def _mem_kv(mem, g, w_bf, gk_t, seg):
    b = mem.shape[0]
    return pl.pallas_call(
        _mem_kv_kernel,
        grid=(b,),
        in_specs=[
            pl.BlockSpec((1, N_MEM, D_MODEL), lambda i: (i, 0, 0)),
            pl.BlockSpec((1, D_MODEL), lambda i: (0, 0)),
            pl.BlockSpec((D_MODEL, 2 * QK_W), lambda i: (0, 0)),
            pl.BlockSpec((1, QK_W), lambda i: (0, 0)),
            pl.BlockSpec((QK_W, QK_W), lambda i: (0, 0)),
        ],
        out_specs=[
            pl.BlockSpec((1, N_MEM, QK_W), lambda i: (i, 0, 0)),
            pl.BlockSpec((1, VT_ALL, N_MEM), lambda i: (i, 0, 0)),
        ],
        out_shape=[
            jax.ShapeDtypeStruct((b, N_MEM, QK_W), BF16),
            jax.ShapeDtypeStruct((b, VT_ALL, N_MEM), BF16),
        ],
        compiler_params=_params(("arbitrary",)),
        name="mem_kv",
    )(mem, g, w_bf, gk_t, seg)


IN_TM = 1024
_C_QK, _C_V, _C_R, _C_MQ, _C_MK, _C_MV, _C_CQ, _C_LR = 0, 512, 1024, 1536, 1792, 2048, 2304, 2560
D_IN = 2576


def _in_proj_kernel(x_ref, g_ref, w_ref, seg_ref, gq_ref, gk_ref, gc_ref,
                    qk_ref, v_ref, r_ref, lr_ref, mq_ref, mk_ref, mvt_ref, kmean_ref, cq_ref):
    h = _rms_rows(x_ref[...], g_ref[...]).astype(BF16)
    p = _dot(h, w_ref[...])
    seg = seg_ref[...]
    qk_ref[...] = p[:, _C_QK:_C_V]
    v_ref[...] = p[:, _C_V:_C_R].astype(BF16)
    r_ref[...] = p[:, _C_R:_C_MQ]
    lr_ref[...] = p[:, _C_LR:D_IN]

    def head_norm(t, gain):
        return t * lax.rsqrt(_head_mean_sq(t, seg) + EPS) * gain

    scale = DH ** -0.5
    mq_ref[...] = (head_norm(p[:, _C_MQ:_C_MK], gq_ref[...]) * scale).astype(BF16)
    cq_ref[...] = (head_norm(p[:, _C_CQ:_C_LR], gc_ref[...]) * scale).astype(BF16)
    kn = head_norm(p[:, _C_MK:_C_MV], gk_ref[...])
    mk_ref[...] = kn.astype(BF16)
    mv = p[:, _C_MV:_C_CQ]
    for j in range(IN_TM // MOBA_BLOCK):
        rows = slice(j * MOBA_BLOCK, (j + 1) * MOBA_BLOCK)
        kmean_ref[0, j:j + 1, :] = jnp.mean(kn[rows], axis=0, keepdims=True)
        mvt_ref[j] = _vt_with_ones(mv[rows], ALIBI_SLOPES)


def _in_proj(x2, g, w_bf, seg, gq_t, gk_t, gc_t):
    t = x2.shape[0]
    nt = t // IN_TM
    nb = IN_TM // MOBA_BLOCK
    row = lambda w: pl.BlockSpec((IN_TM, w), lambda i: (i, 0))
    const = lambda a, b: pl.BlockSpec((a, b), lambda i: (0, 0))
    return pl.pallas_call(
        _in_proj_kernel,
        grid=(nt,),
        in_specs=[row(D_MODEL), const(1, D_MODEL), const(D_MODEL, D_IN), const(QK_W, QK_W),
                  const(1, QK_W), const(1, QK_W), const(1, QK_W)],
        out_specs=[row(2 * QK_W), row(GLA_V_W), row(GLA_V_W), row(GLA_RANK), row(QK_W), row(QK_W),
                   pl.BlockSpec((nb, VT_ALL, MOBA_BLOCK), lambda i: (i, 0, 0)),
                   pl.BlockSpec((1, nb, QK_W), lambda i: (i, 0, 0)),
                   row(QK_W)],
        out_shape=[
            jax.ShapeDtypeStruct((t, 2 * QK_W), F32),
            jax.ShapeDtypeStruct((t, GLA_V_W), BF16),
            jax.ShapeDtypeStruct((t, GLA_V_W), F32),
            jax.ShapeDtypeStruct((t, GLA_RANK), F32),
            jax.ShapeDtypeStruct((t, QK_W), BF16),
            jax.ShapeDtypeStruct((t, QK_W), BF16),
            jax.ShapeDtypeStruct((t // MOBA_BLOCK, VT_ALL, MOBA_BLOCK), BF16),
            jax.ShapeDtypeStruct((nt, nb, QK_W), F32),
            jax.ShapeDtypeStruct((t, QK_W), BF16),
        ],
        compiler_params=_params(("arbitrary",)),
        name="in_proj",
    )(x2, g, w_bf, seg, gq_t, gk_t, gc_t)


GLA_TC = 256


def _gla_kernel(qk_ref, v_ref, r_ref, lr_ref, wgk_ref, bgk_ref, gn_ref, tri_ref, ones_ref,
                o_ref, st_ref):
    @pl.when(pl.program_id(1) == 0)
    def _():
        st_ref[...] = jnp.zeros_like(st_ref)

    qk = qk_ref[0]
    q = qk[:, :QK_W]
    k = qk[:, QK_W:]
    gk = _dot(lr_ref[0].astype(BF16), wgk_ref[...]) + bgk_ref[...]
    g = -(jnp.maximum(-gk, 0.0) + jnp.log1p(jnp.exp(-jnp.abs(gk)))) / GLA_GATE_NORMALIZER
    g1, g2, g3 = _split3(g)
    tri = tri_ref[...]
    ones = ones_ref[...]
    cum = _dot(tri, g1) + _dot(tri, g2) + _dot(tri, g3)
    tot = _dot(ones, g1) + _dot(ones, g2) + _dot(ones, g3)
    q_dec = (q * (DH ** -0.5) * jnp.exp(cum)).astype(BF16)
    k_inv = (k * jnp.exp(-cum)).astype(BF16)
    k_end = (k * jnp.exp(tot - cum)).astype(BF16)
    decay = jnp.exp(tot)

    lane_head = lax.broadcasted_iota(jnp.int32, (GLA_TC, QK_W), 1) // DH
    row_t = lax.broadcasted_iota(jnp.int32, (GLA_TC, GLA_TC), 0)
    col_t = lax.broadcasted_iota(jnp.int32, (GLA_TC, GLA_TC), 1)
    causal = (row_t >= col_t) & (row_t // GLA_CHUNK == col_t // GLA_CHUNK)
    same_head = (lax.broadcasted_iota(jnp.int32, (GLA_V_W, QK_W), 0) // GLA_DV
                 == lax.broadcasted_iota(jnp.int32, (GLA_V_W, QK_W), 1) // DH)
    gain = gn_ref[...]

    v = v_ref[0]
    chunks = [slice(c * GLA_CHUNK, (c + 1) * GLA_CHUNK) for c in range(GLA_TC // GLA_CHUNK)]
    d_st = [jnp.where(same_head, _tn(v[rows], k_end[rows]), 0.0) for rows in chunks]
    states = [st_ref[...]]
    for c, rows in enumerate(chunks):
        states.append(states[c] * decay[rows.start:rows.start + 1, :] + d_st[c])
    st_ref[...] = states[-1]
    o_inter = jnp.concatenate([_nt(q_dec[rows], states[c].astype(BF16)) for c, rows in enumerate(chunks)],
                              axis=0)
    q_stack = jnp.concatenate([jnp.where(lane_head == h, q_dec, jnp.zeros_like(q_dec)) for h in range(HEADS)],
                              axis=0)
    a_all = _nt(q_stack, k_inv)
    outs = []
    for h in range(HEADS):
        a = jnp.where(causal, a_all[h * GLA_TC:(h + 1) * GLA_TC], 0.0).astype(BF16)
        vs = slice(h * GLA_DV, (h + 1) * GLA_DV)
        oh = _dot(a, v[:, vs]) + o_inter[:, vs]
        outs.append(_rms_rows(oh, gain) * _silu(r_ref[0, :, vs]))
    o_ref[0] = jnp.concatenate(outs, axis=-1).astype(BF16)


def _gla(qk, v, r, lr, wgk_bf, bgk, gn, b, s):
    ns = s // GLA_TC
    idx = np.arange(GLA_TC)
    same_chunk = (idx[:, None] // GLA_CHUNK) == (idx[None, :] // GLA_CHUNK)
    tri = jnp.asarray(same_chunk & (idx[:, None] >= idx[None, :]), BF16)
    ones = jnp.asarray(same_chunk, BF16)
    seq = lambda w: pl.BlockSpec((1, GLA_TC, w), lambda i, j: (i, j, 0))
    const = lambda a, c: pl.BlockSpec((a, c), lambda i, j: (0, 0))
    return pl.pallas_call(
        _gla_kernel,
        grid=(b, ns),
        in_specs=[seq(2 * QK_W), seq(GLA_V_W), seq(GLA_V_W), seq(GLA_RANK),
                  const(GLA_RANK, QK_W), const(1, QK_W), const(1, GLA_DV),
                  const(GLA_TC, GLA_TC), const(GLA_TC, GLA_TC)],
        out_specs=seq(GLA_V_W),
        out_shape=jax.ShapeDtypeStruct((b, s, GLA_V_W), BF16),
        scratch_shapes=[pltpu.VMEM((GLA_V_W, QK_W), F32)],
        compiler_params=_params(("arbitrary", "arbitrary")),
        name="gla",
    )(qk.reshape(b, s, 2 * QK_W), v.reshape(b, s, GLA_V_W), r.reshape(b, s, GLA_V_W),
      lr.reshape(b, s, GLA_RANK), wgk_bf, bgk, gn, tri, ones)


TQ = MOBA_BLOCK
KV_UNROLL = 8


def _moba_mem_kernel(q_ref, k_ref, vt_ref, kmean_ref, cq_ref, mk_ref, mvt_ref,
                     o_moba_ref, o_mem_ref,
                     qcat_ref, selb_ref, s_ref, *, n_blocks):
    i = pl.program_id(1)
    lane_head = lax.broadcasted_iota(jnp.int32, (TQ, QK_W), 1) // DH
    dist0 = (lax.broadcasted_iota(jnp.int32, (MOBA_BLOCK, TQ), 1)
             - lax.broadcasted_iota(jnp.int32, (MOBA_BLOCK, TQ), 0)).astype(F32)
    slopes = ALIBI_SLOPES
    heads = [slice(h * TQ, (h + 1) * TQ) for h in range(HEADS)]
    vrows = [slice(h * VT_ROWS, (h + 1) * VT_ROWS) for h in range(HEADS)]

    def stack_heads(x):
        return jnp.concatenate([jnp.where(lane_head == h, x, jnp.zeros_like(x)) for h in range(HEADS)], axis=0)

    def finish(accl):
        return accl[:DH] * (1.0 / accl[DH:DH + 1])

    s_all = _nt(mk_ref[0], stack_heads(cq_ref[0]))
    outs = []
    for h in range(HEADS):
        s = s_all[:, heads[h]]
        p = jnp.exp(s - jnp.max(s, axis=0, keepdims=True)).astype(BF16)
        outs.append(finish(_dot(mvt_ref[0, vrows[h], :], p)))
    o_mem_ref[0] = jnp.concatenate(outs, axis=0).T.astype(BF16)

    qcat = stack_heads(q_ref[0])
    qcat_ref[...] = qcat
    k_own = k_ref[0, pl.ds(pl.multiple_of(i * MOBA_BLOCK, MOBA_BLOCK), MOBA_BLOCK), :]
    gate_all = _nt(kmean_ref[0].astype(BF16), qcat)
    s_all = _nt(k_own, qcat)
    blk = lax.broadcasted_iota(jnp.int32, (n_blocks, TQ), 0)
    blk_f = blk.astype(F32)

    tq = lax.broadcasted_iota(jnp.int32, (1, TQ), 1).astype(F32)
    init = []
    for h in range(HEADS):
        gate = jnp.where(blk < i, gate_all[:, heads[h]], NEG_INF)
        chosen = jnp.zeros((n_blocks, TQ), jnp.bool_)
        for r in range(MOBA_TOPK):
            mx = jnp.max(gate, axis=0, keepdims=True)
            first = jnp.min(jnp.where(gate == mx, blk_f, float(n_blocks)), axis=0, keepdims=True)
            hit = blk_f == first
            chosen = chosen | (hit & (mx > NEG_INF))
            gate = jnp.where(hit, NEG_INF, gate)
        selb_ref[h] = jnp.where(chosen, 0.0, NEG_INF)
        aq = -slopes[h] * tq
        s = jnp.where(dist0 >= 0.0, s_all[:, heads[h]], NEG_INF)
        m0 = jnp.max(s, axis=0, keepdims=True) + aq
        p = jnp.exp(s - (m0 - aq)).astype(BF16)
        init += [m0, _dot(vt_ref[i, vrows[h], :], p)]

    def stage_scores(j, slot):
        kj = k_ref[0, pl.ds(pl.multiple_of(j * MOBA_BLOCK, MOBA_BLOCK), MOBA_BLOCK), :]
        s_ref[slot] = _nt(kj, qcat_ref[...])

    stage_scores(0, 0)

    def past_blocks(g, carry, unroll, base):
        carry = list(carry)
        for u in range(unroll):
            j = base + g * unroll + u
            stage_scores(jnp.minimum(j + 1, n_blocks - 1), (u + 1) % 2)
            off = jnp.full((1, TQ), (i - j) * MOBA_BLOCK, jnp.int32).astype(F32) + tq
            for h in range(HEADS):
                m, accl = carry[2 * h:2 * h + 2]
                s1 = s_ref[u % 2, :, heads[h]]
                rb = selb_ref[h, pl.ds(j, 1), :] - slopes[h] * off
                m_new = jnp.maximum(m, jnp.max(s1, axis=0, keepdims=True) + rb)
                p = jnp.exp(s1 - (m_new - rb)).astype(BF16)
                carry[2 * h:2 * h + 2] = [m_new, jnp.exp(m - m_new) * accl + _dot(vt_ref[j, vrows[h], :], p)]
        return tuple(carry)

    n_main = i // KV_UNROLL
    rest = i - n_main * KV_UNROLL
    carry = lax.fori_loop(0, n_main, functools.partial(past_blocks, unroll=KV_UNROLL, base=0), tuple(init))
    final = lax.fori_loop(0, (rest + 1) // 2,
                          functools.partial(past_blocks, unroll=2, base=n_main * KV_UNROLL), carry)
    o_t = jnp.concatenate([finish(final[2 * h + 1]) for h in range(HEADS)], axis=0)
    o_moba_ref[0] = o_t.T.astype(BF16)


def _moba_mem(mq, mk, mvt, kmean, cq, memk, memvt, b, s):
    nq = s // TQ
    n_blocks = s // MOBA_BLOCK
    qspec = pl.BlockSpec((1, TQ, QK_W), lambda i, j: (i, j, 0))
    return pl.pallas_call(
        functools.partial(_moba_mem_kernel, n_blocks=n_blocks),
        grid=(b, nq),
        in_specs=[
            qspec,
            pl.BlockSpec((1, s, QK_W), lambda i, j: (i, 0, 0)),
            pl.BlockSpec((n_blocks, VT_ALL, MOBA_BLOCK), lambda i, j: (i, 0, 0)),
            pl.BlockSpec((1, n_blocks, QK_W), lambda i, j: (i, 0, 0)),
            qspec,
            pl.BlockSpec((1, N_MEM, QK_W), lambda i, j: (i, 0, 0)),
            pl.BlockSpec((1, VT_ALL, N_MEM), lambda i, j: (i, 0, 0)),
        ],
        out_specs=[qspec, qspec],
        out_shape=[jax.ShapeDtypeStruct((b, s, QK_W), BF16), jax.ShapeDtypeStruct((b, s, QK_W), BF16)],
        scratch_shapes=[
            pltpu.VMEM((HEADS * TQ, QK_W), BF16),
            pltpu.VMEM((HEADS, n_blocks, TQ), F32),
            pltpu.VMEM((2, MOBA_BLOCK, HEADS * TQ), F32),
        ],
        compiler_params=_params(("arbitrary", "arbitrary")),
        name="moba_mem",
    )(mq.reshape(b, s, QK_W), mk.reshape(b, s, QK_W), mvt, kmean.reshape(b, n_blocks, QK_W),
      cq.reshape(b, s, QK_W), memk, memvt)


OUT_TM = 1024
ROUTE_ROWS = 40


def _out_router_kernel(x_ref, og_ref, om_ref, oc_ref, w_ref, g_ref, wr_ref, br_ref, tri_ref,
                       x1_ref, h_ref, e_ref, gate_ref, count_ref, carry_ref):
    w = w_ref[...]
    x1 = (x_ref[...] + _dot(og_ref[...], w[:GLA_V_W]) + _dot(om_ref[...], w[GLA_V_W:GLA_V_W + QK_W])
          + _dot(oc_ref[...], w[GLA_V_W + QK_W:]))
    x1_ref[...] = x1
    hn = _rms_rows(x1, g_ref[...])
    _store_row_tiles(h_ref, _pack_bf16_pairs(hn))
    logits = _dot(hn.astype(BF16), wr_ref[...]) + br_ref[...]
    lt = logits.T[:ROUTE_ROWS]
    row = lax.broadcasted_iota(jnp.int32, lt.shape, 0).astype(F32)
    lg = jnp.where(row < N_GROUPS, lt, NEG_INF)
    mg = jnp.max(lg, axis=0, keepdims=True)
    g_sel = jnp.min(jnp.where(lg == mg, row, float(LANES)), axis=0, keepdims=True)
    p_group = 1.0 / jnp.sum(jnp.exp(lg - mg), axis=0, keepdims=True)
    lo = N_GROUPS + g_sel * EXPERTS_PER_GROUP
    le = jnp.where((row >= lo) & (row < lo + EXPERTS_PER_GROUP), lt, NEG_INF)
    m0 = jnp.max(le, axis=0, keepdims=True)
    i0 = jnp.min(jnp.where(le == m0, row, float(LANES)), axis=0, keepdims=True)
    le1 = jnp.where(row == i0, NEG_INF, le)
    m1 = jnp.max(le1, axis=0, keepdims=True)
    i1 = jnp.min(jnp.where(le1 == m1, row, float(LANES)), axis=0, keepdims=True)
    z = jnp.exp(m1 - m0)
    w0 = p_group / (1.0 + z)
    w1 = p_group * z / (1.0 + z)
    pad = jnp.zeros((8 - 2, lt.shape[1]), F32)
    gate_ref[...] = jnp.concatenate([w0, w1, pad], axis=0)

    @pl.when(pl.program_id(0) == 0)
    def _():
        carry_ref[...] = jnp.zeros_like(carry_ref)

    oh0 = (row == i0).astype(F32)
    oh1 = (row == i1).astype(F32)
    both = oh0 + oh1
    before = _dot(both.astype(BF16), tri_ref[...]) + carry_ref[:, 0:1]
    r0 = jnp.sum(oh0 * before, axis=0, keepdims=True)
    r1 = jnp.sum(oh1 * before, axis=0, keepdims=True)
    carry_ref[...] = carry_ref[...] + jnp.sum(both, axis=1, keepdims=True)
    count_ref[...] = carry_ref[...].astype(jnp.int32)
    e_ref[...] = jnp.concatenate([i0 - N_GROUPS, i1 - N_GROUPS, r0, r1, pad[:4]], axis=0).astype(jnp.int32)


def _out_router(x2, og, om, oc, w_bf, g, wr_bf, br):
    t = x2.shape[0]
    row = lambda w: pl.BlockSpec((OUT_TM, w), lambda i: (i, 0))
    const = lambda a, b: pl.BlockSpec((a, b), lambda i: (0, 0))
    idx = np.arange(OUT_TM)
    tri = jnp.asarray(idx[:, None] < idx[None, :], BF16)
    col = lambda r: pl.BlockSpec((r, OUT_TM), lambda i: (0, i))
    return pl.pallas_call(
        _out_router_kernel,
        grid=(t // OUT_TM,),
        in_specs=[row(D_MODEL), row(GLA_V_W), row(QK_W), row(QK_W), const(D_MODEL, D_MODEL),
                  const(1, D_MODEL), const(D_MODEL, LANES), const(1, LANES), const(OUT_TM, OUT_TM)],
        out_specs=[row(D_MODEL), pl.BlockSpec((OUT_TM * X_TILE, LANES), lambda i: (i, 0)),
                   col(8), col(8), const(ROUTE_ROWS, LANES)],
        out_shape=[
            jax.ShapeDtypeStruct((t, D_MODEL), F32),
            jax.ShapeDtypeStruct((t * X_TILE, LANES), jnp.uint32),
            jax.ShapeDtypeStruct((8, t), jnp.int32),
            jax.ShapeDtypeStruct((8, t), F32),
            jax.ShapeDtypeStruct((ROUTE_ROWS, LANES), jnp.int32),
        ],
        scratch_shapes=[pltpu.VMEM((ROUTE_ROWS, LANES), F32)],
        compiler_params=_params(("arbitrary",)),
        name="out_router",
    )(x2, og, om, oc, w_bf, g, wr_bf, br, tri)


DMA_UNROLL = 8


def _invert_kernel(dest_ref, pad_lo_ref, pad_hi_ref, tok_ref):
    def pad_segment(e, c):
        hi = pad_hi_ref[e]

        def pad(g, c2):
            for r in range(8):
                tok_ref[hi - 8 * (g + 1) + r] = 0
            return c2
        lax.fori_loop(0, lax.shift_right_logical(hi - pad_lo_ref[e] + 7, 3), pad, 0)
        return c

    lax.fori_loop(0, pad_lo_ref.shape[0], pad_segment, 0)

    n_tok = dest_ref.shape[0] // 2

    def place(t, c):
        tok_ref[dest_ref[t]] = t
        tok_ref[dest_ref[n_tok + t]] = t
        return c

    lax.fori_loop(0, n_tok, place, 0, unroll=DMA_UNROLL)


def _invert(dest, pad_lo, pad_hi, n_rows):
    smem = pl.BlockSpec(memory_space=pltpu.SMEM)
    return pl.pallas_call(
        _invert_kernel,
        in_specs=[smem, smem, smem],
        out_specs=smem,
        out_shape=jax.ShapeDtypeStruct((n_rows,), jnp.int32),
        name="invert",
    )(dest, pad_lo, pad_hi)


def _experts_kernel(be_ref, run_ref, next_ref, nused_ref, tok_ref,
                    hn_hbm, wg_hbm, wu_hbm, wd_hbm, ys_hbm,
                    hn_vmem, xg_even, xg_odd, y_even, y_odd, wg_f, wu_f, wd_f, wg_bf, wu_bf, wd_bf,
                    hn_sem, w_sem, y_sem):
    n_used = nused_ref[0]
    n_blocks = be_ref.shape[0]
    block_rows = ROW_BLOCK * ROW_TILE

    def weight_copies(e, slot):
        return [pltpu.make_async_copy(src.at[e], dst.at[slot], w_sem.at[slot])
                for src, dst in ((wg_hbm, wg_f), (wu_hbm, wu_f), (wd_hbm, wd_f))]

    def y_copy(buf, parity, i):
        dst = ys_hbm.at[pl.ds(pl.multiple_of(i * block_rows, block_rows), block_rows), :]
        return pltpu.make_async_copy(buf, dst, y_sem.at[parity])

    def gather(i, dst):
        base = jnp.minimum(i, n_blocks - 1) * ROW_BLOCK
        for r in range(ROW_BLOCK):
            src = pl.ds(pl.multiple_of(tok_ref[base + r] * X_TILE, X_TILE), X_TILE)
            dst[r * X_TILE:(r + 1) * X_TILE, :] = hn_vmem[src, :]

    def block(i, parity, cur, nxt, ybuf):
        @pl.when(jnp.logical_or(i == 0, be_ref[i] != be_ref[jnp.maximum(i - 1, 0)]))
        def _():
            slot = jnp.bitwise_and(run_ref[i], 1)
            for c in weight_copies(be_ref[i], slot):
                c.wait()
            wg_bf[...] = wg_f[slot].astype(BF16)
            wu_bf[...] = wu_f[slot].astype(BF16)
            wd_bf[...] = wd_f[slot].astype(BF16)

            @pl.when(next_ref[i] < n_used)
            def _():
                for c in weight_copies(be_ref[jnp.minimum(next_ref[i], n_blocks - 1)], 1 - slot):
                    c.start()

        @pl.when(i >= 2)
        def _():
            y_copy(ybuf, parity, i).wait()

        kh = D_MODEL // 2
        gather(i + 1, nxt)
        x_lo, x_hi = _unpack_bf16_pairs(_load_row_tiles(cur, ROW_BLOCK))
        gate = _dot(x_lo, wg_bf[:kh, :]) + _dot(x_hi, wg_bf[kh:, :])
        up = _dot(x_lo, wu_bf[:kh, :]) + _dot(x_hi, wu_bf[kh:, :])
        hid = (_silu(gate) * up).astype(BF16)
        _store_row_tiles(ybuf, _dot(hid, wd_bf[...]))
        y_copy(ybuf, parity, i).start()

    hn_load = pltpu.make_async_copy(hn_hbm, hn_vmem, hn_sem)
    hn_load.start()
    for c in weight_copies(be_ref[0], 0):
        c.start()
    hn_load.wait()
    gather(0, xg_even)

    def pair(p, carry):
        block(2 * p, 0, xg_even, xg_odd, y_even)

        @pl.when(2 * p + 1 < n_used)
        def _():
            block(2 * p + 1, 1, xg_odd, xg_even, y_odd)
        return carry

    lax.fori_loop(0, lax.shift_right_logical(n_used + 1, 1), pair, 0)

    y_copy(y_even, 0, 0).wait()

    @pl.when(n_used >= 2)
    def _():
        y_copy(y_odd, 1, 0).wait()

    y_even[...] = jnp.zeros_like(y_even)

    def zero_start(i, carry):
        y_copy(y_even, 0, i).start()
        return carry

    def zero_wait(i, carry):
        y_copy(y_even, 0, i).wait()
        return carry

    lax.fori_loop(n_used, n_blocks, zero_start, 0)
    lax.fori_loop(n_used, n_blocks, zero_wait, 0)


def _experts(hn, tok_of_row, block_expert, block_run, block_next, n_used, w_gate, w_up, w_down):
    n_rows = tok_of_row.shape[0]
    hbm = pl.BlockSpec(memory_space=pl.ANY)
    grid_spec = pltpu.PrefetchScalarGridSpec(
        num_scalar_prefetch=5,
        grid=(1,),
        in_specs=[hbm, hbm, hbm, hbm],
        out_specs=hbm,
        scratch_shapes=[pltpu.VMEM(hn.shape, jnp.uint32),
                        pltpu.VMEM((ROW_BLOCK * X_TILE, LANES), jnp.uint32),
                        pltpu.VMEM((ROW_BLOCK * X_TILE, LANES), jnp.uint32),
                        pltpu.VMEM((ROW_BLOCK * ROW_TILE, LANES), F32),
                        pltpu.VMEM((ROW_BLOCK * ROW_TILE, LANES), F32),
                        pltpu.VMEM((2, D_MODEL, MOE_FF), F32), pltpu.VMEM((2, D_MODEL, MOE_FF), F32),
                        pltpu.VMEM((2, MOE_FF, D_MODEL), F32),
                        pltpu.VMEM((D_MODEL, MOE_FF), BF16), pltpu.VMEM((D_MODEL, MOE_FF), BF16),
                        pltpu.VMEM((MOE_FF, D_MODEL), BF16),
                        pltpu.SemaphoreType.DMA, pltpu.SemaphoreType.DMA((2,)), pltpu.SemaphoreType.DMA((2,))],
    )
    return pl.pallas_call(
        _experts_kernel,
        grid_spec=grid_spec,
        out_shape=jax.ShapeDtypeStruct((n_rows * ROW_TILE, LANES), F32),
        compiler_params=_params(("arbitrary",)),
        name="experts",
    )(block_expert, block_run, block_next, n_used, tok_of_row, hn, w_gate, w_up, w_down)


COMB_TM = 256


def _combine_kernel(dest_ref, dest_next_ref, x1_ref, gate_ref, ys_ref, out_ref, ybuf, sems):
    g = pl.program_id(0)

    def row_copy(d, slot, k, r):
        src = ys_ref.at[pl.ds(pl.multiple_of(d * ROW_TILE, ROW_TILE), ROW_TILE), :]
        dst = ybuf.at[slot, k, pl.ds(pl.multiple_of(r * ROW_TILE, ROW_TILE), ROW_TILE), :]
        return pltpu.make_async_copy(src, dst, sems.at[slot])

    def gather(dref, base, slot):
        def issue(r, c):
            row_copy(dref[0, 0, 0, base + r], slot, 0, r).start(priority=0)
            row_copy(dref[1, 0, 0, base + r], slot, 1, r).start(priority=1)
            return c
        lax.fori_loop(0, COMB_TM, issue, 0, unroll=DMA_UNROLL)

    def finish(slot):
        for k in range(2):
            pltpu.make_async_copy(ys_ref.at[pl.ds(0, COMB_TM * ROW_TILE), :], ybuf.at[slot, k],
                                  sems.at[slot]).wait()
        rows = slice(slot * COMB_TM, (slot + 1) * COMB_TM)
        gate = gate_ref[:, rows].T
        w0 = gate[:, 0:1]
        w1 = gate[:, 1:2]
        y0 = _load_row_tiles(ybuf.at[slot, 0], COMB_TM)
        y1 = _load_row_tiles(ybuf.at[slot, 1], COMB_TM)
        out_ref[rows, :] = x1_ref[rows, :] + (y0 * w0 + y1 * w1)

    @pl.when(g == 0)
    def _():
        gather(dest_ref, 0, 0)

    gather(dest_ref, COMB_TM, 1)
    finish(0)

    @pl.when(g + 1 < pl.num_programs(0))
    def _():
        gather(dest_next_ref, 0, 0)

    finish(1)


def _combine(x1, gates, ys, dest):
    t = x1.shape[0]
    ng = t // (2 * COMB_TM)
    dest3 = dest.reshape(2, ng, 1, 2 * COMB_TM)
    smem = lambda f: pl.BlockSpec((2, 1, 1, 2 * COMB_TM), f, memory_space=pltpu.SMEM)
    return pl.pallas_call(
        _combine_kernel,
        grid=(ng,),
        in_specs=[smem(lambda i: (0, i, 0, 0)),
                  smem(lambda i: (0, jnp.minimum(i + 1, ng - 1), 0, 0)),
                  pl.BlockSpec((2 * COMB_TM, D_MODEL), lambda i: (i, 0)),
                  pl.BlockSpec((8, 2 * COMB_TM), lambda i: (0, i)),
                  pl.BlockSpec(memory_space=pl.ANY)],
        out_specs=pl.BlockSpec((2 * COMB_TM, D_MODEL), lambda i: (i, 0)),
        out_shape=jax.ShapeDtypeStruct((t, D_MODEL), F32),
        scratch_shapes=[pltpu.VMEM((2, 2, COMB_TM * ROW_TILE, LANES), F32),
                        pltpu.SemaphoreType.DMA((2,))],
        compiler_params=_params(("arbitrary",)),
        name="combine",
    )(dest3, dest3, x1, gates, ys)


def _layer(x, mem, attn_norm_g, mem_norm_g, w_in, w_gla_gk, b_gla_gk, gla_out_norm_g,
           moba_q_norm_g, moba_k_norm_g, w_mem_kv, mem_q_norm_g, mem_k_norm_g, w_out,
           ffn_norm_g, w_router_group, b_router_group, w_router_expert, b_router_expert,
           w_gate, w_up, w_down):
    b, s, d = x.shape
    t = b * s
    x2 = x.reshape(t, d)
    row = lambda v: v.reshape(1, -1).astype(F32)
    tile_heads = lambda v: jnp.tile(v.astype(F32), HEADS).reshape(1, QK_W)
    hid = np.arange(QK_W) // DH
    seg = jnp.asarray((hid[:, None] == hid[None, :]) / DH, BF16)

    w_in_p = jnp.concatenate([w_in[:, :1536], w_in[:, 1552:], w_in[:, 1536:1552]], axis=1).astype(BF16)
    wr = jnp.concatenate([w_router_group,
                          jnp.transpose(w_router_expert, (1, 0, 2)).reshape(d, N_EXPERTS),
                          jnp.zeros((d, LANES - N_GROUPS - N_EXPERTS), F32)], axis=1).astype(BF16)
    br = jnp.concatenate([b_router_group, b_router_expert.reshape(N_EXPERTS),
                          jnp.zeros((LANES - N_GROUPS - N_EXPERTS,), F32)]).reshape(1, LANES)

    memk, memvt = _mem_kv(mem, row(mem_norm_g), w_mem_kv.astype(BF16), tile_heads(mem_k_norm_g), seg)
    qk, gv, gr, glr, mq, mk, mvt, kmean, cq = _in_proj(
        x2, row(attn_norm_g), w_in_p, seg, tile_heads(moba_q_norm_g), tile_heads(moba_k_norm_g),
        tile_heads(mem_q_norm_g))
    o_gla = _gla(qk, gv, gr, glr, w_gla_gk.astype(BF16), row(b_gla_gk), row(gla_out_norm_g), b, s)
    o_moba, o_mem = _moba_mem(mq, mk, mvt, kmean, cq, memk, memvt, b, s)
    x1, hn, e_ids, gates, counts = _out_router(
        x2, o_gla.reshape(t, GLA_V_W), o_moba.reshape(t, QK_W), o_mem.reshape(t, QK_W),
        w_out.astype(BF16), row(ffn_norm_g), wr, br)

    counts = counts[N_GROUPS:N_GROUPS + N_EXPERTS, 0]
    padded = (counts + ROW_BLOCK - 1) // ROW_BLOCK * ROW_BLOCK
    pends = jnp.cumsum(padded)
    pstarts = pends - padded
    onehot = e_ids[:2, :, None] == jnp.arange(N_EXPERTS, dtype=jnp.int32)
    dest = (jnp.sum(jnp.where(onehot, pstarts, 0), axis=-1) + e_ids[2:4]).astype(jnp.int32)
    n_rows = (t * 2 + N_EXPERTS * (ROW_BLOCK - 1) + ROW_BLOCK - 1) // ROW_BLOCK * ROW_BLOCK
    nb = n_rows // ROW_BLOCK
    block_start = jnp.arange(nb, dtype=jnp.int32) * ROW_BLOCK
    block_expert = jnp.minimum(jnp.sum(block_start[:, None] >= pends[None, :], axis=1),
                               N_EXPERTS - 1).astype(jnp.int32)
    n_used = (pends[-1] // ROW_BLOCK).astype(jnp.int32).reshape(1)
    ended = block_start[:, None] >= pends[None, :]
    block_run = jnp.sum(ended & (counts > 0)[None, :], axis=1).astype(jnp.int32)
    block_next = (jnp.min(jnp.where(ended, n_rows, pends[None, :]), axis=1) // ROW_BLOCK).astype(jnp.int32)

    pad_lo = jnp.concatenate([pstarts + counts, pends[-1:]]).astype(jnp.int32)
    pad_hi = jnp.concatenate([pends, jnp.full((1,), n_rows)]).astype(jnp.int32)
    tok_of_row = _invert(dest.reshape(-1), pad_lo, pad_hi, n_rows)
    ys = _experts(hn, tok_of_row, block_expert, block_run, block_next, n_used, w_gate, w_up, w_down)
    out = _combine(x1, gates, ys, dest)
    return out.reshape(b, s, d)


def kernel(x, mem, attn_norm_g, mem_norm_g, w_in, w_gla_gk, b_gla_gk, gla_out_norm_g, moba_q_norm_g, moba_k_norm_g, w_mem_kv, mem_q_norm_g, mem_k_norm_g, w_out, ffn_norm_g, w_router_group, b_router_group, w_router_expert, b_router_expert, w_gate, w_up, w_down):
    depth = w_in.shape[0]
    for l in range(depth):
        x = _layer(x, mem, attn_norm_g[l], mem_norm_g[l], w_in[l], w_gla_gk[l], b_gla_gk[l],
                   gla_out_norm_g[l], moba_q_norm_g[l], moba_k_norm_g[l], w_mem_kv[l],
                   mem_q_norm_g[l], mem_k_norm_g[l], w_out[l], ffn_norm_g[l], w_router_group[l],
                   b_router_group[l], w_router_expert[l], b_router_expert[l],
                   w_gate[l], w_up[l], w_down[l])
    return x
```

```python
import functools

import jax
import jax.numpy as jnp
import numpy as np
from jax import lax
from jax.experimental import pallas as pl
from jax.experimental.pallas import tpu as pltpu

F32 = jnp.float32
BF16 = jnp.bfloat16
EPS = 1e-6
NEG_INF = float("-inf")

D_MODEL = 1024
N_MEM = 256
HEADS = 4
DH = 64
GLA_DV = 128
GLA_RANK = 16
GLA_GATE_NORMALIZER = 16.0
GLA_CHUNK = 64
MOBA_BLOCK = 256
MOBA_TOPK = 3
QK_W = HEADS * DH
GLA_V_W = HEADS * GLA_DV
N_GROUPS = 4
EXPERTS_PER_GROUP = 8
N_EXPERTS = N_GROUPS * EXPERTS_PER_GROUP
MOE_FF = 512
LANES = 128
ROW_BLOCK = 256

VMEM_LIMIT = 56 * 1024 * 1024


def _params(sem):
    return pltpu.CompilerParams(dimension_semantics=sem, vmem_limit_bytes=VMEM_LIMIT)


def _nt(a, b):
    return lax.dot_general(a, b, (((1,), (1,)), ((), ())), preferred_element_type=F32)


def _tn(a, b):
    return lax.dot_general(a, b, (((0,), (0,)), ((), ())), preferred_element_type=F32)


def _dot(a, b):
    return jnp.dot(a, b, preferred_element_type=F32)


def _rms_rows(x, g):
    ms = jnp.mean(x * x, axis=-1, keepdims=True)
    return x * lax.rsqrt(ms + EPS) * g


def _split3(x):
    h1 = x.astype(BF16)
    r1 = x - h1.astype(F32)
    h2 = r1.astype(BF16)
    h3 = (r1 - h2.astype(F32)).astype(BF16)
    return h1, h2, h3


def _head_mean_sq(x, seg):
    sq = x * x
    hi = sq.astype(BF16)
    lo = (sq - hi.astype(F32)).astype(BF16)
    return _dot(hi, seg) + _dot(lo, seg)


def _silu(x):
    return x * (1.0 / (1.0 + jnp.exp(-x)))


VT_ROWS = DH + 16
VT_ALL = HEADS * VT_ROWS


ALIBI_SLOPES = tuple(2.0 ** (-8.0 * (h + 1) / HEADS) for h in range(HEADS))


def _vt_with_ones(v, key_slopes=None):
    n = v.shape[0]
    vt = v.T
    ones = jnp.ones((VT_ROWS - DH, n), F32)
    pos = lax.broadcasted_iota(jnp.int32, (1, n), 1).astype(F32)
    parts = []
    for h in range(HEADS):
        scale = 1.0 if key_slopes is None else jnp.exp(key_slopes[h] * pos)
        parts += [vt[h * DH:(h + 1) * DH] * scale, ones * scale]
    return jnp.concatenate(parts, axis=0).astype(BF16)


ROW_TILE = D_MODEL // LANES
X_TILE = ROW_TILE // 2


def _store_row_tiles(ref, x):
    n, tiles = x.shape[0], x.shape[1] // LANES
    for c in range(tiles):
        ref[pl.ds(c, n, stride=tiles), :] = x[:, c * LANES:(c + 1) * LANES]


def _load_row_tiles(ref, n):
    tiles = ref.shape[0] // n
    return jnp.concatenate([ref[pl.ds(c, n, stride=tiles), :] for c in range(tiles)], axis=1)


def _pack_bf16_pairs(x):
    w = x.shape[1] // 2
    bits = lambda t: lax.bitcast_convert_type(t.astype(BF16).astype(F32), jnp.uint32)
    return (bits(x[:, w:]) & jnp.uint32(0xFFFF0000)) | (bits(x[:, :w]) >> 16)


def _unpack_bf16_pairs(words):
    lo = lax.bitcast_convert_type(words << 16, F32).astype(BF16)
    hi = lax.bitcast_convert_type(words & jnp.uint32(0xFFFF0000), F32).astype(BF16)
    return lo, hi


def _mem_kv_kernel(mem_ref, g_ref, w_ref, gk_ref, seg_ref, k_ref, vt_ref):
    h = _rms_rows(mem_ref[0], g_ref[...]).astype(BF16)
    kv = _dot(h, w_ref[...])
    k = kv[:, :QK_W]
    kn = k * lax.rsqrt(_head_mean_sq(k, seg_ref[...]) + EPS) * gk_ref[...]
    k_ref[0] = kn.astype(BF16)
    vt_ref[0] = _vt_with_ones(kv[:, QK_W:])


def _mem_kv(mem, g, w_bf, gk_t, seg):
    b = mem.shape[0]
    return pl.pallas_call(
        _mem_kv_kernel,
        grid=(b,),
        in_specs=[
            pl.BlockSpec((1, N_MEM, D_MODEL), lambda i: (i, 0, 0)),
            pl.BlockSpec((1, D_MODEL), lambda i: (0, 0)),
            pl.BlockSpec((D_MODEL, 2 * QK_W), lambda i: (0, 0)),
            pl.BlockSpec((1, QK_W), lambda i: (0, 0)),
            pl.BlockSpec((QK_W, QK_W), lambda i: (0, 0)),
        ],
        out_specs=[
            pl.BlockSpec((1, N_MEM, QK_W), lambda i: (i, 0, 0)),
            pl.BlockSpec((1, VT_ALL, N_MEM), lambda i: (i, 0, 0)),
        ],
        out_shape=[
            jax.ShapeDtypeStruct((b, N_MEM, QK_W), BF16),
            jax.ShapeDtypeStruct((b, VT_ALL, N_MEM), BF16),
        ],
        compiler_params=_params(("arbitrary",)),
        name="mem_kv",
    )(mem, g, w_bf, gk_t, seg)


IN_TM = 1024
_C_QK, _C_V, _C_R, _C_MQ, _C_MK, _C_MV, _C_CQ, _C_LR = 0, 512, 1024, 1536, 1792, 2048, 2304, 2560
D_IN = 2576


def _in_proj_kernel(x_ref, g_ref, w_ref, seg_ref, gq_ref, gk_ref, gc_ref,
                    qk_ref, v_ref, r_ref, lr_ref, mq_ref, mk_ref, mvt_ref, kmean_ref, cq_ref):
    h = _rms_rows(x_ref[...], g_ref[...]).astype(BF16)
    p = _dot(h, w_ref[...])
    seg = seg_ref[...]
    qk_ref[...] = p[:, _C_QK:_C_V]
    v_ref[...] = p[:, _C_V:_C_R].astype(BF16)
    r_ref[...] = p[:, _C_R:_C_MQ]
    lr_ref[...] = p[:, _C_LR:D_IN]

    def head_norm(t, gain):
        return t * lax.rsqrt(_head_mean_sq(t, seg) + EPS) * gain

    scale = DH ** -0.5
    mq_ref[...] = (head_norm(p[:, _C_MQ:_C_MK], gq_ref[...]) * scale).astype(BF16)
    cq_ref[...] = (head_norm(p[:, _C_CQ:_C_LR], gc_ref[...]) * scale).astype(BF16)
    kn = head_norm(p[:, _C_MK:_C_MV], gk_ref[...])
    mk_ref[...] = kn.astype(BF16)
    mv = p[:, _C_MV:_C_CQ]
    for j in range(IN_TM // MOBA_BLOCK):
        rows = slice(j * MOBA_BLOCK, (j + 1) * MOBA_BLOCK)
        kmean_ref[0, j:j + 1, :] = jnp.mean(kn[rows], axis=0, keepdims=True)
        mvt_ref[j] = _vt_with_ones(mv[rows], ALIBI_SLOPES)


def _in_proj(x2, g, w_bf, seg, gq_t, gk_t, gc_t):
    t = x2.shape[0]
    nt = t // IN_TM
    nb = IN_TM // MOBA_BLOCK
    row = lambda w: pl.BlockSpec((IN_TM, w), lambda i: (i, 0))
    const = lambda a, b: pl.BlockSpec((a, b), lambda i: (0, 0))
    return pl.pallas_call(
        _in_proj_kernel,
        grid=(nt,),
        in_specs=[row(D_MODEL), const(1, D_MODEL), const(D_MODEL, D_IN), const(QK_W, QK_W),
                  const(1, QK_W), const(1, QK_W), const(1, QK_W)],
        out_specs=[row(2 * QK_W), row(GLA_V_W), row(GLA_V_W), row(GLA_RANK), row(QK_W), row(QK_W),
                   pl.BlockSpec((nb, VT_ALL, MOBA_BLOCK), lambda i: (i, 0, 0)),
                   pl.BlockSpec((1, nb, QK_W), lambda i: (i, 0, 0)),
                   row(QK_W)],
        out_shape=[
            jax.ShapeDtypeStruct((t, 2 * QK_W), F32),
            jax.ShapeDtypeStruct((t, GLA_V_W), BF16),
            jax.ShapeDtypeStruct((t, GLA_V_W), F32),
            jax.ShapeDtypeStruct((t, GLA_RANK), F32),
            jax.ShapeDtypeStruct((t, QK_W), BF16),
            jax.ShapeDtypeStruct((t, QK_W), BF16),
            jax.ShapeDtypeStruct((t // MOBA_BLOCK, VT_ALL, MOBA_BLOCK), BF16),
            jax.ShapeDtypeStruct((nt, nb, QK_W), F32),
            jax.ShapeDtypeStruct((t, QK_W), BF16),
        ],
        compiler_params=_params(("arbitrary",)),
        name="in_proj",
    )(x2, g, w_bf, seg, gq_t, gk_t, gc_t)


GLA_TC = 256


def _gla_kernel(qk_ref, v_ref, r_ref, lr_ref, wgk_ref, bgk_ref, gn_ref, tri_ref, ones_ref,
                o_ref, st_ref):
    @pl.when(pl.program_id(1) == 0)
    def _():
        st_ref[...] = jnp.zeros_like(st_ref)

    qk = qk_ref[0]
    q = qk[:, :QK_W]
    k = qk[:, QK_W:]
    gk = _dot(lr_ref[0].astype(BF16), wgk_ref[...]) + bgk_ref[...]
    g = -(jnp.maximum(-gk, 0.0) + jnp.log1p(jnp.exp(-jnp.abs(gk)))) / GLA_GATE_NORMALIZER
    g1, g2, g3 = _split3(g)
    tri = tri_ref[...]
    ones = ones_ref[...]
    cum = _dot(tri, g1) + _dot(tri, g2) + _dot(tri, g3)
    tot = _dot(ones, g1) + _dot(ones, g2) + _dot(ones, g3)
    q_dec = (q * (DH ** -0.5) * jnp.exp(cum)).astype(BF16)
    k_inv = (k * jnp.exp(-cum)).astype(BF16)
    k_end = (k * jnp.exp(tot - cum)).astype(BF16)
    decay = jnp.exp(tot)

    lane_head = lax.broadcasted_iota(jnp.int32, (GLA_TC, QK_W), 1) // DH
    row_t = lax.broadcasted_iota(jnp.int32, (GLA_TC, GLA_TC), 0)
    col_t = lax.broadcasted_iota(jnp.int32, (GLA_TC, GLA_TC), 1)
    causal = (row_t >= col_t) & (row_t // GLA_CHUNK == col_t // GLA_CHUNK)
    same_head = (lax.broadcasted_iota(jnp.int32, (GLA_V_W, QK_W), 0) // GLA_DV
                 == lax.broadcasted_iota(jnp.int32, (GLA_V_W, QK_W), 1) // DH)
    gain = gn_ref[...]

    v = v_ref[0]
    chunks = [slice(c * GLA_CHUNK, (c + 1) * GLA_CHUNK) for c in range(GLA_TC // GLA_CHUNK)]
    d_st = [jnp.where(same_head, _tn(v[rows], k_end[rows]), 0.0) for rows in chunks]
    states = [st_ref[...]]
    for c, rows in enumerate(chunks):
        states.append(states[c] * decay[rows.start:rows.start + 1, :] + d_st[c])
    st_ref[...] = states[-1]
    o_inter = jnp.concatenate([_nt(q_dec[rows], states[c].astype(BF16)) for c, rows in enumerate(chunks)],
                              axis=0)
    q_stack = jnp.concatenate([jnp.where(lane_head == h, q_dec, jnp.zeros_like(q_dec)) for h in range(HEADS)],
                              axis=0)
    a_all = _nt(q_stack, k_inv)
    outs = []
    for h in range(HEADS):
        a = jnp.where(causal, a_all[h * GLA_TC:(h + 1) * GLA_TC], 0.0).astype(BF16)
        vs = slice(h * GLA_DV, (h + 1) * GLA_DV)
        oh = _dot(a, v[:, vs]) + o_inter[:, vs]
        outs.append(_rms_rows(oh, gain) * _silu(r_ref[0, :, vs]))
    o_ref[0] = jnp.concatenate(outs, axis=-1).astype(BF16)


def _gla(qk, v, r, lr, wgk_bf, bgk, gn, b, s):
    ns = s // GLA_TC
    idx = np.arange(GLA_TC)
    same_chunk = (idx[:, None] // GLA_CHUNK) == (idx[None, :] // GLA_CHUNK)
    tri = jnp.asarray(same_chunk & (idx[:, None] >= idx[None, :]), BF16)
    ones = jnp.asarray(same_chunk, BF16)
    seq = lambda w: pl.BlockSpec((1, GLA_TC, w), lambda i, j: (i, j, 0))
    const = lambda a, c: pl.BlockSpec((a, c), lambda i, j: (0, 0))
    return pl.pallas_call(
        _gla_kernel,
        grid=(b, ns),
        in_specs=[seq(2 * QK_W), seq(GLA_V_W), seq(GLA_V_W), seq(GLA_RANK),
                  const(GLA_RANK, QK_W), const(1, QK_W), const(1, GLA_DV),
                  const(GLA_TC, GLA_TC), const(GLA_TC, GLA_TC)],
        out_specs=seq(GLA_V_W),
        out_shape=jax.ShapeDtypeStruct((b, s, GLA_V_W), BF16),
        scratch_shapes=[pltpu.VMEM((GLA_V_W, QK_W), F32)],
        compiler_params=_params(("arbitrary", "arbitrary")),
        name="gla",
    )(qk.reshape(b, s, 2 * QK_W), v.reshape(b, s, GLA_V_W), r.reshape(b, s, GLA_V_W),
      lr.reshape(b, s, GLA_RANK), wgk_bf, bgk, gn, tri, ones)


TQ = MOBA_BLOCK
KV_UNROLL = 8


def _moba_mem_kernel(q_ref, k_ref, vt_ref, kmean_ref, cq_ref, mk_ref, mvt_ref,
                     o_moba_ref, o_mem_ref,
                     qcat_ref, selb_ref, s_ref, *, n_blocks):
    i = pl.program_id(1)
    lane_head = lax.broadcasted_iota(jnp.int32, (TQ, QK_W), 1) // DH
    dist0 = (lax.broadcasted_iota(jnp.int32, (MOBA_BLOCK, TQ), 1)
             - lax.broadcasted_iota(jnp.int32, (MOBA_BLOCK, TQ), 0)).astype(F32)
    slopes = ALIBI_SLOPES
    heads = [slice(h * TQ, (h + 1) * TQ) for h in range(HEADS)]
    vrows = [slice(h * VT_ROWS, (h + 1) * VT_ROWS) for h in range(HEADS)]

    def stack_heads(x):
        return jnp.concatenate([jnp.where(lane_head == h, x, jnp.zeros_like(x)) for h in range(HEADS)], axis=0)

    def finish(accl):
        return accl[:DH] * (1.0 / accl[DH:DH + 1])

    s_all = _nt(mk_ref[0], stack_heads(cq_ref[0]))
    outs = []
    for h in range(HEADS):
        s = s_all[:, heads[h]]
        p = jnp.exp(s - jnp.max(s, axis=0, keepdims=True)).astype(BF16)
        outs.append(finish(_dot(mvt_ref[0, vrows[h], :], p)))
    o_mem_ref[0] = jnp.concatenate(outs, axis=0).T.astype(BF16)

    qcat = stack_heads(q_ref[0])
    qcat_ref[...] = qcat
    k_own = k_ref[0, pl.ds(pl.multiple_of(i * MOBA_BLOCK, MOBA_BLOCK), MOBA_BLOCK), :]
    gate_all = _nt(kmean_ref[0].astype(BF16), qcat)
    s_all = _nt(k_own, qcat)
    blk = lax.broadcasted_iota(jnp.int32, (n_blocks, TQ), 0)
    blk_f = blk.astype(F32)

    tq = lax.broadcasted_iota(jnp.int32, (1, TQ), 1).astype(F32)
    init = []
    for h in range(HEADS):
        gate = jnp.where(blk < i, gate_all[:, heads[h]], NEG_INF)
        chosen = jnp.zeros((n_blocks, TQ), jnp.bool_)
        for r in range(MOBA_TOPK):
            mx = jnp.max(gate, axis=0, keepdims=True)
            first = jnp.min(jnp.where(gate == mx, blk_f, float(n_blocks)), axis=0, keepdims=True)
            hit = blk_f == first
            chosen = chosen | (hit & (mx > NEG_INF))
            gate = jnp.where(hit, NEG_INF, gate)
        selb_ref[h] = jnp.where(chosen, 0.0, NEG_INF)
        aq = -slopes[h] * tq
        s = jnp.where(dist0 >= 0.0, s_all[:, heads[h]], NEG_INF)
        m0 = jnp.max(s, axis=0, keepdims=True) + aq
        p = jnp.exp(s - (m0 - aq)).astype(BF16)
        init += [m0, _dot(vt_ref[i, vrows[h], :], p)]

    def stage_scores(j, slot):
        kj = k_ref[0, pl.ds(pl.multiple_of(j * MOBA_BLOCK, MOBA_BLOCK), MOBA_BLOCK), :]
        s_ref[slot] = _nt(kj, qcat_ref[...])

    stage_scores(0, 0)

    def past_blocks(g, carry, unroll, base):
        carry = list(carry)
        for u in range(unroll):
            j = base + g * unroll + u
            stage_scores(jnp.minimum(j + 1, n_blocks - 1), (u + 1) % 2)
            off = jnp.full((1, TQ), (i - j) * MOBA_BLOCK, jnp.int32).astype(F32) + tq
            for h in range(HEADS):
                m, accl = carry[2 * h:2 * h + 2]
                s1 = s_ref[u % 2, :, heads[h]]
                rb = selb_ref[h, pl.ds(j, 1), :] - slopes[h] * off
                m_new = jnp.maximum(m, jnp.max(s1, axis=0, keepdims=True) + rb)
                p = jnp.exp(s1 - (m_new - rb)).astype(BF16)
                carry[2 * h:2 * h + 2] = [m_new, jnp.exp(m - m_new) * accl + _dot(vt_ref[j, vrows[h], :], p)]
        return tuple(carry)

    n_main = i // KV_UNROLL
    rest = i - n_main * KV_UNROLL
    carry = lax.fori_loop(0, n_main, functools.partial(past_blocks, unroll=KV_UNROLL, base=0), tuple(init))
    final = lax.fori_loop(0, (rest + 1) // 2,
                          functools.partial(past_blocks, unroll=2, base=n_main * KV_UNROLL), carry)
    o_t = jnp.concatenate([finish(final[2 * h + 1]) for h in range(HEADS)], axis=0)
    o_moba_ref[0] = o_t.T.astype(BF16)


def _moba_mem(mq, mk, mvt, kmean, cq, memk, memvt, b, s):
    nq = s // TQ
    n_blocks = s // MOBA_BLOCK
    qspec = pl.BlockSpec((1, TQ, QK_W), lambda i, j: (i, j, 0))
    return pl.pallas_call(
        functools.partial(_moba_mem_kernel, n_blocks=n_blocks),
        grid=(b, nq),
        in_specs=[
            qspec,
            pl.BlockSpec((1, s, QK_W), lambda i, j: (i, 0, 0)),
            pl.BlockSpec((n_blocks, VT_ALL, MOBA_BLOCK), lambda i, j: (i, 0, 0)),
            pl.BlockSpec((1, n_blocks, QK_W), lambda i, j: (i, 0, 0)),
            qspec,
            pl.BlockSpec((1, N_MEM, QK_W), lambda i, j: (i, 0, 0)),
            pl.BlockSpec((1, VT_ALL, N_MEM), lambda i, j: (i, 0, 0)),
        ],
        out_specs=[qspec, qspec],
        out_shape=[jax.ShapeDtypeStruct((b, s, QK_W), BF16), jax.ShapeDtypeStruct((b, s, QK_W), BF16)],
        scratch_shapes=[
            pltpu.VMEM((HEADS * TQ, QK_W), BF16),
            pltpu.VMEM((HEADS, n_blocks, TQ), F32),
            pltpu.VMEM((2, MOBA_BLOCK, HEADS * TQ), F32),
        ],
        compiler_params=_params(("arbitrary", "arbitrary")),
        name="moba_mem",
    )(mq.reshape(b, s, QK_W), mk.reshape(b, s, QK_W), mvt, kmean.reshape(b, n_blocks, QK_W),
      cq.reshape(b, s, QK_W), memk, memvt)


OUT_TM = 1024
ROUTE_ROWS = 40


def _out_router_kernel(x_ref, og_ref, om_ref, oc_ref, w_ref, g_ref, wr_ref, br_ref, tri_ref,
                       x1_ref, h_ref, e_ref, gate_ref, count_ref, carry_ref):
    w = w_ref[...]
    x1 = (x_ref[...] + _dot(og_ref[...], w[:GLA_V_W]) + _dot(om_ref[...], w[GLA_V_W:GLA_V_W + QK_W])
          + _dot(oc_ref[...], w[GLA_V_W + QK_W:]))
    x1_ref[...] = x1
    hn = _rms_rows(x1, g_ref[...])
    _store_row_tiles(h_ref, _pack_bf16_pairs(hn))
    logits = _dot(hn.astype(BF16), wr_ref[...]) + br_ref[...]
    lt = logits.T[:ROUTE_ROWS]
    row = lax.broadcasted_iota(jnp.int32, lt.shape, 0).astype(F32)
    lg = jnp.where(row < N_GROUPS, lt, NEG_INF)
    mg = jnp.max(lg, axis=0, keepdims=True)
    g_sel = jnp.min(jnp.where(lg == mg, row, float(LANES)), axis=0, keepdims=True)
    p_group = 1.0 / jnp.sum(jnp.exp(lg - mg), axis=0, keepdims=True)
    lo = N_GROUPS + g_sel * EXPERTS_PER_GROUP
    le = jnp.where((row >= lo) & (row < lo + EXPERTS_PER_GROUP), lt, NEG_INF)
    m0 = jnp.max(le, axis=0, keepdims=True)
    i0 = jnp.min(jnp.where(le == m0, row, float(LANES)), axis=0, keepdims=True)
    le1 = jnp.where(row == i0, NEG_INF, le)
    m1 = jnp.max(le1, axis=0, keepdims=True)
    i1 = jnp.min(jnp.where(le1 == m1, row, float(LANES)), axis=0, keepdims=True)
    z = jnp.exp(m1 - m0)
    w0 = p_group / (1.0 + z)
    w1 = p_group * z / (1.0 + z)
    pad = jnp.zeros((8 - 2, lt.shape[1]), F32)
    gate_ref[...] = jnp.concatenate([w0, w1, pad], axis=0)

    @pl.when(pl.program_id(0) == 0)
    def _():
        carry_ref[...] = jnp.zeros_like(carry_ref)

    oh0 = (row == i0).astype(F32)
    oh1 = (row == i1).astype(F32)
    both = oh0 + oh1
    before = _dot(both.astype(BF16), tri_ref[...]) + carry_ref[:, 0:1]
    r0 = jnp.sum(oh0 * before, axis=0, keepdims=True)
    r1 = jnp.sum(oh1 * before, axis=0, keepdims=True)
    carry_ref[...] = carry_ref[...] + jnp.sum(both, axis=1, keepdims=True)
    count_ref[...] = carry_ref[...].astype(jnp.int32)
    e_ref[...] = jnp.concatenate([i0 - N_GROUPS, i1 - N_GROUPS, r0, r1, pad[:4]], axis=0).astype(jnp.int32)


def _out_router(x2, og, om, oc, w_bf, g, wr_bf, br):
    t = x2.shape[0]
    row = lambda w: pl.BlockSpec((OUT_TM, w), lambda i: (i, 0))
    const = lambda a, b: pl.BlockSpec((a, b), lambda i: (0, 0))
    idx = np.arange(OUT_TM)
    tri = jnp.asarray(idx[:, None] < idx[None, :], BF16)
    col = lambda r: pl.BlockSpec((r, OUT_TM), lambda i: (0, i))
    return pl.pallas_call(
        _out_router_kernel,
        grid=(t // OUT_TM,),
        in_specs=[row(D_MODEL), row(GLA_V_W), row(QK_W), row(QK_W), const(D_MODEL, D_MODEL),
                  const(1, D_MODEL), const(D_MODEL, LANES), const(1, LANES), const(OUT_TM, OUT_TM)],
        out_specs=[row(D_MODEL), pl.BlockSpec((OUT_TM * X_TILE, LANES), lambda i: (i, 0)),
                   col(8), col(8), const(ROUTE_ROWS, LANES)],
        out_shape=[
            jax.ShapeDtypeStruct((t, D_MODEL), F32),
            jax.ShapeDtypeStruct((t * X_TILE, LANES), jnp.uint32),
            jax.ShapeDtypeStruct((8, t), jnp.int32),
            jax.ShapeDtypeStruct((8, t), F32),
            jax.ShapeDtypeStruct((ROUTE_ROWS, LANES), jnp.int32),
        ],
        scratch_shapes=[pltpu.VMEM((ROUTE_ROWS, LANES), F32)],
        compiler_params=_params(("arbitrary",)),
        name="out_router",
    )(x2, og, om, oc, w_bf, g, wr_bf, br, tri)


DMA_UNROLL = 8
PLACE_UNROLL = 32


def _invert_kernel(dest_ref, pad_lo_ref, pad_hi_ref, tok_ref):
    def pad_segment(e, c):
        hi = pad_hi_ref[e]

        def pad(g, c2):
            for r in range(8):
                tok_ref[hi - 8 * (g + 1) + r] = 0
            return c2
        lax.fori_loop(0, lax.shift_right_logical(hi - pad_lo_ref[e] + 7, 3), pad, 0)
        return c

    lax.fori_loop(0, pad_lo_ref.shape[0], pad_segment, 0)

    n_tok = dest_ref.shape[0] // 2

    def place(t, c):
        tok_ref[dest_ref[t]] = t
        tok_ref[dest_ref[n_tok + t]] = t
        return c

    lax.fori_loop(0, n_tok, place, 0, unroll=PLACE_UNROLL)


def _invert(dest, pad_lo, pad_hi, n_rows):
    smem = pl.BlockSpec(memory_space=pltpu.SMEM)
    return pl.pallas_call(
        _invert_kernel,
        in_specs=[smem, smem, smem],
        out_specs=smem,
        out_shape=jax.ShapeDtypeStruct((n_rows,), jnp.int32),
        name="invert",
    )(dest, pad_lo, pad_hi)


def _experts_kernel(be_ref, run_ref, next_ref, nused_ref, tok_ref,
                    hn_hbm, wg_hbm, wu_hbm, wd_hbm, ys_hbm,
                    hn_vmem, xg_even, xg_odd, y_even, y_odd, wg_f, wu_f, wd_f, wg_bf, wu_bf, wd_bf,
                    hn_sem, w_sem, y_sem):
    n_used = nused_ref[0]
    n_blocks = be_ref.shape[0]
    block_rows = ROW_BLOCK * ROW_TILE

    def weight_copies(e, slot):
        return [pltpu.make_async_copy(src.at[e], dst.at[slot], w_sem.at[slot])
                for src, dst in ((wg_hbm, wg_f), (wu_hbm, wu_f), (wd_hbm, wd_f))]

    def y_copy(buf, parity, i):
        dst = ys_hbm.at[pl.ds(pl.multiple_of(i * block_rows, block_rows), block_rows), :]
        return pltpu.make_async_copy(buf, dst, y_sem.at[parity])

    def gather(i, dst):
        base = jnp.minimum(i, n_blocks - 1) * ROW_BLOCK
        for r in range(ROW_BLOCK):
            src = pl.ds(pl.multiple_of(tok_ref[base + r] * X_TILE, X_TILE), X_TILE)
            dst[r * X_TILE:(r + 1) * X_TILE, :] = hn_vmem[src, :]

    def block(i, parity, cur, nxt, ybuf):
        @pl.when(jnp.logical_or(i == 0, be_ref[i] != be_ref[jnp.maximum(i - 1, 0)]))
        def _():
            slot = jnp.bitwise_and(run_ref[i], 1)
            for c in weight_copies(be_ref[i], slot):
                c.wait()
            wg_bf[...] = wg_f[slot].astype(BF16)
            wu_bf[...] = wu_f[slot].astype(BF16)
            wd_bf[...] = wd_f[slot].astype(BF16)

            @pl.when(next_ref[i] < n_used)
            def _():
                for c in weight_copies(be_ref[jnp.minimum(next_ref[i], n_blocks - 1)], 1 - slot):
                    c.start()

        @pl.when(i >= 2)
        def _():
            y_copy(ybuf, parity, i).wait()

        kh = D_MODEL // 2
        gather(i + 1, nxt)
        x_lo, x_hi = _unpack_bf16_pairs(_load_row_tiles(cur, ROW_BLOCK))
        gate = _dot(x_lo, wg_bf[:kh, :]) + _dot(x_hi, wg_bf[kh:, :])
        up = _dot(x_lo, wu_bf[:kh, :]) + _dot(x_hi, wu_bf[kh:, :])
        hid = (_silu(gate) * up).astype(BF16)
        _store_row_tiles(ybuf, _dot(hid, wd_bf[...]))
        y_copy(ybuf, parity, i).start()

    hn_load = pltpu.make_async_copy(hn_hbm, hn_vmem, hn_sem)
    hn_load.start()
    for c in weight_copies(be_ref[0], 0):
        c.start()
    hn_load.wait()
    gather(0, xg_even)

    def pair(p, carry):
        block(2 * p, 0, xg_even, xg_odd, y_even)

        @pl.when(2 * p + 1 < n_used)
        def _():
            block(2 * p + 1, 1, xg_odd, xg_even, y_odd)
        return carry

    lax.fori_loop(0, lax.shift_right_logical(n_used + 1, 1), pair, 0)

    y_copy(y_even, 0, 0).wait()

    @pl.when(n_used >= 2)
    def _():
        y_copy(y_odd, 1, 0).wait()

    y_even[...] = jnp.zeros_like(y_even)

    def zero_start(i, carry):
        y_copy(y_even, 0, i).start()
        return carry

    def zero_wait(i, carry):
        y_copy(y_even, 0, i).wait()
        return carry

    lax.fori_loop(n_used, n_blocks, zero_start, 0)
    lax.fori_loop(n_used, n_blocks, zero_wait, 0)


def _experts(hn, tok_of_row, block_expert, block_run, block_next, n_used, w_gate, w_up, w_down):
    n_rows = tok_of_row.shape[0]
    hbm = pl.BlockSpec(memory_space=pl.ANY)
    grid_spec = pltpu.PrefetchScalarGridSpec(
        num_scalar_prefetch=5,
        grid=(1,),
        in_specs=[hbm, hbm, hbm, hbm],
        out_specs=hbm,
        scratch_shapes=[pltpu.VMEM(hn.shape, jnp.uint32),
                        pltpu.VMEM((ROW_BLOCK * X_TILE, LANES), jnp.uint32),
                        pltpu.VMEM((ROW_BLOCK * X_TILE, LANES), jnp.uint32),
                        pltpu.VMEM((ROW_BLOCK * ROW_TILE, LANES), F32),
                        pltpu.VMEM((ROW_BLOCK * ROW_TILE, LANES), F32),
                        pltpu.VMEM((2, D_MODEL, MOE_FF), F32), pltpu.VMEM((2, D_MODEL, MOE_FF), F32),
                        pltpu.VMEM((2, MOE_FF, D_MODEL), F32),
                        pltpu.VMEM((D_MODEL, MOE_FF), BF16), pltpu.VMEM((D_MODEL, MOE_FF), BF16),
                        pltpu.VMEM((MOE_FF, D_MODEL), BF16),
                        pltpu.SemaphoreType.DMA, pltpu.SemaphoreType.DMA((2,)), pltpu.SemaphoreType.DMA((2,))],
    )
    return pl.pallas_call(
        _experts_kernel,
        grid_spec=grid_spec,
        out_shape=jax.ShapeDtypeStruct((n_rows * ROW_TILE, LANES), F32),
        compiler_params=_params(("arbitrary",)),
        name="experts",
    )(block_expert, block_run, block_next, n_used, tok_of_row, hn, w_gate, w_up, w_down)


COMB_TM = 256


def _combine_kernel(dest_ref, dest_next_ref, x1_ref, gate_ref, ys_ref, out_ref, ybuf, sems):
    g = pl.program_id(0)

    def row_copy(d, slot, k, r):
        src = ys_ref.at[pl.ds(pl.multiple_of(d * ROW_TILE, ROW_TILE), ROW_TILE), :]
        dst = ybuf.at[slot, k, pl.ds(pl.multiple_of(r * ROW_TILE, ROW_TILE), ROW_TILE), :]
        return pltpu.make_async_copy(src, dst, sems.at[slot])

    def gather(dref, base, slot):
        def issue(r, c):
            row_copy(dref[0, 0, 0, base + r], slot, 0, r).start(priority=0)
            row_copy(dref[1, 0, 0, base + r], slot, 1, r).start(priority=1)
            return c
        lax.fori_loop(0, COMB_TM, issue, 0, unroll=DMA_UNROLL)

    def finish(slot):
        for k in range(2):
            pltpu.make_async_copy(ys_ref.at[pl.ds(0, COMB_TM * ROW_TILE), :], ybuf.at[slot, k],
                                  sems.at[slot]).wait()
        rows = slice(slot * COMB_TM, (slot + 1) * COMB_TM)
        gate = gate_ref[:, rows].T
        w0 = gate[:, 0:1]
        w1 = gate[:, 1:2]
        y0 = _load_row_tiles(ybuf.at[slot, 0], COMB_TM)
        y1 = _load_row_tiles(ybuf.at[slot, 1], COMB_TM)
        out_ref[rows, :] = x1_ref[rows, :] + (y0 * w0 + y1 * w1)

    @pl.when(g == 0)
    def _():
        gather(dest_ref, 0, 0)

    gather(dest_ref, COMB_TM, 1)
    finish(0)

    @pl.when(g + 1 < pl.num_programs(0))
    def _():
        gather(dest_next_ref, 0, 0)

    finish(1)


def _combine(x1, gates, ys, dest):
    t = x1.shape[0]
    ng = t // (2 * COMB_TM)
    dest3 = dest.reshape(2, ng, 1, 2 * COMB_TM)
    smem = lambda f: pl.BlockSpec((2, 1, 1, 2 * COMB_TM), f, memory_space=pltpu.SMEM)
    return pl.pallas_call(
        _combine_kernel,
        grid=(ng,),
        in_specs=[smem(lambda i: (0, i, 0, 0)),
                  smem(lambda i: (0, jnp.minimum(i + 1, ng - 1), 0, 0)),
                  pl.BlockSpec((2 * COMB_TM, D_MODEL), lambda i: (i, 0)),
                  pl.BlockSpec((8, 2 * COMB_TM), lambda i: (0, i)),
                  pl.BlockSpec(memory_space=pl.ANY)],
        out_specs=pl.BlockSpec((2 * COMB_TM, D_MODEL), lambda i: (i, 0)),
        out_shape=jax.ShapeDtypeStruct((t, D_MODEL), F32),
        scratch_shapes=[pltpu.VMEM((2, 2, COMB_TM * ROW_TILE, LANES), F32),
                        pltpu.SemaphoreType.DMA((2,))],
        compiler_params=_params(("arbitrary",)),
        name="combine",
    )(dest3, dest3, x1, gates, ys)


def _layer(x, mem, attn_norm_g, mem_norm_g, w_in, w_gla_gk, b_gla_gk, gla_out_norm_g,
           moba_q_norm_g, moba_k_norm_g, w_mem_kv, mem_q_norm_g, mem_k_norm_g, w_out,
           ffn_norm_g, w_router_group, b_router_group, w_router_expert, b_router_expert,
           w_gate, w_up, w_down):
    b, s, d = x.shape
    t = b * s
    x2 = x.reshape(t, d)
    row = lambda v: v.reshape(1, -1).astype(F32)
    tile_heads = lambda v: jnp.tile(v.astype(F32), HEADS).reshape(1, QK_W)
    hid = np.arange(QK_W) // DH
    seg = jnp.asarray((hid[:, None] == hid[None, :]) / DH, BF16)

    w_in_p = jnp.concatenate([w_in[:, :1536], w_in[:, 1552:], w_in[:, 1536:1552]], axis=1).astype(BF16)
    wr = jnp.concatenate([w_router_group,
                          jnp.transpose(w_router_expert, (1, 0, 2)).reshape(d, N_EXPERTS),
                          jnp.zeros((d, LANES - N_GROUPS - N_EXPERTS), F32)], axis=1).astype(BF16)
    br = jnp.concatenate([b_router_group, b_router_expert.reshape(N_EXPERTS),
                          jnp.zeros((LANES - N_GROUPS - N_EXPERTS,), F32)]).reshape(1, LANES)

    memk, memvt = _mem_kv(mem, row(mem_norm_g), w_mem_kv.astype(BF16), tile_heads(mem_k_norm_g), seg)
    qk, gv, gr, glr, mq, mk, mvt, kmean, cq = _in_proj(
        x2, row(attn_norm_g), w_in_p, seg, tile_heads(moba_q_norm_g), tile_heads(moba_k_norm_g),
        tile_heads(mem_q_norm_g))
    o_gla = _gla(qk, gv, gr, glr, w_gla_gk.astype(BF16), row(b_gla_gk), row(gla_out_norm_g), b, s)
    o_moba, o_mem = _moba_mem(mq, mk, mvt, kmean, cq, memk, memvt, b, s)
    x1, hn, e_ids, gates, counts = _out_router(
        x2, o_gla.reshape(t, GLA_V_W), o_moba.reshape(t, QK_W), o_mem.reshape(t, QK_W),
        w_out.astype(BF16), row(ffn_norm_g), wr, br)

    counts = counts[N_GROUPS:N_GROUPS + N_EXPERTS, 0]
    padded = (counts + ROW_BLOCK - 1) // ROW_BLOCK * ROW_BLOCK
    pends = jnp.cumsum(padded)
    pstarts = pends - padded
    onehot = e_ids[:2, :, None] == jnp.arange(N_EXPERTS, dtype=jnp.int32)
    dest = (jnp.sum(jnp.where(onehot, pstarts, 0), axis=-1) + e_ids[2:4]).astype(jnp.int32)
    n_rows = (t * 2 + N_EXPERTS * (ROW_BLOCK - 1) + ROW_BLOCK - 1) // ROW_BLOCK * ROW_BLOCK
    nb = n_rows // ROW_BLOCK
    block_start = jnp.arange(nb, dtype=jnp.int32) * ROW_BLOCK
    block_expert = jnp.minimum(jnp.sum(block_start[:, None] >= pends[None, :], axis=1),
                               N_EXPERTS - 1).astype(jnp.int32)
    n_used = (pends[-1] // ROW_BLOCK).astype(jnp.int32).reshape(1)
    ended = block_start[:, None] >= pends[None, :]
    block_run = jnp.sum(ended & (counts > 0)[None, :], axis=1).astype(jnp.int32)
    block_next = (jnp.min(jnp.where(ended, n_rows, pends[None, :]), axis=1) // ROW_BLOCK).astype(jnp.int32)

    pad_lo = jnp.concatenate([pstarts + counts, pends[-1:]]).astype(jnp.int32)
    pad_hi = jnp.concatenate([pends, jnp.full((1,), n_rows)]).astype(jnp.int32)
    tok_of_row = _invert(dest.reshape(-1), pad_lo, pad_hi, n_rows)
    ys = _experts(hn, tok_of_row, block_expert, block_run, block_next, n_used, w_gate, w_up, w_down)
    out = _combine(x1, gates, ys, dest)
    return out.reshape(b, s, d)


def kernel(x, mem, attn_norm_g, mem_norm_g, w_in, w_gla_gk, b_gla_gk, gla_out_norm_g, moba_q_norm_g, moba_k_norm_g, w_mem_kv, mem_q_norm_g, mem_k_norm_g, w_out, ffn_norm_g, w_router_group, b_router_group, w_router_expert, b_router_expert, w_gate, w_up, w_down):
    depth = w_in.shape[0]
    for l in range(depth):
        x = _layer(x, mem, attn_norm_g[l], mem_norm_g[l], w_in[l], w_gla_gk[l], b_gla_gk[l],
                   gla_out_norm_g[l], moba_q_norm_g[l], moba_k_norm_g[l], w_mem_kv[l],
                   mem_q_norm_g[l], mem_k_norm_g[l], w_out[l], ffn_norm_g[l], w_router_group[l],
                   b_router_group[l], w_router_expert[l], b_router_expert[l],
                   w_gate[l], w_up[l], w_down[l])
    return x
```

```python
import functools

import jax
import jax.numpy as jnp
import numpy as np
from jax import lax
from jax.experimental import pallas as pl
from jax.experimental.pallas import tpu as pltpu

F32 = jnp.float32
BF16 = jnp.bfloat16
EPS = 1e-6
NEG_INF = float("-inf")

D_MODEL = 1024
N_MEM = 256
HEADS = 4
DH = 64
GLA_DV = 128
GLA_RANK = 16
GLA_GATE_NORMALIZER = 16.0
GLA_CHUNK = 64
MOBA_BLOCK = 256
MOBA_TOPK = 3
QK_W = HEADS * DH
GLA_V_W = HEADS * GLA_DV
N_GROUPS = 4
EXPERTS_PER_GROUP = 8
N_EXPERTS = N_GROUPS * EXPERTS_PER_GROUP
MOE_FF = 512
LANES = 128
ROW_BLOCK = 256

VMEM_LIMIT = 56 * 1024 * 1024


def _params(sem):
    return pltpu.CompilerParams(dimension_semantics=sem, vmem_limit_bytes=VMEM_LIMIT)


def _nt(a, b):
    return lax.dot_general(a, b, (((1,), (1,)), ((), ())), preferred_element_type=F32)


def _tn(a, b):
    return lax.dot_general(a, b, (((0,), (0,)), ((), ())), preferred_element_type=F32)


def _dot(a, b):
    return jnp.dot(a, b, preferred_element_type=F32)


def _rms_rows(x, g):
    ms = jnp.mean(x * x, axis=-1, keepdims=True)
    return x * lax.rsqrt(ms + EPS) * g


def _split3(x):
    h1 = x.astype(BF16)
    r1 = x - h1.astype(F32)
    h2 = r1.astype(BF16)
    h3 = (r1 - h2.astype(F32)).astype(BF16)
    return h1, h2, h3


def _head_mean_sq(x, seg):
    sq = x * x
    hi = sq.astype(BF16)
    lo = (sq - hi.astype(F32)).astype(BF16)
    return _dot(hi, seg) + _dot(lo, seg)


def _silu(x):
    return x * (1.0 / (1.0 + jnp.exp(-x)))


VT_ROWS = DH + 16
VT_ALL = HEADS * VT_ROWS


ALIBI_SLOPES = tuple(2.0 ** (-8.0 * (h + 1) / HEADS) for h in range(HEADS))


def _vt_with_ones(v, key_slopes=None):
    n = v.shape[0]
    vt = v.T
    ones = jnp.ones((VT_ROWS - DH, n), F32)
    pos = lax.broadcasted_iota(jnp.int32, (1, n), 1).astype(F32)
    parts = []
    for h in range(HEADS):
        scale = 1.0 if key_slopes is None else jnp.exp(key_slopes[h] * pos)
        parts += [vt[h * DH:(h + 1) * DH] * scale, ones * scale]
    return jnp.concatenate(parts, axis=0).astype(BF16)


ROW_TILE = D_MODEL // LANES
X_TILE = ROW_TILE // 2


def _store_row_tiles(ref, x):
    n, tiles = x.shape[0], x.shape[1] // LANES
    for c in range(tiles):
        ref[pl.ds(c, n, stride=tiles), :] = x[:, c * LANES:(c + 1) * LANES]


def _load_row_tiles(ref, n):
    tiles = ref.shape[0] // n
    return jnp.concatenate([ref[pl.ds(c, n, stride=tiles), :] for c in range(tiles)], axis=1)


def _pack_bf16_pairs(x):
    w = x.shape[1] // 2
    bits = lambda t: lax.bitcast_convert_type(t.astype(BF16).astype(F32), jnp.uint32)
    return (bits(x[:, w:]) & jnp.uint32(0xFFFF0000)) | (bits(x[:, :w]) >> 16)


def _unpack_bf16_pairs(words):
    lo = lax.bitcast_convert_type(words << 16, F32).astype(BF16)
    hi = lax.bitcast_convert_type(words & jnp.uint32(0xFFFF0000), F32).astype(BF16)
    return lo, hi


def _mem_kv_kernel(mem_ref, g_ref, w_ref, gk_ref, seg_ref, k_ref, vt_ref):
    h = _rms_rows(mem_ref[0], g_ref[...]).astype(BF16)
    kv = _dot(h, w_ref[...])
    k = kv[:, :QK_W]
    kn = k * lax.rsqrt(_head_mean_sq(k, seg_ref[...]) + EPS) * gk_ref[...]
    k_ref[0] = kn.astype(BF16)
    vt_ref[0] = _vt_with_ones(kv[:, QK_W:])


def _mem_kv(mem, g, w_bf, gk_t, seg):
    b = mem.shape[0]
    return pl.pallas_call(
        _mem_kv_kernel,
        grid=(b,),
        in_specs=[
            pl.BlockSpec((1, N_MEM, D_MODEL), lambda i: (i, 0, 0)),
            pl.BlockSpec((1, D_MODEL), lambda i: (0, 0)),
            pl.BlockSpec((D_MODEL, 2 * QK_W), lambda i: (0, 0)),
            pl.BlockSpec((1, QK_W), lambda i: (0, 0)),
            pl.BlockSpec((QK_W, QK_W), lambda i: (0, 0)),
        ],
        out_specs=[
            pl.BlockSpec((1, N_MEM, QK_W), lambda i: (i, 0, 0)),
            pl.BlockSpec((1, VT_ALL, N_MEM), lambda i: (i, 0, 0)),
        ],
        out_shape=[
            jax.ShapeDtypeStruct((b, N_MEM, QK_W), BF16),
            jax.ShapeDtypeStruct((b, VT_ALL, N_MEM), BF16),
        ],
        compiler_params=_params(("arbitrary",)),
        name="mem_kv",
    )(mem, g, w_bf, gk_t, seg)


IN_TM = 1024
_C_QK, _C_V, _C_R, _C_MQ, _C_MK, _C_MV, _C_CQ, _C_LR = 0, 512, 1024, 1536, 1792, 2048, 2304, 2560
D_IN = 2576


def _in_proj_kernel(x_ref, g_ref, w_ref, seg_ref, gq_ref, gk_ref, gc_ref,
                    qk_ref, v_ref, r_ref, lr_ref, mq_ref, mk_ref, mvt_ref, kmean_ref, cq_ref):
    h = _rms_rows(x_ref[...], g_ref[...]).astype(BF16)
    p = _dot(h, w_ref[...])
    seg = seg_ref[...]
    qk_ref[...] = p[:, _C_QK:_C_V]
    v_ref[...] = p[:, _C_V:_C_R].astype(BF16)
    r_ref[...] = p[:, _C_R:_C_MQ]
    lr_ref[...] = p[:, _C_LR:D_IN]

    def head_norm(t, gain):
        return t * lax.rsqrt(_head_mean_sq(t, seg) + EPS) * gain

    scale = DH ** -0.5
    mq_ref[...] = (head_norm(p[:, _C_MQ:_C_MK], gq_ref[...]) * scale).astype(BF16)
    cq_ref[...] = (head_norm(p[:, _C_CQ:_C_LR], gc_ref[...]) * scale).astype(BF16)
    kn = head_norm(p[:, _C_MK:_C_MV], gk_ref[...])
    mk_ref[...] = kn.astype(BF16)
    mv = p[:, _C_MV:_C_CQ]
    for j in range(IN_TM // MOBA_BLOCK):
        rows = slice(j * MOBA_BLOCK, (j + 1) * MOBA_BLOCK)
        kmean_ref[0, j:j + 1, :] = jnp.mean(kn[rows], axis=0, keepdims=True)
        mvt_ref[j] = _vt_with_ones(mv[rows], ALIBI_SLOPES)


def _in_proj(x2, g, w_bf, seg, gq_t, gk_t, gc_t):
    t = x2.shape[0]
    nt = t // IN_TM
    nb = IN_TM // MOBA_BLOCK
    row = lambda w: pl.BlockSpec((IN_TM, w), lambda i: (i, 0))
    const = lambda a, b: pl.BlockSpec((a, b), lambda i: (0, 0))
    return pl.pallas_call(
        _in_proj_kernel,
        grid=(nt,),
        in_specs=[row(D_MODEL), const(1, D_MODEL), const(D_MODEL, D_IN), const(QK_W, QK_W),
                  const(1, QK_W), const(1, QK_W), const(1, QK_W)],
        out_specs=[row(2 * QK_W), row(GLA_V_W), row(GLA_V_W), row(GLA_RANK), row(QK_W), row(QK_W),
                   pl.BlockSpec((nb, VT_ALL, MOBA_BLOCK), lambda i: (i, 0, 0)),
                   pl.BlockSpec((1, nb, QK_W), lambda i: (i, 0, 0)),
                   row(QK_W)],
        out_shape=[
            jax.ShapeDtypeStruct((t, 2 * QK_W), F32),
            jax.ShapeDtypeStruct((t, GLA_V_W), BF16),
            jax.ShapeDtypeStruct((t, GLA_V_W), F32),
            jax.ShapeDtypeStruct((t, GLA_RANK), F32),
            jax.ShapeDtypeStruct((t, QK_W), BF16),
            jax.ShapeDtypeStruct((t, QK_W), BF16),
            jax.ShapeDtypeStruct((t // MOBA_BLOCK, VT_ALL, MOBA_BLOCK), BF16),
            jax.ShapeDtypeStruct((nt, nb, QK_W), F32),
            jax.ShapeDtypeStruct((t, QK_W), BF16),
        ],
        compiler_params=_params(("arbitrary",)),
        name="in_proj",
    )(x2, g, w_bf, seg, gq_t, gk_t, gc_t)


GLA_TC = 256


def _gla_kernel(qk_ref, v_ref, r_ref, lr_ref, wgk_ref, bgk_ref, gn_ref, tri_ref, ones_ref,
                o_ref, st_ref):
    @pl.when(pl.program_id(1) == 0)
    def _():
        st_ref[...] = jnp.zeros_like(st_ref)

    qk = qk_ref[0]
    q = qk[:, :QK_W]
    k = qk[:, QK_W:]
    gk = _dot(lr_ref[0].astype(BF16), wgk_ref[...]) + bgk_ref[...]
    g = -(jnp.maximum(-gk, 0.0) + jnp.log1p(jnp.exp(-jnp.abs(gk)))) / GLA_GATE_NORMALIZER
    g1, g2, g3 = _split3(g)
    tri = tri_ref[...]
    ones = ones_ref[...]
    cum = _dot(tri, g1) + _dot(tri, g2) + _dot(tri, g3)
    tot = _dot(ones, g1) + _dot(ones, g2) + _dot(ones, g3)
    q_dec = (q * (DH ** -0.5) * jnp.exp(cum)).astype(BF16)
    k_inv = (k * jnp.exp(-cum)).astype(BF16)
    k_end = (k * jnp.exp(tot - cum)).astype(BF16)
    decay = jnp.exp(tot)

    lane_head = lax.broadcasted_iota(jnp.int32, (GLA_TC, QK_W), 1) // DH
    row_t = lax.broadcasted_iota(jnp.int32, (GLA_TC, GLA_TC), 0)
    col_t = lax.broadcasted_iota(jnp.int32, (GLA_TC, GLA_TC), 1)
    causal = (row_t >= col_t) & (row_t // GLA_CHUNK == col_t // GLA_CHUNK)
    same_head = (lax.broadcasted_iota(jnp.int32, (GLA_V_W, QK_W), 0) // GLA_DV
                 == lax.broadcasted_iota(jnp.int32, (GLA_V_W, QK_W), 1) // DH)
    gain = gn_ref[...]

    v = v_ref[0]
    chunks = [slice(c * GLA_CHUNK, (c + 1) * GLA_CHUNK) for c in range(GLA_TC // GLA_CHUNK)]
    d_st = [jnp.where(same_head, _tn(v[rows], k_end[rows]), 0.0) for rows in chunks]
    states = [st_ref[...]]
    for c, rows in enumerate(chunks):
        states.append(states[c] * decay[rows.start:rows.start + 1, :] + d_st[c])
    st_ref[...] = states[-1]
    o_inter = jnp.concatenate([_nt(q_dec[rows], states[c].astype(BF16)) for c, rows in enumerate(chunks)],
                              axis=0)
    q_stack = jnp.concatenate([jnp.where(lane_head == h, q_dec, jnp.zeros_like(q_dec)) for h in range(HEADS)],
                              axis=0)
    a_all = _nt(q_stack, k_inv)
    outs = []
    for h in range(HEADS):
        a = jnp.where(causal, a_all[h * GLA_TC:(h + 1) * GLA_TC], 0.0).astype(BF16)
        vs = slice(h * GLA_DV, (h + 1) * GLA_DV)
        oh = _dot(a, v[:, vs]) + o_inter[:, vs]
        outs.append(_rms_rows(oh, gain) * _silu(r_ref[0, :, vs]))
    o_ref[0] = jnp.concatenate(outs, axis=-1).astype(BF16)


def _gla(qk, v, r, lr, wgk_bf, bgk, gn, b, s):
    ns = s // GLA_TC
    idx = np.arange(GLA_TC)
    same_chunk = (idx[:, None] // GLA_CHUNK) == (idx[None, :] // GLA_CHUNK)
    tri = jnp.asarray(same_chunk & (idx[:, None] >= idx[None, :]), BF16)
    ones = jnp.asarray(same_chunk, BF16)
    seq = lambda w: pl.BlockSpec((1, GLA_TC, w), lambda i, j: (i, j, 0))
    const = lambda a, c: pl.BlockSpec((a, c), lambda i, j: (0, 0))
    return pl.pallas_call(
        _gla_kernel,
        grid=(b, ns),
        in_specs=[seq(2 * QK_W), seq(GLA_V_W), seq(GLA_V_W), seq(GLA_RANK),
                  const(GLA_RANK, QK_W), const(1, QK_W), const(1, GLA_DV),
                  const(GLA_TC, GLA_TC), const(GLA_TC, GLA_TC)],
        out_specs=seq(GLA_V_W),
        out_shape=jax.ShapeDtypeStruct((b, s, GLA_V_W), BF16),
        scratch_shapes=[pltpu.VMEM((GLA_V_W, QK_W), F32)],
        compiler_params=_params(("arbitrary", "arbitrary")),
        name="gla",
    )(qk.reshape(b, s, 2 * QK_W), v.reshape(b, s, GLA_V_W), r.reshape(b, s, GLA_V_W),
      lr.reshape(b, s, GLA_RANK), wgk_bf, bgk, gn, tri, ones)


TQ = MOBA_BLOCK
KV_UNROLL = 8


def _moba_mem_kernel(q_ref, k_ref, vt_ref, kmean_ref, cq_ref, mk_ref, mvt_ref,
                     o_moba_ref, o_mem_ref,
                     qcat_ref, selb_ref, s_ref, *, n_blocks):
    i = pl.program_id(1)
    lane_head = lax.broadcasted_iota(jnp.int32, (TQ, QK_W), 1) // DH
    dist0 = (lax.broadcasted_iota(jnp.int32, (MOBA_BLOCK, TQ), 1)
             - lax.broadcasted_iota(jnp.int32, (MOBA_BLOCK, TQ), 0)).astype(F32)
    slopes = ALIBI_SLOPES
    heads = [slice(h * TQ, (h + 1) * TQ) for h in range(HEADS)]
    vrows = [slice(h * VT_ROWS, (h + 1) * VT_ROWS) for h in range(HEADS)]

    def stack_heads(x):
        return jnp.concatenate([jnp.where(lane_head == h, x, jnp.zeros_like(x)) for h in range(HEADS)], axis=0)

    def finish(accl):
        return accl[:DH] * (1.0 / accl[DH:DH + 1])

    s_all = _nt(mk_ref[0], stack_heads(cq_ref[0]))
    outs = []
    for h in range(HEADS):
        s = s_all[:, heads[h]]
        p = jnp.exp(s - jnp.max(s, axis=0, keepdims=True)).astype(BF16)
        outs.append(finish(_dot(mvt_ref[0, vrows[h], :], p)))
    o_mem_ref[0] = jnp.concatenate(outs, axis=0).T.astype(BF16)

    qcat = stack_heads(q_ref[0])
    qcat_ref[...] = qcat
    k_own = k_ref[0, pl.ds(pl.multiple_of(i * MOBA_BLOCK, MOBA_BLOCK), MOBA_BLOCK), :]
    gate_all = _nt(kmean_ref[0].astype(BF16), qcat)
    s_all = _nt(k_own, qcat)
    blk = lax.broadcasted_iota(jnp.int32, (n_blocks, TQ), 0)
    blk_f = blk.astype(F32)

    tq = lax.broadcasted_iota(jnp.int32, (1, TQ), 1).astype(F32)
    init = []
    for h in range(HEADS):
        gate = jnp.where(blk < i, gate_all[:, heads[h]], NEG_INF)
        chosen = jnp.zeros((n_blocks, TQ), jnp.bool_)
        for r in range(MOBA_TOPK):
            mx = jnp.max(gate, axis=0, keepdims=True)
            first = jnp.min(jnp.where(gate == mx, blk_f, float(n_blocks)), axis=0, keepdims=True)
            hit = blk_f == first
            chosen = chosen | (hit & (mx > NEG_INF))
            gate = jnp.where(hit, NEG_INF, gate)
        selb_ref[h] = jnp.where(chosen, 0.0, NEG_INF)
        aq = -slopes[h] * tq
        s = jnp.where(dist0 >= 0.0, s_all[:, heads[h]], NEG_INF)
        m0 = jnp.max(s, axis=0, keepdims=True) + aq
        p = jnp.exp(s - (m0 - aq)).astype(BF16)
        init += [m0, _dot(vt_ref[i, vrows[h], :], p)]

    def stage_scores(j, slot):
        kj = k_ref[0, pl.ds(pl.multiple_of(j * MOBA_BLOCK, MOBA_BLOCK), MOBA_BLOCK), :]
        s_ref[slot] = _nt(kj, qcat_ref[...])

    stage_scores(0, 0)

    def past_blocks(g, carry, unroll, base):
        carry = list(carry)
        for u in range(unroll):
            j = base + g * unroll + u
            stage_scores(jnp.minimum(j + 1, n_blocks - 1), (u + 1) % 2)
            off = jnp.full((1, TQ), (i - j) * MOBA_BLOCK, jnp.int32).astype(F32) + tq
            for h in range(HEADS):
                m, accl = carry[2 * h:2 * h + 2]
                s1 = s_ref[u % 2, :, heads[h]]
                rb = selb_ref[h, pl.ds(j, 1), :] - slopes[h] * off
                m_new = jnp.maximum(m, jnp.max(s1, axis=0, keepdims=True) + rb)
                p = jnp.exp(s1 - (m_new - rb)).astype(BF16)
                carry[2 * h:2 * h + 2] = [m_new, jnp.exp(m - m_new) * accl + _dot(vt_ref[j, vrows[h], :], p)]
        return tuple(carry)

    n_main = i // KV_UNROLL
    rest = i - n_main * KV_UNROLL
    carry = lax.fori_loop(0, n_main, functools.partial(past_blocks, unroll=KV_UNROLL, base=0), tuple(init))
    final = lax.fori_loop(0, (rest + 1) // 2,
                          functools.partial(past_blocks, unroll=2, base=n_main * KV_UNROLL), carry)
    o_t = jnp.concatenate([finish(final[2 * h + 1]) for h in range(HEADS)], axis=0)
    o_moba_ref[0] = o_t.T.astype(BF16)


def _moba_mem(mq, mk, mvt, kmean, cq, memk, memvt, b, s):
    nq = s // TQ
    n_blocks = s // MOBA_BLOCK
    qspec = pl.BlockSpec((1, TQ, QK_W), lambda i, j: (i, j, 0))
    return pl.pallas_call(
        functools.partial(_moba_mem_kernel, n_blocks=n_blocks),
        grid=(b, nq),
        in_specs=[
            qspec,
            pl.BlockSpec((1, s, QK_W), lambda i, j: (i, 0, 0)),
            pl.BlockSpec((n_blocks, VT_ALL, MOBA_BLOCK), lambda i, j: (i, 0, 0)),
            pl.BlockSpec((1, n_blocks, QK_W), lambda i, j: (i, 0, 0)),
            qspec,
            pl.BlockSpec((1, N_MEM, QK_W), lambda i, j: (i, 0, 0)),
            pl.BlockSpec((1, VT_ALL, N_MEM), lambda i, j: (i, 0, 0)),
        ],
        out_specs=[qspec, qspec],
        out_shape=[jax.ShapeDtypeStruct((b, s, QK_W), BF16), jax.ShapeDtypeStruct((b, s, QK_W), BF16)],
        scratch_shapes=[
            pltpu.VMEM((HEADS * TQ, QK_W), BF16),
            pltpu.VMEM((HEADS, n_blocks, TQ), F32),
            pltpu.VMEM((2, MOBA_BLOCK, HEADS * TQ), F32),
        ],
        compiler_params=_params(("arbitrary", "arbitrary")),
        name="moba_mem",
    )(mq.reshape(b, s, QK_W), mk.reshape(b, s, QK_W), mvt, kmean.reshape(b, n_blocks, QK_W),
      cq.reshape(b, s, QK_W), memk, memvt)


OUT_TM = 1024
ROUTE_ROWS = 40


def _out_router_kernel(x_ref, og_ref, om_ref, oc_ref, w_ref, g_ref, wr_ref, br_ref, tri_ref,
                       x1_ref, h_ref, e_ref, gate_ref, count_ref, carry_ref):
    w = w_ref[...]
    x1 = (x_ref[...] + _dot(og_ref[...], w[:GLA_V_W]) + _dot(om_ref[...], w[GLA_V_W:GLA_V_W + QK_W])
          + _dot(oc_ref[...], w[GLA_V_W + QK_W:]))
    x1_ref[...] = x1
    hn = _rms_rows(x1, g_ref[...])
    _store_row_tiles(h_ref, _pack_bf16_pairs(hn))
    logits = _dot(hn.astype(BF16), wr_ref[...]) + br_ref[...]
    lt = logits.T[:ROUTE_ROWS]
    row = lax.broadcasted_iota(jnp.int32, lt.shape, 0).astype(F32)
    lg = jnp.where(row < N_GROUPS, lt, NEG_INF)
    mg = jnp.max(lg, axis=0, keepdims=True)
    g_sel = jnp.min(jnp.where(lg == mg, row, float(LANES)), axis=0, keepdims=True)
    p_group = 1.0 / jnp.sum(jnp.exp(lg - mg), axis=0, keepdims=True)
    lo = N_GROUPS + g_sel * EXPERTS_PER_GROUP
    le = jnp.where((row >= lo) & (row < lo + EXPERTS_PER_GROUP), lt, NEG_INF)
    m0 = jnp.max(le, axis=0, keepdims=True)
    i0 = jnp.min(jnp.where(le == m0, row, float(LANES)), axis=0, keepdims=True)
    le1 = jnp.where(row == i0, NEG_INF, le)
    m1 = jnp.max(le1, axis=0, keepdims=True)
    i1 = jnp.min(jnp.where(le1 == m1, row, float(LANES)), axis=0, keepdims=True)
    z = jnp.exp(m1 - m0)
    w0 = p_group / (1.0 + z)
    w1 = p_group * z / (1.0 + z)
    pad = jnp.zeros((8 - 2, lt.shape[1]), F32)
    gate_ref[...] = jnp.concatenate([w0, w1, pad], axis=0)

    @pl.when(pl.program_id(0) == 0)
    def _():
        carry_ref[...] = jnp.zeros_like(carry_ref)

    oh0 = (row == i0).astype(F32)
    oh1 = (row == i1).astype(F32)
    both = oh0 + oh1
    before = _dot(both.astype(BF16), tri_ref[...]) + carry_ref[:, 0:1]
    r0 = jnp.sum(oh0 * before, axis=0, keepdims=True)
    r1 = jnp.sum(oh1 * before, axis=0, keepdims=True)
    carry_ref[...] = carry_ref[...] + jnp.sum(both, axis=1, keepdims=True)
    count_ref[...] = carry_ref[...].astype(jnp.int32)
    e_ref[...] = jnp.concatenate([i0 - N_GROUPS, i1 - N_GROUPS, r0, r1, pad[:4]], axis=0).astype(jnp.int32)


def _out_router(x2, og, om, oc, w_bf, g, wr_bf, br):
    t = x2.shape[0]
    row = lambda w: pl.BlockSpec((OUT_TM, w), lambda i: (i, 0))
    const = lambda a, b: pl.BlockSpec((a, b), lambda i: (0, 0))
    idx = np.arange(OUT_TM)
    tri = jnp.asarray(idx[:, None] < idx[None, :], BF16)
    col = lambda r: pl.BlockSpec((r, OUT_TM), lambda i: (0, i))
    return pl.pallas_call(
        _out_router_kernel,
        grid=(t // OUT_TM,),
        in_specs=[row(D_MODEL), row(GLA_V_W), row(QK_W), row(QK_W), const(D_MODEL, D_MODEL),
                  const(1, D_MODEL), const(D_MODEL, LANES), const(1, LANES), const(OUT_TM, OUT_TM)],
        out_specs=[row(D_MODEL), pl.BlockSpec((OUT_TM * X_TILE, LANES), lambda i: (i, 0)),
                   col(8), col(8), const(ROUTE_ROWS, LANES)],
        out_shape=[
            jax.ShapeDtypeStruct((t, D_MODEL), F32),
            jax.ShapeDtypeStruct((t * X_TILE, LANES), jnp.uint32),
            jax.ShapeDtypeStruct((8, t), jnp.int32),
            jax.ShapeDtypeStruct((8, t), F32),
            jax.ShapeDtypeStruct((ROUTE_ROWS, LANES), jnp.int32),
        ],
        scratch_shapes=[pltpu.VMEM((ROUTE_ROWS, LANES), F32)],
        compiler_params=_params(("arbitrary",)),
        name="out_router",
    )(x2, og, om, oc, w_bf, g, wr_bf, br, tri)


DMA_UNROLL = 8
PLACE_UNROLL = 32


def _row_to_token(dest_ref, pad_lo_ref, pad_hi_ref, tok_ref):
    def pad_segment(e, c):
        hi = pad_hi_ref[e]

        def pad(g, c2):
            for r in range(8):
                tok_ref[hi - 8 * (g + 1) + r] = 0
            return c2
        lax.fori_loop(0, lax.shift_right_logical(hi - pad_lo_ref[e] + 7, 3), pad, 0)
        return c

    lax.fori_loop(0, pad_lo_ref.shape[0], pad_segment, 0)

    n_tok = dest_ref.shape[0] // 2

    def place(t, c):
        tok_ref[dest_ref[t]] = t
        tok_ref[dest_ref[n_tok + t]] = t
        return c

    lax.fori_loop(0, n_tok, place, 0, unroll=PLACE_UNROLL)


def _experts_kernel(be_ref, run_ref, next_ref, nused_ref, dest_ref, pad_lo_ref, pad_hi_ref,
                    hn_hbm, wg_hbm, wu_hbm, wd_hbm, ys_hbm,
                    hn_vmem, xg_even, xg_odd, y_even, y_odd, wg_f, wu_f, wd_f, wg_bf, wu_bf, wd_bf, tok_ref,
                    hn_sem, w_sem, y_sem):
    n_used = nused_ref[0]
    n_blocks = be_ref.shape[0]
    block_rows = ROW_BLOCK * ROW_TILE

    def weight_copies(e, slot):
        return [pltpu.make_async_copy(src.at[e], dst.at[slot], w_sem.at[slot])
                for src, dst in ((wg_hbm, wg_f), (wu_hbm, wu_f), (wd_hbm, wd_f))]

    def y_copy(buf, parity, i):
        dst = ys_hbm.at[pl.ds(pl.multiple_of(i * block_rows, block_rows), block_rows), :]
        return pltpu.make_async_copy(buf, dst, y_sem.at[parity])

    def gather(i, dst):
        base = jnp.minimum(i, n_blocks - 1) * ROW_BLOCK
        for r in range(ROW_BLOCK):
            src = pl.ds(pl.multiple_of(tok_ref[base + r] * X_TILE, X_TILE), X_TILE)
            dst[r * X_TILE:(r + 1) * X_TILE, :] = hn_vmem[src, :]

    def block(i, parity, cur, nxt, ybuf):
        @pl.when(jnp.logical_or(i == 0, be_ref[i] != be_ref[jnp.maximum(i - 1, 0)]))
        def _():
            slot = jnp.bitwise_and(run_ref[i], 1)
            for c in weight_copies(be_ref[i], slot):
                c.wait()
            wg_bf[...] = wg_f[slot].astype(BF16)
            wu_bf[...] = wu_f[slot].astype(BF16)
            wd_bf[...] = wd_f[slot].astype(BF16)

            @pl.when(next_ref[i] < n_used)
            def _():
                for c in weight_copies(be_ref[jnp.minimum(next_ref[i], n_blocks - 1)], 1 - slot):
                    c.start()

        @pl.when(i >= 2)
        def _():
            y_copy(ybuf, parity, i).wait()

        kh = D_MODEL // 2
        gather(i + 1, nxt)
        x_lo, x_hi = _unpack_bf16_pairs(_load_row_tiles(cur, ROW_BLOCK))
        gate = _dot(x_lo, wg_bf[:kh, :]) + _dot(x_hi, wg_bf[kh:, :])
        up = _dot(x_lo, wu_bf[:kh, :]) + _dot(x_hi, wu_bf[kh:, :])
        hid = (_silu(gate) * up).astype(BF16)
        _store_row_tiles(ybuf, _dot(hid, wd_bf[...]))
        y_copy(ybuf, parity, i).start()

    hn_load = pltpu.make_async_copy(hn_hbm, hn_vmem, hn_sem)
    hn_load.start()
    for c in weight_copies(be_ref[0], 0):
        c.start()
    _row_to_token(dest_ref, pad_lo_ref, pad_hi_ref, tok_ref)
    hn_load.wait()
    gather(0, xg_even)

    def pair(p, carry):
        block(2 * p, 0, xg_even, xg_odd, y_even)

        @pl.when(2 * p + 1 < n_used)
        def _():
            block(2 * p + 1, 1, xg_odd, xg_even, y_odd)
        return carry

    lax.fori_loop(0, lax.shift_right_logical(n_used + 1, 1), pair, 0)

    y_copy(y_even, 0, 0).wait()

    @pl.when(n_used >= 2)
    def _():
        y_copy(y_odd, 1, 0).wait()

    y_even[...] = jnp.zeros_like(y_even)

    def zero_start(i, carry):
        y_copy(y_even, 0, i).start()
        return carry

    def zero_wait(i, carry):
        y_copy(y_even, 0, i).wait()
        return carry

    lax.fori_loop(n_used, n_blocks, zero_start, 0)
    lax.fori_loop(n_used, n_blocks, zero_wait, 0)


def _experts(hn, dest, pad_lo, pad_hi, n_rows, block_expert, block_run, block_next, n_used,
             w_gate, w_up, w_down):
    hbm = pl.BlockSpec(memory_space=pl.ANY)
    grid_spec = pltpu.PrefetchScalarGridSpec(
        num_scalar_prefetch=7,
        grid=(1,),
        in_specs=[hbm, hbm, hbm, hbm],
        out_specs=hbm,
        scratch_shapes=[pltpu.VMEM(hn.shape, jnp.uint32),
                        pltpu.VMEM((ROW_BLOCK * X_TILE, LANES), jnp.uint32),
                        pltpu.VMEM((ROW_BLOCK * X_TILE, LANES), jnp.uint32),
                        pltpu.VMEM((ROW_BLOCK * ROW_TILE, LANES), F32),
                        pltpu.VMEM((ROW_BLOCK * ROW_TILE, LANES), F32),
                        pltpu.VMEM((2, D_MODEL, MOE_FF), F32), pltpu.VMEM((2, D_MODEL, MOE_FF), F32),
                        pltpu.VMEM((2, MOE_FF, D_MODEL), F32),
                        pltpu.VMEM((D_MODEL, MOE_FF), BF16), pltpu.VMEM((D_MODEL, MOE_FF), BF16),
                        pltpu.VMEM((MOE_FF, D_MODEL), BF16),
                        pltpu.SMEM((n_rows,), jnp.int32),
                        pltpu.SemaphoreType.DMA, pltpu.SemaphoreType.DMA((2,)), pltpu.SemaphoreType.DMA((2,))],
    )
    return pl.pallas_call(
        _experts_kernel,
        grid_spec=grid_spec,
        out_shape=jax.ShapeDtypeStruct((n_rows * ROW_TILE, LANES), F32),
        compiler_params=_params(("arbitrary",)),
        name="experts",
    )(block_expert, block_run, block_next, n_used, dest, pad_lo, pad_hi, hn, w_gate, w_up, w_down)


COMB_TM = 256


def _combine_kernel(dest_ref, dest_next_ref, x1_ref, gate_ref, ys_ref, out_ref, ybuf, sems):
    g = pl.program_id(0)

    def row_copy(d, slot, k, r):
        src = ys_ref.at[pl.ds(pl.multiple_of(d * ROW_TILE, ROW_TILE), ROW_TILE), :]
        dst = ybuf.at[slot, k, pl.ds(pl.multiple_of(r * ROW_TILE, ROW_TILE), ROW_TILE), :]
        return pltpu.make_async_copy(src, dst, sems.at[slot])

    def gather(dref, base, slot):
        def issue(r, c):
            row_copy(dref[0, 0, 0, base + r], slot, 0, r).start(priority=0)
            row_copy(dref[1, 0, 0, base + r], slot, 1, r).start(priority=1)
            return c
        lax.fori_loop(0, COMB_TM, issue, 0, unroll=DMA_UNROLL)

    def finish(slot):
        for k in range(2):
            pltpu.make_async_copy(ys_ref.at[pl.ds(0, COMB_TM * ROW_TILE), :], ybuf.at[slot, k],
                                  sems.at[slot]).wait()
        rows = slice(slot * COMB_TM, (slot + 1) * COMB_TM)
        gate = gate_ref[:, rows].T
        w0 = gate[:, 0:1]
        w1 = gate[:, 1:2]
        y0 = _load_row_tiles(ybuf.at[slot, 0], COMB_TM)
        y1 = _load_row_tiles(ybuf.at[slot, 1], COMB_TM)
        out_ref[rows, :] = x1_ref[rows, :] + (y0 * w0 + y1 * w1)

    @pl.when(g == 0)
    def _():
        gather(dest_ref, 0, 0)

    gather(dest_ref, COMB_TM, 1)
    finish(0)

    @pl.when(g + 1 < pl.num_programs(0))
    def _():
        gather(dest_next_ref, 0, 0)

    finish(1)


def _combine(x1, gates, ys, dest):
    t = x1.shape[0]
    ng = t // (2 * COMB_TM)
    dest3 = dest.reshape(2, ng, 1, 2 * COMB_TM)
    smem = lambda f: pl.BlockSpec((2, 1, 1, 2 * COMB_TM), f, memory_space=pltpu.SMEM)
    return pl.pallas_call(
        _combine_kernel,
        grid=(ng,),
        in_specs=[smem(lambda i: (0, i, 0, 0)),
                  smem(lambda i: (0, jnp.minimum(i + 1, ng - 1), 0, 0)),
                  pl.BlockSpec((2 * COMB_TM, D_MODEL), lambda i: (i, 0)),
                  pl.BlockSpec((8, 2 * COMB_TM), lambda i: (0, i)),
                  pl.BlockSpec(memory_space=pl.ANY)],
        out_specs=pl.BlockSpec((2 * COMB_TM, D_MODEL), lambda i: (i, 0)),
        out_shape=jax.ShapeDtypeStruct((t, D_MODEL), F32),
        scratch_shapes=[pltpu.VMEM((2, 2, COMB_TM * ROW_TILE, LANES), F32),
                        pltpu.SemaphoreType.DMA((2,))],
        compiler_params=_params(("arbitrary",)),
        name="combine",
    )(dest3, dest3, x1, gates, ys)


def _layer(x, mem, attn_norm_g, mem_norm_g, w_in, w_gla_gk, b_gla_gk, gla_out_norm_g,
           moba_q_norm_g, moba_k_norm_g, w_mem_kv, mem_q_norm_g, mem_k_norm_g, w_out,
           ffn_norm_g, w_router_group, b_router_group, w_router_expert, b_router_expert,
           w_gate, w_up, w_down):
    b, s, d = x.shape
    t = b * s
    x2 = x.reshape(t, d)
    row = lambda v: v.reshape(1, -1).astype(F32)
    tile_heads = lambda v: jnp.tile(v.astype(F32), HEADS).reshape(1, QK_W)
    hid = np.arange(QK_W) // DH
    seg = jnp.asarray((hid[:, None] == hid[None, :]) / DH, BF16)

    w_in_p = jnp.concatenate([w_in[:, :1536], w_in[:, 1552:], w_in[:, 1536:1552]], axis=1).astype(BF16)
    wr = jnp.concatenate([w_router_group,
                          jnp.transpose(w_router_expert, (1, 0, 2)).reshape(d, N_EXPERTS),
                          jnp.zeros((d, LANES - N_GROUPS - N_EXPERTS), F32)], axis=1).astype(BF16)
    br = jnp.concatenate([b_router_group, b_router_expert.reshape(N_EXPERTS),
                          jnp.zeros((LANES - N_GROUPS - N_EXPERTS,), F32)]).reshape(1, LANES)

    memk, memvt = _mem_kv(mem, row(mem_norm_g), w_mem_kv.astype(BF16), tile_heads(mem_k_norm_g), seg)
    qk, gv, gr, glr, mq, mk, mvt, kmean, cq = _in_proj(
        x2, row(attn_norm_g), w_in_p, seg, tile_heads(moba_q_norm_g), tile_heads(moba_k_norm_g),
        tile_heads(mem_q_norm_g))
    o_gla = _gla(qk, gv, gr, glr, w_gla_gk.astype(BF16), row(b_gla_gk), row(gla_out_norm_g), b, s)
    o_moba, o_mem = _moba_mem(mq, mk, mvt, kmean, cq, memk, memvt, b, s)
    x1, hn, e_ids, gates, counts = _out_router(
        x2, o_gla.reshape(t, GLA_V_W), o_moba.reshape(t, QK_W), o_mem.reshape(t, QK_W),
        w_out.astype(BF16), row(ffn_norm_g), wr, br)

    counts = counts[N_GROUPS:N_GROUPS + N_EXPERTS, 0]
    padded = (counts + ROW_BLOCK - 1) // ROW_BLOCK * ROW_BLOCK
    pends = jnp.cumsum(padded)
    pstarts = pends - padded
    onehot = e_ids[:2, :, None] == jnp.arange(N_EXPERTS, dtype=jnp.int32)
    dest = (jnp.sum(jnp.where(onehot, pstarts, 0), axis=-1) + e_ids[2:4]).astype(jnp.int32)
    n_rows = (t * 2 + N_EXPERTS * (ROW_BLOCK - 1) + ROW_BLOCK - 1) // ROW_BLOCK * ROW_BLOCK
    nb = n_rows // ROW_BLOCK
    block_start = jnp.arange(nb, dtype=jnp.int32) * ROW_BLOCK
    block_expert = jnp.minimum(jnp.sum(block_start[:, None] >= pends[None, :], axis=1),
                               N_EXPERTS - 1).astype(jnp.int32)
    n_used = (pends[-1] // ROW_BLOCK).astype(jnp.int32).reshape(1)
    ended = block_start[:, None] >= pends[None, :]
    block_run = jnp.sum(ended & (counts > 0)[None, :], axis=1).astype(jnp.int32)
    block_next = (jnp.min(jnp.where(ended, n_rows, pends[None, :]), axis=1) // ROW_BLOCK).astype(jnp.int32)

    pad_lo = jnp.concatenate([pstarts + counts, pends[-1:]]).astype(jnp.int32)
    pad_hi = jnp.concatenate([pends, jnp.full((1,), n_rows)]).astype(jnp.int32)
    ys = _experts(hn, dest.reshape(-1), pad_lo, pad_hi, n_rows, block_expert, block_run, block_next, n_used,
                  w_gate, w_up, w_down)
    out = _combine(x1, gates, ys, dest)
    return out.reshape(b, s, d)


def kernel(x, mem, attn_norm_g, mem_norm_g, w_in, w_gla_gk, b_gla_gk, gla_out_norm_g, moba_q_norm_g, moba_k_norm_g, w_mem_kv, mem_q_norm_g, mem_k_norm_g, w_out, ffn_norm_g, w_router_group, b_router_group, w_router_expert, b_router_expert, w_gate, w_up, w_down):
    depth = w_in.shape[0]
    for l in range(depth):
        x = _layer(x, mem, attn_norm_g[l], mem_norm_g[l], w_in[l], w_gla_gk[l], b_gla_gk[l],
                   gla_out_norm_g[l], moba_q_norm_g[l], moba_k_norm_g[l], w_mem_kv[l],
                   mem_q_norm_g[l], mem_k_norm_g[l], w_out[l], ffn_norm_g[l], w_router_group[l],
                   b_router_group[l], w_router_expert[l], b_router_expert[l],
                   w_gate[l], w_up[l], w_down[l])
    return x
```

```python
import functools

import jax
import jax.numpy as jnp
import numpy as np
from jax import lax
from jax.experimental import pallas as pl
from jax.experimental.pallas import tpu as pltpu

F32 = jnp.float32
BF16 = jnp.bfloat16
EPS = 1e-6
NEG_INF = float("-inf")

D_MODEL = 1024
N_MEM = 256
HEADS = 4
DH = 64
GLA_DV = 128
GLA_RANK = 16
GLA_GATE_NORMALIZER = 16.0
GLA_CHUNK = 64
MOBA_BLOCK = 256
MOBA_TOPK = 3
QK_W = HEADS * DH
GLA_V_W = HEADS * GLA_DV
N_GROUPS = 4
EXPERTS_PER_GROUP = 8
N_EXPERTS = N_GROUPS * EXPERTS_PER_GROUP
MOE_FF = 512
LANES = 128
ROW_BLOCK = 256

VMEM_LIMIT = 56 * 1024 * 1024


def _params(sem):
    return pltpu.CompilerParams(dimension_semantics=sem, vmem_limit_bytes=VMEM_LIMIT)


def _nt(a, b):
    return lax.dot_general(a, b, (((1,), (1,)), ((), ())), preferred_element_type=F32)


def _tn(a, b):
    return lax.dot_general(a, b, (((0,), (0,)), ((), ())), preferred_element_type=F32)


def _dot(a, b):
    return jnp.dot(a, b, preferred_element_type=F32)


def _rms_rows(x, g):
    ms = jnp.mean(x * x, axis=-1, keepdims=True)
    return x * lax.rsqrt(ms + EPS) * g


def _split3(x):
    h1 = x.astype(BF16)
    r1 = x - h1.astype(F32)
    h2 = r1.astype(BF16)
    h3 = (r1 - h2.astype(F32)).astype(BF16)
    return h1, h2, h3


def _head_mean_sq(x, seg):
    sq = x * x
    hi = sq.astype(BF16)
    lo = (sq - hi.astype(F32)).astype(BF16)
    return _dot(hi, seg) + _dot(lo, seg)


def _silu(x):
    return x * (1.0 / (1.0 + jnp.exp(-x)))


VT_ROWS = DH + 16
VT_ALL = HEADS * VT_ROWS


ALIBI_SLOPES = tuple(2.0 ** (-8.0 * (h + 1) / HEADS) for h in range(HEADS))


def _vt_with_ones(v, key_slopes=None):
    n = v.shape[0]
    vt = v.T
    ones = jnp.ones((VT_ROWS - DH, n), F32)
    pos = lax.broadcasted_iota(jnp.int32, (1, n), 1).astype(F32)
    parts = []
    for h in range(HEADS):
        scale = 1.0 if key_slopes is None else jnp.exp(key_slopes[h] * pos)
        parts += [vt[h * DH:(h + 1) * DH] * scale, ones * scale]
    return jnp.concatenate(parts, axis=0).astype(BF16)


ROW_TILE = D_MODEL // LANES
X_TILE = ROW_TILE // 2


def _store_row_tiles(ref, x):
    n, tiles = x.shape[0], x.shape[1] // LANES
    for c in range(tiles):
        ref[pl.ds(c, n, stride=tiles), :] = x[:, c * LANES:(c + 1) * LANES]


def _load_row_tiles(ref, n):
    tiles = ref.shape[0] // n
    return jnp.concatenate([ref[pl.ds(c, n, stride=tiles), :] for c in range(tiles)], axis=1)


def _pack_bf16_pairs(x):
    w = x.shape[1] // 2
    bits = lambda t: lax.bitcast_convert_type(t.astype(BF16).astype(F32), jnp.uint32)
    return (bits(x[:, w:]) & jnp.uint32(0xFFFF0000)) | (bits(x[:, :w]) >> 16)


def _unpack_bf16_pairs(words):
    lo = lax.bitcast_convert_type(words << 16, F32).astype(BF16)
    hi = lax.bitcast_convert_type(words & jnp.uint32(0xFFFF0000), F32).astype(BF16)
    return lo, hi


def _mem_kv_kernel(mem_ref, g_ref, w_ref, gk_ref, seg_ref, k_ref, vt_ref):
    h = _rms_rows(mem_ref[0], g_ref[...]).astype(BF16)
    kv = _dot(h, w_ref[...])
    k = kv[:, :QK_W]
    kn = k * lax.rsqrt(_head_mean_sq(k, seg_ref[...]) + EPS) * gk_ref[...]
    k_ref[0] = kn.astype(BF16)
    vt_ref[0] = _vt_with_ones(kv[:, QK_W:])


def _mem_kv(mem, g, w_bf, gk_t, seg):
    b = mem.shape[0]
    return pl.pallas_call(
        _mem_kv_kernel,
        grid=(b,),
        in_specs=[
            pl.BlockSpec((1, N_MEM, D_MODEL), lambda i: (i, 0, 0)),
            pl.BlockSpec((1, D_MODEL), lambda i: (0, 0)),
            pl.BlockSpec((D_MODEL, 2 * QK_W), lambda i: (0, 0)),
            pl.BlockSpec((1, QK_W), lambda i: (0, 0)),
            pl.BlockSpec((QK_W, QK_W), lambda i: (0, 0)),
        ],
        out_specs=[
            pl.BlockSpec((1, N_MEM, QK_W), lambda i: (i, 0, 0)),
            pl.BlockSpec((1, VT_ALL, N_MEM), lambda i: (i, 0, 0)),
        ],
        out_shape=[
            jax.ShapeDtypeStruct((b, N_MEM, QK_W), BF16),
            jax.ShapeDtypeStruct((b, VT_ALL, N_MEM), BF16),
        ],
        compiler_params=_params(("arbitrary",)),
        name="mem_kv",
    )(mem, g, w_bf, gk_t, seg)


IN_TM = 1024
_C_QK, _C_V, _C_R, _C_MQ, _C_MK, _C_MV, _C_CQ, _C_LR = 0, 512, 1024, 1536, 1792, 2048, 2304, 2560
D_IN = 2576


def _in_proj_kernel(x_ref, g_ref, w_ref, seg_ref, gq_ref, gk_ref, gc_ref,
                    qk_ref, v_ref, r_ref, lr_ref, mq_ref, mk_ref, mvt_ref, kmean_ref, cq_ref):
    h = _rms_rows(x_ref[...], g_ref[...]).astype(BF16)
    p = _dot(h, w_ref[...])
    seg = seg_ref[...]
    qk_ref[...] = p[:, _C_QK:_C_V]
    v_ref[...] = p[:, _C_V:_C_R].astype(BF16)
    r_ref[...] = p[:, _C_R:_C_MQ]
    lr_ref[...] = p[:, _C_LR:D_IN]

    def head_norm(t, gain):
        return t * lax.rsqrt(_head_mean_sq(t, seg) + EPS) * gain

    scale = DH ** -0.5
    mq_ref[...] = (head_norm(p[:, _C_MQ:_C_MK], gq_ref[...]) * scale).astype(BF16)
    cq_ref[...] = (head_norm(p[:, _C_CQ:_C_LR], gc_ref[...]) * scale).astype(BF16)
    kn = head_norm(p[:, _C_MK:_C_MV], gk_ref[...])
    mk_ref[...] = kn.astype(BF16)
    mv = p[:, _C_MV:_C_CQ]
    for j in range(IN_TM // MOBA_BLOCK):
        rows = slice(j * MOBA_BLOCK, (j + 1) * MOBA_BLOCK)
        kmean_ref[0, j:j + 1, :] = jnp.mean(kn[rows], axis=0, keepdims=True)
        mvt_ref[j] = _vt_with_ones(mv[rows], ALIBI_SLOPES)


def _in_proj(x2, g, w_bf, seg, gq_t, gk_t, gc_t):
    t = x2.shape[0]
    nt = t // IN_TM
    nb = IN_TM // MOBA_BLOCK
    row = lambda w: pl.BlockSpec((IN_TM, w), lambda i: (i, 0))
    const = lambda a, b: pl.BlockSpec((a, b), lambda i: (0, 0))
    return pl.pallas_call(
        _in_proj_kernel,
        grid=(nt,),
        in_specs=[row(D_MODEL), const(1, D_MODEL), const(D_MODEL, D_IN), const(QK_W, QK_W),
                  const(1, QK_W), const(1, QK_W), const(1, QK_W)],
        out_specs=[row(2 * QK_W), row(GLA_V_W), row(GLA_V_W), row(GLA_RANK), row(QK_W), row(QK_W),
                   pl.BlockSpec((nb, VT_ALL, MOBA_BLOCK), lambda i: (i, 0, 0)),
                   pl.BlockSpec((1, nb, QK_W), lambda i: (i, 0, 0)),
                   row(QK_W)],
        out_shape=[
            jax.ShapeDtypeStruct((t, 2 * QK_W), F32),
            jax.ShapeDtypeStruct((t, GLA_V_W), BF16),
            jax.ShapeDtypeStruct((t, GLA_V_W), F32),
            jax.ShapeDtypeStruct((t, GLA_RANK), F32),
            jax.ShapeDtypeStruct((t, QK_W), BF16),
            jax.ShapeDtypeStruct((t, QK_W), BF16),
            jax.ShapeDtypeStruct((t // MOBA_BLOCK, VT_ALL, MOBA_BLOCK), BF16),
            jax.ShapeDtypeStruct((nt, nb, QK_W), F32),
            jax.ShapeDtypeStruct((t, QK_W), BF16),
        ],
        compiler_params=_params(("arbitrary",)),
        name="in_proj",
    )(x2, g, w_bf, seg, gq_t, gk_t, gc_t)


GLA_TC = 256


def _gla_kernel(qk_ref, v_ref, r_ref, lr_ref, wgk_ref, bgk_ref, gn_ref, tri_ref, ones_ref,
                o_ref, st_ref):
    @pl.when(pl.program_id(1) == 0)
    def _():
        st_ref[...] = jnp.zeros_like(st_ref)

    qk = qk_ref[0]
    q = qk[:, :QK_W]
    k = qk[:, QK_W:]
    gk = _dot(lr_ref[0].astype(BF16), wgk_ref[...]) + bgk_ref[...]
    g = -(jnp.maximum(-gk, 0.0) + jnp.log1p(jnp.exp(-jnp.abs(gk)))) / GLA_GATE_NORMALIZER
    g1, g2, g3 = _split3(g)
    tri = tri_ref[...]
    ones = ones_ref[...]
    cum = _dot(tri, g1) + _dot(tri, g2) + _dot(tri, g3)
    tot = _dot(ones, g1) + _dot(ones, g2) + _dot(ones, g3)
    q_dec = (q * (DH ** -0.5) * jnp.exp(cum)).astype(BF16)
    k_inv = (k * jnp.exp(-cum)).astype(BF16)
    k_end = (k * jnp.exp(tot - cum)).astype(BF16)
    decay = jnp.exp(tot)

    lane_head = lax.broadcasted_iota(jnp.int32, (GLA_TC, QK_W), 1) // DH
    row_t = lax.broadcasted_iota(jnp.int32, (GLA_TC, GLA_TC), 0)
    col_t = lax.broadcasted_iota(jnp.int32, (GLA_TC, GLA_TC), 1)
    causal = (row_t >= col_t) & (row_t // GLA_CHUNK == col_t // GLA_CHUNK)
    same_head = (lax.broadcasted_iota(jnp.int32, (GLA_V_W, QK_W), 0) // GLA_DV
                 == lax.broadcasted_iota(jnp.int32, (GLA_V_W, QK_W), 1) // DH)
    gain = gn_ref[...]

    v = v_ref[0]
    chunks = [slice(c * GLA_CHUNK, (c + 1) * GLA_CHUNK) for c in range(GLA_TC // GLA_CHUNK)]
    d_st = [jnp.where(same_head, _tn(v[rows], k_end[rows]), 0.0) for rows in chunks]
    states = [st_ref[...]]
    for c, rows in enumerate(chunks):
        states.append(states[c] * decay[rows.start:rows.start + 1, :] + d_st[c])
    st_ref[...] = states[-1]
    o_inter = jnp.concatenate([_nt(q_dec[rows], states[c].astype(BF16)) for c, rows in enumerate(chunks)],
                              axis=0)
    q_stack = jnp.concatenate([jnp.where(lane_head == h, q_dec, jnp.zeros_like(q_dec)) for h in range(HEADS)],
                              axis=0)
    a_all = _nt(q_stack, k_inv)
    outs = []
    for h in range(HEADS):
        a = jnp.where(causal, a_all[h * GLA_TC:(h + 1) * GLA_TC], 0.0).astype(BF16)
        vs = slice(h * GLA_DV, (h + 1) * GLA_DV)
        oh = _dot(a, v[:, vs]) + o_inter[:, vs]
        outs.append(_rms_rows(oh, gain) * _silu(r_ref[0, :, vs]))
    o_ref[0] = jnp.concatenate(outs, axis=-1).astype(BF16)


def _gla(qk, v, r, lr, wgk_bf, bgk, gn, b, s):
    ns = s // GLA_TC
    idx = np.arange(GLA_TC)
    same_chunk = (idx[:, None] // GLA_CHUNK) == (idx[None, :] // GLA_CHUNK)
    tri = jnp.asarray(same_chunk & (idx[:, None] >= idx[None, :]), BF16)
    ones = jnp.asarray(same_chunk, BF16)
    seq = lambda w: pl.BlockSpec((1, GLA_TC, w), lambda i, j: (i, j, 0))
    const = lambda a, c: pl.BlockSpec((a, c), lambda i, j: (0, 0))
    return pl.pallas_call(
        _gla_kernel,
        grid=(b, ns),
        in_specs=[seq(2 * QK_W), seq(GLA_V_W), seq(GLA_V_W), seq(GLA_RANK),
                  const(GLA_RANK, QK_W), const(1, QK_W), const(1, GLA_DV),
                  const(GLA_TC, GLA_TC), const(GLA_TC, GLA_TC)],
        out_specs=seq(GLA_V_W),
        out_shape=jax.ShapeDtypeStruct((b, s, GLA_V_W), BF16),
        scratch_shapes=[pltpu.VMEM((GLA_V_W, QK_W), F32)],
        compiler_params=_params(("arbitrary", "arbitrary")),
        name="gla",
    )(qk.reshape(b, s, 2 * QK_W), v.reshape(b, s, GLA_V_W), r.reshape(b, s, GLA_V_W),
      lr.reshape(b, s, GLA_RANK), wgk_bf, bgk, gn, tri, ones)


TQ = MOBA_BLOCK
KV_UNROLL = 8


def _moba_mem_kernel(q_ref, k_ref, vt_ref, kmean_ref, cq_ref, mk_ref, mvt_ref,
                     o_moba_ref, o_mem_ref,
                     qcat_ref, selb_ref, s_ref, *, n_blocks):
    i = pl.program_id(1)
    lane_head = lax.broadcasted_iota(jnp.int32, (TQ, QK_W), 1) // DH
    dist0 = (lax.broadcasted_iota(jnp.int32, (MOBA_BLOCK, TQ), 1)
             - lax.broadcasted_iota(jnp.int32, (MOBA_BLOCK, TQ), 0)).astype(F32)
    slopes = ALIBI_SLOPES
    heads = [slice(h * TQ, (h + 1) * TQ) for h in range(HEADS)]
    vrows = [slice(h * VT_ROWS, (h + 1) * VT_ROWS) for h in range(HEADS)]

    def stack_heads(x):
        return jnp.concatenate([jnp.where(lane_head == h, x, jnp.zeros_like(x)) for h in range(HEADS)], axis=0)

    def finish(accl):
        return accl[:DH] * (1.0 / accl[DH:DH + 1])

    s_all = _nt(mk_ref[0], stack_heads(cq_ref[0]))
    outs = []
    for h in range(HEADS):
        s = s_all[:, heads[h]]
        p = jnp.exp(s - jnp.max(s, axis=0, keepdims=True)).astype(BF16)
        outs.append(finish(_dot(mvt_ref[0, vrows[h], :], p)))
    o_mem_ref[0] = jnp.concatenate(outs, axis=0).T.astype(BF16)

    qcat = stack_heads(q_ref[0])
    qcat_ref[...] = qcat
    k_own = k_ref[0, pl.ds(pl.multiple_of(i * MOBA_BLOCK, MOBA_BLOCK), MOBA_BLOCK), :]
    gate_all = _nt(kmean_ref[0].astype(BF16), qcat)
    s_all = _nt(k_own, qcat)
    blk = lax.broadcasted_iota(jnp.int32, (n_blocks, TQ), 0)
    blk_f = blk.astype(F32)

    tq = lax.broadcasted_iota(jnp.int32, (1, TQ), 1).astype(F32)
    init = []
    for h in range(HEADS):
        gate = jnp.where(blk < i, gate_all[:, heads[h]], NEG_INF)
        chosen = jnp.zeros((n_blocks, TQ), jnp.bool_)
        for r in range(MOBA_TOPK):
            mx = jnp.max(gate, axis=0, keepdims=True)
            first = jnp.min(jnp.where(gate == mx, blk_f, float(n_blocks)), axis=0, keepdims=True)
            hit = blk_f == first
            chosen = chosen | (hit & (mx > NEG_INF))
            gate = jnp.where(hit, NEG_INF, gate)
        selb_ref[h] = jnp.where(chosen, 0.0, NEG_INF)
        aq = -slopes[h] * tq
        s = jnp.where(dist0 >= 0.0, s_all[:, heads[h]], NEG_INF)
        m0 = jnp.max(s, axis=0, keepdims=True) + aq
        p = jnp.exp(s - (m0 - aq)).astype(BF16)
        init += [m0, _dot(vt_ref[i, vrows[h], :], p)]

    def stage_scores(j, slot):
        kj = k_ref[0, pl.ds(pl.multiple_of(j * MOBA_BLOCK, MOBA_BLOCK), MOBA_BLOCK), :]
        s_ref[slot] = _nt(kj, qcat_ref[...])

    stage_scores(0, 0)

    def past_blocks(g, carry, unroll, base):
        carry = list(carry)
        for u in range(unroll):
            j = base + g * unroll + u
            stage_scores(jnp.minimum(j + 1, n_blocks - 1), (u + 1) % 2)
            off = jnp.full((1, TQ), (i - j) * MOBA_BLOCK, jnp.int32).astype(F32) + tq
            for h in range(HEADS):
                m, accl = carry[2 * h:2 * h + 2]
                s1 = s_ref[u % 2, :, heads[h]]
                rb = selb_ref[h, pl.ds(j, 1), :] - slopes[h] * off
                m_new = jnp.maximum(m, jnp.max(s1, axis=0, keepdims=True) + rb)
                p = jnp.exp(s1 - (m_new - rb)).astype(BF16)
                carry[2 * h:2 * h + 2] = [m_new, jnp.exp(m - m_new) * accl + _dot(vt_ref[j, vrows[h], :], p)]
        return tuple(carry)

    n_main = i // KV_UNROLL
    rest = i - n_main * KV_UNROLL
    carry = lax.fori_loop(0, n_main, functools.partial(past_blocks, unroll=KV_UNROLL, base=0), tuple(init))
    final = lax.fori_loop(0, (rest + 1) // 2,
                          functools.partial(past_blocks, unroll=2, base=n_main * KV_UNROLL), carry)
    o_t = jnp.concatenate([finish(final[2 * h + 1]) for h in range(HEADS)], axis=0)
    o_moba_ref[0] = o_t.T.astype(BF16)


def _moba_mem(mq, mk, mvt, kmean, cq, memk, memvt, b, s):
    nq = s // TQ
    n_blocks = s // MOBA_BLOCK
    qspec = pl.BlockSpec((1, TQ, QK_W), lambda i, j: (i, j, 0))
    return pl.pallas_call(
        functools.partial(_moba_mem_kernel, n_blocks=n_blocks),
        grid=(b, nq),
        in_specs=[
            qspec,
            pl.BlockSpec((1, s, QK_W), lambda i, j: (i, 0, 0)),
            pl.BlockSpec((n_blocks, VT_ALL, MOBA_BLOCK), lambda i, j: (i, 0, 0)),
            pl.BlockSpec((1, n_blocks, QK_W), lambda i, j: (i, 0, 0)),
            qspec,
            pl.BlockSpec((1, N_MEM, QK_W), lambda i, j: (i, 0, 0)),
            pl.BlockSpec((1, VT_ALL, N_MEM), lambda i, j: (i, 0, 0)),
        ],
        out_specs=[qspec, qspec],
        out_shape=[jax.ShapeDtypeStruct((b, s, QK_W), BF16), jax.ShapeDtypeStruct((b, s, QK_W), BF16)],
        scratch_shapes=[
            pltpu.VMEM((HEADS * TQ, QK_W), BF16),
            pltpu.VMEM((HEADS, n_blocks, TQ), F32),
            pltpu.VMEM((2, MOBA_BLOCK, HEADS * TQ), F32),
        ],
        compiler_params=_params(("arbitrary", "arbitrary")),
        name="moba_mem",
    )(mq.reshape(b, s, QK_W), mk.reshape(b, s, QK_W), mvt, kmean.reshape(b, n_blocks, QK_W),
      cq.reshape(b, s, QK_W), memk, memvt)


OUT_TM = 1024
ROUTE_ROWS = 40


def _out_router_kernel(x_ref, og_ref, om_ref, oc_ref, w_ref, g_ref, wr_ref, br_ref, tri_ref,
                       x1_ref, h_ref, e_ref, gate_ref, count_ref, carry_ref):
    w = w_ref[...]
    x1 = (x_ref[...] + _dot(og_ref[...], w[:GLA_V_W]) + _dot(om_ref[...], w[GLA_V_W:GLA_V_W + QK_W])
          + _dot(oc_ref[...], w[GLA_V_W + QK_W:]))
    x1_ref[...] = x1
    hn = _rms_rows(x1, g_ref[...])
    _store_row_tiles(h_ref, _pack_bf16_pairs(hn))
    logits = _dot(hn.astype(BF16), wr_ref[...]) + br_ref[...]
    lt = logits.T[:ROUTE_ROWS]
    row = lax.broadcasted_iota(jnp.int32, lt.shape, 0).astype(F32)
    lg = jnp.where(row < N_GROUPS, lt, NEG_INF)
    mg = jnp.max(lg, axis=0, keepdims=True)
    g_sel = jnp.min(jnp.where(lg == mg, row, float(LANES)), axis=0, keepdims=True)
    p_group = 1.0 / jnp.sum(jnp.exp(lg - mg), axis=0, keepdims=True)
    lo = N_GROUPS + g_sel * EXPERTS_PER_GROUP
    le = jnp.where((row >= lo) & (row < lo + EXPERTS_PER_GROUP), lt, NEG_INF)
    m0 = jnp.max(le, axis=0, keepdims=True)
    i0 = jnp.min(jnp.where(le == m0, row, float(LANES)), axis=0, keepdims=True)
    le1 = jnp.where(row == i0, NEG_INF, le)
    m1 = jnp.max(le1, axis=0, keepdims=True)
    i1 = jnp.min(jnp.where(le1 == m1, row, float(LANES)), axis=0, keepdims=True)
    z = jnp.exp(m1 - m0)
    w0 = p_group / (1.0 + z)
    w1 = p_group * z / (1.0 + z)
    pad = jnp.zeros((8 - 2, lt.shape[1]), F32)
    gate_ref[...] = jnp.concatenate([w0, w1, pad], axis=0)

    @pl.when(pl.program_id(0) == 0)
    def _():
        carry_ref[...] = jnp.zeros_like(carry_ref)

    oh0 = (row == i0).astype(F32)
    oh1 = (row == i1).astype(F32)
    both = oh0 + oh1
    before = _dot(both.astype(BF16), tri_ref[...]) + carry_ref[:, 0:1]
    r0 = jnp.sum(oh0 * before, axis=0, keepdims=True)
    r1 = jnp.sum(oh1 * before, axis=0, keepdims=True)
    carry_ref[...] = carry_ref[...] + jnp.sum(both, axis=1, keepdims=True)
    count_ref[...] = carry_ref[...].astype(jnp.int32)
    e_ref[...] = jnp.concatenate([i0 - N_GROUPS, i1 - N_GROUPS, r0, r1, pad[:4]], axis=0).astype(jnp.int32)


def _out_router(x2, og, om, oc, w_bf, g, wr_bf, br):
    t = x2.shape[0]
    row = lambda w: pl.BlockSpec((OUT_TM, w), lambda i: (i, 0))
    const = lambda a, b: pl.BlockSpec((a, b), lambda i: (0, 0))
    idx = np.arange(OUT_TM)
    tri = jnp.asarray(idx[:, None] < idx[None, :], BF16)
    col = lambda r: pl.BlockSpec((r, OUT_TM), lambda i: (0, i))
    return pl.pallas_call(
        _out_router_kernel,
        grid=(t // OUT_TM,),
        in_specs=[row(D_MODEL), row(GLA_V_W), row(QK_W), row(QK_W), const(D_MODEL, D_MODEL),
                  const(1, D_MODEL), const(D_MODEL, LANES), const(1, LANES), const(OUT_TM, OUT_TM)],
        out_specs=[row(D_MODEL), pl.BlockSpec((OUT_TM * X_TILE, LANES), lambda i: (i, 0)),
                   col(8), col(8), const(ROUTE_ROWS, LANES)],
        out_shape=[
            jax.ShapeDtypeStruct((t, D_MODEL), F32),
            jax.ShapeDtypeStruct((t * X_TILE, LANES), jnp.uint32),
            jax.ShapeDtypeStruct((8, t), jnp.int32),
            jax.ShapeDtypeStruct((8, t), F32),
            jax.ShapeDtypeStruct((ROUTE_ROWS, LANES), jnp.int32),
        ],
        scratch_shapes=[pltpu.VMEM((ROUTE_ROWS, LANES), F32)],
        compiler_params=_params(("arbitrary",)),
        name="out_router",
    )(x2, og, om, oc, w_bf, g, wr_bf, br, tri)


DMA_UNROLL = 8
PLACE_UNROLL = 32


def _row_to_token(dest_ref, pad_lo_ref, pad_hi_ref, tok_ref):
    def pad_segment(e, c):
        hi = pad_hi_ref[e]

        def pad(g, c2):
            for r in range(8):
                tok_ref[hi - 8 * (g + 1) + r] = 0
            return c2
        lax.fori_loop(0, lax.shift_right_logical(hi - pad_lo_ref[e] + 7, 3), pad, 0)
        return c

    lax.fori_loop(0, pad_lo_ref.shape[0], pad_segment, 0)

    n_tok = dest_ref.shape[0] // 2

    def place(t, c):
        tok_ref[dest_ref[t]] = t
        tok_ref[dest_ref[n_tok + t]] = t
        return c

    lax.fori_loop(0, n_tok, place, 0, unroll=PLACE_UNROLL)


def _experts_kernel(be_ref, run_ref, next_ref, nused_ref, dest_ref, pad_lo_ref, pad_hi_ref,
                    hn_hbm, wg_hbm, wu_hbm, wd_hbm, ys_hbm,
                    hn_vmem, xg_even, xg_odd, y_even, y_odd, wg_f, wu_f, wd_f, wg_bf, wu_bf, wd_bf, tok_ref,
                    hn_sem, w_sem, y_sem):
    n_used = nused_ref[0]
    n_blocks = be_ref.shape[0]
    block_rows = ROW_BLOCK * ROW_TILE

    def weight_copies(e, slot):
        return [pltpu.make_async_copy(src.at[e], dst.at[slot], w_sem.at[slot])
                for src, dst in ((wg_hbm, wg_f), (wu_hbm, wu_f), (wd_hbm, wd_f))]

    def y_copy(buf, parity, i):
        dst = ys_hbm.at[pl.ds(pl.multiple_of(i * block_rows, block_rows), block_rows), :]
        return pltpu.make_async_copy(buf, dst, y_sem.at[parity])

    def gather(i, dst):
        base = jnp.minimum(i, n_blocks - 1) * ROW_BLOCK
        for r in range(ROW_BLOCK):
            src = pl.ds(pl.multiple_of(tok_ref[base + r] * X_TILE, X_TILE), X_TILE)
            dst[r * X_TILE:(r + 1) * X_TILE, :] = hn_vmem[src, :]

    def block(i, parity, cur, nxt, ybuf):
        @pl.when(jnp.logical_or(i == 0, be_ref[i] != be_ref[jnp.maximum(i - 1, 0)]))
        def _():
            slot = jnp.bitwise_and(run_ref[i], 1)
            for c in weight_copies(be_ref[i], slot):
                c.wait()
            wg_bf[...] = wg_f[slot].astype(BF16)
            wu_bf[...] = wu_f[slot].astype(BF16)
            wd_bf[...] = wd_f[slot].astype(BF16)

            @pl.when(next_ref[i] < n_used)
            def _():
                for c in weight_copies(be_ref[jnp.minimum(next_ref[i], n_blocks - 1)], 1 - slot):
                    c.start()

        @pl.when(i >= 2)
        def _():
            y_copy(ybuf, parity, i).wait()

        kh = D_MODEL // 2
        gather(i + 1, nxt)
        x_lo, x_hi = _unpack_bf16_pairs(_load_row_tiles(cur, ROW_BLOCK))
        gate = _dot(x_lo, wg_bf[:kh, :]) + _dot(x_hi, wg_bf[kh:, :])
        up = _dot(x_lo, wu_bf[:kh, :]) + _dot(x_hi, wu_bf[kh:, :])
        hid = (_silu(gate) * up).astype(BF16)
        _store_row_tiles(ybuf, _dot(hid, wd_bf[...]))
        y_copy(ybuf, parity, i).start()

    hn_load = pltpu.make_async_copy(hn_hbm, hn_vmem, hn_sem)
    hn_load.start()
    for c in weight_copies(be_ref[0], 0):
        c.start()
    _row_to_token(dest_ref, pad_lo_ref, pad_hi_ref, tok_ref)
    hn_load.wait()
    gather(0, xg_even)

    def pair(p, carry):
        block(2 * p, 0, xg_even, xg_odd, y_even)

        @pl.when(2 * p + 1 < n_used)
        def _():
            block(2 * p + 1, 1, xg_odd, xg_even, y_odd)
        return carry

    lax.fori_loop(0, lax.shift_right_logical(n_used + 1, 1), pair, 0)

    y_copy(y_even, 0, 0).wait()

    @pl.when(n_used >= 2)
    def _():
        y_copy(y_odd, 1, 0).wait()

    y_even[...] = jnp.zeros_like(y_even)

    def zero_start(i, carry):
        y_copy(y_even, 0, i).start()
        return carry

    def zero_wait(i, carry):
        y_copy(y_even, 0, i).wait()
        return carry

    lax.fori_loop(n_used, n_blocks, zero_start, 0)
    lax.fori_loop(n_used, n_blocks, zero_wait, 0)


def _experts(hn, dest, pad_lo, pad_hi, n_rows, block_expert, block_run, block_next, n_used,
             w_gate, w_up, w_down):
    hbm = pl.BlockSpec(memory_space=pl.ANY)
    grid_spec = pltpu.PrefetchScalarGridSpec(
        num_scalar_prefetch=7,
        grid=(1,),
        in_specs=[hbm, hbm, hbm, hbm],
        out_specs=hbm,
        scratch_shapes=[pltpu.VMEM(hn.shape, jnp.uint32),
                        pltpu.VMEM((ROW_BLOCK * X_TILE, LANES), jnp.uint32),
                        pltpu.VMEM((ROW_BLOCK * X_TILE, LANES), jnp.uint32),
                        pltpu.VMEM((ROW_BLOCK * ROW_TILE, LANES), F32),
                        pltpu.VMEM((ROW_BLOCK * ROW_TILE, LANES), F32),
                        pltpu.VMEM((2, D_MODEL, MOE_FF), F32), pltpu.VMEM((2, D_MODEL, MOE_FF), F32),
                        pltpu.VMEM((2, MOE_FF, D_MODEL), F32),
                        pltpu.VMEM((D_MODEL, MOE_FF), BF16), pltpu.VMEM((D_MODEL, MOE_FF), BF16),
                        pltpu.VMEM((MOE_FF, D_MODEL), BF16),
                        pltpu.SMEM((n_rows,), jnp.int32),
                        pltpu.SemaphoreType.DMA, pltpu.SemaphoreType.DMA((2,)), pltpu.SemaphoreType.DMA((2,))],
    )
    return pl.pallas_call(
        _experts_kernel,
        grid_spec=grid_spec,
        out_shape=jax.ShapeDtypeStruct((n_rows * ROW_TILE, LANES), F32),
        compiler_params=_params(("arbitrary",)),
        name="experts",
    )(block_expert, block_run, block_next, n_used, dest, pad_lo, pad_hi, hn, w_gate, w_up, w_down)


COMB_TM = 256


def _combine_kernel(d0_ref, d1_ref, d0_next_ref, d1_next_ref, x1_ref, gate_ref, ys_ref, out_ref, ybuf, sems):
    g = pl.program_id(0)

    def row_copy(d, slot, k, r):
        src = ys_ref.at[pl.ds(pl.multiple_of(d * ROW_TILE, ROW_TILE), ROW_TILE), :]
        dst = ybuf.at[slot, k, pl.ds(pl.multiple_of(r * ROW_TILE, ROW_TILE), ROW_TILE), :]
        return pltpu.make_async_copy(src, dst, sems.at[slot])

    def gather(drefs, base, slot):
        def issue(r, c):
            row_copy(drefs[0][0, 0, base + r], slot, 0, r).start(priority=0)
            row_copy(drefs[1][0, 0, base + r], slot, 1, r).start(priority=1)
            return c
        lax.fori_loop(0, COMB_TM, issue, 0, unroll=DMA_UNROLL)

    dest_ref = (d0_ref, d1_ref)
    dest_next_ref = (d0_next_ref, d1_next_ref)

    def finish(slot):
        for k in range(2):
            pltpu.make_async_copy(ys_ref.at[pl.ds(0, COMB_TM * ROW_TILE), :], ybuf.at[slot, k],
                                  sems.at[slot]).wait()
        rows = slice(slot * COMB_TM, (slot + 1) * COMB_TM)
        gate = gate_ref[:, rows].T
        w0 = gate[:, 0:1]
        w1 = gate[:, 1:2]
        y0 = _load_row_tiles(ybuf.at[slot, 0], COMB_TM)
        y1 = _load_row_tiles(ybuf.at[slot, 1], COMB_TM)
        out_ref[rows, :] = x1_ref[rows, :] + (y0 * w0 + y1 * w1)

    @pl.when(g == 0)
    def _():
        gather(dest_ref, 0, 0)

    gather(dest_ref, COMB_TM, 1)
    finish(0)

    @pl.when(g + 1 < pl.num_programs(0))
    def _():
        gather(dest_next_ref, 0, 0)

    finish(1)


def _combine(x1, gates, ys, dest):
    t = x1.shape[0]
    ng = t // (2 * COMB_TM)
    d0, d1 = (dest[k].reshape(ng, 1, 2 * COMB_TM) for k in range(2))
    smem = lambda f: pl.BlockSpec((1, 1, 2 * COMB_TM), f, memory_space=pltpu.SMEM)
    cur = lambda i: (i, 0, 0)
    nxt = lambda i: (jnp.minimum(i + 1, ng - 1), 0, 0)
    return pl.pallas_call(
        _combine_kernel,
        grid=(ng,),
        in_specs=[smem(cur), smem(cur), smem(nxt), smem(nxt),
                  pl.BlockSpec((2 * COMB_TM, D_MODEL), lambda i: (i, 0)),
                  pl.BlockSpec((8, 2 * COMB_TM), lambda i: (0, i)),
                  pl.BlockSpec(memory_space=pl.ANY)],
        out_specs=pl.BlockSpec((2 * COMB_TM, D_MODEL), lambda i: (i, 0)),
        out_shape=jax.ShapeDtypeStruct((t, D_MODEL), F32),
        scratch_shapes=[pltpu.VMEM((2, 2, COMB_TM * ROW_TILE, LANES), F32),
                        pltpu.SemaphoreType.DMA((2,))],
        compiler_params=_params(("arbitrary",)),
        name="combine",
    )(d0, d1, d0, d1, x1, gates, ys)


def _layer(x, mem, attn_norm_g, mem_norm_g, w_in, w_gla_gk, b_gla_gk, gla_out_norm_g,
           moba_q_norm_g, moba_k_norm_g, w_mem_kv, mem_q_norm_g, mem_k_norm_g, w_out,
           ffn_norm_g, w_router_group, b_router_group, w_router_expert, b_router_expert,
           w_gate, w_up, w_down):
    b, s, d = x.shape
    t = b * s
    x2 = x.reshape(t, d)
    row = lambda v: v.reshape(1, -1).astype(F32)
    tile_heads = lambda v: jnp.tile(v.astype(F32), HEADS).reshape(1, QK_W)
    hid = np.arange(QK_W) // DH
    seg = jnp.asarray((hid[:, None] == hid[None, :]) / DH, BF16)

    w_in_p = jnp.concatenate([w_in[:, :1536], w_in[:, 1552:], w_in[:, 1536:1552]], axis=1).astype(BF16)
    wr = jnp.concatenate([w_router_group,
                          jnp.transpose(w_router_expert, (1, 0, 2)).reshape(d, N_EXPERTS),
                          jnp.zeros((d, LANES - N_GROUPS - N_EXPERTS), F32)], axis=1).astype(BF16)
    br = jnp.concatenate([b_router_group, b_router_expert.reshape(N_EXPERTS),
                          jnp.zeros((LANES - N_GROUPS - N_EXPERTS,), F32)]).reshape(1, LANES)

    memk, memvt = _mem_kv(mem, row(mem_norm_g), w_mem_kv.astype(BF16), tile_heads(mem_k_norm_g), seg)
    qk, gv, gr, glr, mq, mk, mvt, kmean, cq = _in_proj(
        x2, row(attn_norm_g), w_in_p, seg, tile_heads(moba_q_norm_g), tile_heads(moba_k_norm_g),
        tile_heads(mem_q_norm_g))
    o_gla = _gla(qk, gv, gr, glr, w_gla_gk.astype(BF16), row(b_gla_gk), row(gla_out_norm_g), b, s)
    o_moba, o_mem = _moba_mem(mq, mk, mvt, kmean, cq, memk, memvt, b, s)
    x1, hn, e_ids, gates, counts = _out_router(
        x2, o_gla.reshape(t, GLA_V_W), o_moba.reshape(t, QK_W), o_mem.reshape(t, QK_W),
        w_out.astype(BF16), row(ffn_norm_g), wr, br)

    counts = counts[N_GROUPS:N_GROUPS + N_EXPERTS, 0]
    padded = (counts + ROW_BLOCK - 1) // ROW_BLOCK * ROW_BLOCK
    pends = jnp.cumsum(padded)
    pstarts = pends - padded
    onehot = e_ids[:2, :, None] == jnp.arange(N_EXPERTS, dtype=jnp.int32)
    dest = (jnp.sum(jnp.where(onehot, pstarts, 0), axis=-1) + e_ids[2:4]).astype(jnp.int32)
    n_rows = (t * 2 + N_EXPERTS * (ROW_BLOCK - 1) + ROW_BLOCK - 1) // ROW_BLOCK * ROW_BLOCK
    nb = n_rows // ROW_BLOCK
    block_start = jnp.arange(nb, dtype=jnp.int32) * ROW_BLOCK
    block_expert = jnp.minimum(jnp.sum(block_start[:, None] >= pends[None, :], axis=1),
                               N_EXPERTS - 1).astype(jnp.int32)
    n_used = (pends[-1] // ROW_BLOCK).astype(jnp.int32).reshape(1)
    ended = block_start[:, None] >= pends[None, :]
    block_run = jnp.sum(ended & (counts > 0)[None, :], axis=1).astype(jnp.int32)
    block_next = (jnp.min(jnp.where(ended, n_rows, pends[None, :]), axis=1) // ROW_BLOCK).astype(jnp.int32)

    pad_lo = jnp.concatenate([pstarts + counts, pends[-1:]]).astype(jnp.int32)
    pad_hi = jnp.concatenate([pends, jnp.full((1,), n_rows)]).astype(jnp.int32)
    ys = _experts(hn, dest.reshape(-1), pad_lo, pad_hi, n_rows, block_expert, block_run, block_next, n_used,
                  w_gate, w_up, w_down)
    out = _combine(x1, gates, ys, dest)
    return out.reshape(b, s, d)


def kernel(x, mem, attn_norm_g, mem_norm_g, w_in, w_gla_gk, b_gla_gk, gla_out_norm_g, moba_q_norm_g, moba_k_norm_g, w_mem_kv, mem_q_norm_g, mem_k_norm_g, w_out, ffn_norm_g, w_router_group, b_router_group, w_router_expert, b_router_expert, w_gate, w_up, w_down):
    depth = w_in.shape[0]
    for l in range(depth):
        x = _layer(x, mem, attn_norm_g[l], mem_norm_g[l], w_in[l], w_gla_gk[l], b_gla_gk[l],
                   gla_out_norm_g[l], moba_q_norm_g[l], moba_k_norm_g[l], w_mem_kv[l],
                   mem_q_norm_g[l], mem_k_norm_g[l], w_out[l], ffn_norm_g[l], w_router_group[l],
                   b_router_group[l], w_router_expert[l], b_router_expert[l],
                   w_gate[l], w_up[l], w_down[l])
    return x
```

```python
import functools

import jax
import jax.numpy as jnp
import numpy as np
from jax import lax
from jax.experimental import pallas as pl
from jax.experimental.pallas import tpu as pltpu

F32 = jnp.float32
BF16 = jnp.bfloat16
EPS = 1e-6
NEG_INF = float("-inf")

D_MODEL = 1024
N_MEM = 256
HEADS = 4
DH = 64
GLA_DV = 128
GLA_RANK = 16
GLA_GATE_NORMALIZER = 16.0
GLA_CHUNK = 64
MOBA_BLOCK = 256
MOBA_TOPK = 3
QK_W = HEADS * DH
GLA_V_W = HEADS * GLA_DV
N_GROUPS = 4
EXPERTS_PER_GROUP = 8
N_EXPERTS = N_GROUPS * EXPERTS_PER_GROUP
MOE_FF = 512
LANES = 128
ROW_BLOCK = 256

VMEM_LIMIT = 56 * 1024 * 1024


def _params(sem):
    return pltpu.CompilerParams(dimension_semantics=sem, vmem_limit_bytes=VMEM_LIMIT)


def _nt(a, b):
    return lax.dot_general(a, b, (((1,), (1,)), ((), ())), preferred_element_type=F32)


def _tn(a, b):
    return lax.dot_general(a, b, (((0,), (0,)), ((), ())), preferred_element_type=F32)


def _dot(a, b):
    return jnp.dot(a, b, preferred_element_type=F32)


def _rms_rows(x, g):
    ms = jnp.mean(x * x, axis=-1, keepdims=True)
    return x * lax.rsqrt(ms + EPS) * g


def _split3(x):
    h1 = x.astype(BF16)
    r1 = x - h1.astype(F32)
    h2 = r1.astype(BF16)
    h3 = (r1 - h2.astype(F32)).astype(BF16)
    return h1, h2, h3


def _head_mean_sq(x, seg):
    sq = x * x
    hi = sq.astype(BF16)
    lo = (sq - hi.astype(F32)).astype(BF16)
    return _dot(hi, seg) + _dot(lo, seg)


def _silu(x):
    return x * (1.0 / (1.0 + jnp.exp(-x)))


VT_ROWS = DH + 16
VT_ALL = HEADS * VT_ROWS


ALIBI_SLOPES = tuple(2.0 ** (-8.0 * (h + 1) / HEADS) for h in range(HEADS))


def _vt_with_ones(v, key_slopes=None):
    n = v.shape[0]
    vt = v.T
    ones = jnp.ones((VT_ROWS - DH, n), F32)
    pos = lax.broadcasted_iota(jnp.int32, (1, n), 1).astype(F32)
    parts = []
    for h in range(HEADS):
        scale = 1.0 if key_slopes is None else jnp.exp(key_slopes[h] * (pos - (n - 1)))
        parts += [vt[h * DH:(h + 1) * DH] * scale, ones * scale]
    return jnp.concatenate(parts, axis=0).astype(BF16)


ROW_TILE = D_MODEL // LANES
X_TILE = ROW_TILE // 2


def _store_row_tiles(ref, x):
    n, tiles = x.shape[0], x.shape[1] // LANES
    for c in range(tiles):
        ref[pl.ds(c, n, stride=tiles), :] = x[:, c * LANES:(c + 1) * LANES]


def _load_row_tiles(ref, n):
    tiles = ref.shape[0] // n
    return jnp.concatenate([ref[pl.ds(c, n, stride=tiles), :] for c in range(tiles)], axis=1)


def _pack_bf16_pairs(x):
    w = x.shape[1] // 2
    bits = lambda t: lax.bitcast_convert_type(t.astype(BF16).astype(F32), jnp.uint32)
    return (bits(x[:, w:]) & jnp.uint32(0xFFFF0000)) | (bits(x[:, :w]) >> 16)


def _unpack_bf16_pairs(words):
    lo = lax.bitcast_convert_type(words << 16, F32).astype(BF16)
    hi = lax.bitcast_convert_type(words & jnp.uint32(0xFFFF0000), F32).astype(BF16)
    return lo, hi


def _mem_kv_kernel(mem_ref, g_ref, w_ref, gk_ref, seg_ref, k_ref, vt_ref):
    h = _rms_rows(mem_ref[0], g_ref[...]).astype(BF16)
    kv = _dot(h, w_ref[...])
    k = kv[:, :QK_W]
    kn = k * lax.rsqrt(_head_mean_sq(k, seg_ref[...]) + EPS) * gk_ref[...]
    k_ref[0] = kn.astype(BF16)
    vt_ref[0] = _vt_with_ones(kv[:, QK_W:])


def _mem_kv(mem, g, w_bf, gk_t, seg):
    b = mem.shape[0]
    return pl.pallas_call(
        _mem_kv_kernel,
        grid=(b,),
        in_specs=[
            pl.BlockSpec((1, N_MEM, D_MODEL), lambda i: (i, 0, 0)),
            pl.BlockSpec((1, D_MODEL), lambda i: (0, 0)),
            pl.BlockSpec((D_MODEL, 2 * QK_W), lambda i: (0, 0)),
            pl.BlockSpec((1, QK_W), lambda i: (0, 0)),
            pl.BlockSpec((QK_W, QK_W), lambda i: (0, 0)),
        ],
        out_specs=[
            pl.BlockSpec((1, N_MEM, QK_W), lambda i: (i, 0, 0)),
            pl.BlockSpec((1, VT_ALL, N_MEM), lambda i: (i, 0, 0)),
        ],
        out_shape=[
            jax.ShapeDtypeStruct((b, N_MEM, QK_W), BF16),
            jax.ShapeDtypeStruct((b, VT_ALL, N_MEM), BF16),
        ],
        compiler_params=_params(("arbitrary",)),
        name="mem_kv",
    )(mem, g, w_bf, gk_t, seg)


IN_TM = 1024
_C_QK, _C_V, _C_R, _C_MQ, _C_MK, _C_MV, _C_CQ, _C_LR = 0, 512, 1024, 1536, 1792, 2048, 2304, 2560
D_IN = 2576


def _in_proj_kernel(x_ref, g_ref, w_ref, seg_ref, gq_ref, gk_ref, gc_ref,
                    qk_ref, v_ref, r_ref, lr_ref, mq_ref, mk_ref, mvt_ref, kmean_ref, cq_ref):
    h = _rms_rows(x_ref[...], g_ref[...]).astype(BF16)
    p = _dot(h, w_ref[...])
    seg = seg_ref[...]
    qk_ref[...] = p[:, _C_QK:_C_V]
    v_ref[...] = p[:, _C_V:_C_R].astype(BF16)
    r_ref[...] = p[:, _C_R:_C_MQ]
    lr_ref[...] = p[:, _C_LR:D_IN]

    def head_norm(t, gain):
        return t * lax.rsqrt(_head_mean_sq(t, seg) + EPS) * gain

    scale = DH ** -0.5
    mq_ref[...] = (head_norm(p[:, _C_MQ:_C_MK], gq_ref[...]) * scale).astype(BF16)
    cq_ref[...] = (head_norm(p[:, _C_CQ:_C_LR], gc_ref[...]) * scale).astype(BF16)
    kn = head_norm(p[:, _C_MK:_C_MV], gk_ref[...])
    mk_ref[...] = kn.astype(BF16)
    mv = p[:, _C_MV:_C_CQ]
    for j in range(IN_TM // MOBA_BLOCK):
        rows = slice(j * MOBA_BLOCK, (j + 1) * MOBA_BLOCK)
        kmean_ref[0, j:j + 1, :] = jnp.mean(kn[rows], axis=0, keepdims=True)
        mvt_ref[j] = _vt_with_ones(mv[rows], ALIBI_SLOPES)


def _in_proj(x2, g, w_bf, seg, gq_t, gk_t, gc_t):
    t = x2.shape[0]
    nt = t // IN_TM
    nb = IN_TM // MOBA_BLOCK
    row = lambda w: pl.BlockSpec((IN_TM, w), lambda i: (i, 0))
    const = lambda a, b: pl.BlockSpec((a, b), lambda i: (0, 0))
    return pl.pallas_call(
        _in_proj_kernel,
        grid=(nt,),
        in_specs=[row(D_MODEL), const(1, D_MODEL), const(D_MODEL, D_IN), const(QK_W, QK_W),
                  const(1, QK_W), const(1, QK_W), const(1, QK_W)],
        out_specs=[row(2 * QK_W), row(GLA_V_W), row(GLA_V_W), row(GLA_RANK), row(QK_W), row(QK_W),
                   pl.BlockSpec((nb, VT_ALL, MOBA_BLOCK), lambda i: (i, 0, 0)),
                   pl.BlockSpec((1, nb, QK_W), lambda i: (i, 0, 0)),
                   row(QK_W)],
        out_shape=[
            jax.ShapeDtypeStruct((t, 2 * QK_W), F32),
            jax.ShapeDtypeStruct((t, GLA_V_W), BF16),
            jax.ShapeDtypeStruct((t, GLA_V_W), F32),
            jax.ShapeDtypeStruct((t, GLA_RANK), F32),
            jax.ShapeDtypeStruct((t, QK_W), BF16),
            jax.ShapeDtypeStruct((t, QK_W), BF16),
            jax.ShapeDtypeStruct((t // MOBA_BLOCK, VT_ALL, MOBA_BLOCK), BF16),
            jax.ShapeDtypeStruct((nt, nb, QK_W), F32),
            jax.ShapeDtypeStruct((t, QK_W), BF16),
        ],
        compiler_params=_params(("arbitrary",)),
        name="in_proj",
    )(x2, g, w_bf, seg, gq_t, gk_t, gc_t)


GLA_TC = 256


def _gla_kernel(qk_ref, v_ref, r_ref, lr_ref, wgk_ref, bgk_ref, gn_ref, tri_ref, ones_ref,
                o_ref, st_ref):
    @pl.when(pl.program_id(1) == 0)
    def _():
        st_ref[...] = jnp.zeros_like(st_ref)

    qk = qk_ref[0]
    q = qk[:, :QK_W]
    k = qk[:, QK_W:]
    gk = _dot(lr_ref[0].astype(BF16), wgk_ref[...]) + bgk_ref[...]
    g = -(jnp.maximum(-gk, 0.0) + jnp.log1p(jnp.exp(-jnp.abs(gk)))) / GLA_GATE_NORMALIZER
    g1, g2, g3 = _split3(g)
    tri = tri_ref[...]
    ones = ones_ref[...]
    cum = _dot(tri, g1) + _dot(tri, g2) + _dot(tri, g3)
    tot = _dot(ones, g1) + _dot(ones, g2) + _dot(ones, g3)
    q_dec = (q * (DH ** -0.5) * jnp.exp(cum)).astype(BF16)
    k_inv = (k * jnp.exp(-cum)).astype(BF16)
    k_end = (k * jnp.exp(tot - cum)).astype(BF16)
    decay = jnp.exp(tot)

    lane_head = lax.broadcasted_iota(jnp.int32, (GLA_TC, QK_W), 1) // DH
    row_t = lax.broadcasted_iota(jnp.int32, (GLA_TC, GLA_TC), 0)
    col_t = lax.broadcasted_iota(jnp.int32, (GLA_TC, GLA_TC), 1)
    causal = (row_t >= col_t) & (row_t // GLA_CHUNK == col_t // GLA_CHUNK)
    same_head = (lax.broadcasted_iota(jnp.int32, (GLA_V_W, QK_W), 0) // GLA_DV
                 == lax.broadcasted_iota(jnp.int32, (GLA_V_W, QK_W), 1) // DH)
    gain = gn_ref[...]

    v = v_ref[0]
    chunks = [slice(c * GLA_CHUNK, (c + 1) * GLA_CHUNK) for c in range(GLA_TC // GLA_CHUNK)]
    d_st = [jnp.where(same_head, _tn(v[rows], k_end[rows]), 0.0) for rows in chunks]
    states = [st_ref[...]]
    for c, rows in enumerate(chunks):
        states.append(states[c] * decay[rows.start:rows.start + 1, :] + d_st[c])
    st_ref[...] = states[-1]
    o_inter = jnp.concatenate([_nt(q_dec[rows], states[c].astype(BF16)) for c, rows in enumerate(chunks)],
                              axis=0)
    q_stack = jnp.concatenate([jnp.where(lane_head == h, q_dec, jnp.zeros_like(q_dec)) for h in range(HEADS)],
                              axis=0)
    a_all = _nt(q_stack, k_inv)
    outs = []
    for h in range(HEADS):
        a = jnp.where(causal, a_all[h * GLA_TC:(h + 1) * GLA_TC], 0.0).astype(BF16)
        vs = slice(h * GLA_DV, (h + 1) * GLA_DV)
        oh = _dot(a, v[:, vs]) + o_inter[:, vs]
        outs.append(_rms_rows(oh, gain) * _silu(r_ref[0, :, vs]))
    o_ref[0] = jnp.concatenate(outs, axis=-1).astype(BF16)


def _gla(qk, v, r, lr, wgk_bf, bgk, gn, b, s):
    ns = s // GLA_TC
    idx = np.arange(GLA_TC)
    same_chunk = (idx[:, None] // GLA_CHUNK) == (idx[None, :] // GLA_CHUNK)
    tri = jnp.asarray(same_chunk & (idx[:, None] >= idx[None, :]), BF16)
    ones = jnp.asarray(same_chunk, BF16)
    seq = lambda w: pl.BlockSpec((1, GLA_TC, w), lambda i, j: (i, j, 0))
    const = lambda a, c: pl.BlockSpec((a, c), lambda i, j: (0, 0))
    return pl.pallas_call(
        _gla_kernel,
        grid=(b, ns),
        in_specs=[seq(2 * QK_W), seq(GLA_V_W), seq(GLA_V_W), seq(GLA_RANK),
                  const(GLA_RANK, QK_W), const(1, QK_W), const(1, GLA_DV),
                  const(GLA_TC, GLA_TC), const(GLA_TC, GLA_TC)],
        out_specs=seq(GLA_V_W),
        out_shape=jax.ShapeDtypeStruct((b, s, GLA_V_W), BF16),
        scratch_shapes=[pltpu.VMEM((GLA_V_W, QK_W), F32)],
        compiler_params=_params(("arbitrary", "arbitrary")),
        name="gla",
    )(qk.reshape(b, s, 2 * QK_W), v.reshape(b, s, GLA_V_W), r.reshape(b, s, GLA_V_W),
      lr.reshape(b, s, GLA_RANK), wgk_bf, bgk, gn, tri, ones)


TQ = MOBA_BLOCK
KV_UNROLL = 8


def _moba_mem_kernel(q_ref, k_ref, vt_ref, kmean_ref, cq_ref, mk_ref, mvt_ref,
                     o_moba_ref, o_mem_ref,
                     qcat_ref, selb_ref, s_ref, *, n_blocks):
    i = pl.program_id(1)
    lane_head = lax.broadcasted_iota(jnp.int32, (TQ, QK_W), 1) // DH
    dist0 = (lax.broadcasted_iota(jnp.int32, (MOBA_BLOCK, TQ), 1)
             - lax.broadcasted_iota(jnp.int32, (MOBA_BLOCK, TQ), 0)).astype(F32)
    slopes = ALIBI_SLOPES
    heads = [slice(h * TQ, (h + 1) * TQ) for h in range(HEADS)]
    vrows = [slice(h * VT_ROWS, (h + 1) * VT_ROWS) for h in range(HEADS)]

    def stack_heads(x):
        return jnp.concatenate([jnp.where(lane_head == h, x, jnp.zeros_like(x)) for h in range(HEADS)], axis=0)

    def finish(accl):
        return accl[:DH] * (1.0 / accl[DH:DH + 1])

    s_all = _nt(mk_ref[0], stack_heads(cq_ref[0]))
    outs = []
    for h in range(HEADS):
        s = s_all[:, heads[h]]
        p = jnp.exp(s - jnp.max(s, axis=0, keepdims=True)).astype(BF16)
        outs.append(finish(_dot(mvt_ref[0, vrows[h], :], p)))
    o_mem_ref[0] = jnp.concatenate(outs, axis=0).T.astype(BF16)

    qcat = stack_heads(q_ref[0])
    qcat_ref[...] = qcat
    k_own = k_ref[0, pl.ds(pl.multiple_of(i * MOBA_BLOCK, MOBA_BLOCK), MOBA_BLOCK), :]
    gate_all = _nt(kmean_ref[0].astype(BF16), qcat)
    s_all = _nt(k_own, qcat)
    blk = lax.broadcasted_iota(jnp.int32, (n_blocks, TQ), 0)
    blk_f = blk.astype(F32)

    tq = (lax.broadcasted_iota(jnp.int32, (1, TQ), 1) - (MOBA_BLOCK - 1)).astype(F32)
    init = []
    for h in range(HEADS):
        gate = jnp.where(blk < i, gate_all[:, heads[h]], NEG_INF)
        chosen = jnp.zeros((n_blocks, TQ), jnp.bool_)
        for r in range(MOBA_TOPK):
            mx = jnp.max(gate, axis=0, keepdims=True)
            first = jnp.min(jnp.where(gate == mx, blk_f, float(n_blocks)), axis=0, keepdims=True)
            hit = blk_f == first
            chosen = chosen | (hit & (mx > NEG_INF))
            gate = jnp.where(hit, NEG_INF, gate)
        selb_ref[h] = jnp.where(chosen, 0.0, NEG_INF)
        aq = -slopes[h] * tq
        s = jnp.where(dist0 >= 0.0, s_all[:, heads[h]], NEG_INF)
        m0 = jnp.max(s, axis=0, keepdims=True) + aq
        p = jnp.exp(s - (m0 - aq)).astype(BF16)
        init += [m0, _dot(vt_ref[i, vrows[h], :], p)]

    def stage_scores(j, slot):
        kj = k_ref[0, pl.ds(pl.multiple_of(j * MOBA_BLOCK, MOBA_BLOCK), MOBA_BLOCK), :]
        s_ref[slot] = _nt(kj, qcat_ref[...])

    stage_scores(0, 0)

    def past_blocks(g, carry, unroll, base):
        carry = list(carry)
        for u in range(unroll):
            j = base + g * unroll + u
            stage_scores(jnp.minimum(j + 1, n_blocks - 1), (u + 1) % 2)
            off = jnp.full((1, TQ), (i - j) * MOBA_BLOCK, jnp.int32).astype(F32) + tq
            for h in range(HEADS):
                m, accl = carry[2 * h:2 * h + 2]
                s1 = s_ref[u % 2, :, heads[h]]
                rb = selb_ref[h, pl.ds(j, 1), :] - slopes[h] * off
                m_new = jnp.maximum(m, jnp.max(s1, axis=0, keepdims=True) + rb)
                p = jnp.exp(s1 - (m_new - rb)).astype(BF16)
                carry[2 * h:2 * h + 2] = [m_new, jnp.exp(m - m_new) * accl + _dot(vt_ref[j, vrows[h], :], p)]
        return tuple(carry)

    n_main = i // KV_UNROLL
    rest = i - n_main * KV_UNROLL
    carry = lax.fori_loop(0, n_main, functools.partial(past_blocks, unroll=KV_UNROLL, base=0), tuple(init))
    final = lax.fori_loop(0, (rest + 1) // 2,
                          functools.partial(past_blocks, unroll=2, base=n_main * KV_UNROLL), carry)
    o_t = jnp.concatenate([finish(final[2 * h + 1]) for h in range(HEADS)], axis=0)
    o_moba_ref[0] = o_t.T.astype(BF16)


def _moba_mem(mq, mk, mvt, kmean, cq, memk, memvt, b, s):
    nq = s // TQ
    n_blocks = s // MOBA_BLOCK
    qspec = pl.BlockSpec((1, TQ, QK_W), lambda i, j: (i, j, 0))
    return pl.pallas_call(
        functools.partial(_moba_mem_kernel, n_blocks=n_blocks),
        grid=(b, nq),
        in_specs=[
            qspec,
            pl.BlockSpec((1, s, QK_W), lambda i, j: (i, 0, 0)),
            pl.BlockSpec((n_blocks, VT_ALL, MOBA_BLOCK), lambda i, j: (i, 0, 0)),
            pl.BlockSpec((1, n_blocks, QK_W), lambda i, j: (i, 0, 0)),
            qspec,
            pl.BlockSpec((1, N_MEM, QK_W), lambda i, j: (i, 0, 0)),
            pl.BlockSpec((1, VT_ALL, N_MEM), lambda i, j: (i, 0, 0)),
        ],
        out_specs=[qspec, qspec],
        out_shape=[jax.ShapeDtypeStruct((b, s, QK_W), BF16), jax.ShapeDtypeStruct((b, s, QK_W), BF16)],
        scratch_shapes=[
            pltpu.VMEM((HEADS * TQ, QK_W), BF16),
            pltpu.VMEM((HEADS, n_blocks, TQ), F32),
            pltpu.VMEM((2, MOBA_BLOCK, HEADS * TQ), F32),
        ],
        compiler_params=_params(("arbitrary", "arbitrary")),
        name="moba_mem",
    )(mq.reshape(b, s, QK_W), mk.reshape(b, s, QK_W), mvt, kmean.reshape(b, n_blocks, QK_W),
      cq.reshape(b, s, QK_W), memk, memvt)


OUT_TM = 1024
ROUTE_ROWS = 40


def _out_router_kernel(x_ref, og_ref, om_ref, oc_ref, w_ref, g_ref, wr_ref, br_ref, tri_ref,
                       x1_ref, h_ref, e_ref, gate_ref, count_ref, carry_ref):
    w = w_ref[...]
    x1 = (x_ref[...] + _dot(og_ref[...], w[:GLA_V_W]) + _dot(om_ref[...], w[GLA_V_W:GLA_V_W + QK_W])
          + _dot(oc_ref[...], w[GLA_V_W + QK_W:]))
    x1_ref[...] = x1
    hn = _rms_rows(x1, g_ref[...])
    _store_row_tiles(h_ref, _pack_bf16_pairs(hn))
    logits = _dot(hn.astype(BF16), wr_ref[...]) + br_ref[...]
    lt = logits.T[:ROUTE_ROWS]
    row = lax.broadcasted_iota(jnp.int32, lt.shape, 0).astype(F32)
    lg = jnp.where(row < N_GROUPS, lt, NEG_INF)
    mg = jnp.max(lg, axis=0, keepdims=True)
    g_sel = jnp.min(jnp.where(lg == mg, row, float(LANES)), axis=0, keepdims=True)
    p_group = 1.0 / jnp.sum(jnp.exp(lg - mg), axis=0, keepdims=True)
    lo = N_GROUPS + g_sel * EXPERTS_PER_GROUP
    le = jnp.where((row >= lo) & (row < lo + EXPERTS_PER_GROUP), lt, NEG_INF)
    m0 = jnp.max(le, axis=0, keepdims=True)
    i0 = jnp.min(jnp.where(le == m0, row, float(LANES)), axis=0, keepdims=True)
    le1 = jnp.where(row == i0, NEG_INF, le)
    m1 = jnp.max(le1, axis=0, keepdims=True)
    i1 = jnp.min(jnp.where(le1 == m1, row, float(LANES)), axis=0, keepdims=True)
    z = jnp.exp(m1 - m0)
    w0 = p_group / (1.0 + z)
    w1 = p_group * z / (1.0 + z)
    pad = jnp.zeros((8 - 2, lt.shape[1]), F32)
    gate_ref[...] = jnp.concatenate([w0, w1, pad], axis=0)

    @pl.when(pl.program_id(0) == 0)
    def _():
        carry_ref[...] = jnp.zeros_like(carry_ref)

    oh0 = (row == i0).astype(F32)
    oh1 = (row == i1).astype(F32)
    both = oh0 + oh1
    before = _dot(both.astype(BF16), tri_ref[...]) + carry_ref[:, 0:1]
    r0 = jnp.sum(oh0 * before, axis=0, keepdims=True)
    r1 = jnp.sum(oh1 * before, axis=0, keepdims=True)
    carry_ref[...] = carry_ref[...] + jnp.sum(both, axis=1, keepdims=True)
    count_ref[...] = carry_ref[...].astype(jnp.int32)
    e_ref[...] = jnp.concatenate([i0 - N_GROUPS, i1 - N_GROUPS, r0, r1, pad[:4]], axis=0).astype(jnp.int32)


def _out_router(x2, og, om, oc, w_bf, g, wr_bf, br):
    t = x2.shape[0]
    row = lambda w: pl.BlockSpec((OUT_TM, w), lambda i: (i, 0))
    const = lambda a, b: pl.BlockSpec((a, b), lambda i: (0, 0))
    idx = np.arange(OUT_TM)
    tri = jnp.asarray(idx[:, None] < idx[None, :], BF16)
    col = lambda r: pl.BlockSpec((r, OUT_TM), lambda i: (0, i))
    return pl.pallas_call(
        _out_router_kernel,
        grid=(t // OUT_TM,),
        in_specs=[row(D_MODEL), row(GLA_V_W), row(QK_W), row(QK_W), const(D_MODEL, D_MODEL),
                  const(1, D_MODEL), const(D_MODEL, LANES), const(1, LANES), const(OUT_TM, OUT_TM)],
        out_specs=[row(D_MODEL), pl.BlockSpec((OUT_TM * X_TILE, LANES), lambda i: (i, 0)),
                   col(8), col(8), const(ROUTE_ROWS, LANES)],
        out_shape=[
            jax.ShapeDtypeStruct((t, D_MODEL), F32),
            jax.ShapeDtypeStruct((t * X_TILE, LANES), jnp.uint32),
            jax.ShapeDtypeStruct((8, t), jnp.int32),
            jax.ShapeDtypeStruct((8, t), F32),
            jax.ShapeDtypeStruct((ROUTE_ROWS, LANES), jnp.int32),
        ],
        scratch_shapes=[pltpu.VMEM((ROUTE_ROWS, LANES), F32)],
        compiler_params=_params(("arbitrary",)),
        name="out_router",
    )(x2, og, om, oc, w_bf, g, wr_bf, br, tri)


DMA_UNROLL = 8
PLACE_UNROLL = 32


def _row_to_token(dest_ref, pad_lo_ref, pad_hi_ref, tok_ref):
    def pad_segment(e, c):
        hi = pad_hi_ref[e]

        def pad(g, c2):
            for r in range(8):
                tok_ref[hi - 8 * (g + 1) + r] = 0
            return c2
        lax.fori_loop(0, lax.shift_right_logical(hi - pad_lo_ref[e] + 7, 3), pad, 0)
        return c

    lax.fori_loop(0, pad_lo_ref.shape[0], pad_segment, 0)

    n_tok = dest_ref.shape[0] // 2

    def place(t, c):
        tok_ref[dest_ref[t]] = t
        tok_ref[dest_ref[n_tok + t]] = t
        return c

    lax.fori_loop(0, n_tok, place, 0, unroll=PLACE_UNROLL)


def _experts_kernel(be_ref, run_ref, next_ref, nused_ref, dest_ref, pad_lo_ref, pad_hi_ref,
                    hn_hbm, wg_hbm, wu_hbm, wd_hbm, ys_hbm,
                    hn_vmem, xg_even, xg_odd, y_even, y_odd, wg_f, wu_f, wd_f, wg_bf, wu_bf, wd_bf, tok_ref,
                    hn_sem, w_sem, y_sem):
    n_used = nused_ref[0]
    n_blocks = be_ref.shape[0]
    block_rows = ROW_BLOCK * ROW_TILE

    def weight_copies(e, slot):
        return [pltpu.make_async_copy(src.at[e], dst.at[slot], w_sem.at[slot])
                for src, dst in ((wg_hbm, wg_f), (wu_hbm, wu_f), (wd_hbm, wd_f))]

    def y_copy(buf, parity, i):
        dst = ys_hbm.at[pl.ds(pl.multiple_of(i * block_rows, block_rows), block_rows), :]
        return pltpu.make_async_copy(buf, dst, y_sem.at[parity])

    def gather(i, dst):
        base = jnp.minimum(i, n_blocks - 1) * ROW_BLOCK
        for r in range(ROW_BLOCK):
            src = pl.ds(pl.multiple_of(tok_ref[base + r] * X_TILE, X_TILE), X_TILE)
            dst[r * X_TILE:(r + 1) * X_TILE, :] = hn_vmem[src, :]

    def block(i, parity, cur, nxt, ybuf):
        @pl.when(jnp.logical_or(i == 0, be_ref[i] != be_ref[jnp.maximum(i - 1, 0)]))
        def _():
            slot = jnp.bitwise_and(run_ref[i], 1)
            for c in weight_copies(be_ref[i], slot):
                c.wait()
            wg_bf[...] = wg_f[slot].astype(BF16)
            wu_bf[...] = wu_f[slot].astype(BF16)
            wd_bf[...] = wd_f[slot].astype(BF16)

            @pl.when(next_ref[i] < n_used)
            def _():
                for c in weight_copies(be_ref[jnp.minimum(next_ref[i], n_blocks - 1)], 1 - slot):
                    c.start()

        @pl.when(i >= 2)
        def _():
            y_copy(ybuf, parity, i).wait()

        kh = D_MODEL // 2
        gather(i + 1, nxt)
        x_lo, x_hi = _unpack_bf16_pairs(_load_row_tiles(cur, ROW_BLOCK))
        gate = _dot(x_lo, wg_bf[:kh, :]) + _dot(x_hi, wg_bf[kh:, :])
        up = _dot(x_lo, wu_bf[:kh, :]) + _dot(x_hi, wu_bf[kh:, :])
        hid = (_silu(gate) * up).astype(BF16)
        _store_row_tiles(ybuf, _dot(hid, wd_bf[...]))
        y_copy(ybuf, parity, i).start()

    hn_load = pltpu.make_async_copy(hn_hbm, hn_vmem, hn_sem)
    hn_load.start()
    for c in weight_copies(be_ref[0], 0):
        c.start()
    _row_to_token(dest_ref, pad_lo_ref, pad_hi_ref, tok_ref)
    hn_load.wait()
    gather(0, xg_even)

    def pair(p, carry):
        block(2 * p, 0, xg_even, xg_odd, y_even)

        @pl.when(2 * p + 1 < n_used)
        def _():
            block(2 * p + 1, 1, xg_odd, xg_even, y_odd)
        return carry

    lax.fori_loop(0, lax.shift_right_logical(n_used + 1, 1), pair, 0)

    y_copy(y_even, 0, 0).wait()

    @pl.when(n_used >= 2)
    def _():
        y_copy(y_odd, 1, 0).wait()

    y_even[...] = jnp.zeros_like(y_even)

    def zero_start(i, carry):
        y_copy(y_even, 0, i).start()
        return carry

    def zero_wait(i, carry):
        y_copy(y_even, 0, i).wait()
        return carry

    lax.fori_loop(n_used, n_blocks, zero_start, 0)
    lax.fori_loop(n_used, n_blocks, zero_wait, 0)


def _experts(hn, dest, pad_lo, pad_hi, n_rows, block_expert, block_run, block_next, n_used,
             w_gate, w_up, w_down):
    hbm = pl.BlockSpec(memory_space=pl.ANY)
    grid_spec = pltpu.PrefetchScalarGridSpec(
        num_scalar_prefetch=7,
        grid=(1,),
        in_specs=[hbm, hbm, hbm, hbm],
        out_specs=hbm,
        scratch_shapes=[pltpu.VMEM(hn.shape, jnp.uint32),
                        pltpu.VMEM((ROW_BLOCK * X_TILE, LANES), jnp.uint32),
                        pltpu.VMEM((ROW_BLOCK * X_TILE, LANES), jnp.uint32),
                        pltpu.VMEM((ROW_BLOCK * ROW_TILE, LANES), F32),
                        pltpu.VMEM((ROW_BLOCK * ROW_TILE, LANES), F32),
                        pltpu.VMEM((2, D_MODEL, MOE_FF), F32), pltpu.VMEM((2, D_MODEL, MOE_FF), F32),
                        pltpu.VMEM((2, MOE_FF, D_MODEL), F32),
                        pltpu.VMEM((D_MODEL, MOE_FF), BF16), pltpu.VMEM((D_MODEL, MOE_FF), BF16),
                        pltpu.VMEM((MOE_FF, D_MODEL), BF16),
                        pltpu.SMEM((n_rows,), jnp.int32),
                        pltpu.SemaphoreType.DMA, pltpu.SemaphoreType.DMA((2,)), pltpu.SemaphoreType.DMA((2,))],
    )
    return pl.pallas_call(
        _experts_kernel,
        grid_spec=grid_spec,
        out_shape=jax.ShapeDtypeStruct((n_rows * ROW_TILE, LANES), F32),
        compiler_params=_params(("arbitrary",)),
        name="experts",
    )(block_expert, block_run, block_next, n_used, dest, pad_lo, pad_hi, hn, w_gate, w_up, w_down)


COMB_TM = 256


def _combine_kernel(d0_ref, d1_ref, d0_next_ref, d1_next_ref, x1_ref, gate_ref, ys_ref, out_ref, ybuf, sems):
    g = pl.program_id(0)

    def row_copy(d, slot, k, r):
        src = ys_ref.at[pl.ds(pl.multiple_of(d * ROW_TILE, ROW_TILE), ROW_TILE), :]
        dst = ybuf.at[slot, k, pl.ds(pl.multiple_of(r * ROW_TILE, ROW_TILE), ROW_TILE), :]
        return pltpu.make_async_copy(src, dst, sems.at[slot])

    def gather(drefs, base, slot):
        def issue(r, c):
            row_copy(drefs[0][0, 0, base + r], slot, 0, r).start(priority=0)
            row_copy(drefs[1][0, 0, base + r], slot, 1, r).start(priority=1)
            return c
        lax.fori_loop(0, COMB_TM, issue, 0, unroll=DMA_UNROLL)

    dest_ref = (d0_ref, d1_ref)
    dest_next_ref = (d0_next_ref, d1_next_ref)

    def finish(slot):
        for k in range(2):
            pltpu.make_async_copy(ys_ref.at[pl.ds(0, COMB_TM * ROW_TILE), :], ybuf.at[slot, k],
                                  sems.at[slot]).wait()
        rows = slice(slot * COMB_TM, (slot + 1) * COMB_TM)
        gate = gate_ref[:, rows].T
        w0 = gate[:, 0:1]
        w1 = gate[:, 1:2]
        y0 = _load_row_tiles(ybuf.at[slot, 0], COMB_TM)
        y1 = _load_row_tiles(ybuf.at[slot, 1], COMB_TM)
        out_ref[rows, :] = x1_ref[rows, :] + (y0 * w0 + y1 * w1)

    @pl.when(g == 0)
    def _():
        gather(dest_ref, 0, 0)

    gather(dest_ref, COMB_TM, 1)
    finish(0)

    @pl.when(g + 1 < pl.num_programs(0))
    def _():
        gather(dest_next_ref, 0, 0)

    finish(1)


def _combine(x1, gates, ys, dest):
    t = x1.shape[0]
    ng = t // (2 * COMB_TM)
    d0, d1 = (dest[k].reshape(ng, 1, 2 * COMB_TM) for k in range(2))
    smem = lambda f: pl.BlockSpec((1, 1, 2 * COMB_TM), f, memory_space=pltpu.SMEM)
    cur = lambda i: (i, 0, 0)
    nxt = lambda i: (jnp.minimum(i + 1, ng - 1), 0, 0)
    return pl.pallas_call(
        _combine_kernel,
        grid=(ng,),
        in_specs=[smem(cur), smem(cur), smem(nxt), smem(nxt),
                  pl.BlockSpec((2 * COMB_TM, D_MODEL), lambda i: (i, 0)),
                  pl.BlockSpec((8, 2 * COMB_TM), lambda i: (0, i)),
                  pl.BlockSpec(memory_space=pl.ANY)],
        out_specs=pl.BlockSpec((2 * COMB_TM, D_MODEL), lambda i: (i, 0)),
        out_shape=jax.ShapeDtypeStruct((t, D_MODEL), F32),
        scratch_shapes=[pltpu.VMEM((2, 2, COMB_TM * ROW_TILE, LANES), F32),
                        pltpu.SemaphoreType.DMA((2,))],
        compiler_params=_params(("arbitrary",)),
        name="combine",
    )(d0, d1, d0, d1, x1, gates, ys)


def _layer(x, mem, attn_norm_g, mem_norm_g, w_in, w_gla_gk, b_gla_gk, gla_out_norm_g,
           moba_q_norm_g, moba_k_norm_g, w_mem_kv, mem_q_norm_g, mem_k_norm_g, w_out,
           ffn_norm_g, w_router_group, b_router_group, w_router_expert, b_router_expert,
           w_gate, w_up, w_down):
    b, s, d = x.shape
    t = b * s
    x2 = x.reshape(t, d)
    row = lambda v: v.reshape(1, -1).astype(F32)
    tile_heads = lambda v: jnp.tile(v.astype(F32), HEADS).reshape(1, QK_W)
    hid = np.arange(QK_W) // DH
    seg = jnp.asarray((hid[:, None] == hid[None, :]) / DH, BF16)

    w_in_p = jnp.concatenate([w_in[:, :1536], w_in[:, 1552:], w_in[:, 1536:1552]], axis=1).astype(BF16)
    wr = jnp.concatenate([w_router_group,
                          jnp.transpose(w_router_expert, (1, 0, 2)).reshape(d, N_EXPERTS),
                          jnp.zeros((d, LANES - N_GROUPS - N_EXPERTS), F32)], axis=1).astype(BF16)
    br = jnp.concatenate([b_router_group, b_router_expert.reshape(N_EXPERTS),
                          jnp.zeros((LANES - N_GROUPS - N_EXPERTS,), F32)]).reshape(1, LANES)

    memk, memvt = _mem_kv(mem, row(mem_norm_g), w_mem_kv.astype(BF16), tile_heads(mem_k_norm_g), seg)
    qk, gv, gr, glr, mq, mk, mvt, kmean, cq = _in_proj(
        x2, row(attn_norm_g), w_in_p, seg, tile_heads(moba_q_norm_g), tile_heads(moba_k_norm_g),
        tile_heads(mem_q_norm_g))
    o_gla = _gla(qk, gv, gr, glr, w_gla_gk.astype(BF16), row(b_gla_gk), row(gla_out_norm_g), b, s)
    o_moba, o_mem = _moba_mem(mq, mk, mvt, kmean, cq, memk, memvt, b, s)
    x1, hn, e_ids, gates, counts = _out_router(
        x2, o_gla.reshape(t, GLA_V_W), o_moba.reshape(t, QK_W), o_mem.reshape(t, QK_W),
        w_out.astype(BF16), row(ffn_norm_g), wr, br)

    counts = counts[N_GROUPS:N_GROUPS + N_EXPERTS, 0]
    padded = (counts + ROW_BLOCK - 1) // ROW_BLOCK * ROW_BLOCK
    pends = jnp.cumsum(padded)
    pstarts = pends - padded
    onehot = e_ids[:2, :, None] == jnp.arange(N_EXPERTS, dtype=jnp.int32)
    dest = (jnp.sum(jnp.where(onehot, pstarts, 0), axis=-1) + e_ids[2:4]).astype(jnp.int32)
    n_rows = (t * 2 + N_EXPERTS * (ROW_BLOCK - 1) + ROW_BLOCK - 1) // ROW_BLOCK * ROW_BLOCK
    nb = n_rows // ROW_BLOCK
    block_start = jnp.arange(nb, dtype=jnp.int32) * ROW_BLOCK
    block_expert = jnp.minimum(jnp.sum(block_start[:, None] >= pends[None, :], axis=1),
                               N_EXPERTS - 1).astype(jnp.int32)
    n_used = (pends[-1] // ROW_BLOCK).astype(jnp.int32).reshape(1)
    ended = block_start[:, None] >= pends[None, :]
    block_run = jnp.sum(ended & (counts > 0)[None, :], axis=1).astype(jnp.int32)
    block_next = (jnp.min(jnp.where(ended, n_rows, pends[None, :]), axis=1) // ROW_BLOCK).astype(jnp.int32)

    pad_lo = jnp.concatenate([pstarts + counts, pends[-1:]]).astype(jnp.int32)
    pad_hi = jnp.concatenate([pends, jnp.full((1,), n_rows)]).astype(jnp.int32)
    ys = _experts(hn, dest.reshape(-1), pad_lo, pad_hi, n_rows, block_expert, block_run, block_next, n_used,
                  w_gate, w_up, w_down)
    out = _combine(x1, gates, ys, dest)
    return out.reshape(b, s, d)


def kernel(x, mem, attn_norm_g, mem_norm_g, w_in, w_gla_gk, b_gla_gk, gla_out_norm_g, moba_q_norm_g, moba_k_norm_g, w_mem_kv, mem_q_norm_g, mem_k_norm_g, w_out, ffn_norm_g, w_router_group, b_router_group, w_router_expert, b_router_expert, w_gate, w_up, w_down):
    depth = w_in.shape[0]
    for l in range(depth):
        x = _layer(x, mem, attn_norm_g[l], mem_norm_g[l], w_in[l], w_gla_gk[l], b_gla_gk[l],
                   gla_out_norm_g[l], moba_q_norm_g[l], moba_k_norm_g[l], w_mem_kv[l],
                   mem_q_norm_g[l], mem_k_norm_g[l], w_out[l], ffn_norm_g[l], w_router_group[l],
                   b_router_group[l], w_router_expert[l], b_router_expert[l],
                   w_gate[l], w_up[l], w_down[l])
    return x
```

```python
import functools

import jax
import jax.numpy as jnp
import numpy as np
from jax import lax
from jax.experimental import pallas as pl
from jax.experimental.pallas import tpu as pltpu

F32 = jnp.float32
BF16 = jnp.bfloat16
EPS = 1e-6
NEG_INF = float("-inf")

D_MODEL = 1024
N_MEM = 256
HEADS = 4
DH = 64
GLA_DV = 128
GLA_RANK = 16
GLA_GATE_NORMALIZER = 16.0
GLA_CHUNK = 64
MOBA_BLOCK = 256
MOBA_TOPK = 3
QK_W = HEADS * DH
GLA_V_W = HEADS * GLA_DV
N_GROUPS = 4
EXPERTS_PER_GROUP = 8
N_EXPERTS = N_GROUPS * EXPERTS_PER_GROUP
MOE_FF = 512
LANES = 128
ROW_BLOCK = 256

VMEM_LIMIT = 56 * 1024 * 1024


def _params(sem):
    return pltpu.CompilerParams(dimension_semantics=sem, vmem_limit_bytes=VMEM_LIMIT)


def _nt(a, b):
    return lax.dot_general(a, b, (((1,), (1,)), ((), ())), preferred_element_type=F32)


def _tn(a, b):
    return lax.dot_general(a, b, (((0,), (0,)), ((), ())), preferred_element_type=F32)


def _dot(a, b):
    return jnp.dot(a, b, preferred_element_type=F32)


def _rms_rows(x, g):
    ms = jnp.mean(x * x, axis=-1, keepdims=True)
    return x * lax.rsqrt(ms + EPS) * g


def _split3(x):
    h1 = x.astype(BF16)
    r1 = x - h1.astype(F32)
    h2 = r1.astype(BF16)
    h3 = (r1 - h2.astype(F32)).astype(BF16)
    return h1, h2, h3


def _head_mean_sq(x, seg):
    sq = x * x
    hi = sq.astype(BF16)
    lo = (sq - hi.astype(F32)).astype(BF16)
    return _dot(hi, seg) + _dot(lo, seg)


def _silu(x):
    return x * (1.0 / (1.0 + jnp.exp(-x)))


VT_ROWS = DH + 16
VT_ALL = HEADS * VT_ROWS


ALIBI_SLOPES = tuple(2.0 ** (-8.0 * (h + 1) / HEADS) for h in range(HEADS))


def _vt_with_ones(v, key_slopes=None):
    n = v.shape[0]
    vt = v.T
    ones = jnp.ones((VT_ROWS - DH, n), F32)
    pos = lax.broadcasted_iota(jnp.int32, (1, n), 1).astype(F32)
    parts = []
    for h in range(HEADS):
        scale = 1.0 if key_slopes is None else jnp.exp(key_slopes[h] * (pos - (n - 1)))
        parts += [vt[h * DH:(h + 1) * DH] * scale, ones * scale]
    return jnp.concatenate(parts, axis=0).astype(BF16)


ROW_TILE = D_MODEL // LANES
X_TILE = ROW_TILE // 2


def _store_row_tiles(ref, x):
    n, tiles = x.shape[0], x.shape[1] // LANES
    for c in range(tiles):
        ref[pl.ds(c, n, stride=tiles), :] = x[:, c * LANES:(c + 1) * LANES]


def _load_row_tiles(ref, n):
    tiles = ref.shape[0] // n
    return jnp.concatenate([ref[pl.ds(c, n, stride=tiles), :] for c in range(tiles)], axis=1)


def _pack_bf16_pairs(x):
    w = x.shape[1] // 2
    bits = lambda t: lax.bitcast_convert_type(t.astype(BF16).astype(F32), jnp.uint32)
    return (bits(x[:, w:]) & jnp.uint32(0xFFFF0000)) | (bits(x[:, :w]) >> 16)


def _unpack_bf16_pairs(words):
    lo = lax.bitcast_convert_type(words << 16, F32).astype(BF16)
    hi = lax.bitcast_convert_type(words & jnp.uint32(0xFFFF0000), F32).astype(BF16)
    return lo, hi


def _mem_kv_kernel(mem_ref, g_ref, w_ref, gk_ref, seg_ref, k_ref, vt_ref):
    h = _rms_rows(mem_ref[0], g_ref[...]).astype(BF16)
    kv = _dot(h, w_ref[...])
    k = kv[:, :QK_W]
    kn = k * lax.rsqrt(_head_mean_sq(k, seg_ref[...]) + EPS) * gk_ref[...]
    k_ref[0] = kn.astype(BF16)
    vt_ref[0] = _vt_with_ones(kv[:, QK_W:])


def _mem_kv(mem, g, w_bf, gk_t, seg):
    b = mem.shape[0]
    return pl.pallas_call(
        _mem_kv_kernel,
        grid=(b,),
        in_specs=[
            pl.BlockSpec((1, N_MEM, D_MODEL), lambda i: (i, 0, 0)),
            pl.BlockSpec((1, D_MODEL), lambda i: (0, 0)),
            pl.BlockSpec((D_MODEL, 2 * QK_W), lambda i: (0, 0)),
            pl.BlockSpec((1, QK_W), lambda i: (0, 0)),
            pl.BlockSpec((QK_W, QK_W), lambda i: (0, 0)),
        ],
        out_specs=[
            pl.BlockSpec((1, N_MEM, QK_W), lambda i: (i, 0, 0)),
            pl.BlockSpec((1, VT_ALL, N_MEM), lambda i: (i, 0, 0)),
        ],
        out_shape=[
            jax.ShapeDtypeStruct((b, N_MEM, QK_W), BF16),
            jax.ShapeDtypeStruct((b, VT_ALL, N_MEM), BF16),
        ],
        compiler_params=_params(("arbitrary",)),
        name="mem_kv",
    )(mem, g, w_bf, gk_t, seg)


IN_TM = 1024
_C_QK, _C_V, _C_R, _C_MQ, _C_MK, _C_MV, _C_CQ, _C_LR = 0, 512, 1024, 1536, 1792, 2048, 2304, 2560
D_IN = 2576


def _in_proj_kernel(x_ref, g_ref, w_ref, seg_ref, gq_ref, gk_ref, gc_ref,
                    qk_ref, v_ref, r_ref, lr_ref, mq_ref, mk_ref, mvt_ref, kmean_ref, cq_ref):
    h = _rms_rows(x_ref[...], g_ref[...]).astype(BF16)
    p = _dot(h, w_ref[...])
    seg = seg_ref[...]
    qk_ref[...] = p[:, _C_QK:_C_V]
    v_ref[...] = p[:, _C_V:_C_R].astype(BF16)
    r_ref[...] = p[:, _C_R:_C_MQ]
    lr_ref[...] = p[:, _C_LR:D_IN]

    def head_norm(t, gain):
        return t * lax.rsqrt(_head_mean_sq(t, seg) + EPS) * gain

    scale = DH ** -0.5
    mq_ref[...] = (head_norm(p[:, _C_MQ:_C_MK], gq_ref[...]) * scale).astype(BF16)
    cq_ref[...] = (head_norm(p[:, _C_CQ:_C_LR], gc_ref[...]) * scale).astype(BF16)
    kn = head_norm(p[:, _C_MK:_C_MV], gk_ref[...])
    mk_ref[...] = kn.astype(BF16)
    mv = p[:, _C_MV:_C_CQ]
    for j in range(IN_TM // MOBA_BLOCK):
        rows = slice(j * MOBA_BLOCK, (j + 1) * MOBA_BLOCK)
        kmean_ref[0, j:j + 1, :] = jnp.mean(kn[rows], axis=0, keepdims=True)
        mvt_ref[j] = _vt_with_ones(mv[rows], ALIBI_SLOPES)


def _in_proj(x2, g, w_bf, seg, gq_t, gk_t, gc_t):
    t = x2.shape[0]
    nt = t // IN_TM
    nb = IN_TM // MOBA_BLOCK
    row = lambda w: pl.BlockSpec((IN_TM, w), lambda i: (i, 0))
    const = lambda a, b: pl.BlockSpec((a, b), lambda i: (0, 0))
    return pl.pallas_call(
        _in_proj_kernel,
        grid=(nt,),
        in_specs=[row(D_MODEL), const(1, D_MODEL), const(D_MODEL, D_IN), const(QK_W, QK_W),
                  const(1, QK_W), const(1, QK_W), const(1, QK_W)],
        out_specs=[row(2 * QK_W), row(GLA_V_W), row(GLA_V_W), row(GLA_RANK), row(QK_W), row(QK_W),
                   pl.BlockSpec((nb, VT_ALL, MOBA_BLOCK), lambda i: (i, 0, 0)),
                   pl.BlockSpec((1, nb, QK_W), lambda i: (i, 0, 0)),
                   row(QK_W)],
        out_shape=[
            jax.ShapeDtypeStruct((t, 2 * QK_W), F32),
            jax.ShapeDtypeStruct((t, GLA_V_W), BF16),
            jax.ShapeDtypeStruct((t, GLA_V_W), F32),
            jax.ShapeDtypeStruct((t, GLA_RANK), F32),
            jax.ShapeDtypeStruct((t, QK_W), BF16),
            jax.ShapeDtypeStruct((t, QK_W), BF16),
            jax.ShapeDtypeStruct((t // MOBA_BLOCK, VT_ALL, MOBA_BLOCK), BF16),
            jax.ShapeDtypeStruct((nt, nb, QK_W), F32),
            jax.ShapeDtypeStruct((t, QK_W), BF16),
        ],
        compiler_params=_params(("arbitrary",)),
        name="in_proj",
    )(x2, g, w_bf, seg, gq_t, gk_t, gc_t)


GLA_TC = 256


def _gla_kernel(qk_ref, v_ref, r_ref, lr_ref, wgk_ref, bgk_ref, gn_ref, tri_ref, ones_ref,
                o_ref, st_ref):
    @pl.when(pl.program_id(1) == 0)
    def _():
        st_ref[...] = jnp.zeros_like(st_ref)

    qk = qk_ref[0]
    q = qk[:, :QK_W]
    k = qk[:, QK_W:]
    gk = _dot(lr_ref[0].astype(BF16), wgk_ref[...]) + bgk_ref[...]
    g = -(jnp.maximum(-gk, 0.0) + jnp.log1p(jnp.exp(-jnp.abs(gk)))) / GLA_GATE_NORMALIZER
    g1, g2, g3 = _split3(g)
    tri = tri_ref[...]
    ones = ones_ref[...]
    cum = _dot(tri, g1) + _dot(tri, g2) + _dot(tri, g3)
    tot = _dot(ones, g1) + _dot(ones, g2) + _dot(ones, g3)
    q_dec = (q * (DH ** -0.5) * jnp.exp(cum)).astype(BF16)
    k_inv = (k * jnp.exp(-cum)).astype(BF16)
    k_end = (k * jnp.exp(tot - cum)).astype(BF16)
    decay = jnp.exp(tot)

    lane_head = lax.broadcasted_iota(jnp.int32, (GLA_TC, QK_W), 1) // DH
    row_t = lax.broadcasted_iota(jnp.int32, (GLA_TC, GLA_TC), 0)
    col_t = lax.broadcasted_iota(jnp.int32, (GLA_TC, GLA_TC), 1)
    causal = (row_t >= col_t) & (row_t // GLA_CHUNK == col_t // GLA_CHUNK)
    same_head = (lax.broadcasted_iota(jnp.int32, (GLA_V_W, QK_W), 0) // GLA_DV
                 == lax.broadcasted_iota(jnp.int32, (GLA_V_W, QK_W), 1) // DH)
    gain = gn_ref[...]

    v = v_ref[0]
    chunks = [slice(c * GLA_CHUNK, (c + 1) * GLA_CHUNK) for c in range(GLA_TC // GLA_CHUNK)]
    d_st = [jnp.where(same_head, _tn(v[rows], k_end[rows]), 0.0) for rows in chunks]
    states = [st_ref[...]]
    for c, rows in enumerate(chunks):
        states.append(states[c] * decay[rows.start:rows.start + 1, :] + d_st[c])
    st_ref[...] = states[-1]
    o_inter = jnp.concatenate([_nt(q_dec[rows], states[c].astype(BF16)) for c, rows in enumerate(chunks)],
                              axis=0)
    q_stack = jnp.concatenate([jnp.where(lane_head == h, q_dec, jnp.zeros_like(q_dec)) for h in range(HEADS)],
                              axis=0)
    a_all = _nt(q_stack, k_inv)
    outs = []
    for h in range(HEADS):
        a = jnp.where(causal, a_all[h * GLA_TC:(h + 1) * GLA_TC], 0.0).astype(BF16)
        vs = slice(h * GLA_DV, (h + 1) * GLA_DV)
        oh = _dot(a, v[:, vs]) + o_inter[:, vs]
        outs.append(_rms_rows(oh, gain) * _silu(r_ref[0, :, vs]))
    o_ref[0] = jnp.concatenate(outs, axis=-1).astype(BF16)


def _gla(qk, v, r, lr, wgk_bf, bgk, gn, b, s):
    ns = s // GLA_TC
    idx = np.arange(GLA_TC)
    same_chunk = (idx[:, None] // GLA_CHUNK) == (idx[None, :] // GLA_CHUNK)
    tri = jnp.asarray(same_chunk & (idx[:, None] >= idx[None, :]), BF16)
    ones = jnp.asarray(same_chunk, BF16)
    seq = lambda w: pl.BlockSpec((1, GLA_TC, w), lambda i, j: (i, j, 0))
    const = lambda a, c: pl.BlockSpec((a, c), lambda i, j: (0, 0))
    return pl.pallas_call(
        _gla_kernel,
        grid=(b, ns),
        in_specs=[seq(2 * QK_W), seq(GLA_V_W), seq(GLA_V_W), seq(GLA_RANK),
                  const(GLA_RANK, QK_W), const(1, QK_W), const(1, GLA_DV),
                  const(GLA_TC, GLA_TC), const(GLA_TC, GLA_TC)],
        out_specs=seq(GLA_V_W),
        out_shape=jax.ShapeDtypeStruct((b, s, GLA_V_W), BF16),
        scratch_shapes=[pltpu.VMEM((GLA_V_W, QK_W), F32)],
        compiler_params=_params(("arbitrary", "arbitrary")),
        name="gla",
    )(qk.reshape(b, s, 2 * QK_W), v.reshape(b, s, GLA_V_W), r.reshape(b, s, GLA_V_W),
      lr.reshape(b, s, GLA_RANK), wgk_bf, bgk, gn, tri, ones)


TQ = MOBA_BLOCK
KV_UNROLL = 8


def _moba_mem_kernel(q_ref, k_ref, vt_ref, kmean_ref, cq_ref, mk_ref, mvt_ref,
                     o_moba_ref, o_mem_ref,
                     qcat_ref, selb_ref, s_ref, acc_ref, *, n_blocks):
    i = pl.program_id(1)
    lane_head = lax.broadcasted_iota(jnp.int32, (TQ, QK_W), 1) // DH
    dist0 = (lax.broadcasted_iota(jnp.int32, (MOBA_BLOCK, TQ), 1)
             - lax.broadcasted_iota(jnp.int32, (MOBA_BLOCK, TQ), 0)).astype(F32)
    slopes = ALIBI_SLOPES
    heads = [slice(h * TQ, (h + 1) * TQ) for h in range(HEADS)]
    vrows = [slice(h * VT_ROWS, (h + 1) * VT_ROWS) for h in range(HEADS)]

    def stack_heads(x):
        return jnp.concatenate([jnp.where(lane_head == h, x, jnp.zeros_like(x)) for h in range(HEADS)], axis=0)

    def finish(accl):
        return accl[:DH] * (1.0 / accl[DH:DH + 1])

    s_all = _nt(mk_ref[0], stack_heads(cq_ref[0]))
    outs = []
    for h in range(HEADS):
        s = s_all[:, heads[h]]
        p = jnp.exp(s - jnp.max(s, axis=0, keepdims=True)).astype(BF16)
        outs.append(finish(_dot(mvt_ref[0, vrows[h], :], p)))
    o_mem_ref[0] = jnp.concatenate(outs, axis=0).T.astype(BF16)

    qcat = stack_heads(q_ref[0])
    qcat_ref[...] = qcat
    k_own = k_ref[0, pl.ds(pl.multiple_of(i * MOBA_BLOCK, MOBA_BLOCK), MOBA_BLOCK), :]
    gate_all = _nt(kmean_ref[0].astype(BF16), qcat)
    s_all = _nt(k_own, qcat)
    blk = lax.broadcasted_iota(jnp.int32, (n_blocks, TQ), 0)
    blk_f = blk.astype(F32)

    tq = (lax.broadcasted_iota(jnp.int32, (1, TQ), 1) - (MOBA_BLOCK - 1)).astype(F32)
    init = []
    for h in range(HEADS):
        gate = jnp.where(blk < i, gate_all[:, heads[h]], NEG_INF)
        chosen = jnp.zeros((n_blocks, TQ), jnp.bool_)
        for r in range(MOBA_TOPK):
            mx = jnp.max(gate, axis=0, keepdims=True)
            first = jnp.min(jnp.where(gate == mx, blk_f, float(n_blocks)), axis=0, keepdims=True)
            hit = blk_f == first
            chosen = chosen | (hit & (mx > NEG_INF))
            gate = jnp.where(hit, NEG_INF, gate)
        selb_ref[h] = jnp.where(chosen, 0.0, NEG_INF)
        aq = -slopes[h] * tq
        s = jnp.where(dist0 >= 0.0, s_all[:, heads[h]], NEG_INF)
        m0 = jnp.max(s, axis=0, keepdims=True) + aq
        p = jnp.exp(s - (m0 - aq)).astype(BF16)
        acc_ref[h] = _dot(vt_ref[i, vrows[h], :], p)
        init += [m0]

    def stage_scores(j, slot):
        kj = k_ref[0, pl.ds(pl.multiple_of(j * MOBA_BLOCK, MOBA_BLOCK), MOBA_BLOCK), :]
        s_ref[slot] = _nt(kj, qcat_ref[...])

    stage_scores(0, 0)

    def past_blocks(g, carry, unroll, base):
        carry = list(carry)
        for u in range(unroll):
            j = base + g * unroll + u
            stage_scores(jnp.minimum(j + 1, n_blocks - 1), (u + 1) % 2)
            off = jnp.full((1, TQ), (i - j) * MOBA_BLOCK, jnp.int32).astype(F32) + tq
            for h in range(HEADS):
                m = carry[h]
                s1 = s_ref[u % 2, :, heads[h]]
                rb = selb_ref[h, pl.ds(j, 1), :] - slopes[h] * off
                m_new = jnp.maximum(m, jnp.max(s1, axis=0, keepdims=True) + rb)
                p = jnp.exp(s1 - (m_new - rb)).astype(BF16)
                acc_ref[h] = jnp.exp(m - m_new) * acc_ref[h] + _dot(vt_ref[j, vrows[h], :], p)
                carry[h] = m_new
        return tuple(carry)

    n_main = i // KV_UNROLL
    rest = i - n_main * KV_UNROLL
    carry = lax.fori_loop(0, n_main, functools.partial(past_blocks, unroll=KV_UNROLL, base=0), tuple(init))
    final = lax.fori_loop(0, (rest + 1) // 2,
                          functools.partial(past_blocks, unroll=2, base=n_main * KV_UNROLL), carry)
    o_t = jnp.concatenate([finish(acc_ref[h]) for h in range(HEADS)], axis=0)
    o_moba_ref[0] = o_t.T.astype(BF16)


def _moba_mem(mq, mk, mvt, kmean, cq, memk, memvt, b, s):
    nq = s // TQ
    n_blocks = s // MOBA_BLOCK
    qspec = pl.BlockSpec((1, TQ, QK_W), lambda i, j: (i, j, 0))
    return pl.pallas_call(
        functools.partial(_moba_mem_kernel, n_blocks=n_blocks),
        grid=(b, nq),
        in_specs=[
            qspec,
            pl.BlockSpec((1, s, QK_W), lambda i, j: (i, 0, 0)),
            pl.BlockSpec((n_blocks, VT_ALL, MOBA_BLOCK), lambda i, j: (i, 0, 0)),
            pl.BlockSpec((1, n_blocks, QK_W), lambda i, j: (i, 0, 0)),
            qspec,
            pl.BlockSpec((1, N_MEM, QK_W), lambda i, j: (i, 0, 0)),
            pl.BlockSpec((1, VT_ALL, N_MEM), lambda i, j: (i, 0, 0)),
        ],
        out_specs=[qspec, qspec],
        out_shape=[jax.ShapeDtypeStruct((b, s, QK_W), BF16), jax.ShapeDtypeStruct((b, s, QK_W), BF16)],
        scratch_shapes=[
            pltpu.VMEM((HEADS * TQ, QK_W), BF16),
            pltpu.VMEM((HEADS, n_blocks, TQ), F32),
            pltpu.VMEM((2, MOBA_BLOCK, HEADS * TQ), F32),
            pltpu.VMEM((HEADS, VT_ROWS, TQ), F32),
        ],
        compiler_params=_params(("arbitrary", "arbitrary")),
        name="moba_mem",
    )(mq.reshape(b, s, QK_W), mk.reshape(b, s, QK_W), mvt, kmean.reshape(b, n_blocks, QK_W),
      cq.reshape(b, s, QK_W), memk, memvt)


OUT_TM = 1024
ROUTE_ROWS = 40


def _out_router_kernel(x_ref, og_ref, om_ref, oc_ref, w_ref, g_ref, wr_ref, br_ref, tri_ref,
                       x1_ref, h_ref, e_ref, gate_ref, count_ref, carry_ref):
    w = w_ref[...]
    x1 = (x_ref[...] + _dot(og_ref[...], w[:GLA_V_W]) + _dot(om_ref[...], w[GLA_V_W:GLA_V_W + QK_W])
          + _dot(oc_ref[...], w[GLA_V_W + QK_W:]))
    x1_ref[...] = x1
    hn = _rms_rows(x1, g_ref[...])
    _store_row_tiles(h_ref, _pack_bf16_pairs(hn))
    logits = _dot(hn.astype(BF16), wr_ref[...]) + br_ref[...]
    lt = logits.T[:ROUTE_ROWS]
    row = lax.broadcasted_iota(jnp.int32, lt.shape, 0).astype(F32)
    lg = jnp.where(row < N_GROUPS, lt, NEG_INF)
    mg = jnp.max(lg, axis=0, keepdims=True)
    g_sel = jnp.min(jnp.where(lg == mg, row, float(LANES)), axis=0, keepdims=True)
    p_group = 1.0 / jnp.sum(jnp.exp(lg - mg), axis=0, keepdims=True)
    lo = N_GROUPS + g_sel * EXPERTS_PER_GROUP
    le = jnp.where((row >= lo) & (row < lo + EXPERTS_PER_GROUP), lt, NEG_INF)
    m0 = jnp.max(le, axis=0, keepdims=True)
    i0 = jnp.min(jnp.where(le == m0, row, float(LANES)), axis=0, keepdims=True)
    le1 = jnp.where(row == i0, NEG_INF, le)
    m1 = jnp.max(le1, axis=0, keepdims=True)
    i1 = jnp.min(jnp.where(le1 == m1, row, float(LANES)), axis=0, keepdims=True)
    z = jnp.exp(m1 - m0)
    w0 = p_group / (1.0 + z)
    w1 = p_group * z / (1.0 + z)
    pad = jnp.zeros((8 - 2, lt.shape[1]), F32)
    gate_ref[...] = jnp.concatenate([w0, w1, pad], axis=0)

    @pl.when(pl.program_id(0) == 0)
    def _():
        carry_ref[...] = jnp.zeros_like(carry_ref)

    oh0 = (row == i0).astype(F32)
    oh1 = (row == i1).astype(F32)
    both = oh0 + oh1
    before = _dot(both.astype(BF16), tri_ref[...]) + carry_ref[:, 0:1]
    r0 = jnp.sum(oh0 * before, axis=0, keepdims=True)
    r1 = jnp.sum(oh1 * before, axis=0, keepdims=True)
    carry_ref[...] = carry_ref[...] + jnp.sum(both, axis=1, keepdims=True)
    count_ref[...] = carry_ref[...].astype(jnp.int32)
    e_ref[...] = jnp.concatenate([i0 - N_GROUPS, i1 - N_GROUPS, r0, r1, pad[:4]], axis=0).astype(jnp.int32)


def _out_router(x2, og, om, oc, w_bf, g, wr_bf, br):
    t = x2.shape[0]
    row = lambda w: pl.BlockSpec((OUT_TM, w), lambda i: (i, 0))
    const = lambda a, b: pl.BlockSpec((a, b), lambda i: (0, 0))
    idx = np.arange(OUT_TM)
    tri = jnp.asarray(idx[:, None] < idx[None, :], BF16)
    col = lambda r: pl.BlockSpec((r, OUT_TM), lambda i: (0, i))
    return pl.pallas_call(
        _out_router_kernel,
        grid=(t // OUT_TM,),
        in_specs=[row(D_MODEL), row(GLA_V_W), row(QK_W), row(QK_W), const(D_MODEL, D_MODEL),
                  const(1, D_MODEL), const(D_MODEL, LANES), const(1, LANES), const(OUT_TM, OUT_TM)],
        out_specs=[row(D_MODEL), pl.BlockSpec((OUT_TM * X_TILE, LANES), lambda i: (i, 0)),
                   col(8), col(8), const(ROUTE_ROWS, LANES)],
        out_shape=[
            jax.ShapeDtypeStruct((t, D_MODEL), F32),
            jax.ShapeDtypeStruct((t * X_TILE, LANES), jnp.uint32),
            jax.ShapeDtypeStruct((8, t), jnp.int32),
            jax.ShapeDtypeStruct((8, t), F32),
            jax.ShapeDtypeStruct((ROUTE_ROWS, LANES), jnp.int32),
        ],
        scratch_shapes=[pltpu.VMEM((ROUTE_ROWS, LANES), F32)],
        compiler_params=_params(("arbitrary",)),
        name="out_router",
    )(x2, og, om, oc, w_bf, g, wr_bf, br, tri)


DMA_UNROLL = 8
PLACE_UNROLL = 32


def _row_to_token(dest_ref, pad_lo_ref, pad_hi_ref, tok_ref):
    def pad_segment(e, c):
        hi = pad_hi_ref[e]

        def pad(g, c2):
            for r in range(8):
                tok_ref[hi - 8 * (g + 1) + r] = 0
            return c2
        lax.fori_loop(0, lax.shift_right_logical(hi - pad_lo_ref[e] + 7, 3), pad, 0)
        return c

    lax.fori_loop(0, pad_lo_ref.shape[0], pad_segment, 0)

    n_tok = dest_ref.shape[0] // 2

    def place(t, c):
        tok_ref[dest_ref[t]] = t
        tok_ref[dest_ref[n_tok + t]] = t
        return c

    lax.fori_loop(0, n_tok, place, 0, unroll=PLACE_UNROLL)


def _experts_kernel(be_ref, run_ref, next_ref, nused_ref, dest_ref, pad_lo_ref, pad_hi_ref,
                    hn_hbm, wg_hbm, wu_hbm, wd_hbm, ys_hbm,
                    hn_vmem, xg_even, xg_odd, y_even, y_odd, wg_f, wu_f, wd_f, wg_bf, wu_bf, wd_bf, tok_ref,
                    hn_sem, w_sem, y_sem):
    n_used = nused_ref[0]
    n_blocks = be_ref.shape[0]
    block_rows = ROW_BLOCK * ROW_TILE

    def weight_copies(e, slot):
        return [pltpu.make_async_copy(src.at[e], dst.at[slot], w_sem.at[slot])
                for src, dst in ((wg_hbm, wg_f), (wu_hbm, wu_f), (wd_hbm, wd_f))]

    def y_copy(buf, parity, i):
        dst = ys_hbm.at[pl.ds(pl.multiple_of(i * block_rows, block_rows), block_rows), :]
        return pltpu.make_async_copy(buf, dst, y_sem.at[parity])

    def gather(i, dst):
        base = jnp.minimum(i, n_blocks - 1) * ROW_BLOCK
        for r in range(ROW_BLOCK):
            src = pl.ds(pl.multiple_of(tok_ref[base + r] * X_TILE, X_TILE), X_TILE)
            dst[r * X_TILE:(r + 1) * X_TILE, :] = hn_vmem[src, :]

    def block(i, parity, cur, nxt, ybuf):
        @pl.when(jnp.logical_or(i == 0, be_ref[i] != be_ref[jnp.maximum(i - 1, 0)]))
        def _():
            slot = jnp.bitwise_and(run_ref[i], 1)
            for c in weight_copies(be_ref[i], slot):
                c.wait()
            wg_bf[...] = wg_f[slot].astype(BF16)
            wu_bf[...] = wu_f[slot].astype(BF16)
            wd_bf[...] = wd_f[slot].astype(BF16)

            @pl.when(next_ref[i] < n_used)
            def _():
                for c in weight_copies(be_ref[jnp.minimum(next_ref[i], n_blocks - 1)], 1 - slot):
                    c.start()

        @pl.when(i >= 2)
        def _():
            y_copy(ybuf, parity, i).wait()

        kh = D_MODEL // 2
        gather(i + 1, nxt)
        x_lo, x_hi = _unpack_bf16_pairs(_load_row_tiles(cur, ROW_BLOCK))
        gate = _dot(x_lo, wg_bf[:kh, :]) + _dot(x_hi, wg_bf[kh:, :])
        up = _dot(x_lo, wu_bf[:kh, :]) + _dot(x_hi, wu_bf[kh:, :])
        hid = (_silu(gate) * up).astype(BF16)
        _store_row_tiles(ybuf, _dot(hid, wd_bf[...]))
        y_copy(ybuf, parity, i).start()

    hn_load = pltpu.make_async_copy(hn_hbm, hn_vmem, hn_sem)
    hn_load.start()
    for c in weight_copies(be_ref[0], 0):
        c.start()
    _row_to_token(dest_ref, pad_lo_ref, pad_hi_ref, tok_ref)
    hn_load.wait()
    gather(0, xg_even)

    def pair(p, carry):
        block(2 * p, 0, xg_even, xg_odd, y_even)

        @pl.when(2 * p + 1 < n_used)
        def _():
            block(2 * p + 1, 1, xg_odd, xg_even, y_odd)
        return carry

    lax.fori_loop(0, lax.shift_right_logical(n_used + 1, 1), pair, 0)

    y_copy(y_even, 0, 0).wait()

    @pl.when(n_used >= 2)
    def _():
        y_copy(y_odd, 1, 0).wait()

    y_even[...] = jnp.zeros_like(y_even)

    def zero_start(i, carry):
        y_copy(y_even, 0, i).start()
        return carry

    def zero_wait(i, carry):
        y_copy(y_even, 0, i).wait()
        return carry

    lax.fori_loop(n_used, n_blocks, zero_start, 0)
    lax.fori_loop(n_used, n_blocks, zero_wait, 0)


def _experts(hn, dest, pad_lo, pad_hi, n_rows, block_expert, block_run, block_next, n_used,
             w_gate, w_up, w_down):
    hbm = pl.BlockSpec(memory_space=pl.ANY)
    grid_spec = pltpu.PrefetchScalarGridSpec(
        num_scalar_prefetch=7,
        grid=(1,),
        in_specs=[hbm, hbm, hbm, hbm],
        out_specs=hbm,
        scratch_shapes=[pltpu.VMEM(hn.shape, jnp.uint32),
                        pltpu.VMEM((ROW_BLOCK * X_TILE, LANES), jnp.uint32),
                        pltpu.VMEM((ROW_BLOCK * X_TILE, LANES), jnp.uint32),
                        pltpu.VMEM((ROW_BLOCK * ROW_TILE, LANES), F32),
                        pltpu.VMEM((ROW_BLOCK * ROW_TILE, LANES), F32),
                        pltpu.VMEM((2, D_MODEL, MOE_FF), F32), pltpu.VMEM((2, D_MODEL, MOE_FF), F32),
                        pltpu.VMEM((2, MOE_FF, D_MODEL), F32),
                        pltpu.VMEM((D_MODEL, MOE_FF), BF16), pltpu.VMEM((D_MODEL, MOE_FF), BF16),
                        pltpu.VMEM((MOE_FF, D_MODEL), BF16),
                        pltpu.SMEM((n_rows,), jnp.int32),
                        pltpu.SemaphoreType.DMA, pltpu.SemaphoreType.DMA((2,)), pltpu.SemaphoreType.DMA((2,))],
    )
    return pl.pallas_call(
        _experts_kernel,
        grid_spec=grid_spec,
        out_shape=jax.ShapeDtypeStruct((n_rows * ROW_TILE, LANES), F32),
        compiler_params=_params(("arbitrary",)),
        name="experts",
    )(block_expert, block_run, block_next, n_used, dest, pad_lo, pad_hi, hn, w_gate, w_up, w_down)


COMB_TM = 256


def _combine_kernel(d0_ref, d1_ref, d0_next_ref, d1_next_ref, x1_ref, gate_ref, ys_ref, out_ref, ybuf, sems):
    g = pl.program_id(0)

    def row_copy(d, slot, k, r):
        src = ys_ref.at[pl.ds(pl.multiple_of(d * ROW_TILE, ROW_TILE), ROW_TILE), :]
        dst = ybuf.at[slot, k, pl.ds(pl.multiple_of(r * ROW_TILE, ROW_TILE), ROW_TILE), :]
        return pltpu.make_async_copy(src, dst, sems.at[slot])

    def gather(drefs, base, slot):
        def issue(r, c):
            row_copy(drefs[0][0, 0, base + r], slot, 0, r).start(priority=0)
            row_copy(drefs[1][0, 0, base + r], slot, 1, r).start(priority=1)
            return c
        lax.fori_loop(0, COMB_TM, issue, 0, unroll=DMA_UNROLL)

    dest_ref = (d0_ref, d1_ref)
    dest_next_ref = (d0_next_ref, d1_next_ref)

    def finish(slot):
        for k in range(2):
            pltpu.make_async_copy(ys_ref.at[pl.ds(0, COMB_TM * ROW_TILE), :], ybuf.at[slot, k],
                                  sems.at[slot]).wait()
        rows = slice(slot * COMB_TM, (slot + 1) * COMB_TM)
        gate = gate_ref[:, rows].T
        w0 = gate[:, 0:1]
        w1 = gate[:, 1:2]
        y0 = _load_row_tiles(ybuf.at[slot, 0], COMB_TM)
        y1 = _load_row_tiles(ybuf.at[slot, 1], COMB_TM)
        out_ref[rows, :] = x1_ref[rows, :] + (y0 * w0 + y1 * w1)

    @pl.when(g == 0)
    def _():
        gather(dest_ref, 0, 0)

    gather(dest_ref, COMB_TM, 1)
    finish(0)

    @pl.when(g + 1 < pl.num_programs(0))
    def _():
        gather(dest_next_ref, 0, 0)

    finish(1)


def _combine(x1, gates, ys, dest):
    t = x1.shape[0]
    ng = t // (2 * COMB_TM)
    d0, d1 = (dest[k].reshape(ng, 1, 2 * COMB_TM) for k in range(2))
    smem = lambda f: pl.BlockSpec((1, 1, 2 * COMB_TM), f, memory_space=pltpu.SMEM)
    cur = lambda i: (i, 0, 0)
    nxt = lambda i: (jnp.minimum(i + 1, ng - 1), 0, 0)
    return pl.pallas_call(
        _combine_kernel,
        grid=(ng,),
        in_specs=[smem(cur), smem(cur), smem(nxt), smem(nxt),
                  pl.BlockSpec((2 * COMB_TM, D_MODEL), lambda i: (i, 0)),
                  pl.BlockSpec((8, 2 * COMB_TM), lambda i: (0, i)),
                  pl.BlockSpec(memory_space=pl.ANY)],
        out_specs=pl.BlockSpec((2 * COMB_TM, D_MODEL), lambda i: (i, 0)),
        out_shape=jax.ShapeDtypeStruct((t, D_MODEL), F32),
        scratch_shapes=[pltpu.VMEM((2, 2, COMB_TM * ROW_TILE, LANES), F32),
                        pltpu.SemaphoreType.DMA((2,))],
        compiler_params=_params(("arbitrary",)),
        name="combine",
    )(d0, d1, d0, d1, x1, gates, ys)


def _layer(x, mem, attn_norm_g, mem_norm_g, w_in, w_gla_gk, b_gla_gk, gla_out_norm_g,
           moba_q_norm_g, moba_k_norm_g, w_mem_kv, mem_q_norm_g, mem_k_norm_g, w_out,
           ffn_norm_g, w_router_group, b_router_group, w_router_expert, b_router_expert,
           w_gate, w_up, w_down):
    b, s, d = x.shape
    t = b * s
    x2 = x.reshape(t, d)
    row = lambda v: v.reshape(1, -1).astype(F32)
    tile_heads = lambda v: jnp.tile(v.astype(F32), HEADS).reshape(1, QK_W)
    hid = np.arange(QK_W) // DH
    seg = jnp.asarray((hid[:, None] == hid[None, :]) / DH, BF16)

    w_in_p = jnp.concatenate([w_in[:, :1536], w_in[:, 1552:], w_in[:, 1536:1552]], axis=1).astype(BF16)
    wr = jnp.concatenate([w_router_group,
                          jnp.transpose(w_router_expert, (1, 0, 2)).reshape(d, N_EXPERTS),
                          jnp.zeros((d, LANES - N_GROUPS - N_EXPERTS), F32)], axis=1).astype(BF16)
    br = jnp.concatenate([b_router_group, b_router_expert.reshape(N_EXPERTS),
                          jnp.zeros((LANES - N_GROUPS - N_EXPERTS,), F32)]).reshape(1, LANES)

    memk, memvt = _mem_kv(mem, row(mem_norm_g), w_mem_kv.astype(BF16), tile_heads(mem_k_norm_g), seg)
    qk, gv, gr, glr, mq, mk, mvt, kmean, cq = _in_proj(
        x2, row(attn_norm_g), w_in_p, seg, tile_heads(moba_q_norm_g), tile_heads(moba_k_norm_g),
        tile_heads(mem_q_norm_g))
    o_gla = _gla(qk, gv, gr, glr, w_gla_gk.astype(BF16), row(b_gla_gk), row(gla_out_norm_g), b, s)
    o_moba, o_mem = _moba_mem(mq, mk, mvt, kmean, cq, memk, memvt, b, s)
    x1, hn, e_ids, gates, counts = _out_router(
        x2, o_gla.reshape(t, GLA_V_W), o_moba.reshape(t, QK_W), o_mem.reshape(t, QK_W),
        w_out.astype(BF16), row(ffn_norm_g), wr, br)

    counts = counts[N_GROUPS:N_GROUPS + N_EXPERTS, 0]
    padded = (counts + ROW_BLOCK - 1) // ROW_BLOCK * ROW_BLOCK
    pends = jnp.cumsum(padded)
    pstarts = pends - padded
    onehot = e_ids[:2, :, None] == jnp.arange(N_EXPERTS, dtype=jnp.int32)
    dest = (jnp.sum(jnp.where(onehot, pstarts, 0), axis=-1) + e_ids[2:4]).astype(jnp.int32)
    n_rows = (t * 2 + N_EXPERTS * (ROW_BLOCK - 1) + ROW_BLOCK - 1) // ROW_BLOCK * ROW_BLOCK
    nb = n_rows // ROW_BLOCK
    block_start = jnp.arange(nb, dtype=jnp.int32) * ROW_BLOCK
    block_expert = jnp.minimum(jnp.sum(block_start[:, None] >= pends[None, :], axis=1),
                               N_EXPERTS - 1).astype(jnp.int32)
    n_used = (pends[-1] // ROW_BLOCK).astype(jnp.int32).reshape(1)
    ended = block_start[:, None] >= pends[None, :]
    block_run = jnp.sum(ended & (counts > 0)[None, :], axis=1).astype(jnp.int32)
    block_next = (jnp.min(jnp.where(ended, n_rows, pends[None, :]), axis=1) // ROW_BLOCK).astype(jnp.int32)

    pad_lo = jnp.concatenate([pstarts + counts, pends[-1:]]).astype(jnp.int32)
    pad_hi = jnp.concatenate([pends, jnp.full((1,), n_rows)]).astype(jnp.int32)
    ys = _experts(hn, dest.reshape(-1), pad_lo, pad_hi, n_rows, block_expert, block_run, block_next, n_used,
                  w_gate, w_up, w_down)
    out = _combine(x1, gates, ys, dest)
    return out.reshape(b, s, d)


def kernel(x, mem, attn_norm_g, mem_norm_g, w_in, w_gla_gk, b_gla_gk, gla_out_norm_g, moba_q_norm_g, moba_k_norm_g, w_mem_kv, mem_q_norm_g, mem_k_norm_g, w_out, ffn_norm_g, w_router_group, b_router_group, w_router_expert, b_router_expert, w_gate, w_up, w_down):
    depth = w_in.shape[0]
    for l in range(depth):
        x = _layer(x, mem, attn_norm_g[l], mem_norm_g[l], w_in[l], w_gla_gk[l], b_gla_gk[l],
                   gla_out_norm_g[l], moba_q_norm_g[l], moba_k_norm_g[l], w_mem_kv[l],
                   mem_q_norm_g[l], mem_k_norm_g[l], w_out[l], ffn_norm_g[l], w_router_group[l],
                   b_router_group[l], w_router_expert[l], b_router_expert[l],
                   w_gate[l], w_up[l], w_down[l])
    return x
```

```python
import functools

import jax
import jax.numpy as jnp
import numpy as np
from jax import lax
from jax.experimental import pallas as pl
from jax.experimental.pallas import tpu as pltpu

F32 = jnp.float32
BF16 = jnp.bfloat16
EPS = 1e-6
NEG_INF = float("-inf")

D_MODEL = 1024
N_MEM = 256
HEADS = 4
DH = 64
GLA_DV = 128
GLA_RANK = 16
GLA_GATE_NORMALIZER = 16.0
GLA_CHUNK = 64
MOBA_BLOCK = 256
MOBA_TOPK = 3
QK_W = HEADS * DH
GLA_V_W = HEADS * GLA_DV
N_GROUPS = 4
EXPERTS_PER_GROUP = 8
N_EXPERTS = N_GROUPS * EXPERTS_PER_GROUP
MOE_FF = 512
LANES = 128
ROW_BLOCK = 256

VMEM_LIMIT = 56 * 1024 * 1024


def _params(sem):
    return pltpu.CompilerParams(dimension_semantics=sem, vmem_limit_bytes=VMEM_LIMIT)


def _nt(a, b):
    return lax.dot_general(a, b, (((1,), (1,)), ((), ())), preferred_element_type=F32)


def _tn(a, b):
    return lax.dot_general(a, b, (((0,), (0,)), ((), ())), preferred_element_type=F32)


def _dot(a, b):
    return jnp.dot(a, b, preferred_element_type=F32)


def _rms_rows(x, g):
    ms = jnp.mean(x * x, axis=-1, keepdims=True)
    return x * lax.rsqrt(ms + EPS) * g


def _split3(x):
    h1 = x.astype(BF16)
    r1 = x - h1.astype(F32)
    h2 = r1.astype(BF16)
    h3 = (r1 - h2.astype(F32)).astype(BF16)
    return h1, h2, h3


def _head_mean_sq(x, seg):
    sq = x * x
    hi = sq.astype(BF16)
    lo = (sq - hi.astype(F32)).astype(BF16)
    return _dot(hi, seg) + _dot(lo, seg)


def _silu(x):
    return x * (1.0 / (1.0 + jnp.exp(-x)))


VT_ROWS = DH + 16
VT_ALL = HEADS * VT_ROWS


ALIBI_SLOPES = tuple(2.0 ** (-8.0 * (h + 1) / HEADS) for h in range(HEADS))


def _vt_with_ones(v, key_slopes=None):
    n = v.shape[0]
    vt = v.T
    ones = jnp.ones((VT_ROWS - DH, n), F32)
    pos = lax.broadcasted_iota(jnp.int32, (1, n), 1).astype(F32)
    parts = []
    for h in range(HEADS):
        scale = 1.0 if key_slopes is None else jnp.exp(key_slopes[h] * (pos - (n - 1)))
        parts += [vt[h * DH:(h + 1) * DH] * scale, ones * scale]
    return jnp.concatenate(parts, axis=0).astype(BF16)


ROW_TILE = D_MODEL // LANES
X_TILE = ROW_TILE // 2


def _store_row_tiles(ref, x):
    n, tiles = x.shape[0], x.shape[1] // LANES
    for c in range(tiles):
        ref[pl.ds(c, n, stride=tiles), :] = x[:, c * LANES:(c + 1) * LANES]


def _load_row_tiles(ref, n):
    tiles = ref.shape[0] // n
    return jnp.concatenate([ref[pl.ds(c, n, stride=tiles), :] for c in range(tiles)], axis=1)


def _pack_bf16_pairs(x):
    w = x.shape[1] // 2
    bits = lambda t: lax.bitcast_convert_type(t.astype(BF16).astype(F32), jnp.uint32)
    return (bits(x[:, w:]) & jnp.uint32(0xFFFF0000)) | (bits(x[:, :w]) >> 16)


def _unpack_bf16_pairs(words):
    lo = lax.bitcast_convert_type(words << 16, F32).astype(BF16)
    hi = lax.bitcast_convert_type(words & jnp.uint32(0xFFFF0000), F32).astype(BF16)
    return lo, hi


def _mem_kv_kernel(mem_ref, g_ref, w_ref, gk_ref, seg_ref, k_ref, vt_ref):
    h = _rms_rows(mem_ref[0], g_ref[...]).astype(BF16)
    kv = _dot(h, w_ref[...])
    k = kv[:, :QK_W]
    kn = k * lax.rsqrt(_head_mean_sq(k, seg_ref[...]) + EPS) * gk_ref[...]
    k_ref[0] = kn.astype(BF16)
    vt_ref[0] = _vt_with_ones(kv[:, QK_W:])


def _mem_kv(mem, g, w_bf, gk_t, seg):
    b = mem.shape[0]
    return pl.pallas_call(
        _mem_kv_kernel,
        grid=(b,),
        in_specs=[
            pl.BlockSpec((1, N_MEM, D_MODEL), lambda i: (i, 0, 0)),
            pl.BlockSpec((1, D_MODEL), lambda i: (0, 0)),
            pl.BlockSpec((D_MODEL, 2 * QK_W), lambda i: (0, 0)),
            pl.BlockSpec((1, QK_W), lambda i: (0, 0)),
            pl.BlockSpec((QK_W, QK_W), lambda i: (0, 0)),
        ],
        out_specs=[
            pl.BlockSpec((1, N_MEM, QK_W), lambda i: (i, 0, 0)),
            pl.BlockSpec((1, VT_ALL, N_MEM), lambda i: (i, 0, 0)),
        ],
        out_shape=[
            jax.ShapeDtypeStruct((b, N_MEM, QK_W), BF16),
            jax.ShapeDtypeStruct((b, VT_ALL, N_MEM), BF16),
        ],
        compiler_params=_params(("arbitrary",)),
        name="mem_kv",
    )(mem, g, w_bf, gk_t, seg)


IN_TM = 1024
_C_QK, _C_V, _C_R, _C_MQ, _C_MK, _C_MV, _C_CQ, _C_LR = 0, 512, 1024, 1536, 1792, 2048, 2304, 2560
D_IN = 2576


def _in_proj_kernel(x_ref, g_ref, w_ref, seg_ref, gq_ref, gk_ref, gc_ref,
                    qk_ref, v_ref, r_ref, lr_ref, mq_ref, mk_ref, mvt_ref, kmean_ref, cq_ref):
    h = _rms_rows(x_ref[...], g_ref[...]).astype(BF16)
    p = _dot(h, w_ref[...])
    seg = seg_ref[...]
    qk_ref[...] = p[:, _C_QK:_C_V]
    v_ref[...] = p[:, _C_V:_C_R].astype(BF16)
    r_ref[...] = p[:, _C_R:_C_MQ]
    lr_ref[...] = p[:, _C_LR:D_IN]

    def head_norm(t, gain):
        return t * lax.rsqrt(_head_mean_sq(t, seg) + EPS) * gain

    scale = DH ** -0.5
    mq_ref[...] = (head_norm(p[:, _C_MQ:_C_MK], gq_ref[...]) * scale).astype(BF16)
    cq_ref[...] = (head_norm(p[:, _C_CQ:_C_LR], gc_ref[...]) * scale).astype(BF16)
    kn = head_norm(p[:, _C_MK:_C_MV], gk_ref[...])
    mk_ref[...] = kn.astype(BF16)
    mv = p[:, _C_MV:_C_CQ]
    for j in range(IN_TM // MOBA_BLOCK):
        rows = slice(j * MOBA_BLOCK, (j + 1) * MOBA_BLOCK)
        kmean_ref[0, j:j + 1, :] = jnp.mean(kn[rows], axis=0, keepdims=True)
        mvt_ref[j] = _vt_with_ones(mv[rows], ALIBI_SLOPES)


def _in_proj(x2, g, w_bf, seg, gq_t, gk_t, gc_t):
    t = x2.shape[0]
    nt = t // IN_TM
    nb = IN_TM // MOBA_BLOCK
    row = lambda w: pl.BlockSpec((IN_TM, w), lambda i: (i, 0))
    const = lambda a, b: pl.BlockSpec((a, b), lambda i: (0, 0))
    return pl.pallas_call(
        _in_proj_kernel,
        grid=(nt,),
        in_specs=[row(D_MODEL), const(1, D_MODEL), const(D_MODEL, D_IN), const(QK_W, QK_W),
                  const(1, QK_W), const(1, QK_W), const(1, QK_W)],
        out_specs=[row(2 * QK_W), row(GLA_V_W), row(GLA_V_W), row(GLA_RANK), row(QK_W), row(QK_W),
                   pl.BlockSpec((nb, VT_ALL, MOBA_BLOCK), lambda i: (i, 0, 0)),
                   pl.BlockSpec((1, nb, QK_W), lambda i: (i, 0, 0)),
                   row(QK_W)],
        out_shape=[
            jax.ShapeDtypeStruct((t, 2 * QK_W), F32),
            jax.ShapeDtypeStruct((t, GLA_V_W), BF16),
            jax.ShapeDtypeStruct((t, GLA_V_W), F32),
            jax.ShapeDtypeStruct((t, GLA_RANK), F32),
            jax.ShapeDtypeStruct((t, QK_W), BF16),
            jax.ShapeDtypeStruct((t, QK_W), BF16),
            jax.ShapeDtypeStruct((t // MOBA_BLOCK, VT_ALL, MOBA_BLOCK), BF16),
            jax.ShapeDtypeStruct((nt, nb, QK_W), F32),
            jax.ShapeDtypeStruct((t, QK_W), BF16),
        ],
        compiler_params=_params(("arbitrary",)),
        name="in_proj",
    )(x2, g, w_bf, seg, gq_t, gk_t, gc_t)


GLA_TC = 256


def _gla_kernel(qk_ref, v_ref, r_ref, lr_ref, wgk_ref, bgk_ref, gn_ref, tri_ref, ones_ref,
                o_ref, st_ref):
    @pl.when(pl.program_id(1) == 0)
    def _():
        st_ref[...] = jnp.zeros_like(st_ref)

    qk = qk_ref[0]
    q = qk[:, :QK_W]
    k = qk[:, QK_W:]
    gk = _dot(lr_ref[0].astype(BF16), wgk_ref[...]) + bgk_ref[...]
    g = -(jnp.maximum(-gk, 0.0) + jnp.log1p(jnp.exp(-jnp.abs(gk)))) / GLA_GATE_NORMALIZER
    g1, g2, g3 = _split3(g)
    tri = tri_ref[...]
    ones = ones_ref[...]
    cum = _dot(tri, g1) + _dot(tri, g2) + _dot(tri, g3)
    tot = _dot(ones, g1) + _dot(ones, g2) + _dot(ones, g3)
    q_dec = (q * (DH ** -0.5) * jnp.exp(cum)).astype(BF16)
    k_inv = (k * jnp.exp(-cum)).astype(BF16)
    k_end = (k * jnp.exp(tot - cum)).astype(BF16)
    decay = jnp.exp(tot)

    lane_head = lax.broadcasted_iota(jnp.int32, (GLA_TC, QK_W), 1) // DH
    row_t = lax.broadcasted_iota(jnp.int32, (GLA_TC, GLA_TC), 0)
    col_t = lax.broadcasted_iota(jnp.int32, (GLA_TC, GLA_TC), 1)
    causal = (row_t >= col_t) & (row_t // GLA_CHUNK == col_t // GLA_CHUNK)
    same_head = (lax.broadcasted_iota(jnp.int32, (GLA_V_W, QK_W), 0) // GLA_DV
                 == lax.broadcasted_iota(jnp.int32, (GLA_V_W, QK_W), 1) // DH)
    gain = gn_ref[...]

    v = v_ref[0]
    chunks = [slice(c * GLA_CHUNK, (c + 1) * GLA_CHUNK) for c in range(GLA_TC // GLA_CHUNK)]
    d_st = [jnp.where(same_head, _tn(v[rows], k_end[rows]), 0.0) for rows in chunks]
    states = [st_ref[...]]
    for c, rows in enumerate(chunks):
        states.append(states[c] * decay[rows.start:rows.start + 1, :] + d_st[c])
    st_ref[...] = states[-1]
    o_inter = jnp.concatenate([_nt(q_dec[rows], states[c].astype(BF16)) for c, rows in enumerate(chunks)],
                              axis=0)
    q_stack = jnp.concatenate([jnp.where(lane_head == h, q_dec, jnp.zeros_like(q_dec)) for h in range(HEADS)],
                              axis=0)
    a_all = _nt(q_stack, k_inv)
    outs = []
    for h in range(HEADS):
        a = jnp.where(causal, a_all[h * GLA_TC:(h + 1) * GLA_TC], 0.0).astype(BF16)
        vs = slice(h * GLA_DV, (h + 1) * GLA_DV)
        oh = _dot(a, v[:, vs]) + o_inter[:, vs]
        outs.append(_rms_rows(oh, gain) * _silu(r_ref[0, :, vs]))
    o_ref[0] = jnp.concatenate(outs, axis=-1).astype(BF16)


def _gla(qk, v, r, lr, wgk_bf, bgk, gn, b, s):
    ns = s // GLA_TC
    idx = np.arange(GLA_TC)
    same_chunk = (idx[:, None] // GLA_CHUNK) == (idx[None, :] // GLA_CHUNK)
    tri = jnp.asarray(same_chunk & (idx[:, None] >= idx[None, :]), BF16)
    ones = jnp.asarray(same_chunk, BF16)
    seq = lambda w: pl.BlockSpec((1, GLA_TC, w), lambda i, j: (i, j, 0))
    const = lambda a, c: pl.BlockSpec((a, c), lambda i, j: (0, 0))
    return pl.pallas_call(
        _gla_kernel,
        grid=(b, ns),
        in_specs=[seq(2 * QK_W), seq(GLA_V_W), seq(GLA_V_W), seq(GLA_RANK),
                  const(GLA_RANK, QK_W), const(1, QK_W), const(1, GLA_DV),
                  const(GLA_TC, GLA_TC), const(GLA_TC, GLA_TC)],
        out_specs=seq(GLA_V_W),
        out_shape=jax.ShapeDtypeStruct((b, s, GLA_V_W), BF16),
        scratch_shapes=[pltpu.VMEM((GLA_V_W, QK_W), F32)],
        compiler_params=_params(("arbitrary", "arbitrary")),
        name="gla",
    )(qk.reshape(b, s, 2 * QK_W), v.reshape(b, s, GLA_V_W), r.reshape(b, s, GLA_V_W),
      lr.reshape(b, s, GLA_RANK), wgk_bf, bgk, gn, tri, ones)


TQ = MOBA_BLOCK
KV_UNROLL = 8


def _moba_mem_kernel(q_ref, k_ref, vt_ref, kmean_ref, cq_ref, mk_ref, mvt_ref,
                     o_moba_ref, o_mem_ref,
                     qcat_ref, selb_ref, s_even, s_odd, acc_ref, *, n_blocks):
    s_slots = (s_even, s_odd)
    i = pl.program_id(1)
    lane_head = lax.broadcasted_iota(jnp.int32, (TQ, QK_W), 1) // DH
    dist0 = (lax.broadcasted_iota(jnp.int32, (MOBA_BLOCK, TQ), 1)
             - lax.broadcasted_iota(jnp.int32, (MOBA_BLOCK, TQ), 0)).astype(F32)
    slopes = ALIBI_SLOPES
    heads = [slice(h * TQ, (h + 1) * TQ) for h in range(HEADS)]
    vrows = [slice(h * VT_ROWS, (h + 1) * VT_ROWS) for h in range(HEADS)]

    def stack_heads(x):
        return jnp.concatenate([jnp.where(lane_head == h, x, jnp.zeros_like(x)) for h in range(HEADS)], axis=0)

    def finish(accl):
        return accl[:DH] * (1.0 / accl[DH:DH + 1])

    s_even[...] = _nt(mk_ref[0], stack_heads(cq_ref[0]))
    outs = []
    for h in range(HEADS):
        s = s_even[:, heads[h]]
        p = jnp.exp(s - jnp.max(s, axis=0, keepdims=True)).astype(BF16)
        outs.append(finish(_dot(mvt_ref[0, vrows[h], :], p)))
    o_mem_ref[0] = jnp.concatenate(outs, axis=0).T.astype(BF16)

    qcat_ref[...] = stack_heads(q_ref[0])
    k_own = k_ref[0, pl.ds(pl.multiple_of(i * MOBA_BLOCK, MOBA_BLOCK), MOBA_BLOCK), :]
    gate_all = _nt(kmean_ref[0].astype(BF16), qcat_ref[...])
    s_odd[...] = _nt(k_own, qcat_ref[...])
    blk = lax.broadcasted_iota(jnp.int32, (n_blocks, TQ), 0)
    blk_f = blk.astype(F32)

    tq = (lax.broadcasted_iota(jnp.int32, (1, TQ), 1) - (MOBA_BLOCK - 1)).astype(F32)
    init = []
    for h in range(HEADS):
        gate = jnp.where(blk < i, gate_all[:, heads[h]], NEG_INF)
        chosen = jnp.zeros((n_blocks, TQ), jnp.bool_)
        for r in range(MOBA_TOPK):
            mx = jnp.max(gate, axis=0, keepdims=True)
            first = jnp.min(jnp.where(gate == mx, blk_f, float(n_blocks)), axis=0, keepdims=True)
            hit = blk_f == first
            chosen = chosen | (hit & (mx > NEG_INF))
            gate = jnp.where(hit, NEG_INF, gate)
        selb_ref[h] = jnp.where(chosen, 0.0, NEG_INF)
        aq = -slopes[h] * tq
        s = jnp.where(dist0 >= 0.0, s_odd[:, heads[h]], NEG_INF)
        m0 = jnp.max(s, axis=0, keepdims=True) + aq
        p = jnp.exp(s - (m0 - aq)).astype(BF16)
        acc_ref[h] = _dot(vt_ref[i, vrows[h], :], p)
        init += [m0]

    def stage_scores(j, slot):
        kj = k_ref[0, pl.ds(pl.multiple_of(j * MOBA_BLOCK, MOBA_BLOCK), MOBA_BLOCK), :]
        s_slots[slot][...] = _nt(kj, qcat_ref[...])

    stage_scores(0, 0)

    row0 = jnp.minimum(i, 0) * MOBA_BLOCK

    def past_blocks(g, carry, unroll, base):
        carry = list(carry)
        for u in range(unroll):
            j = base + g * unroll + u
            stage_scores(jnp.minimum(j + 1, n_blocks - 1), (u + 1) % 2)
            off = jnp.full((1, TQ), (i - j) * MOBA_BLOCK, jnp.int32).astype(F32) + tq
            for h in range(HEADS):
                m = carry[h]
                s1 = s_slots[u % 2][:, heads[h]]
                rb = selb_ref[h, pl.ds(j, 1), :] - slopes[h] * off
                m_new = jnp.maximum(m, jnp.max(s1, axis=0, keepdims=True) + rb)
                s2 = s_slots[u % 2][pl.ds(pl.multiple_of(row0, MOBA_BLOCK), MOBA_BLOCK), heads[h]]
                p = jnp.exp(s2 - (m_new - rb)).astype(BF16)
                acc_ref[h] = jnp.exp(m - m_new) * acc_ref[h] + _dot(vt_ref[j, vrows[h], :], p)
                carry[h] = m_new
        return tuple(carry)

    n_main = i // KV_UNROLL
    rest = i - n_main * KV_UNROLL
    carry = lax.fori_loop(0, n_main, functools.partial(past_blocks, unroll=KV_UNROLL, base=0), tuple(init))
    final = lax.fori_loop(0, (rest + 1) // 2,
                          functools.partial(past_blocks, unroll=2, base=n_main * KV_UNROLL), carry)
    o_t = jnp.concatenate([finish(acc_ref[h]) for h in range(HEADS)], axis=0)
    o_moba_ref[0] = o_t.T.astype(BF16)


def _moba_mem(mq, mk, mvt, kmean, cq, memk, memvt, b, s):
    nq = s // TQ
    n_blocks = s // MOBA_BLOCK
    qspec = pl.BlockSpec((1, TQ, QK_W), lambda i, j: (i, j, 0))
    return pl.pallas_call(
        functools.partial(_moba_mem_kernel, n_blocks=n_blocks),
        grid=(b, nq),
        in_specs=[
            qspec,
            pl.BlockSpec((1, s, QK_W), lambda i, j: (i, 0, 0)),
            pl.BlockSpec((n_blocks, VT_ALL, MOBA_BLOCK), lambda i, j: (i, 0, 0)),
            pl.BlockSpec((1, n_blocks, QK_W), lambda i, j: (i, 0, 0)),
            qspec,
            pl.BlockSpec((1, N_MEM, QK_W), lambda i, j: (i, 0, 0)),
            pl.BlockSpec((1, VT_ALL, N_MEM), lambda i, j: (i, 0, 0)),
        ],
        out_specs=[qspec, qspec],
        out_shape=[jax.ShapeDtypeStruct((b, s, QK_W), BF16), jax.ShapeDtypeStruct((b, s, QK_W), BF16)],
        scratch_shapes=[
            pltpu.VMEM((HEADS * TQ, QK_W), BF16),
            pltpu.VMEM((HEADS, n_blocks, TQ), F32),
            pltpu.VMEM((MOBA_BLOCK, HEADS * TQ), F32),
            pltpu.VMEM((MOBA_BLOCK, HEADS * TQ), F32),
            pltpu.VMEM((HEADS, VT_ROWS, TQ), F32),
        ],
        compiler_params=_params(("arbitrary", "arbitrary")),
        name="moba_mem",
    )(mq.reshape(b, s, QK_W), mk.reshape(b, s, QK_W), mvt, kmean.reshape(b, n_blocks, QK_W),
      cq.reshape(b, s, QK_W), memk, memvt)


OUT_TM = 1024
ROUTE_ROWS = 40


def _out_router_kernel(x_ref, og_ref, om_ref, oc_ref, w_ref, g_ref, wr_ref, br_ref, tri_ref,
                       x1_ref, h_ref, e_ref, gate_ref, count_ref, carry_ref):
    w = w_ref[...]
    x1 = (x_ref[...] + _dot(og_ref[...], w[:GLA_V_W]) + _dot(om_ref[...], w[GLA_V_W:GLA_V_W + QK_W])
          + _dot(oc_ref[...], w[GLA_V_W + QK_W:]))
    x1_ref[...] = x1
    hn = _rms_rows(x1, g_ref[...])
    _store_row_tiles(h_ref, _pack_bf16_pairs(hn))
    logits = _dot(hn.astype(BF16), wr_ref[...]) + br_ref[...]
    lt = logits.T[:ROUTE_ROWS]
    row = lax.broadcasted_iota(jnp.int32, lt.shape, 0).astype(F32)
    lg = jnp.where(row < N_GROUPS, lt, NEG_INF)
    mg = jnp.max(lg, axis=0, keepdims=True)
    g_sel = jnp.min(jnp.where(lg == mg, row, float(LANES)), axis=0, keepdims=True)
    p_group = 1.0 / jnp.sum(jnp.exp(lg - mg), axis=0, keepdims=True)
    lo = N_GROUPS + g_sel * EXPERTS_PER_GROUP
    le = jnp.where((row >= lo) & (row < lo + EXPERTS_PER_GROUP), lt, NEG_INF)
    m0 = jnp.max(le, axis=0, keepdims=True)
    i0 = jnp.min(jnp.where(le == m0, row, float(LANES)), axis=0, keepdims=True)
    le1 = jnp.where(row == i0, NEG_INF, le)
    m1 = jnp.max(le1, axis=0, keepdims=True)
    i1 = jnp.min(jnp.where(le1 == m1, row, float(LANES)), axis=0, keepdims=True)
    z = jnp.exp(m1 - m0)
    w0 = p_group / (1.0 + z)
    w1 = p_group * z / (1.0 + z)
    pad = jnp.zeros((8 - 2, lt.shape[1]), F32)
    gate_ref[...] = jnp.concatenate([w0, w1, pad], axis=0)

    @pl.when(pl.program_id(0) == 0)
    def _():
        carry_ref[...] = jnp.zeros_like(carry_ref)

    oh0 = (row == i0).astype(F32)
    oh1 = (row == i1).astype(F32)
    both = oh0 + oh1
    before = _dot(both.astype(BF16), tri_ref[...]) + carry_ref[:, 0:1]
    r0 = jnp.sum(oh0 * before, axis=0, keepdims=True)
    r1 = jnp.sum(oh1 * before, axis=0, keepdims=True)
    carry_ref[...] = carry_ref[...] + jnp.sum(both, axis=1, keepdims=True)
    count_ref[...] = carry_ref[...].astype(jnp.int32)
    e_ref[...] = jnp.concatenate([i0 - N_GROUPS, i1 - N_GROUPS, r0, r1, pad[:4]], axis=0).astype(jnp.int32)


def _out_router(x2, og, om, oc, w_bf, g, wr_bf, br):
    t = x2.shape[0]
    row = lambda w: pl.BlockSpec((OUT_TM, w), lambda i: (i, 0))
    const = lambda a, b: pl.BlockSpec((a, b), lambda i: (0, 0))
    idx = np.arange(OUT_TM)
    tri = jnp.asarray(idx[:, None] < idx[None, :], BF16)
    col = lambda r: pl.BlockSpec((r, OUT_TM), lambda i: (0, i))
    return pl.pallas_call(
        _out_router_kernel,
        grid=(t // OUT_TM,),
        in_specs=[row(D_MODEL), row(GLA_V_W), row(QK_W), row(QK_W), const(D_MODEL, D_MODEL),
                  const(1, D_MODEL), const(D_MODEL, LANES), const(1, LANES), const(OUT_TM, OUT_TM)],
        out_specs=[row(D_MODEL), pl.BlockSpec((OUT_TM * X_TILE, LANES), lambda i: (i, 0)),
                   col(8), col(8), const(ROUTE_ROWS, LANES)],
        out_shape=[
            jax.ShapeDtypeStruct((t, D_MODEL), F32),
            jax.ShapeDtypeStruct((t * X_TILE, LANES), jnp.uint32),
            jax.ShapeDtypeStruct((8, t), jnp.int32),
            jax.ShapeDtypeStruct((8, t), F32),
            jax.ShapeDtypeStruct((ROUTE_ROWS, LANES), jnp.int32),
        ],
        scratch_shapes=[pltpu.VMEM((ROUTE_ROWS, LANES), F32)],
        compiler_params=_params(("arbitrary",)),
        name="out_router",
    )(x2, og, om, oc, w_bf, g, wr_bf, br, tri)


DMA_UNROLL = 8
PLACE_UNROLL = 32


def _row_to_token(dest_ref, pad_lo_ref, pad_hi_ref, tok_ref):
    def pad_segment(e, c):
        hi = pad_hi_ref[e]

        def pad(g, c2):
            for r in range(8):
                tok_ref[hi - 8 * (g + 1) + r] = 0
            return c2
        lax.fori_loop(0, lax.shift_right_logical(hi - pad_lo_ref[e] + 7, 3), pad, 0)
        return c

    lax.fori_loop(0, pad_lo_ref.shape[0], pad_segment, 0)

    n_tok = dest_ref.shape[0] // 2

    def place(t, c):
        tok_ref[dest_ref[t]] = t
        tok_ref[dest_ref[n_tok + t]] = t
        return c

    lax.fori_loop(0, n_tok, place, 0, unroll=PLACE_UNROLL)


def _experts_kernel(be_ref, run_ref, next_ref, nused_ref, dest_ref, pad_lo_ref, pad_hi_ref,
                    hn_hbm, wg_hbm, wu_hbm, wd_hbm, ys_hbm,
                    hn_vmem, xg_even, xg_odd, y_even, y_odd, wg_f, wu_f, wd_f, wg_bf, wu_bf, wd_bf, tok_ref,
                    hn_sem, w_sem, y_sem):
    n_used = nused_ref[0]
    n_blocks = be_ref.shape[0]
    block_rows = ROW_BLOCK * ROW_TILE

    def weight_copies(e, slot):
        return [pltpu.make_async_copy(src.at[e], dst.at[slot], w_sem.at[slot])
                for src, dst in ((wg_hbm, wg_f), (wu_hbm, wu_f), (wd_hbm, wd_f))]

    def y_copy(buf, parity, i):
        dst = ys_hbm.at[pl.ds(pl.multiple_of(i * block_rows, block_rows), block_rows), :]
        return pltpu.make_async_copy(buf, dst, y_sem.at[parity])

    def gather(i, dst):
        base = jnp.minimum(i, n_blocks - 1) * ROW_BLOCK
        for r in range(ROW_BLOCK):
            src = pl.ds(pl.multiple_of(tok_ref[base + r] * X_TILE, X_TILE), X_TILE)
            dst[r * X_TILE:(r + 1) * X_TILE, :] = hn_vmem[src, :]

    def block(i, parity, cur, nxt, ybuf):
        @pl.when(jnp.logical_or(i == 0, be_ref[i] != be_ref[jnp.maximum(i - 1, 0)]))
        def _():
            slot = jnp.bitwise_and(run_ref[i], 1)
            for c in weight_copies(be_ref[i], slot):
                c.wait()
            wg_bf[...] = wg_f[slot].astype(BF16)
            wu_bf[...] = wu_f[slot].astype(BF16)
            wd_bf[...] = wd_f[slot].astype(BF16)

            @pl.when(next_ref[i] < n_used)
            def _():
                for c in weight_copies(be_ref[jnp.minimum(next_ref[i], n_blocks - 1)], 1 - slot):
                    c.start()

        @pl.when(i >= 2)
        def _():
            y_copy(ybuf, parity, i).wait()

        kh = D_MODEL // 2
        gather(i + 1, nxt)
        x_lo, x_hi = _unpack_bf16_pairs(_load_row_tiles(cur, ROW_BLOCK))
        gate = _dot(x_lo, wg_bf[:kh, :]) + _dot(x_hi, wg_bf[kh:, :])
        up = _dot(x_lo, wu_bf[:kh, :]) + _dot(x_hi, wu_bf[kh:, :])
        hid = (_silu(gate) * up).astype(BF16)
        _store_row_tiles(ybuf, _dot(hid, wd_bf[...]))
        y_copy(ybuf, parity, i).start()

    hn_load = pltpu.make_async_copy(hn_hbm, hn_vmem, hn_sem)
    hn_load.start()
    for c in weight_copies(be_ref[0], 0):
        c.start()
    _row_to_token(dest_ref, pad_lo_ref, pad_hi_ref, tok_ref)
    hn_load.wait()
    gather(0, xg_even)

    def pair(p, carry):
        block(2 * p, 0, xg_even, xg_odd, y_even)

        @pl.when(2 * p + 1 < n_used)
        def _():
            block(2 * p + 1, 1, xg_odd, xg_even, y_odd)
        return carry

    lax.fori_loop(0, lax.shift_right_logical(n_used + 1, 1), pair, 0)

    y_copy(y_even, 0, 0).wait()

    @pl.when(n_used >= 2)
    def _():
        y_copy(y_odd, 1, 0).wait()

    y_even[...] = jnp.zeros_like(y_even)

    def zero_start(i, carry):
        y_copy(y_even, 0, i).start()
        return carry

    def zero_wait(i, carry):
        y_copy(y_even, 0, i).wait()
        return carry

    lax.fori_loop(n_used, n_blocks, zero_start, 0)
    lax.fori_loop(n_used, n_blocks, zero_wait, 0)


def _experts(hn, dest, pad_lo, pad_hi, n_rows, block_expert, block_run, block_next, n_used,
             w_gate, w_up, w_down):
    hbm = pl.BlockSpec(memory_space=pl.ANY)
    grid_spec = pltpu.PrefetchScalarGridSpec(
        num_scalar_prefetch=7,
        grid=(1,),
        in_specs=[hbm, hbm, hbm, hbm],
        out_specs=hbm,
        scratch_shapes=[pltpu.VMEM(hn.shape, jnp.uint32),
                        pltpu.VMEM((ROW_BLOCK * X_TILE, LANES), jnp.uint32),
                        pltpu.VMEM((ROW_BLOCK * X_TILE, LANES), jnp.uint32),
                        pltpu.VMEM((ROW_BLOCK * ROW_TILE, LANES), F32),
                        pltpu.VMEM((ROW_BLOCK * ROW_TILE, LANES), F32),
                        pltpu.VMEM((2, D_MODEL, MOE_FF), F32), pltpu.VMEM((2, D_MODEL, MOE_FF), F32),
                        pltpu.VMEM((2, MOE_FF, D_MODEL), F32),
                        pltpu.VMEM((D_MODEL, MOE_FF), BF16), pltpu.VMEM((D_MODEL, MOE_FF), BF16),
                        pltpu.VMEM((MOE_FF, D_MODEL), BF16),
                        pltpu.SMEM((n_rows,), jnp.int32),
                        pltpu.SemaphoreType.DMA, pltpu.SemaphoreType.DMA((2,)), pltpu.SemaphoreType.DMA((2,))],
    )
    return pl.pallas_call(
        _experts_kernel,
        grid_spec=grid_spec,
        out_shape=jax.ShapeDtypeStruct((n_rows * ROW_TILE, LANES), F32),
        compiler_params=_params(("arbitrary",)),
        name="experts",
    )(block_expert, block_run, block_next, n_used, dest, pad_lo, pad_hi, hn, w_gate, w_up, w_down)


COMB_TM = 256


def _combine_kernel(d0_ref, d1_ref, d0_next_ref, d1_next_ref, x1_ref, gate_ref, ys_ref, out_ref, ybuf, sems):
    g = pl.program_id(0)

    def row_copy(d, slot, k, r):
        src = ys_ref.at[pl.ds(pl.multiple_of(d * ROW_TILE, ROW_TILE), ROW_TILE), :]
        dst = ybuf.at[slot, k, pl.ds(pl.multiple_of(r * ROW_TILE, ROW_TILE), ROW_TILE), :]
        return pltpu.make_async_copy(src, dst, sems.at[slot])

    def gather(drefs, base, slot):
        def issue(r, c):
            row_copy(drefs[0][0, 0, base + r], slot, 0, r).start(priority=0)
            row_copy(drefs[1][0, 0, base + r], slot, 1, r).start(priority=1)
            return c
        lax.fori_loop(0, COMB_TM, issue, 0, unroll=DMA_UNROLL)

    dest_ref = (d0_ref, d1_ref)
    dest_next_ref = (d0_next_ref, d1_next_ref)

    def finish(slot):
        for k in range(2):
            pltpu.make_async_copy(ys_ref.at[pl.ds(0, COMB_TM * ROW_TILE), :], ybuf.at[slot, k],
                                  sems.at[slot]).wait()
        rows = slice(slot * COMB_TM, (slot + 1) * COMB_TM)
        gate = gate_ref[:, rows].T
        w0 = gate[:, 0:1]
        w1 = gate[:, 1:2]
        y0 = _load_row_tiles(ybuf.at[slot, 0], COMB_TM)
        y1 = _load_row_tiles(ybuf.at[slot, 1], COMB_TM)
        out_ref[rows, :] = x1_ref[rows, :] + (y0 * w0 + y1 * w1)

    @pl.when(g == 0)
    def _():
        gather(dest_ref, 0, 0)

    gather(dest_ref, COMB_TM, 1)
    finish(0)

    @pl.when(g + 1 < pl.num_programs(0))
    def _():
        gather(dest_next_ref, 0, 0)

    finish(1)


def _combine(x1, gates, ys, dest):
    t = x1.shape[0]
    ng = t // (2 * COMB_TM)
    d0, d1 = (dest[k].reshape(ng, 1, 2 * COMB_TM) for k in range(2))
    smem = lambda f: pl.BlockSpec((1, 1, 2 * COMB_TM), f, memory_space=pltpu.SMEM)
    cur = lambda i: (i, 0, 0)
    nxt = lambda i: (jnp.minimum(i + 1, ng - 1), 0, 0)
    return pl.pallas_call(
        _combine_kernel,
        grid=(ng,),
        in_specs=[smem(cur), smem(cur), smem(nxt), smem(nxt),
                  pl.BlockSpec((2 * COMB_TM, D_MODEL), lambda i: (i, 0)),
                  pl.BlockSpec((8, 2 * COMB_TM), lambda i: (0, i)),
                  pl.BlockSpec(memory_space=pl.ANY)],
        out_specs=pl.BlockSpec((2 * COMB_TM, D_MODEL), lambda i: (i, 0)),
        out_shape=jax.ShapeDtypeStruct((t, D_MODEL), F32),
        scratch_shapes=[pltpu.VMEM((2, 2, COMB_TM * ROW_TILE, LANES), F32),
                        pltpu.SemaphoreType.DMA((2,))],
        compiler_params=_params(("arbitrary",)),
        name="combine",
    )(d0, d1, d0, d1, x1, gates, ys)


def _layer(x, mem, attn_norm_g, mem_norm_g, w_in, w_gla_gk, b_gla_gk, gla_out_norm_g,
           moba_q_norm_g, moba_k_norm_g, w_mem_kv, mem_q_norm_g, mem_k_norm_g, w_out,
           ffn_norm_g, w_router_group, b_router_group, w_router_expert, b_router_expert,
           w_gate, w_up, w_down):
    b, s, d = x.shape
    t = b * s
    x2 = x.reshape(t, d)
    row = lambda v: v.reshape(1, -1).astype(F32)
    tile_heads = lambda v: jnp.tile(v.astype(F32), HEADS).reshape(1, QK_W)
    hid = np.arange(QK_W) // DH
    seg = jnp.asarray((hid[:, None] == hid[None, :]) / DH, BF16)

    w_in_p = jnp.concatenate([w_in[:, :1536], w_in[:, 1552:], w_in[:, 1536:1552]], axis=1).astype(BF16)
    wr = jnp.concatenate([w_router_group,
                          jnp.transpose(w_router_expert, (1, 0, 2)).reshape(d, N_EXPERTS),
                          jnp.zeros((d, LANES - N_GROUPS - N_EXPERTS), F32)], axis=1).astype(BF16)
    br = jnp.concatenate([b_router_group, b_router_expert.reshape(N_EXPERTS),
                          jnp.zeros((LANES - N_GROUPS - N_EXPERTS,), F32)]).reshape(1, LANES)

    memk, memvt = _mem_kv(mem, row(mem_norm_g), w_mem_kv.astype(BF16), tile_heads(mem_k_norm_g), seg)
    qk, gv, gr, glr, mq, mk, mvt, kmean, cq = _in_proj(
        x2, row(attn_norm_g), w_in_p, seg, tile_heads(moba_q_norm_g), tile_heads(moba_k_norm_g),
        tile_heads(mem_q_norm_g))
    o_gla = _gla(qk, gv, gr, glr, w_gla_gk.astype(BF16), row(b_gla_gk), row(gla_out_norm_g), b, s)
    o_moba, o_mem = _moba_mem(mq, mk, mvt, kmean, cq, memk, memvt, b, s)
    x1, hn, e_ids, gates, counts = _out_router(
        x2, o_gla.reshape(t, GLA_V_W), o_moba.reshape(t, QK_W), o_mem.reshape(t, QK_W),
        w_out.astype(BF16), row(ffn_norm_g), wr, br)

    counts = counts[N_GROUPS:N_GROUPS + N_EXPERTS, 0]
    padded = (counts + ROW_BLOCK - 1) // ROW_BLOCK * ROW_BLOCK
    pends = jnp.cumsum(padded)
    pstarts = pends - padded
    onehot = e_ids[:2, :, None] == jnp.arange(N_EXPERTS, dtype=jnp.int32)
    dest = (jnp.sum(jnp.where(onehot, pstarts, 0), axis=-1) + e_ids[2:4]).astype(jnp.int32)
    n_rows = (t * 2 + N_EXPERTS * (ROW_BLOCK - 1) + ROW_BLOCK - 1) // ROW_BLOCK * ROW_BLOCK
    nb = n_rows // ROW_BLOCK
    block_start = jnp.arange(nb, dtype=jnp.int32) * ROW_BLOCK
    block_expert = jnp.minimum(jnp.sum(block_start[:, None] >= pends[None, :], axis=1),
                               N_EXPERTS - 1).astype(jnp.int32)
    n_used = (pends[-1] // ROW_BLOCK).astype(jnp.int32).reshape(1)
    ended = block_start[:, None] >= pends[None, :]
    block_run = jnp.sum(ended & (counts > 0)[None, :], axis=1).astype(jnp.int32)
    block_next = (jnp.min(jnp.where(ended, n_rows, pends[None, :]), axis=1) // ROW_BLOCK).astype(jnp.int32)

    pad_lo = jnp.concatenate([pstarts + counts, pends[-1:]]).astype(jnp.int32)
    pad_hi = jnp.concatenate([pends, jnp.full((1,), n_rows)]).astype(jnp.int32)
    ys = _experts(hn, dest.reshape(-1), pad_lo, pad_hi, n_rows, block_expert, block_run, block_next, n_used,
                  w_gate, w_up, w_down)
    out = _combine(x1, gates, ys, dest)
    return out.reshape(b, s, d)


def kernel(x, mem, attn_norm_g, mem_norm_g, w_in, w_gla_gk, b_gla_gk, gla_out_norm_g, moba_q_norm_g, moba_k_norm_g, w_mem_kv, mem_q_norm_g, mem_k_norm_g, w_out, ffn_norm_g, w_router_group, b_router_group, w_router_expert, b_router_expert, w_gate, w_up, w_down):
    depth = w_in.shape[0]
    for l in range(depth):
        x = _layer(x, mem, attn_norm_g[l], mem_norm_g[l], w_in[l], w_gla_gk[l], b_gla_gk[l],
                   gla_out_norm_g[l], moba_q_norm_g[l], moba_k_norm_g[l], w_mem_kv[l],
                   mem_q_norm_g[l], mem_k_norm_g[l], w_out[l], ffn_norm_g[l], w_router_group[l],
                   b_router_group[l], w_router_expert[l], b_router_expert[l],
                   w_gate[l], w_up[l], w_down[l])
    return x
```

```python
import functools

import jax
import jax.numpy as jnp
import numpy as np
from jax import lax
from jax.experimental import pallas as pl
from jax.experimental.pallas import tpu as pltpu

F32 = jnp.float32
BF16 = jnp.bfloat16
EPS = 1e-6
NEG_INF = float("-inf")

D_MODEL = 1024
N_MEM = 256
HEADS = 4
DH = 64
GLA_DV = 128
GLA_RANK = 16
GLA_GATE_NORMALIZER = 16.0
GLA_CHUNK = 64
MOBA_BLOCK = 256
MOBA_TOPK = 3
QK_W = HEADS * DH
GLA_V_W = HEADS * GLA_DV
N_GROUPS = 4
EXPERTS_PER_GROUP = 8
N_EXPERTS = N_GROUPS * EXPERTS_PER_GROUP
MOE_FF = 512
LANES = 128
ROW_BLOCK = 256

VMEM_LIMIT = 56 * 1024 * 1024


def _params(sem):
    return pltpu.CompilerParams(dimension_semantics=sem, vmem_limit_bytes=VMEM_LIMIT)


def _nt(a, b):
    return lax.dot_general(a, b, (((1,), (1,)), ((), ())), preferred_element_type=F32)


def _tn(a, b):
    return lax.dot_general(a, b, (((0,), (0,)), ((), ())), preferred_element_type=F32)


def _dot(a, b):
    return jnp.dot(a, b, preferred_element_type=F32)


def _rms_rows(x, g):
    ms = jnp.mean(x * x, axis=-1, keepdims=True)
    return x * lax.rsqrt(ms + EPS) * g


def _split3(x):
    h1 = x.astype(BF16)
    r1 = x - h1.astype(F32)
    h2 = r1.astype(BF16)
    h3 = (r1 - h2.astype(F32)).astype(BF16)
    return h1, h2, h3


def _head_mean_sq(x, seg):
    sq = x * x
    hi = sq.astype(BF16)
    lo = (sq - hi.astype(F32)).astype(BF16)
    return _dot(hi, seg) + _dot(lo, seg)


def _silu(x):
    return x * (1.0 / (1.0 + jnp.exp(-x)))


VT_ROWS = DH + 16
VT_ALL = HEADS * VT_ROWS


ALIBI_SLOPES = tuple(2.0 ** (-8.0 * (h + 1) / HEADS) for h in range(HEADS))


def _vt_with_ones(v, key_slopes=None):
    n = v.shape[0]
    vt = v.T
    ones = jnp.ones((VT_ROWS - DH, n), F32)
    pos = lax.broadcasted_iota(jnp.int32, (1, n), 1).astype(F32)
    parts = []
    for h in range(HEADS):
        scale = 1.0 if key_slopes is None else jnp.exp(key_slopes[h] * (pos - (n - 1)))
        parts += [vt[h * DH:(h + 1) * DH] * scale, ones * scale]
    return jnp.concatenate(parts, axis=0).astype(BF16)


ROW_TILE = D_MODEL // LANES
X_TILE = ROW_TILE // 2


def _store_row_tiles(ref, x):
    n, tiles = x.shape[0], x.shape[1] // LANES
    for c in range(tiles):
        ref[pl.ds(c, n, stride=tiles), :] = x[:, c * LANES:(c + 1) * LANES]


def _load_row_tiles(ref, n):
    tiles = ref.shape[0] // n
    return jnp.concatenate([ref[pl.ds(c, n, stride=tiles), :] for c in range(tiles)], axis=1)


def _pack_bf16_pairs(x):
    w = x.shape[1] // 2
    bits = lambda t: lax.bitcast_convert_type(t.astype(BF16).astype(F32), jnp.uint32)
    return (bits(x[:, w:]) & jnp.uint32(0xFFFF0000)) | (bits(x[:, :w]) >> 16)


def _unpack_bf16_pairs(words):
    lo = lax.bitcast_convert_type(words << 16, F32).astype(BF16)
    hi = lax.bitcast_convert_type(words & jnp.uint32(0xFFFF0000), F32).astype(BF16)
    return lo, hi


def _mem_kv_kernel(mem_ref, g_ref, w_ref, gk_ref, seg_ref, k_ref, vt_ref):
    h = _rms_rows(mem_ref[0], g_ref[...]).astype(BF16)
    kv = _dot(h, w_ref[...])
    k = kv[:, :QK_W]
    kn = k * lax.rsqrt(_head_mean_sq(k, seg_ref[...]) + EPS) * gk_ref[...]
    k_ref[0] = kn.astype(BF16)
    vt_ref[0] = _vt_with_ones(kv[:, QK_W:])


def _mem_kv(mem, g, w_bf, gk_t, seg):
    b = mem.shape[0]
    return pl.pallas_call(
        _mem_kv_kernel,
        grid=(b,),
        in_specs=[
            pl.BlockSpec((1, N_MEM, D_MODEL), lambda i: (i, 0, 0)),
            pl.BlockSpec((1, D_MODEL), lambda i: (0, 0)),
            pl.BlockSpec((D_MODEL, 2 * QK_W), lambda i: (0, 0)),
            pl.BlockSpec((1, QK_W), lambda i: (0, 0)),
            pl.BlockSpec((QK_W, QK_W), lambda i: (0, 0)),
        ],
        out_specs=[
            pl.BlockSpec((1, N_MEM, QK_W), lambda i: (i, 0, 0)),
            pl.BlockSpec((1, VT_ALL, N_MEM), lambda i: (i, 0, 0)),
        ],
        out_shape=[
            jax.ShapeDtypeStruct((b, N_MEM, QK_W), BF16),
            jax.ShapeDtypeStruct((b, VT_ALL, N_MEM), BF16),
        ],
        compiler_params=_params(("arbitrary",)),
        name="mem_kv",
    )(mem, g, w_bf, gk_t, seg)


IN_TM = 1024
_C_QK, _C_V, _C_R, _C_MQ, _C_MK, _C_MV, _C_CQ, _C_LR = 0, 512, 1024, 1536, 1792, 2048, 2304, 2560
D_IN = 2576


def _in_proj_kernel(x_ref, g_ref, w_ref, seg_ref, gq_ref, gk_ref, gc_ref,
                    qk_ref, v_ref, r_ref, lr_ref, mq_ref, mk_ref, mvt_ref, kmean_ref, cq_ref):
    h = _rms_rows(x_ref[...], g_ref[...]).astype(BF16)
    p = _dot(h, w_ref[...])
    seg = seg_ref[...]
    qk_ref[...] = p[:, _C_QK:_C_V]
    v_ref[...] = p[:, _C_V:_C_R].astype(BF16)
    r_ref[...] = p[:, _C_R:_C_MQ]
    lr_ref[...] = p[:, _C_LR:D_IN]

    def head_norm(t, gain):
        return t * lax.rsqrt(_head_mean_sq(t, seg) + EPS) * gain

    scale = DH ** -0.5
    mq_ref[...] = (head_norm(p[:, _C_MQ:_C_MK], gq_ref[...]) * scale).astype(BF16)
    cq_ref[...] = (head_norm(p[:, _C_CQ:_C_LR], gc_ref[...]) * scale).astype(BF16)
    kn = head_norm(p[:, _C_MK:_C_MV], gk_ref[...])
    mk_ref[...] = kn.astype(BF16)
    mv = p[:, _C_MV:_C_CQ]
    for j in range(IN_TM // MOBA_BLOCK):
        rows = slice(j * MOBA_BLOCK, (j + 1) * MOBA_BLOCK)
        kmean_ref[0, j:j + 1, :] = jnp.mean(kn[rows], axis=0, keepdims=True)
        mvt_ref[j] = _vt_with_ones(mv[rows], ALIBI_SLOPES)


def _in_proj(x2, g, w_bf, seg, gq_t, gk_t, gc_t):
    t = x2.shape[0]
    nt = t // IN_TM
    nb = IN_TM // MOBA_BLOCK
    row = lambda w: pl.BlockSpec((IN_TM, w), lambda i: (i, 0))
    const = lambda a, b: pl.BlockSpec((a, b), lambda i: (0, 0))
    return pl.pallas_call(
        _in_proj_kernel,
        grid=(nt,),
        in_specs=[row(D_MODEL), const(1, D_MODEL), const(D_MODEL, D_IN), const(QK_W, QK_W),
                  const(1, QK_W), const(1, QK_W), const(1, QK_W)],
        out_specs=[row(2 * QK_W), row(GLA_V_W), row(GLA_V_W), row(GLA_RANK), row(QK_W), row(QK_W),
                   pl.BlockSpec((nb, VT_ALL, MOBA_BLOCK), lambda i: (i, 0, 0)),
                   pl.BlockSpec((1, nb, QK_W), lambda i: (i, 0, 0)),
                   row(QK_W)],
        out_shape=[
            jax.ShapeDtypeStruct((t, 2 * QK_W), F32),
            jax.ShapeDtypeStruct((t, GLA_V_W), BF16),
            jax.ShapeDtypeStruct((t, GLA_V_W), F32),
            jax.ShapeDtypeStruct((t, GLA_RANK), F32),
            jax.ShapeDtypeStruct((t, QK_W), BF16),
            jax.ShapeDtypeStruct((t, QK_W), BF16),
            jax.ShapeDtypeStruct((t // MOBA_BLOCK, VT_ALL, MOBA_BLOCK), BF16),
            jax.ShapeDtypeStruct((nt, nb, QK_W), F32),
            jax.ShapeDtypeStruct((t, QK_W), BF16),
        ],
        compiler_params=_params(("arbitrary",)),
        name="in_proj",
    )(x2, g, w_bf, seg, gq_t, gk_t, gc_t)


GLA_TC = 256


def _gla_kernel(qk_ref, v_ref, r_ref, lr_ref, wgk_ref, bgk_ref, gn_ref, tri_ref, ones_ref,
                o_ref, st_ref):
    @pl.when(pl.program_id(1) == 0)
    def _():
        st_ref[...] = jnp.zeros_like(st_ref)

    qk = qk_ref[0]
    q = qk[:, :QK_W]
    k = qk[:, QK_W:]
    gk = _dot(lr_ref[0].astype(BF16), wgk_ref[...]) + bgk_ref[...]
    g = -(jnp.maximum(-gk, 0.0) + jnp.log1p(jnp.exp(-jnp.abs(gk)))) / GLA_GATE_NORMALIZER
    g1, g2, g3 = _split3(g)
    tri = tri_ref[...]
    ones = ones_ref[...]
    cum = _dot(tri, g1) + _dot(tri, g2) + _dot(tri, g3)
    tot = _dot(ones, g1) + _dot(ones, g2) + _dot(ones, g3)
    q_dec = (q * (DH ** -0.5) * jnp.exp(cum)).astype(BF16)
    k_inv = (k * jnp.exp(-cum)).astype(BF16)
    k_end = (k * jnp.exp(tot - cum)).astype(BF16)
    decay = jnp.exp(tot)

    lane_head = lax.broadcasted_iota(jnp.int32, (GLA_TC, QK_W), 1) // DH
    row_t = lax.broadcasted_iota(jnp.int32, (GLA_TC, GLA_TC), 0)
    col_t = lax.broadcasted_iota(jnp.int32, (GLA_TC, GLA_TC), 1)
    causal = (row_t >= col_t) & (row_t // GLA_CHUNK == col_t // GLA_CHUNK)
    same_head = (lax.broadcasted_iota(jnp.int32, (GLA_V_W, QK_W), 0) // GLA_DV
                 == lax.broadcasted_iota(jnp.int32, (GLA_V_W, QK_W), 1) // DH)
    gain = gn_ref[...]

    v = v_ref[0]
    chunks = [slice(c * GLA_CHUNK, (c + 1) * GLA_CHUNK) for c in range(GLA_TC // GLA_CHUNK)]
    d_st = [jnp.where(same_head, _tn(v[rows], k_end[rows]), 0.0) for rows in chunks]
    states = [st_ref[...]]
    for c, rows in enumerate(chunks):
        states.append(states[c] * decay[rows.start:rows.start + 1, :] + d_st[c])
    st_ref[...] = states[-1]
    o_inter = jnp.concatenate([_nt(q_dec[rows], states[c].astype(BF16)) for c, rows in enumerate(chunks)],
                              axis=0)
    q_stack = jnp.concatenate([jnp.where(lane_head == h, q_dec, jnp.zeros_like(q_dec)) for h in range(HEADS)],
                              axis=0)
    a_all = _nt(q_stack, k_inv)
    outs = []
    for h in range(HEADS):
        a = jnp.where(causal, a_all[h * GLA_TC:(h + 1) * GLA_TC], 0.0).astype(BF16)
        vs = slice(h * GLA_DV, (h + 1) * GLA_DV)
        oh = _dot(a, v[:, vs]) + o_inter[:, vs]
        outs.append(_rms_rows(oh, gain) * _silu(r_ref[0, :, vs]))
    o_ref[0] = jnp.concatenate(outs, axis=-1).astype(BF16)


def _gla(qk, v, r, lr, wgk_bf, bgk, gn, b, s):
    ns = s // GLA_TC
    idx = np.arange(GLA_TC)
    same_chunk = (idx[:, None] // GLA_CHUNK) == (idx[None, :] // GLA_CHUNK)
    tri = jnp.asarray(same_chunk & (idx[:, None] >= idx[None, :]), BF16)
    ones = jnp.asarray(same_chunk, BF16)
    seq = lambda w: pl.BlockSpec((1, GLA_TC, w), lambda i, j: (i, j, 0))
    const = lambda a, c: pl.BlockSpec((a, c), lambda i, j: (0, 0))
    return pl.pallas_call(
        _gla_kernel,
        grid=(b, ns),
        in_specs=[seq(2 * QK_W), seq(GLA_V_W), seq(GLA_V_W), seq(GLA_RANK),
                  const(GLA_RANK, QK_W), const(1, QK_W), const(1, GLA_DV),
                  const(GLA_TC, GLA_TC), const(GLA_TC, GLA_TC)],
        out_specs=seq(GLA_V_W),
        out_shape=jax.ShapeDtypeStruct((b, s, GLA_V_W), BF16),
        scratch_shapes=[pltpu.VMEM((GLA_V_W, QK_W), F32)],
        compiler_params=_params(("arbitrary", "arbitrary")),
        name="gla",
    )(qk.reshape(b, s, 2 * QK_W), v.reshape(b, s, GLA_V_W), r.reshape(b, s, GLA_V_W),
      lr.reshape(b, s, GLA_RANK), wgk_bf, bgk, gn, tri, ones)


TQ = MOBA_BLOCK
KV_UNROLL = 8


def _moba_mem_kernel(q_ref, k_ref, vt_ref, kmean_ref, cq_ref, mk_ref, mvt_ref,
                     o_moba_ref, o_mem_ref,
                     qcat_ref, selb_ref, s_even, s_odd, acc_ref, *, n_blocks):
    s_slots = (s_even, s_odd)
    i = pl.program_id(1)
    lane_head = lax.broadcasted_iota(jnp.int32, (TQ, QK_W), 1) // DH
    dist0 = (lax.broadcasted_iota(jnp.int32, (MOBA_BLOCK, TQ), 1)
             - lax.broadcasted_iota(jnp.int32, (MOBA_BLOCK, TQ), 0)).astype(F32)
    slopes = ALIBI_SLOPES
    heads = [slice(h * TQ, (h + 1) * TQ) for h in range(HEADS)]
    vrows = [slice(h * VT_ROWS, (h + 1) * VT_ROWS) for h in range(HEADS)]

    def stack_heads(x):
        return jnp.concatenate([jnp.where(lane_head == h, x, jnp.zeros_like(x)) for h in range(HEADS)], axis=0)

    def finish(accl):
        return accl[:DH] * (1.0 / accl[DH:DH + 1])

    s_even[...] = _nt(mk_ref[0], stack_heads(cq_ref[0]))
    outs = []
    for h in range(HEADS):
        s = s_even[:, heads[h]]
        p = jnp.exp(s - jnp.max(s, axis=0, keepdims=True)).astype(BF16)
        outs.append(finish(_dot(mvt_ref[0, vrows[h], :], p)))
    o_mem_ref[0] = jnp.concatenate(outs, axis=0).T.astype(BF16)

    qcat_ref[...] = stack_heads(q_ref[0])
    k_own = k_ref[0, pl.ds(pl.multiple_of(i * MOBA_BLOCK, MOBA_BLOCK), MOBA_BLOCK), :]
    gate_all = _nt(kmean_ref[0].astype(BF16), qcat_ref[...])
    s_odd[...] = _nt(k_own, qcat_ref[...])
    blk = lax.broadcasted_iota(jnp.int32, (n_blocks, TQ), 0)
    blk_f = blk.astype(F32)

    tq = (lax.broadcasted_iota(jnp.int32, (1, TQ), 1) - (MOBA_BLOCK - 1)).astype(F32)
    init = []
    for h in range(HEADS):
        gate = jnp.where(blk < i, gate_all[:, heads[h]], NEG_INF)
        chosen = jnp.zeros((n_blocks, TQ), jnp.bool_)
        for r in range(MOBA_TOPK):
            mx = jnp.max(gate, axis=0, keepdims=True)
            first = jnp.min(jnp.where(gate == mx, blk_f, float(n_blocks)), axis=0, keepdims=True)
            hit = blk_f == first
            chosen = chosen | (hit & (mx > NEG_INF))
            gate = jnp.where(hit, NEG_INF, gate)
        selb_ref[h] = jnp.where(chosen, 0.0, NEG_INF)
        aq = -slopes[h] * tq
        s = jnp.where(dist0 >= 0.0, s_odd[:, heads[h]], NEG_INF)
        m0 = jnp.max(s, axis=0, keepdims=True) + aq
        p = jnp.exp(s - (m0 - aq)).astype(BF16)
        acc_ref[h] = _dot(vt_ref[i, vrows[h], :], p)
        init += [m0]

    def stage_scores(j, slot):
        kj = k_ref[0, pl.ds(pl.multiple_of(j * MOBA_BLOCK, MOBA_BLOCK), MOBA_BLOCK), :]
        s_slots[slot][...] = _nt(kj, qcat_ref[...])

    stage_scores(0, 0)

    row0 = jnp.minimum(i, 0) * MOBA_BLOCK

    def past_blocks(g, carry, unroll, base):
        carry = list(carry)
        for u in range(unroll):
            j = base + g * unroll + u
            stage_scores(jnp.minimum(j + 1, n_blocks - 1), (u + 1) % 2)
            off = jnp.full((1, TQ), (i - j) * MOBA_BLOCK, jnp.int32).astype(F32) + tq
            for h in range(HEADS):
                m = carry[h]
                s1 = s_slots[u % 2][:, heads[h]]
                rb = selb_ref[h, pl.ds(j, 1), :] - slopes[h] * off
                m_new = jnp.maximum(m, jnp.max(s1, axis=0, keepdims=True) + rb)
                s2 = s_slots[u % 2][pl.ds(pl.multiple_of(row0, MOBA_BLOCK), MOBA_BLOCK), heads[h]]
                p = jnp.exp(s2 - (m_new - rb)).astype(BF16)
                acc_ref[h] = jnp.exp(m - m_new) * acc_ref[h] + _dot(vt_ref[j, vrows[h], :], p)
                carry[h] = m_new
        return tuple(carry)

    n_main = i // KV_UNROLL
    rest = i - n_main * KV_UNROLL
    carry = lax.fori_loop(0, n_main, functools.partial(past_blocks, unroll=KV_UNROLL, base=0), tuple(init))
    carry = lax.fori_loop(0, rest // 2,
                          functools.partial(past_blocks, unroll=2, base=n_main * KV_UNROLL), carry)
    final = lax.fori_loop(0, rest % 2, functools.partial(past_blocks, unroll=1, base=i - 1), carry)
    o_t = jnp.concatenate([finish(acc_ref[h]) for h in range(HEADS)], axis=0)
    o_moba_ref[0] = o_t.T.astype(BF16)


def _moba_mem(mq, mk, mvt, kmean, cq, memk, memvt, b, s):
    nq = s // TQ
    n_blocks = s // MOBA_BLOCK
    qspec = pl.BlockSpec((1, TQ, QK_W), lambda i, j: (i, j, 0))
    return pl.pallas_call(
        functools.partial(_moba_mem_kernel, n_blocks=n_blocks),
        grid=(b, nq),
        in_specs=[
            qspec,
            pl.BlockSpec((1, s, QK_W), lambda i, j: (i, 0, 0)),
            pl.BlockSpec((n_blocks, VT_ALL, MOBA_BLOCK), lambda i, j: (i, 0, 0)),
            pl.BlockSpec((1, n_blocks, QK_W), lambda i, j: (i, 0, 0)),
            qspec,
            pl.BlockSpec((1, N_MEM, QK_W), lambda i, j: (i, 0, 0)),
            pl.BlockSpec((1, VT_ALL, N_MEM), lambda i, j: (i, 0, 0)),
        ],
        out_specs=[qspec, qspec],
        out_shape=[jax.ShapeDtypeStruct((b, s, QK_W), BF16), jax.ShapeDtypeStruct((b, s, QK_W), BF16)],
        scratch_shapes=[
            pltpu.VMEM((HEADS * TQ, QK_W), BF16),
            pltpu.VMEM((HEADS, n_blocks, TQ), F32),
            pltpu.VMEM((MOBA_BLOCK, HEADS * TQ), F32),
            pltpu.VMEM((MOBA_BLOCK, HEADS * TQ), F32),
            pltpu.VMEM((HEADS, VT_ROWS, TQ), F32),
        ],
        compiler_params=_params(("arbitrary", "arbitrary")),
        name="moba_mem",
    )(mq.reshape(b, s, QK_W), mk.reshape(b, s, QK_W), mvt, kmean.reshape(b, n_blocks, QK_W),
      cq.reshape(b, s, QK_W), memk, memvt)


OUT_TM = 1024
ROUTE_ROWS = 40


def _out_router_kernel(x_ref, og_ref, om_ref, oc_ref, w_ref, g_ref, wr_ref, br_ref, tri_ref,
                       x1_ref, h_ref, e_ref, gate_ref, count_ref, carry_ref):
    w = w_ref[...]
    x1 = (x_ref[...] + _dot(og_ref[...], w[:GLA_V_W]) + _dot(om_ref[...], w[GLA_V_W:GLA_V_W + QK_W])
          + _dot(oc_ref[...], w[GLA_V_W + QK_W:]))
    x1_ref[...] = x1
    hn = _rms_rows(x1, g_ref[...])
    _store_row_tiles(h_ref, _pack_bf16_pairs(hn))
    logits = _dot(hn.astype(BF16), wr_ref[...]) + br_ref[...]
    lt = logits.T[:ROUTE_ROWS]
    row = lax.broadcasted_iota(jnp.int32, lt.shape, 0).astype(F32)
    lg = jnp.where(row < N_GROUPS, lt, NEG_INF)
    mg = jnp.max(lg, axis=0, keepdims=True)
    g_sel = jnp.min(jnp.where(lg == mg, row, float(LANES)), axis=0, keepdims=True)
    p_group = 1.0 / jnp.sum(jnp.exp(lg - mg), axis=0, keepdims=True)
    lo = N_GROUPS + g_sel * EXPERTS_PER_GROUP
    le = jnp.where((row >= lo) & (row < lo + EXPERTS_PER_GROUP), lt, NEG_INF)
    m0 = jnp.max(le, axis=0, keepdims=True)
    i0 = jnp.min(jnp.where(le == m0, row, float(LANES)), axis=0, keepdims=True)
    le1 = jnp.where(row == i0, NEG_INF, le)
    m1 = jnp.max(le1, axis=0, keepdims=True)
    i1 = jnp.min(jnp.where(le1 == m1, row, float(LANES)), axis=0, keepdims=True)
    z = jnp.exp(m1 - m0)
    w0 = p_group / (1.0 + z)
    w1 = p_group * z / (1.0 + z)
    pad = jnp.zeros((8 - 2, lt.shape[1]), F32)
    gate_ref[...] = jnp.concatenate([w0, w1, pad], axis=0)

    @pl.when(pl.program_id(0) == 0)
    def _():
        carry_ref[...] = jnp.zeros_like(carry_ref)

    oh0 = (row == i0).astype(F32)
    oh1 = (row == i1).astype(F32)
    both = oh0 + oh1
    before = _dot(both.astype(BF16), tri_ref[...]) + carry_ref[:, 0:1]
    r0 = jnp.sum(oh0 * before, axis=0, keepdims=True)
    r1 = jnp.sum(oh1 * before, axis=0, keepdims=True)
    carry_ref[...] = carry_ref[...] + jnp.sum(both, axis=1, keepdims=True)
    count_ref[...] = carry_ref[...].astype(jnp.int32)
    e_ref[...] = jnp.concatenate([i0 - N_GROUPS, i1 - N_GROUPS, r0, r1, pad[:4]], axis=0).astype(jnp.int32)


def _out_router(x2, og, om, oc, w_bf, g, wr_bf, br):
    t = x2.shape[0]
    row = lambda w: pl.BlockSpec((OUT_TM, w), lambda i: (i, 0))
    const = lambda a, b: pl.BlockSpec((a, b), lambda i: (0, 0))
    idx = np.arange(OUT_TM)
    tri = jnp.asarray(idx[:, None] < idx[None, :], BF16)
    col = lambda r: pl.BlockSpec((r, OUT_TM), lambda i: (0, i))
    return pl.pallas_call(
        _out_router_kernel,
        grid=(t // OUT_TM,),
        in_specs=[row(D_MODEL), row(GLA_V_W), row(QK_W), row(QK_W), const(D_MODEL, D_MODEL),
                  const(1, D_MODEL), const(D_MODEL, LANES), const(1, LANES), const(OUT_TM, OUT_TM)],
        out_specs=[row(D_MODEL), pl.BlockSpec((OUT_TM * X_TILE, LANES), lambda i: (i, 0)),
                   col(8), col(8), const(ROUTE_ROWS, LANES)],
        out_shape=[
            jax.ShapeDtypeStruct((t, D_MODEL), F32),
            jax.ShapeDtypeStruct((t * X_TILE, LANES), jnp.uint32),
            jax.ShapeDtypeStruct((8, t), jnp.int32),
            jax.ShapeDtypeStruct((8, t), F32),
            jax.ShapeDtypeStruct((ROUTE_ROWS, LANES), jnp.int32),
        ],
        scratch_shapes=[pltpu.VMEM((ROUTE_ROWS, LANES), F32)],
        compiler_params=_params(("arbitrary",)),
        name="out_router",
    )(x2, og, om, oc, w_bf, g, wr_bf, br, tri)


DMA_UNROLL = 8
PLACE_UNROLL = 32


def _row_to_token(dest_ref, pad_lo_ref, pad_hi_ref, tok_ref):
    def pad_segment(e, c):
        hi = pad_hi_ref[e]

        def pad(g, c2):
            for r in range(8):
                tok_ref[hi - 8 * (g + 1) + r] = 0
            return c2
        lax.fori_loop(0, lax.shift_right_logical(hi - pad_lo_ref[e] + 7, 3), pad, 0)
        return c

    lax.fori_loop(0, pad_lo_ref.shape[0], pad_segment, 0)

    n_tok = dest_ref.shape[0] // 2

    def place(t, c):
        tok_ref[dest_ref[t]] = t
        tok_ref[dest_ref[n_tok + t]] = t
        return c

    lax.fori_loop(0, n_tok, place, 0, unroll=PLACE_UNROLL)


def _experts_kernel(be_ref, run_ref, next_ref, nused_ref, dest_ref, pad_lo_ref, pad_hi_ref,
                    hn_hbm, wg_hbm, wu_hbm, wd_hbm, ys_hbm,
                    hn_vmem, xg_even, xg_odd, y_even, y_odd, wg_f, wu_f, wd_f, wg_bf, wu_bf, wd_bf, tok_ref,
                    hn_sem, w_sem, y_sem):
    n_used = nused_ref[0]
    n_blocks = be_ref.shape[0]
    block_rows = ROW_BLOCK * ROW_TILE

    def weight_copies(e, slot):
        return [pltpu.make_async_copy(src.at[e], dst.at[slot], w_sem.at[slot])
                for src, dst in ((wg_hbm, wg_f), (wu_hbm, wu_f), (wd_hbm, wd_f))]

    def y_copy(buf, parity, i):
        dst = ys_hbm.at[pl.ds(pl.multiple_of(i * block_rows, block_rows), block_rows), :]
        return pltpu.make_async_copy(buf, dst, y_sem.at[parity])

    def gather(i, dst):
        base = jnp.minimum(i, n_blocks - 1) * ROW_BLOCK
        for r in range(ROW_BLOCK):
            src = pl.ds(pl.multiple_of(tok_ref[base + r] * X_TILE, X_TILE), X_TILE)
            dst[r * X_TILE:(r + 1) * X_TILE, :] = hn_vmem[src, :]

    def block(i, parity, cur, nxt, ybuf):
        @pl.when(jnp.logical_or(i == 0, be_ref[i] != be_ref[jnp.maximum(i - 1, 0)]))
        def _():
            slot = jnp.bitwise_and(run_ref[i], 1)
            for c in weight_copies(be_ref[i], slot):
                c.wait()
            wg_bf[...] = wg_f[slot].astype(BF16)
            wu_bf[...] = wu_f[slot].astype(BF16)
            wd_bf[...] = wd_f[slot].astype(BF16)

            @pl.when(next_ref[i] < n_used)
            def _():
                for c in weight_copies(be_ref[jnp.minimum(next_ref[i], n_blocks - 1)], 1 - slot):
                    c.start()

        @pl.when(i >= 2)
        def _():
            y_copy(ybuf, parity, i).wait()

        kh = D_MODEL // 2
        gather(i + 1, nxt)
        x_lo, x_hi = _unpack_bf16_pairs(_load_row_tiles(cur, ROW_BLOCK))
        gate = _dot(x_lo, wg_bf[:kh, :]) + _dot(x_hi, wg_bf[kh:, :])
        up = _dot(x_lo, wu_bf[:kh, :]) + _dot(x_hi, wu_bf[kh:, :])
        hid = (_silu(gate) * up).astype(BF16)
        _store_row_tiles(ybuf, _dot(hid, wd_bf[...]))
        y_copy(ybuf, parity, i).start()

    hn_load = pltpu.make_async_copy(hn_hbm, hn_vmem, hn_sem)
    hn_load.start()
    for c in weight_copies(be_ref[0], 0):
        c.start()
    _row_to_token(dest_ref, pad_lo_ref, pad_hi_ref, tok_ref)
    hn_load.wait()
    gather(0, xg_even)

    def pair(p, carry):
        block(2 * p, 0, xg_even, xg_odd, y_even)

        @pl.when(2 * p + 1 < n_used)
        def _():
            block(2 * p + 1, 1, xg_odd, xg_even, y_odd)
        return carry

    lax.fori_loop(0, lax.shift_right_logical(n_used + 1, 1), pair, 0)

    y_copy(y_even, 0, 0).wait()

    @pl.when(n_used >= 2)
    def _():
        y_copy(y_odd, 1, 0).wait()

    y_even[...] = jnp.zeros_like(y_even)

    def zero_start(i, carry):
        y_copy(y_even, 0, i).start()
        return carry

    def zero_wait(i, carry):
        y_copy(y_even, 0, i).wait()
        return carry

    lax.fori_loop(n_used, n_blocks, zero_start, 0)
    lax.fori_loop(n_used, n_blocks, zero_wait, 0)


def _experts(hn, dest, pad_lo, pad_hi, n_rows, block_expert, block_run, block_next, n_used,
             w_gate, w_up, w_down):
    hbm = pl.BlockSpec(memory_space=pl.ANY)
    grid_spec = pltpu.PrefetchScalarGridSpec(
        num_scalar_prefetch=7,
        grid=(1,),
        in_specs=[hbm, hbm, hbm, hbm],
        out_specs=hbm,
        scratch_shapes=[pltpu.VMEM(hn.shape, jnp.uint32),
                        pltpu.VMEM((ROW_BLOCK * X_TILE, LANES), jnp.uint32),
                        pltpu.VMEM((ROW_BLOCK * X_TILE, LANES), jnp.uint32),
                        pltpu.VMEM((ROW_BLOCK * ROW_TILE, LANES), F32),
                        pltpu.VMEM((ROW_BLOCK * ROW_TILE, LANES), F32),
                        pltpu.VMEM((2, D_MODEL, MOE_FF), F32), pltpu.VMEM((2, D_MODEL, MOE_FF), F32),
                        pltpu.VMEM((2, MOE_FF, D_MODEL), F32),
                        pltpu.VMEM((D_MODEL, MOE_FF), BF16), pltpu.VMEM((D_MODEL, MOE_FF), BF16),
                        pltpu.VMEM((MOE_FF, D_MODEL), BF16),
                        pltpu.SMEM((n_rows,), jnp.int32),
                        pltpu.SemaphoreType.DMA, pltpu.SemaphoreType.DMA((2,)), pltpu.SemaphoreType.DMA((2,))],
    )
    return pl.pallas_call(
        _experts_kernel,
        grid_spec=grid_spec,
        out_shape=jax.ShapeDtypeStruct((n_rows * ROW_TILE, LANES), F32),
        compiler_params=_params(("arbitrary",)),
        name="experts",
    )(block_expert, block_run, block_next, n_used, dest, pad_lo, pad_hi, hn, w_gate, w_up, w_down)


COMB_TM = 256


def _combine_kernel(d0_ref, d1_ref, d0_next_ref, d1_next_ref, x1_ref, gate_ref, ys_ref, out_ref, ybuf, sems):
    g = pl.program_id(0)

    def row_copy(d, slot, k, r):
        src = ys_ref.at[pl.ds(pl.multiple_of(d * ROW_TILE, ROW_TILE), ROW_TILE), :]
        dst = ybuf.at[slot, k, pl.ds(pl.multiple_of(r * ROW_TILE, ROW_TILE), ROW_TILE), :]
        return pltpu.make_async_copy(src, dst, sems.at[slot])

    def gather(drefs, base, slot):
        def issue(r, c):
            row_copy(drefs[0][0, 0, base + r], slot, 0, r).start(priority=0)
            row_copy(drefs[1][0, 0, base + r], slot, 1, r).start(priority=1)
            return c
        lax.fori_loop(0, COMB_TM, issue, 0, unroll=DMA_UNROLL)

    dest_ref = (d0_ref, d1_ref)
    dest_next_ref = (d0_next_ref, d1_next_ref)

    def finish(slot):
        for k in range(2):
            pltpu.make_async_copy(ys_ref.at[pl.ds(0, COMB_TM * ROW_TILE), :], ybuf.at[slot, k],
                                  sems.at[slot]).wait()
        rows = slice(slot * COMB_TM, (slot + 1) * COMB_TM)
        gate = gate_ref[:, rows].T
        w0 = gate[:, 0:1]
        w1 = gate[:, 1:2]
        y0 = _load_row_tiles(ybuf.at[slot, 0], COMB_TM)
        y1 = _load_row_tiles(ybuf.at[slot, 1], COMB_TM)
        out_ref[rows, :] = x1_ref[rows, :] + (y0 * w0 + y1 * w1)

    @pl.when(g == 0)
    def _():
        gather(dest_ref, 0, 0)

    gather(dest_ref, COMB_TM, 1)
    finish(0)

    @pl.when(g + 1 < pl.num_programs(0))
    def _():
        gather(dest_next_ref, 0, 0)

    finish(1)


def _combine(x1, gates, ys, dest):
    t = x1.shape[0]
    ng = t // (2 * COMB_TM)
    d0, d1 = (dest[k].reshape(ng, 1, 2 * COMB_TM) for k in range(2))
    smem = lambda f: pl.BlockSpec((1, 1, 2 * COMB_TM), f, memory_space=pltpu.SMEM)
    cur = lambda i: (i, 0, 0)
    nxt = lambda i: (jnp.minimum(i + 1, ng - 1), 0, 0)
    return pl.pallas_call(
        _combine_kernel,
        grid=(ng,),
        in_specs=[smem(cur), smem(cur), smem(nxt), smem(nxt),
                  pl.BlockSpec((2 * COMB_TM, D_MODEL), lambda i: (i, 0)),
                  pl.BlockSpec((8, 2 * COMB_TM), lambda i: (0, i)),
                  pl.BlockSpec(memory_space=pl.ANY)],
        out_specs=pl.BlockSpec((2 * COMB_TM, D_MODEL), lambda i: (i, 0)),
        out_shape=jax.ShapeDtypeStruct((t, D_MODEL), F32),
        scratch_shapes=[pltpu.VMEM((2, 2, COMB_TM * ROW_TILE, LANES), F32),
                        pltpu.SemaphoreType.DMA((2,))],
        compiler_params=_params(("arbitrary",)),
        name="combine",
    )(d0, d1, d0, d1, x1, gates, ys)


def _layer(x, mem, attn_norm_g, mem_norm_g, w_in, w_gla_gk, b_gla_gk, gla_out_norm_g,
           moba_q_norm_g, moba_k_norm_g, w_mem_kv, mem_q_norm_g, mem_k_norm_g, w_out,
           ffn_norm_g, w_router_group, b_router_group, w_router_expert, b_router_expert,
           w_gate, w_up, w_down):
    b, s, d = x.shape
    t = b * s
    x2 = x.reshape(t, d)
    row = lambda v: v.reshape(1, -1).astype(F32)
    tile_heads = lambda v: jnp.tile(v.astype(F32), HEADS).reshape(1, QK_W)
    hid = np.arange(QK_W) // DH
    seg = jnp.asarray((hid[:, None] == hid[None, :]) / DH, BF16)

    w_in_p = jnp.concatenate([w_in[:, :1536], w_in[:, 1552:], w_in[:, 1536:1552]], axis=1).astype(BF16)
    wr = jnp.concatenate([w_router_group,
                          jnp.transpose(w_router_expert, (1, 0, 2)).reshape(d, N_EXPERTS),
                          jnp.zeros((d, LANES - N_GROUPS - N_EXPERTS), F32)], axis=1).astype(BF16)
    br = jnp.concatenate([b_router_group, b_router_expert.reshape(N_EXPERTS),
                          jnp.zeros((LANES - N_GROUPS - N_EXPERTS,), F32)]).reshape(1, LANES)

    memk, memvt = _mem_kv(mem, row(mem_norm_g), w_mem_kv.astype(BF16), tile_heads(mem_k_norm_g), seg)
    qk, gv, gr, glr, mq, mk, mvt, kmean, cq = _in_proj(
        x2, row(attn_norm_g), w_in_p, seg, tile_heads(moba_q_norm_g), tile_heads(moba_k_norm_g),
        tile_heads(mem_q_norm_g))
    o_gla = _gla(qk, gv, gr, glr, w_gla_gk.astype(BF16), row(b_gla_gk), row(gla_out_norm_g), b, s)
    o_moba, o_mem = _moba_mem(mq, mk, mvt, kmean, cq, memk, memvt, b, s)
    x1, hn, e_ids, gates, counts = _out_router(
        x2, o_gla.reshape(t, GLA_V_W), o_moba.reshape(t, QK_W), o_mem.reshape(t, QK_W),
        w_out.astype(BF16), row(ffn_norm_g), wr, br)

    counts = counts[N_GROUPS:N_GROUPS + N_EXPERTS, 0]
    padded = (counts + ROW_BLOCK - 1) // ROW_BLOCK * ROW_BLOCK
    pends = jnp.cumsum(padded)
    pstarts = pends - padded
    onehot = e_ids[:2, :, None] == jnp.arange(N_EXPERTS, dtype=jnp.int32)
    dest = (jnp.sum(jnp.where(onehot, pstarts, 0), axis=-1) + e_ids[2:4]).astype(jnp.int32)
    n_rows = (t * 2 + N_EXPERTS * (ROW_BLOCK - 1) + ROW_BLOCK - 1) // ROW_BLOCK * ROW_BLOCK
    nb = n_rows // ROW_BLOCK
    block_start = jnp.arange(nb, dtype=jnp.int32) * ROW_BLOCK
    block_expert = jnp.minimum(jnp.sum(block_start[:, None] >= pends[None, :], axis=1),
                               N_EXPERTS - 1).astype(jnp.int32)
    n_used = (pends[-1] // ROW_BLOCK).astype(jnp.int32).reshape(1)
    ended = block_start[:, None] >= pends[None, :]
    block_run = jnp.sum(ended & (counts > 0)[None, :], axis=1).astype(jnp.int32)
    block_next = (jnp.min(jnp.where(ended, n_rows, pends[None, :]), axis=1) // ROW_BLOCK).astype(jnp.int32)

    pad_lo = jnp.concatenate([pstarts + counts, pends[-1:]]).astype(jnp.int32)
    pad_hi = jnp.concatenate([pends, jnp.full((1,), n_rows)]).astype(jnp.int32)
    ys = _experts(hn, dest.reshape(-1), pad_lo, pad_hi, n_rows, block_expert, block_run, block_next, n_used,
                  w_gate, w_up, w_down)
    out = _combine(x1, gates, ys, dest)
    return out.reshape(b, s, d)


def kernel(x, mem, attn_norm_g, mem_norm_g, w_in, w_gla_gk, b_gla_gk, gla_out_norm_g, moba_q_norm_g, moba_k_norm_g, w_mem_kv, mem_q_norm_g, mem_k_norm_g, w_out, ffn_norm_g, w_router_group, b_router_group, w_router_expert, b_router_expert, w_gate, w_up, w_down):
    depth = w_in.shape[0]
    for l in range(depth):
        x = _layer(x, mem, attn_norm_g[l], mem_norm_g[l], w_in[l], w_gla_gk[l], b_gla_gk[l],
                   gla_out_norm_g[l], moba_q_norm_g[l], moba_k_norm_g[l], w_mem_kv[l],
                   mem_q_norm_g[l], mem_k_norm_g[l], w_out[l], ffn_norm_g[l], w_router_group[l],
                   b_router_group[l], w_router_expert[l], b_router_expert[l],
                   w_gate[l], w_up[l], w_down[l])
    return x
```

```python
import functools

import jax
import jax.numpy as jnp
import numpy as np
from jax import lax
from jax.experimental import pallas as pl
from jax.experimental.pallas import tpu as pltpu

F32 = jnp.float32
BF16 = jnp.bfloat16
EPS = 1e-6
NEG_INF = float("-inf")

D_MODEL = 1024
N_MEM = 256
HEADS = 4
DH = 64
GLA_DV = 128
GLA_RANK = 16
GLA_GATE_NORMALIZER = 16.0
GLA_CHUNK = 64
MOBA_BLOCK = 256
MOBA_TOPK = 3
QK_W = HEADS * DH
GLA_V_W = HEADS * GLA_DV
N_GROUPS = 4
EXPERTS_PER_GROUP = 8
N_EXPERTS = N_GROUPS * EXPERTS_PER_GROUP
MOE_FF = 512
LANES = 128
ROW_BLOCK = 256

VMEM_LIMIT = 56 * 1024 * 1024


def _params(sem):
    return pltpu.CompilerParams(dimension_semantics=sem, vmem_limit_bytes=VMEM_LIMIT)


def _nt(a, b):
    return lax.dot_general(a, b, (((1,), (1,)), ((), ())), preferred_element_type=F32)


def _tn(a, b):
    return lax.dot_general(a, b, (((0,), (0,)), ((), ())), preferred_element_type=F32)


def _dot(a, b):
    return jnp.dot(a, b, preferred_element_type=F32)


def _rms_rows(x, g):
    ms = jnp.mean(x * x, axis=-1, keepdims=True)
    return x * lax.rsqrt(ms + EPS) * g


def _split3(x):
    h1 = x.astype(BF16)
    r1 = x - h1.astype(F32)
    h2 = r1.astype(BF16)
    h3 = (r1 - h2.astype(F32)).astype(BF16)
    return h1, h2, h3


def _head_mean_sq(x, seg):
    sq = x * x
    hi = sq.astype(BF16)
    lo = (sq - hi.astype(F32)).astype(BF16)
    return _dot(hi, seg) + _dot(lo, seg)


def _silu(x):
    return x * (1.0 / (1.0 + jnp.exp(-x)))


VT_ROWS = DH + 16
VT_ALL = HEADS * VT_ROWS


ALIBI_SLOPES = tuple(2.0 ** (-8.0 * (h + 1) / HEADS) for h in range(HEADS))


def _vt_with_ones(v, key_slopes=None):
    n = v.shape[0]
    vt = v.T
    ones = jnp.ones((VT_ROWS - DH, n), F32)
    pos = lax.broadcasted_iota(jnp.int32, (1, n), 1).astype(F32)
    parts = []
    for h in range(HEADS):
        scale = 1.0 if key_slopes is None else jnp.exp(key_slopes[h] * (pos - (n - 1)))
        parts += [vt[h * DH:(h + 1) * DH] * scale, ones * scale]
    return jnp.concatenate(parts, axis=0).astype(BF16)


ROW_TILE = D_MODEL // LANES
X_TILE = ROW_TILE // 2


def _store_row_tiles(ref, x):
    n, tiles = x.shape[0], x.shape[1] // LANES
    for c in range(tiles):
        ref[pl.ds(c, n, stride=tiles), :] = x[:, c * LANES:(c + 1) * LANES]


def _load_row_tiles(ref, n):
    tiles = ref.shape[0] // n
    return jnp.concatenate([ref[pl.ds(c, n, stride=tiles), :] for c in range(tiles)], axis=1)


def _pack_bf16_pairs(x):
    w = x.shape[1] // 2
    bits = lambda t: lax.bitcast_convert_type(t.astype(BF16).astype(F32), jnp.uint32)
    return (bits(x[:, w:]) & jnp.uint32(0xFFFF0000)) | (bits(x[:, :w]) >> 16)


def _unpack_bf16_pairs(words):
    lo = lax.bitcast_convert_type(words << 16, F32).astype(BF16)
    hi = lax.bitcast_convert_type(words & jnp.uint32(0xFFFF0000), F32).astype(BF16)
    return lo, hi


def _mem_kv_kernel(mem_ref, g_ref, w_ref, gk_ref, seg_ref, k_ref, vt_ref):
    h = _rms_rows(mem_ref[0], g_ref[...]).astype(BF16)
    kv = _dot(h, w_ref[...])
    k = kv[:, :QK_W]
    kn = k * lax.rsqrt(_head_mean_sq(k, seg_ref[...]) + EPS) * gk_ref[...]
    k_ref[0] = kn.astype(BF16)
    vt_ref[0] = _vt_with_ones(kv[:, QK_W:])


def _mem_kv(mem, g, w_bf, gk_t, seg):
    b = mem.shape[0]
    return pl.pallas_call(
        _mem_kv_kernel,
        grid=(b,),
        in_specs=[
            pl.BlockSpec((1, N_MEM, D_MODEL), lambda i: (i, 0, 0)),
            pl.BlockSpec((1, D_MODEL), lambda i: (0, 0)),
            pl.BlockSpec((D_MODEL, 2 * QK_W), lambda i: (0, 0)),
            pl.BlockSpec((1, QK_W), lambda i: (0, 0)),
            pl.BlockSpec((QK_W, QK_W), lambda i: (0, 0)),
        ],
        out_specs=[
            pl.BlockSpec((1, N_MEM, QK_W), lambda i: (i, 0, 0)),
            pl.BlockSpec((1, VT_ALL, N_MEM), lambda i: (i, 0, 0)),
        ],
        out_shape=[
            jax.ShapeDtypeStruct((b, N_MEM, QK_W), BF16),
            jax.ShapeDtypeStruct((b, VT_ALL, N_MEM), BF16),
        ],
        compiler_params=_params(("arbitrary",)),
        name="mem_kv",
    )(mem, g, w_bf, gk_t, seg)


IN_TM = 1024
_C_QK, _C_V, _C_R, _C_MQ, _C_MK, _C_MV, _C_CQ, _C_LR = 0, 512, 1024, 1536, 1792, 2048, 2304, 2560
D_IN = 2576


def _in_proj_kernel(x_ref, g_ref, w_ref, seg_ref, gq_ref, gk_ref, gc_ref,
                    qk_ref, v_ref, r_ref, lr_ref, mq_ref, mk_ref, mvt_ref, kmean_ref, cq_ref):
    h = _rms_rows(x_ref[...], g_ref[...]).astype(BF16)
    p = _dot(h, w_ref[...])
    seg = seg_ref[...]
    qk_ref[...] = p[:, _C_QK:_C_V]
    v_ref[...] = p[:, _C_V:_C_R].astype(BF16)
    r_ref[...] = p[:, _C_R:_C_MQ]
    lr_ref[...] = p[:, _C_LR:D_IN]

    def head_norm(t, gain):
        return t * lax.rsqrt(_head_mean_sq(t, seg) + EPS) * gain

    scale = DH ** -0.5
    mq_ref[...] = (head_norm(p[:, _C_MQ:_C_MK], gq_ref[...]) * scale).astype(BF16)
    cq_ref[...] = (head_norm(p[:, _C_CQ:_C_LR], gc_ref[...]) * scale).astype(BF16)
    kn = head_norm(p[:, _C_MK:_C_MV], gk_ref[...])
    mk_ref[...] = kn.astype(BF16)
    mv = p[:, _C_MV:_C_CQ]
    for j in range(IN_TM // MOBA_BLOCK):
        rows = slice(j * MOBA_BLOCK, (j + 1) * MOBA_BLOCK)
        kmean_ref[0, j:j + 1, :] = jnp.mean(kn[rows], axis=0, keepdims=True)
        mvt_ref[j] = _vt_with_ones(mv[rows], ALIBI_SLOPES)


def _in_proj(x2, g, w_bf, seg, gq_t, gk_t, gc_t):
    t = x2.shape[0]
    nt = t // IN_TM
    nb = IN_TM // MOBA_BLOCK
    row = lambda w: pl.BlockSpec((IN_TM, w), lambda i: (i, 0))
    const = lambda a, b: pl.BlockSpec((a, b), lambda i: (0, 0))
    return pl.pallas_call(
        _in_proj_kernel,
        grid=(nt,),
        in_specs=[row(D_MODEL), const(1, D_MODEL), const(D_MODEL, D_IN), const(QK_W, QK_W),
                  const(1, QK_W), const(1, QK_W), const(1, QK_W)],
        out_specs=[row(2 * QK_W), row(GLA_V_W), row(GLA_V_W), row(GLA_RANK), row(QK_W), row(QK_W),
                   pl.BlockSpec((nb, VT_ALL, MOBA_BLOCK), lambda i: (i, 0, 0)),
                   pl.BlockSpec((1, nb, QK_W), lambda i: (i, 0, 0)),
                   row(QK_W)],
        out_shape=[
            jax.ShapeDtypeStruct((t, 2 * QK_W), F32),
            jax.ShapeDtypeStruct((t, GLA_V_W), BF16),
            jax.ShapeDtypeStruct((t, GLA_V_W), F32),
            jax.ShapeDtypeStruct((t, GLA_RANK), F32),
            jax.ShapeDtypeStruct((t, QK_W), BF16),
            jax.ShapeDtypeStruct((t, QK_W), BF16),
            jax.ShapeDtypeStruct((t // MOBA_BLOCK, VT_ALL, MOBA_BLOCK), BF16),
            jax.ShapeDtypeStruct((nt, nb, QK_W), F32),
            jax.ShapeDtypeStruct((t, QK_W), BF16),
        ],
        compiler_params=_params(("arbitrary",)),
        name="in_proj",
    )(x2, g, w_bf, seg, gq_t, gk_t, gc_t)


GLA_TC = 256


def _gla_kernel(qk_ref, v_ref, r_ref, lr_ref, wgk_ref, bgk_ref, gn_ref, tri_ref, ones_ref,
                o_ref, st_ref):
    @pl.when(pl.program_id(1) == 0)
    def _():
        st_ref[...] = jnp.zeros_like(st_ref)

    qk = qk_ref[0]
    q = qk[:, :QK_W]
    k = qk[:, QK_W:]
    gk = _dot(lr_ref[0].astype(BF16), wgk_ref[...]) + bgk_ref[...]
    g = -(jnp.maximum(-gk, 0.0) + jnp.log1p(jnp.exp(-jnp.abs(gk)))) / GLA_GATE_NORMALIZER
    g1, g2, g3 = _split3(g)
    tri = tri_ref[...]
    ones = ones_ref[...]
    cum = _dot(tri, g1) + _dot(tri, g2) + _dot(tri, g3)
    tot = _dot(ones, g1) + _dot(ones, g2) + _dot(ones, g3)
    q_dec = (q * (DH ** -0.5) * jnp.exp(cum)).astype(BF16)
    k_inv = (k * jnp.exp(-cum)).astype(BF16)
    k_end = (k * jnp.exp(tot - cum)).astype(BF16)
    decay = jnp.exp(tot)

    lane_head = lax.broadcasted_iota(jnp.int32, (GLA_TC, QK_W), 1) // DH
    row_t = lax.broadcasted_iota(jnp.int32, (GLA_TC, GLA_TC), 0)
    col_t = lax.broadcasted_iota(jnp.int32, (GLA_TC, GLA_TC), 1)
    causal = (row_t >= col_t) & (row_t // GLA_CHUNK == col_t // GLA_CHUNK)
    same_head = (lax.broadcasted_iota(jnp.int32, (GLA_V_W, QK_W), 0) // GLA_DV
                 == lax.broadcasted_iota(jnp.int32, (GLA_V_W, QK_W), 1) // DH)
    gain = gn_ref[...]

    v = v_ref[0]
    chunks = [slice(c * GLA_CHUNK, (c + 1) * GLA_CHUNK) for c in range(GLA_TC // GLA_CHUNK)]
    d_st = [jnp.where(same_head, _tn(v[rows], k_end[rows]), 0.0) for rows in chunks]
    states = [st_ref[...]]
    for c, rows in enumerate(chunks):
        states.append(states[c] * decay[rows.start:rows.start + 1, :] + d_st[c])
    st_ref[...] = states[-1]
    o_inter = jnp.concatenate([_nt(q_dec[rows], states[c].astype(BF16)) for c, rows in enumerate(chunks)],
                              axis=0)
    q_stack = jnp.concatenate([jnp.where(lane_head == h, q_dec, jnp.zeros_like(q_dec)) for h in range(HEADS)],
                              axis=0)
    a_all = _nt(q_stack, k_inv)
    outs = []
    for h in range(HEADS):
        a = jnp.where(causal, a_all[h * GLA_TC:(h + 1) * GLA_TC], 0.0).astype(BF16)
        vs = slice(h * GLA_DV, (h + 1) * GLA_DV)
        oh = _dot(a, v[:, vs]) + o_inter[:, vs]
        outs.append(_rms_rows(oh, gain) * _silu(r_ref[0, :, vs]))
    o_ref[0] = jnp.concatenate(outs, axis=-1).astype(BF16)


def _gla(qk, v, r, lr, wgk_bf, bgk, gn, b, s):
    ns = s // GLA_TC
    idx = np.arange(GLA_TC)
    same_chunk = (idx[:, None] // GLA_CHUNK) == (idx[None, :] // GLA_CHUNK)
    tri = jnp.asarray(same_chunk & (idx[:, None] >= idx[None, :]), BF16)
    ones = jnp.asarray(same_chunk, BF16)
    seq = lambda w: pl.BlockSpec((1, GLA_TC, w), lambda i, j: (i, j, 0))
    const = lambda a, c: pl.BlockSpec((a, c), lambda i, j: (0, 0))
    return pl.pallas_call(
        _gla_kernel,
        grid=(b, ns),
        in_specs=[seq(2 * QK_W), seq(GLA_V_W), seq(GLA_V_W), seq(GLA_RANK),
                  const(GLA_RANK, QK_W), const(1, QK_W), const(1, GLA_DV),
                  const(GLA_TC, GLA_TC), const(GLA_TC, GLA_TC)],
        out_specs=seq(GLA_V_W),
        out_shape=jax.ShapeDtypeStruct((b, s, GLA_V_W), BF16),
        scratch_shapes=[pltpu.VMEM((GLA_V_W, QK_W), F32)],
        compiler_params=_params(("arbitrary", "arbitrary")),
        name="gla",
    )(qk.reshape(b, s, 2 * QK_W), v.reshape(b, s, GLA_V_W), r.reshape(b, s, GLA_V_W),
      lr.reshape(b, s, GLA_RANK), wgk_bf, bgk, gn, tri, ones)


TQ = MOBA_BLOCK
KV_UNROLL = 8


def _moba_mem_kernel(q_ref, k_ref, vt_ref, kmean_ref, cq_ref, mk_ref, mvt_ref,
                     o_moba_ref, o_mem_ref,
                     qcat_ref, selb_ref, s_even, s_odd, acc_ref, *, n_blocks):
    s_slots = (s_even, s_odd)
    i = pl.program_id(1)
    lane_head = lax.broadcasted_iota(jnp.int32, (TQ, QK_W), 1) // DH
    dist0 = (lax.broadcasted_iota(jnp.int32, (MOBA_BLOCK, TQ), 1)
             - lax.broadcasted_iota(jnp.int32, (MOBA_BLOCK, TQ), 0)).astype(F32)
    slopes = ALIBI_SLOPES
    heads = [slice(h * TQ, (h + 1) * TQ) for h in range(HEADS)]
    vrows = [slice(h * VT_ROWS, (h + 1) * VT_ROWS) for h in range(HEADS)]

    def stack_heads(x):
        return jnp.concatenate([jnp.where(lane_head == h, x, jnp.zeros_like(x)) for h in range(HEADS)], axis=0)

    def finish(accl):
        return accl[:DH] * (1.0 / accl[DH:DH + 1])

    s_even[...] = _nt(mk_ref[0], stack_heads(cq_ref[0]))
    outs = []
    for h in range(HEADS):
        s = s_even[:, heads[h]]
        p = jnp.exp(s - jnp.max(s, axis=0, keepdims=True)).astype(BF16)
        outs.append(finish(_dot(mvt_ref[0, vrows[h], :], p)))
    o_mem_ref[0] = jnp.concatenate(outs, axis=0).T.astype(BF16)

    qcat_ref[...] = stack_heads(q_ref[0])
    k_own = k_ref[0, pl.ds(pl.multiple_of(i * MOBA_BLOCK, MOBA_BLOCK), MOBA_BLOCK), :]
    gate_all = _nt(kmean_ref[0].astype(BF16), qcat_ref[...])
    s_odd[...] = _nt(k_own, qcat_ref[...])
    blk = lax.broadcasted_iota(jnp.int32, (n_blocks, TQ), 0)
    blk_f = blk.astype(F32)

    tq = (lax.broadcasted_iota(jnp.int32, (1, TQ), 1) - (MOBA_BLOCK - 1)).astype(F32)
    init = []
    for h in range(HEADS):
        gate = jnp.where(blk < i, gate_all[:, heads[h]], NEG_INF)
        chosen = jnp.zeros((n_blocks, TQ), jnp.bool_)
        for r in range(MOBA_TOPK):
            mx = jnp.max(gate, axis=0, keepdims=True)
            first = jnp.min(jnp.where(gate == mx, blk_f, float(n_blocks)), axis=0, keepdims=True)
            hit = blk_f == first
            chosen = chosen | (hit & (mx > NEG_INF))
            gate = jnp.where(hit, NEG_INF, gate)
        selb_ref[h] = jnp.where(chosen, 0.0, NEG_INF)
        aq = -slopes[h] * tq
        s = jnp.where(dist0 >= 0.0, s_odd[:, heads[h]], NEG_INF)
        m0 = jnp.max(s, axis=0, keepdims=True) + aq
        p = jnp.exp(s - (m0 - aq)).astype(BF16)
        acc_ref[h] = _dot(vt_ref[i, vrows[h], :], p)
        init += [m0]

    def stage_scores(j, slot):
        kj = k_ref[0, pl.ds(pl.multiple_of(j * MOBA_BLOCK, MOBA_BLOCK), MOBA_BLOCK), :]
        s_slots[slot][...] = _nt(kj, qcat_ref[...])

    stage_scores(0, 0)

    row0 = jnp.minimum(i, 0) * MOBA_BLOCK

    def past_blocks(g, carry, unroll, base, stage_next=True):
        carry = list(carry)
        for u in range(unroll):
            j = base + g * unroll + u
            if stage_next:
                stage_scores(jnp.minimum(j + 1, n_blocks - 1), (u + 1) % 2)
            off = jnp.full((1, TQ), (i - j) * MOBA_BLOCK, jnp.int32).astype(F32) + tq
            for h in range(HEADS):
                m = carry[h]
                s1 = s_slots[u % 2][:, heads[h]]
                rb = selb_ref[h, pl.ds(j, 1), :] - slopes[h] * off
                m_new = jnp.maximum(m, jnp.max(s1, axis=0, keepdims=True) + rb)
                s2 = s_slots[u % 2][pl.ds(pl.multiple_of(row0, MOBA_BLOCK), MOBA_BLOCK), heads[h]]
                p = jnp.exp(s2 - (m_new - rb)).astype(BF16)
                acc_ref[h] = jnp.exp(m - m_new) * acc_ref[h] + _dot(vt_ref[j, vrows[h], :], p)
                carry[h] = m_new
        return tuple(carry)

    half = KV_UNROLL // 2
    n_main = i // KV_UNROLL
    n_half = (i // half) % 2
    done = n_main * KV_UNROLL + n_half * half
    carry = lax.fori_loop(0, n_main, functools.partial(past_blocks, unroll=KV_UNROLL, base=0), tuple(init))
    carry = lax.fori_loop(0, n_half,
                          functools.partial(past_blocks, unroll=half, base=n_main * KV_UNROLL), carry)
    carry = lax.fori_loop(0, (i - done) // 2, functools.partial(past_blocks, unroll=2, base=done), carry)
    final = lax.fori_loop(0, i % 2,
                          functools.partial(past_blocks, unroll=1, base=i - 1, stage_next=False), carry)
    o_t = jnp.concatenate([finish(acc_ref[h]) for h in range(HEADS)], axis=0)
    o_moba_ref[0] = o_t.T.astype(BF16)


def _moba_mem(mq, mk, mvt, kmean, cq, memk, memvt, b, s):
    nq = s // TQ
    n_blocks = s // MOBA_BLOCK
    qspec = pl.BlockSpec((1, TQ, QK_W), lambda i, j: (i, j, 0))
    return pl.pallas_call(
        functools.partial(_moba_mem_kernel, n_blocks=n_blocks),
        grid=(b, nq),
        in_specs=[
            qspec,
            pl.BlockSpec((1, s, QK_W), lambda i, j: (i, 0, 0)),
            pl.BlockSpec((n_blocks, VT_ALL, MOBA_BLOCK), lambda i, j: (i, 0, 0)),
            pl.BlockSpec((1, n_blocks, QK_W), lambda i, j: (i, 0, 0)),
            qspec,
            pl.BlockSpec((1, N_MEM, QK_W), lambda i, j: (i, 0, 0)),
            pl.BlockSpec((1, VT_ALL, N_MEM), lambda i, j: (i, 0, 0)),
        ],
        out_specs=[qspec, qspec],
        out_shape=[jax.ShapeDtypeStruct((b, s, QK_W), BF16), jax.ShapeDtypeStruct((b, s, QK_W), BF16)],
        scratch_shapes=[
            pltpu.VMEM((HEADS * TQ, QK_W), BF16),
            pltpu.VMEM((HEADS, n_blocks, TQ), F32),
            pltpu.VMEM((MOBA_BLOCK, HEADS * TQ), F32),
            pltpu.VMEM((MOBA_BLOCK, HEADS * TQ), F32),
            pltpu.VMEM((HEADS, VT_ROWS, TQ), F32),
        ],
        compiler_params=_params(("arbitrary", "arbitrary")),
        name="moba_mem",
    )(mq.reshape(b, s, QK_W), mk.reshape(b, s, QK_W), mvt, kmean.reshape(b, n_blocks, QK_W),
      cq.reshape(b, s, QK_W), memk, memvt)


OUT_TM = 1024
ROUTE_ROWS = 40


def _out_router_kernel(x_ref, og_ref, om_ref, oc_ref, w_ref, g_ref, wr_ref, br_ref, tri_ref,
                       x1_ref, h_ref, e_ref, gate_ref, count_ref, carry_ref):
    w = w_ref[...]
    x1 = (x_ref[...] + _dot(og_ref[...], w[:GLA_V_W]) + _dot(om_ref[...], w[GLA_V_W:GLA_V_W + QK_W])
          + _dot(oc_ref[...], w[GLA_V_W + QK_W:]))
    x1_ref[...] = x1
    hn = _rms_rows(x1, g_ref[...])
    _store_row_tiles(h_ref, _pack_bf16_pairs(hn))
    logits = _dot(hn.astype(BF16), wr_ref[...]) + br_ref[...]
    lt = logits.T[:ROUTE_ROWS]
    row = lax.broadcasted_iota(jnp.int32, lt.shape, 0).astype(F32)
    lg = jnp.where(row < N_GROUPS, lt, NEG_INF)
    mg = jnp.max(lg, axis=0, keepdims=True)
    g_sel = jnp.min(jnp.where(lg == mg, row, float(LANES)), axis=0, keepdims=True)
    p_group = 1.0 / jnp.sum(jnp.exp(lg - mg), axis=0, keepdims=True)
    lo = N_GROUPS + g_sel * EXPERTS_PER_GROUP
    le = jnp.where((row >= lo) & (row < lo + EXPERTS_PER_GROUP), lt, NEG_INF)
    m0 = jnp.max(le, axis=0, keepdims=True)
    i0 = jnp.min(jnp.where(le == m0, row, float(LANES)), axis=0, keepdims=True)
    le1 = jnp.where(row == i0, NEG_INF, le)
    m1 = jnp.max(le1, axis=0, keepdims=True)
    i1 = jnp.min(jnp.where(le1 == m1, row, float(LANES)), axis=0, keepdims=True)
    z = jnp.exp(m1 - m0)
    w0 = p_group / (1.0 + z)
    w1 = p_group * z / (1.0 + z)
    pad = jnp.zeros((8 - 2, lt.shape[1]), F32)
    gate_ref[...] = jnp.concatenate([w0, w1, pad], axis=0)

    @pl.when(pl.program_id(0) == 0)
    def _():
        carry_ref[...] = jnp.zeros_like(carry_ref)

    oh0 = (row == i0).astype(F32)
    oh1 = (row == i1).astype(F32)
    both = oh0 + oh1
    before = _dot(both.astype(BF16), tri_ref[...]) + carry_ref[:, 0:1]
    r0 = jnp.sum(oh0 * before, axis=0, keepdims=True)
    r1 = jnp.sum(oh1 * before, axis=0, keepdims=True)
    carry_ref[...] = carry_ref[...] + jnp.sum(both, axis=1, keepdims=True)
    count_ref[...] = carry_ref[...].astype(jnp.int32)
    e_ref[...] = jnp.concatenate([i0 - N_GROUPS, i1 - N_GROUPS, r0, r1, pad[:4]], axis=0).astype(jnp.int32)


def _out_router(x2, og, om, oc, w_bf, g, wr_bf, br):
    t = x2.shape[0]
    row = lambda w: pl.BlockSpec((OUT_TM, w), lambda i: (i, 0))
    const = lambda a, b: pl.BlockSpec((a, b), lambda i: (0, 0))
    idx = np.arange(OUT_TM)
    tri = jnp.asarray(idx[:, None] < idx[None, :], BF16)
    col = lambda r: pl.BlockSpec((r, OUT_TM), lambda i: (0, i))
    return pl.pallas_call(
        _out_router_kernel,
        grid=(t // OUT_TM,),
        in_specs=[row(D_MODEL), row(GLA_V_W), row(QK_W), row(QK_W), const(D_MODEL, D_MODEL),
                  const(1, D_MODEL), const(D_MODEL, LANES), const(1, LANES), const(OUT_TM, OUT_TM)],
        out_specs=[row(D_MODEL), pl.BlockSpec((OUT_TM * X_TILE, LANES), lambda i: (i, 0)),
                   col(8), col(8), const(ROUTE_ROWS, LANES)],
        out_shape=[
            jax.ShapeDtypeStruct((t, D_MODEL), F32),
            jax.ShapeDtypeStruct((t * X_TILE, LANES), jnp.uint32),
            jax.ShapeDtypeStruct((8, t), jnp.int32),
            jax.ShapeDtypeStruct((8, t), F32),
            jax.ShapeDtypeStruct((ROUTE_ROWS, LANES), jnp.int32),
        ],
        scratch_shapes=[pltpu.VMEM((ROUTE_ROWS, LANES), F32)],
        compiler_params=_params(("arbitrary",)),
        name="out_router",
    )(x2, og, om, oc, w_bf, g, wr_bf, br, tri)


DMA_UNROLL = 8
PLACE_UNROLL = 32


def _row_to_token(dest_ref, pad_lo_ref, pad_hi_ref, tok_ref):
    def pad_segment(e, c):
        hi = pad_hi_ref[e]

        def pad(g, c2):
            for r in range(8):
                tok_ref[hi - 8 * (g + 1) + r] = 0
            return c2
        lax.fori_loop(0, lax.shift_right_logical(hi - pad_lo_ref[e] + 7, 3), pad, 0)
        return c

    lax.fori_loop(0, pad_lo_ref.shape[0], pad_segment, 0)

    n_tok = dest_ref.shape[0] // 2

    def place(t, c):
        tok_ref[dest_ref[t]] = t
        tok_ref[dest_ref[n_tok + t]] = t
        return c

    lax.fori_loop(0, n_tok, place, 0, unroll=PLACE_UNROLL)


def _experts_kernel(be_ref, run_ref, next_ref, nused_ref, dest_ref, pad_lo_ref, pad_hi_ref,
                    hn_hbm, wg_hbm, wu_hbm, wd_hbm, ys_hbm,
                    hn_vmem, xg_even, xg_odd, y_even, y_odd, wg_f, wu_f, wd_f, wg_bf, wu_bf, wd_bf, tok_ref,
                    hn_sem, w_sem, y_sem):
    n_used = nused_ref[0]
    n_blocks = be_ref.shape[0]
    block_rows = ROW_BLOCK * ROW_TILE

    def weight_copies(e, slot):
        return [pltpu.make_async_copy(src.at[e], dst.at[slot], w_sem.at[slot])
                for src, dst in ((wg_hbm, wg_f), (wu_hbm, wu_f), (wd_hbm, wd_f))]

    def y_copy(buf, parity, i):
        dst = ys_hbm.at[pl.ds(pl.multiple_of(i * block_rows, block_rows), block_rows), :]
        return pltpu.make_async_copy(buf, dst, y_sem.at[parity])

    def gather(i, dst):
        base = jnp.minimum(i, n_blocks - 1) * ROW_BLOCK
        for r in range(ROW_BLOCK):
            src = pl.ds(pl.multiple_of(tok_ref[base + r] * X_TILE, X_TILE), X_TILE)
            dst[r * X_TILE:(r + 1) * X_TILE, :] = hn_vmem[src, :]

    def block(i, parity, cur, nxt, ybuf):
        @pl.when(jnp.logical_or(i == 0, be_ref[i] != be_ref[jnp.maximum(i - 1, 0)]))
        def _():
            slot = jnp.bitwise_and(run_ref[i], 1)
            for c in weight_copies(be_ref[i], slot):
                c.wait()
            wg_bf[...] = wg_f[slot].astype(BF16)
            wu_bf[...] = wu_f[slot].astype(BF16)
            wd_bf[...] = wd_f[slot].astype(BF16)

            @pl.when(next_ref[i] < n_used)
            def _():
                for c in weight_copies(be_ref[jnp.minimum(next_ref[i], n_blocks - 1)], 1 - slot):
                    c.start()

        @pl.when(i >= 2)
        def _():
            y_copy(ybuf, parity, i).wait()

        kh = D_MODEL // 2
        gather(i + 1, nxt)
        x_lo, x_hi = _unpack_bf16_pairs(_load_row_tiles(cur, ROW_BLOCK))
        gate = _dot(x_lo, wg_bf[:kh, :]) + _dot(x_hi, wg_bf[kh:, :])
        up = _dot(x_lo, wu_bf[:kh, :]) + _dot(x_hi, wu_bf[kh:, :])
        hid = (_silu(gate) * up).astype(BF16)
        _store_row_tiles(ybuf, _dot(hid, wd_bf[...]))
        y_copy(ybuf, parity, i).start()

    hn_load = pltpu.make_async_copy(hn_hbm, hn_vmem, hn_sem)
    hn_load.start()
    for c in weight_copies(be_ref[0], 0):
        c.start()
    _row_to_token(dest_ref, pad_lo_ref, pad_hi_ref, tok_ref)
    hn_load.wait()
    gather(0, xg_even)

    def pair(p, carry):
        block(2 * p, 0, xg_even, xg_odd, y_even)

        @pl.when(2 * p + 1 < n_used)
        def _():
            block(2 * p + 1, 1, xg_odd, xg_even, y_odd)
        return carry

    lax.fori_loop(0, lax.shift_right_logical(n_used + 1, 1), pair, 0)

    y_copy(y_even, 0, 0).wait()

    @pl.when(n_used >= 2)
    def _():
        y_copy(y_odd, 1, 0).wait()

    y_even[...] = jnp.zeros_like(y_even)

    def zero_start(i, carry):
        y_copy(y_even, 0, i).start()
        return carry

    def zero_wait(i, carry):
        y_copy(y_even, 0, i).wait()
        return carry

    lax.fori_loop(n_used, n_blocks, zero_start, 0)
    lax.fori_loop(n_used, n_blocks, zero_wait, 0)


def _experts(hn, dest, pad_lo, pad_hi, n_rows, block_expert, block_run, block_next, n_used,
             w_gate, w_up, w_down):
    hbm = pl.BlockSpec(memory_space=pl.ANY)
    grid_spec = pltpu.PrefetchScalarGridSpec(
        num_scalar_prefetch=7,
        grid=(1,),
        in_specs=[hbm, hbm, hbm, hbm],
        out_specs=hbm,
        scratch_shapes=[pltpu.VMEM(hn.shape, jnp.uint32),
                        pltpu.VMEM((ROW_BLOCK * X_TILE, LANES), jnp.uint32),
                        pltpu.VMEM((ROW_BLOCK * X_TILE, LANES), jnp.uint32),
                        pltpu.VMEM((ROW_BLOCK * ROW_TILE, LANES), F32),
                        pltpu.VMEM((ROW_BLOCK * ROW_TILE, LANES), F32),
                        pltpu.VMEM((2, D_MODEL, MOE_FF), F32), pltpu.VMEM((2, D_MODEL, MOE_FF), F32),
                        pltpu.VMEM((2, MOE_FF, D_MODEL), F32),
                        pltpu.VMEM((D_MODEL, MOE_FF), BF16), pltpu.VMEM((D_MODEL, MOE_FF), BF16),
                        pltpu.VMEM((MOE_FF, D_MODEL), BF16),
                        pltpu.SMEM((n_rows,), jnp.int32),
                        pltpu.SemaphoreType.DMA, pltpu.SemaphoreType.DMA((2,)), pltpu.SemaphoreType.DMA((2,))],
    )
    return pl.pallas_call(
        _experts_kernel,
        grid_spec=grid_spec,
        out_shape=jax.ShapeDtypeStruct((n_rows * ROW_TILE, LANES), F32),
        compiler_params=_params(("arbitrary",)),
        name="experts",
    )(block_expert, block_run, block_next, n_used, dest, pad_lo, pad_hi, hn, w_gate, w_up, w_down)


COMB_TM = 256


def _combine_kernel(d0_ref, d1_ref, d0_next_ref, d1_next_ref, x1_ref, gate_ref, ys_ref, out_ref, ybuf, sems):
    g = pl.program_id(0)

    def row_copy(d, slot, k, r):
        src = ys_ref.at[pl.ds(pl.multiple_of(d * ROW_TILE, ROW_TILE), ROW_TILE), :]
        dst = ybuf.at[slot, k, pl.ds(pl.multiple_of(r * ROW_TILE, ROW_TILE), ROW_TILE), :]
        return pltpu.make_async_copy(src, dst, sems.at[slot])

    def gather(drefs, base, slot):
        def issue(r, c):
            row_copy(drefs[0][0, 0, base + r], slot, 0, r).start(priority=0)
            row_copy(drefs[1][0, 0, base + r], slot, 1, r).start(priority=1)
            return c
        lax.fori_loop(0, COMB_TM, issue, 0, unroll=DMA_UNROLL)

    dest_ref = (d0_ref, d1_ref)
    dest_next_ref = (d0_next_ref, d1_next_ref)

    def finish(slot):
        for k in range(2):
            pltpu.make_async_copy(ys_ref.at[pl.ds(0, COMB_TM * ROW_TILE), :], ybuf.at[slot, k],
                                  sems.at[slot]).wait()
        rows = slice(slot * COMB_TM, (slot + 1) * COMB_TM)
        gate = gate_ref[:, rows].T
        w0 = gate[:, 0:1]
        w1 = gate[:, 1:2]
        y0 = _load_row_tiles(ybuf.at[slot, 0], COMB_TM)
        y1 = _load_row_tiles(ybuf.at[slot, 1], COMB_TM)
        out_ref[rows, :] = x1_ref[rows, :] + (y0 * w0 + y1 * w1)

    @pl.when(g == 0)
    def _():
        gather(dest_ref, 0, 0)

    gather(dest_ref, COMB_TM, 1)
    finish(0)

    @pl.when(g + 1 < pl.num_programs(0))
    def _():
        gather(dest_next_ref, 0, 0)

    finish(1)


def _combine(x1, gates, ys, dest):
    t = x1.shape[0]
    ng = t // (2 * COMB_TM)
    d0, d1 = (dest[k].reshape(ng, 1, 2 * COMB_TM) for k in range(2))
    smem = lambda f: pl.BlockSpec((1, 1, 2 * COMB_TM), f, memory_space=pltpu.SMEM)
    cur = lambda i: (i, 0, 0)
    nxt = lambda i: (jnp.minimum(i + 1, ng - 1), 0, 0)
    return pl.pallas_call(
        _combine_kernel,
        grid=(ng,),
        in_specs=[smem(cur), smem(cur), smem(nxt), smem(nxt),
                  pl.BlockSpec((2 * COMB_TM, D_MODEL), lambda i: (i, 0)),
                  pl.BlockSpec((8, 2 * COMB_TM), lambda i: (0, i)),
                  pl.BlockSpec(memory_space=pl.ANY)],
        out_specs=pl.BlockSpec((2 * COMB_TM, D_MODEL), lambda i: (i, 0)),
        out_shape=jax.ShapeDtypeStruct((t, D_MODEL), F32),
        scratch_shapes=[pltpu.VMEM((2, 2, COMB_TM * ROW_TILE, LANES), F32),
                        pltpu.SemaphoreType.DMA((2,))],
        compiler_params=_params(("arbitrary",)),
        name="combine",
    )(d0, d1, d0, d1, x1, gates, ys)


def _layer(x, mem, attn_norm_g, mem_norm_g, w_in, w_gla_gk, b_gla_gk, gla_out_norm_g,
           moba_q_norm_g, moba_k_norm_g, w_mem_kv, mem_q_norm_g, mem_k_norm_g, w_out,
           ffn_norm_g, w_router_group, b_router_group, w_router_expert, b_router_expert,
           w_gate, w_up, w_down):
    b, s, d = x.shape
    t = b * s
    x2 = x.reshape(t, d)
    row = lambda v: v.reshape(1, -1).astype(F32)
    tile_heads = lambda v: jnp.tile(v.astype(F32), HEADS).reshape(1, QK_W)
    hid = np.arange(QK_W) // DH
    seg = jnp.asarray((hid[:, None] == hid[None, :]) / DH, BF16)

    w_in_p = jnp.concatenate([w_in[:, :1536], w_in[:, 1552:], w_in[:, 1536:1552]], axis=1).astype(BF16)
    wr = jnp.concatenate([w_router_group,
                          jnp.transpose(w_router_expert, (1, 0, 2)).reshape(d, N_EXPERTS),
                          jnp.zeros((d, LANES - N_GROUPS - N_EXPERTS), F32)], axis=1).astype(BF16)
    br = jnp.concatenate([b_router_group, b_router_expert.reshape(N_EXPERTS),
                          jnp.zeros((LANES - N_GROUPS - N_EXPERTS,), F32)]).reshape(1, LANES)

    memk, memvt = _mem_kv(mem, row(mem_norm_g), w_mem_kv.astype(BF16), tile_heads(mem_k_norm_g), seg)
    qk, gv, gr, glr, mq, mk, mvt, kmean, cq = _in_proj(
        x2, row(attn_norm_g), w_in_p, seg, tile_heads(moba_q_norm_g), tile_heads(moba_k_norm_g),
        tile_heads(mem_q_norm_g))
    o_gla = _gla(qk, gv, gr, glr, w_gla_gk.astype(BF16), row(b_gla_gk), row(gla_out_norm_g), b, s)
    o_moba, o_mem = _moba_mem(mq, mk, mvt, kmean, cq, memk, memvt, b, s)
    x1, hn, e_ids, gates, counts = _out_router(
        x2, o_gla.reshape(t, GLA_V_W), o_moba.reshape(t, QK_W), o_mem.reshape(t, QK_W),
        w_out.astype(BF16), row(ffn_norm_g), wr, br)

    counts = counts[N_GROUPS:N_GROUPS + N_EXPERTS, 0]
    padded = (counts + ROW_BLOCK - 1) // ROW_BLOCK * ROW_BLOCK
    pends = jnp.cumsum(padded)
    pstarts = pends - padded
    onehot = e_ids[:2, :, None] == jnp.arange(N_EXPERTS, dtype=jnp.int32)
    dest = (jnp.sum(jnp.where(onehot, pstarts, 0), axis=-1) + e_ids[2:4]).astype(jnp.int32)
    n_rows = (t * 2 + N_EXPERTS * (ROW_BLOCK - 1) + ROW_BLOCK - 1) // ROW_BLOCK * ROW_BLOCK
    nb = n_rows // ROW_BLOCK
    block_start = jnp.arange(nb, dtype=jnp.int32) * ROW_BLOCK
    block_expert = jnp.minimum(jnp.sum(block_start[:, None] >= pends[None, :], axis=1),
                               N_EXPERTS - 1).astype(jnp.int32)
    n_used = (pends[-1] // ROW_BLOCK).astype(jnp.int32).reshape(1)
    ended = block_start[:, None] >= pends[None, :]
    block_run = jnp.sum(ended & (counts > 0)[None, :], axis=1).astype(jnp.int32)
    block_next = (jnp.min(jnp.where(ended, n_rows, pends[None, :]), axis=1) // ROW_BLOCK).astype(jnp.int32)

    pad_lo = jnp.concatenate([pstarts + counts, pends[-1:]]).astype(jnp.int32)
    pad_hi = jnp.concatenate([pends, jnp.full((1,), n_rows)]).astype(jnp.int32)
    ys = _experts(hn, dest.reshape(-1), pad_lo, pad_hi, n_rows, block_expert, block_run, block_next, n_used,
                  w_gate, w_up, w_down)
    out = _combine(x1, gates, ys, dest)
    return out.reshape(b, s, d)


def kernel(x, mem, attn_norm_g, mem_norm_g, w_in, w_gla_gk, b_gla_gk, gla_out_norm_g, moba_q_norm_g, moba_k_norm_g, w_mem_kv, mem_q_norm_g, mem_k_norm_g, w_out, ffn_norm_g, w_router_group, b_router_group, w_router_expert, b_router_expert, w_gate, w_up, w_down):
    depth = w_in.shape[0]
    for l in range(depth):
        x = _layer(x, mem, attn_norm_g[l], mem_norm_g[l], w_in[l], w_gla_gk[l], b_gla_gk[l],
                   gla_out_norm_g[l], moba_q_norm_g[l], moba_k_norm_g[l], w_mem_kv[l],
                   mem_q_norm_g[l], mem_k_norm_g[l], w_out[l], ffn_norm_g[l], w_router_group[l],
                   b_router_group[l], w_router_expert[l], b_router_expert[l],
                   w_gate[l], w_up[l], w_down[l])
    return x
```

```python
import functools

import jax
import jax.numpy as jnp
import numpy as np
from jax import lax
from jax.experimental import pallas as pl
from jax.experimental.pallas import tpu as pltpu

F32 = jnp.float32
BF16 = jnp.bfloat16
EPS = 1e-6
NEG_INF = float("-inf")

D_MODEL = 1024
N_MEM = 256
HEADS = 4
DH = 64
GLA_DV = 128
GLA_RANK = 16
GLA_GATE_NORMALIZER = 16.0
GLA_CHUNK = 64
MOBA_BLOCK = 256
MOBA_TOPK = 3
QK_W = HEADS * DH
GLA_V_W = HEADS * GLA_DV
N_GROUPS = 4
EXPERTS_PER_GROUP = 8
N_EXPERTS = N_GROUPS * EXPERTS_PER_GROUP
MOE_FF = 512
LANES = 128
ROW_BLOCK = 256

VMEM_LIMIT = 56 * 1024 * 1024


def _params(sem):
    return pltpu.CompilerParams(dimension_semantics=sem, vmem_limit_bytes=VMEM_LIMIT)


def _nt(a, b):
    return lax.dot_general(a, b, (((1,), (1,)), ((), ())), preferred_element_type=F32)


def _tn(a, b):
    return lax.dot_general(a, b, (((0,), (0,)), ((), ())), preferred_element_type=F32)


def _dot(a, b):
    return jnp.dot(a, b, preferred_element_type=F32)


def _rms_rows(x, g):
    ms = jnp.mean(x * x, axis=-1, keepdims=True)
    return x * lax.rsqrt(ms + EPS) * g


def _split3(x):
    h1 = x.astype(BF16)
    r1 = x - h1.astype(F32)
    h2 = r1.astype(BF16)
    h3 = (r1 - h2.astype(F32)).astype(BF16)
    return h1, h2, h3


def _head_mean_sq(x, seg):
    sq = x * x
    hi = sq.astype(BF16)
    lo = (sq - hi.astype(F32)).astype(BF16)
    return _dot(hi, seg) + _dot(lo, seg)


def _silu(x):
    return x * (1.0 / (1.0 + jnp.exp(-x)))


VT_ROWS = DH + 16
VT_ALL = HEADS * VT_ROWS


ALIBI_SLOPES = tuple(2.0 ** (-8.0 * (h + 1) / HEADS) for h in range(HEADS))


def _vt_with_ones(v, key_slopes=None):
    n = v.shape[0]
    vt = v.T
    ones = jnp.ones((VT_ROWS - DH, n), F32)
    pos = lax.broadcasted_iota(jnp.int32, (1, n), 1).astype(F32)
    parts = []
    for h in range(HEADS):
        scale = 1.0 if key_slopes is None else jnp.exp(key_slopes[h] * (pos - (n - 1)))
        parts += [vt[h * DH:(h + 1) * DH] * scale, ones * scale]
    return jnp.concatenate(parts, axis=0).astype(BF16)


ROW_TILE = D_MODEL // LANES
X_TILE = ROW_TILE // 2


def _store_row_tiles(ref, x):
    n, tiles = x.shape[0], x.shape[1] // LANES
    for c in range(tiles):
        ref[pl.ds(c, n, stride=tiles), :] = x[:, c * LANES:(c + 1) * LANES]


def _load_row_tiles(ref, n):
    tiles = ref.shape[0] // n
    return jnp.concatenate([ref[pl.ds(c, n, stride=tiles), :] for c in range(tiles)], axis=1)


def _pack_bf16_pairs(x):
    w = x.shape[1] // 2
    bits = lambda t: lax.bitcast_convert_type(t.astype(BF16).astype(F32), jnp.uint32)
    return (bits(x[:, w:]) & jnp.uint32(0xFFFF0000)) | (bits(x[:, :w]) >> 16)


def _unpack_bf16_pairs(words):
    lo = lax.bitcast_convert_type(words << 16, F32).astype(BF16)
    hi = lax.bitcast_convert_type(words & jnp.uint32(0xFFFF0000), F32).astype(BF16)
    return lo, hi


def _mem_kv_kernel(mem_ref, g_ref, w_ref, gk_ref, seg_ref, k_ref, vt_ref):
    h = _rms_rows(mem_ref[0], g_ref[...]).astype(BF16)
    kv = _dot(h, w_ref[...])
    k = kv[:, :QK_W]
    kn = k * lax.rsqrt(_head_mean_sq(k, seg_ref[...]) + EPS) * gk_ref[...]
    k_ref[0] = kn.astype(BF16)
    vt_ref[0] = _vt_with_ones(kv[:, QK_W:])


def _mem_kv(mem, g, w_bf, gk_t, seg):
    b = mem.shape[0]
    return pl.pallas_call(
        _mem_kv_kernel,
        grid=(b,),
        in_specs=[
            pl.BlockSpec((1, N_MEM, D_MODEL), lambda i: (i, 0, 0)),
            pl.BlockSpec((1, D_MODEL), lambda i: (0, 0)),
            pl.BlockSpec((D_MODEL, 2 * QK_W), lambda i: (0, 0)),
            pl.BlockSpec((1, QK_W), lambda i: (0, 0)),
            pl.BlockSpec((QK_W, QK_W), lambda i: (0, 0)),
        ],
        out_specs=[
            pl.BlockSpec((1, N_MEM, QK_W), lambda i: (i, 0, 0)),
            pl.BlockSpec((1, VT_ALL, N_MEM), lambda i: (i, 0, 0)),
        ],
        out_shape=[
            jax.ShapeDtypeStruct((b, N_MEM, QK_W), BF16),
            jax.ShapeDtypeStruct((b, VT_ALL, N_MEM), BF16),
        ],
        compiler_params=_params(("arbitrary",)),
        name="mem_kv",
    )(mem, g, w_bf, gk_t, seg)


IN_TM = 1024
_C_QK, _C_V, _C_R, _C_MQ, _C_MK, _C_MV, _C_CQ, _C_LR = 0, 512, 1024, 1536, 1792, 2048, 2304, 2560
D_IN = 2576


def _in_proj_kernel(x_ref, g_ref, w_ref, seg_ref, gq_ref, gk_ref, gc_ref,
                    qk_ref, v_ref, r_ref, lr_ref, mq_ref, mk_ref, mvt_ref, kmean_ref, cq_ref):
    h = _rms_rows(x_ref[...], g_ref[...]).astype(BF16)
    p = _dot(h, w_ref[...])
    seg = seg_ref[...]
    qk_ref[...] = p[:, _C_QK:_C_V]
    v_ref[...] = p[:, _C_V:_C_R].astype(BF16)
    r_ref[...] = p[:, _C_R:_C_MQ]
    lr_ref[...] = p[:, _C_LR:D_IN]

    def head_norm(t, gain):
        return t * lax.rsqrt(_head_mean_sq(t, seg) + EPS) * gain

    scale = DH ** -0.5
    mq_ref[...] = (head_norm(p[:, _C_MQ:_C_MK], gq_ref[...]) * scale).astype(BF16)
    cq_ref[...] = (head_norm(p[:, _C_CQ:_C_LR], gc_ref[...]) * scale).astype(BF16)
    kn = head_norm(p[:, _C_MK:_C_MV], gk_ref[...])
    mk_ref[...] = kn.astype(BF16)
    mv = p[:, _C_MV:_C_CQ]
    for j in range(IN_TM // MOBA_BLOCK):
        rows = slice(j * MOBA_BLOCK, (j + 1) * MOBA_BLOCK)
        kmean_ref[0, j:j + 1, :] = jnp.mean(kn[rows], axis=0, keepdims=True)
        mvt_ref[j] = _vt_with_ones(mv[rows], ALIBI_SLOPES)


def _in_proj(x2, g, w_bf, seg, gq_t, gk_t, gc_t):
    t = x2.shape[0]
    nt = t // IN_TM
    nb = IN_TM // MOBA_BLOCK
    row = lambda w: pl.BlockSpec((IN_TM, w), lambda i: (i, 0))
    const = lambda a, b: pl.BlockSpec((a, b), lambda i: (0, 0))
    return pl.pallas_call(
        _in_proj_kernel,
        grid=(nt,),
        in_specs=[row(D_MODEL), const(1, D_MODEL), const(D_MODEL, D_IN), const(QK_W, QK_W),
                  const(1, QK_W), const(1, QK_W), const(1, QK_W)],
        out_specs=[row(2 * QK_W), row(GLA_V_W), row(GLA_V_W), row(GLA_RANK), row(QK_W), row(QK_W),
                   pl.BlockSpec((nb, VT_ALL, MOBA_BLOCK), lambda i: (i, 0, 0)),
                   pl.BlockSpec((1, nb, QK_W), lambda i: (i, 0, 0)),
                   row(QK_W)],
        out_shape=[
            jax.ShapeDtypeStruct((t, 2 * QK_W), F32),
            jax.ShapeDtypeStruct((t, GLA_V_W), BF16),
            jax.ShapeDtypeStruct((t, GLA_V_W), F32),
            jax.ShapeDtypeStruct((t, GLA_RANK), F32),
            jax.ShapeDtypeStruct((t, QK_W), BF16),
            jax.ShapeDtypeStruct((t, QK_W), BF16),
            jax.ShapeDtypeStruct((t // MOBA_BLOCK, VT_ALL, MOBA_BLOCK), BF16),
            jax.ShapeDtypeStruct((nt, nb, QK_W), F32),
            jax.ShapeDtypeStruct((t, QK_W), BF16),
        ],
        compiler_params=_params(("arbitrary",)),
        name="in_proj",
    )(x2, g, w_bf, seg, gq_t, gk_t, gc_t)


GLA_TC = 256


def _gla_kernel(qk_ref, v_ref, r_ref, lr_ref, wgk_ref, bgk_ref, gn_ref, tri_ref, ones_ref,
                o_ref, st_ref):
    @pl.when(pl.program_id(1) == 0)
    def _():
        st_ref[...] = jnp.zeros_like(st_ref)

    qk = qk_ref[0]
    q = qk[:, :QK_W]
    k = qk[:, QK_W:]
    gk = _dot(lr_ref[0].astype(BF16), wgk_ref[...]) + bgk_ref[...]
    g = -(jnp.maximum(-gk, 0.0) + jnp.log1p(jnp.exp(-jnp.abs(gk)))) / GLA_GATE_NORMALIZER
    g1, g2, g3 = _split3(g)
    tri = tri_ref[...]
    ones = ones_ref[...]
    cum = _dot(tri, g1) + _dot(tri, g2) + _dot(tri, g3)
    tot = _dot(ones, g1) + _dot(ones, g2) + _dot(ones, g3)
    q_dec = (q * (DH ** -0.5) * jnp.exp(cum)).astype(BF16)
    k_inv = (k * jnp.exp(-cum)).astype(BF16)
    k_end = (k * jnp.exp(tot - cum)).astype(BF16)
    decay = jnp.exp(tot)

    lane_head = lax.broadcasted_iota(jnp.int32, (GLA_TC, QK_W), 1) // DH
    row_t = lax.broadcasted_iota(jnp.int32, (GLA_TC, GLA_TC), 0)
    col_t = lax.broadcasted_iota(jnp.int32, (GLA_TC, GLA_TC), 1)
    causal = (row_t >= col_t) & (row_t // GLA_CHUNK == col_t // GLA_CHUNK)
    same_head = (lax.broadcasted_iota(jnp.int32, (GLA_V_W, QK_W), 0) // GLA_DV
                 == lax.broadcasted_iota(jnp.int32, (GLA_V_W, QK_W), 1) // DH)
    gain = gn_ref[...]

    v = v_ref[0]
    chunks = [slice(c * GLA_CHUNK, (c + 1) * GLA_CHUNK) for c in range(GLA_TC // GLA_CHUNK)]
    d_st = [jnp.where(same_head, _tn(v[rows], k_end[rows]), 0.0) for rows in chunks]
    states = [st_ref[...]]
    for c, rows in enumerate(chunks):
        states.append(states[c] * decay[rows.start:rows.start + 1, :] + d_st[c])
    st_ref[...] = states[-1]
    o_inter = jnp.concatenate([_nt(q_dec[rows], states[c].astype(BF16)) for c, rows in enumerate(chunks)],
                              axis=0)
    q_stack = jnp.concatenate([jnp.where(lane_head == h, q_dec, jnp.zeros_like(q_dec)) for h in range(HEADS)],
                              axis=0)
    a_all = _nt(q_stack, k_inv)
    outs = []
    for h in range(HEADS):
        a = jnp.where(causal, a_all[h * GLA_TC:(h + 1) * GLA_TC], 0.0).astype(BF16)
        vs = slice(h * GLA_DV, (h + 1) * GLA_DV)
        oh = _dot(a, v[:, vs]) + o_inter[:, vs]
        outs.append(_rms_rows(oh, gain) * _silu(r_ref[0, :, vs]))
    o_ref[0] = jnp.concatenate(outs, axis=-1).astype(BF16)


def _gla(qk, v, r, lr, wgk_bf, bgk, gn, b, s):
    ns = s // GLA_TC
    idx = np.arange(GLA_TC)
    same_chunk = (idx[:, None] // GLA_CHUNK) == (idx[None, :] // GLA_CHUNK)
    tri = jnp.asarray(same_chunk & (idx[:, None] >= idx[None, :]), BF16)
    ones = jnp.asarray(same_chunk, BF16)
    seq = lambda w: pl.BlockSpec((1, GLA_TC, w), lambda i, j: (i, j, 0))
    const = lambda a, c: pl.BlockSpec((a, c), lambda i, j: (0, 0))
    return pl.pallas_call(
        _gla_kernel,
        grid=(b, ns),
        in_specs=[seq(2 * QK_W), seq(GLA_V_W), seq(GLA_V_W), seq(GLA_RANK),
                  const(GLA_RANK, QK_W), const(1, QK_W), const(1, GLA_DV),
                  const(GLA_TC, GLA_TC), const(GLA_TC, GLA_TC)],
        out_specs=seq(GLA_V_W),
        out_shape=jax.ShapeDtypeStruct((b, s, GLA_V_W), BF16),
        scratch_shapes=[pltpu.VMEM((GLA_V_W, QK_W), F32)],
        compiler_params=_params(("arbitrary", "arbitrary")),
        name="gla",
    )(qk.reshape(b, s, 2 * QK_W), v.reshape(b, s, GLA_V_W), r.reshape(b, s, GLA_V_W),
      lr.reshape(b, s, GLA_RANK), wgk_bf, bgk, gn, tri, ones)


TQ = MOBA_BLOCK
KV_UNROLL = 8


def _moba_mem_kernel(q_ref, k_ref, vt_ref, kmean_ref, cq_ref, mk_ref, mvt_ref,
                     o_moba_ref, o_mem_ref,
                     qcat_ref, selb_ref, s_even, s_odd, acc_ref, *, n_blocks):
    s_slots = (s_even, s_odd)
    i = pl.program_id(1)
    lane_head = lax.broadcasted_iota(jnp.int32, (TQ, QK_W), 1) // DH
    dist0 = (lax.broadcasted_iota(jnp.int32, (MOBA_BLOCK, TQ), 1)
             - lax.broadcasted_iota(jnp.int32, (MOBA_BLOCK, TQ), 0)).astype(F32)
    slopes = ALIBI_SLOPES
    heads = [slice(h * TQ, (h + 1) * TQ) for h in range(HEADS)]
    vrows = [slice(h * VT_ROWS, (h + 1) * VT_ROWS) for h in range(HEADS)]

    def stack_heads(x):
        return jnp.concatenate([jnp.where(lane_head == h, x, jnp.zeros_like(x)) for h in range(HEADS)], axis=0)

    def finish(accl):
        return accl[:DH] * (1.0 / accl[DH:DH + 1])

    def park(ref, scores):
        for h in range(HEADS):
            ref[h] = scores[:, heads[h]]

    park(s_even, _nt(mk_ref[0], stack_heads(cq_ref[0])))
    outs = []
    for h in range(HEADS):
        s = s_even[h]
        p = jnp.exp(s - jnp.max(s, axis=0, keepdims=True)).astype(BF16)
        outs.append(finish(_dot(mvt_ref[0, vrows[h], :], p)))
    o_mem_ref[0] = jnp.concatenate(outs, axis=0).T.astype(BF16)

    qcat_ref[...] = stack_heads(q_ref[0])
    k_own = k_ref[0, pl.ds(pl.multiple_of(i * MOBA_BLOCK, MOBA_BLOCK), MOBA_BLOCK), :]
    gate_all = _nt(kmean_ref[0].astype(BF16), qcat_ref[...])
    park(s_odd, _nt(k_own, qcat_ref[...]))
    blk = lax.broadcasted_iota(jnp.int32, (n_blocks, TQ), 0)
    blk_f = blk.astype(F32)

    tq = (lax.broadcasted_iota(jnp.int32, (1, TQ), 1) - (MOBA_BLOCK - 1)).astype(F32)
    init = []
    for h in range(HEADS):
        gate = jnp.where(blk < i, gate_all[:, heads[h]], NEG_INF)
        chosen = jnp.zeros((n_blocks, TQ), jnp.bool_)
        for r in range(MOBA_TOPK):
            mx = jnp.max(gate, axis=0, keepdims=True)
            first = jnp.min(jnp.where(gate == mx, blk_f, float(n_blocks)), axis=0, keepdims=True)
            hit = blk_f == first
            chosen = chosen | (hit & (mx > NEG_INF))
            gate = jnp.where(hit, NEG_INF, gate)
        selb_ref[h] = jnp.where(chosen, 0.0, NEG_INF)
        aq = -slopes[h] * tq
        s = jnp.where(dist0 >= 0.0, s_odd[h], NEG_INF)
        m0 = jnp.max(s, axis=0, keepdims=True) + aq
        p = jnp.exp(s - (m0 - aq)).astype(BF16)
        acc_ref[h] = _dot(vt_ref[i, vrows[h], :], p)
        init += [m0]

    def stage_scores(j, slot):
        kj = k_ref[0, pl.ds(pl.multiple_of(j * MOBA_BLOCK, MOBA_BLOCK), MOBA_BLOCK), :]
        park(s_slots[slot], _nt(kj, qcat_ref[...]))

    stage_scores(0, 0)

    row0 = jnp.minimum(i, 0) * MOBA_BLOCK

    def past_blocks(g, carry, unroll, base, stage_next=True):
        carry = list(carry)
        for u in range(unroll):
            j = base + g * unroll + u
            if stage_next:
                stage_scores(jnp.minimum(j + 1, n_blocks - 1), (u + 1) % 2)
            off = jnp.full((1, TQ), (i - j) * MOBA_BLOCK, jnp.int32).astype(F32) + tq
            for h in range(HEADS):
                m = carry[h]
                s1 = s_slots[u % 2][h]
                rb = selb_ref[h, pl.ds(j, 1), :] - slopes[h] * off
                m_new = jnp.maximum(m, jnp.max(s1, axis=0, keepdims=True) + rb)
                s2 = s_slots[u % 2][h, pl.ds(pl.multiple_of(row0, MOBA_BLOCK), MOBA_BLOCK), :]
                p = jnp.exp(s2 - (m_new - rb)).astype(BF16)
                acc_ref[h] = jnp.exp(m - m_new) * acc_ref[h] + _dot(vt_ref[j, vrows[h], :], p)
                carry[h] = m_new
        return tuple(carry)

    half = KV_UNROLL // 2
    n_main = i // KV_UNROLL
    n_half = (i // half) % 2
    done = n_main * KV_UNROLL + n_half * half
    carry = lax.fori_loop(0, n_main, functools.partial(past_blocks, unroll=KV_UNROLL, base=0), tuple(init))
    carry = lax.fori_loop(0, n_half,
                          functools.partial(past_blocks, unroll=half, base=n_main * KV_UNROLL), carry)
    carry = lax.fori_loop(0, (i - done) // 2, functools.partial(past_blocks, unroll=2, base=done), carry)
    final = lax.fori_loop(0, i % 2,
                          functools.partial(past_blocks, unroll=1, base=i - 1, stage_next=False), carry)
    o_t = jnp.concatenate([finish(acc_ref[h]) for h in range(HEADS)], axis=0)
    o_moba_ref[0] = o_t.astype(BF16).T


def _moba_mem(mq, mk, mvt, kmean, cq, memk, memvt, b, s):
    nq = s // TQ
    n_blocks = s // MOBA_BLOCK
    qspec = pl.BlockSpec((1, TQ, QK_W), lambda i, j: (i, j, 0))
    return pl.pallas_call(
        functools.partial(_moba_mem_kernel, n_blocks=n_blocks),
        grid=(b, nq),
        in_specs=[
            qspec,
            pl.BlockSpec((1, s, QK_W), lambda i, j: (i, 0, 0)),
            pl.BlockSpec((n_blocks, VT_ALL, MOBA_BLOCK), lambda i, j: (i, 0, 0)),
            pl.BlockSpec((1, n_blocks, QK_W), lambda i, j: (i, 0, 0)),
            qspec,
            pl.BlockSpec((1, N_MEM, QK_W), lambda i, j: (i, 0, 0)),
            pl.BlockSpec((1, VT_ALL, N_MEM), lambda i, j: (i, 0, 0)),
        ],
        out_specs=[qspec, qspec],
        out_shape=[jax.ShapeDtypeStruct((b, s, QK_W), BF16), jax.ShapeDtypeStruct((b, s, QK_W), BF16)],
        scratch_shapes=[
            pltpu.VMEM((HEADS * TQ, QK_W), BF16),
            pltpu.VMEM((HEADS, n_blocks, TQ), F32),
            pltpu.VMEM((HEADS, MOBA_BLOCK, TQ), F32),
            pltpu.VMEM((HEADS, MOBA_BLOCK, TQ), F32),
            pltpu.VMEM((HEADS, VT_ROWS, TQ), F32),
        ],
        compiler_params=_params(("arbitrary", "arbitrary")),
        name="moba_mem",
    )(mq.reshape(b, s, QK_W), mk.reshape(b, s, QK_W), mvt, kmean.reshape(b, n_blocks, QK_W),
      cq.reshape(b, s, QK_W), memk, memvt)


OUT_TM = 1024
ROUTE_ROWS = 40


def _out_router_kernel(x_ref, og_ref, om_ref, oc_ref, w_ref, g_ref, wr_ref, br_ref, tri_ref,
                       x1_ref, h_ref, e_ref, gate_ref, count_ref, carry_ref):
    w = w_ref[...]
    x1 = (x_ref[...] + _dot(og_ref[...], w[:GLA_V_W]) + _dot(om_ref[...], w[GLA_V_W:GLA_V_W + QK_W])
          + _dot(oc_ref[...], w[GLA_V_W + QK_W:]))
    x1_ref[...] = x1
    hn = _rms_rows(x1, g_ref[...])
    _store_row_tiles(h_ref, _pack_bf16_pairs(hn))
    logits = _dot(hn.astype(BF16), wr_ref[...]) + br_ref[...]
    lt = logits.T[:ROUTE_ROWS]
    row = lax.broadcasted_iota(jnp.int32, lt.shape, 0).astype(F32)
    lg = jnp.where(row < N_GROUPS, lt, NEG_INF)
    mg = jnp.max(lg, axis=0, keepdims=True)
    g_sel = jnp.min(jnp.where(lg == mg, row, float(LANES)), axis=0, keepdims=True)
    p_group = 1.0 / jnp.sum(jnp.exp(lg - mg), axis=0, keepdims=True)
    lo = N_GROUPS + g_sel * EXPERTS_PER_GROUP
    le = jnp.where((row >= lo) & (row < lo + EXPERTS_PER_GROUP), lt, NEG_INF)
    m0 = jnp.max(le, axis=0, keepdims=True)
    i0 = jnp.min(jnp.where(le == m0, row, float(LANES)), axis=0, keepdims=True)
    le1 = jnp.where(row == i0, NEG_INF, le)
    m1 = jnp.max(le1, axis=0, keepdims=True)
    i1 = jnp.min(jnp.where(le1 == m1, row, float(LANES)), axis=0, keepdims=True)
    z = jnp.exp(m1 - m0)
    w0 = p_group / (1.0 + z)
    w1 = p_group * z / (1.0 + z)
    pad = jnp.zeros((8 - 2, lt.shape[1]), F32)
    gate_ref[...] = jnp.concatenate([w0, w1, pad], axis=0)

    @pl.when(pl.program_id(0) == 0)
    def _():
        carry_ref[...] = jnp.zeros_like(carry_ref)

    oh0 = (row == i0).astype(F32)
    oh1 = (row == i1).astype(F32)
    both = oh0 + oh1
    before = _dot(both.astype(BF16), tri_ref[...]) + carry_ref[:, 0:1]
    r0 = jnp.sum(oh0 * before, axis=0, keepdims=True)
    r1 = jnp.sum(oh1 * before, axis=0, keepdims=True)
    carry_ref[...] = carry_ref[...] + jnp.sum(both, axis=1, keepdims=True)
    count_ref[...] = carry_ref[...].astype(jnp.int32)
    e_ref[...] = jnp.concatenate([i0 - N_GROUPS, i1 - N_GROUPS, r0, r1, pad[:4]], axis=0).astype(jnp.int32)


def _out_router(x2, og, om, oc, w_bf, g, wr_bf, br):
    t = x2.shape[0]
    row = lambda w: pl.BlockSpec((OUT_TM, w), lambda i: (i, 0))
    const = lambda a, b: pl.BlockSpec((a, b), lambda i: (0, 0))
    idx = np.arange(OUT_TM)
    tri = jnp.asarray(idx[:, None] < idx[None, :], BF16)
    col = lambda r: pl.BlockSpec((r, OUT_TM), lambda i: (0, i))
    return pl.pallas_call(
        _out_router_kernel,
        grid=(t // OUT_TM,),
        in_specs=[row(D_MODEL), row(GLA_V_W), row(QK_W), row(QK_W), const(D_MODEL, D_MODEL),
                  const(1, D_MODEL), const(D_MODEL, LANES), const(1, LANES), const(OUT_TM, OUT_TM)],
        out_specs=[row(D_MODEL), pl.BlockSpec((OUT_TM * X_TILE, LANES), lambda i: (i, 0)),
                   col(8), col(8), const(ROUTE_ROWS, LANES)],
        out_shape=[
            jax.ShapeDtypeStruct((t, D_MODEL), F32),
            jax.ShapeDtypeStruct((t * X_TILE, LANES), jnp.uint32),
            jax.ShapeDtypeStruct((8, t), jnp.int32),
            jax.ShapeDtypeStruct((8, t), F32),
            jax.ShapeDtypeStruct((ROUTE_ROWS, LANES), jnp.int32),
        ],
        scratch_shapes=[pltpu.VMEM((ROUTE_ROWS, LANES), F32)],
        compiler_params=_params(("arbitrary",)),
        name="out_router",
    )(x2, og, om, oc, w_bf, g, wr_bf, br, tri)


DMA_UNROLL = 8
PLACE_UNROLL = 32


def _row_to_token(dest_ref, pad_lo_ref, pad_hi_ref, tok_ref):
    def pad_segment(e, c):
        hi = pad_hi_ref[e]

        def pad(g, c2):
            for r in range(8):
                tok_ref[hi - 8 * (g + 1) + r] = 0
            return c2
        lax.fori_loop(0, lax.shift_right_logical(hi - pad_lo_ref[e] + 7, 3), pad, 0)
        return c

    lax.fori_loop(0, pad_lo_ref.shape[0], pad_segment, 0)

    n_tok = dest_ref.shape[0] // 2

    def place(t, c):
        tok_ref[dest_ref[t]] = t
        tok_ref[dest_ref[n_tok + t]] = t
        return c

    lax.fori_loop(0, n_tok, place, 0, unroll=PLACE_UNROLL)


def _experts_kernel(be_ref, run_ref, next_ref, nused_ref, dest_ref, pad_lo_ref, pad_hi_ref,
                    hn_hbm, wg_hbm, wu_hbm, wd_hbm, ys_hbm,
                    hn_vmem, xg_even, xg_odd, y_even, y_odd, wg_f, wu_f, wd_f, wg_bf, wu_bf, wd_bf, tok_ref,
                    hn_sem, w_sem, y_sem):
    n_used = nused_ref[0]
    n_blocks = be_ref.shape[0]
    block_rows = ROW_BLOCK * ROW_TILE

    def weight_copies(e, slot):
        return [pltpu.make_async_copy(src.at[e], dst.at[slot], w_sem.at[slot])
                for src, dst in ((wg_hbm, wg_f), (wu_hbm, wu_f), (wd_hbm, wd_f))]

    def y_copy(buf, parity, i):
        dst = ys_hbm.at[pl.ds(pl.multiple_of(i * block_rows, block_rows), block_rows), :]
        return pltpu.make_async_copy(buf, dst, y_sem.at[parity])

    def gather(i, dst):
        base = jnp.minimum(i, n_blocks - 1) * ROW_BLOCK
        for r in range(ROW_BLOCK):
            src = pl.ds(pl.multiple_of(tok_ref[base + r] * X_TILE, X_TILE), X_TILE)
            dst[r * X_TILE:(r + 1) * X_TILE, :] = hn_vmem[src, :]

    def block(i, parity, cur, nxt, ybuf):
        @pl.when(jnp.logical_or(i == 0, be_ref[i] != be_ref[jnp.maximum(i - 1, 0)]))
        def _():
            slot = jnp.bitwise_and(run_ref[i], 1)
            for c in weight_copies(be_ref[i], slot):
                c.wait()
            wg_bf[...] = wg_f[slot].astype(BF16)
            wu_bf[...] = wu_f[slot].astype(BF16)
            wd_bf[...] = wd_f[slot].astype(BF16)

            @pl.when(next_ref[i] < n_used)
            def _():
                for c in weight_copies(be_ref[jnp.minimum(next_ref[i], n_blocks - 1)], 1 - slot):
                    c.start()

        @pl.when(i >= 2)
        def _():
            y_copy(ybuf, parity, i).wait()

        kh = D_MODEL // 2
        gather(i + 1, nxt)
        x_lo, x_hi = _unpack_bf16_pairs(_load_row_tiles(cur, ROW_BLOCK))
        gate = _dot(x_lo, wg_bf[:kh, :]) + _dot(x_hi, wg_bf[kh:, :])
        up = _dot(x_lo, wu_bf[:kh, :]) + _dot(x_hi, wu_bf[kh:, :])
        hid = (_silu(gate) * up).astype(BF16)
        _store_row_tiles(ybuf, _dot(hid, wd_bf[...]))
        y_copy(ybuf, parity, i).start()

    hn_load = pltpu.make_async_copy(hn_hbm, hn_vmem, hn_sem)
    hn_load.start()
    for c in weight_copies(be_ref[0], 0):
        c.start()
    _row_to_token(dest_ref, pad_lo_ref, pad_hi_ref, tok_ref)
    hn_load.wait()
    gather(0, xg_even)

    def pair(p, carry):
        block(2 * p, 0, xg_even, xg_odd, y_even)

        @pl.when(2 * p + 1 < n_used)
        def _():
            block(2 * p + 1, 1, xg_odd, xg_even, y_odd)
        return carry

    lax.fori_loop(0, lax.shift_right_logical(n_used + 1, 1), pair, 0)

    y_copy(y_even, 0, 0).wait()

    @pl.when(n_used >= 2)
    def _():
        y_copy(y_odd, 1, 0).wait()

    y_even[...] = jnp.zeros_like(y_even)

    def zero_start(i, carry):
        y_copy(y_even, 0, i).start()
        return carry

    def zero_wait(i, carry):
        y_copy(y_even, 0, i).wait()
        return carry

    lax.fori_loop(n_used, n_blocks, zero_start, 0)
    lax.fori_loop(n_used, n_blocks, zero_wait, 0)


def _experts(hn, dest, pad_lo, pad_hi, n_rows, block_expert, block_run, block_next, n_used,
             w_gate, w_up, w_down):
    hbm = pl.BlockSpec(memory_space=pl.ANY)
    grid_spec = pltpu.PrefetchScalarGridSpec(
        num_scalar_prefetch=7,
        grid=(1,),
        in_specs=[hbm, hbm, hbm, hbm],
        out_specs=hbm,
        scratch_shapes=[pltpu.VMEM(hn.shape, jnp.uint32),
                        pltpu.VMEM((ROW_BLOCK * X_TILE, LANES), jnp.uint32),
                        pltpu.VMEM((ROW_BLOCK * X_TILE, LANES), jnp.uint32),
                        pltpu.VMEM((ROW_BLOCK * ROW_TILE, LANES), F32),
                        pltpu.VMEM((ROW_BLOCK * ROW_TILE, LANES), F32),
                        pltpu.VMEM((2, D_MODEL, MOE_FF), F32), pltpu.VMEM((2, D_MODEL, MOE_FF), F32),
                        pltpu.VMEM((2, MOE_FF, D_MODEL), F32),
                        pltpu.VMEM((D_MODEL, MOE_FF), BF16), pltpu.VMEM((D_MODEL, MOE_FF), BF16),
                        pltpu.VMEM((MOE_FF, D_MODEL), BF16),
                        pltpu.SMEM((n_rows,), jnp.int32),
                        pltpu.SemaphoreType.DMA, pltpu.SemaphoreType.DMA((2,)), pltpu.SemaphoreType.DMA((2,))],
    )
    return pl.pallas_call(
        _experts_kernel,
        grid_spec=grid_spec,
        out_shape=jax.ShapeDtypeStruct((n_rows * ROW_TILE, LANES), F32),
        compiler_params=_params(("arbitrary",)),
        name="experts",
    )(block_expert, block_run, block_next, n_used, dest, pad_lo, pad_hi, hn, w_gate, w_up, w_down)


COMB_TM = 256


def _combine_kernel(d0_ref, d1_ref, d0_next_ref, d1_next_ref, x1_ref, gate_ref, ys_ref, out_ref, ybuf, sems):
    g = pl.program_id(0)

    def row_copy(d, slot, k, r):
        src = ys_ref.at[pl.ds(pl.multiple_of(d * ROW_TILE, ROW_TILE), ROW_TILE), :]
        dst = ybuf.at[slot, k, pl.ds(pl.multiple_of(r * ROW_TILE, ROW_TILE), ROW_TILE), :]
        return pltpu.make_async_copy(src, dst, sems.at[slot])

    def gather(drefs, base, slot):
        def issue(r, c):
            row_copy(drefs[0][0, 0, base + r], slot, 0, r).start(priority=0)
            row_copy(drefs[1][0, 0, base + r], slot, 1, r).start(priority=1)
            return c
        lax.fori_loop(0, COMB_TM, issue, 0, unroll=DMA_UNROLL)

    dest_ref = (d0_ref, d1_ref)
    dest_next_ref = (d0_next_ref, d1_next_ref)

    def finish(slot):
        for k in range(2):
            pltpu.make_async_copy(ys_ref.at[pl.ds(0, COMB_TM * ROW_TILE), :], ybuf.at[slot, k],
                                  sems.at[slot]).wait()
        rows = slice(slot * COMB_TM, (slot + 1) * COMB_TM)
        gate = gate_ref[:, rows].T
        w0 = gate[:, 0:1]
        w1 = gate[:, 1:2]
        y0 = _load_row_tiles(ybuf.at[slot, 0], COMB_TM)
        y1 = _load_row_tiles(ybuf.at[slot, 1], COMB_TM)
        out_ref[rows, :] = x1_ref[rows, :] + (y0 * w0 + y1 * w1)

    @pl.when(g == 0)
    def _():
        gather(dest_ref, 0, 0)

    gather(dest_ref, COMB_TM, 1)
    finish(0)

    @pl.when(g + 1 < pl.num_programs(0))
    def _():
        gather(dest_next_ref, 0, 0)

    finish(1)


def _combine(x1, gates, ys, dest):
    t = x1.shape[0]
    ng = t // (2 * COMB_TM)
    d0, d1 = (dest[k].reshape(ng, 1, 2 * COMB_TM) for k in range(2))
    smem = lambda f: pl.BlockSpec((1, 1, 2 * COMB_TM), f, memory_space=pltpu.SMEM)
    cur = lambda i: (i, 0, 0)
    nxt = lambda i: (jnp.minimum(i + 1, ng - 1), 0, 0)
    return pl.pallas_call(
        _combine_kernel,
        grid=(ng,),
        in_specs=[smem(cur), smem(cur), smem(nxt), smem(nxt),
                  pl.BlockSpec((2 * COMB_TM, D_MODEL), lambda i: (i, 0)),
                  pl.BlockSpec((8, 2 * COMB_TM), lambda i: (0, i)),
                  pl.BlockSpec(memory_space=pl.ANY)],
        out_specs=pl.BlockSpec((2 * COMB_TM, D_MODEL), lambda i: (i, 0)),
        out_shape=jax.ShapeDtypeStruct((t, D_MODEL), F32),
        scratch_shapes=[pltpu.VMEM((2, 2, COMB_TM * ROW_TILE, LANES), F32),
                        pltpu.SemaphoreType.DMA((2,))],
        compiler_params=_params(("arbitrary",)),
        name="combine",
    )(d0, d1, d0, d1, x1, gates, ys)


def _layer(x, mem, attn_norm_g, mem_norm_g, w_in, w_gla_gk, b_gla_gk, gla_out_norm_g,
           moba_q_norm_g, moba_k_norm_g, w_mem_kv, mem_q_norm_g, mem_k_norm_g, w_out,
           ffn_norm_g, w_router_group, b_router_group, w_router_expert, b_router_expert,
           w_gate, w_up, w_down):
    b, s, d = x.shape
    t = b * s
    x2 = x.reshape(t, d)
    row = lambda v: v.reshape(1, -1).astype(F32)
    tile_heads = lambda v: jnp.tile(v.astype(F32), HEADS).reshape(1, QK_W)
    hid = np.arange(QK_W) // DH
    seg = jnp.asarray((hid[:, None] == hid[None, :]) / DH, BF16)

    w_in_p = jnp.concatenate([w_in[:, :1536], w_in[:, 1552:], w_in[:, 1536:1552]], axis=1).astype(BF16)
    wr = jnp.concatenate([w_router_group,
                          jnp.transpose(w_router_expert, (1, 0, 2)).reshape(d, N_EXPERTS),
                          jnp.zeros((d, LANES - N_GROUPS - N_EXPERTS), F32)], axis=1).astype(BF16)
    br = jnp.concatenate([b_router_group, b_router_expert.reshape(N_EXPERTS),
                          jnp.zeros((LANES - N_GROUPS - N_EXPERTS,), F32)]).reshape(1, LANES)

    memk, memvt = _mem_kv(mem, row(mem_norm_g), w_mem_kv.astype(BF16), tile_heads(mem_k_norm_g), seg)
    qk, gv, gr, glr, mq, mk, mvt, kmean, cq = _in_proj(
        x2, row(attn_norm_g), w_in_p, seg, tile_heads(moba_q_norm_g), tile_heads(moba_k_norm_g),
        tile_heads(mem_q_norm_g))
    o_gla = _gla(qk, gv, gr, glr, w_gla_gk.astype(BF16), row(b_gla_gk), row(gla_out_norm_g), b, s)
    o_moba, o_mem = _moba_mem(mq, mk, mvt, kmean, cq, memk, memvt, b, s)
    x1, hn, e_ids, gates, counts = _out_router(
        x2, o_gla.reshape(t, GLA_V_W), o_moba.reshape(t, QK_W), o_mem.reshape(t, QK_W),
        w_out.astype(BF16), row(ffn_norm_g), wr, br)

    counts = counts[N_GROUPS:N_GROUPS + N_EXPERTS, 0]
    padded = (counts + ROW_BLOCK - 1) // ROW_BLOCK * ROW_BLOCK
    pends = jnp.cumsum(padded)
    pstarts = pends - padded
    onehot = e_ids[:2, :, None] == jnp.arange(N_EXPERTS, dtype=jnp.int32)
    dest = (jnp.sum(jnp.where(onehot, pstarts, 0), axis=-1) + e_ids[2:4]).astype(jnp.int32)
    n_rows = (t * 2 + N_EXPERTS * (ROW_BLOCK - 1) + ROW_BLOCK - 1) // ROW_BLOCK * ROW_BLOCK
    nb = n_rows // ROW_BLOCK
    block_start = jnp.arange(nb, dtype=jnp.int32) * ROW_BLOCK
    block_expert = jnp.minimum(jnp.sum(block_start[:, None] >= pends[None, :], axis=1),
                               N_EXPERTS - 1).astype(jnp.int32)
    n_used = (pends[-1] // ROW_BLOCK).astype(jnp.int32).reshape(1)
    ended = block_start[:, None] >= pends[None, :]
    block_run = jnp.sum(ended & (counts > 0)[None, :], axis=1).astype(jnp.int32)
    block_next = (jnp.min(jnp.where(ended, n_rows, pends[None, :]), axis=1) // ROW_BLOCK).astype(jnp.int32)

    pad_lo = jnp.concatenate([pstarts + counts, pends[-1:]]).astype(jnp.int32)
    pad_hi = jnp.concatenate([pends, jnp.full((1,), n_rows)]).astype(jnp.int32)
    ys = _experts(hn, dest.reshape(-1), pad_lo, pad_hi, n_rows, block_expert, block_run, block_next, n_used,
                  w_gate, w_up, w_down)
    out = _combine(x1, gates, ys, dest)
    return out.reshape(b, s, d)


def kernel(x, mem, attn_norm_g, mem_norm_g, w_in, w_gla_gk, b_gla_gk, gla_out_norm_g, moba_q_norm_g, moba_k_norm_g, w_mem_kv, mem_q_norm_g, mem_k_norm_g, w_out, ffn_norm_g, w_router_group, b_router_group, w_router_expert, b_router_expert, w_gate, w_up, w_down):
    depth = w_in.shape[0]
    for l in range(depth):
        x = _layer(x, mem, attn_norm_g[l], mem_norm_g[l], w_in[l], w_gla_gk[l], b_gla_gk[l],
                   gla_out_norm_g[l], moba_q_norm_g[l], moba_k_norm_g[l], w_mem_kv[l],
                   mem_q_norm_g[l], mem_k_norm_g[l], w_out[l], ffn_norm_g[l], w_router_group[l],
                   b_router_group[l], w_router_expert[l], b_router_expert[l],
                   w_gate[l], w_up[l], w_down[l])
    return x
```
